```python
import jax, jax.numpy as jnp
from jax import lax
import numpy as np

D_MODEL = 1024
BATCH = 8
SEQ = 2048
DEPTH = 1

RWKV_HEADS = 8
RWKV_HEAD_DIM = 64
RWKV_WIDTH = RWKV_HEADS * RWKV_HEAD_DIM
RET_HEADS = 4
RET_HEAD_DIM = 128
RET_WIDTH = RET_HEADS * RET_HEAD_DIM
MIX_WIDTH = RWKV_WIDTH + RET_WIDTH
DECAY_LORA = 64
AAA_LORA = 64
GATE_LORA = 128
IN_COLS = 3 * RWKV_WIDTH + 4 * RET_WIDTH
IN_SPLITS = (RWKV_WIDTH, 2 * RWKV_WIDTH, 3 * RWKV_WIDTH,
             3 * RWKV_WIDTH + RET_WIDTH, 3 * RWKV_WIDTH + 2 * RET_WIDTH,
             3 * RWKV_WIDTH + 3 * RET_WIDTH)
RET_CHUNK = 128
ROPE_BASE = 10000.0
D_FF = 2816
CONV_WIDTH = 3
NORM_EPS = 1e-6
RWKV_GN_EPS = 64e-5
RET_GN_EPS = 1e-5

kernel_name = "hymba_rwkv7_retnet_convglu"


def rms_norm(x, g):
    xf = x.astype(jnp.float32)
    y = xf * lax.rsqrt(jnp.mean(xf * xf, axis=-1, keepdims=True) + NORM_EPS)
    return (y * g.astype(jnp.float32)).astype(x.dtype)


def token_shift(x):
    return jnp.pad(x, ((0, 0), (1, 0), (0, 0)))[:, :-1, :]


def head_norm(y, eps):
    mu = jnp.mean(y, axis=-1, keepdims=True)
    var = jnp.mean(jnp.square(y - mu), axis=-1, keepdims=True)
    yn = (y - mu) * lax.rsqrt(var + eps)
    return yn.reshape(y.shape[0], y.shape[1], -1)


def rotary(x, positions):
    half = x.shape[-1] // 2
    inv_freq = ROPE_BASE ** (-jnp.arange(half, dtype=jnp.float32) / half)
    ang = positions.astype(jnp.float32)[:, None] * inv_freq[None, :]
    cos = jnp.cos(ang)[None, :, None, :]
    sin = jnp.sin(ang)[None, :, None, :]
    x1, x2 = x[..., :half], x[..., half:]
    return jnp.concatenate([x1 * cos - x2 * sin, x1 * sin + x2 * cos], axis=-1)


def rwkv7_scan(r, w, k, v, a, b):
    Bsz, T, H, D = r.shape
    seq_first = lambda z: jnp.swapaxes(z, 0, 1)

    def step(S, inp):
        r_t, w_t, k_t, v_t, a_t, b_t = inp
        sa = jnp.einsum('bhij,bhj->bhi', S, a_t)
        S = (S * w_t[:, :, None, :] + sa[..., None] * b_t[:, :, None, :]
             + v_t[..., None] * k_t[:, :, None, :])
        y = jnp.einsum('bhij,bhj->bhi', S, r_t)
        return S, y

    S0 = jnp.zeros((Bsz, H, D, D), jnp.float32)
    _, ys = lax.scan(step, S0, tuple(seq_first(z) for z in (r, w, k, v, a, b)))
    return seq_first(ys)


def rwkv7_group(xn, p_r, p_k, p_v, mu_r, mu_k, mu_v, mu_w, mu_a, mu_g,
                w0, w1, w2, a0, a1, a2, g1, g2, k_k, k_a, r_k, lnx_w, lnx_b):
    Bsz, T, _ = xn.shape
    f32 = jnp.float32
    heads = lambda z: z.reshape(Bsz, T, RWKV_HEADS, RWKV_HEAD_DIM)
    dx = token_shift(xn) - xn
    xw = xn + dx * mu_w
    xa = xn + dx * mu_a
    xg = xn + dx * mu_g
    r = (p_r + (token_shift(p_r) - p_r) * mu_r).astype(f32)
    k = (p_k + (token_shift(p_k) - p_k) * mu_k).astype(f32)
    v = (p_v + (token_shift(p_v) - p_v) * mu_v).astype(f32)
    w_log = -jax.nn.softplus(-(w0 + jnp.tanh(xw @ w1) @ w2).astype(f32)) - 0.5
    decay = jnp.exp(-jnp.exp(w_log))
    a = jax.nn.sigmoid((a0 + (xa @ a1) @ a2).astype(f32))
    g = (jax.nn.sigmoid(xg @ g1) @ g2).astype(f32)
    kk = heads(k * k_k.astype(f32))
    kk = kk / jnp.maximum(jnp.linalg.norm(kk, axis=-1, keepdims=True), 1e-12)
    k = k * (1.0 + (a - 1.0) * k_a.astype(f32))
    rh, kh, vh, ah = heads(r), heads(k), heads(v), heads(a)
    y = rwkv7_scan(rh, heads(decay), kh, vh, -kk, kk * ah)
    y = head_norm(y, RWKV_GN_EPS) * lnx_w.astype(f32) + lnx_b.astype(f32)
    bonus = jnp.sum(rh * kh * r_k.astype(f32), axis=-1, keepdims=True) * vh
    y = (y + bonus.reshape(Bsz, T, RWKV_WIDTH)) * g
    return y


def retention_group(q_p, k_p, v_p, g_p, gn_w):
    Bsz, T, _ = q_p.shape
    f32 = jnp.float32
    H, D, C = RET_HEADS, RET_HEAD_DIM, RET_CHUNK
    n = T // C
    pos = jnp.arange(T)
    q = rotary(q_p.astype(f32).reshape(Bsz, T, H, D), pos)
    k = rotary(k_p.astype(f32).reshape(Bsz, T, H, D), pos) * (D ** -0.5)
    v = v_p.astype(f32).reshape(Bsz, T, H, D)
    log_gamma = jnp.log(1.0 - 2.0 ** (-5.0 - jnp.arange(H, dtype=f32)))
    qc = q.reshape(Bsz, n, C, H, D)
    kc = k.reshape(Bsz, n, C, H, D)
    vc = v.reshape(Bsz, n, C, H, D)
    idx = jnp.arange(C, dtype=f32)
    diff = idx[:, None] - idx[None, :]
    dmask = jnp.where(diff[None] >= 0,
                      jnp.exp(jnp.maximum(diff, 0.0)[None] * log_gamma[:, None, None]), 0.0)
    scores = jnp.einsum('bnihd,bnjhd->bnhij', qc, kc) * dmask
    intra = jnp.einsum('bnhij,bnjhe->bnihe', scores, vc)
    zeta = jnp.exp((C - 1.0 - idx)[None, :] * log_gamma[:, None])
    kv = jnp.einsum('bnjhd,hj,bnjhe->bnhde', kc, zeta, vc)
    gamma_c = jnp.exp(C * log_gamma)[None, :, None, None]

    def step(R, kv_i):
        return R * gamma_c + kv_i, R

    R0 = jnp.zeros((Bsz, H, D, D), f32)
    _, R_prev = lax.scan(step, R0, jnp.swapaxes(kv, 0, 1))
    R_prev = jnp.swapaxes(R_prev, 0, 1)
    xi = jnp.exp((idx + 1.0)[None, :] * log_gamma[:, None])
    inter = jnp.einsum('bnihd,bnhde,hi->bnihe', qc, R_prev, xi)
    y = (intra + inter).reshape(Bsz, T, H, D)
    y = head_norm(y, RET_GN_EPS) * gn_w.astype(f32)
    return jax.nn.silu(g_p.astype(f32)) * y


def hybrid_mixer(xn, w_in, mu_r, mu_k, mu_v, mu_w, mu_a, mu_g, w0, w1, w2,
                 a0, a1, a2, g1, g2, k_k, k_a, r_k, lnx_w, lnx_b, ret_gn_w, w_out):
    proj = xn @ w_in
    p_r, p_k, p_v, q_ret, k_ret, v_ret, g_ret = jnp.split(proj, IN_SPLITS, axis=-1)
    y_rwkv = rwkv7_group(xn, p_r, p_k, p_v, mu_r, mu_k, mu_v, mu_w, mu_a, mu_g,
                         w0, w1, w2, a0, a1, a2, g1, g2, k_k, k_a, r_k, lnx_w, lnx_b)
    y_ret = retention_group(q_ret, k_ret, v_ret, g_ret, ret_gn_w)
    y = jnp.concatenate([y_rwkv, y_ret], axis=-1).astype(xn.dtype)
    return y @ w_out


def conv_glu(xn, w_gate, w_up, conv_w, conv_b, w_down):
    gate = xn @ w_gate
    up = xn @ w_up
    gate = lax.conv_general_dilated(
        gate, conv_w, window_strides=(1,), padding=[(CONV_WIDTH - 1, 0)],
        dimension_numbers=('NWC', 'WIO', 'NWC'), feature_group_count=D_FF) + conv_b
    return (jax.nn.silu(gate) * up) @ w_down


def _fwd_setup_inputs(seed: int = 0) -> dict:
    key = jax.random.key(seed)
    ks = iter(jax.random.split(key, 40))
    L, D = DEPTH, D_MODEL
    nrm = lambda shape, s: jax.random.normal(next(ks), shape, jnp.float32) * s
    uni = lambda shape, lo, hi: jax.random.uniform(next(ks), shape, jnp.float32, lo, hi)
    gain = lambda shape: 1.0 + nrm(shape, 0.02)
    return {
        "x": nrm((BATCH, SEQ, D), 1.0),
        "norm_mix_g": gain((L, D)),
        "w_in": nrm((L, D, IN_COLS), D ** -0.5),
        "rwkv_mu_r": uni((L, RWKV_WIDTH), 0.0, 1.0),
        "rwkv_mu_k": uni((L, RWKV_WIDTH), 0.0, 1.0),
        "rwkv_mu_v": uni((L, RWKV_WIDTH), 0.0, 1.0),
        "rwkv_mu_w": uni((L, D), 0.0, 1.0),
        "rwkv_mu_a": uni((L, D), 0.0, 1.0),
        "rwkv_mu_g": uni((L, D), 0.0, 1.0),
        "rwkv_w0": uni((L, RWKV_WIDTH), -6.0, -1.0),
        "rwkv_w1": nrm((L, D, DECAY_LORA), D ** -0.5),
        "rwkv_w2": nrm((L, DECAY_LORA, RWKV_WIDTH), 0.5 * DECAY_LORA ** -0.5),
        "rwkv_a0": nrm((L, RWKV_WIDTH), 0.1),
        "rwkv_a1": nrm((L, D, AAA_LORA), D ** -0.5),
        "rwkv_a2": nrm((L, AAA_LORA, RWKV_WIDTH), 0.5 * AAA_LORA ** -0.5),
        "rwkv_g1": nrm((L, D, GATE_LORA), D ** -0.5),
        "rwkv_g2": nrm((L, GATE_LORA, RWKV_WIDTH), GATE_LORA ** -0.5),
        "rwkv_k_k": 0.85 + nrm((L, RWKV_WIDTH), 0.05),
        "rwkv_k_a": 1.0 + nrm((L, RWKV_WIDTH), 0.05),
        "rwkv_r_k": nrm((L, RWKV_HEADS, RWKV_HEAD_DIM), 0.1),
        "rwkv_lnx_w": gain((L, RWKV_WIDTH)),
        "rwkv_lnx_b": nrm((L, RWKV_WIDTH), 0.02),
        "ret_gn_w": gain((L, RET_WIDTH)),
        "w_out": nrm((L, MIX_WIDTH, D), MIX_WIDTH ** -0.5),
        "norm_ffn_g": gain((L, D)),
        "ffn_w_gate": nrm((L, D, D_FF), D ** -0.5),
        "ffn_w_up": nrm((L, D, D_FF), D ** -0.5),
        "ffn_conv_w": nrm((L, CONV_WIDTH, 1, D_FF), CONV_WIDTH ** -0.5),
        "ffn_conv_b": nrm((L, D_FF), 0.02),
        "ffn_w_down": nrm((L, D_FF, D), D_FF ** -0.5),
        "norm_final_g": gain((D,)),
    }


def _fwd_reference(x, norm_mix_g, w_in, rwkv_mu_r, rwkv_mu_k, rwkv_mu_v, rwkv_mu_w,
              rwkv_mu_a, rwkv_mu_g, rwkv_w0, rwkv_w1, rwkv_w2, rwkv_a0, rwkv_a1,
              rwkv_a2, rwkv_g1, rwkv_g2, rwkv_k_k, rwkv_k_a, rwkv_r_k, rwkv_lnx_w,
              rwkv_lnx_b, ret_gn_w, w_out, norm_ffn_g, ffn_w_gate, ffn_w_up,
              ffn_conv_w, ffn_conv_b, ffn_w_down, norm_final_g):
    for l in range(DEPTH):
        h = rms_norm(x, norm_mix_g[l])
        x = x + hybrid_mixer(h, w_in[l], rwkv_mu_r[l], rwkv_mu_k[l], rwkv_mu_v[l],
                             rwkv_mu_w[l], rwkv_mu_a[l], rwkv_mu_g[l], rwkv_w0[l],
                             rwkv_w1[l], rwkv_w2[l], rwkv_a0[l], rwkv_a1[l], rwkv_a2[l],
                             rwkv_g1[l], rwkv_g2[l], rwkv_k_k[l], rwkv_k_a[l], rwkv_r_k[l],
                             rwkv_lnx_w[l], rwkv_lnx_b[l], ret_gn_w[l], w_out[l])
        h = rms_norm(x, norm_ffn_g[l])
        x = x + conv_glu(h, ffn_w_gate[l], ffn_w_up[l], ffn_conv_w[l], ffn_conv_b[l],
                         ffn_w_down[l])
    return rms_norm(x, norm_final_g)


import jax as _jax
import jax.numpy as _jnp

TWIN_FORMAT = 'train_step'
FWD_PARAMS = ['x', 'norm_mix_g', 'w_in', 'rwkv_mu_r', 'rwkv_mu_k', 'rwkv_mu_v', 'rwkv_mu_w', 'rwkv_mu_a', 'rwkv_mu_g', 'rwkv_w0', 'rwkv_w1', 'rwkv_w2', 'rwkv_a0', 'rwkv_a1', 'rwkv_a2', 'rwkv_g1', 'rwkv_g2', 'rwkv_k_k', 'rwkv_k_a', 'rwkv_r_k', 'rwkv_lnx_w', 'rwkv_lnx_b', 'ret_gn_w', 'w_out', 'norm_ffn_g', 'ffn_w_gate', 'ffn_w_up', 'ffn_conv_w', 'ffn_conv_b', 'ffn_w_down', 'norm_final_g']
TWIN_WEIGHTS = ['norm_mix_g', 'w_in', 'rwkv_mu_r', 'rwkv_mu_k', 'rwkv_mu_v', 'rwkv_mu_w', 'rwkv_mu_a', 'rwkv_mu_g', 'rwkv_w0', 'rwkv_w1', 'rwkv_w2', 'rwkv_a0', 'rwkv_a1', 'rwkv_a2', 'rwkv_g1', 'rwkv_g2', 'rwkv_k_k', 'rwkv_k_a', 'rwkv_r_k', 'rwkv_lnx_w', 'rwkv_lnx_b', 'ret_gn_w', 'w_out', 'norm_ffn_g', 'ffn_w_gate', 'ffn_w_up', 'ffn_conv_w', 'ffn_conv_b', 'ffn_w_down', 'norm_final_g']
TWIN_DIFF_INPUT = 'x'
TWIN_INPUTS = ['x', 'norm_mix_g', 'w_in', 'rwkv_mu_r', 'rwkv_mu_k', 'rwkv_mu_v', 'rwkv_mu_w', 'rwkv_mu_a', 'rwkv_mu_g', 'rwkv_w0', 'rwkv_w1', 'rwkv_w2', 'rwkv_a0', 'rwkv_a1', 'rwkv_a2', 'rwkv_g1', 'rwkv_g2', 'rwkv_k_k', 'rwkv_k_a', 'rwkv_r_k', 'rwkv_lnx_w', 'rwkv_lnx_b', 'ret_gn_w', 'w_out', 'norm_ffn_g', 'ffn_w_gate', 'ffn_w_up', 'ffn_conv_w', 'ffn_conv_b', 'ffn_w_down', 'norm_final_g', 'loss_target', 'm_norm_mix_g', 'm_w_in', 'm_rwkv_mu_r', 'm_rwkv_mu_k', 'm_rwkv_mu_v', 'm_rwkv_mu_w', 'm_rwkv_mu_a', 'm_rwkv_mu_g', 'm_rwkv_w0', 'm_rwkv_w1', 'm_rwkv_w2', 'm_rwkv_a0', 'm_rwkv_a1', 'm_rwkv_a2', 'm_rwkv_g1', 'm_rwkv_g2', 'm_rwkv_k_k', 'm_rwkv_k_a', 'm_rwkv_r_k', 'm_rwkv_lnx_w', 'm_rwkv_lnx_b', 'm_ret_gn_w', 'm_w_out', 'm_norm_ffn_g', 'm_ffn_w_gate', 'm_ffn_w_up', 'm_ffn_conv_w', 'm_ffn_conv_b', 'm_ffn_w_down', 'm_norm_final_g', 'v_norm_mix_g', 'v_w_in', 'v_rwkv_mu_r', 'v_rwkv_mu_k', 'v_rwkv_mu_v', 'v_rwkv_mu_w', 'v_rwkv_mu_a', 'v_rwkv_mu_g', 'v_rwkv_w0', 'v_rwkv_w1', 'v_rwkv_w2', 'v_rwkv_a0', 'v_rwkv_a1', 'v_rwkv_a2', 'v_rwkv_g1', 'v_rwkv_g2', 'v_rwkv_k_k', 'v_rwkv_k_a', 'v_rwkv_r_k', 'v_rwkv_lnx_w', 'v_rwkv_lnx_b', 'v_ret_gn_w', 'v_w_out', 'v_norm_ffn_g', 'v_ffn_w_gate', 'v_ffn_w_up', 'v_ffn_conv_w', 'v_ffn_conv_b', 'v_ffn_w_down', 'v_norm_final_g']
TWIN_OUTPUTS = ['loss', 'grad_x', 'grad_norm_mix_g', 'grad_w_in', 'grad_rwkv_mu_r', 'grad_rwkv_mu_k', 'grad_rwkv_mu_v', 'grad_rwkv_mu_w', 'grad_rwkv_mu_a', 'grad_rwkv_mu_g', 'grad_rwkv_w0', 'grad_rwkv_w1', 'grad_rwkv_w2', 'grad_rwkv_a0', 'grad_rwkv_a1', 'grad_rwkv_a2', 'grad_rwkv_g1', 'grad_rwkv_g2', 'grad_rwkv_k_k', 'grad_rwkv_k_a', 'grad_rwkv_r_k', 'grad_rwkv_lnx_w', 'grad_rwkv_lnx_b', 'grad_ret_gn_w', 'grad_w_out', 'grad_norm_ffn_g', 'grad_ffn_w_gate', 'grad_ffn_w_up', 'grad_ffn_conv_w', 'grad_ffn_conv_b', 'grad_ffn_w_down', 'grad_norm_final_g', 'delta_norm_mix_g', 'delta_w_in', 'delta_rwkv_mu_r', 'delta_rwkv_mu_k', 'delta_rwkv_mu_v', 'delta_rwkv_mu_w', 'delta_rwkv_mu_a', 'delta_rwkv_mu_g', 'delta_rwkv_w0', 'delta_rwkv_w1', 'delta_rwkv_w2', 'delta_rwkv_a0', 'delta_rwkv_a1', 'delta_rwkv_a2', 'delta_rwkv_g1', 'delta_rwkv_g2', 'delta_rwkv_k_k', 'delta_rwkv_k_a', 'delta_rwkv_r_k', 'delta_rwkv_lnx_w', 'delta_rwkv_lnx_b', 'delta_ret_gn_w', 'delta_w_out', 'delta_norm_ffn_g', 'delta_ffn_w_gate', 'delta_ffn_w_up', 'delta_ffn_conv_w', 'delta_ffn_conv_b', 'delta_ffn_w_down', 'delta_norm_final_g', 'new_m_norm_mix_g', 'new_m_w_in', 'new_m_rwkv_mu_r', 'new_m_rwkv_mu_k', 'new_m_rwkv_mu_v', 'new_m_rwkv_mu_w', 'new_m_rwkv_mu_a', 'new_m_rwkv_mu_g', 'new_m_rwkv_w0', 'new_m_rwkv_w1', 'new_m_rwkv_w2', 'new_m_rwkv_a0', 'new_m_rwkv_a1', 'new_m_rwkv_a2', 'new_m_rwkv_g1', 'new_m_rwkv_g2', 'new_m_rwkv_k_k', 'new_m_rwkv_k_a', 'new_m_rwkv_r_k', 'new_m_rwkv_lnx_w', 'new_m_rwkv_lnx_b', 'new_m_ret_gn_w', 'new_m_w_out', 'new_m_norm_ffn_g', 'new_m_ffn_w_gate', 'new_m_ffn_w_up', 'new_m_ffn_conv_w', 'new_m_ffn_conv_b', 'new_m_ffn_w_down', 'new_m_norm_final_g', 'new_v_norm_mix_g', 'new_v_w_in', 'new_v_rwkv_mu_r', 'new_v_rwkv_mu_k', 'new_v_rwkv_mu_v', 'new_v_rwkv_mu_w', 'new_v_rwkv_mu_a', 'new_v_rwkv_mu_g', 'new_v_rwkv_w0', 'new_v_rwkv_w1', 'new_v_rwkv_w2', 'new_v_rwkv_a0', 'new_v_rwkv_a1', 'new_v_rwkv_a2', 'new_v_rwkv_g1', 'new_v_rwkv_g2', 'new_v_rwkv_k_k', 'new_v_rwkv_k_a', 'new_v_rwkv_r_k', 'new_v_rwkv_lnx_w', 'new_v_rwkv_lnx_b', 'new_v_ret_gn_w', 'new_v_w_out', 'new_v_norm_ffn_g', 'new_v_ffn_w_gate', 'new_v_ffn_w_up', 'new_v_ffn_conv_w', 'new_v_ffn_conv_b', 'new_v_ffn_w_down', 'new_v_norm_final_g']
TWIN_LEAF_KINDS = {'loss': 'loss', 'grad_x': 'grad_x', 'grad_norm_mix_g': 'grad_w', 'grad_w_in': 'grad_w', 'grad_rwkv_mu_r': 'grad_w', 'grad_rwkv_mu_k': 'grad_w', 'grad_rwkv_mu_v': 'grad_w', 'grad_rwkv_mu_w': 'grad_w', 'grad_rwkv_mu_a': 'grad_w', 'grad_rwkv_mu_g': 'grad_w', 'grad_rwkv_w0': 'grad_w', 'grad_rwkv_w1': 'grad_w', 'grad_rwkv_w2': 'grad_w', 'grad_rwkv_a0': 'grad_w', 'grad_rwkv_a1': 'grad_w', 'grad_rwkv_a2': 'grad_w', 'grad_rwkv_g1': 'grad_w', 'grad_rwkv_g2': 'grad_w', 'grad_rwkv_k_k': 'grad_w', 'grad_rwkv_k_a': 'grad_w', 'grad_rwkv_r_k': 'grad_w', 'grad_rwkv_lnx_w': 'grad_w', 'grad_rwkv_lnx_b': 'grad_w', 'grad_ret_gn_w': 'grad_w', 'grad_w_out': 'grad_w', 'grad_norm_ffn_g': 'grad_w', 'grad_ffn_w_gate': 'grad_w', 'grad_ffn_w_up': 'grad_w', 'grad_ffn_conv_w': 'grad_w', 'grad_ffn_conv_b': 'grad_w', 'grad_ffn_w_down': 'grad_w', 'grad_norm_final_g': 'grad_w', 'delta_norm_mix_g': 'delta_w', 'delta_w_in': 'delta_w', 'delta_rwkv_mu_r': 'delta_w', 'delta_rwkv_mu_k': 'delta_w', 'delta_rwkv_mu_v': 'delta_w', 'delta_rwkv_mu_w': 'delta_w', 'delta_rwkv_mu_a': 'delta_w', 'delta_rwkv_mu_g': 'delta_w', 'delta_rwkv_w0': 'delta_w', 'delta_rwkv_w1': 'delta_w', 'delta_rwkv_w2': 'delta_w', 'delta_rwkv_a0': 'delta_w', 'delta_rwkv_a1': 'delta_w', 'delta_rwkv_a2': 'delta_w', 'delta_rwkv_g1': 'delta_w', 'delta_rwkv_g2': 'delta_w', 'delta_rwkv_k_k': 'delta_w', 'delta_rwkv_k_a': 'delta_w', 'delta_rwkv_r_k': 'delta_w', 'delta_rwkv_lnx_w': 'delta_w', 'delta_rwkv_lnx_b': 'delta_w', 'delta_ret_gn_w': 'delta_w', 'delta_w_out': 'delta_w', 'delta_norm_ffn_g': 'delta_w', 'delta_ffn_w_gate': 'delta_w', 'delta_ffn_w_up': 'delta_w', 'delta_ffn_conv_w': 'delta_w', 'delta_ffn_conv_b': 'delta_w', 'delta_ffn_w_down': 'delta_w', 'delta_norm_final_g': 'delta_w', 'new_m_norm_mix_g': 'new_m', 'new_m_w_in': 'new_m', 'new_m_rwkv_mu_r': 'new_m', 'new_m_rwkv_mu_k': 'new_m', 'new_m_rwkv_mu_v': 'new_m', 'new_m_rwkv_mu_w': 'new_m', 'new_m_rwkv_mu_a': 'new_m', 'new_m_rwkv_mu_g': 'new_m', 'new_m_rwkv_w0': 'new_m', 'new_m_rwkv_w1': 'new_m', 'new_m_rwkv_w2': 'new_m', 'new_m_rwkv_a0': 'new_m', 'new_m_rwkv_a1': 'new_m', 'new_m_rwkv_a2': 'new_m', 'new_m_rwkv_g1': 'new_m', 'new_m_rwkv_g2': 'new_m', 'new_m_rwkv_k_k': 'new_m', 'new_m_rwkv_k_a': 'new_m', 'new_m_rwkv_r_k': 'new_m', 'new_m_rwkv_lnx_w': 'new_m', 'new_m_rwkv_lnx_b': 'new_m', 'new_m_ret_gn_w': 'new_m', 'new_m_w_out': 'new_m', 'new_m_norm_ffn_g': 'new_m', 'new_m_ffn_w_gate': 'new_m', 'new_m_ffn_w_up': 'new_m', 'new_m_ffn_conv_w': 'new_m', 'new_m_ffn_conv_b': 'new_m', 'new_m_ffn_w_down': 'new_m', 'new_m_norm_final_g': 'new_m', 'new_v_norm_mix_g': 'new_v', 'new_v_w_in': 'new_v', 'new_v_rwkv_mu_r': 'new_v', 'new_v_rwkv_mu_k': 'new_v', 'new_v_rwkv_mu_v': 'new_v', 'new_v_rwkv_mu_w': 'new_v', 'new_v_rwkv_mu_a': 'new_v', 'new_v_rwkv_mu_g': 'new_v', 'new_v_rwkv_w0': 'new_v', 'new_v_rwkv_w1': 'new_v', 'new_v_rwkv_w2': 'new_v', 'new_v_rwkv_a0': 'new_v', 'new_v_rwkv_a1': 'new_v', 'new_v_rwkv_a2': 'new_v', 'new_v_rwkv_g1': 'new_v', 'new_v_rwkv_g2': 'new_v', 'new_v_rwkv_k_k': 'new_v', 'new_v_rwkv_k_a': 'new_v', 'new_v_rwkv_r_k': 'new_v', 'new_v_rwkv_lnx_w': 'new_v', 'new_v_rwkv_lnx_b': 'new_v', 'new_v_ret_gn_w': 'new_v', 'new_v_w_out': 'new_v', 'new_v_norm_ffn_g': 'new_v', 'new_v_ffn_w_gate': 'new_v', 'new_v_ffn_w_up': 'new_v', 'new_v_ffn_conv_w': 'new_v', 'new_v_ffn_conv_b': 'new_v', 'new_v_ffn_w_down': 'new_v', 'new_v_norm_final_g': 'new_v'}


def _forward(args):
    return _fwd_reference(*[args[k] for k in FWD_PARAMS])


def _output_shape():
    out = _jax.eval_shape(lambda: _forward(_fwd_setup_inputs(0)))
    return out.shape, out.dtype

N_MICROBATCH = 1
ADAM_LR = 0.001
ADAM_B1 = 0.9
ADAM_B2 = 0.999
ADAM_EPS = 1e-08
ADAM_WD = 0.01
ADAM_STEP = 10
PER_EXAMPLE_BATCH_AXIS = {'x': 0, 'loss_target': 0}
SHARED_INPUTS = []
_WEIGHT_DTYPES = {'norm_mix_g': _jnp.float32, 'w_in': _jnp.float32, 'rwkv_mu_r': _jnp.float32, 'rwkv_mu_k': _jnp.float32, 'rwkv_mu_v': _jnp.float32, 'rwkv_mu_w': _jnp.float32, 'rwkv_mu_a': _jnp.float32, 'rwkv_mu_g': _jnp.float32, 'rwkv_w0': _jnp.float32, 'rwkv_w1': _jnp.float32, 'rwkv_w2': _jnp.float32, 'rwkv_a0': _jnp.float32, 'rwkv_a1': _jnp.float32, 'rwkv_a2': _jnp.float32, 'rwkv_g1': _jnp.float32, 'rwkv_g2': _jnp.float32, 'rwkv_k_k': _jnp.float32, 'rwkv_k_a': _jnp.float32, 'rwkv_r_k': _jnp.float32, 'rwkv_lnx_w': _jnp.float32, 'rwkv_lnx_b': _jnp.float32, 'ret_gn_w': _jnp.float32, 'w_out': _jnp.float32, 'norm_ffn_g': _jnp.float32, 'ffn_w_gate': _jnp.float32, 'ffn_w_up': _jnp.float32, 'ffn_conv_w': _jnp.float32, 'ffn_conv_b': _jnp.float32, 'ffn_w_down': _jnp.float32, 'norm_final_g': _jnp.float32}
MOMENT_SCALE = {'norm_mix_g': 1.369888e-01, 'w_in': 7.188074e-02, 'rwkv_mu_r': 1.180240e-01, 'rwkv_mu_k': 1.156046e-01, 'rwkv_mu_v': 1.080135e-01, 'rwkv_mu_w': 8.790543e-04, 'rwkv_mu_a': 1.603411e-02, 'rwkv_mu_g': 2.682138e-02, 'rwkv_w0': 2.493150e-02, 'rwkv_w1': 4.480541e-03, 'rwkv_w2': 3.050804e-03, 'rwkv_a0': 2.622110e-02, 'rwkv_a1': 3.576642e-02, 'rwkv_a2': 2.561099e-02, 'rwkv_g1': 4.475546e-02, 'rwkv_g2': 6.929736e-02, 'rwkv_k_k': 7.414069e-02, 'rwkv_k_a': 7.272060e-02, 'rwkv_r_k': 1.414991e-01, 'rwkv_lnx_w': 7.102451e-02, 'rwkv_lnx_b': 6.955471e-02, 'ret_gn_w': 7.163128e-02, 'w_out': 6.863631e-02, 'norm_ffn_g': 8.778713e-02, 'ffn_w_gate': 3.699566e-02, 'ffn_w_up': 3.584681e-02, 'ffn_conv_w': 3.738133e-02, 'ffn_conv_b': 3.540499e-02, 'ffn_w_down': 5.932377e-02, 'norm_final_g': 1.599780e+01}


def _to_microbatches(a, axis):
    t = _jnp.moveaxis(a, axis, 0)
    t = t.reshape((N_MICROBATCH, t.shape[0] // N_MICROBATCH) + t.shape[1:])
    return _jnp.moveaxis(t, 1, axis + 1)


def setup_inputs(seed: int = 0) -> dict:
    inp = _fwd_setup_inputs(seed)
    key = _jax.random.fold_in(_jax.random.key(seed), 7919)
    shape, _ = _output_shape()
    out = dict(inp)
    out["loss_target"] = _jax.random.normal(_jax.random.fold_in(key, 0), shape, _jnp.float32)
    for i, name in enumerate(TWIN_WEIGHTS):
        w = inp[name].astype(_jnp.float32)
        if MOMENT_SCALE is None:
            s = _jnp.sqrt(_jnp.mean(_jnp.square(w)) + 1e-30)
        else:
            s = MOMENT_SCALE[name]
        km, kv = _jax.random.split(_jax.random.fold_in(key, i + 1))
        out[name] = w
        out["m_" + name] = s * _jax.random.normal(km, w.shape, _jnp.float32)
        out["v_" + name] = (s * s) * _jax.random.uniform(kv, w.shape, _jnp.float32, 0.5, 1.5)
    if N_MICROBATCH > 1:
        for name, axis in PER_EXAMPLE_BATCH_AXIS.items():
            out[name] = _to_microbatches(out[name], axis)
    return {'x': out['x'], 'norm_mix_g': out['norm_mix_g'], 'w_in': out['w_in'], 'rwkv_mu_r': out['rwkv_mu_r'], 'rwkv_mu_k': out['rwkv_mu_k'], 'rwkv_mu_v': out['rwkv_mu_v'], 'rwkv_mu_w': out['rwkv_mu_w'], 'rwkv_mu_a': out['rwkv_mu_a'], 'rwkv_mu_g': out['rwkv_mu_g'], 'rwkv_w0': out['rwkv_w0'], 'rwkv_w1': out['rwkv_w1'], 'rwkv_w2': out['rwkv_w2'], 'rwkv_a0': out['rwkv_a0'], 'rwkv_a1': out['rwkv_a1'], 'rwkv_a2': out['rwkv_a2'], 'rwkv_g1': out['rwkv_g1'], 'rwkv_g2': out['rwkv_g2'], 'rwkv_k_k': out['rwkv_k_k'], 'rwkv_k_a': out['rwkv_k_a'], 'rwkv_r_k': out['rwkv_r_k'], 'rwkv_lnx_w': out['rwkv_lnx_w'], 'rwkv_lnx_b': out['rwkv_lnx_b'], 'ret_gn_w': out['ret_gn_w'], 'w_out': out['w_out'], 'norm_ffn_g': out['norm_ffn_g'], 'ffn_w_gate': out['ffn_w_gate'], 'ffn_w_up': out['ffn_w_up'], 'ffn_conv_w': out['ffn_conv_w'], 'ffn_conv_b': out['ffn_conv_b'], 'ffn_w_down': out['ffn_w_down'], 'norm_final_g': out['norm_final_g'], 'loss_target': out['loss_target'], 'm_norm_mix_g': out['m_norm_mix_g'], 'm_w_in': out['m_w_in'], 'm_rwkv_mu_r': out['m_rwkv_mu_r'], 'm_rwkv_mu_k': out['m_rwkv_mu_k'], 'm_rwkv_mu_v': out['m_rwkv_mu_v'], 'm_rwkv_mu_w': out['m_rwkv_mu_w'], 'm_rwkv_mu_a': out['m_rwkv_mu_a'], 'm_rwkv_mu_g': out['m_rwkv_mu_g'], 'm_rwkv_w0': out['m_rwkv_w0'], 'm_rwkv_w1': out['m_rwkv_w1'], 'm_rwkv_w2': out['m_rwkv_w2'], 'm_rwkv_a0': out['m_rwkv_a0'], 'm_rwkv_a1': out['m_rwkv_a1'], 'm_rwkv_a2': out['m_rwkv_a2'], 'm_rwkv_g1': out['m_rwkv_g1'], 'm_rwkv_g2': out['m_rwkv_g2'], 'm_rwkv_k_k': out['m_rwkv_k_k'], 'm_rwkv_k_a': out['m_rwkv_k_a'], 'm_rwkv_r_k': out['m_rwkv_r_k'], 'm_rwkv_lnx_w': out['m_rwkv_lnx_w'], 'm_rwkv_lnx_b': out['m_rwkv_lnx_b'], 'm_ret_gn_w': out['m_ret_gn_w'], 'm_w_out': out['m_w_out'], 'm_norm_ffn_g': out['m_norm_ffn_g'], 'm_ffn_w_gate': out['m_ffn_w_gate'], 'm_ffn_w_up': out['m_ffn_w_up'], 'm_ffn_conv_w': out['m_ffn_conv_w'], 'm_ffn_conv_b': out['m_ffn_conv_b'], 'm_ffn_w_down': out['m_ffn_w_down'], 'm_norm_final_g': out['m_norm_final_g'], 'v_norm_mix_g': out['v_norm_mix_g'], 'v_w_in': out['v_w_in'], 'v_rwkv_mu_r': out['v_rwkv_mu_r'], 'v_rwkv_mu_k': out['v_rwkv_mu_k'], 'v_rwkv_mu_v': out['v_rwkv_mu_v'], 'v_rwkv_mu_w': out['v_rwkv_mu_w'], 'v_rwkv_mu_a': out['v_rwkv_mu_a'], 'v_rwkv_mu_g': out['v_rwkv_mu_g'], 'v_rwkv_w0': out['v_rwkv_w0'], 'v_rwkv_w1': out['v_rwkv_w1'], 'v_rwkv_w2': out['v_rwkv_w2'], 'v_rwkv_a0': out['v_rwkv_a0'], 'v_rwkv_a1': out['v_rwkv_a1'], 'v_rwkv_a2': out['v_rwkv_a2'], 'v_rwkv_g1': out['v_rwkv_g1'], 'v_rwkv_g2': out['v_rwkv_g2'], 'v_rwkv_k_k': out['v_rwkv_k_k'], 'v_rwkv_k_a': out['v_rwkv_k_a'], 'v_rwkv_r_k': out['v_rwkv_r_k'], 'v_rwkv_lnx_w': out['v_rwkv_lnx_w'], 'v_rwkv_lnx_b': out['v_rwkv_lnx_b'], 'v_ret_gn_w': out['v_ret_gn_w'], 'v_w_out': out['v_w_out'], 'v_norm_ffn_g': out['v_norm_ffn_g'], 'v_ffn_w_gate': out['v_ffn_w_gate'], 'v_ffn_w_up': out['v_ffn_w_up'], 'v_ffn_conv_w': out['v_ffn_conv_w'], 'v_ffn_conv_b': out['v_ffn_conv_b'], 'v_ffn_w_down': out['v_ffn_w_down'], 'v_norm_final_g': out['v_norm_final_g']}


def _loss(weights, diff, rest, loss_target):
    with _jax.named_scope("forward"):
        args = {**rest, TWIN_DIFF_INPUT: diff, **{k: w.astype(_WEIGHT_DTYPES[k]) for k, w in weights.items()}}
        y = _forward(args)
    with _jax.named_scope("loss_head"):
        err = _jnp.square(y.astype(_jnp.float32) - loss_target)
        return 0.5 * _jnp.sum(_jnp.mean(err, axis=-1)) if err.ndim else 0.5 * err


def _adamw(w, g, m, v):
    m = ADAM_B1 * m + (1.0 - ADAM_B1) * g
    v = ADAM_B2 * v + (1.0 - ADAM_B2) * _jnp.square(g)
    m_hat = m / (1.0 - ADAM_B1 ** ADAM_STEP)
    v_hat = v / (1.0 - ADAM_B2 ** ADAM_STEP)
    delta = -ADAM_LR * (m_hat / (_jnp.sqrt(v_hat) + ADAM_EPS) + ADAM_WD * w)
    return delta, m, v


def reference(x, norm_mix_g, w_in, rwkv_mu_r, rwkv_mu_k, rwkv_mu_v, rwkv_mu_w, rwkv_mu_a, rwkv_mu_g, rwkv_w0, rwkv_w1, rwkv_w2, rwkv_a0, rwkv_a1, rwkv_a2, rwkv_g1, rwkv_g2, rwkv_k_k, rwkv_k_a, rwkv_r_k, rwkv_lnx_w, rwkv_lnx_b, ret_gn_w, w_out, norm_ffn_g, ffn_w_gate, ffn_w_up, ffn_conv_w, ffn_conv_b, ffn_w_down, norm_final_g, loss_target, m_norm_mix_g, m_w_in, m_rwkv_mu_r, m_rwkv_mu_k, m_rwkv_mu_v, m_rwkv_mu_w, m_rwkv_mu_a, m_rwkv_mu_g, m_rwkv_w0, m_rwkv_w1, m_rwkv_w2, m_rwkv_a0, m_rwkv_a1, m_rwkv_a2, m_rwkv_g1, m_rwkv_g2, m_rwkv_k_k, m_rwkv_k_a, m_rwkv_r_k, m_rwkv_lnx_w, m_rwkv_lnx_b, m_ret_gn_w, m_w_out, m_norm_ffn_g, m_ffn_w_gate, m_ffn_w_up, m_ffn_conv_w, m_ffn_conv_b, m_ffn_w_down, m_norm_final_g, v_norm_mix_g, v_w_in, v_rwkv_mu_r, v_rwkv_mu_k, v_rwkv_mu_v, v_rwkv_mu_w, v_rwkv_mu_a, v_rwkv_mu_g, v_rwkv_w0, v_rwkv_w1, v_rwkv_w2, v_rwkv_a0, v_rwkv_a1, v_rwkv_a2, v_rwkv_g1, v_rwkv_g2, v_rwkv_k_k, v_rwkv_k_a, v_rwkv_r_k, v_rwkv_lnx_w, v_rwkv_lnx_b, v_ret_gn_w, v_w_out, v_norm_ffn_g, v_ffn_w_gate, v_ffn_w_up, v_ffn_conv_w, v_ffn_conv_b, v_ffn_w_down, v_norm_final_g):
    given = dict(x=x, norm_mix_g=norm_mix_g, w_in=w_in, rwkv_mu_r=rwkv_mu_r, rwkv_mu_k=rwkv_mu_k, rwkv_mu_v=rwkv_mu_v, rwkv_mu_w=rwkv_mu_w, rwkv_mu_a=rwkv_mu_a, rwkv_mu_g=rwkv_mu_g, rwkv_w0=rwkv_w0, rwkv_w1=rwkv_w1, rwkv_w2=rwkv_w2, rwkv_a0=rwkv_a0, rwkv_a1=rwkv_a1, rwkv_a2=rwkv_a2, rwkv_g1=rwkv_g1, rwkv_g2=rwkv_g2, rwkv_k_k=rwkv_k_k, rwkv_k_a=rwkv_k_a, rwkv_r_k=rwkv_r_k, rwkv_lnx_w=rwkv_lnx_w, rwkv_lnx_b=rwkv_lnx_b, ret_gn_w=ret_gn_w, w_out=w_out, norm_ffn_g=norm_ffn_g, ffn_w_gate=ffn_w_gate, ffn_w_up=ffn_w_up, ffn_conv_w=ffn_conv_w, ffn_conv_b=ffn_conv_b, ffn_w_down=ffn_w_down, norm_final_g=norm_final_g, loss_target=loss_target, m_norm_mix_g=m_norm_mix_g, m_w_in=m_w_in, m_rwkv_mu_r=m_rwkv_mu_r, m_rwkv_mu_k=m_rwkv_mu_k, m_rwkv_mu_v=m_rwkv_mu_v, m_rwkv_mu_w=m_rwkv_mu_w, m_rwkv_mu_a=m_rwkv_mu_a, m_rwkv_mu_g=m_rwkv_mu_g, m_rwkv_w0=m_rwkv_w0, m_rwkv_w1=m_rwkv_w1, m_rwkv_w2=m_rwkv_w2, m_rwkv_a0=m_rwkv_a0, m_rwkv_a1=m_rwkv_a1, m_rwkv_a2=m_rwkv_a2, m_rwkv_g1=m_rwkv_g1, m_rwkv_g2=m_rwkv_g2, m_rwkv_k_k=m_rwkv_k_k, m_rwkv_k_a=m_rwkv_k_a, m_rwkv_r_k=m_rwkv_r_k, m_rwkv_lnx_w=m_rwkv_lnx_w, m_rwkv_lnx_b=m_rwkv_lnx_b, m_ret_gn_w=m_ret_gn_w, m_w_out=m_w_out, m_norm_ffn_g=m_norm_ffn_g, m_ffn_w_gate=m_ffn_w_gate, m_ffn_w_up=m_ffn_w_up, m_ffn_conv_w=m_ffn_conv_w, m_ffn_conv_b=m_ffn_conv_b, m_ffn_w_down=m_ffn_w_down, m_norm_final_g=m_norm_final_g, v_norm_mix_g=v_norm_mix_g, v_w_in=v_w_in, v_rwkv_mu_r=v_rwkv_mu_r, v_rwkv_mu_k=v_rwkv_mu_k, v_rwkv_mu_v=v_rwkv_mu_v, v_rwkv_mu_w=v_rwkv_mu_w, v_rwkv_mu_a=v_rwkv_mu_a, v_rwkv_mu_g=v_rwkv_mu_g, v_rwkv_w0=v_rwkv_w0, v_rwkv_w1=v_rwkv_w1, v_rwkv_w2=v_rwkv_w2, v_rwkv_a0=v_rwkv_a0, v_rwkv_a1=v_rwkv_a1, v_rwkv_a2=v_rwkv_a2, v_rwkv_g1=v_rwkv_g1, v_rwkv_g2=v_rwkv_g2, v_rwkv_k_k=v_rwkv_k_k, v_rwkv_k_a=v_rwkv_k_a, v_rwkv_r_k=v_rwkv_r_k, v_rwkv_lnx_w=v_rwkv_lnx_w, v_rwkv_lnx_b=v_rwkv_lnx_b, v_ret_gn_w=v_ret_gn_w, v_w_out=v_w_out, v_norm_ffn_g=v_norm_ffn_g, v_ffn_w_gate=v_ffn_w_gate, v_ffn_w_up=v_ffn_w_up, v_ffn_conv_w=v_ffn_conv_w, v_ffn_conv_b=v_ffn_conv_b, v_ffn_w_down=v_ffn_w_down, v_norm_final_g=v_norm_final_g)
    weights = {n: given[n] for n in TWIN_WEIGHTS}
    shared = {n: given[n] for n in SHARED_INPUTS}
    per_example = {n: given[n] for n in ['x']}
    grad_fn = _jax.value_and_grad(_loss, argnums=(0, 1))

    def one_microbatch(ex, loss_target):
        ex = dict(ex)
        diff = ex.pop(TWIN_DIFF_INPUT)
        return grad_fn(weights, diff, {**shared, **ex}, loss_target)

    if N_MICROBATCH == 1:
        loss, (grad_w, grad_x) = one_microbatch(per_example, given["loss_target"])
    else:
        def body(carry, xs):
            loss_sum, grad_sum = carry
            l_k, (gw_k, gx_k) = one_microbatch(xs[0], xs[1])
            with _jax.named_scope("update"):
                return (loss_sum + l_k, _jax.tree.map(_jnp.add, grad_sum, gw_k)), gx_k

        init = (_jnp.zeros((), _jnp.float32), _jax.tree.map(_jnp.zeros_like, weights))
        (loss, grad_w), grad_x = _jax.lax.scan(body, init, (per_example, given["loss_target"]))
    with _jax.named_scope("update"):
        delta_w, new_m, new_v = {}, {}, {}
        for n in TWIN_WEIGHTS:
            delta_w[n], new_m[n], new_v[n] = _adamw(weights[n], grad_w[n], given["m_" + n], given["v_" + n])
    return (loss, grad_x, *[grad_w[n] for n in TWIN_WEIGHTS], *[delta_w[n] for n in TWIN_WEIGHTS],
            *[new_m[n] for n in TWIN_WEIGHTS], *[new_v[n] for n in TWIN_WEIGHTS])
```

```python
import functools
import math

import numpy as np
import jax
import jax.numpy as jnp
from jax import lax
from jax.experimental import pallas as pl
from jax.experimental.pallas import tpu as pltpu

F32 = jnp.float32
BF16 = jnp.bfloat16

N_DEV = 8
D_MODEL = 1024
RWKV_HEADS = 8
RWKV_HEAD_DIM = 64
RWKV_WIDTH = 512
RET_HEADS = 4
RET_HEAD_DIM = 128
RET_WIDTH = 512
LORA_PAD = 128
D_FF = 2816
NORM_EPS = 1e-6
RWKV_GN_EPS = 64e-5
RET_GN_EPS = 1e-5
ROPE_BASE = 10000.0
ADAM_LR, ADAM_B1, ADAM_B2, ADAM_EPS, ADAM_WD, ADAM_STEP = 0.001, 0.9, 0.999, 1e-08, 0.01, 10

VMEM_LIMIT = 56 * 1024 * 1024
TOK_BLOCK = 256
SCAN_CHUNK = 128
SCAN_CKPT = 32
ATT_BLOCK = 256
BF16_SEG = 2048
F32_SEG = 1024

WEIGHT_NAMES = ['norm_mix_g', 'w_in', 'rwkv_mu_r', 'rwkv_mu_k', 'rwkv_mu_v', 'rwkv_mu_w', 'rwkv_mu_a',
                'rwkv_mu_g', 'rwkv_w0', 'rwkv_w1', 'rwkv_w2', 'rwkv_a0', 'rwkv_a1', 'rwkv_a2', 'rwkv_g1',
                'rwkv_g2', 'rwkv_k_k', 'rwkv_k_a', 'rwkv_r_k', 'rwkv_lnx_w', 'rwkv_lnx_b', 'ret_gn_w',
                'w_out', 'norm_ffn_g', 'ffn_w_gate', 'ffn_w_up', 'ffn_conv_w', 'ffn_conv_b', 'ffn_w_down',
                'norm_final_g']
SHARDED = {
    'w_in': ((1, 1024, 3584), 2), 'rwkv_w1': ((1, 1024, 64), 1), 'rwkv_w2': ((1, 64, 512), 2),
    'rwkv_a1': ((1, 1024, 64), 1), 'rwkv_a2': ((1, 64, 512), 2), 'rwkv_g1': ((1, 1024, 128), 1),
    'rwkv_g2': ((1, 128, 512), 2), 'w_out': ((1, 1024, 1024), 1), 'ffn_w_gate': ((1, 1024, 2816), 2),
    'ffn_w_up': ((1, 1024, 2816), 2), 'ffn_conv_w': ((1, 3, 1, 2816), 3), 'ffn_w_down': ((1, 2816, 1024), 1),
}
SHARDED_NAMES = [n for n in WEIGHT_NAMES if n in SHARDED]
REPL_NAMES = [n for n in WEIGHT_NAMES if n not in SHARDED]


def _cparams(n_grid):
    return pltpu.CompilerParams(dimension_semantics=("arbitrary",) * n_grid, vmem_limit_bytes=VMEM_LIMIT)


def _round_up(n, m):
    return (n + m - 1) // m * m


def _tile(n, cap):
    best = None
    for t in range(128, min(n, cap) + 1, 128):
        if n % t == 0:
            best = t
    assert best is not None, n
    return best


@jax.custom_vjp
def _bdot(x, w):
    return jnp.dot(x.astype(BF16), w.astype(BF16), preferred_element_type=F32)


def _bdot_fwd(x, w):
    return _bdot(x, w), (x, w)


def _bdot_bwd(res, g):
    x, w = res
    gb = g.astype(BF16)
    dx = lax.dot_general(gb, w.astype(BF16), (((1,), (1,)), ((), ())), preferred_element_type=F32)
    dw = lax.dot_general(x.astype(BF16), gb, (((0,), (0,)), ((), ())), preferred_element_type=F32)
    return dx, dw.astype(w.dtype)


_bdot.defvjp(_bdot_fwd, _bdot_bwd)


@jax.custom_vjp
def _shift_rows(x, prev):
    rolled = pltpu.roll(x, 1, 0)
    row = lax.broadcasted_iota(jnp.int32, x.shape, 0)
    return jnp.where(row == 0, jnp.broadcast_to(prev, x.shape), rolled)


def _shift_rows_fwd(x, prev):
    return _shift_rows(x, prev), None


def _shift_rows_bwd(_, g):
    n = g.shape[0]
    rolled = pltpu.roll(g, n - 1, 0)
    row = lax.broadcasted_iota(jnp.int32, g.shape, 0)
    return jnp.where(row == n - 1, 0.0, rolled), g[0:1, :]


_shift_rows.defvjp(_shift_rows_fwd, _shift_rows_bwd)


@jax.custom_vjp
def _swap_halves(x):
    return pltpu.roll(x, 64, 1)


_swap_halves.defvjp(lambda x: (_swap_halves(x), None), lambda _, g: (pltpu.roll(g, 64, 1),))


def _sigmoid(x):
    return 1.0 / (1.0 + jnp.exp(-x))


def _softplus(x):
    return jnp.maximum(x, 0.0) + jnp.log(1.0 + jnp.exp(-jnp.abs(x)))


def _rms_fn(x, g):
    return x * lax.rsqrt(jnp.mean(x * x, axis=-1, keepdims=True) + NORM_EPS) * g


def _pre_a_fn(h1, h1p, p, pp, mu_w, mu_a, mu_g, mu_r, mu_k, mu_v, w0, w1, w2, a0, a1, a2, g1, g2):
    W = RWKV_WIDTH
    h1s = _shift_rows(h1, h1p)
    ps = _shift_rows(p, pp)
    dx = h1s - h1
    xw = h1 + dx * mu_w
    xa = h1 + dx * mu_a
    xg = h1 + dx * mu_g
    dp = ps - p
    r = p[:, 0:W] + dp[:, 0:W] * mu_r
    k0 = p[:, W:2 * W] + dp[:, W:2 * W] * mu_k
    v = p[:, 2 * W:3 * W] + dp[:, 2 * W:3 * W] * mu_v
    wl = w0 + _bdot(jnp.tanh(_bdot(xw, w1)), w2)
    w_log = -_softplus(-wl) - 0.5
    decay = jnp.exp(-jnp.exp(w_log))
    a = _sigmoid(a0 + _bdot(_bdot(xa, a1), a2))
    g = _bdot(_sigmoid(_bdot(xg, g1)), g2)
    return r, k0, v, decay, a, g


def _pre_b_fn(k0, a, k_k, k_a):
    kkr = k0 * k_k
    nrm = jnp.sqrt(jnp.sum(kkr * kkr, axis=-1, keepdims=True))
    kk = kkr / jnp.maximum(nrm, 1e-12)
    k = k0 * (1.0 + (a - 1.0) * k_a)
    return -kk, k, kk * a


def _rwkv_post_fn(y, r, k, v, g, lnx_w, lnx_b, r_k):
    mu = jnp.mean(y, axis=-1, keepdims=True)
    yc = y - mu
    var = jnp.mean(yc * yc, axis=-1, keepdims=True)
    yn = yc * lax.rsqrt(var + RWKV_GN_EPS) * lnx_w + lnx_b
    bonus = jnp.sum(r * k * r_k, axis=-1, keepdims=True) * v
    return ((yn + bonus) * g,)


def _rotary_fn(cos2, sin2, q, k):
    qs, ks = [], []
    for h in range(RET_HEADS):
        sl = slice(h * RET_HEAD_DIM, (h + 1) * RET_HEAD_DIM)
        qh, kh = q[:, sl], k[:, sl]
        qs.append(qh * cos2 + _swap_halves(qh) * sin2)
        ks.append((kh * cos2 + _swap_halves(kh) * sin2) * (RET_HEAD_DIM ** -0.5))
    return jnp.concatenate(qs, axis=1), jnp.concatenate(ks, axis=1)


def _ret_post_fn(y, gp, gn_w):
    outs = []
    for h in range(RET_HEADS):
        sl = slice(h * RET_HEAD_DIM, (h + 1) * RET_HEAD_DIM)
        yh = y[:, sl]
        mu = jnp.mean(yh, axis=-1, keepdims=True)
        yc = yh - mu
        var = jnp.mean(yc * yc, axis=-1, keepdims=True)
        outs.append(yc * lax.rsqrt(var + RET_GN_EPS) * gn_w[:, sl])
    yn = jnp.concatenate(outs, axis=1)
    return (gp * _sigmoid(gp) * yn,)


def _blk_spec(a, tb, rev_nb=None):
    nd = a.ndim
    if rev_nb is None:
        return pl.BlockSpec((tb,) + a.shape[1:], lambda i: (i,) + (0,) * (nd - 1))
    return pl.BlockSpec((tb,) + a.shape[1:], lambda i: (rev_nb - 1 - i,) + (0,) * (nd - 1))


def _full_spec(a):
    nd = a.ndim
    return pl.BlockSpec(a.shape, lambda i: (0,) * nd)


def _tok_fwd(name, fn, toks, consts, out_tails, tb=TOK_BLOCK):
    n_in = len(toks) + len(consts)
    tn = toks[0].shape[0]

    def body(*refs):
        outs = fn(*[r[...] for r in refs[:n_in]])
        for r, o in zip(refs[n_in:], outs):
            r[...] = o

    out_shape = [jax.ShapeDtypeStruct((tn,) + tuple(s), F32) for s in out_tails]
    return pl.pallas_call(
        body, name=name, grid=(tn // tb,),
        in_specs=[_blk_spec(a, tb) for a in toks] + [_full_spec(c) for c in consts],
        out_specs=[_blk_spec(o, tb) for o in out_shape], out_shape=out_shape,
        compiler_params=_cparams(1))(*toks, *consts)


def _tok_bwd(name, fn, aux, toks, consts, cts, add=None, tb=TOK_BLOCK):
    n_aux, n_tok, n_c = len(aux), len(toks), len(consts)
    ct_groups = [c if isinstance(c, (tuple, list)) else (c,) for c in cts]
    ct_flat = [a for grp in ct_groups for a in grp]
    n_ct = len(ct_flat)
    n_add = 0 if add is None else 1
    tn = toks[0].shape[0]

    def body(*refs):
        pos = 0
        aux_v = [r[...] for r in refs[pos:pos + n_aux]]; pos += n_aux
        tok_v = [r[...] for r in refs[pos:pos + n_tok]]; pos += n_tok
        const_v = [r[...] for r in refs[pos:pos + n_c]]; pos += n_c
        ct_refs = refs[pos:pos + n_ct]; pos += n_ct
        add_refs = refs[pos:pos + n_add]; pos += n_add
        dtok_refs = refs[pos:pos + n_tok]; pos += n_tok
        dconst_refs = refs[pos:pos + n_c]
        ct_v, q = [], 0
        for grp in ct_groups:
            s = ct_refs[q][...]
            for r in ct_refs[q + 1:q + len(grp)]:
                s = s + r[...]
            q += len(grp)
            ct_v.append(s)
        _, vjp = jax.vjp(lambda *tc: fn(*aux_v, *tc), *tok_v, *const_v)
        grads = vjp(tuple(ct_v))
        for j, r in enumerate(dtok_refs):
            gj = grads[j]
            if j == 0 and n_add:
                gj = gj + add_refs[0][...]
            r[...] = gj

        @pl.when(pl.program_id(0) == 0)
        def _():
            for r in dconst_refs:
                r[...] = jnp.zeros(r.shape, F32)

        for j, r in enumerate(dconst_refs):
            r[...] += grads[n_tok + j]

    ins = list(aux) + list(toks) + list(consts) + ct_flat + ([add] if n_add else [])
    in_specs = ([_blk_spec(a, tb) for a in aux] + [_blk_spec(a, tb) for a in toks] + [_full_spec(c) for c in consts]
                + [_blk_spec(a, tb) for a in ct_flat] + ([_blk_spec(add, tb)] if n_add else []))
    out_shape = [jax.ShapeDtypeStruct(a.shape, F32) for a in toks] + [jax.ShapeDtypeStruct(c.shape, F32) for c in consts]
    out_specs = [_blk_spec(a, tb) for a in toks] + [_full_spec(c) for c in consts]
    return pl.pallas_call(body, name=name, grid=(tn // tb,), in_specs=in_specs, out_specs=out_specs,
                          out_shape=out_shape, compiler_params=_cparams(1))(*ins)


def _mm(name, a, b, ta=False, tb=False, add=None):
    if ta:
        kd, m = a.shape
    else:
        m, kd = a.shape
    if tb:
        n, kb = b.shape
    else:
        kb, n = b.shape
    assert kd == kb, (a.shape, b.shape)
    tm, tn, tk = _tile(m, 1024), _tile(n, 512), _tile(kd, 512)
    nk = kd // tk
    has_add = add is not None
    dims = (((0 if ta else 1,), (1 if tb else 0,)), ((), ()))

    def body(*refs):
        a_ref, b_ref = refs[0], refs[1]
        o_ref, acc_ref = refs[-2], refs[-1]
        k = pl.program_id(2)

        @pl.when(k == 0)
        def _():
            acc_ref[...] = refs[2][...] if has_add else jnp.zeros(acc_ref.shape, F32)

        acc_ref[...] += lax.dot_general(a_ref[...].astype(BF16), b_ref[...].astype(BF16), dims,
                                        preferred_element_type=F32)

        @pl.when(k == nk - 1)
        def _():
            o_ref[...] = acc_ref[...]

    a_spec = pl.BlockSpec((tk, tm), lambda i, j, k: (k, i)) if ta else pl.BlockSpec((tm, tk), lambda i, j, k: (i, k))
    b_spec = pl.BlockSpec((tn, tk), lambda i, j, k: (j, k)) if tb else pl.BlockSpec((tk, tn), lambda i, j, k: (k, j))
    o_spec = pl.BlockSpec((tm, tn), lambda i, j, k: (i, j))
    ins = [a, b] + ([add] if has_add else [])
    in_specs = [a_spec, b_spec] + ([o_spec] if has_add else [])
    return pl.pallas_call(body, name=name, grid=(m // tm, n // tn, nk), in_specs=in_specs, out_specs=o_spec,
                          out_shape=jax.ShapeDtypeStruct((m, n), F32),
                          scratch_shapes=[pltpu.VMEM((tm, tn), F32)], compiler_params=_cparams(3))(*ins)


def _prev8_spec(a, tb, rev_nb=None):
    r = tb // 8
    if rev_nb is None:
        return pl.BlockSpec((8, a.shape[1]), lambda i: (jnp.maximum(i * r - 1, 0), 0))
    return pl.BlockSpec((8, a.shape[1]), lambda i: (jnp.maximum((rev_nb - 1 - i) * r - 1, 0), 0))


def _pre_a_fwd(h1, p, consts, tb=TOK_BLOCK):
    tn = h1.shape[0]

    def body(h1_ref, h1h_ref, p_ref, ph_ref, *rest):
        c_refs, o_refs = rest[:len(consts)], rest[len(consts):]
        first = pl.program_id(0) == 0
        h1p = jnp.where(first, 0.0, h1h_ref[7:8, :])
        pp = jnp.where(first, 0.0, ph_ref[7:8, :])
        outs = _pre_a_fn(h1_ref[...], h1p, p_ref[...], pp, *[c[...] for c in c_refs])
        for r, o in zip(o_refs, outs):
            r[...] = o

    out_shape = [jax.ShapeDtypeStruct((tn, RWKV_WIDTH), F32) for _ in range(6)]
    return pl.pallas_call(
        body, name="rwkv_pre_a_fwd", grid=(tn // tb,),
        in_specs=[_blk_spec(h1, tb), _prev8_spec(h1, tb), _blk_spec(p, tb), _prev8_spec(p, tb)]
        + [_full_spec(c) for c in consts],
        out_specs=[_blk_spec(o, tb) for o in out_shape], out_shape=out_shape,
        compiler_params=_cparams(1))(h1, h1, p, p, *consts)


def _pre_a_bwd(h1, p, consts, cts, tb=TOK_BLOCK):
    tn = h1.shape[0]
    nb = tn // tb
    n_c = len(consts)
    ct_groups = [c if isinstance(c, (tuple, list)) else (c,) for c in cts]
    ct_flat = [a for grp in ct_groups for a in grp]
    n_ct = len(ct_flat)

    def body(*refs):
        h1_ref, h1h_ref, p_ref, ph_ref = refs[:4]
        c_refs = refs[4:4 + n_c]
        ct_refs = refs[4 + n_c:4 + n_c + n_ct]
        dh1_ref, dp_ref = refs[4 + n_c + n_ct:6 + n_c + n_ct]
        dc_refs = refs[6 + n_c + n_ct:6 + 2 * n_c + n_ct]
        ch_ref, cp_ref = refs[-2], refs[-1]
        i = pl.program_id(0)
        first_block = i == nb - 1
        h1p = jnp.where(first_block, 0.0, h1h_ref[7:8, :])
        pp = jnp.where(first_block, 0.0, ph_ref[7:8, :])
        ct_v, q = [], 0
        for grp in ct_groups:
            s = ct_refs[q][...]
            for r in ct_refs[q + 1:q + len(grp)]:
                s = s + r[...]
            q += len(grp)
            ct_v.append(s)
        _, vjp = jax.vjp(_pre_a_fn, h1_ref[...], h1p, p_ref[...], pp, *[c[...] for c in c_refs])
        grads = vjp(tuple(ct_v))

        @pl.when(i == 0)
        def _():
            ch_ref[...] = jnp.zeros(ch_ref.shape, F32)
            cp_ref[...] = jnp.zeros(cp_ref.shape, F32)
            for r in dc_refs:
                r[...] = jnp.zeros(r.shape, F32)

        rowh = lax.broadcasted_iota(jnp.int32, (tb, h1.shape[1]), 0)
        rowp = lax.broadcasted_iota(jnp.int32, (tb, p.shape[1]), 0)
        dh1_ref[...] = grads[0] + jnp.where(rowh == tb - 1, jnp.broadcast_to(ch_ref[0:1, :], rowh.shape), 0.0)
        dp_ref[...] = grads[2] + jnp.where(rowp == tb - 1, jnp.broadcast_to(cp_ref[0:1, :], rowp.shape), 0.0)
        ch_ref[0:1, :] = grads[1]
        cp_ref[0:1, :] = grads[3]
        for j, r in enumerate(dc_refs):
            r[...] += grads[4 + j]

    ins = [h1, h1, p, p] + list(consts) + ct_flat
    in_specs = ([_blk_spec(h1, tb, nb), _prev8_spec(h1, tb, nb), _blk_spec(p, tb, nb), _prev8_spec(p, tb, nb)]
                + [_full_spec(c) for c in consts] + [_blk_spec(a, tb, nb) for a in ct_flat])
    out_shape = ([jax.ShapeDtypeStruct(h1.shape, F32), jax.ShapeDtypeStruct(p.shape, F32)]
                 + [jax.ShapeDtypeStruct(c.shape, F32) for c in consts])
    out_specs = [_blk_spec(h1, tb, nb), _blk_spec(p, tb, nb)] + [_full_spec(c) for c in consts]
    return pl.pallas_call(body, name="rwkv_pre_a_bwd", grid=(nb,), in_specs=in_specs, out_specs=out_specs,
                          out_shape=out_shape,
                          scratch_shapes=[pltpu.VMEM((8, h1.shape[1]), F32), pltpu.VMEM((8, p.shape[1]), F32)],
                          compiler_params=_cparams(1))(*ins)


def _scan_state_step(s_ref, w_ref, k_ref, a_ref, b_ref, vt_ref, t, sel):
    new = []
    for h in range(RWKV_HEADS):
        S = s_ref[h]
        hs = pl.ds(h, 1)
        vcol = jnp.sum(jnp.where(sel, vt_ref[h * 64:(h + 1) * 64, :], 0.0), axis=1, keepdims=True)
        sa = jnp.sum(S * a_ref[t, hs, :], axis=1, keepdims=True)
        S = S * w_ref[t, hs, :] + sa * b_ref[t, hs, :] + vcol * k_ref[t, hs, :]
        s_ref[h] = S
        new.append(S)
    return new


def _scan_fwd(r, w, k, a, b, vt):
    tn = r.shape[0]
    nc = tn // SCAN_CHUNK
    nsub = SCAN_CHUNK // SCAN_CKPT
    H, Dh = RWKV_HEADS, RWKV_HEAD_DIM

    def body(r_ref, w_ref, k_ref, a_ref, b_ref, vt_ref, yt_ref, ck_ref, s_ref):
        @pl.when(pl.program_id(0) == 0)
        def _():
            s_ref[...] = jnp.zeros(s_ref.shape, F32)

        yt_ref[...] = jnp.zeros(yt_ref.shape, F32)
        lane = lax.broadcasted_iota(jnp.int32, (Dh, SCAN_CHUNK), 1)

        def sub_body(sub, carry):
            ck_ref[sub] = s_ref[...]

            def step(tt, carry2):
                t = sub * SCAN_CKPT + tt
                sel = lane == t
                new = _scan_state_step(s_ref, w_ref, k_ref, a_ref, b_ref, vt_ref, t, sel)
                for h in range(H):
                    ycol = jnp.sum(new[h] * r_ref[t, pl.ds(h, 1), :], axis=1, keepdims=True)
                    rows = slice(h * Dh, (h + 1) * Dh)
                    yt_ref[rows, :] = jnp.where(sel, ycol, yt_ref[rows, :])
                return carry2

            return lax.fori_loop(0, SCAN_CKPT, step, carry)

        lax.fori_loop(0, nsub, sub_body, 0)

    hl = pl.BlockSpec((SCAN_CHUNK, H, Dh), lambda c: (c, 0, 0))
    tl = pl.BlockSpec((H * Dh, SCAN_CHUNK), lambda c: (0, c))
    return pl.pallas_call(
        body, name="rwkv_scan_fwd", grid=(nc,), in_specs=[hl, hl, hl, hl, hl, tl],
        out_specs=[tl, pl.BlockSpec((nsub, H, Dh, Dh), lambda c: (c, 0, 0, 0))],
        out_shape=[jax.ShapeDtypeStruct((H * Dh, tn), F32),
                   jax.ShapeDtypeStruct((nc * nsub, H, Dh, Dh), F32)],
        scratch_shapes=[pltpu.VMEM((H, Dh, Dh), F32)], compiler_params=_cparams(1))(r, w, k, a, b, vt)


def _scan_bwd(r, w, k, a, b, vt, dyt, ck):
    tn = r.shape[0]
    nc = tn // SCAN_CHUNK
    nsub = SCAN_CHUNK // SCAN_CKPT
    H, Dh = RWKV_HEADS, RWKV_HEAD_DIM

    def body(r_ref, w_ref, k_ref, a_ref, b_ref, vt_ref, dyt_ref, ck_ref,
             dr_ref, dw_ref, dk_ref, da_ref, db_ref, dvt_ref, s_ref, g_ref, sp_ref):
        @pl.when(pl.program_id(0) == 0)
        def _():
            g_ref[...] = jnp.zeros(g_ref.shape, F32)

        dvt_ref[...] = jnp.zeros(dvt_ref.shape, F32)
        lane = lax.broadcasted_iota(jnp.int32, (Dh, SCAN_CHUNK), 1)

        def sub_body(ss, carry):
            sub = nsub - 1 - ss
            s_ref[...] = ck_ref[sub]

            def fstep(tt, c2):
                t = sub * SCAN_CKPT + tt
                sp_ref[tt] = s_ref[...]
                _scan_state_step(s_ref, w_ref, k_ref, a_ref, b_ref, vt_ref, t, lane == t)
                return c2

            lax.fori_loop(0, SCAN_CKPT, fstep, 0)

            def bstep(uu, c2):
                tt = SCAN_CKPT - 1 - uu
                t = sub * SCAN_CKPT + tt
                sel = lane == t
                for h in range(H):
                    hs = pl.ds(h, 1)
                    rows = slice(h * Dh, (h + 1) * Dh)
                    P = sp_ref[tt, h]
                    G = g_ref[h]
                    wr, kr, ar, br, rr = w_ref[t, hs, :], k_ref[t, hs, :], a_ref[t, hs, :], b_ref[t, hs, :], r_ref[t, hs, :]
                    vcol = jnp.sum(jnp.where(sel, vt_ref[rows, :], 0.0), axis=1, keepdims=True)
                    dycol = jnp.sum(jnp.where(sel, dyt_ref[rows, :], 0.0), axis=1, keepdims=True)
                    sa = jnp.sum(P * ar, axis=1, keepdims=True)
                    Sn = P * wr + sa * br + vcol * kr
                    G = G + dycol * rr
                    dr_ref[t, hs, :] = jnp.sum(Sn * dycol, axis=0, keepdims=True)
                    dvcol = jnp.sum(G * kr, axis=1, keepdims=True)
                    dvt_ref[rows, :] = jnp.where(sel, dvcol, dvt_ref[rows, :])
                    dk_ref[t, hs, :] = jnp.sum(G * vcol, axis=0, keepdims=True)
                    gb = jnp.sum(G * br, axis=1, keepdims=True)
                    dw_ref[t, hs, :] = jnp.sum(P * G, axis=0, keepdims=True)
                    da_ref[t, hs, :] = jnp.sum(P * gb, axis=0, keepdims=True)
                    db_ref[t, hs, :] = jnp.sum(G * sa, axis=0, keepdims=True)
                    g_ref[h] = G * wr + gb * ar
                return c2

            lax.fori_loop(0, SCAN_CKPT, bstep, 0)
            return carry

        lax.fori_loop(0, nsub, sub_body, 0)

    hl = pl.BlockSpec((SCAN_CHUNK, H, Dh), lambda c: (nc - 1 - c, 0, 0))
    tl = pl.BlockSpec((H * Dh, SCAN_CHUNK), lambda c: (0, nc - 1 - c))
    hshape = jax.ShapeDtypeStruct((tn, H, Dh), F32)
    return pl.pallas_call(
        body, name="rwkv_scan_bwd", grid=(nc,),
        in_specs=[hl, hl, hl, hl, hl, tl, tl, pl.BlockSpec((nsub, H, Dh, Dh), lambda c: (nc - 1 - c, 0, 0, 0))],
        out_specs=[hl, hl, hl, hl, hl, tl],
        out_shape=[hshape, hshape, hshape, hshape, hshape, jax.ShapeDtypeStruct((H * Dh, tn), F32)],
        scratch_shapes=[pltpu.VMEM((H, Dh, Dh), F32), pltpu.VMEM((H, Dh, Dh), F32),
                        pltpu.VMEM((SCAN_CKPT, H, Dh, Dh), F32)],
        compiler_params=_cparams(1))(r, w, k, a, b, vt, dyt, ck)


def _decay_mask(lg, i, j, blk):
    rows = lax.broadcasted_iota(jnp.int32, (blk, blk), 0)
    cols = lax.broadcasted_iota(jnp.int32, (blk, blk), 1)
    dd = (rows - cols + (i - j) * blk).astype(F32)
    return jnp.where(dd >= 0.0, jnp.exp(lg * jnp.maximum(dd, 0.0)), 0.0)


_NT = (((1,), (1,)), ((), ()))
_TN = (((0,), (0,)), ((), ()))


def _ret_attn_fwd(lg, q, k, v, blk=ATT_BLOCK):
    tn = q.shape[0]
    Dh = RET_HEAD_DIM

    def body(lg_ref, q_ref, k_ref, v_ref, o_ref):
        i = pl.program_id(1)
        lgv = lg_ref[0][:, 0:1]
        qb = q_ref[...].astype(BF16)

        def jb(j, acc):
            ks = pl.ds(pl.multiple_of(j * blk, blk), blk)
            s = lax.dot_general(qb, k_ref[ks, :].astype(BF16), _NT, preferred_element_type=F32)
            s = s * _decay_mask(lgv, i, j, blk)
            return acc + jnp.dot(s.astype(BF16), v_ref[ks, :].astype(BF16), preferred_element_type=F32)

        o_ref[...] = lax.fori_loop(0, i + 1, jb, jnp.zeros((blk, Dh), F32))

    full = pl.BlockSpec((tn, Dh), lambda h, i: (0, h))
    qs = pl.BlockSpec((blk, Dh), lambda h, i: (i, h))
    return pl.pallas_call(
        body, name="ret_attn_fwd", grid=(RET_HEADS, tn // blk),
        in_specs=[pl.BlockSpec((1, 1, 128), lambda h, i: (h, 0, 0)), qs, full, full],
        out_specs=qs, out_shape=jax.ShapeDtypeStruct(q.shape, F32), compiler_params=_cparams(2))(lg, q, k, v)


def _ret_attn_bwd(lg, q, k, v, do, blk=ATT_BLOCK):
    tn = q.shape[0]
    nb = tn // blk
    Dh = RET_HEAD_DIM

    def body(lg_ref, q_ref, k_ref, v_ref, do_ref, dq_ref, dk_ref, dv_ref):
        lgv = lg_ref[0][:, 0:1]
        dk_ref[...] = jnp.zeros(dk_ref.shape, F32)
        dv_ref[...] = jnp.zeros(dv_ref.shape, F32)

        def ib(i, carry):
            qs = pl.ds(pl.multiple_of(i * blk, blk), blk)
            qb = q_ref[qs, :].astype(BF16)
            dob = do_ref[qs, :].astype(BF16)

            def jb(j, dq):
                ks = pl.ds(pl.multiple_of(j * blk, blk), blk)
                kb = k_ref[ks, :].astype(BF16)
                vb = v_ref[ks, :].astype(BF16)
                dm = _decay_mask(lgv, i, j, blk)
                s = lax.dot_general(qb, kb, _NT, preferred_element_type=F32) * dm
                ds = lax.dot_general(dob, vb, _NT, preferred_element_type=F32) * dm
                sb, dsb = s.astype(BF16), ds.astype(BF16)
                dv_ref[ks, :] += lax.dot_general(sb, dob, _TN, preferred_element_type=F32)
                dk_ref[ks, :] += lax.dot_general(dsb, qb, _TN, preferred_element_type=F32)
                return dq + jnp.dot(dsb, kb, preferred_element_type=F32)

            dq_ref[qs, :] = lax.fori_loop(0, i + 1, jb, jnp.zeros((blk, Dh), F32))
            return carry

        lax.fori_loop(0, nb, ib, 0)

    full = pl.BlockSpec((tn, Dh), lambda h: (0, h))
    sh = jax.ShapeDtypeStruct(q.shape, F32)
    return pl.pallas_call(
        body, name="ret_attn_bwd", grid=(RET_HEADS,),
        in_specs=[pl.BlockSpec((1, 1, 128), lambda h: (h, 0, 0)), full, full, full, full],
        out_specs=[full, full, full], out_shape=[sh, sh, sh], compiler_params=_cparams(1))(lg, q, k, v, do)


def _next8_spec(a, tb):
    r = tb // 8
    last = a.shape[0] // 8 - 1
    return pl.BlockSpec((8, a.shape[1]), lambda i: (jnp.minimum((i + 1) * r, last), 0))


def _conv_taps(g_ext, cw_ref, cb_ref):
    return (cw_ref[2:3, :] * g_ext + cw_ref[1:2, :] * pltpu.roll(g_ext, 1, 0)
            + cw_ref[0:1, :] * pltpu.roll(g_ext, 2, 0) + cb_ref[...])


def _glu_fwd(gate, up, cw, cb, tb=TOK_BLOCK):
    tn = gate.shape[0]

    def body(g_ref, gh_ref, u_ref, cw_ref, cb_ref, o_ref):
        halo = jnp.where(pl.program_id(0) == 0, 0.0, gh_ref[...])
        g_ext = jnp.concatenate([halo, g_ref[...]], axis=0)
        gc = _conv_taps(g_ext, cw_ref, cb_ref)[8:, :]
        o_ref[...] = gc * _sigmoid(gc) * u_ref[...]

    return pl.pallas_call(
        body, name="glu_fwd", grid=(tn // tb,),
        in_specs=[_blk_spec(gate, tb), _prev8_spec(gate, tb), _blk_spec(up, tb), _full_spec(cw), _full_spec(cb)],
        out_specs=_blk_spec(gate, tb), out_shape=jax.ShapeDtypeStruct(gate.shape, F32),
        compiler_params=_cparams(1))(gate, gate, up, cw, cb)


def _glu_bwd(gate, up, dact, cw, cb, tb=TOK_BLOCK):
    tn = gate.shape[0]
    nb = tn // tb

    def body(g_ref, gp_ref, gn_ref, u_ref, un_ref, d_ref, dn_ref, cw_ref, cb_ref, dg_ref, du_ref, dcw_ref, dcb_ref):
        i = pl.program_id(0)
        gprev = jnp.where(i == 0, 0.0, gp_ref[...])
        dnext = jnp.where(i == nb - 1, 0.0, dn_ref[...])
        g_ext = jnp.concatenate([gprev, g_ref[...], gn_ref[...]], axis=0)
        gc = _conv_taps(g_ext, cw_ref, cb_ref)[8:, :]
        u_e = jnp.concatenate([u_ref[...], un_ref[...]], axis=0)
        d_e = jnp.concatenate([d_ref[...], dnext], axis=0)
        s = _sigmoid(gc)
        dgc = d_e * u_e * (s * (1.0 + gc * (1.0 - s)))
        du_ref[...] = d_ref[...] * (gc * s)[:tb, :]
        n_e = tb + 8
        dg_ref[...] = (cw_ref[2:3, :] * dgc + cw_ref[1:2, :] * pltpu.roll(dgc, n_e - 1, 0)
                       + cw_ref[0:1, :] * pltpu.roll(dgc, n_e - 2, 0))[:tb, :]

        @pl.when(i == 0)
        def _():
            dcw_ref[...] = jnp.zeros(dcw_ref.shape, F32)
            dcb_ref[...] = jnp.zeros(dcb_ref.shape, F32)

        dgc_b = dgc[:tb, :]
        g0 = g_ext[8:8 + tb, :]
        g1 = pltpu.roll(g_ext, 1, 0)[8:8 + tb, :]
        g2 = pltpu.roll(g_ext, 2, 0)[8:8 + tb, :]
        dcw_ref[2:3, :] += jnp.sum(dgc_b * g0, axis=0, keepdims=True)
        dcw_ref[1:2, :] += jnp.sum(dgc_b * g1, axis=0, keepdims=True)
        dcw_ref[0:1, :] += jnp.sum(dgc_b * g2, axis=0, keepdims=True)
        dcb_ref[...] += jnp.sum(dgc_b, axis=0, keepdims=True)

    sh = jax.ShapeDtypeStruct(gate.shape, F32)
    return pl.pallas_call(
        body, name="glu_bwd", grid=(nb,),
        in_specs=[_blk_spec(gate, tb), _prev8_spec(gate, tb), _next8_spec(gate, tb), _blk_spec(up, tb),
                  _next8_spec(up, tb), _blk_spec(dact, tb), _next8_spec(dact, tb), _full_spec(cw), _full_spec(cb)],
        out_specs=[_blk_spec(gate, tb), _blk_spec(gate, tb), _full_spec(cw), _full_spec(cb)],
        out_shape=[sh, sh, jax.ShapeDtypeStruct(cw.shape, F32), jax.ShapeDtypeStruct(cb.shape, F32)],
        compiler_params=_cparams(1))(gate, gate, gate, up, up, dact, dact, cw, cb)


def _final_loss(x2, tgt, g, tb=TOK_BLOCK):
    tn, dm = x2.shape

    def body(x_ref, t_ref, g_ref, l_ref, dx_ref, dg_ref):
        y, vjp = jax.vjp(_rms_fn, x_ref[...], g_ref[...])
        err = y - t_ref[...]
        dx, dg = vjp(err * (1.0 / dm))

        @pl.when(pl.program_id(0) == 0)
        def _():
            l_ref[...] = jnp.zeros(l_ref.shape, F32)
            dg_ref[...] = jnp.zeros(dg_ref.shape, F32)

        part = 0.5 * jnp.sum(jnp.mean(err * err, axis=-1, keepdims=True), axis=0, keepdims=True)
        l_ref[...] += jnp.broadcast_to(part, l_ref.shape)
        dx_ref[...] = dx
        dg_ref[...] += dg

    return pl.pallas_call(
        body, name="final_loss", grid=(tn // tb,),
        in_specs=[_blk_spec(x2, tb), _blk_spec(tgt, tb), _full_spec(g)],
        out_specs=[pl.BlockSpec((8, 128), lambda i: (0, 0)), _blk_spec(x2, tb), _full_spec(g)],
        out_shape=[jax.ShapeDtypeStruct((8, 128), F32), jax.ShapeDtypeStruct(x2.shape, F32),
                   jax.ShapeDtypeStruct(g.shape, F32)],
        compiler_params=_cparams(1))(x2, tgt, g)


def _pad_cols(w, n):
    return jnp.pad(w, ((0, 0), (0, n - w.shape[1])))


def _pad_rows(w, n):
    return jnp.pad(w, ((0, n - w.shape[0]), (0, 0)))


def _to_heads(z):
    return z.reshape(z.shape[0], RWKV_HEADS, RWKV_HEAD_DIM)


def _to_flat(z):
    return z.reshape(z.shape[0], RWKV_WIDTH)


def _local_step(x, tgt, W):
    tn = x.shape[0]
    Wd = RWKV_WIDTH
    row = lambda z: z.reshape(1, -1)
    g_mix, g_ffn, g_fin = row(W['norm_mix_g']), row(W['norm_ffn_g']), row(W['norm_final_g'])

    (h1,) = _tok_fwd("norm_mix_fwd", lambda a, g: (_rms_fn(a, g),), [x], [g_mix], [(D_MODEL,)])
    proj = _mm("proj_fwd", h1, W['w_in'])
    p_rkv = proj[:, :3 * Wd]
    pre_consts = [row(W['rwkv_mu_w']), row(W['rwkv_mu_a']), row(W['rwkv_mu_g']), row(W['rwkv_mu_r']),
                  row(W['rwkv_mu_k']), row(W['rwkv_mu_v']), row(W['rwkv_w0']),
                  _pad_cols(W['rwkv_w1'], LORA_PAD), _pad_rows(W['rwkv_w2'], LORA_PAD), row(W['rwkv_a0']),
                  _pad_cols(W['rwkv_a1'], LORA_PAD), _pad_rows(W['rwkv_a2'], LORA_PAD),
                  W['rwkv_g1'], W['rwkv_g2']]
    r, k0, v, decay, a, g = _pre_a_fwd(h1, p_rkv, pre_consts)
    k_k, k_a = _to_heads(row(W['rwkv_k_k']))[0], _to_heads(row(W['rwkv_k_a']))[0]
    k0h, ah = _to_heads(k0), _to_heads(a)
    nkk, kh, bh = _tok_fwd("rwkv_pre_b_fwd", _pre_b_fn, [k0h, ah], [k_k, k_a], [(RWKV_HEADS, RWKV_HEAD_DIM)] * 3)
    rh, wh, vh, gh = _to_heads(r), _to_heads(decay), _to_heads(v), _to_heads(g)
    vt = v.T
    yt, ck = _scan_fwd(rh, wh, kh, nkk, bh, vt)
    yh = _to_heads(yt.T)
    post_consts = [_to_heads(row(W['rwkv_lnx_w']))[0], _to_heads(row(W['rwkv_lnx_b']))[0], W['rwkv_r_k']]
    (y_rwkv_h,) = _tok_fwd("rwkv_post_fwd", _rwkv_post_fn, [yh, rh, kh, vh, gh], post_consts,
                           [(RWKV_HEADS, RWKV_HEAD_DIM)])
    y_rwkv = _to_flat(y_rwkv_h)

    pos = jnp.arange(tn, dtype=F32)
    half = RET_HEAD_DIM // 2
    inv_freq = ROPE_BASE ** (-jnp.arange(half, dtype=F32) / half)
    ang = pos[:, None] * inv_freq[None, :]
    cos2 = jnp.concatenate([jnp.cos(ang), jnp.cos(ang)], axis=1)
    sin2 = jnp.concatenate([-jnp.sin(ang), jnp.sin(ang)], axis=1)
    lg = jnp.log(1.0 - 2.0 ** (-5.0 - jnp.arange(RET_HEADS, dtype=F32)))
    lg = jnp.broadcast_to(lg[:, None, None], (RET_HEADS, 1, 128))
    q_p, k_p = proj[:, 3 * Wd:4 * Wd], proj[:, 4 * Wd:5 * Wd]
    v_ret, g_ret = proj[:, 5 * Wd:6 * Wd], proj[:, 6 * Wd:7 * Wd]
    q_rot, k_rot = _tok_fwd("ret_rotary_fwd", _rotary_fn, [cos2, sin2, q_p, k_p], [], [(RET_WIDTH,)] * 2)
    y_ret_raw = _ret_attn_fwd(lg, q_rot, k_rot, v_ret)
    gn_w = row(W['ret_gn_w'])
    (y_ret,) = _tok_fwd("ret_post_fwd", _ret_post_fn, [y_ret_raw, g_ret], [gn_w], [(RET_WIDTH,)])

    ycat = jnp.concatenate([y_rwkv, y_ret], axis=1)
    x1 = _mm("out_proj_fwd", ycat, W['w_out'], add=x)
    (h2,) = _tok_fwd("norm_ffn_fwd", lambda a_, g_: (_rms_fn(a_, g_),), [x1], [g_ffn], [(D_MODEL,)])
    gate = _mm("ffn_gate_fwd", h2, W['ffn_w_gate'])
    up = _mm("ffn_up_fwd", h2, W['ffn_w_up'])
    cw = W['ffn_conv_w'].reshape(3, D_FF)
    cb = row(W['ffn_conv_b'])
    act = _glu_fwd(gate, up, cw, cb)
    x2 = _mm("ffn_down_fwd", act, W['ffn_w_down'], add=x1)
    loss8, dx2, dg_fin = _final_loss(x2, tgt, g_fin)

    G = {'norm_final_g': dg_fin}
    dact = _mm("ffn_down_dx", dx2, W['ffn_w_down'], tb=True)
    G['ffn_w_down'] = _mm("ffn_down_dw", act, dx2, ta=True)
    dgate, dup, dcw, dcb = _glu_bwd(gate, up, dact, cw, cb)
    G['ffn_conv_w'], G['ffn_conv_b'] = dcw, dcb
    dh2 = _mm("ffn_gate_dx", dgate, W['ffn_w_gate'], tb=True)
    dh2 = _mm("ffn_up_dx", dup, W['ffn_w_up'], tb=True, add=dh2)
    G['ffn_w_gate'] = _mm("ffn_gate_dw", h2, dgate, ta=True)
    G['ffn_w_up'] = _mm("ffn_up_dw", h2, dup, ta=True)
    dx1, G['norm_ffn_g'] = _tok_bwd("norm_ffn_bwd", lambda a_, g_: (_rms_fn(a_, g_),), [], [x1], [g_ffn], [dh2], add=dx2)
    dycat = _mm("out_proj_dx", dx1, W['w_out'], tb=True)
    G['w_out'] = _mm("out_proj_dw", ycat, dx1, ta=True)
    dy_rwkv, dy_ret = dycat[:, :Wd], dycat[:, Wd:]

    dyr_raw, dg_ret, G['ret_gn_w'] = _tok_bwd("ret_post_bwd", _ret_post_fn, [], [y_ret_raw, g_ret], [gn_w], [dy_ret])
    dq_rot, dk_rot, dv_ret = _ret_attn_bwd(lg, q_rot, k_rot, v_ret, dyr_raw)
    dq_p, dk_p = _tok_bwd("ret_rotary_bwd", _rotary_fn, [cos2, sin2], [q_p, k_p], [], [dq_rot, dk_rot])

    dyh, drh1, dkh1, dvh1, dgh, G['rwkv_lnx_w'], G['rwkv_lnx_b'], G['rwkv_r_k'] = _tok_bwd(
        "rwkv_post_bwd", _rwkv_post_fn, [], [yh, rh, kh, vh, gh], post_consts, [_to_heads(dy_rwkv)])
    dyt = _to_flat(dyh).T
    drh2, dwh, dkh2, dnkk, dbh, dvt = _scan_bwd(rh, wh, kh, nkk, bh, vt, dyt, ck)
    dk0h, dah, G['rwkv_k_k'], G['rwkv_k_a'] = _tok_bwd(
        "rwkv_pre_b_bwd", _pre_b_fn, [], [k0h, ah], [k_k, k_a], [dnkk, (dkh1, dkh2), dbh])
    pre_cts = [(_to_flat(drh1), _to_flat(drh2)), _to_flat(dk0h), (_to_flat(dvh1), dvt.T),
               _to_flat(dwh), _to_flat(dah), _to_flat(dgh)]
    pre_out = _pre_a_bwd(h1, p_rkv, pre_consts, pre_cts)
    dh1_a, dp_rkv = pre_out[0], pre_out[1]
    (G['rwkv_mu_w'], G['rwkv_mu_a'], G['rwkv_mu_g'], G['rwkv_mu_r'], G['rwkv_mu_k'], G['rwkv_mu_v'], G['rwkv_w0'],
     dw1, dw2, G['rwkv_a0'], da1, da2, G['rwkv_g1'], G['rwkv_g2']) = pre_out[2:]
    G['rwkv_w1'], G['rwkv_w2'] = dw1[:, :64], dw2[:64, :]
    G['rwkv_a1'], G['rwkv_a2'] = da1[:, :64], da2[:64, :]

    dproj = jnp.concatenate([dp_rkv, dq_p, dk_p, dv_ret, dg_ret], axis=1)
    dh1 = _mm("proj_dx", dproj, W['w_in'], tb=True, add=dh1_a)
    G['w_in'] = _mm("proj_dw", h1, dproj, ta=True)
    dx, G['norm_mix_g'] = _tok_bwd("norm_mix_bwd", lambda a_, g_: (_rms_fn(a_, g_),), [], [x], [g_mix], [dh1], add=dx1)
    return loss8[0, 0], dx, G


def _peer(k):
    x, y, c = lax.axis_index("x"), lax.axis_index("y"), lax.axis_index("c")
    px = 1 - x if k & 4 else x
    py = 1 - y if k & 2 else y
    pc = 1 - c if k & 1 else c
    return (px, py, pc), 4 * px + 2 * py + pc


def _my_index():
    return 4 * lax.axis_index("x") + 2 * lax.axis_index("y") + lax.axis_index("c")


def _all_gather(shard):
    rows, lanes = shard.shape

    def body(x_ref, out_ref, send_sems, recv_sems, local_sem):
        me = _my_index()
        mine = pltpu.make_async_copy(x_ref, out_ref.at[me], local_sem)
        mine.start()
        copies = []
        for k in range(1, N_DEV):
            peer, _ = _peer(k)
            cp = pltpu.make_async_remote_copy(src_ref=x_ref, dst_ref=out_ref.at[me], send_sem=send_sems.at[k - 1],
                                              recv_sem=recv_sems.at[k - 1], device_id=peer,
                                              device_id_type=pl.DeviceIdType.MESH)
            cp.start()
            copies.append(cp)
        for k in range(1, N_DEV):
            peer, plin = _peer(k)
            pltpu.make_async_remote_copy(src_ref=x_ref, dst_ref=out_ref.at[plin], send_sem=send_sems.at[k - 1],
                                         recv_sem=recv_sems.at[k - 1], device_id=peer,
                                         device_id_type=pl.DeviceIdType.MESH).wait_recv()
        for cp in copies:
            cp.wait_send()
        mine.wait()

    return pl.pallas_call(
        body, name="weights_all_gather",
        in_specs=[pl.BlockSpec(memory_space=pl.ANY)], out_specs=pl.BlockSpec(memory_space=pl.ANY),
        out_shape=jax.ShapeDtypeStruct((N_DEV, rows, lanes), shard.dtype),
        scratch_shapes=[pltpu.SemaphoreType.DMA((N_DEV - 1,)), pltpu.SemaphoreType.DMA((N_DEV - 1,)),
                        pltpu.SemaphoreType.DMA(())],
    )(shard)


def _all_to_all(blocks):
    n, rows, lanes = blocks.shape

    def body(x_ref, out_ref, send_sems, recv_sems, local_sem):
        me = _my_index()
        mine = pltpu.make_async_copy(x_ref.at[me], out_ref.at[me], local_sem)
        mine.start()
        copies = []
        for k in range(1, N_DEV):
            peer, plin = _peer(k)
            cp = pltpu.make_async_remote_copy(src_ref=x_ref.at[plin], dst_ref=out_ref.at[me],
                                              send_sem=send_sems.at[k - 1], recv_sem=recv_sems.at[k - 1],
                                              device_id=peer, device_id_type=pl.DeviceIdType.MESH)
            cp.start()
            copies.append(cp)
        for k in range(1, N_DEV):
            peer, plin = _peer(k)
            pltpu.make_async_remote_copy(src_ref=x_ref.at[me], dst_ref=out_ref.at[plin],
                                         send_sem=send_sems.at[k - 1], recv_sem=recv_sems.at[k - 1],
                                         device_id=peer, device_id_type=pl.DeviceIdType.MESH).wait_recv()
        for cp in copies:
            cp.wait_send()
        mine.wait()

    return pl.pallas_call(
        body, name="grads_all_to_all",
        in_specs=[pl.BlockSpec(memory_space=pl.ANY)], out_specs=pl.BlockSpec(memory_space=pl.ANY),
        out_shape=jax.ShapeDtypeStruct((n, rows, lanes), blocks.dtype),
        scratch_shapes=[pltpu.SemaphoreType.DMA((N_DEV - 1,)), pltpu.SemaphoreType.DMA((N_DEV - 1,)),
                        pltpu.SemaphoreType.DMA(())],
    )(blocks)


def _adamw(parts, w, m, v, tb=512):
    rows = w.shape[0]
    c1 = 1.0 - ADAM_B1 ** ADAM_STEP
    c2 = 1.0 - ADAM_B2 ** ADAM_STEP

    def body(p_ref, w_ref, m_ref, v_ref, g_ref, d_ref, nm_ref, nv_ref):
        g = p_ref[0]
        for d in range(1, N_DEV):
            g = g + p_ref[d]
        mn = ADAM_B1 * m_ref[...] + (1.0 - ADAM_B1) * g
        vn = ADAM_B2 * v_ref[...] + (1.0 - ADAM_B2) * (g * g)
        m_hat = mn / c1
        v_hat = vn / c2
        g_ref[...] = g
        d_ref[...] = -ADAM_LR * (m_hat / (jnp.sqrt(v_hat) + ADAM_EPS) + ADAM_WD * w_ref[...])
        nm_ref[...] = mn
        nv_ref[...] = vn

    spec = pl.BlockSpec((tb, 128), lambda i: (i, 0))
    sh = jax.ShapeDtypeStruct((rows, 128), F32)
    return pl.pallas_call(
        body, name="adamw", grid=(rows // tb,),
        in_specs=[pl.BlockSpec((N_DEV, tb, 128), lambda i: (0, i, 0)), spec, spec, spec],
        out_specs=[spec] * 4, out_shape=[sh] * 4, compiler_params=_cparams(1))(parts, w, m, v)


def _local_shape(name):
    gs, ax = SHARDED[name]
    ls = list(gs)
    ls[ax] //= N_DEV
    return tuple(ls)


def _seg(flat, seg):
    n = flat.shape[-1]
    pad = _round_up(n, seg) - n
    if pad:
        flat = jnp.pad(flat, [(0, 0)] * (flat.ndim - 1) + [(0, pad)])
    return flat


def _split3(w):
    hi = w.astype(BF16)
    r1 = w - hi.astype(F32)
    mid = r1.astype(BF16)
    lo = (r1 - mid.astype(F32)).astype(BF16)
    return hi, mid, lo


def _pack_weight_shards(shards):
    pieces = []
    for name in SHARDED_NAMES:
        flat = shards[name].reshape(-1)
        if name == 'ffn_conv_w':
            pieces += [_seg(p, BF16_SEG) for p in _split3(flat)]
        else:
            pieces.append(_seg(flat.astype(BF16), BF16_SEG))
    return jnp.concatenate(pieces).reshape(-1, 128)


def _unpack_gathered(gathered):
    flat = gathered.reshape(N_DEV, -1)
    out, off = {}, 0
    for name in SHARDED_NAMES:
        gs, ax = SHARDED[name]
        ls = _local_shape(name)
        n = int(np.prod(ls))
        nseg = _round_up(n, BF16_SEG)
        if name == 'ffn_conv_w':
            hi, mid, lo = (flat[:, off + j * nseg: off + j * nseg + n].astype(F32) for j in range(3))
            sh = (hi + mid) + lo
            off += 3 * nseg
        else:
            sh = flat[:, off:off + n]
            off += nseg
        sh = sh.reshape((N_DEV,) + ls)
        full = jnp.concatenate([sh[d] for d in range(N_DEV)], axis=ax)
        out[name] = full.reshape(gs[1:])
    return out


def _pack_local(sharded, repl):
    pieces = [_seg(sharded[n].reshape(-1), BF16_SEG) for n in SHARDED_NAMES]
    pieces += [_seg(repl[n].reshape(-1), F32_SEG) for n in REPL_NAMES]
    flat = jnp.concatenate(pieces)
    return _seg(flat, 512 * 128).reshape(-1, 128)


def _pack_grad_blocks(G):
    pieces = []
    for name in SHARDED_NAMES:
        gs, ax = SHARDED[name]
        g = G[name].reshape(gs)
        sh = jnp.stack(jnp.split(g, N_DEV, axis=ax)).reshape(N_DEV, -1)
        pieces.append(_seg(sh, BF16_SEG))
    for name in REPL_NAMES:
        flat = G[name].reshape(1, -1)
        pieces.append(jnp.broadcast_to(_seg(flat, F32_SEG), (N_DEV, _round_up(flat.shape[1], F32_SEG))))
    flat = _seg(jnp.concatenate(pieces, axis=1), 512 * 128)
    return flat.reshape(N_DEV, -1, 128)


def _unpack_local(packed, local_shapes):
    flat = packed.reshape(-1)
    out, off = {}, 0
    for name in SHARDED_NAMES:
        n = int(np.prod(local_shapes[name]))
        out[name] = flat[off:off + n].reshape(local_shapes[name])
        off += _round_up(n, BF16_SEG)
    for name in REPL_NAMES:
        n = int(np.prod(local_shapes[name]))
        out[name] = flat[off:off + n].reshape(local_shapes[name])
        off += _round_up(n, F32_SEG)
    return out


def kernel(x, *rest):
    nw = len(WEIGHT_NAMES)
    assert len(rest) == 3 * nw + 1
    weights = dict(zip(WEIGHT_NAMES, rest[:nw]))
    loss_target = rest[nw]
    moms = dict(zip(WEIGHT_NAMES, rest[nw + 1:2 * nw + 1]))
    vars_ = dict(zip(WEIGHT_NAMES, rest[2 * nw + 1:]))
    local_shapes = {n: weights[n].shape for n in WEIGHT_NAMES}

    gathered = _all_gather(_pack_weight_shards({n: weights[n] for n in SHARDED_NAMES}))
    W = _unpack_gathered(gathered)
    for n in REPL_NAMES:
        W[n] = weights[n][0] if n != 'norm_final_g' else weights[n]

    loss, dx, G = _local_step(x[0], loss_target[0], W)

    parts = _all_to_all(_pack_grad_blocks(G))
    pk = lambda d: _pack_local({n: d[n] for n in SHARDED_NAMES}, {n: d[n] for n in REPL_NAMES})
    grad_p, delta_p, m_p, v_p = _adamw(parts, pk(weights), pk(moms), pk(vars_))

    loss = lax.psum(loss, ("x", "y", "c"))
    outs = [loss, dx[None]]
    for packed in (grad_p, delta_p, m_p, v_p):
        d = _unpack_local(packed, local_shapes)
        outs += [d[n] for n in WEIGHT_NAMES]
    return tuple(outs)
```

```python
import functools
import math

import numpy as np
import jax
import jax.numpy as jnp
from jax import lax
from jax.experimental import pallas as pl
from jax.experimental.pallas import tpu as pltpu

F32 = jnp.float32
BF16 = jnp.bfloat16

N_DEV = 8
D_MODEL = 1024
RWKV_HEADS = 8
RWKV_HEAD_DIM = 64
RWKV_WIDTH = 512
RET_HEADS = 4
RET_HEAD_DIM = 128
RET_WIDTH = 512
LORA_PAD = 128
D_FF = 2816
NORM_EPS = 1e-6
RWKV_GN_EPS = 64e-5
RET_GN_EPS = 1e-5
ROPE_BASE = 10000.0
ADAM_LR, ADAM_B1, ADAM_B2, ADAM_EPS, ADAM_WD, ADAM_STEP = 0.001, 0.9, 0.999, 1e-08, 0.01, 10

VMEM_LIMIT = 56 * 1024 * 1024
TOK_BLOCK = 256
SCAN_CHUNK = 128
SCAN_CKPT = 32
ATT_BLOCK = 256
BF16_SEG = 2048
F32_SEG = 1024

WEIGHT_NAMES = ['norm_mix_g', 'w_in', 'rwkv_mu_r', 'rwkv_mu_k', 'rwkv_mu_v', 'rwkv_mu_w', 'rwkv_mu_a',
                'rwkv_mu_g', 'rwkv_w0', 'rwkv_w1', 'rwkv_w2', 'rwkv_a0', 'rwkv_a1', 'rwkv_a2', 'rwkv_g1',
                'rwkv_g2', 'rwkv_k_k', 'rwkv_k_a', 'rwkv_r_k', 'rwkv_lnx_w', 'rwkv_lnx_b', 'ret_gn_w',
                'w_out', 'norm_ffn_g', 'ffn_w_gate', 'ffn_w_up', 'ffn_conv_w', 'ffn_conv_b', 'ffn_w_down',
                'norm_final_g']
SHARDED = {
    'w_in': ((1, 1024, 3584), 2), 'rwkv_w1': ((1, 1024, 64), 1), 'rwkv_w2': ((1, 64, 512), 2),
    'rwkv_a1': ((1, 1024, 64), 1), 'rwkv_a2': ((1, 64, 512), 2), 'rwkv_g1': ((1, 1024, 128), 1),
    'rwkv_g2': ((1, 128, 512), 2), 'w_out': ((1, 1024, 1024), 1), 'ffn_w_gate': ((1, 1024, 2816), 2),
    'ffn_w_up': ((1, 1024, 2816), 2), 'ffn_conv_w': ((1, 3, 1, 2816), 3), 'ffn_w_down': ((1, 2816, 1024), 1),
}
REPL_NAMES = [n for n in WEIGHT_NAMES if n not in SHARDED]
BIG_NAMES = ['w_in', 'w_out', 'ffn_w_gate', 'ffn_w_up', 'ffn_w_down']
BIG_T = ('w_in', 'ffn_w_gate', 'ffn_w_up')
LATE_NAMES = ['w_out', 'ffn_w_gate', 'ffn_w_up', 'ffn_w_down']
SMALL_NAMES = [n for n in WEIGHT_NAMES if n in SHARDED and n not in BIG_NAMES]


def _cparams(n_grid):
    return pltpu.CompilerParams(dimension_semantics=("arbitrary",) * n_grid, vmem_limit_bytes=VMEM_LIMIT)


def _round_up(n, m):
    return (n + m - 1) // m * m


def _tile(n, cap):
    best = None
    for t in range(128, min(n, cap) + 1, 128):
        if n % t == 0:
            best = t
    assert best is not None, n
    return best


@jax.custom_vjp
def _bdot(x, w):
    return jnp.dot(x.astype(BF16), w.astype(BF16), preferred_element_type=F32)


def _bdot_fwd(x, w):
    return _bdot(x, w), (x, w)


def _bdot_bwd(res, g):
    x, w = res
    gb = g.astype(BF16)
    dx = lax.dot_general(gb, w.astype(BF16), (((1,), (1,)), ((), ())), preferred_element_type=F32)
    dw = lax.dot_general(x.astype(BF16), gb, (((0,), (0,)), ((), ())), preferred_element_type=F32)
    return dx, dw.astype(w.dtype)


_bdot.defvjp(_bdot_fwd, _bdot_bwd)


@jax.custom_vjp
def _shift_rows(x, prev):
    rolled = pltpu.roll(x, 1, 0)
    row = lax.broadcasted_iota(jnp.int32, x.shape, 0)
    return jnp.where(row == 0, jnp.broadcast_to(prev, x.shape), rolled)


def _shift_rows_fwd(x, prev):
    return _shift_rows(x, prev), None


def _shift_rows_bwd(_, g):
    n = g.shape[0]
    rolled = pltpu.roll(g, n - 1, 0)
    row = lax.broadcasted_iota(jnp.int32, g.shape, 0)
    return jnp.where(row == n - 1, 0.0, rolled), g[0:1, :]


_shift_rows.defvjp(_shift_rows_fwd, _shift_rows_bwd)


@jax.custom_vjp
def _swap_halves(x):
    return pltpu.roll(x, 64, 1)


_swap_halves.defvjp(lambda x: (_swap_halves(x), None), lambda _, g: (pltpu.roll(g, 64, 1),))


def _sigmoid(x):
    return 1.0 / (1.0 + jnp.exp(-x))


def _softplus(x):
    return jnp.maximum(x, 0.0) + jnp.log(1.0 + jnp.exp(-jnp.abs(x)))


def _rms_fn(x, g):
    return x * lax.rsqrt(jnp.mean(x * x, axis=-1, keepdims=True) + NORM_EPS) * g


def _pre_a_fn(h1, h1p, p, pp, mu_w, mu_a, mu_g, mu_r, mu_k, mu_v, w0, w1, w2, a0, a1, a2, g1, g2):
    W = RWKV_WIDTH
    h1s = _shift_rows(h1, h1p)
    ps = _shift_rows(p, pp)
    dx = h1s - h1
    xw = h1 + dx * mu_w
    xa = h1 + dx * mu_a
    xg = h1 + dx * mu_g
    dp = ps - p
    r = p[:, 0:W] + dp[:, 0:W] * mu_r
    k0 = p[:, W:2 * W] + dp[:, W:2 * W] * mu_k
    v = p[:, 2 * W:3 * W] + dp[:, 2 * W:3 * W] * mu_v
    wl = w0 + _bdot(jnp.tanh(_bdot(xw, w1)), w2)
    w_log = -_softplus(-wl) - 0.5
    decay = jnp.exp(-jnp.exp(w_log))
    a = _sigmoid(a0 + _bdot(_bdot(xa, a1), a2))
    g = _bdot(_sigmoid(_bdot(xg, g1)), g2)
    return r, k0, v, decay, a, g


def _pre_b_fn(k0, a, k_k, k_a):
    kkr = k0 * k_k
    nrm = jnp.sqrt(jnp.sum(kkr * kkr, axis=-1, keepdims=True))
    kk = kkr / jnp.maximum(nrm, 1e-12)
    k = k0 * (1.0 + (a - 1.0) * k_a)
    return -kk, k, kk * a


def _rwkv_post_fn(y, r, k, v, g, lnx_w, lnx_b, r_k):
    mu = jnp.mean(y, axis=-1, keepdims=True)
    yc = y - mu
    var = jnp.mean(yc * yc, axis=-1, keepdims=True)
    yn = yc * lax.rsqrt(var + RWKV_GN_EPS) * lnx_w + lnx_b
    bonus = jnp.sum(r * k * r_k, axis=-1, keepdims=True) * v
    return ((yn + bonus) * g,)


def _rotary_fn(cos2, sin2, q, k):
    qs, ks = [], []
    for h in range(RET_HEADS):
        sl = slice(h * RET_HEAD_DIM, (h + 1) * RET_HEAD_DIM)
        qh, kh = q[:, sl], k[:, sl]
        qs.append(qh * cos2 + _swap_halves(qh) * sin2)
        ks.append((kh * cos2 + _swap_halves(kh) * sin2) * (RET_HEAD_DIM ** -0.5))
    return jnp.concatenate(qs, axis=1), jnp.concatenate(ks, axis=1)


def _ret_post_fn(y, gp, gn_w):
    outs = []
    for h in range(RET_HEADS):
        sl = slice(h * RET_HEAD_DIM, (h + 1) * RET_HEAD_DIM)
        yh = y[:, sl]
        mu = jnp.mean(yh, axis=-1, keepdims=True)
        yc = yh - mu
        var = jnp.mean(yc * yc, axis=-1, keepdims=True)
        outs.append(yc * lax.rsqrt(var + RET_GN_EPS) * gn_w[:, sl])
    yn = jnp.concatenate(outs, axis=1)
    return (gp * _sigmoid(gp) * yn,)


def _blk_spec(a, tb, rev_nb=None):
    nd = a.ndim
    if rev_nb is None:
        return pl.BlockSpec((tb,) + a.shape[1:], lambda i: (i,) + (0,) * (nd - 1))
    return pl.BlockSpec((tb,) + a.shape[1:], lambda i: (rev_nb - 1 - i,) + (0,) * (nd - 1))


def _full_spec(a):
    nd = a.ndim
    return pl.BlockSpec(a.shape, lambda i: (0,) * nd)


def _tok_fwd(name, fn, toks, consts, out_tails, tb=TOK_BLOCK):
    n_in = len(toks) + len(consts)
    tn = toks[0].shape[0]

    def body(*refs):
        outs = fn(*[r[...] for r in refs[:n_in]])
        for r, o in zip(refs[n_in:], outs):
            r[...] = o

    out_shape = [jax.ShapeDtypeStruct((tn,) + tuple(s), F32) for s in out_tails]
    return pl.pallas_call(
        body, name=name, grid=(tn // tb,),
        in_specs=[_blk_spec(a, tb) for a in toks] + [_full_spec(c) for c in consts],
        out_specs=[_blk_spec(o, tb) for o in out_shape], out_shape=out_shape,
        compiler_params=_cparams(1))(*toks, *consts)


def _tok_bwd(name, fn, aux, toks, consts, cts, add=None, tb=TOK_BLOCK):
    n_aux, n_tok, n_c = len(aux), len(toks), len(consts)
    ct_groups = [c if isinstance(c, (tuple, list)) else (c,) for c in cts]
    ct_flat = [a for grp in ct_groups for a in grp]
    n_ct = len(ct_flat)
    n_add = 0 if add is None else 1
    tn = toks[0].shape[0]

    def body(*refs):
        pos = 0
        aux_v = [r[...] for r in refs[pos:pos + n_aux]]; pos += n_aux
        tok_v = [r[...] for r in refs[pos:pos + n_tok]]; pos += n_tok
        const_v = [r[...] for r in refs[pos:pos + n_c]]; pos += n_c
        ct_refs = refs[pos:pos + n_ct]; pos += n_ct
        add_refs = refs[pos:pos + n_add]; pos += n_add
        dtok_refs = refs[pos:pos + n_tok]; pos += n_tok
        dconst_refs = refs[pos:pos + n_c]
        ct_v, q = [], 0
        for grp in ct_groups:
            s = ct_refs[q][...]
            for r in ct_refs[q + 1:q + len(grp)]:
                s = s + r[...]
            q += len(grp)
            ct_v.append(s)
        _, vjp = jax.vjp(lambda *tc: fn(*aux_v, *tc), *tok_v, *const_v)
        grads = vjp(tuple(ct_v))
        for j, r in enumerate(dtok_refs):
            gj = grads[j]
            if j == 0 and n_add:
                gj = gj + add_refs[0][...]
            r[...] = gj

        @pl.when(pl.program_id(0) == 0)
        def _():
            for r in dconst_refs:
                r[...] = jnp.zeros(r.shape, F32)

        for j, r in enumerate(dconst_refs):
            r[...] += grads[n_tok + j]

    ins = list(aux) + list(toks) + list(consts) + ct_flat + ([add] if n_add else [])
    in_specs = ([_blk_spec(a, tb) for a in aux] + [_blk_spec(a, tb) for a in toks] + [_full_spec(c) for c in consts]
                + [_blk_spec(a, tb) for a in ct_flat] + ([_blk_spec(add, tb)] if n_add else []))
    out_shape = [jax.ShapeDtypeStruct(a.shape, F32) for a in toks] + [jax.ShapeDtypeStruct(c.shape, F32) for c in consts]
    out_specs = [_blk_spec(a, tb) for a in toks] + [_full_spec(c) for c in consts]
    return pl.pallas_call(body, name=name, grid=(tn // tb,), in_specs=in_specs, out_specs=out_specs,
                          out_shape=out_shape, compiler_params=_cparams(1))(*ins)


def _mm(name, a, b, ta=False, tb=False, add=None):
    if ta:
        kd, m = a.shape
    else:
        m, kd = a.shape
    if tb:
        n, kb = b.shape
    else:
        kb, n = b.shape
    assert kd == kb, (a.shape, b.shape)
    tm, tn, tk = _tile(m, 1024), _tile(n, 512), _tile(kd, 512)
    nk = kd // tk
    has_add = add is not None
    dims = (((0 if ta else 1,), (1 if tb else 0,)), ((), ()))

    def body(*refs):
        a_ref, b_ref = refs[0], refs[1]
        o_ref, acc_ref = refs[-2], refs[-1]
        k = pl.program_id(2)

        @pl.when(k == 0)
        def _():
            acc_ref[...] = refs[2][...] if has_add else jnp.zeros(acc_ref.shape, F32)

        acc_ref[...] += lax.dot_general(a_ref[...].astype(BF16), b_ref[...].astype(BF16), dims,
                                        preferred_element_type=F32)

        @pl.when(k == nk - 1)
        def _():
            o_ref[...] = acc_ref[...]

    a_spec = pl.BlockSpec((tk, tm), lambda i, j, k: (k, i)) if ta else pl.BlockSpec((tm, tk), lambda i, j, k: (i, k))
    b_spec = pl.BlockSpec((tn, tk), lambda i, j, k: (j, k)) if tb else pl.BlockSpec((tk, tn), lambda i, j, k: (k, j))
    o_spec = pl.BlockSpec((tm, tn), lambda i, j, k: (i, j))
    ins = [a, b] + ([add] if has_add else [])
    in_specs = [a_spec, b_spec] + ([o_spec] if has_add else [])
    return pl.pallas_call(body, name=name, grid=(m // tm, n // tn, nk), in_specs=in_specs, out_specs=o_spec,
                          out_shape=jax.ShapeDtypeStruct((m, n), F32),
                          scratch_shapes=[pltpu.VMEM((tm, tn), F32)], compiler_params=_cparams(3))(*ins)


def _prev8_spec(a, tb, rev_nb=None):
    r = tb // 8
    if rev_nb is None:
        return pl.BlockSpec((8, a.shape[1]), lambda i: (jnp.maximum(i * r - 1, 0), 0))
    return pl.BlockSpec((8, a.shape[1]), lambda i: (jnp.maximum((rev_nb - 1 - i) * r - 1, 0), 0))


def _pre_a_fwd(h1, p, consts, tb=TOK_BLOCK):
    tn = h1.shape[0]

    def body(h1_ref, h1h_ref, p_ref, ph_ref, *rest):
        c_refs, o_refs = rest[:len(consts)], rest[len(consts):]
        first = pl.program_id(0) == 0
        h1p = jnp.where(first, 0.0, h1h_ref[7:8, :])
        pp = jnp.where(first, 0.0, ph_ref[7:8, :])
        outs = _pre_a_fn(h1_ref[...], h1p, p_ref[...], pp, *[c[...] for c in c_refs])
        for r, o in zip(o_refs, outs):
            r[...] = o

    out_shape = [jax.ShapeDtypeStruct((tn, RWKV_WIDTH), F32) for _ in range(6)]
    return pl.pallas_call(
        body, name="rwkv_pre_a_fwd", grid=(tn // tb,),
        in_specs=[_blk_spec(h1, tb), _prev8_spec(h1, tb), _blk_spec(p, tb), _prev8_spec(p, tb)]
        + [_full_spec(c) for c in consts],
        out_specs=[_blk_spec(o, tb) for o in out_shape], out_shape=out_shape,
        compiler_params=_cparams(1))(h1, h1, p, p, *consts)


def _pre_a_bwd(h1, p, consts, cts, tb=TOK_BLOCK):
    tn = h1.shape[0]
    nb = tn // tb
    n_c = len(consts)
    ct_groups = [c if isinstance(c, (tuple, list)) else (c,) for c in cts]
    ct_flat = [a for grp in ct_groups for a in grp]
    n_ct = len(ct_flat)

    def body(*refs):
        h1_ref, h1h_ref, p_ref, ph_ref = refs[:4]
        c_refs = refs[4:4 + n_c]
        ct_refs = refs[4 + n_c:4 + n_c + n_ct]
        dh1_ref, dp_ref = refs[4 + n_c + n_ct:6 + n_c + n_ct]
        dc_refs = refs[6 + n_c + n_ct:6 + 2 * n_c + n_ct]
        ch_ref, cp_ref = refs[-2], refs[-1]
        i = pl.program_id(0)
        first_block = i == nb - 1
        h1p = jnp.where(first_block, 0.0, h1h_ref[7:8, :])
        pp = jnp.where(first_block, 0.0, ph_ref[7:8, :])
        ct_v, q = [], 0
        for grp in ct_groups:
            s = ct_refs[q][...]
            for r in ct_refs[q + 1:q + len(grp)]:
                s = s + r[...]
            q += len(grp)
            ct_v.append(s)
        _, vjp = jax.vjp(_pre_a_fn, h1_ref[...], h1p, p_ref[...], pp, *[c[...] for c in c_refs])
        grads = vjp(tuple(ct_v))

        @pl.when(i == 0)
        def _():
            ch_ref[...] = jnp.zeros(ch_ref.shape, F32)
            cp_ref[...] = jnp.zeros(cp_ref.shape, F32)
            for r in dc_refs:
                r[...] = jnp.zeros(r.shape, F32)

        rowh = lax.broadcasted_iota(jnp.int32, (tb, h1.shape[1]), 0)
        rowp = lax.broadcasted_iota(jnp.int32, (tb, p.shape[1]), 0)
        dh1_ref[...] = grads[0] + jnp.where(rowh == tb - 1, jnp.broadcast_to(ch_ref[0:1, :], rowh.shape), 0.0)
        dp_ref[...] = grads[2] + jnp.where(rowp == tb - 1, jnp.broadcast_to(cp_ref[0:1, :], rowp.shape), 0.0)
        ch_ref[0:1, :] = grads[1]
        cp_ref[0:1, :] = grads[3]
        for j, r in enumerate(dc_refs):
            r[...] += grads[4 + j]

    ins = [h1, h1, p, p] + list(consts) + ct_flat
    in_specs = ([_blk_spec(h1, tb, nb), _prev8_spec(h1, tb, nb), _blk_spec(p, tb, nb), _prev8_spec(p, tb, nb)]
                + [_full_spec(c) for c in consts] + [_blk_spec(a, tb, nb) for a in ct_flat])
    out_shape = ([jax.ShapeDtypeStruct(h1.shape, F32), jax.ShapeDtypeStruct(p.shape, F32)]
                 + [jax.ShapeDtypeStruct(c.shape, F32) for c in consts])
    out_specs = [_blk_spec(h1, tb, nb), _blk_spec(p, tb, nb)] + [_full_spec(c) for c in consts]
    return pl.pallas_call(body, name="rwkv_pre_a_bwd", grid=(nb,), in_specs=in_specs, out_specs=out_specs,
                          out_shape=out_shape,
                          scratch_shapes=[pltpu.VMEM((8, h1.shape[1]), F32), pltpu.VMEM((8, p.shape[1]), F32)],
                          compiler_params=_cparams(1))(*ins)


def _my_index():
    return 4 * lax.axis_index("x") + 2 * lax.axis_index("y") + lax.axis_index("c")


def _peer(k):
    x, y, c = lax.axis_index("x"), lax.axis_index("y"), lax.axis_index("c")
    px = 1 - x if k & 4 else x
    py = 1 - y if k & 2 else y
    pc = 1 - c if k & 1 else c
    return (px, py, pc), 4 * px + 2 * py + pc


def _xchg_sems(n):
    return [pltpu.SemaphoreType.DMA((n * (N_DEV - 1),)), pltpu.SemaphoreType.DMA((n * (N_DEV - 1),)),
            pltpu.SemaphoreType.DMA((n,))]


def _xchg_copies(srcs, dsts, sems, scatter, incoming=False):
    send_sems, recv_sems, local_sems = sems
    me = _my_index()
    local, remote = [], []
    for i, (s, d) in enumerate(zip(srcs, dsts)):
        if not incoming:
            local.append(pltpu.make_async_copy(s.at[me] if scatter else s, d.at[me], local_sems.at[i]))
        for k in range(1, N_DEV):
            peer, plin = _peer(k)
            j = i * (N_DEV - 1) + k - 1
            s_slot, d_slot = (me, plin) if incoming else (plin, me)
            remote.append(pltpu.make_async_remote_copy(
                src_ref=s.at[s_slot] if scatter else s, dst_ref=d.at[d_slot], send_sem=send_sems.at[j],
                recv_sem=recv_sems.at[j], device_id=peer, device_id_type=pl.DeviceIdType.MESH))
    return local, remote


def _xchg_start(srcs, dsts, sems, scatter):
    local, out = _xchg_copies(srcs, dsts, sems, scatter)
    for cp in local + out:
        cp.start()


def _xchg_wait(srcs, dsts, sems, scatter):
    for cp in _xchg_copies(srcs, dsts, sems, scatter, incoming=True)[1]:
        cp.wait_recv()
    local, out = _xchg_copies(srcs, dsts, sems, scatter)
    for cp in out:
        cp.wait_send()
    for cp in local:
        cp.wait()


def _xchg_out_shapes(srcs, scatter):
    return [jax.ShapeDtypeStruct(s.shape if scatter else (N_DEV,) + s.shape, s.dtype) for s in srcs]


_ANY = pl.BlockSpec(memory_space=pl.ANY)


def _exchange(name, srcs, scatter):
    n = len(srcs)

    def body(*refs):
        s, d, sems = refs[:n], refs[n:2 * n], refs[2 * n:]
        _xchg_start(s, d, sems, scatter)
        _xchg_wait(s, d, sems, scatter)

    return pl.pallas_call(body, name=name, in_specs=[_ANY] * n, out_specs=[_ANY] * n,
                          out_shape=_xchg_out_shapes(srcs, scatter), scratch_shapes=_xchg_sems(n))(*srcs)


def _scan_state_step(s_ref, w_ref, k_ref, a_ref, b_ref, vt_ref, t, sel):
    new = []
    for h in range(RWKV_HEADS):
        S = s_ref[h]
        hs = pl.ds(h, 1)
        vcol = jnp.sum(jnp.where(sel, vt_ref[h * 64:(h + 1) * 64, :], 0.0), axis=1, keepdims=True)
        sa = jnp.sum(S * a_ref[t, hs, :], axis=1, keepdims=True)
        S = S * w_ref[t, hs, :] + sa * b_ref[t, hs, :] + vcol * k_ref[t, hs, :]
        s_ref[h] = S
        new.append(S)
    return new


def _scan_fwd(r, w, k, a, b, vt, xs):
    n_x = len(xs)
    tn = r.shape[0]
    nc = tn // SCAN_CHUNK
    nsub = SCAN_CHUNK // SCAN_CKPT
    H, Dh = RWKV_HEADS, RWKV_HEAD_DIM

    def body(r_ref, w_ref, k_ref, a_ref, b_ref, vt_ref, *rest):
        x_src, (yt_ref, ck_ref) = rest[:n_x], rest[n_x:n_x + 2]
        x_dst, s_ref, sems = rest[n_x + 2:2 * n_x + 2], rest[2 * n_x + 2], rest[2 * n_x + 3:]

        @pl.when(pl.program_id(0) == 0)
        def _():
            s_ref[...] = jnp.zeros(s_ref.shape, F32)
            _xchg_start(x_src, x_dst, sems, False)

        yt_ref[...] = jnp.zeros(yt_ref.shape, F32)
        lane = lax.broadcasted_iota(jnp.int32, (Dh, SCAN_CHUNK), 1)

        def sub_body(sub, carry):
            ck_ref[sub] = s_ref[...]

            def step(tt, carry2):
                t = sub * SCAN_CKPT + tt
                sel = lane == t
                new = _scan_state_step(s_ref, w_ref, k_ref, a_ref, b_ref, vt_ref, t, sel)
                for h in range(H):
                    ycol = jnp.sum(new[h] * r_ref[t, pl.ds(h, 1), :], axis=1, keepdims=True)
                    rows = slice(h * Dh, (h + 1) * Dh)
                    yt_ref[rows, :] = jnp.where(sel, ycol, yt_ref[rows, :])
                return carry2

            return lax.fori_loop(0, SCAN_CKPT, step, carry)

        lax.fori_loop(0, nsub, sub_body, 0)

        @pl.when(pl.program_id(0) == nc - 1)
        def _():
            _xchg_wait(x_src, x_dst, sems, False)

    hl = pl.BlockSpec((SCAN_CHUNK, H, Dh), lambda c: (c, 0, 0))
    tl = pl.BlockSpec((H * Dh, SCAN_CHUNK), lambda c: (0, c))
    res = pl.pallas_call(
        body, name="rwkv_scan_fwd", grid=(nc,), in_specs=[hl, hl, hl, hl, hl, tl] + [_ANY] * n_x,
        out_specs=[tl, pl.BlockSpec((nsub, H, Dh, Dh), lambda c: (c, 0, 0, 0))] + [_ANY] * n_x,
        out_shape=[jax.ShapeDtypeStruct((H * Dh, tn), F32),
                   jax.ShapeDtypeStruct((nc * nsub, H, Dh, Dh), F32)] + _xchg_out_shapes(xs, False),
        scratch_shapes=[pltpu.VMEM((H, Dh, Dh), F32)] + _xchg_sems(n_x),
        compiler_params=_cparams(1))(r, w, k, a, b, vt, *xs)
    return res[0], res[1], res[2:]


def _scan_bwd(r, w, k, a, b, vt, dyt, ck, xs):
    n_x = len(xs)
    tn = r.shape[0]
    nc = tn // SCAN_CHUNK
    nsub = SCAN_CHUNK // SCAN_CKPT
    H, Dh = RWKV_HEADS, RWKV_HEAD_DIM

    def body(r_ref, w_ref, k_ref, a_ref, b_ref, vt_ref, dyt_ref, ck_ref, *rest):
        x_src = rest[:n_x]
        dr_ref, dw_ref, dk_ref, da_ref, db_ref, dvt_ref = rest[n_x:n_x + 6]
        x_dst = rest[n_x + 6:2 * n_x + 6]
        s_ref, g_ref, sp_ref = rest[2 * n_x + 6:2 * n_x + 9]
        sems = rest[2 * n_x + 9:]

        @pl.when(pl.program_id(0) == 0)
        def _():
            g_ref[...] = jnp.zeros(g_ref.shape, F32)
            _xchg_start(x_src, x_dst, sems, True)

        dvt_ref[...] = jnp.zeros(dvt_ref.shape, F32)
        lane = lax.broadcasted_iota(jnp.int32, (Dh, SCAN_CHUNK), 1)

        def sub_body(ss, carry):
            sub = nsub - 1 - ss
            s_ref[...] = ck_ref[sub]

            def fstep(tt, c2):
                t = sub * SCAN_CKPT + tt
                sp_ref[tt] = s_ref[...]
                _scan_state_step(s_ref, w_ref, k_ref, a_ref, b_ref, vt_ref, t, lane == t)
                return c2

            lax.fori_loop(0, SCAN_CKPT, fstep, 0)

            def bstep(uu, c2):
                tt = SCAN_CKPT - 1 - uu
                t = sub * SCAN_CKPT + tt
                sel = lane == t
                for h in range(H):
                    hs = pl.ds(h, 1)
                    rows = slice(h * Dh, (h + 1) * Dh)
                    P = sp_ref[tt, h]
                    G = g_ref[h]
                    wr, kr, ar, br, rr = w_ref[t, hs, :], k_ref[t, hs, :], a_ref[t, hs, :], b_ref[t, hs, :], r_ref[t, hs, :]
                    vcol = jnp.sum(jnp.where(sel, vt_ref[rows, :], 0.0), axis=1, keepdims=True)
                    dycol = jnp.sum(jnp.where(sel, dyt_ref[rows, :], 0.0), axis=1, keepdims=True)
                    sa = jnp.sum(P * ar, axis=1, keepdims=True)
                    Sn = P * wr + sa * br + vcol * kr
                    G = G + dycol * rr
                    dr_ref[t, hs, :] = jnp.sum(Sn * dycol, axis=0, keepdims=True)
                    dvcol = jnp.sum(G * kr, axis=1, keepdims=True)
                    dvt_ref[rows, :] = jnp.where(sel, dvcol, dvt_ref[rows, :])
                    dk_ref[t, hs, :] = jnp.sum(G * vcol, axis=0, keepdims=True)
                    gb = jnp.sum(G * br, axis=1, keepdims=True)
                    dw_ref[t, hs, :] = jnp.sum(P * G, axis=0, keepdims=True)
                    da_ref[t, hs, :] = jnp.sum(P * gb, axis=0, keepdims=True)
                    db_ref[t, hs, :] = jnp.sum(G * sa, axis=0, keepdims=True)
                    g_ref[h] = G * wr + gb * ar
                return c2

            lax.fori_loop(0, SCAN_CKPT, bstep, 0)
            return carry

        lax.fori_loop(0, nsub, sub_body, 0)

        @pl.when(pl.program_id(0) == nc - 1)
        def _():
            _xchg_wait(x_src, x_dst, sems, True)

    hl = pl.BlockSpec((SCAN_CHUNK, H, Dh), lambda c: (nc - 1 - c, 0, 0))
    tl = pl.BlockSpec((H * Dh, SCAN_CHUNK), lambda c: (0, nc - 1 - c))
    hshape = jax.ShapeDtypeStruct((tn, H, Dh), F32)
    res = pl.pallas_call(
        body, name="rwkv_scan_bwd", grid=(nc,),
        in_specs=[hl, hl, hl, hl, hl, tl, tl, pl.BlockSpec((nsub, H, Dh, Dh), lambda c: (nc - 1 - c, 0, 0, 0))]
        + [_ANY] * n_x,
        out_specs=[hl, hl, hl, hl, hl, tl] + [_ANY] * n_x,
        out_shape=[hshape, hshape, hshape, hshape, hshape, jax.ShapeDtypeStruct((H * Dh, tn), F32)]
        + _xchg_out_shapes(xs, True),
        scratch_shapes=[pltpu.VMEM((H, Dh, Dh), F32), pltpu.VMEM((H, Dh, Dh), F32),
                        pltpu.VMEM((SCAN_CKPT, H, Dh, Dh), F32)] + _xchg_sems(n_x),
        compiler_params=_cparams(1))(r, w, k, a, b, vt, dyt, ck, *xs)
    return res[:6], res[6:]


def _decay_mask(lg, i, j, blk):
    rows = lax.broadcasted_iota(jnp.int32, (blk, blk), 0)
    cols = lax.broadcasted_iota(jnp.int32, (blk, blk), 1)
    dd = (rows - cols + (i - j) * blk).astype(F32)
    return jnp.where(dd >= 0.0, jnp.exp(lg * jnp.maximum(dd, 0.0)), 0.0)


_NT = (((1,), (1,)), ((), ()))
_TN = (((0,), (0,)), ((), ()))


def _ret_attn_fwd(lg, q, k, v, blk=ATT_BLOCK):
    tn = q.shape[0]
    Dh = RET_HEAD_DIM

    def body(lg_ref, q_ref, k_ref, v_ref, o_ref):
        i = pl.program_id(1)
        lgv = lg_ref[0][:, 0:1]
        qb = q_ref[...].astype(BF16)

        def jb(j, acc):
            ks = pl.ds(pl.multiple_of(j * blk, blk), blk)
            s = lax.dot_general(qb, k_ref[ks, :].astype(BF16), _NT, preferred_element_type=F32)
            s = s * _decay_mask(lgv, i, j, blk)
            return acc + jnp.dot(s.astype(BF16), v_ref[ks, :].astype(BF16), preferred_element_type=F32)

        o_ref[...] = lax.fori_loop(0, i + 1, jb, jnp.zeros((blk, Dh), F32))

    full = pl.BlockSpec((tn, Dh), lambda h, i: (0, h))
    qs = pl.BlockSpec((blk, Dh), lambda h, i: (i, h))
    return pl.pallas_call(
        body, name="ret_attn_fwd", grid=(RET_HEADS, tn // blk),
        in_specs=[pl.BlockSpec((1, 1, 128), lambda h, i: (h, 0, 0)), qs, full, full],
        out_specs=qs, out_shape=jax.ShapeDtypeStruct(q.shape, F32), compiler_params=_cparams(2))(lg, q, k, v)


def _ret_attn_bwd(lg, q, k, v, do, blk=ATT_BLOCK):
    tn = q.shape[0]
    nb = tn // blk
    Dh = RET_HEAD_DIM

    def body(lg_ref, q_ref, k_ref, v_ref, do_ref, dq_ref, dk_ref, dv_ref):
        lgv = lg_ref[0][:, 0:1]
        dk_ref[...] = jnp.zeros(dk_ref.shape, F32)
        dv_ref[...] = jnp.zeros(dv_ref.shape, F32)

        def ib(i, carry):
            qs = pl.ds(pl.multiple_of(i * blk, blk), blk)
            qb = q_ref[qs, :].astype(BF16)
            dob = do_ref[qs, :].astype(BF16)

            def jb(j, dq):
                ks = pl.ds(pl.multiple_of(j * blk, blk), blk)
                kb = k_ref[ks, :].astype(BF16)
                vb = v_ref[ks, :].astype(BF16)
                dm = _decay_mask(lgv, i, j, blk)
                s = lax.dot_general(qb, kb, _NT, preferred_element_type=F32) * dm
                ds = lax.dot_general(dob, vb, _NT, preferred_element_type=F32) * dm
                sb, dsb = s.astype(BF16), ds.astype(BF16)
                dv_ref[ks, :] += lax.dot_general(sb, dob, _TN, preferred_element_type=F32)
                dk_ref[ks, :] += lax.dot_general(dsb, qb, _TN, preferred_element_type=F32)
                return dq + jnp.dot(dsb, kb, preferred_element_type=F32)

            dq_ref[qs, :] = lax.fori_loop(0, i + 1, jb, jnp.zeros((blk, Dh), F32))
            return carry

        lax.fori_loop(0, nb, ib, 0)

    full = pl.BlockSpec((tn, Dh), lambda h: (0, h))
    sh = jax.ShapeDtypeStruct(q.shape, F32)
    return pl.pallas_call(
        body, name="ret_attn_bwd", grid=(RET_HEADS,),
        in_specs=[pl.BlockSpec((1, 1, 128), lambda h: (h, 0, 0)), full, full, full, full],
        out_specs=[full, full, full], out_shape=[sh, sh, sh], compiler_params=_cparams(1))(lg, q, k, v, do)


def _next8_spec(a, tb):
    r = tb // 8
    last = a.shape[0] // 8 - 1
    return pl.BlockSpec((8, a.shape[1]), lambda i: (jnp.minimum((i + 1) * r, last), 0))


def _conv_taps(g_ext, cw_ref, cb_ref):
    return (cw_ref[2:3, :] * g_ext + cw_ref[1:2, :] * pltpu.roll(g_ext, 1, 0)
            + cw_ref[0:1, :] * pltpu.roll(g_ext, 2, 0) + cb_ref[...])


def _glu_fwd(gate, up, cw, cb, tb=TOK_BLOCK):
    tn = gate.shape[0]

    def body(g_ref, gh_ref, u_ref, cw_ref, cb_ref, o_ref):
        halo = jnp.where(pl.program_id(0) == 0, 0.0, gh_ref[...])
        g_ext = jnp.concatenate([halo, g_ref[...]], axis=0)
        gc = _conv_taps(g_ext, cw_ref, cb_ref)[8:, :]
        o_ref[...] = gc * _sigmoid(gc) * u_ref[...]

    return pl.pallas_call(
        body, name="glu_fwd", grid=(tn // tb,),
        in_specs=[_blk_spec(gate, tb), _prev8_spec(gate, tb), _blk_spec(up, tb), _full_spec(cw), _full_spec(cb)],
        out_specs=_blk_spec(gate, tb), out_shape=jax.ShapeDtypeStruct(gate.shape, F32),
        compiler_params=_cparams(1))(gate, gate, up, cw, cb)


def _glu_bwd(gate, up, dact, cw, cb, tb=TOK_BLOCK):
    tn = gate.shape[0]
    nb = tn // tb

    def body(g_ref, gp_ref, gn_ref, u_ref, un_ref, d_ref, dn_ref, cw_ref, cb_ref, dg_ref, du_ref, dcw_ref, dcb_ref):
        i = pl.program_id(0)
        gprev = jnp.where(i == 0, 0.0, gp_ref[...])
        dnext = jnp.where(i == nb - 1, 0.0, dn_ref[...])
        g_ext = jnp.concatenate([gprev, g_ref[...], gn_ref[...]], axis=0)
        gc = _conv_taps(g_ext, cw_ref, cb_ref)[8:, :]
        u_e = jnp.concatenate([u_ref[...], un_ref[...]], axis=0)
        d_e = jnp.concatenate([d_ref[...], dnext], axis=0)
        s = _sigmoid(gc)
        dgc = d_e * u_e * (s * (1.0 + gc * (1.0 - s)))
        du_ref[...] = d_ref[...] * (gc * s)[:tb, :]
        n_e = tb + 8
        dg_ref[...] = (cw_ref[2:3, :] * dgc + cw_ref[1:2, :] * pltpu.roll(dgc, n_e - 1, 0)
                       + cw_ref[0:1, :] * pltpu.roll(dgc, n_e - 2, 0))[:tb, :]

        @pl.when(i == 0)
        def _():
            dcw_ref[...] = jnp.zeros(dcw_ref.shape, F32)
            dcb_ref[...] = jnp.zeros(dcb_ref.shape, F32)

        dgc_b = dgc[:tb, :]
        g0 = g_ext[8:8 + tb, :]
        g1 = pltpu.roll(g_ext, 1, 0)[8:8 + tb, :]
        g2 = pltpu.roll(g_ext, 2, 0)[8:8 + tb, :]
        dcw_ref[2:3, :] += jnp.sum(dgc_b * g0, axis=0, keepdims=True)
        dcw_ref[1:2, :] += jnp.sum(dgc_b * g1, axis=0, keepdims=True)
        dcw_ref[0:1, :] += jnp.sum(dgc_b * g2, axis=0, keepdims=True)
        dcb_ref[...] += jnp.sum(dgc_b, axis=0, keepdims=True)

    sh = jax.ShapeDtypeStruct(gate.shape, F32)
    return pl.pallas_call(
        body, name="glu_bwd", grid=(nb,),
        in_specs=[_blk_spec(gate, tb), _prev8_spec(gate, tb), _next8_spec(gate, tb), _blk_spec(up, tb),
                  _next8_spec(up, tb), _blk_spec(dact, tb), _next8_spec(dact, tb), _full_spec(cw), _full_spec(cb)],
        out_specs=[_blk_spec(gate, tb), _blk_spec(gate, tb), _full_spec(cw), _full_spec(cb)],
        out_shape=[sh, sh, jax.ShapeDtypeStruct(cw.shape, F32), jax.ShapeDtypeStruct(cb.shape, F32)],
        compiler_params=_cparams(1))(gate, gate, gate, up, up, dact, dact, cw, cb)


def _final_loss(x2, tgt, g, tb=TOK_BLOCK):
    tn, dm = x2.shape

    def body(x_ref, t_ref, g_ref, l_ref, dx_ref, dg_ref):
        y, vjp = jax.vjp(_rms_fn, x_ref[...], g_ref[...])
        err = y - t_ref[...]
        dx, dg = vjp(err * (1.0 / dm))

        @pl.when(pl.program_id(0) == 0)
        def _():
            l_ref[...] = jnp.zeros(l_ref.shape, F32)
            dg_ref[...] = jnp.zeros(dg_ref.shape, F32)

        part = 0.5 * jnp.sum(jnp.mean(err * err, axis=-1, keepdims=True), axis=0, keepdims=True)
        l_ref[...] += jnp.broadcast_to(part, l_ref.shape)
        dx_ref[...] = dx
        dg_ref[...] += dg

    return pl.pallas_call(
        body, name="final_loss", grid=(tn // tb,),
        in_specs=[_blk_spec(x2, tb), _blk_spec(tgt, tb), _full_spec(g)],
        out_specs=[pl.BlockSpec((8, 128), lambda i: (0, 0)), _blk_spec(x2, tb), _full_spec(g)],
        out_shape=[jax.ShapeDtypeStruct((8, 128), F32), jax.ShapeDtypeStruct(x2.shape, F32),
                   jax.ShapeDtypeStruct(g.shape, F32)],
        compiler_params=_cparams(1))(x2, tgt, g)


def _pad_cols(w, n):
    return jnp.pad(w, ((0, 0), (0, n - w.shape[1])))


def _pad_rows(w, n):
    return jnp.pad(w, ((0, n - w.shape[0]), (0, 0)))


def _to_heads(z):
    return z.reshape(z.shape[0], RWKV_HEADS, RWKV_HEAD_DIM)


def _to_flat(z):
    return z.reshape(z.shape[0], RWKV_WIDTH)


def _local_step(x, tgt, W, late):
    tn = x.shape[0]
    Wd = RWKV_WIDTH
    row = lambda z: z.reshape(1, -1)
    g_mix, g_ffn, g_fin = row(W['norm_mix_g']), row(W['norm_ffn_g']), row(W['norm_final_g'])

    (h1,) = _tok_fwd("norm_mix_fwd", lambda a, g: (_rms_fn(a, g),), [x], [g_mix], [(D_MODEL,)])
    proj = _mm("proj_fwd", h1, W['w_in_t'], tb=True)
    p_rkv = proj[:, :3 * Wd]
    pre_consts = [row(W['rwkv_mu_w']), row(W['rwkv_mu_a']), row(W['rwkv_mu_g']), row(W['rwkv_mu_r']),
                  row(W['rwkv_mu_k']), row(W['rwkv_mu_v']), row(W['rwkv_w0']),
                  _pad_cols(W['rwkv_w1'], LORA_PAD), _pad_rows(W['rwkv_w2'], LORA_PAD), row(W['rwkv_a0']),
                  _pad_cols(W['rwkv_a1'], LORA_PAD), _pad_rows(W['rwkv_a2'], LORA_PAD),
                  W['rwkv_g1'], W['rwkv_g2']]
    r, k0, v, decay, a, g = _pre_a_fwd(h1, p_rkv, pre_consts)
    k_k, k_a = _to_heads(row(W['rwkv_k_k']))[0], _to_heads(row(W['rwkv_k_a']))[0]
    k0h, ah = _to_heads(k0), _to_heads(a)
    nkk, kh, bh = _tok_fwd("rwkv_pre_b_fwd", _pre_b_fn, [k0h, ah], [k_k, k_a], [(RWKV_HEADS, RWKV_HEAD_DIM)] * 3)
    rh, wh, vh, gh = _to_heads(r), _to_heads(decay), _to_heads(v), _to_heads(g)
    vt = v.T
    yt, ck, gathered = _scan_fwd(rh, wh, kh, nkk, bh, vt, late)
    w_out, w_gate_t, w_up_t, w_down = [g_.reshape(-1, D_MODEL) for g_ in gathered]
    yh = _to_heads(yt.T)
    post_consts = [_to_heads(row(W['rwkv_lnx_w']))[0], _to_heads(row(W['rwkv_lnx_b']))[0], W['rwkv_r_k']]
    (y_rwkv_h,) = _tok_fwd("rwkv_post_fwd", _rwkv_post_fn, [yh, rh, kh, vh, gh], post_consts,
                           [(RWKV_HEADS, RWKV_HEAD_DIM)])
    y_rwkv = _to_flat(y_rwkv_h)

    pos = jnp.arange(tn, dtype=F32)
    half = RET_HEAD_DIM // 2
    inv_freq = ROPE_BASE ** (-jnp.arange(half, dtype=F32) / half)
    ang = pos[:, None] * inv_freq[None, :]
    cos2 = jnp.concatenate([jnp.cos(ang), jnp.cos(ang)], axis=1)
    sin2 = jnp.concatenate([-jnp.sin(ang), jnp.sin(ang)], axis=1)
    lg = jnp.log(1.0 - 2.0 ** (-5.0 - jnp.arange(RET_HEADS, dtype=F32)))
    lg = jnp.broadcast_to(lg[:, None, None], (RET_HEADS, 1, 128))
    q_p, k_p = proj[:, 3 * Wd:4 * Wd], proj[:, 4 * Wd:5 * Wd]
    v_ret, g_ret = proj[:, 5 * Wd:6 * Wd], proj[:, 6 * Wd:7 * Wd]
    q_rot, k_rot = _tok_fwd("ret_rotary_fwd", _rotary_fn, [cos2, sin2, q_p, k_p], [], [(RET_WIDTH,)] * 2)
    y_ret_raw = _ret_attn_fwd(lg, q_rot, k_rot, v_ret)
    gn_w = row(W['ret_gn_w'])
    (y_ret,) = _tok_fwd("ret_post_fwd", _ret_post_fn, [y_ret_raw, g_ret], [gn_w], [(RET_WIDTH,)])

    ycat = jnp.concatenate([y_rwkv, y_ret], axis=1)
    x1 = _mm("out_proj_fwd", ycat, w_out, add=x)
    (h2,) = _tok_fwd("norm_ffn_fwd", lambda a_, g_: (_rms_fn(a_, g_),), [x1], [g_ffn], [(D_MODEL,)])
    gate = _mm("ffn_gate_fwd", h2, w_gate_t, tb=True)
    up = _mm("ffn_up_fwd", h2, w_up_t, tb=True)
    cw = W['ffn_conv_w']
    cb = row(W['ffn_conv_b'])
    act = _glu_fwd(gate, up, cw, cb)
    x2 = _mm("ffn_down_fwd", act, w_down, add=x1)
    loss8, dx2, dg_fin = _final_loss(x2, tgt, g_fin)

    G = {'norm_final_g': dg_fin}
    dact = _mm("ffn_down_dx", dx2, w_down, tb=True)
    d_down = _mm("ffn_down_dw", act, dx2, ta=True)
    dgate, dup, dcw, dcb = _glu_bwd(gate, up, dact, cw, cb)
    G['ffn_conv_w'], G['ffn_conv_b'] = dcw, dcb
    dh2 = _mm("ffn_gate_dx", dgate, w_gate_t)
    dh2 = _mm("ffn_up_dx", dup, w_up_t, add=dh2)
    d_gate_t = _mm("ffn_gate_dw", dgate, h2, ta=True)
    d_up_t = _mm("ffn_up_dw", dup, h2, ta=True)
    dx1, G['norm_ffn_g'] = _tok_bwd("norm_ffn_bwd", lambda a_, g_: (_rms_fn(a_, g_),), [], [x1], [g_ffn], [dh2], add=dx2)
    dycat = _mm("out_proj_dx", dx1, w_out, tb=True)
    d_out = _mm("out_proj_dw", ycat, dx1, ta=True)
    late_grads = [z.reshape(N_DEV, -1, D_MODEL) for z in (d_out, d_gate_t, d_up_t, d_down)]
    dy_rwkv, dy_ret = dycat[:, :Wd], dycat[:, Wd:]

    dyr_raw, dg_ret, G['ret_gn_w'] = _tok_bwd("ret_post_bwd", _ret_post_fn, [], [y_ret_raw, g_ret], [gn_w], [dy_ret])
    dq_rot, dk_rot, dv_ret = _ret_attn_bwd(lg, q_rot, k_rot, v_ret, dyr_raw)
    dq_p, dk_p = _tok_bwd("ret_rotary_bwd", _rotary_fn, [cos2, sin2], [q_p, k_p], [], [dq_rot, dk_rot])

    dyh, drh1, dkh1, dvh1, dgh, G['rwkv_lnx_w'], G['rwkv_lnx_b'], G['rwkv_r_k'] = _tok_bwd(
        "rwkv_post_bwd", _rwkv_post_fn, [], [yh, rh, kh, vh, gh], post_consts, [_to_heads(dy_rwkv)])
    dyt = _to_flat(dyh).T
    (drh2, dwh, dkh2, dnkk, dbh, dvt), late_parts = _scan_bwd(rh, wh, kh, nkk, bh, vt, dyt, ck, late_grads)
    dk0h, dah, G['rwkv_k_k'], G['rwkv_k_a'] = _tok_bwd(
        "rwkv_pre_b_bwd", _pre_b_fn, [], [k0h, ah], [k_k, k_a], [dnkk, (dkh1, dkh2), dbh])
    pre_cts = [(_to_flat(drh1), _to_flat(drh2)), _to_flat(dk0h), (_to_flat(dvh1), dvt.T),
               _to_flat(dwh), _to_flat(dah), _to_flat(dgh)]
    pre_out = _pre_a_bwd(h1, p_rkv, pre_consts, pre_cts)
    dh1_a, dp_rkv = pre_out[0], pre_out[1]
    (G['rwkv_mu_w'], G['rwkv_mu_a'], G['rwkv_mu_g'], G['rwkv_mu_r'], G['rwkv_mu_k'], G['rwkv_mu_v'], G['rwkv_w0'],
     dw1, dw2, G['rwkv_a0'], da1, da2, G['rwkv_g1'], G['rwkv_g2']) = pre_out[2:]
    G['rwkv_w1'], G['rwkv_w2'] = dw1[:, :64], dw2[:64, :]
    G['rwkv_a1'], G['rwkv_a2'] = da1[:, :64], da2[:64, :]

    dproj = jnp.concatenate([dp_rkv, dq_p, dk_p, dv_ret, dg_ret], axis=1)
    dh1 = _mm("proj_dx", dproj, W['w_in_t'], add=dh1_a)
    G['w_in_t'] = _mm("proj_dw", dproj, h1, ta=True)
    dx, G['norm_mix_g'] = _tok_bwd("norm_mix_bwd", lambda a_, g_: (_rms_fn(a_, g_),), [], [x], [g_mix], [dh1], add=dx1)
    return loss8[0, 0], dx, G, late_parts


def _adamw(name, parts, w, m, v):
    rows, cols = w.shape
    tb = max(t for t in range(8, 65, 8) if rows % t == 0) if rows > 64 else rows
    c1 = 1.0 - ADAM_B1 ** ADAM_STEP
    c2 = 1.0 - ADAM_B2 ** ADAM_STEP

    def body(p_ref, w_ref, m_ref, v_ref, g_ref, d_ref, nm_ref, nv_ref):
        g = p_ref[0]
        for d in range(1, N_DEV):
            g = g + p_ref[d]
        mn = ADAM_B1 * m_ref[...] + (1.0 - ADAM_B1) * g
        vn = ADAM_B2 * v_ref[...] + (1.0 - ADAM_B2) * (g * g)
        m_hat = mn / c1
        v_hat = vn / c2
        g_ref[...] = g
        d_ref[...] = -ADAM_LR * (m_hat / (jnp.sqrt(v_hat) + ADAM_EPS) + ADAM_WD * w_ref[...])
        nm_ref[...] = mn
        nv_ref[...] = vn

    spec = pl.BlockSpec((tb, cols), lambda i: (i, 0))
    sh = jax.ShapeDtypeStruct((rows, cols), F32)
    return pl.pallas_call(
        body, name=name, grid=(rows // tb,),
        in_specs=[pl.BlockSpec((N_DEV, tb, cols), lambda i: (0, i, 0)), spec, spec, spec],
        out_specs=[spec] * 4, out_shape=[sh] * 4, compiler_params=_cparams(1))(parts, w, m, v)


def _local_shape(name):
    gs, ax = SHARDED[name]
    ls = list(gs)
    ls[ax] //= N_DEV
    return tuple(ls)


def _seg(flat, seg):
    n = flat.shape[-1]
    pad = _round_up(n, seg) - n
    if pad:
        flat = jnp.pad(flat, [(0, 0)] * (flat.ndim - 1) + [(0, pad)])
    return flat


def _split3(w):
    hi = w.astype(BF16)
    r1 = w - hi.astype(F32)
    mid = r1.astype(BF16)
    lo = (r1 - mid.astype(F32)).astype(BF16)
    return hi, mid, lo


def _pack_small_shards(shards):
    pieces = []
    for name in SMALL_NAMES:
        flat = shards[name].reshape(-1)
        if name == 'ffn_conv_w':
            pieces += [_seg(p, BF16_SEG) for p in _split3(flat)]
        else:
            pieces.append(flat.astype(BF16))
    return jnp.concatenate(pieces).reshape(-1, 128)


def _unpack_small(gathered):
    flat = gathered.reshape(N_DEV, -1)
    out, off = {}, 0
    for name in SMALL_NAMES:
        gs, ax = SHARDED[name]
        ls = _local_shape(name)
        n = int(np.prod(ls))
        if name == 'ffn_conv_w':
            nseg = _round_up(n, BF16_SEG)
            hi, mid, lo = (flat[:, off + j * nseg: off + j * nseg + n].astype(F32) for j in range(3))
            sh = ((hi + mid) + lo).reshape(N_DEV, 3, -1)
            out[name] = jnp.swapaxes(sh, 0, 1).reshape(3, D_FF)
            off += 3 * nseg
        else:
            sh = flat[:, off:off + n].reshape((N_DEV,) + ls[1:])
            out[name] = sh.reshape(gs[1:]) if ax == 1 else jnp.swapaxes(sh, 0, 1).reshape(gs[1:])
            off += n
    return out


def _small_pieces(sharded, repl):
    return [sharded[n].reshape(-1) for n in SMALL_NAMES] + [repl[n].reshape(-1) for n in REPL_NAMES]


def _pack_small_local(d):
    flat = jnp.concatenate(_small_pieces(d, d))
    return _seg(flat, F32_SEG).reshape(-1, 128)


def _pack_small_grads(G):
    pieces = []
    for name in SMALL_NAMES:
        gs, ax = SHARDED[name]
        g = G[name]
        if name == 'ffn_conv_w':
            sh = jnp.swapaxes(g.reshape(3, N_DEV, -1), 0, 1)
        elif ax == 1:
            sh = g
        else:
            sh = jnp.swapaxes(g.reshape(g.shape[0], N_DEV, -1), 0, 1)
        pieces.append(sh.reshape(N_DEV, -1))
    rep = jnp.concatenate([G[n].reshape(-1) for n in REPL_NAMES])
    pieces.append(jnp.broadcast_to(rep[None, :], (N_DEV, rep.shape[0])))
    flat = _seg(jnp.concatenate(pieces, axis=1), F32_SEG)
    return flat.reshape(N_DEV, -1, 128)


def _unpack_small_local(packed, local_shapes):
    flat = packed.reshape(-1)
    out, off = {}, 0
    for name in SMALL_NAMES + REPL_NAMES:
        n = int(np.prod(local_shapes[name]))
        out[name] = flat[off:off + n].reshape(local_shapes[name])
        off += n
    return out


def kernel(x, *rest):
    nw = len(WEIGHT_NAMES)
    assert len(rest) == 3 * nw + 1
    weights = dict(zip(WEIGHT_NAMES, rest[:nw]))
    loss_target = rest[nw]
    moms = dict(zip(WEIGHT_NAMES, rest[nw + 1:2 * nw + 1]))
    vars_ = dict(zip(WEIGHT_NAMES, rest[2 * nw + 1:]))
    local_shapes = {n: weights[n].shape for n in WEIGHT_NAMES}

    def native2d(name, a):
        a2 = a.reshape(a.shape[-2], a.shape[-1])
        return a2.T if name in BIG_T else a2

    def from2d(name, a2):
        return (a2.T if name in BIG_T else a2).reshape(local_shapes[name])

    big_w = {n: native2d(n, weights[n]) for n in BIG_NAMES}
    w_in_t_sh = big_w['w_in'].astype(BF16)
    late = [big_w[n].astype(BF16) for n in LATE_NAMES]
    small_sh = _pack_small_shards({n: weights[n] for n in SMALL_NAMES})
    w_in_g, small_g = _exchange("weights_all_gather", [w_in_t_sh, small_sh], False)
    W = _unpack_small(small_g)
    W['w_in_t'] = w_in_g.reshape(-1, D_MODEL)
    for n in REPL_NAMES:
        W[n] = weights[n][0] if n != 'norm_final_g' else weights[n]

    loss, dx, G, late_parts = _local_step(x[0], loss_target[0], W, late)

    w_in_parts, small_parts = _exchange(
        "grads_all_to_all", [G['w_in_t'].reshape(N_DEV, -1, D_MODEL), _pack_small_grads(G)], True)
    results = {}
    for n, parts in zip(['w_in'] + LATE_NAMES, [w_in_parts] + list(late_parts)):
        res = _adamw("adamw_" + n, parts, big_w[n], native2d(n, moms[n]), native2d(n, vars_[n]))
        results[n] = [from2d(n, r) for r in res]
    small_res = _adamw("adamw_small", small_parts, _pack_small_local(weights), _pack_small_local(moms),
                       _pack_small_local(vars_))
    small_out = [_unpack_small_local(p, local_shapes) for p in small_res]

    loss = lax.psum(loss, ("x", "y", "c"))
    outs = [loss, dx[None]]
    for j in range(4):
        outs += [results[n][j] if n in results else small_out[j][n] for n in WEIGHT_NAMES]
    return tuple(outs)
```

```python
import functools
import math

import numpy as np
import jax
import jax.numpy as jnp
from jax import lax
from jax.experimental import pallas as pl
from jax.experimental.pallas import tpu as pltpu

F32 = jnp.float32
BF16 = jnp.bfloat16

N_DEV = 8
D_MODEL = 1024
RWKV_HEADS = 8
RWKV_HEAD_DIM = 64
RWKV_WIDTH = 512
RET_HEADS = 4
RET_HEAD_DIM = 128
RET_WIDTH = 512
LORA_PAD = 128
D_FF = 2816
NORM_EPS = 1e-6
RWKV_GN_EPS = 64e-5
RET_GN_EPS = 1e-5
ROPE_BASE = 10000.0
ADAM_LR, ADAM_B1, ADAM_B2, ADAM_EPS, ADAM_WD, ADAM_STEP = 0.001, 0.9, 0.999, 1e-08, 0.01, 10

VMEM_LIMIT = 56 * 1024 * 1024
TOK_BLOCK = 256
SCAN_CHUNK = 64
ATT_BLOCK = 256
BF16_SEG = 2048
F32_SEG = 1024

WEIGHT_NAMES = ['norm_mix_g', 'w_in', 'rwkv_mu_r', 'rwkv_mu_k', 'rwkv_mu_v', 'rwkv_mu_w', 'rwkv_mu_a',
                'rwkv_mu_g', 'rwkv_w0', 'rwkv_w1', 'rwkv_w2', 'rwkv_a0', 'rwkv_a1', 'rwkv_a2', 'rwkv_g1',
                'rwkv_g2', 'rwkv_k_k', 'rwkv_k_a', 'rwkv_r_k', 'rwkv_lnx_w', 'rwkv_lnx_b', 'ret_gn_w',
                'w_out', 'norm_ffn_g', 'ffn_w_gate', 'ffn_w_up', 'ffn_conv_w', 'ffn_conv_b', 'ffn_w_down',
                'norm_final_g']
SHARDED = {
    'w_in': ((1, 1024, 3584), 2), 'rwkv_w1': ((1, 1024, 64), 1), 'rwkv_w2': ((1, 64, 512), 2),
    'rwkv_a1': ((1, 1024, 64), 1), 'rwkv_a2': ((1, 64, 512), 2), 'rwkv_g1': ((1, 1024, 128), 1),
    'rwkv_g2': ((1, 128, 512), 2), 'w_out': ((1, 1024, 1024), 1), 'ffn_w_gate': ((1, 1024, 2816), 2),
    'ffn_w_up': ((1, 1024, 2816), 2), 'ffn_conv_w': ((1, 3, 1, 2816), 3), 'ffn_w_down': ((1, 2816, 1024), 1),
}
REPL_NAMES = [n for n in WEIGHT_NAMES if n not in SHARDED]
BIG_NAMES = ['w_in', 'w_out', 'ffn_w_gate', 'ffn_w_up', 'ffn_w_down']
BIG_T = ('w_in', 'ffn_w_gate', 'ffn_w_up')
LATE_NAMES = ['w_out', 'ffn_w_gate', 'ffn_w_up', 'ffn_w_down']
SMALL_NAMES = [n for n in WEIGHT_NAMES if n in SHARDED and n not in BIG_NAMES]


def _cparams(n_grid):
    return pltpu.CompilerParams(dimension_semantics=("arbitrary",) * n_grid, vmem_limit_bytes=VMEM_LIMIT)


def _round_up(n, m):
    return (n + m - 1) // m * m


def _tile(n, cap):
    best = None
    for t in range(128, min(n, cap) + 1, 128):
        if n % t == 0:
            best = t
    assert best is not None, n
    return best


@jax.custom_vjp
def _bdot(x, w):
    return jnp.dot(x.astype(BF16), w.astype(BF16), preferred_element_type=F32)


def _bdot_fwd(x, w):
    return _bdot(x, w), (x, w)


def _bdot_bwd(res, g):
    x, w = res
    gb = g.astype(BF16)
    dx = lax.dot_general(gb, w.astype(BF16), (((1,), (1,)), ((), ())), preferred_element_type=F32)
    dw = lax.dot_general(x.astype(BF16), gb, (((0,), (0,)), ((), ())), preferred_element_type=F32)
    return dx, dw.astype(w.dtype)


_bdot.defvjp(_bdot_fwd, _bdot_bwd)


@jax.custom_vjp
def _shift_rows(x, prev):
    rolled = pltpu.roll(x, 1, 0)
    row = lax.broadcasted_iota(jnp.int32, x.shape, 0)
    return jnp.where(row == 0, jnp.broadcast_to(prev, x.shape), rolled)


def _shift_rows_fwd(x, prev):
    return _shift_rows(x, prev), None


def _shift_rows_bwd(_, g):
    n = g.shape[0]
    rolled = pltpu.roll(g, n - 1, 0)
    row = lax.broadcasted_iota(jnp.int32, g.shape, 0)
    return jnp.where(row == n - 1, 0.0, rolled), g[0:1, :]


_shift_rows.defvjp(_shift_rows_fwd, _shift_rows_bwd)


@jax.custom_vjp
def _swap_halves(x):
    return pltpu.roll(x, 64, 1)


_swap_halves.defvjp(lambda x: (_swap_halves(x), None), lambda _, g: (pltpu.roll(g, 64, 1),))


def _sigmoid(x):
    return 1.0 / (1.0 + jnp.exp(-x))


def _softplus(x):
    return jnp.maximum(x, 0.0) + jnp.log(1.0 + jnp.exp(-jnp.abs(x)))


def _rms_fn(x, g):
    return x * lax.rsqrt(jnp.mean(x * x, axis=-1, keepdims=True) + NORM_EPS) * g


def _pre_a_fn(h1, h1p, p, pp, mu_w, mu_a, mu_g, mu_r, mu_k, mu_v, w0, w1, w2, a0, a1, a2, g1, g2):
    W = RWKV_WIDTH
    h1s = _shift_rows(h1, h1p)
    ps = _shift_rows(p, pp)
    dx = h1s - h1
    xw = h1 + dx * mu_w
    xa = h1 + dx * mu_a
    xg = h1 + dx * mu_g
    dp = ps - p
    r = p[:, 0:W] + dp[:, 0:W] * mu_r
    k0 = p[:, W:2 * W] + dp[:, W:2 * W] * mu_k
    v = p[:, 2 * W:3 * W] + dp[:, 2 * W:3 * W] * mu_v
    wl = w0 + _bdot(jnp.tanh(_bdot(xw, w1)), w2)
    w_log = -_softplus(-wl) - 0.5
    lw = -jnp.exp(w_log)
    a = _sigmoid(a0 + _bdot(_bdot(xa, a1), a2))
    g = _bdot(_sigmoid(_bdot(xg, g1)), g2)
    return r, k0, v, lw, a, g


def _pre_b_fn(k0, a, k_k, k_a):
    kkr = k0 * k_k
    nrm = jnp.sqrt(jnp.sum(kkr * kkr, axis=-1, keepdims=True))
    kk = kkr / jnp.maximum(nrm, 1e-12)
    k = k0 * (1.0 + (a - 1.0) * k_a)
    return -kk, k, kk * a


def _rwkv_post_fn(y, r, k, v, g, lnx_w, lnx_b, r_k):
    mu = jnp.mean(y, axis=-1, keepdims=True)
    yc = y - mu
    var = jnp.mean(yc * yc, axis=-1, keepdims=True)
    yn = yc * lax.rsqrt(var + RWKV_GN_EPS) * lnx_w + lnx_b
    bonus = jnp.sum(r * k * r_k, axis=-1, keepdims=True) * v
    return ((yn + bonus) * g,)


def _rotary_fn(cos2, sin2, q, k):
    qs, ks = [], []
    for h in range(RET_HEADS):
        sl = slice(h * RET_HEAD_DIM, (h + 1) * RET_HEAD_DIM)
        qh, kh = q[:, sl], k[:, sl]
        qs.append(qh * cos2 + _swap_halves(qh) * sin2)
        ks.append((kh * cos2 + _swap_halves(kh) * sin2) * (RET_HEAD_DIM ** -0.5))
    return jnp.concatenate(qs, axis=1), jnp.concatenate(ks, axis=1)


def _ret_post_fn(y, gp, gn_w):
    outs = []
    for h in range(RET_HEADS):
        sl = slice(h * RET_HEAD_DIM, (h + 1) * RET_HEAD_DIM)
        yh = y[:, sl]
        mu = jnp.mean(yh, axis=-1, keepdims=True)
        yc = yh - mu
        var = jnp.mean(yc * yc, axis=-1, keepdims=True)
        outs.append(yc * lax.rsqrt(var + RET_GN_EPS) * gn_w[:, sl])
    yn = jnp.concatenate(outs, axis=1)
    return (gp * _sigmoid(gp) * yn,)


def _tok_axis(a):
    return 1 if a.ndim == 3 else 0


def _blk_spec(a, tb, rev_nb=None):
    nd, ax = a.ndim, _tok_axis(a)
    shape = a.shape[:ax] + (tb,) + a.shape[ax + 1:]

    def imap(i):
        idx = [0] * nd
        idx[ax] = i if rev_nb is None else rev_nb - 1 - i
        return tuple(idx)

    return pl.BlockSpec(shape, imap)


def _full_spec(a):
    nd = a.ndim
    return pl.BlockSpec(a.shape, lambda i: (0,) * nd)


def _tok_fwd(name, fn, toks, consts, out_tails, tb=TOK_BLOCK):
    n_in = len(toks) + len(consts)
    tn = toks[0].shape[_tok_axis(toks[0])]

    def body(*refs):
        outs = fn(*[r[...] for r in refs[:n_in]])
        for r, o in zip(refs[n_in:], outs):
            r[...] = o

    out_shape = [jax.ShapeDtypeStruct((tn,) + tuple(s) if len(s) == 1 else (s[0], tn, s[1]), F32) for s in out_tails]
    return pl.pallas_call(
        body, name=name, grid=(tn // tb,),
        in_specs=[_blk_spec(a, tb) for a in toks] + [_full_spec(c) for c in consts],
        out_specs=[_blk_spec(o, tb) for o in out_shape], out_shape=out_shape,
        compiler_params=_cparams(1))(*toks, *consts)


def _tok_bwd(name, fn, aux, toks, consts, cts, add=None, tb=TOK_BLOCK):
    n_aux, n_tok, n_c = len(aux), len(toks), len(consts)
    ct_groups = [c if isinstance(c, (tuple, list)) else (c,) for c in cts]
    ct_flat = [a for grp in ct_groups for a in grp]
    n_ct = len(ct_flat)
    n_add = 0 if add is None else 1
    tn = toks[0].shape[_tok_axis(toks[0])]

    def body(*refs):
        pos = 0
        aux_v = [r[...] for r in refs[pos:pos + n_aux]]; pos += n_aux
        tok_v = [r[...] for r in refs[pos:pos + n_tok]]; pos += n_tok
        const_v = [r[...] for r in refs[pos:pos + n_c]]; pos += n_c
        ct_refs = refs[pos:pos + n_ct]; pos += n_ct
        add_refs = refs[pos:pos + n_add]; pos += n_add
        dtok_refs = refs[pos:pos + n_tok]; pos += n_tok
        dconst_refs = refs[pos:pos + n_c]
        ct_v, q = [], 0
        for grp in ct_groups:
            s = ct_refs[q][...]
            for r in ct_refs[q + 1:q + len(grp)]:
                s = s + r[...]
            q += len(grp)
            ct_v.append(s)
        _, vjp = jax.vjp(lambda *tc: fn(*aux_v, *tc), *tok_v, *const_v)
        grads = vjp(tuple(ct_v))
        for j, r in enumerate(dtok_refs):
            gj = grads[j]
            if j == 0 and n_add:
                gj = gj + add_refs[0][...]
            r[...] = gj

        @pl.when(pl.program_id(0) == 0)
        def _():
            for r in dconst_refs:
                r[...] = jnp.zeros(r.shape, F32)

        for j, r in enumerate(dconst_refs):
            r[...] += grads[n_tok + j]

    ins = list(aux) + list(toks) + list(consts) + ct_flat + ([add] if n_add else [])
    in_specs = ([_blk_spec(a, tb) for a in aux] + [_blk_spec(a, tb) for a in toks] + [_full_spec(c) for c in consts]
                + [_blk_spec(a, tb) for a in ct_flat] + ([_blk_spec(add, tb)] if n_add else []))
    out_shape = [jax.ShapeDtypeStruct(a.shape, F32) for a in toks] + [jax.ShapeDtypeStruct(c.shape, F32) for c in consts]
    out_specs = [_blk_spec(a, tb) for a in toks] + [_full_spec(c) for c in consts]
    return pl.pallas_call(body, name=name, grid=(tn // tb,), in_specs=in_specs, out_specs=out_specs,
                          out_shape=out_shape, compiler_params=_cparams(1))(*ins)


def _mm(name, a, b, ta=False, tb=False, add=None):
    if ta:
        kd, m = a.shape
    else:
        m, kd = a.shape
    if tb:
        n, kb = b.shape
    else:
        kb, n = b.shape
    assert kd == kb, (a.shape, b.shape)
    tm, tn, tk = _tile(m, 1024), _tile(n, 512), _tile(kd, 512)
    nk = kd // tk
    has_add = add is not None
    dims = (((0 if ta else 1,), (1 if tb else 0,)), ((), ()))

    def body(*refs):
        a_ref, b_ref = refs[0], refs[1]
        o_ref, acc_ref = refs[-2], refs[-1]
        k = pl.program_id(2)

        @pl.when(k == 0)
        def _():
            acc_ref[...] = refs[2][...] if has_add else jnp.zeros(acc_ref.shape, F32)

        acc_ref[...] += lax.dot_general(a_ref[...].astype(BF16), b_ref[...].astype(BF16), dims,
                                        preferred_element_type=F32)

        @pl.when(k == nk - 1)
        def _():
            o_ref[...] = acc_ref[...]

    a_spec = pl.BlockSpec((tk, tm), lambda i, j, k: (k, i)) if ta else pl.BlockSpec((tm, tk), lambda i, j, k: (i, k))
    b_spec = pl.BlockSpec((tn, tk), lambda i, j, k: (j, k)) if tb else pl.BlockSpec((tk, tn), lambda i, j, k: (k, j))
    o_spec = pl.BlockSpec((tm, tn), lambda i, j, k: (i, j))
    ins = [a, b] + ([add] if has_add else [])
    in_specs = [a_spec, b_spec] + ([o_spec] if has_add else [])
    return pl.pallas_call(body, name=name, grid=(m // tm, n // tn, nk), in_specs=in_specs, out_specs=o_spec,
                          out_shape=jax.ShapeDtypeStruct((m, n), F32),
                          scratch_shapes=[pltpu.VMEM((tm, tn), F32)], compiler_params=_cparams(3))(*ins)


def _prev8_spec(a, tb, rev_nb=None):
    r = tb // 8
    if rev_nb is None:
        return pl.BlockSpec((8, a.shape[1]), lambda i: (jnp.maximum(i * r - 1, 0), 0))
    return pl.BlockSpec((8, a.shape[1]), lambda i: (jnp.maximum((rev_nb - 1 - i) * r - 1, 0), 0))


def _pre_a_fwd(h1, p, consts, tb=TOK_BLOCK):
    tn = h1.shape[0]

    def body(h1_ref, h1h_ref, p_ref, ph_ref, *rest):
        c_refs, o_refs = rest[:len(consts)], rest[len(consts):]
        first = pl.program_id(0) == 0
        h1p = jnp.where(first, 0.0, h1h_ref[7:8, :])
        pp = jnp.where(first, 0.0, ph_ref[7:8, :])
        outs = _pre_a_fn(h1_ref[...], h1p, p_ref[...], pp, *[c[...] for c in c_refs])
        for r, o in zip(o_refs, outs):
            r[...] = o

    out_shape = [jax.ShapeDtypeStruct((tn, RWKV_WIDTH), F32) for _ in range(6)]
    return pl.pallas_call(
        body, name="rwkv_pre_a_fwd", grid=(tn // tb,),
        in_specs=[_blk_spec(h1, tb), _prev8_spec(h1, tb), _blk_spec(p, tb), _prev8_spec(p, tb)]
        + [_full_spec(c) for c in consts],
        out_specs=[_blk_spec(o, tb) for o in out_shape], out_shape=out_shape,
        compiler_params=_cparams(1))(h1, h1, p, p, *consts)


def _pre_a_bwd(h1, p, consts, cts, tb=TOK_BLOCK):
    tn = h1.shape[0]
    nb = tn // tb
    n_c = len(consts)
    ct_groups = [c if isinstance(c, (tuple, list)) else (c,) for c in cts]
    ct_flat = [a for grp in ct_groups for a in grp]
    n_ct = len(ct_flat)

    def body(*refs):
        h1_ref, h1h_ref, p_ref, ph_ref = refs[:4]
        c_refs = refs[4:4 + n_c]
        ct_refs = refs[4 + n_c:4 + n_c + n_ct]
        dh1_ref, dp_ref = refs[4 + n_c + n_ct:6 + n_c + n_ct]
        dc_refs = refs[6 + n_c + n_ct:6 + 2 * n_c + n_ct]
        ch_ref, cp_ref = refs[-2], refs[-1]
        i = pl.program_id(0)
        first_block = i == nb - 1
        h1p = jnp.where(first_block, 0.0, h1h_ref[7:8, :])
        pp = jnp.where(first_block, 0.0, ph_ref[7:8, :])
        ct_v, q = [], 0
        for grp in ct_groups:
            s = ct_refs[q][...]
            for r in ct_refs[q + 1:q + len(grp)]:
                s = s + r[...]
            q += len(grp)
            ct_v.append(s)
        _, vjp = jax.vjp(_pre_a_fn, h1_ref[...], h1p, p_ref[...], pp, *[c[...] for c in c_refs])
        grads = vjp(tuple(ct_v))

        @pl.when(i == 0)
        def _():
            ch_ref[...] = jnp.zeros(ch_ref.shape, F32)
            cp_ref[...] = jnp.zeros(cp_ref.shape, F32)
            for r in dc_refs:
                r[...] = jnp.zeros(r.shape, F32)

        rowh = lax.broadcasted_iota(jnp.int32, (tb, h1.shape[1]), 0)
        rowp = lax.broadcasted_iota(jnp.int32, (tb, p.shape[1]), 0)
        dh1_ref[...] = grads[0] + jnp.where(rowh == tb - 1, jnp.broadcast_to(ch_ref[0:1, :], rowh.shape), 0.0)
        dp_ref[...] = grads[2] + jnp.where(rowp == tb - 1, jnp.broadcast_to(cp_ref[0:1, :], rowp.shape), 0.0)
        ch_ref[0:1, :] = grads[1]
        cp_ref[0:1, :] = grads[3]
        for j, r in enumerate(dc_refs):
            r[...] += grads[4 + j]

    ins = [h1, h1, p, p] + list(consts) + ct_flat
    in_specs = ([_blk_spec(h1, tb, nb), _prev8_spec(h1, tb, nb), _blk_spec(p, tb, nb), _prev8_spec(p, tb, nb)]
                + [_full_spec(c) for c in consts] + [_blk_spec(a, tb, nb) for a in ct_flat])
    out_shape = ([jax.ShapeDtypeStruct(h1.shape, F32), jax.ShapeDtypeStruct(p.shape, F32)]
                 + [jax.ShapeDtypeStruct(c.shape, F32) for c in consts])
    out_specs = [_blk_spec(h1, tb, nb), _blk_spec(p, tb, nb)] + [_full_spec(c) for c in consts]
    return pl.pallas_call(body, name="rwkv_pre_a_bwd", grid=(nb,), in_specs=in_specs, out_specs=out_specs,
                          out_shape=out_shape,
                          scratch_shapes=[pltpu.VMEM((8, h1.shape[1]), F32), pltpu.VMEM((8, p.shape[1]), F32)],
                          compiler_params=_cparams(1))(*ins)


def _my_index():
    return 4 * lax.axis_index("x") + 2 * lax.axis_index("y") + lax.axis_index("c")


def _peer(k):
    x, y, c = lax.axis_index("x"), lax.axis_index("y"), lax.axis_index("c")
    px = 1 - x if k & 4 else x
    py = 1 - y if k & 2 else y
    pc = 1 - c if k & 1 else c
    return (px, py, pc), 4 * px + 2 * py + pc


def _xchg_sems(n):
    return [pltpu.SemaphoreType.DMA((n * (N_DEV - 1),)), pltpu.SemaphoreType.DMA((n * (N_DEV - 1),)),
            pltpu.SemaphoreType.DMA((n,))]


def _xchg_copies(srcs, dsts, sems, scatter, incoming=False):
    send_sems, recv_sems, local_sems = sems
    me = _my_index()
    local, remote = [], []
    for i, (s, d) in enumerate(zip(srcs, dsts)):
        if not incoming:
            local.append(pltpu.make_async_copy(s.at[me] if scatter else s, d.at[me], local_sems.at[i]))
        for k in range(1, N_DEV):
            peer, plin = _peer(k)
            j = i * (N_DEV - 1) + k - 1
            s_slot, d_slot = (me, plin) if incoming else (plin, me)
            remote.append(pltpu.make_async_remote_copy(
                src_ref=s.at[s_slot] if scatter else s, dst_ref=d.at[d_slot], send_sem=send_sems.at[j],
                recv_sem=recv_sems.at[j], device_id=peer, device_id_type=pl.DeviceIdType.MESH))
    return local, remote


def _xchg_start(srcs, dsts, sems, scatter):
    local, out = _xchg_copies(srcs, dsts, sems, scatter)
    for cp in local + out:
        cp.start()


def _xchg_wait(srcs, dsts, sems, scatter):
    for cp in _xchg_copies(srcs, dsts, sems, scatter, incoming=True)[1]:
        cp.wait_recv()
    local, out = _xchg_copies(srcs, dsts, sems, scatter)
    for cp in out:
        cp.wait_send()
    for cp in local:
        cp.wait()


def _xchg_out_shapes(srcs, scatter):
    return [jax.ShapeDtypeStruct(s.shape if scatter else (N_DEV,) + s.shape, s.dtype) for s in srcs]


_ANY = pl.BlockSpec(memory_space=pl.ANY)


def _exchange(name, srcs, scatter):
    n = len(srcs)

    def body(*refs):
        s, d, sems = refs[:n], refs[n:2 * n], refs[2 * n:]
        _xchg_start(s, d, sems, scatter)
        _xchg_wait(s, d, sems, scatter)

    return pl.pallas_call(body, name=name, in_specs=[_ANY] * n, out_specs=[_ANY] * n,
                          out_shape=_xchg_out_shapes(srcs, scatter), scratch_shapes=_xchg_sems(n))(*srcs)


_HI = lax.Precision.HIGHEST
_MM_DIMS = {'nn': (((1,), (0,)), ((), ())), 'nt': (((1,), (1,)), ((), ())), 'tn': (((0,), (0,)), ((), ()))}


def _cmm_raw(x, y, kind):
    return lax.dot_general(x.astype(BF16), y.astype(BF16), _MM_DIMS[kind], preferred_element_type=F32)


@functools.partial(jax.custom_vjp, nondiff_argnums=(2,))
def _cmm(x, y, kind):
    return _cmm_raw(x, y, kind)


def _cmm_fwd(x, y, kind):
    return _cmm_raw(x, y, kind), (x, y)


def _cmm_bwd(kind, res, g):
    x, y = res
    if kind == 'nn':
        return _cmm_raw(g, y, 'nt'), _cmm_raw(x, g, 'tn')
    if kind == 'nt':
        return _cmm_raw(g, y, 'nn'), _cmm_raw(g, x, 'tn')
    return _cmm_raw(y, g, 'nt'), _cmm_raw(x, g, 'nn')


_cmm.defvjp(_cmm_fwd, _cmm_bwd)


def _chunk_fn(S0, r, lw, k, v, a, b):
    C = r.shape[0]
    ii = lax.broadcasted_iota(jnp.int32, (C, C), 0)
    jj = lax.broadcasted_iota(jnp.int32, (C, C), 1)
    incl, strict = ii >= jj, ii > jj
    eye = (ii == jj).astype(F32)
    cum = jnp.dot(incl.astype(F32), lw, precision=_HI)
    e_inv = jnp.exp(-cum)
    At = a * jnp.exp(cum - lw)
    Rt = r * jnp.exp(cum)
    Kh, Bh = k * e_inv, b * e_inv
    Mab = jnp.where(strict, _cmm(At, Bh, 'nt'), 0.0)
    Mak = jnp.where(strict, _cmm(At, Kh, 'nt'), 0.0)
    Mrk = jnp.where(incl, _cmm(Rt, Kh, 'nt'), 0.0)
    Mrb = jnp.where(incl, _cmm(Rt, Bh, 'nt'), 0.0)
    rhs = _cmm(At, S0, 'nt') + _cmm(Mak, v, 'nn')
    P = Mab
    Tm = eye + P
    n = 1
    while 2 * n < C:
        P = jnp.dot(P, P, precision=_HI)
        Tm = jnp.dot(Tm, eye + P, precision=_HI)
        n *= 2
    U = jnp.dot(Tm, rhs, precision=_HI)
    Y = _cmm(Rt, S0, 'nt') + _cmm(Mrk, v, 'nn') + _cmm(Mrb, U, 'nn')
    gC = jnp.exp(jnp.sum(lw, axis=0, keepdims=True))
    SC = S0 * gC + _cmm(v, Kh * gC, 'tn') + _cmm(U, Bh * gC, 'tn')
    return Y, SC


def _cscan_fwd(r, lw, k, v, a, b, xs):
    n_x = len(xs)
    H, tn, Dh = r.shape
    nc = tn // SCAN_CHUNK

    def body(r_ref, lw_ref, k_ref, v_ref, a_ref, b_ref, *rest):
        x_src, (y_ref, ck_ref) = rest[:n_x], rest[n_x:n_x + 2]
        x_dst, s_ref, sems = rest[n_x + 2:2 * n_x + 2], rest[2 * n_x + 2], rest[2 * n_x + 3:]

        @pl.when(pl.program_id(0) == 0)
        def _():
            s_ref[...] = jnp.zeros(s_ref.shape, F32)
            _xchg_start(x_src, x_dst, sems, False)

        ck_ref[0] = s_ref[...]
        for h in range(H):
            y, sc = _chunk_fn(s_ref[h], r_ref[h], lw_ref[h], k_ref[h], v_ref[h], a_ref[h], b_ref[h])
            y_ref[h] = y
            s_ref[h] = sc

        @pl.when(pl.program_id(0) == nc - 1)
        def _():
            _xchg_wait(x_src, x_dst, sems, False)

    hm = pl.BlockSpec((H, SCAN_CHUNK, Dh), lambda c: (0, c, 0))
    res = pl.pallas_call(
        body, name="rwkv_scan_fwd", grid=(nc,), in_specs=[hm] * 6 + [_ANY] * n_x,
        out_specs=[hm, pl.BlockSpec((1, H, Dh, Dh), lambda c: (c, 0, 0, 0))] + [_ANY] * n_x,
        out_shape=[jax.ShapeDtypeStruct((H, tn, Dh), F32), jax.ShapeDtypeStruct((nc, H, Dh, Dh), F32)]
        + _xchg_out_shapes(xs, False),
        scratch_shapes=[pltpu.VMEM((H, Dh, Dh), F32)] + _xchg_sems(n_x),
        compiler_params=_cparams(1))(r, lw, k, v, a, b, *xs)
    return res[0], res[1], res[2:]


def _cscan_bwd(r, lw, k, v, a, b, dy, ck, xs):
    n_x = len(xs)
    H, tn, Dh = r.shape
    nc = tn // SCAN_CHUNK

    def body(r_ref, lw_ref, k_ref, v_ref, a_ref, b_ref, dy_ref, ck_ref, *rest):
        x_src = rest[:n_x]
        d_refs = rest[n_x:n_x + 6]
        x_dst = rest[n_x + 6:2 * n_x + 6]
        g_ref = rest[2 * n_x + 6]
        sems = rest[2 * n_x + 7:]

        @pl.when(pl.program_id(0) == 0)
        def _():
            g_ref[...] = jnp.zeros(g_ref.shape, F32)
            _xchg_start(x_src, x_dst, sems, True)

        for h in range(H):
            _, vjp = jax.vjp(_chunk_fn, ck_ref[0, h], r_ref[h], lw_ref[h], k_ref[h], v_ref[h], a_ref[h], b_ref[h])
            grads = vjp((dy_ref[h], g_ref[h]))
            g_ref[h] = grads[0]
            for d_ref, gz in zip(d_refs, grads[1:]):
                d_ref[h] = gz

        @pl.when(pl.program_id(0) == nc - 1)
        def _():
            _xchg_wait(x_src, x_dst, sems, True)

    hm = pl.BlockSpec((H, SCAN_CHUNK, Dh), lambda c: (0, nc - 1 - c, 0))
    hshape = jax.ShapeDtypeStruct((H, tn, Dh), F32)
    res = pl.pallas_call(
        body, name="rwkv_scan_bwd", grid=(nc,),
        in_specs=[hm] * 7 + [pl.BlockSpec((1, H, Dh, Dh), lambda c: (nc - 1 - c, 0, 0, 0))] + [_ANY] * n_x,
        out_specs=[hm] * 6 + [_ANY] * n_x, out_shape=[hshape] * 6 + _xchg_out_shapes(xs, True),
        scratch_shapes=[pltpu.VMEM((H, Dh, Dh), F32)] + _xchg_sems(n_x),
        compiler_params=_cparams(1))(r, lw, k, v, a, b, dy, ck, *xs)
    return res[:6], res[6:]


def _decay_mask(lg, i, j, blk):
    rows = lax.broadcasted_iota(jnp.int32, (blk, blk), 0)
    cols = lax.broadcasted_iota(jnp.int32, (blk, blk), 1)
    dd = (rows - cols + (i - j) * blk).astype(F32)
    return jnp.where(dd >= 0.0, jnp.exp(lg * jnp.maximum(dd, 0.0)), 0.0)


_NT = (((1,), (1,)), ((), ()))
_TN = (((0,), (0,)), ((), ()))


def _ret_attn_fwd(lg, q, k, v, blk=ATT_BLOCK):
    tn = q.shape[0]
    Dh = RET_HEAD_DIM

    def body(lg_ref, q_ref, k_ref, v_ref, o_ref):
        i = pl.program_id(1)
        lgv = lg_ref[0][:, 0:1]
        qb = q_ref[...].astype(BF16)

        def jb(j, acc):
            ks = pl.ds(pl.multiple_of(j * blk, blk), blk)
            s = lax.dot_general(qb, k_ref[ks, :].astype(BF16), _NT, preferred_element_type=F32)
            s = s * _decay_mask(lgv, i, j, blk)
            return acc + jnp.dot(s.astype(BF16), v_ref[ks, :].astype(BF16), preferred_element_type=F32)

        o_ref[...] = lax.fori_loop(0, i + 1, jb, jnp.zeros((blk, Dh), F32))

    full = pl.BlockSpec((tn, Dh), lambda h, i: (0, h))
    qs = pl.BlockSpec((blk, Dh), lambda h, i: (i, h))
    return pl.pallas_call(
        body, name="ret_attn_fwd", grid=(RET_HEADS, tn // blk),
        in_specs=[pl.BlockSpec((1, 1, 128), lambda h, i: (h, 0, 0)), qs, full, full],
        out_specs=qs, out_shape=jax.ShapeDtypeStruct(q.shape, F32), compiler_params=_cparams(2))(lg, q, k, v)


def _ret_attn_bwd(lg, q, k, v, do, blk=ATT_BLOCK):
    tn = q.shape[0]
    nb = tn // blk
    Dh = RET_HEAD_DIM

    def body(lg_ref, q_ref, k_ref, v_ref, do_ref, dq_ref, dk_ref, dv_ref):
        lgv = lg_ref[0][:, 0:1]
        dk_ref[...] = jnp.zeros(dk_ref.shape, F32)
        dv_ref[...] = jnp.zeros(dv_ref.shape, F32)

        def ib(i, carry):
            qs = pl.ds(pl.multiple_of(i * blk, blk), blk)
            qb = q_ref[qs, :].astype(BF16)
            dob = do_ref[qs, :].astype(BF16)

            def jb(j, dq):
                ks = pl.ds(pl.multiple_of(j * blk, blk), blk)
                kb = k_ref[ks, :].astype(BF16)
                vb = v_ref[ks, :].astype(BF16)
                dm = _decay_mask(lgv, i, j, blk)
                s = lax.dot_general(qb, kb, _NT, preferred_element_type=F32) * dm
                ds = lax.dot_general(dob, vb, _NT, preferred_element_type=F32) * dm
                sb, dsb = s.astype(BF16), ds.astype(BF16)
                dv_ref[ks, :] += lax.dot_general(sb, dob, _TN, preferred_element_type=F32)
                dk_ref[ks, :] += lax.dot_general(dsb, qb, _TN, preferred_element_type=F32)
                return dq + jnp.dot(dsb, kb, preferred_element_type=F32)

            dq_ref[qs, :] = lax.fori_loop(0, i + 1, jb, jnp.zeros((blk, Dh), F32))
            return carry

        lax.fori_loop(0, nb, ib, 0)

    full = pl.BlockSpec((tn, Dh), lambda h: (0, h))
    sh = jax.ShapeDtypeStruct(q.shape, F32)
    return pl.pallas_call(
        body, name="ret_attn_bwd", grid=(RET_HEADS,),
        in_specs=[pl.BlockSpec((1, 1, 128), lambda h: (h, 0, 0)), full, full, full, full],
        out_specs=[full, full, full], out_shape=[sh, sh, sh], compiler_params=_cparams(1))(lg, q, k, v, do)


def _next8_spec(a, tb):
    r = tb // 8
    last = a.shape[0] // 8 - 1
    return pl.BlockSpec((8, a.shape[1]), lambda i: (jnp.minimum((i + 1) * r, last), 0))


def _conv_taps(g_ext, cw_ref, cb_ref):
    return (cw_ref[2:3, :] * g_ext + cw_ref[1:2, :] * pltpu.roll(g_ext, 1, 0)
            + cw_ref[0:1, :] * pltpu.roll(g_ext, 2, 0) + cb_ref[...])


def _glu_fwd(gate, up, cw, cb, tb=TOK_BLOCK):
    tn = gate.shape[0]

    def body(g_ref, gh_ref, u_ref, cw_ref, cb_ref, o_ref):
        halo = jnp.where(pl.program_id(0) == 0, 0.0, gh_ref[...])
        g_ext = jnp.concatenate([halo, g_ref[...]], axis=0)
        gc = _conv_taps(g_ext, cw_ref, cb_ref)[8:, :]
        o_ref[...] = gc * _sigmoid(gc) * u_ref[...]

    return pl.pallas_call(
        body, name="glu_fwd", grid=(tn // tb,),
        in_specs=[_blk_spec(gate, tb), _prev8_spec(gate, tb), _blk_spec(up, tb), _full_spec(cw), _full_spec(cb)],
        out_specs=_blk_spec(gate, tb), out_shape=jax.ShapeDtypeStruct(gate.shape, F32),
        compiler_params=_cparams(1))(gate, gate, up, cw, cb)


def _glu_bwd(gate, up, dact, cw, cb, tb=TOK_BLOCK):
    tn = gate.shape[0]
    nb = tn // tb

    def body(g_ref, gp_ref, gn_ref, u_ref, un_ref, d_ref, dn_ref, cw_ref, cb_ref, dg_ref, du_ref, dcw_ref, dcb_ref):
        i = pl.program_id(0)
        gprev = jnp.where(i == 0, 0.0, gp_ref[...])
        dnext = jnp.where(i == nb - 1, 0.0, dn_ref[...])
        g_ext = jnp.concatenate([gprev, g_ref[...], gn_ref[...]], axis=0)
        gc = _conv_taps(g_ext, cw_ref, cb_ref)[8:, :]
        u_e = jnp.concatenate([u_ref[...], un_ref[...]], axis=0)
        d_e = jnp.concatenate([d_ref[...], dnext], axis=0)
        s = _sigmoid(gc)
        dgc = d_e * u_e * (s * (1.0 + gc * (1.0 - s)))
        du_ref[...] = d_ref[...] * (gc * s)[:tb, :]
        n_e = tb + 8
        dg_ref[...] = (cw_ref[2:3, :] * dgc + cw_ref[1:2, :] * pltpu.roll(dgc, n_e - 1, 0)
                       + cw_ref[0:1, :] * pltpu.roll(dgc, n_e - 2, 0))[:tb, :]

        @pl.when(i == 0)
        def _():
            dcw_ref[...] = jnp.zeros(dcw_ref.shape, F32)
            dcb_ref[...] = jnp.zeros(dcb_ref.shape, F32)

        dgc_b = dgc[:tb, :]
        g0 = g_ext[8:8 + tb, :]
        g1 = pltpu.roll(g_ext, 1, 0)[8:8 + tb, :]
        g2 = pltpu.roll(g_ext, 2, 0)[8:8 + tb, :]
        dcw_ref[2:3, :] += jnp.sum(dgc_b * g0, axis=0, keepdims=True)
        dcw_ref[1:2, :] += jnp.sum(dgc_b * g1, axis=0, keepdims=True)
        dcw_ref[0:1, :] += jnp.sum(dgc_b * g2, axis=0, keepdims=True)
        dcb_ref[...] += jnp.sum(dgc_b, axis=0, keepdims=True)

    sh = jax.ShapeDtypeStruct(gate.shape, F32)
    return pl.pallas_call(
        body, name="glu_bwd", grid=(nb,),
        in_specs=[_blk_spec(gate, tb), _prev8_spec(gate, tb), _next8_spec(gate, tb), _blk_spec(up, tb),
                  _next8_spec(up, tb), _blk_spec(dact, tb), _next8_spec(dact, tb), _full_spec(cw), _full_spec(cb)],
        out_specs=[_blk_spec(gate, tb), _blk_spec(gate, tb), _full_spec(cw), _full_spec(cb)],
        out_shape=[sh, sh, jax.ShapeDtypeStruct(cw.shape, F32), jax.ShapeDtypeStruct(cb.shape, F32)],
        compiler_params=_cparams(1))(gate, gate, gate, up, up, dact, dact, cw, cb)


def _final_loss(x2, tgt, g, tb=TOK_BLOCK):
    tn, dm = x2.shape

    def body(x_ref, t_ref, g_ref, l_ref, dx_ref, dg_ref):
        y, vjp = jax.vjp(_rms_fn, x_ref[...], g_ref[...])
        err = y - t_ref[...]
        dx, dg = vjp(err * (1.0 / dm))

        @pl.when(pl.program_id(0) == 0)
        def _():
            l_ref[...] = jnp.zeros(l_ref.shape, F32)
            dg_ref[...] = jnp.zeros(dg_ref.shape, F32)

        part = 0.5 * jnp.sum(jnp.mean(err * err, axis=-1, keepdims=True), axis=0, keepdims=True)
        l_ref[...] += jnp.broadcast_to(part, l_ref.shape)
        dx_ref[...] = dx
        dg_ref[...] += dg

    return pl.pallas_call(
        body, name="final_loss", grid=(tn // tb,),
        in_specs=[_blk_spec(x2, tb), _blk_spec(tgt, tb), _full_spec(g)],
        out_specs=[pl.BlockSpec((8, 128), lambda i: (0, 0)), _blk_spec(x2, tb), _full_spec(g)],
        out_shape=[jax.ShapeDtypeStruct((8, 128), F32), jax.ShapeDtypeStruct(x2.shape, F32),
                   jax.ShapeDtypeStruct(g.shape, F32)],
        compiler_params=_cparams(1))(x2, tgt, g)


def _pad_cols(w, n):
    return jnp.pad(w, ((0, 0), (0, n - w.shape[1])))


def _pad_rows(w, n):
    return jnp.pad(w, ((0, n - w.shape[0]), (0, 0)))


def _to_heads(z):
    return jnp.swapaxes(z.reshape(z.shape[0], RWKV_HEADS, RWKV_HEAD_DIM), 0, 1)


def _to_flat(z):
    return jnp.swapaxes(z, 0, 1).reshape(z.shape[1], RWKV_WIDTH)


def _head_param(p):
    return p.reshape(RWKV_HEADS, 1, RWKV_HEAD_DIM)


def _local_step(x, tgt, W, late):
    tn = x.shape[0]
    Wd = RWKV_WIDTH
    row = lambda z: z.reshape(1, -1)
    g_mix, g_ffn, g_fin = row(W['norm_mix_g']), row(W['norm_ffn_g']), row(W['norm_final_g'])

    (h1,) = _tok_fwd("norm_mix_fwd", lambda a, g: (_rms_fn(a, g),), [x], [g_mix], [(D_MODEL,)])
    proj = _mm("proj_fwd", h1, W['w_in_t'], tb=True)
    p_rkv = proj[:, :3 * Wd]
    pre_consts = [row(W['rwkv_mu_w']), row(W['rwkv_mu_a']), row(W['rwkv_mu_g']), row(W['rwkv_mu_r']),
                  row(W['rwkv_mu_k']), row(W['rwkv_mu_v']), row(W['rwkv_w0']),
                  _pad_cols(W['rwkv_w1'], LORA_PAD), _pad_rows(W['rwkv_w2'], LORA_PAD), row(W['rwkv_a0']),
                  _pad_cols(W['rwkv_a1'], LORA_PAD), _pad_rows(W['rwkv_a2'], LORA_PAD),
                  W['rwkv_g1'], W['rwkv_g2']]
    r, k0, v, lw, a, g = _pre_a_fwd(h1, p_rkv, pre_consts)
    k_k, k_a = _head_param(W['rwkv_k_k']), _head_param(W['rwkv_k_a'])
    k0h, ah = _to_heads(k0), _to_heads(a)
    nkk, kh, bh = _tok_fwd("rwkv_pre_b_fwd", _pre_b_fn, [k0h, ah], [k_k, k_a], [(RWKV_HEADS, RWKV_HEAD_DIM)] * 3)
    rh, lwh, vh, gh = _to_heads(r), _to_heads(lw), _to_heads(v), _to_heads(g)
    yh, ck, gathered = _cscan_fwd(rh, lwh, kh, vh, nkk, bh, late)
    w_out, w_gate_t, w_up_t, w_down = [g_.reshape(-1, D_MODEL) for g_ in gathered]
    post_consts = [_head_param(W['rwkv_lnx_w']), _head_param(W['rwkv_lnx_b']), _head_param(W['rwkv_r_k'])]
    (y_rwkv_h,) = _tok_fwd("rwkv_post_fwd", _rwkv_post_fn, [yh, rh, kh, vh, gh], post_consts,
                           [(RWKV_HEADS, RWKV_HEAD_DIM)])
    y_rwkv = _to_flat(y_rwkv_h)

    pos = jnp.arange(tn, dtype=F32)
    half = RET_HEAD_DIM // 2
    inv_freq = ROPE_BASE ** (-jnp.arange(half, dtype=F32) / half)
    ang = pos[:, None] * inv_freq[None, :]
    cos2 = jnp.concatenate([jnp.cos(ang), jnp.cos(ang)], axis=1)
    sin2 = jnp.concatenate([-jnp.sin(ang), jnp.sin(ang)], axis=1)
    lg = jnp.log(1.0 - 2.0 ** (-5.0 - jnp.arange(RET_HEADS, dtype=F32)))
    lg = jnp.broadcast_to(lg[:, None, None], (RET_HEADS, 1, 128))
    q_p, k_p = proj[:, 3 * Wd:4 * Wd], proj[:, 4 * Wd:5 * Wd]
    v_ret, g_ret = proj[:, 5 * Wd:6 * Wd], proj[:, 6 * Wd:7 * Wd]
    q_rot, k_rot = _tok_fwd("ret_rotary_fwd", _rotary_fn, [cos2, sin2, q_p, k_p], [], [(RET_WIDTH,)] * 2)
    y_ret_raw = _ret_attn_fwd(lg, q_rot, k_rot, v_ret)
    gn_w = row(W['ret_gn_w'])
    (y_ret,) = _tok_fwd("ret_post_fwd", _ret_post_fn, [y_ret_raw, g_ret], [gn_w], [(RET_WIDTH,)])

    ycat = jnp.concatenate([y_rwkv, y_ret], axis=1)
    x1 = _mm("out_proj_fwd", ycat, w_out, add=x)
    (h2,) = _tok_fwd("norm_ffn_fwd", lambda a_, g_: (_rms_fn(a_, g_),), [x1], [g_ffn], [(D_MODEL,)])
    gate = _mm("ffn_gate_fwd", h2, w_gate_t, tb=True)
    up = _mm("ffn_up_fwd", h2, w_up_t, tb=True)
    cw = W['ffn_conv_w']
    cb = row(W['ffn_conv_b'])
    act = _glu_fwd(gate, up, cw, cb)
    x2 = _mm("ffn_down_fwd", act, w_down, add=x1)
    loss8, dx2, dg_fin = _final_loss(x2, tgt, g_fin)

    G = {'norm_final_g': dg_fin}
    dact = _mm("ffn_down_dx", dx2, w_down, tb=True)
    d_down = _mm("ffn_down_dw", act, dx2, ta=True)
    dgate, dup, dcw, dcb = _glu_bwd(gate, up, dact, cw, cb)
    G['ffn_conv_w'], G['ffn_conv_b'] = dcw, dcb
    dh2 = _mm("ffn_gate_dx", dgate, w_gate_t)
    dh2 = _mm("ffn_up_dx", dup, w_up_t, add=dh2)
    d_gate_t = _mm("ffn_gate_dw", dgate, h2, ta=True)
    d_up_t = _mm("ffn_up_dw", dup, h2, ta=True)
    dx1, G['norm_ffn_g'] = _tok_bwd("norm_ffn_bwd", lambda a_, g_: (_rms_fn(a_, g_),), [], [x1], [g_ffn], [dh2], add=dx2)
    dycat = _mm("out_proj_dx", dx1, w_out, tb=True)
    d_out = _mm("out_proj_dw", ycat, dx1, ta=True)
    late_grads = [z.reshape(N_DEV, -1, D_MODEL) for z in (d_out, d_gate_t, d_up_t, d_down)]
    dy_rwkv, dy_ret = dycat[:, :Wd], dycat[:, Wd:]

    dyr_raw, dg_ret, G['ret_gn_w'] = _tok_bwd("ret_post_bwd", _ret_post_fn, [], [y_ret_raw, g_ret], [gn_w], [dy_ret])
    dq_rot, dk_rot, dv_ret = _ret_attn_bwd(lg, q_rot, k_rot, v_ret, dyr_raw)
    dq_p, dk_p = _tok_bwd("ret_rotary_bwd", _rotary_fn, [cos2, sin2], [q_p, k_p], [], [dq_rot, dk_rot])

    dyh, drh1, dkh1, dvh1, dgh, G['rwkv_lnx_w'], G['rwkv_lnx_b'], G['rwkv_r_k'] = _tok_bwd(
        "rwkv_post_bwd", _rwkv_post_fn, [], [yh, rh, kh, vh, gh], post_consts, [_to_heads(dy_rwkv)])
    (drh2, dlwh, dkh2, dvh2, dnkk, dbh), late_parts = _cscan_bwd(rh, lwh, kh, vh, nkk, bh, dyh, ck, late_grads)
    dk0h, dah, G['rwkv_k_k'], G['rwkv_k_a'] = _tok_bwd(
        "rwkv_pre_b_bwd", _pre_b_fn, [], [k0h, ah], [k_k, k_a], [dnkk, (dkh1, dkh2), dbh])
    pre_cts = [(_to_flat(drh1), _to_flat(drh2)), _to_flat(dk0h), (_to_flat(dvh1), _to_flat(dvh2)),
               _to_flat(dlwh), _to_flat(dah), _to_flat(dgh)]
    pre_out = _pre_a_bwd(h1, p_rkv, pre_consts, pre_cts)
    dh1_a, dp_rkv = pre_out[0], pre_out[1]
    (G['rwkv_mu_w'], G['rwkv_mu_a'], G['rwkv_mu_g'], G['rwkv_mu_r'], G['rwkv_mu_k'], G['rwkv_mu_v'], G['rwkv_w0'],
     dw1, dw2, G['rwkv_a0'], da1, da2, G['rwkv_g1'], G['rwkv_g2']) = pre_out[2:]
    G['rwkv_w1'], G['rwkv_w2'] = dw1[:, :64], dw2[:64, :]
    G['rwkv_a1'], G['rwkv_a2'] = da1[:, :64], da2[:64, :]

    dproj = jnp.concatenate([dp_rkv, dq_p, dk_p, dv_ret, dg_ret], axis=1)
    dh1 = _mm("proj_dx", dproj, W['w_in_t'], add=dh1_a)
    G['w_in_t'] = _mm("proj_dw", dproj, h1, ta=True)
    dx, G['norm_mix_g'] = _tok_bwd("norm_mix_bwd", lambda a_, g_: (_rms_fn(a_, g_),), [], [x], [g_mix], [dh1], add=dx1)
    return loss8[0, 0], dx, G, late_parts


def _adamw(name, parts, w, m, v):
    rows, cols = w.shape
    tb = max(t for t in range(8, 65, 8) if rows % t == 0) if rows > 64 else rows
    c1 = 1.0 - ADAM_B1 ** ADAM_STEP
    c2 = 1.0 - ADAM_B2 ** ADAM_STEP

    def body(p_ref, w_ref, m_ref, v_ref, g_ref, d_ref, nm_ref, nv_ref):
        g = p_ref[0]
        for d in range(1, N_DEV):
            g = g + p_ref[d]
        mn = ADAM_B1 * m_ref[...] + (1.0 - ADAM_B1) * g
        vn = ADAM_B2 * v_ref[...] + (1.0 - ADAM_B2) * (g * g)
        m_hat = mn / c1
        v_hat = vn / c2
        g_ref[...] = g
        d_ref[...] = -ADAM_LR * (m_hat / (jnp.sqrt(v_hat) + ADAM_EPS) + ADAM_WD * w_ref[...])
        nm_ref[...] = mn
        nv_ref[...] = vn

    spec = pl.BlockSpec((tb, cols), lambda i: (i, 0))
    sh = jax.ShapeDtypeStruct((rows, cols), F32)
    return pl.pallas_call(
        body, name=name, grid=(rows // tb,),
        in_specs=[pl.BlockSpec((N_DEV, tb, cols), lambda i: (0, i, 0)), spec, spec, spec],
        out_specs=[spec] * 4, out_shape=[sh] * 4, compiler_params=_cparams(1))(parts, w, m, v)


def _local_shape(name):
    gs, ax = SHARDED[name]
    ls = list(gs)
    ls[ax] //= N_DEV
    return tuple(ls)


def _seg(flat, seg):
    n = flat.shape[-1]
    pad = _round_up(n, seg) - n
    if pad:
        flat = jnp.pad(flat, [(0, 0)] * (flat.ndim - 1) + [(0, pad)])
    return flat


def _split3(w):
    hi = w.astype(BF16)
    r1 = w - hi.astype(F32)
    mid = r1.astype(BF16)
    lo = (r1 - mid.astype(F32)).astype(BF16)
    return hi, mid, lo


def _pack_small_shards(shards):
    pieces = []
    for name in SMALL_NAMES:
        flat = shards[name].reshape(-1)
        if name == 'ffn_conv_w':
            pieces += [_seg(p, BF16_SEG) for p in _split3(flat)]
        else:
            pieces.append(flat.astype(BF16))
    return jnp.concatenate(pieces).reshape(-1, 128)


def _unpack_small(gathered):
    flat = gathered.reshape(N_DEV, -1)
    out, off = {}, 0
    for name in SMALL_NAMES:
        gs, ax = SHARDED[name]
        ls = _local_shape(name)
        n = int(np.prod(ls))
        if name == 'ffn_conv_w':
            nseg = _round_up(n, BF16_SEG)
            hi, mid, lo = (flat[:, off + j * nseg: off + j * nseg + n].astype(F32) for j in range(3))
            sh = ((hi + mid) + lo).reshape(N_DEV, 3, -1)
            out[name] = jnp.swapaxes(sh, 0, 1).reshape(3, D_FF)
            off += 3 * nseg
        else:
            sh = flat[:, off:off + n].reshape((N_DEV,) + ls[1:])
            out[name] = sh.reshape(gs[1:]) if ax == 1 else jnp.swapaxes(sh, 0, 1).reshape(gs[1:])
            off += n
    return out


def _small_pieces(sharded, repl):
    return [sharded[n].reshape(-1) for n in SMALL_NAMES] + [repl[n].reshape(-1) for n in REPL_NAMES]


def _pack_small_local(d):
    flat = jnp.concatenate(_small_pieces(d, d))
    return _seg(flat, F32_SEG).reshape(-1, 128)


def _pack_small_grads(G):
    pieces = []
    for name in SMALL_NAMES:
        gs, ax = SHARDED[name]
        g = G[name]
        if name == 'ffn_conv_w':
            sh = jnp.swapaxes(g.reshape(3, N_DEV, -1), 0, 1)
        elif ax == 1:
            sh = g
        else:
            sh = jnp.swapaxes(g.reshape(g.shape[0], N_DEV, -1), 0, 1)
        pieces.append(sh.reshape(N_DEV, -1))
    rep = jnp.concatenate([G[n].reshape(-1) for n in REPL_NAMES])
    pieces.append(jnp.broadcast_to(rep[None, :], (N_DEV, rep.shape[0])))
    flat = _seg(jnp.concatenate(pieces, axis=1), F32_SEG)
    return flat.reshape(N_DEV, -1, 128)


def _unpack_small_local(packed, local_shapes):
    flat = packed.reshape(-1)
    out, off = {}, 0
    for name in SMALL_NAMES + REPL_NAMES:
        n = int(np.prod(local_shapes[name]))
        out[name] = flat[off:off + n].reshape(local_shapes[name])
        off += n
    return out


def kernel(x, *rest):
    nw = len(WEIGHT_NAMES)
    assert len(rest) == 3 * nw + 1
    weights = dict(zip(WEIGHT_NAMES, rest[:nw]))
    loss_target = rest[nw]
    moms = dict(zip(WEIGHT_NAMES, rest[nw + 1:2 * nw + 1]))
    vars_ = dict(zip(WEIGHT_NAMES, rest[2 * nw + 1:]))
    local_shapes = {n: weights[n].shape for n in WEIGHT_NAMES}

    def native2d(name, a):
        a2 = a.reshape(a.shape[-2], a.shape[-1])
        return a2.T if name in BIG_T else a2

    def from2d(name, a2):
        return (a2.T if name in BIG_T else a2).reshape(local_shapes[name])

    big_w = {n: native2d(n, weights[n]) for n in BIG_NAMES}
    w_in_t_sh = big_w['w_in'].astype(BF16)
    late = [big_w[n].astype(BF16) for n in LATE_NAMES]
    small_sh = _pack_small_shards({n: weights[n] for n in SMALL_NAMES})
    w_in_g, small_g = _exchange("weights_all_gather", [w_in_t_sh, small_sh], False)
    W = _unpack_small(small_g)
    W['w_in_t'] = w_in_g.reshape(-1, D_MODEL)
    for n in REPL_NAMES:
        W[n] = weights[n][0] if n != 'norm_final_g' else weights[n]

    loss, dx, G, late_parts = _local_step(x[0], loss_target[0], W, late)

    w_in_parts, small_parts = _exchange(
        "grads_all_to_all", [G['w_in_t'].reshape(N_DEV, -1, D_MODEL), _pack_small_grads(G)], True)
    results = {}
    for n, parts in zip(['w_in'] + LATE_NAMES, [w_in_parts] + list(late_parts)):
        res = _adamw("adamw_" + n, parts, big_w[n], native2d(n, moms[n]), native2d(n, vars_[n]))
        results[n] = [from2d(n, r) for r in res]
    small_res = _adamw("adamw_small", small_parts, _pack_small_local(weights), _pack_small_local(moms),
                       _pack_small_local(vars_))
    small_out = [_unpack_small_local(p, local_shapes) for p in small_res]

    loss = lax.psum(loss, ("x", "y", "c"))
    outs = [loss, dx[None]]
    for j in range(4):
        outs += [results[n][j] if n in results else small_out[j][n] for n in WEIGHT_NAMES]
    return tuple(outs)
```

```python
import functools
import math

import numpy as np
import jax
import jax.numpy as jnp
from jax import lax
from jax.experimental import pallas as pl
from jax.experimental.pallas import tpu as pltpu

F32 = jnp.float32
BF16 = jnp.bfloat16

N_DEV = 8
D_MODEL = 1024
RWKV_HEADS = 8
RWKV_HEAD_DIM = 64
RWKV_WIDTH = 512
RET_HEADS = 4
RET_HEAD_DIM = 128
RET_WIDTH = 512
LORA_PAD = 128
D_FF = 2816
NORM_EPS = 1e-6
RWKV_GN_EPS = 64e-5
RET_GN_EPS = 1e-5
ROPE_BASE = 10000.0
ADAM_LR, ADAM_B1, ADAM_B2, ADAM_EPS, ADAM_WD, ADAM_STEP = 0.001, 0.9, 0.999, 1e-08, 0.01, 10

VMEM_LIMIT = 56 * 1024 * 1024
TOK_BLOCK = 256
SCAN_CHUNK = 64
ATT_BLOCK = 256
BF16_SEG = 2048
F32_SEG = 1024

WEIGHT_NAMES = ['norm_mix_g', 'w_in', 'rwkv_mu_r', 'rwkv_mu_k', 'rwkv_mu_v', 'rwkv_mu_w', 'rwkv_mu_a',
                'rwkv_mu_g', 'rwkv_w0', 'rwkv_w1', 'rwkv_w2', 'rwkv_a0', 'rwkv_a1', 'rwkv_a2', 'rwkv_g1',
                'rwkv_g2', 'rwkv_k_k', 'rwkv_k_a', 'rwkv_r_k', 'rwkv_lnx_w', 'rwkv_lnx_b', 'ret_gn_w',
                'w_out', 'norm_ffn_g', 'ffn_w_gate', 'ffn_w_up', 'ffn_conv_w', 'ffn_conv_b', 'ffn_w_down',
                'norm_final_g']
SHARDED = {
    'w_in': ((1, 1024, 3584), 2), 'rwkv_w1': ((1, 1024, 64), 1), 'rwkv_w2': ((1, 64, 512), 2),
    'rwkv_a1': ((1, 1024, 64), 1), 'rwkv_a2': ((1, 64, 512), 2), 'rwkv_g1': ((1, 1024, 128), 1),
    'rwkv_g2': ((1, 128, 512), 2), 'w_out': ((1, 1024, 1024), 1), 'ffn_w_gate': ((1, 1024, 2816), 2),
    'ffn_w_up': ((1, 1024, 2816), 2), 'ffn_conv_w': ((1, 3, 1, 2816), 3), 'ffn_w_down': ((1, 2816, 1024), 1),
}
REPL_NAMES = [n for n in WEIGHT_NAMES if n not in SHARDED]
BIG_NAMES = ['w_in', 'w_out', 'ffn_w_gate', 'ffn_w_up', 'ffn_w_down']
BIG_T = ('w_in', 'ffn_w_gate', 'ffn_w_up')
LATE_NAMES = ['w_out', 'ffn_w_gate', 'ffn_w_up', 'ffn_w_down']
SMALL_NAMES = [n for n in WEIGHT_NAMES if n in SHARDED and n not in BIG_NAMES]


def _cparams(n_grid):
    return pltpu.CompilerParams(dimension_semantics=("arbitrary",) * n_grid, vmem_limit_bytes=VMEM_LIMIT)


def _round_up(n, m):
    return (n + m - 1) // m * m


def _tile(n, cap):
    best = None
    for t in range(128, min(n, cap) + 1, 128):
        if n % t == 0:
            best = t
    assert best is not None, n
    return best


@jax.custom_vjp
def _bdot(x, w):
    return jnp.dot(x.astype(BF16), w.astype(BF16), preferred_element_type=F32)


def _bdot_fwd(x, w):
    return _bdot(x, w), (x, w)


def _bdot_bwd(res, g):
    x, w = res
    gb = g.astype(BF16)
    dx = lax.dot_general(gb, w.astype(BF16), (((1,), (1,)), ((), ())), preferred_element_type=F32)
    dw = lax.dot_general(x.astype(BF16), gb, (((0,), (0,)), ((), ())), preferred_element_type=F32)
    return dx, dw.astype(w.dtype)


_bdot.defvjp(_bdot_fwd, _bdot_bwd)


@jax.custom_vjp
def _shift_rows(x, prev):
    rolled = pltpu.roll(x, 1, 0)
    row = lax.broadcasted_iota(jnp.int32, x.shape, 0)
    return jnp.where(row == 0, jnp.broadcast_to(prev, x.shape), rolled)


def _shift_rows_fwd(x, prev):
    return _shift_rows(x, prev), None


def _shift_rows_bwd(_, g):
    n = g.shape[0]
    rolled = pltpu.roll(g, n - 1, 0)
    row = lax.broadcasted_iota(jnp.int32, g.shape, 0)
    return jnp.where(row == n - 1, 0.0, rolled), g[0:1, :]


_shift_rows.defvjp(_shift_rows_fwd, _shift_rows_bwd)


@jax.custom_vjp
def _swap_halves(x):
    return pltpu.roll(x, 64, 1)


_swap_halves.defvjp(lambda x: (_swap_halves(x), None), lambda _, g: (pltpu.roll(g, 64, 1),))


def _sigmoid(x):
    return 1.0 / (1.0 + jnp.exp(-x))


def _softplus(x):
    return jnp.maximum(x, 0.0) + jnp.log(1.0 + jnp.exp(-jnp.abs(x)))


def _rms_fn(x, g):
    return x * lax.rsqrt(jnp.mean(x * x, axis=-1, keepdims=True) + NORM_EPS) * g


def _pre_a_fn(h1, h1p, p, pp, mu_w, mu_a, mu_g, mu_r, mu_k, mu_v, w0, w1, w2, a0, a1, a2, g1, g2):
    W = RWKV_WIDTH
    h1s = _shift_rows(h1, h1p)
    ps = _shift_rows(p, pp)
    dx = h1s - h1
    xw = h1 + dx * mu_w
    xa = h1 + dx * mu_a
    xg = h1 + dx * mu_g
    dp = ps - p
    r = p[:, 0:W] + dp[:, 0:W] * mu_r
    k0 = p[:, W:2 * W] + dp[:, W:2 * W] * mu_k
    v = p[:, 2 * W:3 * W] + dp[:, 2 * W:3 * W] * mu_v
    wl = w0 + _bdot(jnp.tanh(_bdot(xw, w1)), w2)
    w_log = -_softplus(-wl) - 0.5
    lw = -jnp.exp(w_log)
    a = _sigmoid(a0 + _bdot(_bdot(xa, a1), a2))
    g = _bdot(_sigmoid(_bdot(xg, g1)), g2)
    return r, k0, v, lw, a, g


def _pre_b_fn(k0, a, k_k, k_a):
    kkr = k0 * k_k
    nrm = jnp.sqrt(jnp.sum(kkr * kkr, axis=-1, keepdims=True))
    kk = kkr / jnp.maximum(nrm, 1e-12)
    k = k0 * (1.0 + (a - 1.0) * k_a)
    return -kk, k, kk * a


def _rwkv_post_fn(y, r, k, v, g, lnx_w, lnx_b, r_k):
    mu = jnp.mean(y, axis=-1, keepdims=True)
    yc = y - mu
    var = jnp.mean(yc * yc, axis=-1, keepdims=True)
    yn = yc * lax.rsqrt(var + RWKV_GN_EPS) * lnx_w + lnx_b
    bonus = jnp.sum(r * k * r_k, axis=-1, keepdims=True) * v
    return ((yn + bonus) * g,)


def _rotary_fn(cos2, sin2, q, k):
    qs, ks = [], []
    for h in range(RET_HEADS):
        sl = slice(h * RET_HEAD_DIM, (h + 1) * RET_HEAD_DIM)
        qh, kh = q[:, sl], k[:, sl]
        qs.append(qh * cos2 + _swap_halves(qh) * sin2)
        ks.append((kh * cos2 + _swap_halves(kh) * sin2) * (RET_HEAD_DIM ** -0.5))
    return jnp.concatenate(qs, axis=1), jnp.concatenate(ks, axis=1)


def _ret_post_fn(y, gp, gn_w):
    outs = []
    for h in range(RET_HEADS):
        sl = slice(h * RET_HEAD_DIM, (h + 1) * RET_HEAD_DIM)
        yh = y[:, sl]
        mu = jnp.mean(yh, axis=-1, keepdims=True)
        yc = yh - mu
        var = jnp.mean(yc * yc, axis=-1, keepdims=True)
        outs.append(yc * lax.rsqrt(var + RET_GN_EPS) * gn_w[:, sl])
    yn = jnp.concatenate(outs, axis=1)
    return (gp * _sigmoid(gp) * yn,)


def _tok_axis(a):
    return 1 if a.ndim == 3 else 0


def _blk_spec(a, tb, rev_nb=None):
    nd, ax = a.ndim, _tok_axis(a)
    shape = a.shape[:ax] + (tb,) + a.shape[ax + 1:]

    def imap(i):
        idx = [0] * nd
        idx[ax] = i if rev_nb is None else rev_nb - 1 - i
        return tuple(idx)

    return pl.BlockSpec(shape, imap)


def _full_spec(a):
    nd = a.ndim
    return pl.BlockSpec(a.shape, lambda i: (0,) * nd)


def _tok_fwd(name, fn, toks, consts, out_tails, tb=TOK_BLOCK):
    n_in = len(toks) + len(consts)
    tn = toks[0].shape[_tok_axis(toks[0])]

    def body(*refs):
        outs = fn(*[r[...] for r in refs[:n_in]])
        for r, o in zip(refs[n_in:], outs):
            r[...] = o

    out_shape = [jax.ShapeDtypeStruct((tn,) + tuple(s) if len(s) == 1 else (s[0], tn, s[1]), F32) for s in out_tails]
    return pl.pallas_call(
        body, name=name, grid=(tn // tb,),
        in_specs=[_blk_spec(a, tb) for a in toks] + [_full_spec(c) for c in consts],
        out_specs=[_blk_spec(o, tb) for o in out_shape], out_shape=out_shape,
        compiler_params=_cparams(1))(*toks, *consts)


def _tok_bwd(name, fn, aux, toks, consts, cts, add=None, tb=TOK_BLOCK):
    n_aux, n_tok, n_c = len(aux), len(toks), len(consts)
    ct_groups = [c if isinstance(c, (tuple, list)) else (c,) for c in cts]
    ct_flat = [a for grp in ct_groups for a in grp]
    n_ct = len(ct_flat)
    n_add = 0 if add is None else 1
    tn = toks[0].shape[_tok_axis(toks[0])]

    def body(*refs):
        pos = 0
        aux_v = [r[...] for r in refs[pos:pos + n_aux]]; pos += n_aux
        tok_v = [r[...] for r in refs[pos:pos + n_tok]]; pos += n_tok
        const_v = [r[...] for r in refs[pos:pos + n_c]]; pos += n_c
        ct_refs = refs[pos:pos + n_ct]; pos += n_ct
        add_refs = refs[pos:pos + n_add]; pos += n_add
        dtok_refs = refs[pos:pos + n_tok]; pos += n_tok
        dconst_refs = refs[pos:pos + n_c]
        ct_v, q = [], 0
        for grp in ct_groups:
            s = ct_refs[q][...]
            for r in ct_refs[q + 1:q + len(grp)]:
                s = s + r[...]
            q += len(grp)
            ct_v.append(s)
        _, vjp = jax.vjp(lambda *tc: fn(*aux_v, *tc), *tok_v, *const_v)
        grads = vjp(tuple(ct_v))
        for j, r in enumerate(dtok_refs):
            gj = grads[j]
            if j == 0 and n_add:
                gj = gj + add_refs[0][...]
            r[...] = gj

        @pl.when(pl.program_id(0) == 0)
        def _():
            for r in dconst_refs:
                r[...] = jnp.zeros(r.shape, F32)

        for j, r in enumerate(dconst_refs):
            r[...] += grads[n_tok + j]

    ins = list(aux) + list(toks) + list(consts) + ct_flat + ([add] if n_add else [])
    in_specs = ([_blk_spec(a, tb) for a in aux] + [_blk_spec(a, tb) for a in toks] + [_full_spec(c) for c in consts]
                + [_blk_spec(a, tb) for a in ct_flat] + ([_blk_spec(add, tb)] if n_add else []))
    out_shape = [jax.ShapeDtypeStruct(a.shape, F32) for a in toks] + [jax.ShapeDtypeStruct(c.shape, F32) for c in consts]
    out_specs = [_blk_spec(a, tb) for a in toks] + [_full_spec(c) for c in consts]
    return pl.pallas_call(body, name=name, grid=(tn // tb,), in_specs=in_specs, out_specs=out_specs,
                          out_shape=out_shape, compiler_params=_cparams(1))(*ins)


def _mm(name, a, b, ta=False, tb=False, add=None):
    if ta:
        kd, m = a.shape
    else:
        m, kd = a.shape
    if tb:
        n, kb = b.shape
    else:
        kb, n = b.shape
    assert kd == kb, (a.shape, b.shape)
    tm, tn, tk = _tile(m, 1024), _tile(n, 512), _tile(kd, 512)
    nk = kd // tk
    has_add = add is not None
    dims = (((0 if ta else 1,), (1 if tb else 0,)), ((), ()))

    def body(*refs):
        a_ref, b_ref = refs[0], refs[1]
        o_ref, acc_ref = refs[-2], refs[-1]
        k = pl.program_id(2)

        @pl.when(k == 0)
        def _():
            acc_ref[...] = refs[2][...] if has_add else jnp.zeros(acc_ref.shape, F32)

        acc_ref[...] += lax.dot_general(a_ref[...].astype(BF16), b_ref[...].astype(BF16), dims,
                                        preferred_element_type=F32)

        @pl.when(k == nk - 1)
        def _():
            o_ref[...] = acc_ref[...]

    a_spec = pl.BlockSpec((tk, tm), lambda i, j, k: (k, i)) if ta else pl.BlockSpec((tm, tk), lambda i, j, k: (i, k))
    b_spec = pl.BlockSpec((tn, tk), lambda i, j, k: (j, k)) if tb else pl.BlockSpec((tk, tn), lambda i, j, k: (k, j))
    o_spec = pl.BlockSpec((tm, tn), lambda i, j, k: (i, j))
    ins = [a, b] + ([add] if has_add else [])
    in_specs = [a_spec, b_spec] + ([o_spec] if has_add else [])
    return pl.pallas_call(body, name=name, grid=(m // tm, n // tn, nk), in_specs=in_specs, out_specs=o_spec,
                          out_shape=jax.ShapeDtypeStruct((m, n), F32),
                          scratch_shapes=[pltpu.VMEM((tm, tn), F32)], compiler_params=_cparams(3))(*ins)


def _prev8_spec(a, tb, rev_nb=None):
    r = tb // 8
    if rev_nb is None:
        return pl.BlockSpec((8, a.shape[1]), lambda i: (jnp.maximum(i * r - 1, 0), 0))
    return pl.BlockSpec((8, a.shape[1]), lambda i: (jnp.maximum((rev_nb - 1 - i) * r - 1, 0), 0))


def _pre_a_fwd(h1, p, consts, tb=TOK_BLOCK):
    tn = h1.shape[0]

    def body(h1_ref, h1h_ref, p_ref, ph_ref, *rest):
        c_refs, o_refs = rest[:len(consts)], rest[len(consts):]
        first = pl.program_id(0) == 0
        h1p = jnp.where(first, 0.0, h1h_ref[7:8, :])
        pp = jnp.where(first, 0.0, ph_ref[7:8, :])
        outs = _pre_a_fn(h1_ref[...], h1p, p_ref[...], pp, *[c[...] for c in c_refs])
        for r, o in zip(o_refs, outs):
            r[...] = o

    out_shape = [jax.ShapeDtypeStruct((tn, RWKV_WIDTH), F32) for _ in range(6)]
    return pl.pallas_call(
        body, name="rwkv_pre_a_fwd", grid=(tn // tb,),
        in_specs=[_blk_spec(h1, tb), _prev8_spec(h1, tb), _blk_spec(p, tb), _prev8_spec(p, tb)]
        + [_full_spec(c) for c in consts],
        out_specs=[_blk_spec(o, tb) for o in out_shape], out_shape=out_shape,
        compiler_params=_cparams(1))(h1, h1, p, p, *consts)


def _pre_a_bwd(h1, p, consts, cts, tb=TOK_BLOCK):
    tn = h1.shape[0]
    nb = tn // tb
    n_c = len(consts)
    ct_groups = [c if isinstance(c, (tuple, list)) else (c,) for c in cts]
    ct_flat = [a for grp in ct_groups for a in grp]
    n_ct = len(ct_flat)

    def body(*refs):
        h1_ref, h1h_ref, p_ref, ph_ref = refs[:4]
        c_refs = refs[4:4 + n_c]
        ct_refs = refs[4 + n_c:4 + n_c + n_ct]
        dh1_ref, dp_ref = refs[4 + n_c + n_ct:6 + n_c + n_ct]
        dc_refs = refs[6 + n_c + n_ct:6 + 2 * n_c + n_ct]
        ch_ref, cp_ref = refs[-2], refs[-1]
        i = pl.program_id(0)
        first_block = i == nb - 1
        h1p = jnp.where(first_block, 0.0, h1h_ref[7:8, :])
        pp = jnp.where(first_block, 0.0, ph_ref[7:8, :])
        ct_v, q = [], 0
        for grp in ct_groups:
            s = ct_refs[q][...]
            for r in ct_refs[q + 1:q + len(grp)]:
                s = s + r[...]
            q += len(grp)
            ct_v.append(s)
        _, vjp = jax.vjp(_pre_a_fn, h1_ref[...], h1p, p_ref[...], pp, *[c[...] for c in c_refs])
        grads = vjp(tuple(ct_v))

        @pl.when(i == 0)
        def _():
            ch_ref[...] = jnp.zeros(ch_ref.shape, F32)
            cp_ref[...] = jnp.zeros(cp_ref.shape, F32)
            for r in dc_refs:
                r[...] = jnp.zeros(r.shape, F32)

        rowh = lax.broadcasted_iota(jnp.int32, (tb, h1.shape[1]), 0)
        rowp = lax.broadcasted_iota(jnp.int32, (tb, p.shape[1]), 0)
        dh1_ref[...] = grads[0] + jnp.where(rowh == tb - 1, jnp.broadcast_to(ch_ref[0:1, :], rowh.shape), 0.0)
        dp_ref[...] = grads[2] + jnp.where(rowp == tb - 1, jnp.broadcast_to(cp_ref[0:1, :], rowp.shape), 0.0)
        ch_ref[0:1, :] = grads[1]
        cp_ref[0:1, :] = grads[3]
        for j, r in enumerate(dc_refs):
            r[...] += grads[4 + j]

    ins = [h1, h1, p, p] + list(consts) + ct_flat
    in_specs = ([_blk_spec(h1, tb, nb), _prev8_spec(h1, tb, nb), _blk_spec(p, tb, nb), _prev8_spec(p, tb, nb)]
                + [_full_spec(c) for c in consts] + [_blk_spec(a, tb, nb) for a in ct_flat])
    out_shape = ([jax.ShapeDtypeStruct(h1.shape, F32), jax.ShapeDtypeStruct(p.shape, F32)]
                 + [jax.ShapeDtypeStruct(c.shape, F32) for c in consts])
    out_specs = [_blk_spec(h1, tb, nb), _blk_spec(p, tb, nb)] + [_full_spec(c) for c in consts]
    return pl.pallas_call(body, name="rwkv_pre_a_bwd", grid=(nb,), in_specs=in_specs, out_specs=out_specs,
                          out_shape=out_shape,
                          scratch_shapes=[pltpu.VMEM((8, h1.shape[1]), F32), pltpu.VMEM((8, p.shape[1]), F32)],
                          compiler_params=_cparams(1))(*ins)


def _my_index():
    return 4 * lax.axis_index("x") + 2 * lax.axis_index("y") + lax.axis_index("c")


def _peer(k):
    x, y, c = lax.axis_index("x"), lax.axis_index("y"), lax.axis_index("c")
    px = 1 - x if k & 4 else x
    py = 1 - y if k & 2 else y
    pc = 1 - c if k & 1 else c
    return (px, py, pc), 4 * px + 2 * py + pc


def _xchg_sems(n):
    return [pltpu.SemaphoreType.DMA((n * (N_DEV - 1),)), pltpu.SemaphoreType.DMA((n * (N_DEV - 1),)),
            pltpu.SemaphoreType.DMA((n,))]


def _xchg_copies(srcs, dsts, sems, scatter, incoming=False):
    send_sems, recv_sems, local_sems = sems
    me = _my_index()
    local, remote = [], []
    for i, (s, d) in enumerate(zip(srcs, dsts)):
        if not incoming:
            local.append(pltpu.make_async_copy(s.at[me] if scatter else s, d.at[me], local_sems.at[i]))
        for k in range(1, N_DEV):
            peer, plin = _peer(k)
            j = i * (N_DEV - 1) + k - 1
            s_slot, d_slot = (me, plin) if incoming else (plin, me)
            remote.append(pltpu.make_async_remote_copy(
                src_ref=s.at[s_slot] if scatter else s, dst_ref=d.at[d_slot], send_sem=send_sems.at[j],
                recv_sem=recv_sems.at[j], device_id=peer, device_id_type=pl.DeviceIdType.MESH))
    return local, remote


def _xchg_start(srcs, dsts, sems, scatter):
    local, out = _xchg_copies(srcs, dsts, sems, scatter)
    for cp in local + out:
        cp.start()


def _xchg_wait(srcs, dsts, sems, scatter):
    for cp in _xchg_copies(srcs, dsts, sems, scatter, incoming=True)[1]:
        cp.wait_recv()
    local, out = _xchg_copies(srcs, dsts, sems, scatter)
    for cp in out:
        cp.wait_send()
    for cp in local:
        cp.wait()


def _xchg_out_shapes(srcs, scatter):
    return [jax.ShapeDtypeStruct(s.shape if scatter else (N_DEV,) + s.shape, s.dtype) for s in srcs]


_ANY = pl.BlockSpec(memory_space=pl.ANY)


def _exchange(name, srcs, scatter):
    n = len(srcs)

    def body(*refs):
        s, d, sems = refs[:n], refs[n:2 * n], refs[2 * n:]
        _xchg_start(s, d, sems, scatter)
        _xchg_wait(s, d, sems, scatter)

    return pl.pallas_call(body, name=name, in_specs=[_ANY] * n, out_specs=[_ANY] * n,
                          out_shape=_xchg_out_shapes(srcs, scatter), scratch_shapes=_xchg_sems(n))(*srcs)


_HI = lax.Precision.HIGHEST
_MM_DIMS = {'nn': (((1,), (0,)), ((), ())), 'nt': (((1,), (1,)), ((), ())), 'tn': (((0,), (0,)), ((), ()))}


def _cmm_raw(x, y, kind):
    return lax.dot_general(x.astype(BF16), y.astype(BF16), _MM_DIMS[kind], preferred_element_type=F32)


@functools.partial(jax.custom_vjp, nondiff_argnums=(2,))
def _cmm(x, y, kind):
    return _cmm_raw(x, y, kind)


def _cmm_fwd(x, y, kind):
    return _cmm_raw(x, y, kind), (x, y)


def _cmm_bwd(kind, res, g):
    x, y = res
    if kind == 'nn':
        return _cmm_raw(g, y, 'nt'), _cmm_raw(x, g, 'tn')
    if kind == 'nt':
        return _cmm_raw(g, y, 'nn'), _cmm_raw(g, x, 'tn')
    return _cmm_raw(y, g, 'nt'), _cmm_raw(x, g, 'nn')


_cmm.defvjp(_cmm_fwd, _cmm_bwd)


def _chunk_fn(S0, r, lw, k, v, a, b):
    hs = range(len(r))
    C = r[0].shape[0]
    ii = lax.broadcasted_iota(jnp.int32, (C, C), 0)
    jj = lax.broadcasted_iota(jnp.int32, (C, C), 1)
    incl, strict = ii >= jj, ii > jj
    eye = (ii == jj).astype(F32)
    inclf = incl.astype(F32)
    cum = [jnp.dot(inclf, lw[h], precision=_HI) for h in hs]
    e_inv = [jnp.exp(-cum[h]) for h in hs]
    At = [a[h] * jnp.exp(cum[h] - lw[h]) for h in hs]
    Rt = [r[h] * jnp.exp(cum[h]) for h in hs]
    Kh = [k[h] * e_inv[h] for h in hs]
    Bh = [b[h] * e_inv[h] for h in hs]
    Mab = [jnp.where(strict, _cmm(At[h], Bh[h], 'nt'), 0.0) for h in hs]
    Mak = [jnp.where(strict, _cmm(At[h], Kh[h], 'nt'), 0.0) for h in hs]
    Mrk = [jnp.where(incl, _cmm(Rt[h], Kh[h], 'nt'), 0.0) for h in hs]
    Mrb = [jnp.where(incl, _cmm(Rt[h], Bh[h], 'nt'), 0.0) for h in hs]
    rhs = [_cmm(At[h], S0[h], 'nt') + _cmm(Mak[h], v[h], 'nn') for h in hs]
    P = Mab
    Tm = [eye + P[h] for h in hs]
    n = 1
    while 2 * n < C:
        P = [jnp.dot(P[h], P[h], precision=_HI) for h in hs]
        Tm = [jnp.dot(Tm[h], eye + P[h], precision=_HI) for h in hs]
        n *= 2
    U = [jnp.dot(Tm[h], rhs[h], precision=_HI) for h in hs]
    Y = [_cmm(Rt[h], S0[h], 'nt') + _cmm(Mrk[h], v[h], 'nn') + _cmm(Mrb[h], U[h], 'nn') for h in hs]
    gC = [jnp.exp(jnp.sum(lw[h], axis=0, keepdims=True)) for h in hs]
    SC = [S0[h] * gC[h] + _cmm(v[h], Kh[h] * gC[h], 'tn') + _cmm(U[h], Bh[h] * gC[h], 'tn') for h in hs]
    return tuple(Y), tuple(SC)


def _cscan_fwd(r, lw, k, v, a, b, xs):
    n_x = len(xs)
    H, tn, Dh = r.shape
    nc = tn // SCAN_CHUNK

    def body(r_ref, lw_ref, k_ref, v_ref, a_ref, b_ref, *rest):
        x_src, (y_ref, ck_ref) = rest[:n_x], rest[n_x:n_x + 2]
        x_dst, s_ref, sems = rest[n_x + 2:2 * n_x + 2], rest[2 * n_x + 2], rest[2 * n_x + 3:]

        @pl.when(pl.program_id(0) == 0)
        def _():
            s_ref[...] = jnp.zeros(s_ref.shape, F32)
            _xchg_start(x_src, x_dst, sems, False)

        ck_ref[0] = s_ref[...]
        heads = lambda ref: tuple(ref[h] for h in range(H))
        y, sc = _chunk_fn(heads(s_ref), heads(r_ref), heads(lw_ref), heads(k_ref), heads(v_ref), heads(a_ref),
                          heads(b_ref))
        for h in range(H):
            y_ref[h] = y[h]
            s_ref[h] = sc[h]

        @pl.when(pl.program_id(0) == nc - 1)
        def _():
            _xchg_wait(x_src, x_dst, sems, False)

    hm = pl.BlockSpec((H, SCAN_CHUNK, Dh), lambda c: (0, c, 0))
    res = pl.pallas_call(
        body, name="rwkv_scan_fwd", grid=(nc,), in_specs=[hm] * 6 + [_ANY] * n_x,
        out_specs=[hm, pl.BlockSpec((1, H, Dh, Dh), lambda c: (c, 0, 0, 0))] + [_ANY] * n_x,
        out_shape=[jax.ShapeDtypeStruct((H, tn, Dh), F32), jax.ShapeDtypeStruct((nc, H, Dh, Dh), F32)]
        + _xchg_out_shapes(xs, False),
        scratch_shapes=[pltpu.VMEM((H, Dh, Dh), F32)] + _xchg_sems(n_x),
        compiler_params=_cparams(1))(r, lw, k, v, a, b, *xs)
    return res[0], res[1], res[2:]


def _cscan_bwd(r, lw, k, v, a, b, dy, ck, xs):
    n_x = len(xs)
    H, tn, Dh = r.shape
    nc = tn // SCAN_CHUNK

    def body(r_ref, lw_ref, k_ref, v_ref, a_ref, b_ref, dy_ref, ck_ref, *rest):
        x_src = rest[:n_x]
        d_refs = rest[n_x:n_x + 6]
        x_dst = rest[n_x + 6:2 * n_x + 6]
        g_ref = rest[2 * n_x + 6]
        sems = rest[2 * n_x + 7:]

        @pl.when(pl.program_id(0) == 0)
        def _():
            g_ref[...] = jnp.zeros(g_ref.shape, F32)
            _xchg_start(x_src, x_dst, sems, True)

        heads = lambda ref: tuple(ref[h] for h in range(H))
        s0 = tuple(ck_ref[0, h] for h in range(H))
        _, vjp = jax.vjp(_chunk_fn, s0, heads(r_ref), heads(lw_ref), heads(k_ref), heads(v_ref), heads(a_ref),
                         heads(b_ref))
        grads = vjp((heads(dy_ref), heads(g_ref)))
        for h in range(H):
            g_ref[h] = grads[0][h]
            for d_ref, gz in zip(d_refs, grads[1:]):
                d_ref[h] = gz[h]

        @pl.when(pl.program_id(0) == nc - 1)
        def _():
            _xchg_wait(x_src, x_dst, sems, True)

    hm = pl.BlockSpec((H, SCAN_CHUNK, Dh), lambda c: (0, nc - 1 - c, 0))
    hshape = jax.ShapeDtypeStruct((H, tn, Dh), F32)
    res = pl.pallas_call(
        body, name="rwkv_scan_bwd", grid=(nc,),
        in_specs=[hm] * 7 + [pl.BlockSpec((1, H, Dh, Dh), lambda c: (nc - 1 - c, 0, 0, 0))] + [_ANY] * n_x,
        out_specs=[hm] * 6 + [_ANY] * n_x, out_shape=[hshape] * 6 + _xchg_out_shapes(xs, True),
        scratch_shapes=[pltpu.VMEM((H, Dh, Dh), F32)] + _xchg_sems(n_x),
        compiler_params=_cparams(1))(r, lw, k, v, a, b, dy, ck, *xs)
    return res[:6], res[6:]


def _decay_mask(lg, i, j, blk):
    rows = lax.broadcasted_iota(jnp.int32, (blk, blk), 0)
    cols = lax.broadcasted_iota(jnp.int32, (blk, blk), 1)
    dd = (rows - cols + (i - j) * blk).astype(F32)
    return jnp.where(dd >= 0.0, jnp.exp(lg * jnp.maximum(dd, 0.0)), 0.0)


_NT = (((1,), (1,)), ((), ()))
_TN = (((0,), (0,)), ((), ()))


def _ret_attn_fwd(lg, q, k, v, blk=ATT_BLOCK):
    tn = q.shape[0]
    Dh = RET_HEAD_DIM

    def body(lg_ref, q_ref, k_ref, v_ref, o_ref):
        i = pl.program_id(1)
        lgv = lg_ref[0][:, 0:1]
        qb = q_ref[...].astype(BF16)

        def jb(j, acc):
            ks = pl.ds(pl.multiple_of(j * blk, blk), blk)
            s = lax.dot_general(qb, k_ref[ks, :].astype(BF16), _NT, preferred_element_type=F32)
            s = s * _decay_mask(lgv, i, j, blk)
            return acc + jnp.dot(s.astype(BF16), v_ref[ks, :].astype(BF16), preferred_element_type=F32)

        o_ref[...] = lax.fori_loop(0, i + 1, jb, jnp.zeros((blk, Dh), F32))

    full = pl.BlockSpec((tn, Dh), lambda h, i: (0, h))
    qs = pl.BlockSpec((blk, Dh), lambda h, i: (i, h))
    return pl.pallas_call(
        body, name="ret_attn_fwd", grid=(RET_HEADS, tn // blk),
        in_specs=[pl.BlockSpec((1, 1, 128), lambda h, i: (h, 0, 0)), qs, full, full],
        out_specs=qs, out_shape=jax.ShapeDtypeStruct(q.shape, F32), compiler_params=_cparams(2))(lg, q, k, v)


def _ret_attn_bwd(lg, q, k, v, do, blk=ATT_BLOCK):
    tn = q.shape[0]
    nb = tn // blk
    Dh = RET_HEAD_DIM

    def body(lg_ref, q_ref, k_ref, v_ref, do_ref, dq_ref, dk_ref, dv_ref):
        lgv = lg_ref[0][:, 0:1]
        dk_ref[...] = jnp.zeros(dk_ref.shape, F32)
        dv_ref[...] = jnp.zeros(dv_ref.shape, F32)

        def ib(i, carry):
            qs = pl.ds(pl.multiple_of(i * blk, blk), blk)
            qb = q_ref[qs, :].astype(BF16)
            dob = do_ref[qs, :].astype(BF16)

            def jb(j, dq):
                ks = pl.ds(pl.multiple_of(j * blk, blk), blk)
                kb = k_ref[ks, :].astype(BF16)
                vb = v_ref[ks, :].astype(BF16)
                dm = _decay_mask(lgv, i, j, blk)
                s = lax.dot_general(qb, kb, _NT, preferred_element_type=F32) * dm
                ds = lax.dot_general(dob, vb, _NT, preferred_element_type=F32) * dm
                sb, dsb = s.astype(BF16), ds.astype(BF16)
                dv_ref[ks, :] += lax.dot_general(sb, dob, _TN, preferred_element_type=F32)
                dk_ref[ks, :] += lax.dot_general(dsb, qb, _TN, preferred_element_type=F32)
                return dq + jnp.dot(dsb, kb, preferred_element_type=F32)

            dq_ref[qs, :] = lax.fori_loop(0, i + 1, jb, jnp.zeros((blk, Dh), F32))
            return carry

        lax.fori_loop(0, nb, ib, 0)

    full = pl.BlockSpec((tn, Dh), lambda h: (0, h))
    sh = jax.ShapeDtypeStruct(q.shape, F32)
    return pl.pallas_call(
        body, name="ret_attn_bwd", grid=(RET_HEADS,),
        in_specs=[pl.BlockSpec((1, 1, 128), lambda h: (h, 0, 0)), full, full, full, full],
        out_specs=[full, full, full], out_shape=[sh, sh, sh], compiler_params=_cparams(1))(lg, q, k, v, do)


def _next8_spec(a, tb):
    r = tb // 8
    last = a.shape[0] // 8 - 1
    return pl.BlockSpec((8, a.shape[1]), lambda i: (jnp.minimum((i + 1) * r, last), 0))


def _conv_taps(g_ext, cw_ref, cb_ref):
    return (cw_ref[2:3, :] * g_ext + cw_ref[1:2, :] * pltpu.roll(g_ext, 1, 0)
            + cw_ref[0:1, :] * pltpu.roll(g_ext, 2, 0) + cb_ref[...])


def _glu_fwd(gate, up, cw, cb, tb=TOK_BLOCK):
    tn = gate.shape[0]

    def body(g_ref, gh_ref, u_ref, cw_ref, cb_ref, o_ref):
        halo = jnp.where(pl.program_id(0) == 0, 0.0, gh_ref[...])
        g_ext = jnp.concatenate([halo, g_ref[...]], axis=0)
        gc = _conv_taps(g_ext, cw_ref, cb_ref)[8:, :]
        o_ref[...] = gc * _sigmoid(gc) * u_ref[...]

    return pl.pallas_call(
        body, name="glu_fwd", grid=(tn // tb,),
        in_specs=[_blk_spec(gate, tb), _prev8_spec(gate, tb), _blk_spec(up, tb), _full_spec(cw), _full_spec(cb)],
        out_specs=_blk_spec(gate, tb), out_shape=jax.ShapeDtypeStruct(gate.shape, F32),
        compiler_params=_cparams(1))(gate, gate, up, cw, cb)


def _glu_bwd(gate, up, dact, cw, cb, tb=TOK_BLOCK):
    tn = gate.shape[0]
    nb = tn // tb

    def body(g_ref, gp_ref, gn_ref, u_ref, un_ref, d_ref, dn_ref, cw_ref, cb_ref, dg_ref, du_ref, dcw_ref, dcb_ref):
        i = pl.program_id(0)
        gprev = jnp.where(i == 0, 0.0, gp_ref[...])
        dnext = jnp.where(i == nb - 1, 0.0, dn_ref[...])
        g_ext = jnp.concatenate([gprev, g_ref[...], gn_ref[...]], axis=0)
        gc = _conv_taps(g_ext, cw_ref, cb_ref)[8:, :]
        u_e = jnp.concatenate([u_ref[...], un_ref[...]], axis=0)
        d_e = jnp.concatenate([d_ref[...], dnext], axis=0)
        s = _sigmoid(gc)
        dgc = d_e * u_e * (s * (1.0 + gc * (1.0 - s)))
        du_ref[...] = d_ref[...] * (gc * s)[:tb, :]
        n_e = tb + 8
        dg_ref[...] = (cw_ref[2:3, :] * dgc + cw_ref[1:2, :] * pltpu.roll(dgc, n_e - 1, 0)
                       + cw_ref[0:1, :] * pltpu.roll(dgc, n_e - 2, 0))[:tb, :]

        @pl.when(i == 0)
        def _():
            dcw_ref[...] = jnp.zeros(dcw_ref.shape, F32)
            dcb_ref[...] = jnp.zeros(dcb_ref.shape, F32)

        dgc_b = dgc[:tb, :]
        g0 = g_ext[8:8 + tb, :]
        g1 = pltpu.roll(g_ext, 1, 0)[8:8 + tb, :]
        g2 = pltpu.roll(g_ext, 2, 0)[8:8 + tb, :]
        dcw_ref[2:3, :] += jnp.sum(dgc_b * g0, axis=0, keepdims=True)
        dcw_ref[1:2, :] += jnp.sum(dgc_b * g1, axis=0, keepdims=True)
        dcw_ref[0:1, :] += jnp.sum(dgc_b * g2, axis=0, keepdims=True)
        dcb_ref[...] += jnp.sum(dgc_b, axis=0, keepdims=True)

    sh = jax.ShapeDtypeStruct(gate.shape, F32)
    return pl.pallas_call(
        body, name="glu_bwd", grid=(nb,),
        in_specs=[_blk_spec(gate, tb), _prev8_spec(gate, tb), _next8_spec(gate, tb), _blk_spec(up, tb),
                  _next8_spec(up, tb), _blk_spec(dact, tb), _next8_spec(dact, tb), _full_spec(cw), _full_spec(cb)],
        out_specs=[_blk_spec(gate, tb), _blk_spec(gate, tb), _full_spec(cw), _full_spec(cb)],
        out_shape=[sh, sh, jax.ShapeDtypeStruct(cw.shape, F32), jax.ShapeDtypeStruct(cb.shape, F32)],
        compiler_params=_cparams(1))(gate, gate, gate, up, up, dact, dact, cw, cb)


def _final_loss(x2, tgt, g, tb=TOK_BLOCK):
    tn, dm = x2.shape

    def body(x_ref, t_ref, g_ref, l_ref, dx_ref, dg_ref):
        y, vjp = jax.vjp(_rms_fn, x_ref[...], g_ref[...])
        err = y - t_ref[...]
        dx, dg = vjp(err * (1.0 / dm))

        @pl.when(pl.program_id(0) == 0)
        def _():
            l_ref[...] = jnp.zeros(l_ref.shape, F32)
            dg_ref[...] = jnp.zeros(dg_ref.shape, F32)

        part = 0.5 * jnp.sum(jnp.mean(err * err, axis=-1, keepdims=True), axis=0, keepdims=True)
        l_ref[...] += jnp.broadcast_to(part, l_ref.shape)
        dx_ref[...] = dx
        dg_ref[...] += dg

    return pl.pallas_call(
        body, name="final_loss", grid=(tn // tb,),
        in_specs=[_blk_spec(x2, tb), _blk_spec(tgt, tb), _full_spec(g)],
        out_specs=[pl.BlockSpec((8, 128), lambda i: (0, 0)), _blk_spec(x2, tb), _full_spec(g)],
        out_shape=[jax.ShapeDtypeStruct((8, 128), F32), jax.ShapeDtypeStruct(x2.shape, F32),
                   jax.ShapeDtypeStruct(g.shape, F32)],
        compiler_params=_cparams(1))(x2, tgt, g)


def _pad_cols(w, n):
    return jnp.pad(w, ((0, 0), (0, n - w.shape[1])))


def _pad_rows(w, n):
    return jnp.pad(w, ((0, n - w.shape[0]), (0, 0)))


def _to_heads(z):
    return jnp.swapaxes(z.reshape(z.shape[0], RWKV_HEADS, RWKV_HEAD_DIM), 0, 1)


def _to_flat(z):
    return jnp.swapaxes(z, 0, 1).reshape(z.shape[1], RWKV_WIDTH)


def _head_param(p):
    return p.reshape(RWKV_HEADS, 1, RWKV_HEAD_DIM)


def _local_step(x, tgt, W, late):
    tn = x.shape[0]
    Wd = RWKV_WIDTH
    row = lambda z: z.reshape(1, -1)
    g_mix, g_ffn, g_fin = row(W['norm_mix_g']), row(W['norm_ffn_g']), row(W['norm_final_g'])

    (h1,) = _tok_fwd("norm_mix_fwd", lambda a, g: (_rms_fn(a, g),), [x], [g_mix], [(D_MODEL,)])
    proj = _mm("proj_fwd", h1, W['w_in_t'], tb=True)
    p_rkv = proj[:, :3 * Wd]
    pre_consts = [row(W['rwkv_mu_w']), row(W['rwkv_mu_a']), row(W['rwkv_mu_g']), row(W['rwkv_mu_r']),
                  row(W['rwkv_mu_k']), row(W['rwkv_mu_v']), row(W['rwkv_w0']),
                  _pad_cols(W['rwkv_w1'], LORA_PAD), _pad_rows(W['rwkv_w2'], LORA_PAD), row(W['rwkv_a0']),
                  _pad_cols(W['rwkv_a1'], LORA_PAD), _pad_rows(W['rwkv_a2'], LORA_PAD),
                  W['rwkv_g1'], W['rwkv_g2']]
    r, k0, v, lw, a, g = _pre_a_fwd(h1, p_rkv, pre_consts)
    k_k, k_a = _head_param(W['rwkv_k_k']), _head_param(W['rwkv_k_a'])
    k0h, ah = _to_heads(k0), _to_heads(a)
    nkk, kh, bh = _tok_fwd("rwkv_pre_b_fwd", _pre_b_fn, [k0h, ah], [k_k, k_a], [(RWKV_HEADS, RWKV_HEAD_DIM)] * 3)
    rh, lwh, vh, gh = _to_heads(r), _to_heads(lw), _to_heads(v), _to_heads(g)
    yh, ck, gathered = _cscan_fwd(rh, lwh, kh, vh, nkk, bh, late)
    w_out, w_gate_t, w_up_t, w_down = [g_.reshape(-1, D_MODEL) for g_ in gathered]
    post_consts = [_head_param(W['rwkv_lnx_w']), _head_param(W['rwkv_lnx_b']), _head_param(W['rwkv_r_k'])]
    (y_rwkv_h,) = _tok_fwd("rwkv_post_fwd", _rwkv_post_fn, [yh, rh, kh, vh, gh], post_consts,
                           [(RWKV_HEADS, RWKV_HEAD_DIM)])
    y_rwkv = _to_flat(y_rwkv_h)

    pos = jnp.arange(tn, dtype=F32)
    half = RET_HEAD_DIM // 2
    inv_freq = ROPE_BASE ** (-jnp.arange(half, dtype=F32) / half)
    ang = pos[:, None] * inv_freq[None, :]
    cos2 = jnp.concatenate([jnp.cos(ang), jnp.cos(ang)], axis=1)
    sin2 = jnp.concatenate([-jnp.sin(ang), jnp.sin(ang)], axis=1)
    lg = jnp.log(1.0 - 2.0 ** (-5.0 - jnp.arange(RET_HEADS, dtype=F32)))
    lg = jnp.broadcast_to(lg[:, None, None], (RET_HEADS, 1, 128))
    q_p, k_p = proj[:, 3 * Wd:4 * Wd], proj[:, 4 * Wd:5 * Wd]
    v_ret, g_ret = proj[:, 5 * Wd:6 * Wd], proj[:, 6 * Wd:7 * Wd]
    q_rot, k_rot = _tok_fwd("ret_rotary_fwd", _rotary_fn, [cos2, sin2, q_p, k_p], [], [(RET_WIDTH,)] * 2)
    y_ret_raw = _ret_attn_fwd(lg, q_rot, k_rot, v_ret)
    gn_w = row(W['ret_gn_w'])
    (y_ret,) = _tok_fwd("ret_post_fwd", _ret_post_fn, [y_ret_raw, g_ret], [gn_w], [(RET_WIDTH,)])

    ycat = jnp.concatenate([y_rwkv, y_ret], axis=1)
    x1 = _mm("out_proj_fwd", ycat, w_out, add=x)
    (h2,) = _tok_fwd("norm_ffn_fwd", lambda a_, g_: (_rms_fn(a_, g_),), [x1], [g_ffn], [(D_MODEL,)])
    gate = _mm("ffn_gate_fwd", h2, w_gate_t, tb=True)
    up = _mm("ffn_up_fwd", h2, w_up_t, tb=True)
    cw = W['ffn_conv_w']
    cb = row(W['ffn_conv_b'])
    act = _glu_fwd(gate, up, cw, cb)
    x2 = _mm("ffn_down_fwd", act, w_down, add=x1)
    loss8, dx2, dg_fin = _final_loss(x2, tgt, g_fin)

    G = {'norm_final_g': dg_fin}
    dact = _mm("ffn_down_dx", dx2, w_down, tb=True)
    d_down = _mm("ffn_down_dw", act, dx2, ta=True)
    dgate, dup, dcw, dcb = _glu_bwd(gate, up, dact, cw, cb)
    G['ffn_conv_w'], G['ffn_conv_b'] = dcw, dcb
    dh2 = _mm("ffn_gate_dx", dgate, w_gate_t)
    dh2 = _mm("ffn_up_dx", dup, w_up_t, add=dh2)
    d_gate_t = _mm("ffn_gate_dw", dgate, h2, ta=True)
    d_up_t = _mm("ffn_up_dw", dup, h2, ta=True)
    dx1, G['norm_ffn_g'] = _tok_bwd("norm_ffn_bwd", lambda a_, g_: (_rms_fn(a_, g_),), [], [x1], [g_ffn], [dh2], add=dx2)
    dycat = _mm("out_proj_dx", dx1, w_out, tb=True)
    d_out = _mm("out_proj_dw", ycat, dx1, ta=True)
    late_grads = [z.reshape(N_DEV, -1, D_MODEL) for z in (d_out, d_gate_t, d_up_t, d_down)]
    dy_rwkv, dy_ret = dycat[:, :Wd], dycat[:, Wd:]

    dyr_raw, dg_ret, G['ret_gn_w'] = _tok_bwd("ret_post_bwd", _ret_post_fn, [], [y_ret_raw, g_ret], [gn_w], [dy_ret])
    dq_rot, dk_rot, dv_ret = _ret_attn_bwd(lg, q_rot, k_rot, v_ret, dyr_raw)
    dq_p, dk_p = _tok_bwd("ret_rotary_bwd", _rotary_fn, [cos2, sin2], [q_p, k_p], [], [dq_rot, dk_rot])

    dyh, drh1, dkh1, dvh1, dgh, G['rwkv_lnx_w'], G['rwkv_lnx_b'], G['rwkv_r_k'] = _tok_bwd(
        "rwkv_post_bwd", _rwkv_post_fn, [], [yh, rh, kh, vh, gh], post_consts, [_to_heads(dy_rwkv)])
    (drh2, dlwh, dkh2, dvh2, dnkk, dbh), late_parts = _cscan_bwd(rh, lwh, kh, vh, nkk, bh, dyh, ck, late_grads)
    dk0h, dah, G['rwkv_k_k'], G['rwkv_k_a'] = _tok_bwd(
        "rwkv_pre_b_bwd", _pre_b_fn, [], [k0h, ah], [k_k, k_a], [dnkk, (dkh1, dkh2), dbh])
    pre_cts = [(_to_flat(drh1), _to_flat(drh2)), _to_flat(dk0h), (_to_flat(dvh1), _to_flat(dvh2)),
               _to_flat(dlwh), _to_flat(dah), _to_flat(dgh)]
    pre_out = _pre_a_bwd(h1, p_rkv, pre_consts, pre_cts)
    dh1_a, dp_rkv = pre_out[0], pre_out[1]
    (G['rwkv_mu_w'], G['rwkv_mu_a'], G['rwkv_mu_g'], G['rwkv_mu_r'], G['rwkv_mu_k'], G['rwkv_mu_v'], G['rwkv_w0'],
     dw1, dw2, G['rwkv_a0'], da1, da2, G['rwkv_g1'], G['rwkv_g2']) = pre_out[2:]
    G['rwkv_w1'], G['rwkv_w2'] = dw1[:, :64], dw2[:64, :]
    G['rwkv_a1'], G['rwkv_a2'] = da1[:, :64], da2[:64, :]

    dproj = jnp.concatenate([dp_rkv, dq_p, dk_p, dv_ret, dg_ret], axis=1)
    dh1 = _mm("proj_dx", dproj, W['w_in_t'], add=dh1_a)
    G['w_in_t'] = _mm("proj_dw", dproj, h1, ta=True)
    dx, G['norm_mix_g'] = _tok_bwd("norm_mix_bwd", lambda a_, g_: (_rms_fn(a_, g_),), [], [x], [g_mix], [dh1], add=dx1)
    return loss8[0, 0], dx, G, late_parts


def _adamw(name, parts, w, m, v):
    rows, cols = w.shape
    tb = max(t for t in range(8, 65, 8) if rows % t == 0) if rows > 64 else rows
    c1 = 1.0 - ADAM_B1 ** ADAM_STEP
    c2 = 1.0 - ADAM_B2 ** ADAM_STEP

    def body(p_ref, w_ref, m_ref, v_ref, g_ref, d_ref, nm_ref, nv_ref):
        g = p_ref[0]
        for d in range(1, N_DEV):
            g = g + p_ref[d]
        mn = ADAM_B1 * m_ref[...] + (1.0 - ADAM_B1) * g
        vn = ADAM_B2 * v_ref[...] + (1.0 - ADAM_B2) * (g * g)
        m_hat = mn / c1
        v_hat = vn / c2
        g_ref[...] = g
        d_ref[...] = -ADAM_LR * (m_hat / (jnp.sqrt(v_hat) + ADAM_EPS) + ADAM_WD * w_ref[...])
        nm_ref[...] = mn
        nv_ref[...] = vn

    spec = pl.BlockSpec((tb, cols), lambda i: (i, 0))
    sh = jax.ShapeDtypeStruct((rows, cols), F32)
    return pl.pallas_call(
        body, name=name, grid=(rows // tb,),
        in_specs=[pl.BlockSpec((N_DEV, tb, cols), lambda i: (0, i, 0)), spec, spec, spec],
        out_specs=[spec] * 4, out_shape=[sh] * 4, compiler_params=_cparams(1))(parts, w, m, v)


def _local_shape(name):
    gs, ax = SHARDED[name]
    ls = list(gs)
    ls[ax] //= N_DEV
    return tuple(ls)


def _seg(flat, seg):
    n = flat.shape[-1]
    pad = _round_up(n, seg) - n
    if pad:
        flat = jnp.pad(flat, [(0, 0)] * (flat.ndim - 1) + [(0, pad)])
    return flat


def _split3(w):
    hi = w.astype(BF16)
    r1 = w - hi.astype(F32)
    mid = r1.astype(BF16)
    lo = (r1 - mid.astype(F32)).astype(BF16)
    return hi, mid, lo


def _pack_small_shards(shards):
    pieces = []
    for name in SMALL_NAMES:
        flat = shards[name].reshape(-1)
        if name == 'ffn_conv_w':
            pieces += [_seg(p, BF16_SEG) for p in _split3(flat)]
        else:
            pieces.append(flat.astype(BF16))
    return jnp.concatenate(pieces).reshape(-1, 128)


def _unpack_small(gathered):
    flat = gathered.reshape(N_DEV, -1)
    out, off = {}, 0
    for name in SMALL_NAMES:
        gs, ax = SHARDED[name]
        ls = _local_shape(name)
        n = int(np.prod(ls))
        if name == 'ffn_conv_w':
            nseg = _round_up(n, BF16_SEG)
            hi, mid, lo = (flat[:, off + j * nseg: off + j * nseg + n].astype(F32) for j in range(3))
            sh = ((hi + mid) + lo).reshape(N_DEV, 3, -1)
            out[name] = jnp.swapaxes(sh, 0, 1).reshape(3, D_FF)
            off += 3 * nseg
        else:
            sh = flat[:, off:off + n].reshape((N_DEV,) + ls[1:])
            out[name] = sh.reshape(gs[1:]) if ax == 1 else jnp.swapaxes(sh, 0, 1).reshape(gs[1:])
            off += n
    return out


def _small_pieces(sharded, repl):
    return [sharded[n].reshape(-1) for n in SMALL_NAMES] + [repl[n].reshape(-1) for n in REPL_NAMES]


def _pack_small_local(d):
    flat = jnp.concatenate(_small_pieces(d, d))
    return _seg(flat, F32_SEG).reshape(-1, 128)


def _pack_small_grads(G):
    pieces = []
    for name in SMALL_NAMES:
        gs, ax = SHARDED[name]
        g = G[name]
        if name == 'ffn_conv_w':
            sh = jnp.swapaxes(g.reshape(3, N_DEV, -1), 0, 1)
        elif ax == 1:
            sh = g
        else:
            sh = jnp.swapaxes(g.reshape(g.shape[0], N_DEV, -1), 0, 1)
        pieces.append(sh.reshape(N_DEV, -1))
    rep = jnp.concatenate([G[n].reshape(-1) for n in REPL_NAMES])
    pieces.append(jnp.broadcast_to(rep[None, :], (N_DEV, rep.shape[0])))
    flat = _seg(jnp.concatenate(pieces, axis=1), F32_SEG)
    return flat.reshape(N_DEV, -1, 128)


def _unpack_small_local(packed, local_shapes):
    flat = packed.reshape(-1)
    out, off = {}, 0
    for name in SMALL_NAMES + REPL_NAMES:
        n = int(np.prod(local_shapes[name]))
        out[name] = flat[off:off + n].reshape(local_shapes[name])
        off += n
    return out


def kernel(x, *rest):
    nw = len(WEIGHT_NAMES)
    assert len(rest) == 3 * nw + 1
    weights = dict(zip(WEIGHT_NAMES, rest[:nw]))
    loss_target = rest[nw]
    moms = dict(zip(WEIGHT_NAMES, rest[nw + 1:2 * nw + 1]))
    vars_ = dict(zip(WEIGHT_NAMES, rest[2 * nw + 1:]))
    local_shapes = {n: weights[n].shape for n in WEIGHT_NAMES}

    def native2d(name, a):
        a2 = a.reshape(a.shape[-2], a.shape[-1])
        return a2.T if name in BIG_T else a2

    def from2d(name, a2):
        return (a2.T if name in BIG_T else a2).reshape(local_shapes[name])

    big_w = {n: native2d(n, weights[n]) for n in BIG_NAMES}
    w_in_t_sh = big_w['w_in'].astype(BF16)
    late = [big_w[n].astype(BF16) for n in LATE_NAMES]
    small_sh = _pack_small_shards({n: weights[n] for n in SMALL_NAMES})
    w_in_g, small_g = _exchange("weights_all_gather", [w_in_t_sh, small_sh], False)
    W = _unpack_small(small_g)
    W['w_in_t'] = w_in_g.reshape(-1, D_MODEL)
    for n in REPL_NAMES:
        W[n] = weights[n][0] if n != 'norm_final_g' else weights[n]

    loss, dx, G, late_parts = _local_step(x[0], loss_target[0], W, late)

    w_in_parts, small_parts = _exchange(
        "grads_all_to_all", [G['w_in_t'].reshape(N_DEV, -1, D_MODEL), _pack_small_grads(G)], True)
    results = {}
    for n, parts in zip(['w_in'] + LATE_NAMES, [w_in_parts] + list(late_parts)):
        res = _adamw("adamw_" + n, parts, big_w[n], native2d(n, moms[n]), native2d(n, vars_[n]))
        results[n] = [from2d(n, r) for r in res]
    small_res = _adamw("adamw_small", small_parts, _pack_small_local(weights), _pack_small_local(moms),
                       _pack_small_local(vars_))
    small_out = [_unpack_small_local(p, local_shapes) for p in small_res]

    loss = lax.psum(loss, ("x", "y", "c"))
    outs = [loss, dx[None]]
    for j in range(4):
        outs += [results[n][j] if n in results else small_out[j][n] for n in WEIGHT_NAMES]
    return tuple(outs)
```

```python
import functools
import math

import numpy as np
import jax
import jax.numpy as jnp
from jax import lax
from jax.experimental import pallas as pl
from jax.experimental.pallas import tpu as pltpu

F32 = jnp.float32
BF16 = jnp.bfloat16

N_DEV = 8
D_MODEL = 1024
RWKV_HEADS = 8
RWKV_HEAD_DIM = 64
RWKV_WIDTH = 512
RET_HEADS = 4
RET_HEAD_DIM = 128
RET_WIDTH = 512
LORA_PAD = 128
D_FF = 2816
NORM_EPS = 1e-6
RWKV_GN_EPS = 64e-5
RET_GN_EPS = 1e-5
ROPE_BASE = 10000.0
ADAM_LR, ADAM_B1, ADAM_B2, ADAM_EPS, ADAM_WD, ADAM_STEP = 0.001, 0.9, 0.999, 1e-08, 0.01, 10

VMEM_LIMIT = 56 * 1024 * 1024
TOK_BLOCK = 256
SCAN_CHUNK = 64
ATT_BLOCK = 256
BF16_SEG = 2048
F32_SEG = 1024

WEIGHT_NAMES = ['norm_mix_g', 'w_in', 'rwkv_mu_r', 'rwkv_mu_k', 'rwkv_mu_v', 'rwkv_mu_w', 'rwkv_mu_a',
                'rwkv_mu_g', 'rwkv_w0', 'rwkv_w1', 'rwkv_w2', 'rwkv_a0', 'rwkv_a1', 'rwkv_a2', 'rwkv_g1',
                'rwkv_g2', 'rwkv_k_k', 'rwkv_k_a', 'rwkv_r_k', 'rwkv_lnx_w', 'rwkv_lnx_b', 'ret_gn_w',
                'w_out', 'norm_ffn_g', 'ffn_w_gate', 'ffn_w_up', 'ffn_conv_w', 'ffn_conv_b', 'ffn_w_down',
                'norm_final_g']
SHARDED = {
    'w_in': ((1, 1024, 3584), 2), 'rwkv_w1': ((1, 1024, 64), 1), 'rwkv_w2': ((1, 64, 512), 2),
    'rwkv_a1': ((1, 1024, 64), 1), 'rwkv_a2': ((1, 64, 512), 2), 'rwkv_g1': ((1, 1024, 128), 1),
    'rwkv_g2': ((1, 128, 512), 2), 'w_out': ((1, 1024, 1024), 1), 'ffn_w_gate': ((1, 1024, 2816), 2),
    'ffn_w_up': ((1, 1024, 2816), 2), 'ffn_conv_w': ((1, 3, 1, 2816), 3), 'ffn_w_down': ((1, 2816, 1024), 1),
}
REPL_NAMES = [n for n in WEIGHT_NAMES if n not in SHARDED]
BIG_NAMES = ['w_in', 'w_out', 'ffn_w_gate', 'ffn_w_up', 'ffn_w_down']
BIG_T = ('w_in', 'ffn_w_gate', 'ffn_w_up')
LATE_NAMES = ['w_out', 'ffn_w_gate', 'ffn_w_up', 'ffn_w_down']
SMALL_NAMES = [n for n in WEIGHT_NAMES if n in SHARDED and n not in BIG_NAMES]


def _cparams(n_grid):
    return pltpu.CompilerParams(dimension_semantics=("arbitrary",) * n_grid, vmem_limit_bytes=VMEM_LIMIT)


def _round_up(n, m):
    return (n + m - 1) // m * m


@jax.custom_vjp
def _bdot(x, w):
    return jnp.dot(x.astype(BF16), w.astype(BF16), preferred_element_type=F32)


def _bdot_fwd(x, w):
    return _bdot(x, w), (x, w)


def _bdot_bwd(res, g):
    x, w = res
    gb = g.astype(BF16)
    dx = lax.dot_general(gb, w.astype(BF16), (((1,), (1,)), ((), ())), preferred_element_type=F32)
    dw = lax.dot_general(x.astype(BF16), gb, (((0,), (0,)), ((), ())), preferred_element_type=F32)
    return dx, dw.astype(w.dtype)


_bdot.defvjp(_bdot_fwd, _bdot_bwd)


@jax.custom_vjp
def _shift_rows(x, prev):
    rolled = pltpu.roll(x, 1, 0)
    row = lax.broadcasted_iota(jnp.int32, x.shape, 0)
    return jnp.where(row == 0, jnp.broadcast_to(prev, x.shape), rolled)


def _shift_rows_fwd(x, prev):
    return _shift_rows(x, prev), None


def _shift_rows_bwd(_, g):
    n = g.shape[0]
    rolled = pltpu.roll(g, n - 1, 0)
    row = lax.broadcasted_iota(jnp.int32, g.shape, 0)
    return jnp.where(row == n - 1, 0.0, rolled), g[0:1, :]


_shift_rows.defvjp(_shift_rows_fwd, _shift_rows_bwd)


@jax.custom_vjp
def _swap_halves(x):
    return pltpu.roll(x, 64, 1)


_swap_halves.defvjp(lambda x: (_swap_halves(x), None), lambda _, g: (pltpu.roll(g, 64, 1),))


def _sigmoid(x):
    return 1.0 / (1.0 + jnp.exp(-x))


def _softplus(x):
    return jnp.maximum(x, 0.0) + jnp.log(1.0 + jnp.exp(-jnp.abs(x)))


def _rms_fn(x, g):
    return x * lax.rsqrt(jnp.mean(x * x, axis=-1, keepdims=True) + NORM_EPS) * g


def _pre_a_fn(h1, h1p, p, pp, mu_w, mu_a, mu_g, mu_r, mu_k, mu_v, w0, w1, w2, a0, a1, a2, g1, g2):
    W = RWKV_WIDTH
    h1s = _shift_rows(h1, h1p)
    ps = _shift_rows(p, pp)
    dx = h1s - h1
    xw = h1 + dx * mu_w
    xa = h1 + dx * mu_a
    xg = h1 + dx * mu_g
    dp = ps - p
    r = p[:, 0:W] + dp[:, 0:W] * mu_r
    k0 = p[:, W:2 * W] + dp[:, W:2 * W] * mu_k
    v = p[:, 2 * W:3 * W] + dp[:, 2 * W:3 * W] * mu_v
    wl = w0 + _bdot(jnp.tanh(_bdot(xw, w1)), w2)
    w_log = -_softplus(-wl) - 0.5
    lw = -jnp.exp(w_log)
    a = _sigmoid(a0 + _bdot(_bdot(xa, a1), a2))
    g = _bdot(_sigmoid(_bdot(xg, g1)), g2)
    return r, k0, v, lw, a, g


def _pre_b_fn(k0, a, k_k, k_a):
    kkr = k0 * k_k
    nrm = jnp.sqrt(jnp.sum(kkr * kkr, axis=-1, keepdims=True))
    kk = kkr / jnp.maximum(nrm, 1e-12)
    k = k0 * (1.0 + (a - 1.0) * k_a)
    return -kk, k, kk * a


def _rwkv_post_fn(y, r, k, v, g, lnx_w, lnx_b, r_k):
    mu = jnp.mean(y, axis=-1, keepdims=True)
    yc = y - mu
    var = jnp.mean(yc * yc, axis=-1, keepdims=True)
    yn = yc * lax.rsqrt(var + RWKV_GN_EPS) * lnx_w + lnx_b
    bonus = jnp.sum(r * k * r_k, axis=-1, keepdims=True) * v
    return ((yn + bonus) * g,)


def _rotary_fn(cos2, sin2, q, k):
    qs, ks = [], []
    for h in range(RET_HEADS):
        sl = slice(h * RET_HEAD_DIM, (h + 1) * RET_HEAD_DIM)
        qh, kh = q[:, sl], k[:, sl]
        qs.append(qh * cos2 + _swap_halves(qh) * sin2)
        ks.append((kh * cos2 + _swap_halves(kh) * sin2) * (RET_HEAD_DIM ** -0.5))
    return jnp.concatenate(qs, axis=1), jnp.concatenate(ks, axis=1)


def _ret_post_fn(y, gp, gn_w):
    outs = []
    for h in range(RET_HEADS):
        sl = slice(h * RET_HEAD_DIM, (h + 1) * RET_HEAD_DIM)
        yh = y[:, sl]
        mu = jnp.mean(yh, axis=-1, keepdims=True)
        yc = yh - mu
        var = jnp.mean(yc * yc, axis=-1, keepdims=True)
        outs.append(yc * lax.rsqrt(var + RET_GN_EPS) * gn_w[:, sl])
    yn = jnp.concatenate(outs, axis=1)
    return (gp * _sigmoid(gp) * yn,)


def _tok_axis(a):
    return 1 if a.ndim == 3 else 0


def _blk_spec(a, tb, rev_nb=None):
    nd, ax = a.ndim, _tok_axis(a)
    shape = a.shape[:ax] + (tb,) + a.shape[ax + 1:]

    def imap(i):
        idx = [0] * nd
        idx[ax] = i if rev_nb is None else rev_nb - 1 - i
        return tuple(idx)

    return pl.BlockSpec(shape, imap)


def _full_spec(a):
    nd = a.ndim
    return pl.BlockSpec(a.shape, lambda i: (0,) * nd)


def _tok_fwd(name, fn, toks, consts, out_tails, tb=TOK_BLOCK):
    n_in = len(toks) + len(consts)
    tn = toks[0].shape[_tok_axis(toks[0])]

    def body(*refs):
        outs = fn(*[r[...] for r in refs[:n_in]])
        for r, o in zip(refs[n_in:], outs):
            r[...] = o

    out_shape = [jax.ShapeDtypeStruct((tn,) + tuple(s) if len(s) == 1 else (s[0], tn, s[1]), F32) for s in out_tails]
    return pl.pallas_call(
        body, name=name, grid=(tn // tb,),
        in_specs=[_blk_spec(a, tb) for a in toks] + [_full_spec(c) for c in consts],
        out_specs=[_blk_spec(o, tb) for o in out_shape], out_shape=out_shape,
        compiler_params=_cparams(1))(*toks, *consts)


def _tok_bwd(name, fn, aux, toks, consts, cts, add=None, tb=TOK_BLOCK):
    n_aux, n_tok, n_c = len(aux), len(toks), len(consts)
    ct_groups = [c if isinstance(c, (tuple, list)) else (c,) for c in cts]
    ct_flat = [a for grp in ct_groups for a in grp]
    n_ct = len(ct_flat)
    n_add = 0 if add is None else 1
    tn = toks[0].shape[_tok_axis(toks[0])]

    def body(*refs):
        pos = 0
        aux_v = [r[...] for r in refs[pos:pos + n_aux]]; pos += n_aux
        tok_v = [r[...] for r in refs[pos:pos + n_tok]]; pos += n_tok
        const_v = [r[...] for r in refs[pos:pos + n_c]]; pos += n_c
        ct_refs = refs[pos:pos + n_ct]; pos += n_ct
        add_refs = refs[pos:pos + n_add]; pos += n_add
        dtok_refs = refs[pos:pos + n_tok]; pos += n_tok
        dconst_refs = refs[pos:pos + n_c]
        ct_v, q = [], 0
        for grp in ct_groups:
            s = ct_refs[q][...]
            for r in ct_refs[q + 1:q + len(grp)]:
                s = s + r[...]
            q += len(grp)
            ct_v.append(s)
        _, vjp = jax.vjp(lambda *tc: fn(*aux_v, *tc), *tok_v, *const_v)
        grads = vjp(tuple(ct_v))
        for j, r in enumerate(dtok_refs):
            gj = grads[j]
            if j == 0 and n_add:
                gj = gj + add_refs[0][...]
            r[...] = gj

        @pl.when(pl.program_id(0) == 0)
        def _():
            for r in dconst_refs:
                r[...] = jnp.zeros(r.shape, F32)

        for j, r in enumerate(dconst_refs):
            r[...] += grads[n_tok + j]

    ins = list(aux) + list(toks) + list(consts) + ct_flat + ([add] if n_add else [])
    in_specs = ([_blk_spec(a, tb) for a in aux] + [_blk_spec(a, tb) for a in toks] + [_full_spec(c) for c in consts]
                + [_blk_spec(a, tb) for a in ct_flat] + ([_blk_spec(add, tb)] if n_add else []))
    out_shape = [jax.ShapeDtypeStruct(a.shape, F32) for a in toks] + [jax.ShapeDtypeStruct(c.shape, F32) for c in consts]
    out_specs = [_blk_spec(a, tb) for a in toks] + [_full_spec(c) for c in consts]
    return pl.pallas_call(body, name=name, grid=(tn // tb,), in_specs=in_specs, out_specs=out_specs,
                          out_shape=out_shape, compiler_params=_cparams(1))(*ins)


MM_VMEM_BUDGET = 40 * 1024 * 1024
MM_STEP_SECONDS = 0.4e-6
MM_HBM_BYTES_PER_SECOND = 2.5e12


def _mm_tiles(m, n, kd, a_bytes, b_bytes, o_bytes, has_add):
    divs = lambda d: [t for t in range(128, d + 1, 128) if d % t == 0]
    best = None
    for tm in divs(m):
        for tn in divs(n):
            for tk in divs(kd):
                ni, nj, nk = m // tm, n // tn, kd // tk
                vmem = (2 * tm * tk * a_bytes + 2 * tk * tn * b_bytes + tm * tn * 4 + 2 * tm * tn * o_bytes
                        + (2 * tm * tn * 4 if has_add else 0) + 2 * (tm * tk + tk * tn) + tm * tn * 4)
                if vmem > MM_VMEM_BUDGET:
                    continue
                a_traffic = m * kd * a_bytes * (nj if nk > 1 else 1)
                b_traffic = kd * n * b_bytes * (ni if nj * nk > 1 else 1)
                cost = ni * nj * nk * MM_STEP_SECONDS + (a_traffic + b_traffic) / MM_HBM_BYTES_PER_SECOND
                if best is None or cost < best[0]:
                    best = (cost, tm, tn, tk)
    return best[1:]


def _mm(name, a, b, ta=False, tb=False, add=None, out_dtype=F32):
    if ta:
        kd, m = a.shape
    else:
        m, kd = a.shape
    if tb:
        n, kb = b.shape
    else:
        kb, n = b.shape
    assert kd == kb, (a.shape, b.shape)
    tm, tn, tk = _mm_tiles(m, n, kd, a.dtype.itemsize, b.dtype.itemsize, jnp.dtype(out_dtype).itemsize,
                           add is not None)
    nk = kd // tk
    has_add = add is not None
    dims = (((0 if ta else 1,), (1 if tb else 0,)), ((), ()))

    def body(*refs):
        a_ref, b_ref = refs[0], refs[1]
        o_ref, acc_ref = refs[-2], refs[-1]
        k = pl.program_id(2)

        @pl.when(k == 0)
        def _():
            acc_ref[...] = refs[2][...] if has_add else jnp.zeros(acc_ref.shape, F32)

        acc_ref[...] += lax.dot_general(a_ref[...].astype(BF16), b_ref[...].astype(BF16), dims,
                                        preferred_element_type=F32)

        @pl.when(k == nk - 1)
        def _():
            o_ref[...] = acc_ref[...].astype(out_dtype)

    a_spec = pl.BlockSpec((tk, tm), lambda i, j, k: (k, i)) if ta else pl.BlockSpec((tm, tk), lambda i, j, k: (i, k))
    b_spec = pl.BlockSpec((tn, tk), lambda i, j, k: (j, k)) if tb else pl.BlockSpec((tk, tn), lambda i, j, k: (k, j))
    o_spec = pl.BlockSpec((tm, tn), lambda i, j, k: (i, j))
    ins = [a, b] + ([add] if has_add else [])
    in_specs = [a_spec, b_spec] + ([o_spec] if has_add else [])
    return pl.pallas_call(body, name=name, grid=(m // tm, n // tn, nk), in_specs=in_specs, out_specs=o_spec,
                          out_shape=jax.ShapeDtypeStruct((m, n), out_dtype),
                          scratch_shapes=[pltpu.VMEM((tm, tn), F32)], compiler_params=_cparams(3))(*ins)


def _prev8_spec(a, tb, rev_nb=None):
    r = tb // 8
    if rev_nb is None:
        return pl.BlockSpec((8, a.shape[1]), lambda i: (jnp.maximum(i * r - 1, 0), 0))
    return pl.BlockSpec((8, a.shape[1]), lambda i: (jnp.maximum((rev_nb - 1 - i) * r - 1, 0), 0))


def _pre_a_fwd(h1, p, consts, tb=TOK_BLOCK):
    tn = h1.shape[0]

    def body(h1_ref, h1h_ref, p_ref, ph_ref, *rest):
        c_refs, o_refs = rest[:len(consts)], rest[len(consts):]
        first = pl.program_id(0) == 0
        h1p = jnp.where(first, 0.0, h1h_ref[7:8, :])
        pp = jnp.where(first, 0.0, ph_ref[7:8, :])
        outs = _pre_a_fn(h1_ref[...], h1p, p_ref[...], pp, *[c[...] for c in c_refs])
        for r, o in zip(o_refs, outs):
            r[...] = o

    out_shape = [jax.ShapeDtypeStruct((tn, RWKV_WIDTH), F32) for _ in range(6)]
    return pl.pallas_call(
        body, name="rwkv_pre_a_fwd", grid=(tn // tb,),
        in_specs=[_blk_spec(h1, tb), _prev8_spec(h1, tb), _blk_spec(p, tb), _prev8_spec(p, tb)]
        + [_full_spec(c) for c in consts],
        out_specs=[_blk_spec(o, tb) for o in out_shape], out_shape=out_shape,
        compiler_params=_cparams(1))(h1, h1, p, p, *consts)


def _pre_a_bwd(h1, p, consts, cts, tb=TOK_BLOCK):
    tn = h1.shape[0]
    nb = tn // tb
    n_c = len(consts)
    ct_groups = [c if isinstance(c, (tuple, list)) else (c,) for c in cts]
    ct_flat = [a for grp in ct_groups for a in grp]
    n_ct = len(ct_flat)

    def body(*refs):
        h1_ref, h1h_ref, p_ref, ph_ref = refs[:4]
        c_refs = refs[4:4 + n_c]
        ct_refs = refs[4 + n_c:4 + n_c + n_ct]
        dh1_ref, dp_ref = refs[4 + n_c + n_ct:6 + n_c + n_ct]
        dc_refs = refs[6 + n_c + n_ct:6 + 2 * n_c + n_ct]
        ch_ref, cp_ref = refs[-2], refs[-1]
        i = pl.program_id(0)
        first_block = i == nb - 1
        h1p = jnp.where(first_block, 0.0, h1h_ref[7:8, :])
        pp = jnp.where(first_block, 0.0, ph_ref[7:8, :])
        ct_v, q = [], 0
        for grp in ct_groups:
            s = ct_refs[q][...]
            for r in ct_refs[q + 1:q + len(grp)]:
                s = s + r[...]
            q += len(grp)
            ct_v.append(s)
        _, vjp = jax.vjp(_pre_a_fn, h1_ref[...], h1p, p_ref[...], pp, *[c[...] for c in c_refs])
        grads = vjp(tuple(ct_v))

        @pl.when(i == 0)
        def _():
            ch_ref[...] = jnp.zeros(ch_ref.shape, F32)
            cp_ref[...] = jnp.zeros(cp_ref.shape, F32)
            for r in dc_refs:
                r[...] = jnp.zeros(r.shape, F32)

        rowh = lax.broadcasted_iota(jnp.int32, (tb, h1.shape[1]), 0)
        rowp = lax.broadcasted_iota(jnp.int32, (tb, p.shape[1]), 0)
        dh1_ref[...] = grads[0] + jnp.where(rowh == tb - 1, jnp.broadcast_to(ch_ref[0:1, :], rowh.shape), 0.0)
        dp_ref[...] = grads[2] + jnp.where(rowp == tb - 1, jnp.broadcast_to(cp_ref[0:1, :], rowp.shape), 0.0)
        ch_ref[0:1, :] = grads[1]
        cp_ref[0:1, :] = grads[3]
        for j, r in enumerate(dc_refs):
            r[...] += grads[4 + j]

    ins = [h1, h1, p, p] + list(consts) + ct_flat
    in_specs = ([_blk_spec(h1, tb, nb), _prev8_spec(h1, tb, nb), _blk_spec(p, tb, nb), _prev8_spec(p, tb, nb)]
                + [_full_spec(c) for c in consts] + [_blk_spec(a, tb, nb) for a in ct_flat])
    out_shape = ([jax.ShapeDtypeStruct(h1.shape, F32), jax.ShapeDtypeStruct(p.shape, F32)]
                 + [jax.ShapeDtypeStruct(c.shape, F32) for c in consts])
    out_specs = [_blk_spec(h1, tb, nb), _blk_spec(p, tb, nb)] + [_full_spec(c) for c in consts]
    return pl.pallas_call(body, name="rwkv_pre_a_bwd", grid=(nb,), in_specs=in_specs, out_specs=out_specs,
                          out_shape=out_shape,
                          scratch_shapes=[pltpu.VMEM((8, h1.shape[1]), F32), pltpu.VMEM((8, p.shape[1]), F32)],
                          compiler_params=_cparams(1))(*ins)


def _my_index():
    return 4 * lax.axis_index("x") + 2 * lax.axis_index("y") + lax.axis_index("c")


def _peer(k):
    x, y, c = lax.axis_index("x"), lax.axis_index("y"), lax.axis_index("c")
    px = 1 - x if k & 4 else x
    py = 1 - y if k & 2 else y
    pc = 1 - c if k & 1 else c
    return (px, py, pc), 4 * px + 2 * py + pc


def _xchg_sems(n):
    return [pltpu.SemaphoreType.DMA((n * (N_DEV - 1),)), pltpu.SemaphoreType.DMA((n * (N_DEV - 1),)),
            pltpu.SemaphoreType.DMA((n,))]


def _xchg_copies(srcs, dsts, sems, scatter, incoming=False):
    send_sems, recv_sems, local_sems = sems
    me = _my_index()
    local, remote = [], []
    for i, (s, d) in enumerate(zip(srcs, dsts)):
        if not incoming:
            local.append(pltpu.make_async_copy(s.at[me] if scatter else s, d.at[me], local_sems.at[i]))
        for k in range(1, N_DEV):
            peer, plin = _peer(k)
            j = i * (N_DEV - 1) + k - 1
            s_slot, d_slot = (me, plin) if incoming else (plin, me)
            remote.append(pltpu.make_async_remote_copy(
                src_ref=s.at[s_slot] if scatter else s, dst_ref=d.at[d_slot], send_sem=send_sems.at[j],
                recv_sem=recv_sems.at[j], device_id=peer, device_id_type=pl.DeviceIdType.MESH))
    return local, remote


def _xchg_start(srcs, dsts, sems, scatter):
    local, out = _xchg_copies(srcs, dsts, sems, scatter)
    for cp in local + out:
        cp.start()


def _xchg_wait(srcs, dsts, sems, scatter):
    for cp in _xchg_copies(srcs, dsts, sems, scatter, incoming=True)[1]:
        cp.wait_recv()
    local, out = _xchg_copies(srcs, dsts, sems, scatter)
    for cp in out:
        cp.wait_send()
    for cp in local:
        cp.wait()


def _xchg_out_shapes(srcs, scatter):
    return [jax.ShapeDtypeStruct(s.shape if scatter else (N_DEV,) + s.shape, s.dtype) for s in srcs]


_ANY = pl.BlockSpec(memory_space=pl.ANY)


def _exchange(name, srcs, scatter):
    n = len(srcs)

    def body(*refs):
        s, d, sems = refs[:n], refs[n:2 * n], refs[2 * n:]
        _xchg_start(s, d, sems, scatter)
        _xchg_wait(s, d, sems, scatter)

    return pl.pallas_call(body, name=name, in_specs=[_ANY] * n, out_specs=[_ANY] * n,
                          out_shape=_xchg_out_shapes(srcs, scatter), scratch_shapes=_xchg_sems(n))(*srcs)


_HI = lax.Precision.HIGHEST
_MM_DIMS = {'nn': (((1,), (0,)), ((), ())), 'nt': (((1,), (1,)), ((), ())), 'tn': (((0,), (0,)), ((), ()))}


def _cmm_raw(x, y, kind):
    return lax.dot_general(x.astype(BF16), y.astype(BF16), _MM_DIMS[kind], preferred_element_type=F32)


@functools.partial(jax.custom_vjp, nondiff_argnums=(2,))
def _cmm(x, y, kind):
    return _cmm_raw(x, y, kind)


def _cmm_fwd(x, y, kind):
    return _cmm_raw(x, y, kind), (x, y)


def _cmm_bwd(kind, res, g):
    x, y = res
    if kind == 'nn':
        return _cmm_raw(g, y, 'nt'), _cmm_raw(x, g, 'tn')
    if kind == 'nt':
        return _cmm_raw(g, y, 'nn'), _cmm_raw(g, x, 'tn')
    return _cmm_raw(y, g, 'nt'), _cmm_raw(x, g, 'nn')


_cmm.defvjp(_cmm_fwd, _cmm_bwd)


def _chunk_fn(S0, r, lw, k, v, a, b):
    hs = range(len(r))
    C = r[0].shape[0]
    ii = lax.broadcasted_iota(jnp.int32, (C, C), 0)
    jj = lax.broadcasted_iota(jnp.int32, (C, C), 1)
    incl, strict = ii >= jj, ii > jj
    eye = (ii == jj).astype(F32)
    inclf = incl.astype(F32)
    cum = [jnp.dot(inclf, lw[h], precision=_HI) for h in hs]
    e_inv = [jnp.exp(-cum[h]) for h in hs]
    At = [a[h] * jnp.exp(cum[h] - lw[h]) for h in hs]
    Rt = [r[h] * jnp.exp(cum[h]) for h in hs]
    Kh = [k[h] * e_inv[h] for h in hs]
    Bh = [b[h] * e_inv[h] for h in hs]
    Mab = [jnp.where(strict, _cmm(At[h], Bh[h], 'nt'), 0.0) for h in hs]
    Mak = [jnp.where(strict, _cmm(At[h], Kh[h], 'nt'), 0.0) for h in hs]
    Mrk = [jnp.where(incl, _cmm(Rt[h], Kh[h], 'nt'), 0.0) for h in hs]
    Mrb = [jnp.where(incl, _cmm(Rt[h], Bh[h], 'nt'), 0.0) for h in hs]
    rhs = [_cmm(At[h], S0[h], 'nt') + _cmm(Mak[h], v[h], 'nn') for h in hs]
    P = Mab
    Tm = [eye + P[h] for h in hs]
    n = 1
    while 2 * n < C:
        P = [jnp.dot(P[h], P[h], precision=_HI) for h in hs]
        Tm = [jnp.dot(Tm[h], eye + P[h], precision=_HI) for h in hs]
        n *= 2
    U = [jnp.dot(Tm[h], rhs[h], precision=_HI) for h in hs]
    Y = [_cmm(Rt[h], S0[h], 'nt') + _cmm(Mrk[h], v[h], 'nn') + _cmm(Mrb[h], U[h], 'nn') for h in hs]
    gC = [jnp.exp(jnp.sum(lw[h], axis=0, keepdims=True)) for h in hs]
    SC = [S0[h] * gC[h] + _cmm(v[h], Kh[h] * gC[h], 'tn') + _cmm(U[h], Bh[h] * gC[h], 'tn') for h in hs]
    return tuple(Y), tuple(SC)


def _cscan_fwd(r, lw, k, v, a, b, xs):
    n_x = len(xs)
    H, tn, Dh = r.shape
    nc = tn // SCAN_CHUNK

    def body(r_ref, lw_ref, k_ref, v_ref, a_ref, b_ref, *rest):
        x_src, (y_ref, ck_ref) = rest[:n_x], rest[n_x:n_x + 2]
        x_dst, s_ref, sems = rest[n_x + 2:2 * n_x + 2], rest[2 * n_x + 2], rest[2 * n_x + 3:]

        @pl.when(pl.program_id(0) == 0)
        def _():
            s_ref[...] = jnp.zeros(s_ref.shape, F32)
            _xchg_start(x_src, x_dst, sems, False)

        ck_ref[0] = s_ref[...]
        heads = lambda ref: tuple(ref[h] for h in range(H))
        y, sc = _chunk_fn(heads(s_ref), heads(r_ref), heads(lw_ref), heads(k_ref), heads(v_ref), heads(a_ref),
                          heads(b_ref))
        for h in range(H):
            y_ref[h] = y[h]
            s_ref[h] = sc[h]

        @pl.when(pl.program_id(0) == nc - 1)
        def _():
            _xchg_wait(x_src, x_dst, sems, False)

    hm = pl.BlockSpec((H, SCAN_CHUNK, Dh), lambda c: (0, c, 0))
    res = pl.pallas_call(
        body, name="rwkv_scan_fwd", grid=(nc,), in_specs=[hm] * 6 + [_ANY] * n_x,
        out_specs=[hm, pl.BlockSpec((1, H, Dh, Dh), lambda c: (c, 0, 0, 0))] + [_ANY] * n_x,
        out_shape=[jax.ShapeDtypeStruct((H, tn, Dh), F32), jax.ShapeDtypeStruct((nc, H, Dh, Dh), F32)]
        + _xchg_out_shapes(xs, False),
        scratch_shapes=[pltpu.VMEM((H, Dh, Dh), F32)] + _xchg_sems(n_x),
        compiler_params=_cparams(1))(r, lw, k, v, a, b, *xs)
    return res[0], res[1], res[2:]


def _cscan_bwd(r, lw, k, v, a, b, dy, ck, xs):
    n_x = len(xs)
    H, tn, Dh = r.shape
    nc = tn // SCAN_CHUNK

    def body(r_ref, lw_ref, k_ref, v_ref, a_ref, b_ref, dy_ref, ck_ref, *rest):
        x_src = rest[:n_x]
        d_refs = rest[n_x:n_x + 6]
        x_dst = rest[n_x + 6:2 * n_x + 6]
        g_ref = rest[2 * n_x + 6]
        sems = rest[2 * n_x + 7:]

        @pl.when(pl.program_id(0) == 0)
        def _():
            g_ref[...] = jnp.zeros(g_ref.shape, F32)
            _xchg_start(x_src, x_dst, sems, True)

        heads = lambda ref: tuple(ref[h] for h in range(H))
        s0 = tuple(ck_ref[0, h] for h in range(H))
        _, vjp = jax.vjp(_chunk_fn, s0, heads(r_ref), heads(lw_ref), heads(k_ref), heads(v_ref), heads(a_ref),
                         heads(b_ref))
        grads = vjp((heads(dy_ref), heads(g_ref)))
        for h in range(H):
            g_ref[h] = grads[0][h]
            for d_ref, gz in zip(d_refs, grads[1:]):
                d_ref[h] = gz[h]

        @pl.when(pl.program_id(0) == nc - 1)
        def _():
            _xchg_wait(x_src, x_dst, sems, True)

    hm = pl.BlockSpec((H, SCAN_CHUNK, Dh), lambda c: (0, nc - 1 - c, 0))
    hshape = jax.ShapeDtypeStruct((H, tn, Dh), F32)
    res = pl.pallas_call(
        body, name="rwkv_scan_bwd", grid=(nc,),
        in_specs=[hm] * 7 + [pl.BlockSpec((1, H, Dh, Dh), lambda c: (nc - 1 - c, 0, 0, 0))] + [_ANY] * n_x,
        out_specs=[hm] * 6 + [_ANY] * n_x, out_shape=[hshape] * 6 + _xchg_out_shapes(xs, True),
        scratch_shapes=[pltpu.VMEM((H, Dh, Dh), F32)] + _xchg_sems(n_x),
        compiler_params=_cparams(1))(r, lw, k, v, a, b, dy, ck, *xs)
    return res[:6], res[6:]


def _decay_mask(lg, i, j, blk):
    rows = lax.broadcasted_iota(jnp.int32, (blk, blk), 0)
    cols = lax.broadcasted_iota(jnp.int32, (blk, blk), 1)
    dd = (rows - cols + (i - j) * blk).astype(F32)
    return jnp.where(dd >= 0.0, jnp.exp(lg * jnp.maximum(dd, 0.0)), 0.0)


_NT = (((1,), (1,)), ((), ()))
_TN = (((0,), (0,)), ((), ()))


def _ret_attn_fwd(lg, q, k, v, blk=ATT_BLOCK):
    tn = q.shape[0]
    Dh = RET_HEAD_DIM

    def body(lg_ref, q_ref, k_ref, v_ref, o_ref):
        i = pl.program_id(1)
        lgv = lg_ref[0][:, 0:1]
        qb = q_ref[...].astype(BF16)

        def jb(j, acc):
            ks = pl.ds(pl.multiple_of(j * blk, blk), blk)
            s = lax.dot_general(qb, k_ref[ks, :].astype(BF16), _NT, preferred_element_type=F32)
            s = s * _decay_mask(lgv, i, j, blk)
            return acc + jnp.dot(s.astype(BF16), v_ref[ks, :].astype(BF16), preferred_element_type=F32)

        o_ref[...] = lax.fori_loop(0, i + 1, jb, jnp.zeros((blk, Dh), F32))

    full = pl.BlockSpec((tn, Dh), lambda h, i: (0, h))
    qs = pl.BlockSpec((blk, Dh), lambda h, i: (i, h))
    return pl.pallas_call(
        body, name="ret_attn_fwd", grid=(RET_HEADS, tn // blk),
        in_specs=[pl.BlockSpec((1, 1, 128), lambda h, i: (h, 0, 0)), qs, full, full],
        out_specs=qs, out_shape=jax.ShapeDtypeStruct(q.shape, F32), compiler_params=_cparams(2))(lg, q, k, v)


def _ret_attn_bwd(lg, q, k, v, do, blk=ATT_BLOCK):
    tn = q.shape[0]
    nb = tn // blk
    Dh = RET_HEAD_DIM

    def body(lg_ref, q_ref, k_ref, v_ref, do_ref, dq_ref, dk_ref, dv_ref):
        lgv = lg_ref[0][:, 0:1]
        dk_ref[...] = jnp.zeros(dk_ref.shape, F32)
        dv_ref[...] = jnp.zeros(dv_ref.shape, F32)

        def ib(i, carry):
            qs = pl.ds(pl.multiple_of(i * blk, blk), blk)
            qb = q_ref[qs, :].astype(BF16)
            dob = do_ref[qs, :].astype(BF16)

            def jb(j, dq):
                ks = pl.ds(pl.multiple_of(j * blk, blk), blk)
                kb = k_ref[ks, :].astype(BF16)
                vb = v_ref[ks, :].astype(BF16)
                dm = _decay_mask(lgv, i, j, blk)
                s = lax.dot_general(qb, kb, _NT, preferred_element_type=F32) * dm
                ds = lax.dot_general(dob, vb, _NT, preferred_element_type=F32) * dm
                sb, dsb = s.astype(BF16), ds.astype(BF16)
                dv_ref[ks, :] += lax.dot_general(sb, dob, _TN, preferred_element_type=F32)
                dk_ref[ks, :] += lax.dot_general(dsb, qb, _TN, preferred_element_type=F32)
                return dq + jnp.dot(dsb, kb, preferred_element_type=F32)

            dq_ref[qs, :] = lax.fori_loop(0, i + 1, jb, jnp.zeros((blk, Dh), F32))
            return carry

        lax.fori_loop(0, nb, ib, 0)

    full = pl.BlockSpec((tn, Dh), lambda h: (0, h))
    sh = jax.ShapeDtypeStruct(q.shape, F32)
    return pl.pallas_call(
        body, name="ret_attn_bwd", grid=(RET_HEADS,),
        in_specs=[pl.BlockSpec((1, 1, 128), lambda h: (h, 0, 0)), full, full, full, full],
        out_specs=[full, full, full], out_shape=[sh, sh, sh], compiler_params=_cparams(1))(lg, q, k, v, do)


def _next8_spec(a, tb):
    r = tb // 8
    last = a.shape[0] // 8 - 1
    return pl.BlockSpec((8, a.shape[1]), lambda i: (jnp.minimum((i + 1) * r, last), 0))


def _conv_taps(g_ext, cw_ref, cb_ref):
    return (cw_ref[2:3, :] * g_ext + cw_ref[1:2, :] * pltpu.roll(g_ext, 1, 0)
            + cw_ref[0:1, :] * pltpu.roll(g_ext, 2, 0) + cb_ref[...])


def _glu_fwd(gate, up, cw, cb, tb=TOK_BLOCK):
    tn = gate.shape[0]

    def body(g_ref, gh_ref, u_ref, cw_ref, cb_ref, o_ref):
        halo = jnp.where(pl.program_id(0) == 0, 0.0, gh_ref[...])
        g_ext = jnp.concatenate([halo, g_ref[...]], axis=0)
        gc = _conv_taps(g_ext, cw_ref, cb_ref)[8:, :]
        o_ref[...] = gc * _sigmoid(gc) * u_ref[...]

    return pl.pallas_call(
        body, name="glu_fwd", grid=(tn // tb,),
        in_specs=[_blk_spec(gate, tb), _prev8_spec(gate, tb), _blk_spec(up, tb), _full_spec(cw), _full_spec(cb)],
        out_specs=_blk_spec(gate, tb), out_shape=jax.ShapeDtypeStruct(gate.shape, F32),
        compiler_params=_cparams(1))(gate, gate, up, cw, cb)


def _glu_bwd(gate, up, dact, cw, cb, tb=TOK_BLOCK):
    tn = gate.shape[0]
    nb = tn // tb

    def body(g_ref, gp_ref, gn_ref, u_ref, un_ref, d_ref, dn_ref, cw_ref, cb_ref, dg_ref, du_ref, dcw_ref, dcb_ref):
        i = pl.program_id(0)
        gprev = jnp.where(i == 0, 0.0, gp_ref[...])
        dnext = jnp.where(i == nb - 1, 0.0, dn_ref[...])
        g_ext = jnp.concatenate([gprev, g_ref[...], gn_ref[...]], axis=0)
        gc = _conv_taps(g_ext, cw_ref, cb_ref)[8:, :]
        u_e = jnp.concatenate([u_ref[...], un_ref[...]], axis=0)
        d_e = jnp.concatenate([d_ref[...], dnext], axis=0)
        s = _sigmoid(gc)
        dgc = d_e * u_e * (s * (1.0 + gc * (1.0 - s)))
        du_ref[...] = d_ref[...] * (gc * s)[:tb, :]
        n_e = tb + 8
        dg_ref[...] = (cw_ref[2:3, :] * dgc + cw_ref[1:2, :] * pltpu.roll(dgc, n_e - 1, 0)
                       + cw_ref[0:1, :] * pltpu.roll(dgc, n_e - 2, 0))[:tb, :]

        @pl.when(i == 0)
        def _():
            dcw_ref[...] = jnp.zeros(dcw_ref.shape, F32)
            dcb_ref[...] = jnp.zeros(dcb_ref.shape, F32)

        dgc_b = dgc[:tb, :]
        g0 = g_ext[8:8 + tb, :]
        g1 = pltpu.roll(g_ext, 1, 0)[8:8 + tb, :]
        g2 = pltpu.roll(g_ext, 2, 0)[8:8 + tb, :]
        dcw_ref[2:3, :] += jnp.sum(dgc_b * g0, axis=0, keepdims=True)
        dcw_ref[1:2, :] += jnp.sum(dgc_b * g1, axis=0, keepdims=True)
        dcw_ref[0:1, :] += jnp.sum(dgc_b * g2, axis=0, keepdims=True)
        dcb_ref[...] += jnp.sum(dgc_b, axis=0, keepdims=True)

    sh = jax.ShapeDtypeStruct(gate.shape, F32)
    return pl.pallas_call(
        body, name="glu_bwd", grid=(nb,),
        in_specs=[_blk_spec(gate, tb), _prev8_spec(gate, tb), _next8_spec(gate, tb), _blk_spec(up, tb),
                  _next8_spec(up, tb), _blk_spec(dact, tb), _next8_spec(dact, tb), _full_spec(cw), _full_spec(cb)],
        out_specs=[_blk_spec(gate, tb), _blk_spec(gate, tb), _full_spec(cw), _full_spec(cb)],
        out_shape=[sh, sh, jax.ShapeDtypeStruct(cw.shape, F32), jax.ShapeDtypeStruct(cb.shape, F32)],
        compiler_params=_cparams(1))(gate, gate, gate, up, up, dact, dact, cw, cb)


def _final_loss(x2, tgt, g, tb=TOK_BLOCK):
    tn, dm = x2.shape

    def body(x_ref, t_ref, g_ref, l_ref, dx_ref, dg_ref):
        y, vjp = jax.vjp(_rms_fn, x_ref[...], g_ref[...])
        err = y - t_ref[...]
        dx, dg = vjp(err * (1.0 / dm))

        @pl.when(pl.program_id(0) == 0)
        def _():
            l_ref[...] = jnp.zeros(l_ref.shape, F32)
            dg_ref[...] = jnp.zeros(dg_ref.shape, F32)

        part = 0.5 * jnp.sum(jnp.mean(err * err, axis=-1, keepdims=True), axis=0, keepdims=True)
        l_ref[...] += jnp.broadcast_to(part, l_ref.shape)
        dx_ref[...] = dx
        dg_ref[...] += dg

    return pl.pallas_call(
        body, name="final_loss", grid=(tn // tb,),
        in_specs=[_blk_spec(x2, tb), _blk_spec(tgt, tb), _full_spec(g)],
        out_specs=[pl.BlockSpec((8, 128), lambda i: (0, 0)), _blk_spec(x2, tb), _full_spec(g)],
        out_shape=[jax.ShapeDtypeStruct((8, 128), F32), jax.ShapeDtypeStruct(x2.shape, F32),
                   jax.ShapeDtypeStruct(g.shape, F32)],
        compiler_params=_cparams(1))(x2, tgt, g)


def _pad_cols(w, n):
    return jnp.pad(w, ((0, 0), (0, n - w.shape[1])))


def _pad_rows(w, n):
    return jnp.pad(w, ((0, n - w.shape[0]), (0, 0)))


def _to_heads(z):
    return jnp.swapaxes(z.reshape(z.shape[0], RWKV_HEADS, RWKV_HEAD_DIM), 0, 1)


def _to_flat(z):
    return jnp.swapaxes(z, 0, 1).reshape(z.shape[1], RWKV_WIDTH)


def _head_param(p):
    return p.reshape(RWKV_HEADS, 1, RWKV_HEAD_DIM)


def _local_step(x, tgt, W, late):
    tn = x.shape[0]
    Wd = RWKV_WIDTH
    row = lambda z: z.reshape(1, -1)
    g_mix, g_ffn, g_fin = row(W['norm_mix_g']), row(W['norm_ffn_g']), row(W['norm_final_g'])

    (h1,) = _tok_fwd("norm_mix_fwd", lambda a, g: (_rms_fn(a, g),), [x], [g_mix], [(D_MODEL,)])
    proj = _mm("proj_fwd", h1, W['w_in_t'], tb=True)
    p_rkv = proj[:, :3 * Wd]
    pre_consts = [row(W['rwkv_mu_w']), row(W['rwkv_mu_a']), row(W['rwkv_mu_g']), row(W['rwkv_mu_r']),
                  row(W['rwkv_mu_k']), row(W['rwkv_mu_v']), row(W['rwkv_w0']),
                  _pad_cols(W['rwkv_w1'], LORA_PAD), _pad_rows(W['rwkv_w2'], LORA_PAD), row(W['rwkv_a0']),
                  _pad_cols(W['rwkv_a1'], LORA_PAD), _pad_rows(W['rwkv_a2'], LORA_PAD),
                  W['rwkv_g1'], W['rwkv_g2']]
    r, k0, v, lw, a, g = _pre_a_fwd(h1, p_rkv, pre_consts)
    k_k, k_a = _head_param(W['rwkv_k_k']), _head_param(W['rwkv_k_a'])
    k0h, ah = _to_heads(k0), _to_heads(a)
    nkk, kh, bh = _tok_fwd("rwkv_pre_b_fwd", _pre_b_fn, [k0h, ah], [k_k, k_a], [(RWKV_HEADS, RWKV_HEAD_DIM)] * 3)
    rh, lwh, vh, gh = _to_heads(r), _to_heads(lw), _to_heads(v), _to_heads(g)
    yh, ck, gathered = _cscan_fwd(rh, lwh, kh, vh, nkk, bh, late)
    w_out, w_gate_t, w_up_t, w_down = [g_.reshape(-1, D_MODEL) for g_ in gathered]
    post_consts = [_head_param(W['rwkv_lnx_w']), _head_param(W['rwkv_lnx_b']), _head_param(W['rwkv_r_k'])]
    (y_rwkv_h,) = _tok_fwd("rwkv_post_fwd", _rwkv_post_fn, [yh, rh, kh, vh, gh], post_consts,
                           [(RWKV_HEADS, RWKV_HEAD_DIM)])
    y_rwkv = _to_flat(y_rwkv_h)

    pos = jnp.arange(tn, dtype=F32)
    half = RET_HEAD_DIM // 2
    inv_freq = ROPE_BASE ** (-jnp.arange(half, dtype=F32) / half)
    ang = pos[:, None] * inv_freq[None, :]
    cos2 = jnp.concatenate([jnp.cos(ang), jnp.cos(ang)], axis=1)
    sin2 = jnp.concatenate([-jnp.sin(ang), jnp.sin(ang)], axis=1)
    lg = jnp.log(1.0 - 2.0 ** (-5.0 - jnp.arange(RET_HEADS, dtype=F32)))
    lg = jnp.broadcast_to(lg[:, None, None], (RET_HEADS, 1, 128))
    q_p, k_p = proj[:, 3 * Wd:4 * Wd], proj[:, 4 * Wd:5 * Wd]
    v_ret, g_ret = proj[:, 5 * Wd:6 * Wd], proj[:, 6 * Wd:7 * Wd]
    q_rot, k_rot = _tok_fwd("ret_rotary_fwd", _rotary_fn, [cos2, sin2, q_p, k_p], [], [(RET_WIDTH,)] * 2)
    y_ret_raw = _ret_attn_fwd(lg, q_rot, k_rot, v_ret)
    gn_w = row(W['ret_gn_w'])
    (y_ret,) = _tok_fwd("ret_post_fwd", _ret_post_fn, [y_ret_raw, g_ret], [gn_w], [(RET_WIDTH,)])

    ycat = jnp.concatenate([y_rwkv, y_ret], axis=1)
    x1 = _mm("out_proj_fwd", ycat, w_out, add=x)
    (h2,) = _tok_fwd("norm_ffn_fwd", lambda a_, g_: (_rms_fn(a_, g_),), [x1], [g_ffn], [(D_MODEL,)])
    gate = _mm("ffn_gate_fwd", h2, w_gate_t, tb=True)
    up = _mm("ffn_up_fwd", h2, w_up_t, tb=True)
    cw = W['ffn_conv_w']
    cb = row(W['ffn_conv_b'])
    act = _glu_fwd(gate, up, cw, cb)
    x2 = _mm("ffn_down_fwd", act, w_down, add=x1)
    loss8, dx2, dg_fin = _final_loss(x2, tgt, g_fin)

    G = {'norm_final_g': dg_fin}
    dact = _mm("ffn_down_dx", dx2, w_down, tb=True)
    d_down = _mm("ffn_down_dw", act, dx2, ta=True, out_dtype=BF16)
    dgate, dup, dcw, dcb = _glu_bwd(gate, up, dact, cw, cb)
    G['ffn_conv_w'], G['ffn_conv_b'] = dcw, dcb
    dh2 = _mm("ffn_gate_dx", dgate, w_gate_t)
    dh2 = _mm("ffn_up_dx", dup, w_up_t, add=dh2)
    d_gate_t = _mm("ffn_gate_dw", dgate, h2, ta=True, out_dtype=BF16)
    d_up_t = _mm("ffn_up_dw", dup, h2, ta=True, out_dtype=BF16)
    dx1, G['norm_ffn_g'] = _tok_bwd("norm_ffn_bwd", lambda a_, g_: (_rms_fn(a_, g_),), [], [x1], [g_ffn], [dh2], add=dx2)
    dycat = _mm("out_proj_dx", dx1, w_out, tb=True)
    d_out = _mm("out_proj_dw", ycat, dx1, ta=True, out_dtype=BF16)
    late_grads = [z.reshape(N_DEV, -1, D_MODEL) for z in (d_out, d_gate_t, d_up_t, d_down)]
    dy_rwkv, dy_ret = dycat[:, :Wd], dycat[:, Wd:]

    dyr_raw, dg_ret, G['ret_gn_w'] = _tok_bwd("ret_post_bwd", _ret_post_fn, [], [y_ret_raw, g_ret], [gn_w], [dy_ret])
    dq_rot, dk_rot, dv_ret = _ret_attn_bwd(lg, q_rot, k_rot, v_ret, dyr_raw)
    dq_p, dk_p = _tok_bwd("ret_rotary_bwd", _rotary_fn, [cos2, sin2], [q_p, k_p], [], [dq_rot, dk_rot])

    dyh, drh1, dkh1, dvh1, dgh, G['rwkv_lnx_w'], G['rwkv_lnx_b'], G['rwkv_r_k'] = _tok_bwd(
        "rwkv_post_bwd", _rwkv_post_fn, [], [yh, rh, kh, vh, gh], post_consts, [_to_heads(dy_rwkv)])
    (drh2, dlwh, dkh2, dvh2, dnkk, dbh), late_parts = _cscan_bwd(rh, lwh, kh, vh, nkk, bh, dyh, ck, late_grads)
    dk0h, dah, G['rwkv_k_k'], G['rwkv_k_a'] = _tok_bwd(
        "rwkv_pre_b_bwd", _pre_b_fn, [], [k0h, ah], [k_k, k_a], [dnkk, (dkh1, dkh2), dbh])
    pre_cts = [(_to_flat(drh1), _to_flat(drh2)), _to_flat(dk0h), (_to_flat(dvh1), _to_flat(dvh2)),
               _to_flat(dlwh), _to_flat(dah), _to_flat(dgh)]
    pre_out = _pre_a_bwd(h1, p_rkv, pre_consts, pre_cts)
    dh1_a, dp_rkv = pre_out[0], pre_out[1]
    (G['rwkv_mu_w'], G['rwkv_mu_a'], G['rwkv_mu_g'], G['rwkv_mu_r'], G['rwkv_mu_k'], G['rwkv_mu_v'], G['rwkv_w0'],
     dw1, dw2, G['rwkv_a0'], da1, da2, G['rwkv_g1'], G['rwkv_g2']) = pre_out[2:]
    G['rwkv_w1'], G['rwkv_w2'] = dw1[:, :64], dw2[:64, :]
    G['rwkv_a1'], G['rwkv_a2'] = da1[:, :64], da2[:64, :]

    dproj = jnp.concatenate([dp_rkv, dq_p, dk_p, dv_ret, dg_ret], axis=1)
    dh1 = _mm("proj_dx", dproj, W['w_in_t'], add=dh1_a)
    G['w_in_t'] = _mm("proj_dw", dproj, h1, ta=True, out_dtype=BF16)
    dx, G['norm_mix_g'] = _tok_bwd("norm_mix_bwd", lambda a_, g_: (_rms_fn(a_, g_),), [], [x], [g_mix], [dh1], add=dx1)
    return loss8[0, 0], dx, G, late_parts


def _adamw(name, parts, w, m, v):
    rows, cols = w.shape
    sub = 8 * 4 // parts.dtype.itemsize
    tb = max(t for t in range(sub, 65, sub) if rows % t == 0) if rows > 64 else rows
    c1 = 1.0 - ADAM_B1 ** ADAM_STEP
    c2 = 1.0 - ADAM_B2 ** ADAM_STEP

    def body(p_ref, w_ref, m_ref, v_ref, g_ref, d_ref, nm_ref, nv_ref):
        g = p_ref[0].astype(F32)
        for d in range(1, N_DEV):
            g = g + p_ref[d].astype(F32)
        mn = ADAM_B1 * m_ref[...] + (1.0 - ADAM_B1) * g
        vn = ADAM_B2 * v_ref[...] + (1.0 - ADAM_B2) * (g * g)
        m_hat = mn / c1
        v_hat = vn / c2
        g_ref[...] = g
        d_ref[...] = -ADAM_LR * (m_hat / (jnp.sqrt(v_hat) + ADAM_EPS) + ADAM_WD * w_ref[...])
        nm_ref[...] = mn
        nv_ref[...] = vn

    spec = pl.BlockSpec((tb, cols), lambda i: (i, 0))
    sh = jax.ShapeDtypeStruct((rows, cols), F32)
    return pl.pallas_call(
        body, name=name, grid=(rows // tb,),
        in_specs=[pl.BlockSpec((N_DEV, tb, cols), lambda i: (0, i, 0)), spec, spec, spec],
        out_specs=[spec] * 4, out_shape=[sh] * 4, compiler_params=_cparams(1))(parts, w, m, v)


def _local_shape(name):
    gs, ax = SHARDED[name]
    ls = list(gs)
    ls[ax] //= N_DEV
    return tuple(ls)


def _seg(flat, seg):
    n = flat.shape[-1]
    pad = _round_up(n, seg) - n
    if pad:
        flat = jnp.pad(flat, [(0, 0)] * (flat.ndim - 1) + [(0, pad)])
    return flat


def _split3(w):
    hi = w.astype(BF16)
    r1 = w - hi.astype(F32)
    mid = r1.astype(BF16)
    lo = (r1 - mid.astype(F32)).astype(BF16)
    return hi, mid, lo


def _pack_small_shards(shards):
    pieces = []
    for name in SMALL_NAMES:
        flat = shards[name].reshape(-1)
        if name == 'ffn_conv_w':
            pieces += [_seg(p, BF16_SEG) for p in _split3(flat)]
        else:
            pieces.append(flat.astype(BF16))
    return jnp.concatenate(pieces).reshape(-1, 128)


def _unpack_small(gathered):
    flat = gathered.reshape(N_DEV, -1)
    out, off = {}, 0
    for name in SMALL_NAMES:
        gs, ax = SHARDED[name]
        ls = _local_shape(name)
        n = int(np.prod(ls))
        if name == 'ffn_conv_w':
            nseg = _round_up(n, BF16_SEG)
            hi, mid, lo = (flat[:, off + j * nseg: off + j * nseg + n].astype(F32) for j in range(3))
            sh = ((hi + mid) + lo).reshape(N_DEV, 3, -1)
            out[name] = jnp.swapaxes(sh, 0, 1).reshape(3, D_FF)
            off += 3 * nseg
        else:
            sh = flat[:, off:off + n].reshape((N_DEV,) + ls[1:])
            out[name] = sh.reshape(gs[1:]) if ax == 1 else jnp.swapaxes(sh, 0, 1).reshape(gs[1:])
            off += n
    return out


def _small_pieces(sharded, repl):
    return [sharded[n].reshape(-1) for n in SMALL_NAMES] + [repl[n].reshape(-1) for n in REPL_NAMES]


def _pack_small_local(d):
    flat = jnp.concatenate(_small_pieces(d, d))
    return _seg(flat, F32_SEG).reshape(-1, 128)


def _pack_small_grads(G):
    pieces = []
    for name in SMALL_NAMES:
        gs, ax = SHARDED[name]
        g = G[name]
        if name == 'ffn_conv_w':
            sh = jnp.swapaxes(g.reshape(3, N_DEV, -1), 0, 1)
        elif ax == 1:
            sh = g
        else:
            sh = jnp.swapaxes(g.reshape(g.shape[0], N_DEV, -1), 0, 1)
        pieces.append(sh.reshape(N_DEV, -1))
    rep = jnp.concatenate([G[n].reshape(-1) for n in REPL_NAMES])
    pieces.append(jnp.broadcast_to(rep[None, :], (N_DEV, rep.shape[0])))
    flat = _seg(jnp.concatenate(pieces, axis=1), F32_SEG)
    return flat.reshape(N_DEV, -1, 128)


def _unpack_small_local(packed, local_shapes):
    flat = packed.reshape(-1)
    out, off = {}, 0
    for name in SMALL_NAMES + REPL_NAMES:
        n = int(np.prod(local_shapes[name]))
        out[name] = flat[off:off + n].reshape(local_shapes[name])
        off += n
    return out


def kernel(x, *rest):
    nw = len(WEIGHT_NAMES)
    assert len(rest) == 3 * nw + 1
    weights = dict(zip(WEIGHT_NAMES, rest[:nw]))
    loss_target = rest[nw]
    moms = dict(zip(WEIGHT_NAMES, rest[nw + 1:2 * nw + 1]))
    vars_ = dict(zip(WEIGHT_NAMES, rest[2 * nw + 1:]))
    local_shapes = {n: weights[n].shape for n in WEIGHT_NAMES}

    def native2d(name, a):
        a2 = a.reshape(a.shape[-2], a.shape[-1])
        return a2.T if name in BIG_T else a2

    def from2d(name, a2):
        return (a2.T if name in BIG_T else a2).reshape(local_shapes[name])

    big_w = {n: native2d(n, weights[n]) for n in BIG_NAMES}
    w_in_t_sh = big_w['w_in'].astype(BF16)
    late = [big_w[n].astype(BF16) for n in LATE_NAMES]
    small_sh = _pack_small_shards({n: weights[n] for n in SMALL_NAMES})
    w_in_g, small_g = _exchange("weights_all_gather", [w_in_t_sh, small_sh], False)
    W = _unpack_small(small_g)
    W['w_in_t'] = w_in_g.reshape(-1, D_MODEL)
    for n in REPL_NAMES:
        W[n] = weights[n][0] if n != 'norm_final_g' else weights[n]

    loss, dx, G, late_parts = _local_step(x[0], loss_target[0], W, late)

    w_in_parts, small_parts = _exchange(
        "grads_all_to_all", [G['w_in_t'].reshape(N_DEV, -1, D_MODEL), _pack_small_grads(G)], True)
    results = {}
    for n, parts in zip(['w_in'] + LATE_NAMES, [w_in_parts] + list(late_parts)):
        res = _adamw("adamw_" + n, parts, big_w[n], native2d(n, moms[n]), native2d(n, vars_[n]))
        results[n] = [from2d(n, r) for r in res]
    small_res = _adamw("adamw_small", small_parts, _pack_small_local(weights), _pack_small_local(moms),
                       _pack_small_local(vars_))
    small_out = [_unpack_small_local(p, local_shapes) for p in small_res]

    loss = lax.psum(loss, ("x", "y", "c"))
    outs = [loss, dx[None]]
    for j in range(4):
        outs += [results[n][j] if n in results else small_out[j][n] for n in WEIGHT_NAMES]
    return tuple(outs)
```

```python
import functools
import math

import numpy as np
import jax
import jax.numpy as jnp
from jax import lax
from jax.experimental import pallas as pl
from jax.experimental.pallas import tpu as pltpu

F32 = jnp.float32
BF16 = jnp.bfloat16

N_DEV = 8
D_MODEL = 1024
RWKV_HEADS = 8
RWKV_HEAD_DIM = 64
RWKV_WIDTH = 512
RET_HEADS = 4
RET_HEAD_DIM = 128
RET_WIDTH = 512
LORA_PAD = 128
D_FF = 2816
NORM_EPS = 1e-6
RWKV_GN_EPS = 64e-5
RET_GN_EPS = 1e-5
ROPE_BASE = 10000.0
ADAM_LR, ADAM_B1, ADAM_B2, ADAM_EPS, ADAM_WD, ADAM_STEP = 0.001, 0.9, 0.999, 1e-08, 0.01, 10

VMEM_LIMIT = 56 * 1024 * 1024
TOK_BLOCK = 256
SCAN_CHUNK = 64
ATT_BLOCK = 512
BF16_SEG = 2048
F32_SEG = 1024

WEIGHT_NAMES = ['norm_mix_g', 'w_in', 'rwkv_mu_r', 'rwkv_mu_k', 'rwkv_mu_v', 'rwkv_mu_w', 'rwkv_mu_a',
                'rwkv_mu_g', 'rwkv_w0', 'rwkv_w1', 'rwkv_w2', 'rwkv_a0', 'rwkv_a1', 'rwkv_a2', 'rwkv_g1',
                'rwkv_g2', 'rwkv_k_k', 'rwkv_k_a', 'rwkv_r_k', 'rwkv_lnx_w', 'rwkv_lnx_b', 'ret_gn_w',
                'w_out', 'norm_ffn_g', 'ffn_w_gate', 'ffn_w_up', 'ffn_conv_w', 'ffn_conv_b', 'ffn_w_down',
                'norm_final_g']
SHARDED = {
    'w_in': ((1, 1024, 3584), 2), 'rwkv_w1': ((1, 1024, 64), 1), 'rwkv_w2': ((1, 64, 512), 2),
    'rwkv_a1': ((1, 1024, 64), 1), 'rwkv_a2': ((1, 64, 512), 2), 'rwkv_g1': ((1, 1024, 128), 1),
    'rwkv_g2': ((1, 128, 512), 2), 'w_out': ((1, 1024, 1024), 1), 'ffn_w_gate': ((1, 1024, 2816), 2),
    'ffn_w_up': ((1, 1024, 2816), 2), 'ffn_conv_w': ((1, 3, 1, 2816), 3), 'ffn_w_down': ((1, 2816, 1024), 1),
}
REPL_NAMES = [n for n in WEIGHT_NAMES if n not in SHARDED]
BIG_NAMES = ['w_in', 'w_out', 'ffn_w_gate', 'ffn_w_up', 'ffn_w_down']
BIG_T = ('w_in', 'ffn_w_gate', 'ffn_w_up')
LATE_NAMES = ['w_out', 'ffn_w_gate', 'ffn_w_up', 'ffn_w_down']
SMALL_NAMES = [n for n in WEIGHT_NAMES if n in SHARDED and n not in BIG_NAMES]


def _cparams(n_grid):
    return pltpu.CompilerParams(dimension_semantics=("arbitrary",) * n_grid, vmem_limit_bytes=VMEM_LIMIT)


def _round_up(n, m):
    return (n + m - 1) // m * m


@jax.custom_vjp
def _bdot(x, w):
    return jnp.dot(x.astype(BF16), w.astype(BF16), preferred_element_type=F32)


def _bdot_fwd(x, w):
    return _bdot(x, w), (x, w)


def _bdot_bwd(res, g):
    x, w = res
    gb = g.astype(BF16)
    dx = lax.dot_general(gb, w.astype(BF16), (((1,), (1,)), ((), ())), preferred_element_type=F32)
    dw = lax.dot_general(x.astype(BF16), gb, (((0,), (0,)), ((), ())), preferred_element_type=F32)
    return dx, dw.astype(w.dtype)


_bdot.defvjp(_bdot_fwd, _bdot_bwd)


@jax.custom_vjp
def _shift_rows(x, prev):
    rolled = pltpu.roll(x, 1, 0)
    row = lax.broadcasted_iota(jnp.int32, x.shape, 0)
    return jnp.where(row == 0, jnp.broadcast_to(prev, x.shape), rolled)


def _shift_rows_fwd(x, prev):
    return _shift_rows(x, prev), None


def _shift_rows_bwd(_, g):
    n = g.shape[0]
    rolled = pltpu.roll(g, n - 1, 0)
    row = lax.broadcasted_iota(jnp.int32, g.shape, 0)
    return jnp.where(row == n - 1, 0.0, rolled), g[0:1, :]


_shift_rows.defvjp(_shift_rows_fwd, _shift_rows_bwd)


@jax.custom_vjp
def _swap_halves(x):
    return pltpu.roll(x, 64, 1)


_swap_halves.defvjp(lambda x: (_swap_halves(x), None), lambda _, g: (pltpu.roll(g, 64, 1),))


def _sigmoid(x):
    return 1.0 / (1.0 + jnp.exp(-x))


def _softplus(x):
    return jnp.maximum(x, 0.0) + jnp.log(1.0 + jnp.exp(-jnp.abs(x)))


def _rms_fn(x, g):
    return x * lax.rsqrt(jnp.mean(x * x, axis=-1, keepdims=True) + NORM_EPS) * g


def _pre_a_fn(h1, h1p, p, pp, mu_w, mu_a, mu_g, mu_r, mu_k, mu_v, w0, w1, w2, a0, a1, a2, g1, g2):
    W = RWKV_WIDTH
    h1s = _shift_rows(h1, h1p)
    ps = _shift_rows(p, pp)
    dx = h1s - h1
    xw = h1 + dx * mu_w
    xa = h1 + dx * mu_a
    xg = h1 + dx * mu_g
    dp = ps - p
    r = p[:, 0:W] + dp[:, 0:W] * mu_r
    k0 = p[:, W:2 * W] + dp[:, W:2 * W] * mu_k
    v = p[:, 2 * W:3 * W] + dp[:, 2 * W:3 * W] * mu_v
    wl = w0 + _bdot(jnp.tanh(_bdot(xw, w1)), w2)
    w_log = -_softplus(-wl) - 0.5
    lw = -jnp.exp(w_log)
    a = _sigmoid(a0 + _bdot(_bdot(xa, a1), a2))
    g = _bdot(_sigmoid(_bdot(xg, g1)), g2)
    return r, k0, v, lw, a, g


def _pre_b_fn(k0, a, k_k, k_a):
    kkr = k0 * k_k
    nrm = jnp.sqrt(jnp.sum(kkr * kkr, axis=-1, keepdims=True))
    kk = kkr / jnp.maximum(nrm, 1e-12)
    k = k0 * (1.0 + (a - 1.0) * k_a)
    return -kk, k, kk * a


def _rwkv_post_fn(y, r, k, v, g, lnx_w, lnx_b, r_k):
    mu = jnp.mean(y, axis=-1, keepdims=True)
    yc = y - mu
    var = jnp.mean(yc * yc, axis=-1, keepdims=True)
    yn = yc * lax.rsqrt(var + RWKV_GN_EPS) * lnx_w + lnx_b
    bonus = jnp.sum(r * k * r_k, axis=-1, keepdims=True) * v
    return ((yn + bonus) * g,)


def _rotary_fn(cos2, sin2, q, k):
    qs, ks = [], []
    for h in range(RET_HEADS):
        sl = slice(h * RET_HEAD_DIM, (h + 1) * RET_HEAD_DIM)
        qh, kh = q[:, sl], k[:, sl]
        qs.append(qh * cos2 + _swap_halves(qh) * sin2)
        ks.append((kh * cos2 + _swap_halves(kh) * sin2) * (RET_HEAD_DIM ** -0.5))
    return jnp.concatenate(qs, axis=1), jnp.concatenate(ks, axis=1)


def _ret_post_fn(y, gp, gn_w):
    outs = []
    for h in range(RET_HEADS):
        sl = slice(h * RET_HEAD_DIM, (h + 1) * RET_HEAD_DIM)
        yh = y[:, sl]
        mu = jnp.mean(yh, axis=-1, keepdims=True)
        yc = yh - mu
        var = jnp.mean(yc * yc, axis=-1, keepdims=True)
        outs.append(yc * lax.rsqrt(var + RET_GN_EPS) * gn_w[:, sl])
    yn = jnp.concatenate(outs, axis=1)
    return (gp * _sigmoid(gp) * yn,)


def _tok_axis(a):
    return 1 if a.ndim == 3 else 0


def _blk_spec(a, tb, rev_nb=None):
    nd, ax = a.ndim, _tok_axis(a)
    shape = a.shape[:ax] + (tb,) + a.shape[ax + 1:]

    def imap(i):
        idx = [0] * nd
        idx[ax] = i if rev_nb is None else rev_nb - 1 - i
        return tuple(idx)

    return pl.BlockSpec(shape, imap)


def _full_spec(a):
    nd = a.ndim
    return pl.BlockSpec(a.shape, lambda i: (0,) * nd)


def _tok_fwd(name, fn, toks, consts, out_tails, tb=TOK_BLOCK):
    n_in = len(toks) + len(consts)
    tn = toks[0].shape[_tok_axis(toks[0])]

    def body(*refs):
        outs = fn(*[r[...] for r in refs[:n_in]])
        for r, o in zip(refs[n_in:], outs):
            r[...] = o

    out_shape = [jax.ShapeDtypeStruct((tn,) + tuple(s) if len(s) == 1 else (s[0], tn, s[1]), F32) for s in out_tails]
    return pl.pallas_call(
        body, name=name, grid=(tn // tb,),
        in_specs=[_blk_spec(a, tb) for a in toks] + [_full_spec(c) for c in consts],
        out_specs=[_blk_spec(o, tb) for o in out_shape], out_shape=out_shape,
        compiler_params=_cparams(1))(*toks, *consts)


def _tok_bwd(name, fn, aux, toks, consts, cts, add=None, tb=TOK_BLOCK):
    n_aux, n_tok, n_c = len(aux), len(toks), len(consts)
    ct_groups = [c if isinstance(c, (tuple, list)) else (c,) for c in cts]
    ct_flat = [a for grp in ct_groups for a in grp]
    n_ct = len(ct_flat)
    n_add = 0 if add is None else 1
    tn = toks[0].shape[_tok_axis(toks[0])]

    def body(*refs):
        pos = 0
        aux_v = [r[...] for r in refs[pos:pos + n_aux]]; pos += n_aux
        tok_v = [r[...] for r in refs[pos:pos + n_tok]]; pos += n_tok
        const_v = [r[...] for r in refs[pos:pos + n_c]]; pos += n_c
        ct_refs = refs[pos:pos + n_ct]; pos += n_ct
        add_refs = refs[pos:pos + n_add]; pos += n_add
        dtok_refs = refs[pos:pos + n_tok]; pos += n_tok
        dconst_refs = refs[pos:pos + n_c]
        ct_v, q = [], 0
        for grp in ct_groups:
            s = ct_refs[q][...]
            for r in ct_refs[q + 1:q + len(grp)]:
                s = s + r[...]
            q += len(grp)
            ct_v.append(s)
        _, vjp = jax.vjp(lambda *tc: fn(*aux_v, *tc), *tok_v, *const_v)
        grads = vjp(tuple(ct_v))
        for j, r in enumerate(dtok_refs):
            gj = grads[j]
            if j == 0 and n_add:
                gj = gj + add_refs[0][...]
            r[...] = gj

        @pl.when(pl.program_id(0) == 0)
        def _():
            for r in dconst_refs:
                r[...] = jnp.zeros(r.shape, F32)

        for j, r in enumerate(dconst_refs):
            r[...] += grads[n_tok + j]

    ins = list(aux) + list(toks) + list(consts) + ct_flat + ([add] if n_add else [])
    in_specs = ([_blk_spec(a, tb) for a in aux] + [_blk_spec(a, tb) for a in toks] + [_full_spec(c) for c in consts]
                + [_blk_spec(a, tb) for a in ct_flat] + ([_blk_spec(add, tb)] if n_add else []))
    out_shape = [jax.ShapeDtypeStruct(a.shape, F32) for a in toks] + [jax.ShapeDtypeStruct(c.shape, F32) for c in consts]
    out_specs = [_blk_spec(a, tb) for a in toks] + [_full_spec(c) for c in consts]
    return pl.pallas_call(body, name=name, grid=(tn // tb,), in_specs=in_specs, out_specs=out_specs,
                          out_shape=out_shape, compiler_params=_cparams(1))(*ins)


MM_VMEM_BUDGET = 40 * 1024 * 1024
MM_STEP_SECONDS = 0.4e-6
MM_HBM_BYTES_PER_SECOND = 2.5e12


def _mm_tiles(m, n, kd, a_bytes, b_bytes, o_bytes, has_add):
    divs = lambda d: [t for t in range(128, d + 1, 128) if d % t == 0]
    best = None
    for tm in divs(m):
        for tn in divs(n):
            for tk in divs(kd):
                ni, nj, nk = m // tm, n // tn, kd // tk
                vmem = (2 * tm * tk * a_bytes + 2 * tk * tn * b_bytes + tm * tn * 4 + 2 * tm * tn * o_bytes
                        + (2 * tm * tn * 4 if has_add else 0) + 2 * (tm * tk + tk * tn) + tm * tn * 4)
                if vmem > MM_VMEM_BUDGET:
                    continue
                a_traffic = m * kd * a_bytes * (nj if nk > 1 else 1)
                b_traffic = kd * n * b_bytes * (ni if nj * nk > 1 else 1)
                cost = ni * nj * nk * MM_STEP_SECONDS + (a_traffic + b_traffic) / MM_HBM_BYTES_PER_SECOND
                if best is None or cost < best[0]:
                    best = (cost, tm, tn, tk)
    return best[1:]


def _mm(name, a, b, ta=False, tb=False, add=None, out_dtype=F32):
    if ta:
        kd, m = a.shape
    else:
        m, kd = a.shape
    if tb:
        n, kb = b.shape
    else:
        kb, n = b.shape
    assert kd == kb, (a.shape, b.shape)
    tm, tn, tk = _mm_tiles(m, n, kd, a.dtype.itemsize, b.dtype.itemsize, jnp.dtype(out_dtype).itemsize,
                           add is not None)
    nk = kd // tk
    has_add = add is not None
    dims = (((0 if ta else 1,), (1 if tb else 0,)), ((), ()))

    def body(*refs):
        a_ref, b_ref = refs[0], refs[1]
        o_ref, acc_ref = refs[-2], refs[-1]
        k = pl.program_id(2)

        @pl.when(k == 0)
        def _():
            acc_ref[...] = refs[2][...] if has_add else jnp.zeros(acc_ref.shape, F32)

        acc_ref[...] += lax.dot_general(a_ref[...].astype(BF16), b_ref[...].astype(BF16), dims,
                                        preferred_element_type=F32)

        @pl.when(k == nk - 1)
        def _():
            o_ref[...] = acc_ref[...].astype(out_dtype)

    a_spec = pl.BlockSpec((tk, tm), lambda i, j, k: (k, i)) if ta else pl.BlockSpec((tm, tk), lambda i, j, k: (i, k))
    b_spec = pl.BlockSpec((tn, tk), lambda i, j, k: (j, k)) if tb else pl.BlockSpec((tk, tn), lambda i, j, k: (k, j))
    o_spec = pl.BlockSpec((tm, tn), lambda i, j, k: (i, j))
    ins = [a, b] + ([add] if has_add else [])
    in_specs = [a_spec, b_spec] + ([o_spec] if has_add else [])
    return pl.pallas_call(body, name=name, grid=(m // tm, n // tn, nk), in_specs=in_specs, out_specs=o_spec,
                          out_shape=jax.ShapeDtypeStruct((m, n), out_dtype),
                          scratch_shapes=[pltpu.VMEM((tm, tn), F32)], compiler_params=_cparams(3))(*ins)


def _prev8_spec(a, tb, rev_nb=None):
    r = tb // 8
    if rev_nb is None:
        return pl.BlockSpec((8, a.shape[1]), lambda i: (jnp.maximum(i * r - 1, 0), 0))
    return pl.BlockSpec((8, a.shape[1]), lambda i: (jnp.maximum((rev_nb - 1 - i) * r - 1, 0), 0))


def _pre_a_fwd(h1, p, consts, tb=TOK_BLOCK):
    tn = h1.shape[0]

    def body(h1_ref, h1h_ref, p_ref, ph_ref, *rest):
        c_refs, o_refs = rest[:len(consts)], rest[len(consts):]
        first = pl.program_id(0) == 0
        h1p = jnp.where(first, 0.0, h1h_ref[7:8, :])
        pp = jnp.where(first, 0.0, ph_ref[7:8, :])
        outs = _pre_a_fn(h1_ref[...], h1p, p_ref[...], pp, *[c[...] for c in c_refs])
        for r, o in zip(o_refs, outs):
            r[...] = o

    out_shape = [jax.ShapeDtypeStruct((tn, RWKV_WIDTH), F32) for _ in range(6)]
    return pl.pallas_call(
        body, name="rwkv_pre_a_fwd", grid=(tn // tb,),
        in_specs=[_blk_spec(h1, tb), _prev8_spec(h1, tb), _blk_spec(p, tb), _prev8_spec(p, tb)]
        + [_full_spec(c) for c in consts],
        out_specs=[_blk_spec(o, tb) for o in out_shape], out_shape=out_shape,
        compiler_params=_cparams(1))(h1, h1, p, p, *consts)


def _pre_a_bwd(h1, p, consts, cts, tb=TOK_BLOCK):
    tn = h1.shape[0]
    nb = tn // tb
    n_c = len(consts)
    ct_groups = [c if isinstance(c, (tuple, list)) else (c,) for c in cts]
    ct_flat = [a for grp in ct_groups for a in grp]
    n_ct = len(ct_flat)

    def body(*refs):
        h1_ref, h1h_ref, p_ref, ph_ref = refs[:4]
        c_refs = refs[4:4 + n_c]
        ct_refs = refs[4 + n_c:4 + n_c + n_ct]
        dh1_ref, dp_ref = refs[4 + n_c + n_ct:6 + n_c + n_ct]
        dc_refs = refs[6 + n_c + n_ct:6 + 2 * n_c + n_ct]
        ch_ref, cp_ref = refs[-2], refs[-1]
        i = pl.program_id(0)
        first_block = i == nb - 1
        h1p = jnp.where(first_block, 0.0, h1h_ref[7:8, :])
        pp = jnp.where(first_block, 0.0, ph_ref[7:8, :])
        ct_v, q = [], 0
        for grp in ct_groups:
            s = ct_refs[q][...]
            for r in ct_refs[q + 1:q + len(grp)]:
                s = s + r[...]
            q += len(grp)
            ct_v.append(s)
        _, vjp = jax.vjp(_pre_a_fn, h1_ref[...], h1p, p_ref[...], pp, *[c[...] for c in c_refs])
        grads = vjp(tuple(ct_v))

        @pl.when(i == 0)
        def _():
            ch_ref[...] = jnp.zeros(ch_ref.shape, F32)
            cp_ref[...] = jnp.zeros(cp_ref.shape, F32)
            for r in dc_refs:
                r[...] = jnp.zeros(r.shape, F32)

        rowh = lax.broadcasted_iota(jnp.int32, (tb, h1.shape[1]), 0)
        rowp = lax.broadcasted_iota(jnp.int32, (tb, p.shape[1]), 0)
        dh1_ref[...] = grads[0] + jnp.where(rowh == tb - 1, jnp.broadcast_to(ch_ref[0:1, :], rowh.shape), 0.0)
        dp_ref[...] = grads[2] + jnp.where(rowp == tb - 1, jnp.broadcast_to(cp_ref[0:1, :], rowp.shape), 0.0)
        ch_ref[0:1, :] = grads[1]
        cp_ref[0:1, :] = grads[3]
        for j, r in enumerate(dc_refs):
            r[...] += grads[4 + j]

    ins = [h1, h1, p, p] + list(consts) + ct_flat
    in_specs = ([_blk_spec(h1, tb, nb), _prev8_spec(h1, tb, nb), _blk_spec(p, tb, nb), _prev8_spec(p, tb, nb)]
                + [_full_spec(c) for c in consts] + [_blk_spec(a, tb, nb) for a in ct_flat])
    out_shape = ([jax.ShapeDtypeStruct(h1.shape, F32), jax.ShapeDtypeStruct(p.shape, F32)]
                 + [jax.ShapeDtypeStruct(c.shape, F32) for c in consts])
    out_specs = [_blk_spec(h1, tb, nb), _blk_spec(p, tb, nb)] + [_full_spec(c) for c in consts]
    return pl.pallas_call(body, name="rwkv_pre_a_bwd", grid=(nb,), in_specs=in_specs, out_specs=out_specs,
                          out_shape=out_shape,
                          scratch_shapes=[pltpu.VMEM((8, h1.shape[1]), F32), pltpu.VMEM((8, p.shape[1]), F32)],
                          compiler_params=_cparams(1))(*ins)


def _my_index():
    return 4 * lax.axis_index("x") + 2 * lax.axis_index("y") + lax.axis_index("c")


def _peer(k):
    x, y, c = lax.axis_index("x"), lax.axis_index("y"), lax.axis_index("c")
    px = 1 - x if k & 4 else x
    py = 1 - y if k & 2 else y
    pc = 1 - c if k & 1 else c
    return (px, py, pc), 4 * px + 2 * py + pc


def _xchg_sems(n):
    return [pltpu.SemaphoreType.DMA((n * (N_DEV - 1),)), pltpu.SemaphoreType.DMA((n * (N_DEV - 1),)),
            pltpu.SemaphoreType.DMA((n,))]


def _xchg_copies(srcs, dsts, sems, scatter, incoming=False):
    send_sems, recv_sems, local_sems = sems
    me = _my_index()
    local, remote = [], []
    for i, (s, d) in enumerate(zip(srcs, dsts)):
        if not incoming:
            local.append(pltpu.make_async_copy(s.at[me] if scatter else s, d.at[me], local_sems.at[i]))
        for k in range(1, N_DEV):
            peer, plin = _peer(k)
            j = i * (N_DEV - 1) + k - 1
            s_slot, d_slot = (me, plin) if incoming else (plin, me)
            remote.append(pltpu.make_async_remote_copy(
                src_ref=s.at[s_slot] if scatter else s, dst_ref=d.at[d_slot], send_sem=send_sems.at[j],
                recv_sem=recv_sems.at[j], device_id=peer, device_id_type=pl.DeviceIdType.MESH))
    return local, remote


def _xchg_start(srcs, dsts, sems, scatter):
    local, out = _xchg_copies(srcs, dsts, sems, scatter)
    for cp in local + out:
        cp.start()


def _xchg_wait(srcs, dsts, sems, scatter):
    for cp in _xchg_copies(srcs, dsts, sems, scatter, incoming=True)[1]:
        cp.wait_recv()
    local, out = _xchg_copies(srcs, dsts, sems, scatter)
    for cp in out:
        cp.wait_send()
    for cp in local:
        cp.wait()


def _xchg_out_shapes(srcs, scatter):
    return [jax.ShapeDtypeStruct(s.shape if scatter else (N_DEV,) + s.shape, s.dtype) for s in srcs]


_ANY = pl.BlockSpec(memory_space=pl.ANY)


def _exchange(name, srcs, scatter):
    n = len(srcs)

    def body(*refs):
        s, d, sems = refs[:n], refs[n:2 * n], refs[2 * n:]
        _xchg_start(s, d, sems, scatter)
        _xchg_wait(s, d, sems, scatter)

    return pl.pallas_call(body, name=name, in_specs=[_ANY] * n, out_specs=[_ANY] * n,
                          out_shape=_xchg_out_shapes(srcs, scatter), scratch_shapes=_xchg_sems(n))(*srcs)


_MM_DIMS = {'nn': (((1,), (0,)), ((), ())), 'nt': (((1,), (1,)), ((), ())), 'tn': (((0,), (0,)), ((), ()))}


def _cmm_raw(x, y, kind, split):
    dot = functools.partial(lax.dot_general, dimension_numbers=_MM_DIMS[kind], preferred_element_type=F32)
    xh, yh = x.astype(BF16), y.astype(BF16)
    out = dot(xh, yh)
    if split:
        xl = (x - xh.astype(F32)).astype(BF16)
        yl = (y - yh.astype(F32)).astype(BF16)
        out = out + (dot(xh, yl) + dot(xl, yh))
    return out


@functools.partial(jax.custom_vjp, nondiff_argnums=(2, 3))
def _cmm(x, y, kind, split=False):
    return _cmm_raw(x, y, kind, split)


def _cmm_fwd(x, y, kind, split):
    return _cmm_raw(x, y, kind, split), (x, y)


def _cmm_bwd(kind, split, res, g):
    x, y = res
    if kind == 'nn':
        return _cmm_raw(g, y, 'nt', split), _cmm_raw(x, g, 'tn', split)
    if kind == 'nt':
        return _cmm_raw(g, y, 'nn', split), _cmm_raw(g, x, 'tn', split)
    return _cmm_raw(y, g, 'nt', split), _cmm_raw(x, g, 'nn', split)


_cmm.defvjp(_cmm_fwd, _cmm_bwd)


def _tri_sum_raw(tri, x, kind):
    dot = functools.partial(lax.dot_general, dimension_numbers=_MM_DIMS[kind], preferred_element_type=F32)
    tb = tri.astype(BF16)
    hi, mid, lo = _split3(x)
    return (dot(tb, hi) + dot(tb, mid)) + dot(tb, lo)


@functools.partial(jax.custom_vjp, nondiff_argnums=(2,))
def _tri_sum(tri, x, kind):
    return _tri_sum_raw(tri, x, kind)


def _tri_sum_fwd(tri, x, kind):
    return _tri_sum_raw(tri, x, kind), tri


def _tri_sum_bwd(kind, tri, g):
    return jnp.zeros_like(tri), _tri_sum_raw(tri, g, 'tn' if kind == 'nn' else 'nn')


_tri_sum.defvjp(_tri_sum_fwd, _tri_sum_bwd)


def _chunk_fn(S0, r, lw, k, v, a, b):
    hs = range(len(r))
    C = r[0].shape[0]
    ii = lax.broadcasted_iota(jnp.int32, (C, C), 0)
    jj = lax.broadcasted_iota(jnp.int32, (C, C), 1)
    incl, strict = ii >= jj, ii > jj
    eye = (ii == jj).astype(F32)
    inclf = incl.astype(F32)
    cum = [_tri_sum(inclf, lw[h], 'nn') for h in hs]
    e_inv = [jnp.exp(-cum[h]) for h in hs]
    At = [a[h] * jnp.exp(cum[h] - lw[h]) for h in hs]
    Rt = [r[h] * jnp.exp(cum[h]) for h in hs]
    Kh = [k[h] * e_inv[h] for h in hs]
    Bh = [b[h] * e_inv[h] for h in hs]
    Mab = [jnp.where(strict, _cmm(At[h], Bh[h], 'nt'), 0.0) for h in hs]
    Mak = [jnp.where(strict, _cmm(At[h], Kh[h], 'nt'), 0.0) for h in hs]
    Mrk = [jnp.where(incl, _cmm(Rt[h], Kh[h], 'nt'), 0.0) for h in hs]
    Mrb = [jnp.where(incl, _cmm(Rt[h], Bh[h], 'nt'), 0.0) for h in hs]
    rhs = [_cmm(At[h], S0[h], 'nt') + _cmm(Mak[h], v[h], 'nn') for h in hs]
    P = Mab
    Tm = [eye + P[h] for h in hs]
    n = 1
    while 2 * n < C:
        P = [_cmm(P[h], P[h], 'nn', True) for h in hs]
        Tm = [_cmm(Tm[h], eye + P[h], 'nn', True) for h in hs]
        n *= 2
    U = [_cmm(Tm[h], rhs[h], 'nn', True) for h in hs]
    Y = [_cmm(Rt[h], S0[h], 'nt') + _cmm(Mrk[h], v[h], 'nn') + _cmm(Mrb[h], U[h], 'nn') for h in hs]
    gC = [jnp.exp(jnp.sum(lw[h], axis=0, keepdims=True)) for h in hs]
    SC = [S0[h] * gC[h] + _cmm(v[h], Kh[h] * gC[h], 'tn') + _cmm(U[h], Bh[h] * gC[h], 'tn') for h in hs]
    return tuple(Y), tuple(SC)


def _cscan_fwd(r, lw, k, v, a, b, xs):
    n_x = len(xs)
    H, tn, Dh = r.shape
    nc = tn // SCAN_CHUNK

    def body(r_ref, lw_ref, k_ref, v_ref, a_ref, b_ref, *rest):
        x_src, (y_ref, ck_ref) = rest[:n_x], rest[n_x:n_x + 2]
        x_dst, s_ref, sems = rest[n_x + 2:2 * n_x + 2], rest[2 * n_x + 2], rest[2 * n_x + 3:]

        @pl.when(pl.program_id(0) == 0)
        def _():
            s_ref[...] = jnp.zeros(s_ref.shape, F32)
            _xchg_start(x_src, x_dst, sems, False)

        ck_ref[0] = s_ref[...]
        heads = lambda ref: tuple(ref[h] for h in range(H))
        y, sc = _chunk_fn(heads(s_ref), heads(r_ref), heads(lw_ref), heads(k_ref), heads(v_ref), heads(a_ref),
                          heads(b_ref))
        for h in range(H):
            y_ref[h] = y[h]
            s_ref[h] = sc[h]

        @pl.when(pl.program_id(0) == nc - 1)
        def _():
            _xchg_wait(x_src, x_dst, sems, False)

    hm = pl.BlockSpec((H, SCAN_CHUNK, Dh), lambda c: (0, c, 0))
    res = pl.pallas_call(
        body, name="rwkv_scan_fwd", grid=(nc,), in_specs=[hm] * 6 + [_ANY] * n_x,
        out_specs=[hm, pl.BlockSpec((1, H, Dh, Dh), lambda c: (c, 0, 0, 0))] + [_ANY] * n_x,
        out_shape=[jax.ShapeDtypeStruct((H, tn, Dh), F32), jax.ShapeDtypeStruct((nc, H, Dh, Dh), F32)]
        + _xchg_out_shapes(xs, False),
        scratch_shapes=[pltpu.VMEM((H, Dh, Dh), F32)] + _xchg_sems(n_x),
        compiler_params=_cparams(1))(r, lw, k, v, a, b, *xs)
    return res[0], res[1], res[2:]


def _cscan_bwd(r, lw, k, v, a, b, dy, ck, xs):
    n_x = len(xs)
    H, tn, Dh = r.shape
    nc = tn // SCAN_CHUNK

    def body(r_ref, lw_ref, k_ref, v_ref, a_ref, b_ref, dy_ref, ck_ref, *rest):
        x_src = rest[:n_x]
        d_refs = rest[n_x:n_x + 6]
        x_dst = rest[n_x + 6:2 * n_x + 6]
        g_ref = rest[2 * n_x + 6]
        sems = rest[2 * n_x + 7:]

        @pl.when(pl.program_id(0) == 0)
        def _():
            g_ref[...] = jnp.zeros(g_ref.shape, F32)
            _xchg_start(x_src, x_dst, sems, True)

        heads = lambda ref: tuple(ref[h] for h in range(H))
        s0 = tuple(ck_ref[0, h] for h in range(H))
        _, vjp = jax.vjp(_chunk_fn, s0, heads(r_ref), heads(lw_ref), heads(k_ref), heads(v_ref), heads(a_ref),
                         heads(b_ref))
        grads = vjp((heads(dy_ref), heads(g_ref)))
        for h in range(H):
            g_ref[h] = grads[0][h]
            for d_ref, gz in zip(d_refs, grads[1:]):
                d_ref[h] = gz[h]

        @pl.when(pl.program_id(0) == nc - 1)
        def _():
            _xchg_wait(x_src, x_dst, sems, True)

    hm = pl.BlockSpec((H, SCAN_CHUNK, Dh), lambda c: (0, nc - 1 - c, 0))
    hshape = jax.ShapeDtypeStruct((H, tn, Dh), F32)
    res = pl.pallas_call(
        body, name="rwkv_scan_bwd", grid=(nc,),
        in_specs=[hm] * 7 + [pl.BlockSpec((1, H, Dh, Dh), lambda c: (nc - 1 - c, 0, 0, 0))] + [_ANY] * n_x,
        out_specs=[hm] * 6 + [_ANY] * n_x, out_shape=[hshape] * 6 + _xchg_out_shapes(xs, True),
        scratch_shapes=[pltpu.VMEM((H, Dh, Dh), F32)] + _xchg_sems(n_x),
        compiler_params=_cparams(1))(r, lw, k, v, a, b, dy, ck, *xs)
    return res[:6], res[6:]


def _decay_mask(lg, i, j, blk):
    rows = lax.broadcasted_iota(jnp.int32, (blk, blk), 0)
    cols = lax.broadcasted_iota(jnp.int32, (blk, blk), 1)
    dd = (rows - cols + (i - j) * blk).astype(F32)
    return jnp.where(dd >= 0.0, jnp.exp(lg * jnp.maximum(dd, 0.0)), 0.0)


_NT = (((1,), (1,)), ((), ()))
_TN = (((0,), (0,)), ((), ()))


def _ret_attn_fwd(lg, q, k, v, blk=ATT_BLOCK):
    tn = q.shape[0]
    Dh = RET_HEAD_DIM

    def body(lg_ref, q_ref, k_ref, v_ref, o_ref):
        i = pl.program_id(1)
        lgv = lg_ref[0][:, 0:1]
        qb = q_ref[...].astype(BF16)

        def jb(j, acc):
            ks = pl.ds(pl.multiple_of(j * blk, blk), blk)
            s = lax.dot_general(qb, k_ref[ks, :].astype(BF16), _NT, preferred_element_type=F32)
            s = s * _decay_mask(lgv, i, j, blk)
            return acc + jnp.dot(s.astype(BF16), v_ref[ks, :].astype(BF16), preferred_element_type=F32)

        o_ref[...] = lax.fori_loop(0, i + 1, jb, jnp.zeros((blk, Dh), F32))

    full = pl.BlockSpec((tn, Dh), lambda h, i: (0, h))
    qs = pl.BlockSpec((blk, Dh), lambda h, i: (i, h))
    return pl.pallas_call(
        body, name="ret_attn_fwd", grid=(RET_HEADS, tn // blk),
        in_specs=[pl.BlockSpec((1, 1, 128), lambda h, i: (h, 0, 0)), qs, full, full],
        out_specs=qs, out_shape=jax.ShapeDtypeStruct(q.shape, F32), compiler_params=_cparams(2))(lg, q, k, v)


def _ret_attn_bwd(lg, q, k, v, do, blk=ATT_BLOCK):
    tn = q.shape[0]
    nb = tn // blk
    Dh = RET_HEAD_DIM

    def body(lg_ref, q_ref, k_ref, v_ref, do_ref, dq_ref, dk_ref, dv_ref):
        lgv = lg_ref[0][:, 0:1]
        dk_ref[...] = jnp.zeros(dk_ref.shape, F32)
        dv_ref[...] = jnp.zeros(dv_ref.shape, F32)

        def ib(i, carry):
            qs = pl.ds(pl.multiple_of(i * blk, blk), blk)
            qb = q_ref[qs, :].astype(BF16)
            dob = do_ref[qs, :].astype(BF16)

            def jb(j, dq):
                ks = pl.ds(pl.multiple_of(j * blk, blk), blk)
                kb = k_ref[ks, :].astype(BF16)
                vb = v_ref[ks, :].astype(BF16)
                dm = _decay_mask(lgv, i, j, blk)
                s = lax.dot_general(qb, kb, _NT, preferred_element_type=F32) * dm
                ds = lax.dot_general(dob, vb, _NT, preferred_element_type=F32) * dm
                sb, dsb = s.astype(BF16), ds.astype(BF16)
                dv_ref[ks, :] += lax.dot_general(sb, dob, _TN, preferred_element_type=F32)
                dk_ref[ks, :] += lax.dot_general(dsb, qb, _TN, preferred_element_type=F32)
                return dq + jnp.dot(dsb, kb, preferred_element_type=F32)

            dq_ref[qs, :] = lax.fori_loop(0, i + 1, jb, jnp.zeros((blk, Dh), F32))
            return carry

        lax.fori_loop(0, nb, ib, 0)

    full = pl.BlockSpec((tn, Dh), lambda h: (0, h))
    sh = jax.ShapeDtypeStruct(q.shape, F32)
    return pl.pallas_call(
        body, name="ret_attn_bwd", grid=(RET_HEADS,),
        in_specs=[pl.BlockSpec((1, 1, 128), lambda h: (h, 0, 0)), full, full, full, full],
        out_specs=[full, full, full], out_shape=[sh, sh, sh], compiler_params=_cparams(1))(lg, q, k, v, do)


def _next8_spec(a, tb):
    r = tb // 8
    last = a.shape[0] // 8 - 1
    return pl.BlockSpec((8, a.shape[1]), lambda i: (jnp.minimum((i + 1) * r, last), 0))


def _conv_taps(g_ext, cw_ref, cb_ref):
    return (cw_ref[2:3, :] * g_ext + cw_ref[1:2, :] * pltpu.roll(g_ext, 1, 0)
            + cw_ref[0:1, :] * pltpu.roll(g_ext, 2, 0) + cb_ref[...])


def _glu_fwd(gate, up, cw, cb, tb=TOK_BLOCK):
    tn = gate.shape[0]

    def body(g_ref, gh_ref, u_ref, cw_ref, cb_ref, o_ref):
        halo = jnp.where(pl.program_id(0) == 0, 0.0, gh_ref[...])
        g_ext = jnp.concatenate([halo, g_ref[...]], axis=0)
        gc = _conv_taps(g_ext, cw_ref, cb_ref)[8:, :]
        o_ref[...] = gc * _sigmoid(gc) * u_ref[...]

    return pl.pallas_call(
        body, name="glu_fwd", grid=(tn // tb,),
        in_specs=[_blk_spec(gate, tb), _prev8_spec(gate, tb), _blk_spec(up, tb), _full_spec(cw), _full_spec(cb)],
        out_specs=_blk_spec(gate, tb), out_shape=jax.ShapeDtypeStruct(gate.shape, F32),
        compiler_params=_cparams(1))(gate, gate, up, cw, cb)


def _glu_bwd(gate, up, dact, cw, cb, tb=TOK_BLOCK):
    tn = gate.shape[0]
    nb = tn // tb

    def body(g_ref, gp_ref, gn_ref, u_ref, un_ref, d_ref, dn_ref, cw_ref, cb_ref, dg_ref, du_ref, dcw_ref, dcb_ref):
        i = pl.program_id(0)
        gprev = jnp.where(i == 0, 0.0, gp_ref[...])
        dnext = jnp.where(i == nb - 1, 0.0, dn_ref[...])
        g_ext = jnp.concatenate([gprev, g_ref[...], gn_ref[...]], axis=0)
        gc = _conv_taps(g_ext, cw_ref, cb_ref)[8:, :]
        u_e = jnp.concatenate([u_ref[...], un_ref[...]], axis=0)
        d_e = jnp.concatenate([d_ref[...], dnext], axis=0)
        s = _sigmoid(gc)
        dgc = d_e * u_e * (s * (1.0 + gc * (1.0 - s)))
        du_ref[...] = d_ref[...] * (gc * s)[:tb, :]
        n_e = tb + 8
        dg_ref[...] = (cw_ref[2:3, :] * dgc + cw_ref[1:2, :] * pltpu.roll(dgc, n_e - 1, 0)
                       + cw_ref[0:1, :] * pltpu.roll(dgc, n_e - 2, 0))[:tb, :]

        @pl.when(i == 0)
        def _():
            dcw_ref[...] = jnp.zeros(dcw_ref.shape, F32)
            dcb_ref[...] = jnp.zeros(dcb_ref.shape, F32)

        dgc_b = dgc[:tb, :]
        g0 = g_ext[8:8 + tb, :]
        g1 = pltpu.roll(g_ext, 1, 0)[8:8 + tb, :]
        g2 = pltpu.roll(g_ext, 2, 0)[8:8 + tb, :]
        dcw_ref[2:3, :] += jnp.sum(dgc_b * g0, axis=0, keepdims=True)
        dcw_ref[1:2, :] += jnp.sum(dgc_b * g1, axis=0, keepdims=True)
        dcw_ref[0:1, :] += jnp.sum(dgc_b * g2, axis=0, keepdims=True)
        dcb_ref[...] += jnp.sum(dgc_b, axis=0, keepdims=True)

    sh = jax.ShapeDtypeStruct(gate.shape, F32)
    return pl.pallas_call(
        body, name="glu_bwd", grid=(nb,),
        in_specs=[_blk_spec(gate, tb), _prev8_spec(gate, tb), _next8_spec(gate, tb), _blk_spec(up, tb),
                  _next8_spec(up, tb), _blk_spec(dact, tb), _next8_spec(dact, tb), _full_spec(cw), _full_spec(cb)],
        out_specs=[_blk_spec(gate, tb), _blk_spec(gate, tb), _full_spec(cw), _full_spec(cb)],
        out_shape=[sh, sh, jax.ShapeDtypeStruct(cw.shape, F32), jax.ShapeDtypeStruct(cb.shape, F32)],
        compiler_params=_cparams(1))(gate, gate, gate, up, up, dact, dact, cw, cb)


def _final_loss(x2, tgt, g, tb=TOK_BLOCK):
    tn, dm = x2.shape

    def body(x_ref, t_ref, g_ref, l_ref, dx_ref, dg_ref):
        y, vjp = jax.vjp(_rms_fn, x_ref[...], g_ref[...])
        err = y - t_ref[...]
        dx, dg = vjp(err * (1.0 / dm))

        @pl.when(pl.program_id(0) == 0)
        def _():
            l_ref[...] = jnp.zeros(l_ref.shape, F32)
            dg_ref[...] = jnp.zeros(dg_ref.shape, F32)

        part = 0.5 * jnp.sum(jnp.mean(err * err, axis=-1, keepdims=True), axis=0, keepdims=True)
        l_ref[...] += jnp.broadcast_to(part, l_ref.shape)
        dx_ref[...] = dx
        dg_ref[...] += dg

    return pl.pallas_call(
        body, name="final_loss", grid=(tn // tb,),
        in_specs=[_blk_spec(x2, tb), _blk_spec(tgt, tb), _full_spec(g)],
        out_specs=[pl.BlockSpec((8, 128), lambda i: (0, 0)), _blk_spec(x2, tb), _full_spec(g)],
        out_shape=[jax.ShapeDtypeStruct((8, 128), F32), jax.ShapeDtypeStruct(x2.shape, F32),
                   jax.ShapeDtypeStruct(g.shape, F32)],
        compiler_params=_cparams(1))(x2, tgt, g)


def _pad_cols(w, n):
    return jnp.pad(w, ((0, 0), (0, n - w.shape[1])))


def _pad_rows(w, n):
    return jnp.pad(w, ((0, n - w.shape[0]), (0, 0)))


def _to_heads(z):
    return jnp.swapaxes(z.reshape(z.shape[0], RWKV_HEADS, RWKV_HEAD_DIM), 0, 1)


def _to_flat(z):
    return jnp.swapaxes(z, 0, 1).reshape(z.shape[1], RWKV_WIDTH)


def _head_param(p):
    return p.reshape(RWKV_HEADS, 1, RWKV_HEAD_DIM)


def _local_step(x, tgt, W, late):
    tn = x.shape[0]
    Wd = RWKV_WIDTH
    row = lambda z: z.reshape(1, -1)
    g_mix, g_ffn, g_fin = row(W['norm_mix_g']), row(W['norm_ffn_g']), row(W['norm_final_g'])

    (h1,) = _tok_fwd("norm_mix_fwd", lambda a, g: (_rms_fn(a, g),), [x], [g_mix], [(D_MODEL,)])
    proj = _mm("proj_fwd", h1, W['w_in_t'], tb=True)
    p_rkv = proj[:, :3 * Wd]
    pre_consts = [row(W['rwkv_mu_w']), row(W['rwkv_mu_a']), row(W['rwkv_mu_g']), row(W['rwkv_mu_r']),
                  row(W['rwkv_mu_k']), row(W['rwkv_mu_v']), row(W['rwkv_w0']),
                  _pad_cols(W['rwkv_w1'], LORA_PAD), _pad_rows(W['rwkv_w2'], LORA_PAD), row(W['rwkv_a0']),
                  _pad_cols(W['rwkv_a1'], LORA_PAD), _pad_rows(W['rwkv_a2'], LORA_PAD),
                  W['rwkv_g1'], W['rwkv_g2']]
    r, k0, v, lw, a, g = _pre_a_fwd(h1, p_rkv, pre_consts)
    k_k, k_a = _head_param(W['rwkv_k_k']), _head_param(W['rwkv_k_a'])
    k0h, ah = _to_heads(k0), _to_heads(a)
    nkk, kh, bh = _tok_fwd("rwkv_pre_b_fwd", _pre_b_fn, [k0h, ah], [k_k, k_a], [(RWKV_HEADS, RWKV_HEAD_DIM)] * 3)
    rh, lwh, vh, gh = _to_heads(r), _to_heads(lw), _to_heads(v), _to_heads(g)
    yh, ck, gathered = _cscan_fwd(rh, lwh, kh, vh, nkk, bh, late)
    w_out, w_gate_t, w_up_t, w_down = [g_.reshape(-1, D_MODEL) for g_ in gathered]
    post_consts = [_head_param(W['rwkv_lnx_w']), _head_param(W['rwkv_lnx_b']), _head_param(W['rwkv_r_k'])]
    (y_rwkv_h,) = _tok_fwd("rwkv_post_fwd", _rwkv_post_fn, [yh, rh, kh, vh, gh], post_consts,
                           [(RWKV_HEADS, RWKV_HEAD_DIM)])
    y_rwkv = _to_flat(y_rwkv_h)

    pos = jnp.arange(tn, dtype=F32)
    half = RET_HEAD_DIM // 2
    inv_freq = ROPE_BASE ** (-jnp.arange(half, dtype=F32) / half)
    ang = pos[:, None] * inv_freq[None, :]
    cos2 = jnp.concatenate([jnp.cos(ang), jnp.cos(ang)], axis=1)
    sin2 = jnp.concatenate([-jnp.sin(ang), jnp.sin(ang)], axis=1)
    lg = jnp.log(1.0 - 2.0 ** (-5.0 - jnp.arange(RET_HEADS, dtype=F32)))
    lg = jnp.broadcast_to(lg[:, None, None], (RET_HEADS, 1, 128))
    q_p, k_p = proj[:, 3 * Wd:4 * Wd], proj[:, 4 * Wd:5 * Wd]
    v_ret, g_ret = proj[:, 5 * Wd:6 * Wd], proj[:, 6 * Wd:7 * Wd]
    q_rot, k_rot = _tok_fwd("ret_rotary_fwd", _rotary_fn, [cos2, sin2, q_p, k_p], [], [(RET_WIDTH,)] * 2)
    y_ret_raw = _ret_attn_fwd(lg, q_rot, k_rot, v_ret)
    gn_w = row(W['ret_gn_w'])
    (y_ret,) = _tok_fwd("ret_post_fwd", _ret_post_fn, [y_ret_raw, g_ret], [gn_w], [(RET_WIDTH,)])

    ycat = jnp.concatenate([y_rwkv, y_ret], axis=1)
    x1 = _mm("out_proj_fwd", ycat, w_out, add=x)
    (h2,) = _tok_fwd("norm_ffn_fwd", lambda a_, g_: (_rms_fn(a_, g_),), [x1], [g_ffn], [(D_MODEL,)])
    gate = _mm("ffn_gate_fwd", h2, w_gate_t, tb=True)
    up = _mm("ffn_up_fwd", h2, w_up_t, tb=True)
    cw = W['ffn_conv_w']
    cb = row(W['ffn_conv_b'])
    act = _glu_fwd(gate, up, cw, cb)
    x2 = _mm("ffn_down_fwd", act, w_down, add=x1)
    loss8, dx2, dg_fin = _final_loss(x2, tgt, g_fin)

    G = {'norm_final_g': dg_fin}
    dact = _mm("ffn_down_dx", dx2, w_down, tb=True)
    d_down = _mm("ffn_down_dw", act, dx2, ta=True, out_dtype=BF16)
    dgate, dup, dcw, dcb = _glu_bwd(gate, up, dact, cw, cb)
    G['ffn_conv_w'], G['ffn_conv_b'] = dcw, dcb
    dh2 = _mm("ffn_gate_dx", dgate, w_gate_t)
    dh2 = _mm("ffn_up_dx", dup, w_up_t, add=dh2)
    d_gate_t = _mm("ffn_gate_dw", dgate, h2, ta=True, out_dtype=BF16)
    d_up_t = _mm("ffn_up_dw", dup, h2, ta=True, out_dtype=BF16)
    dx1, G['norm_ffn_g'] = _tok_bwd("norm_ffn_bwd", lambda a_, g_: (_rms_fn(a_, g_),), [], [x1], [g_ffn], [dh2], add=dx2)
    dycat = _mm("out_proj_dx", dx1, w_out, tb=True)
    d_out = _mm("out_proj_dw", ycat, dx1, ta=True, out_dtype=BF16)
    late_grads = [z.reshape(N_DEV, -1, D_MODEL) for z in (d_out, d_gate_t, d_up_t, d_down)]
    dy_rwkv, dy_ret = dycat[:, :Wd], dycat[:, Wd:]

    dyr_raw, dg_ret, G['ret_gn_w'] = _tok_bwd("ret_post_bwd", _ret_post_fn, [], [y_ret_raw, g_ret], [gn_w], [dy_ret])
    dq_rot, dk_rot, dv_ret = _ret_attn_bwd(lg, q_rot, k_rot, v_ret, dyr_raw)
    dq_p, dk_p = _tok_bwd("ret_rotary_bwd", _rotary_fn, [cos2, sin2], [q_p, k_p], [], [dq_rot, dk_rot])

    dyh, drh1, dkh1, dvh1, dgh, G['rwkv_lnx_w'], G['rwkv_lnx_b'], G['rwkv_r_k'] = _tok_bwd(
        "rwkv_post_bwd", _rwkv_post_fn, [], [yh, rh, kh, vh, gh], post_consts, [_to_heads(dy_rwkv)])
    (drh2, dlwh, dkh2, dvh2, dnkk, dbh), late_parts = _cscan_bwd(rh, lwh, kh, vh, nkk, bh, dyh, ck, late_grads)
    dk0h, dah, G['rwkv_k_k'], G['rwkv_k_a'] = _tok_bwd(
        "rwkv_pre_b_bwd", _pre_b_fn, [], [k0h, ah], [k_k, k_a], [dnkk, (dkh1, dkh2), dbh])
    pre_cts = [(_to_flat(drh1), _to_flat(drh2)), _to_flat(dk0h), (_to_flat(dvh1), _to_flat(dvh2)),
               _to_flat(dlwh), _to_flat(dah), _to_flat(dgh)]
    pre_out = _pre_a_bwd(h1, p_rkv, pre_consts, pre_cts)
    dh1_a, dp_rkv = pre_out[0], pre_out[1]
    (G['rwkv_mu_w'], G['rwkv_mu_a'], G['rwkv_mu_g'], G['rwkv_mu_r'], G['rwkv_mu_k'], G['rwkv_mu_v'], G['rwkv_w0'],
     dw1, dw2, G['rwkv_a0'], da1, da2, G['rwkv_g1'], G['rwkv_g2']) = pre_out[2:]
    G['rwkv_w1'], G['rwkv_w2'] = dw1[:, :64], dw2[:64, :]
    G['rwkv_a1'], G['rwkv_a2'] = da1[:, :64], da2[:64, :]

    dproj = jnp.concatenate([dp_rkv, dq_p, dk_p, dv_ret, dg_ret], axis=1)
    dh1 = _mm("proj_dx", dproj, W['w_in_t'], add=dh1_a)
    G['w_in_t'] = _mm("proj_dw", dproj, h1, ta=True, out_dtype=BF16)
    dx, G['norm_mix_g'] = _tok_bwd("norm_mix_bwd", lambda a_, g_: (_rms_fn(a_, g_),), [], [x], [g_mix], [dh1], add=dx1)
    return loss8[0, 0], dx, G, late_parts


def _adamw(name, parts, w, m, v):
    rows, cols = w.shape
    sub = 8 * 4 // parts.dtype.itemsize
    tb = max(t for t in range(sub, 65, sub) if rows % t == 0) if rows > 64 else rows
    c1 = 1.0 - ADAM_B1 ** ADAM_STEP
    c2 = 1.0 - ADAM_B2 ** ADAM_STEP

    def body(p_ref, w_ref, m_ref, v_ref, g_ref, d_ref, nm_ref, nv_ref):
        g = p_ref[0].astype(F32)
        for d in range(1, N_DEV):
            g = g + p_ref[d].astype(F32)
        mn = ADAM_B1 * m_ref[...] + (1.0 - ADAM_B1) * g
        vn = ADAM_B2 * v_ref[...] + (1.0 - ADAM_B2) * (g * g)
        m_hat = mn / c1
        v_hat = vn / c2
        g_ref[...] = g
        d_ref[...] = -ADAM_LR * (m_hat / (jnp.sqrt(v_hat) + ADAM_EPS) + ADAM_WD * w_ref[...])
        nm_ref[...] = mn
        nv_ref[...] = vn

    spec = pl.BlockSpec((tb, cols), lambda i: (i, 0))
    sh = jax.ShapeDtypeStruct((rows, cols), F32)
    return pl.pallas_call(
        body, name=name, grid=(rows // tb,),
        in_specs=[pl.BlockSpec((N_DEV, tb, cols), lambda i: (0, i, 0)), spec, spec, spec],
        out_specs=[spec] * 4, out_shape=[sh] * 4, compiler_params=_cparams(1))(parts, w, m, v)


def _local_shape(name):
    gs, ax = SHARDED[name]
    ls = list(gs)
    ls[ax] //= N_DEV
    return tuple(ls)


def _seg(flat, seg):
    n = flat.shape[-1]
    pad = _round_up(n, seg) - n
    if pad:
        flat = jnp.pad(flat, [(0, 0)] * (flat.ndim - 1) + [(0, pad)])
    return flat


def _split3(w):
    hi = w.astype(BF16)
    r1 = w - hi.astype(F32)
    mid = r1.astype(BF16)
    lo = (r1 - mid.astype(F32)).astype(BF16)
    return hi, mid, lo


def _pack_small_shards(shards):
    pieces = []
    for name in SMALL_NAMES:
        flat = shards[name].reshape(-1)
        if name == 'ffn_conv_w':
            pieces += [_seg(p, BF16_SEG) for p in _split3(flat)]
        else:
            pieces.append(flat.astype(BF16))
    return jnp.concatenate(pieces).reshape(-1, 128)


def _unpack_small(gathered):
    flat = gathered.reshape(N_DEV, -1)
    out, off = {}, 0
    for name in SMALL_NAMES:
        gs, ax = SHARDED[name]
        ls = _local_shape(name)
        n = int(np.prod(ls))
        if name == 'ffn_conv_w':
            nseg = _round_up(n, BF16_SEG)
            hi, mid, lo = (flat[:, off + j * nseg: off + j * nseg + n].astype(F32) for j in range(3))
            sh = ((hi + mid) + lo).reshape(N_DEV, 3, -1)
            out[name] = jnp.swapaxes(sh, 0, 1).reshape(3, D_FF)
            off += 3 * nseg
        else:
            sh = flat[:, off:off + n].reshape((N_DEV,) + ls[1:])
            out[name] = sh.reshape(gs[1:]) if ax == 1 else jnp.swapaxes(sh, 0, 1).reshape(gs[1:])
            off += n
    return out


def _small_pieces(sharded, repl):
    return [sharded[n].reshape(-1) for n in SMALL_NAMES] + [repl[n].reshape(-1) for n in REPL_NAMES]


def _pack_small_local(d):
    flat = jnp.concatenate(_small_pieces(d, d))
    return _seg(flat, F32_SEG).reshape(-1, 128)


def _pack_small_grads(G):
    pieces = []
    for name in SMALL_NAMES:
        gs, ax = SHARDED[name]
        g = G[name]
        if name == 'ffn_conv_w':
            sh = jnp.swapaxes(g.reshape(3, N_DEV, -1), 0, 1)
        elif ax == 1:
            sh = g
        else:
            sh = jnp.swapaxes(g.reshape(g.shape[0], N_DEV, -1), 0, 1)
        pieces.append(sh.reshape(N_DEV, -1))
    rep = jnp.concatenate([G[n].reshape(-1) for n in REPL_NAMES])
    pieces.append(jnp.broadcast_to(rep[None, :], (N_DEV, rep.shape[0])))
    flat = _seg(jnp.concatenate(pieces, axis=1), F32_SEG)
    return flat.reshape(N_DEV, -1, 128)


def _unpack_small_local(packed, local_shapes):
    flat = packed.reshape(-1)
    out, off = {}, 0
    for name in SMALL_NAMES + REPL_NAMES:
        n = int(np.prod(local_shapes[name]))
        out[name] = flat[off:off + n].reshape(local_shapes[name])
        off += n
    return out


def kernel(x, *rest):
    nw = len(WEIGHT_NAMES)
    assert len(rest) == 3 * nw + 1
    weights = dict(zip(WEIGHT_NAMES, rest[:nw]))
    loss_target = rest[nw]
    moms = dict(zip(WEIGHT_NAMES, rest[nw + 1:2 * nw + 1]))
    vars_ = dict(zip(WEIGHT_NAMES, rest[2 * nw + 1:]))
    local_shapes = {n: weights[n].shape for n in WEIGHT_NAMES}

    def native2d(name, a):
        a2 = a.reshape(a.shape[-2], a.shape[-1])
        return a2.T if name in BIG_T else a2

    def from2d(name, a2):
        return (a2.T if name in BIG_T else a2).reshape(local_shapes[name])

    big_w = {n: native2d(n, weights[n]) for n in BIG_NAMES}
    w_in_t_sh = big_w['w_in'].astype(BF16)
    late = [big_w[n].astype(BF16) for n in LATE_NAMES]
    small_sh = _pack_small_shards({n: weights[n] for n in SMALL_NAMES})
    w_in_g, small_g = _exchange("weights_all_gather", [w_in_t_sh, small_sh], False)
    W = _unpack_small(small_g)
    W['w_in_t'] = w_in_g.reshape(-1, D_MODEL)
    for n in REPL_NAMES:
        W[n] = weights[n][0] if n != 'norm_final_g' else weights[n]

    loss, dx, G, late_parts = _local_step(x[0], loss_target[0], W, late)

    w_in_parts, small_parts = _exchange(
        "grads_all_to_all", [G['w_in_t'].reshape(N_DEV, -1, D_MODEL), _pack_small_grads(G)], True)
    results = {}
    for n, parts in zip(['w_in'] + LATE_NAMES, [w_in_parts] + list(late_parts)):
        res = _adamw("adamw_" + n, parts, big_w[n], native2d(n, moms[n]), native2d(n, vars_[n]))
        results[n] = [from2d(n, r) for r in res]
    small_res = _adamw("adamw_small", small_parts, _pack_small_local(weights), _pack_small_local(moms),
                       _pack_small_local(vars_))
    small_out = [_unpack_small_local(p, local_shapes) for p in small_res]

    loss = lax.psum(loss, ("x", "y", "c"))
    outs = [loss, dx[None]]
    for j in range(4):
        outs += [results[n][j] if n in results else small_out[j][n] for n in WEIGHT_NAMES]
    return tuple(outs)
```

```python
import functools
import math

import numpy as np
import jax
import jax.numpy as jnp
from jax import lax
from jax.experimental import pallas as pl
from jax.experimental.pallas import tpu as pltpu

F32 = jnp.float32
BF16 = jnp.bfloat16

N_DEV = 8
D_MODEL = 1024
RWKV_HEADS = 8
RWKV_HEAD_DIM = 64
RWKV_WIDTH = 512
RET_HEADS = 4
RET_HEAD_DIM = 128
RET_WIDTH = 512
LORA_PAD = 128
D_FF = 2816
NORM_EPS = 1e-6
RWKV_GN_EPS = 64e-5
RET_GN_EPS = 1e-5
ROPE_BASE = 10000.0
ADAM_LR, ADAM_B1, ADAM_B2, ADAM_EPS, ADAM_WD, ADAM_STEP = 0.001, 0.9, 0.999, 1e-08, 0.01, 10

VMEM_LIMIT = 56 * 1024 * 1024
TOK_BLOCK = 256
SCAN_CHUNK = 64
ATT_BLOCK = 512
BF16_SEG = 2048
F32_SEG = 1024

WEIGHT_NAMES = ['norm_mix_g', 'w_in', 'rwkv_mu_r', 'rwkv_mu_k', 'rwkv_mu_v', 'rwkv_mu_w', 'rwkv_mu_a',
                'rwkv_mu_g', 'rwkv_w0', 'rwkv_w1', 'rwkv_w2', 'rwkv_a0', 'rwkv_a1', 'rwkv_a2', 'rwkv_g1',
                'rwkv_g2', 'rwkv_k_k', 'rwkv_k_a', 'rwkv_r_k', 'rwkv_lnx_w', 'rwkv_lnx_b', 'ret_gn_w',
                'w_out', 'norm_ffn_g', 'ffn_w_gate', 'ffn_w_up', 'ffn_conv_w', 'ffn_conv_b', 'ffn_w_down',
                'norm_final_g']
SHARDED = {
    'w_in': ((1, 1024, 3584), 2), 'rwkv_w1': ((1, 1024, 64), 1), 'rwkv_w2': ((1, 64, 512), 2),
    'rwkv_a1': ((1, 1024, 64), 1), 'rwkv_a2': ((1, 64, 512), 2), 'rwkv_g1': ((1, 1024, 128), 1),
    'rwkv_g2': ((1, 128, 512), 2), 'w_out': ((1, 1024, 1024), 1), 'ffn_w_gate': ((1, 1024, 2816), 2),
    'ffn_w_up': ((1, 1024, 2816), 2), 'ffn_conv_w': ((1, 3, 1, 2816), 3), 'ffn_w_down': ((1, 2816, 1024), 1),
}
REPL_NAMES = [n for n in WEIGHT_NAMES if n not in SHARDED]
BIG_NAMES = ['w_in', 'w_out', 'ffn_w_gate', 'ffn_w_up', 'ffn_w_down']
BIG_T = ('w_in', 'ffn_w_gate', 'ffn_w_up')
LATE_NAMES = ['w_out', 'ffn_w_gate', 'ffn_w_up', 'ffn_w_down']
SMALL_NAMES = [n for n in WEIGHT_NAMES if n in SHARDED and n not in BIG_NAMES]


def _cparams(n_grid):
    return pltpu.CompilerParams(dimension_semantics=("arbitrary",) * n_grid, vmem_limit_bytes=VMEM_LIMIT)


def _round_up(n, m):
    return (n + m - 1) // m * m


@jax.custom_vjp
def _bdot(x, w):
    return jnp.dot(x.astype(BF16), w.astype(BF16), preferred_element_type=F32)


def _bdot_fwd(x, w):
    return _bdot(x, w), (x, w)


def _bdot_bwd(res, g):
    x, w = res
    gb = g.astype(BF16)
    dx = lax.dot_general(gb, w.astype(BF16), (((1,), (1,)), ((), ())), preferred_element_type=F32)
    dw = lax.dot_general(x.astype(BF16), gb, (((0,), (0,)), ((), ())), preferred_element_type=F32)
    return dx, dw.astype(w.dtype)


_bdot.defvjp(_bdot_fwd, _bdot_bwd)


@jax.custom_vjp
def _shift_rows(x, prev):
    rolled = pltpu.roll(x, 1, 0)
    row = lax.broadcasted_iota(jnp.int32, x.shape, 0)
    return jnp.where(row == 0, jnp.broadcast_to(prev, x.shape), rolled)


def _shift_rows_fwd(x, prev):
    return _shift_rows(x, prev), None


def _shift_rows_bwd(_, g):
    n = g.shape[0]
    rolled = pltpu.roll(g, n - 1, 0)
    row = lax.broadcasted_iota(jnp.int32, g.shape, 0)
    return jnp.where(row == n - 1, 0.0, rolled), g[0:1, :]


_shift_rows.defvjp(_shift_rows_fwd, _shift_rows_bwd)


@jax.custom_vjp
def _swap_halves(x):
    return pltpu.roll(x, 64, 1)


_swap_halves.defvjp(lambda x: (_swap_halves(x), None), lambda _, g: (pltpu.roll(g, 64, 1),))


def _sigmoid(x):
    return 1.0 / (1.0 + jnp.exp(-x))


def _softplus(x):
    return jnp.maximum(x, 0.0) + jnp.log(1.0 + jnp.exp(-jnp.abs(x)))


def _rms_fn(x, g):
    return x * lax.rsqrt(jnp.mean(x * x, axis=-1, keepdims=True) + NORM_EPS) * g


def _pre_a_fn(h1, h1p, p, pp, mu_w, mu_a, mu_g, mu_r, mu_k, mu_v, w0, w1, w2, a0, a1, a2, g1, g2):
    W = RWKV_WIDTH
    h1s = _shift_rows(h1, h1p)
    ps = _shift_rows(p, pp)
    dx = h1s - h1
    xw = h1 + dx * mu_w
    xa = h1 + dx * mu_a
    xg = h1 + dx * mu_g
    dp = ps - p
    r = p[:, 0:W] + dp[:, 0:W] * mu_r
    k0 = p[:, W:2 * W] + dp[:, W:2 * W] * mu_k
    v = p[:, 2 * W:3 * W] + dp[:, 2 * W:3 * W] * mu_v
    wl = w0 + _bdot(jnp.tanh(_bdot(xw, w1)), w2)
    w_log = -_softplus(-wl) - 0.5
    lw = -jnp.exp(w_log)
    a = _sigmoid(a0 + _bdot(_bdot(xa, a1), a2))
    g = _bdot(_sigmoid(_bdot(xg, g1)), g2)
    return r, k0, v, lw, a, g


def _head_sum_raw(x):
    n = x.shape[1]
    ii = lax.broadcasted_iota(jnp.int32, (n, n), 0) // RWKV_HEAD_DIM
    jj = lax.broadcasted_iota(jnp.int32, (n, n), 1) // RWKV_HEAD_DIM
    ones = (ii == jj).astype(BF16)
    xh = x.astype(BF16)
    xl = (x - xh.astype(F32)).astype(BF16)
    return jnp.dot(xh, ones, preferred_element_type=F32) + jnp.dot(xl, ones, preferred_element_type=F32)


@jax.custom_vjp
def _head_sum(x):
    return _head_sum_raw(x)


_head_sum.defvjp(lambda x: (_head_sum_raw(x), None), lambda _, g: (_head_sum_raw(g),))


def _pre_b_fn(k0, a, k_k, k_a):
    kkr = k0 * k_k
    nrm = jnp.sqrt(_head_sum(kkr * kkr))
    kk = kkr / jnp.maximum(nrm, 1e-12)
    k = k0 * (1.0 + (a - 1.0) * k_a)
    return -kk, k, kk * a


def _rwkv_post_fn(y, r, k, v, g, lnx_w, lnx_b, r_k):
    inv = 1.0 / RWKV_HEAD_DIM
    mu = _head_sum(y) * inv
    yc = y - mu
    var = _head_sum(yc * yc) * inv
    yn = yc * lax.rsqrt(var + RWKV_GN_EPS) * lnx_w + lnx_b
    bonus = _head_sum(r * k * r_k) * v
    return ((yn + bonus) * g,)


def _rotary_fn(cos2, sin2, q, k):
    qs, ks = [], []
    for h in range(RET_HEADS):
        sl = slice(h * RET_HEAD_DIM, (h + 1) * RET_HEAD_DIM)
        qh, kh = q[:, sl], k[:, sl]
        qs.append(qh * cos2 + _swap_halves(qh) * sin2)
        ks.append((kh * cos2 + _swap_halves(kh) * sin2) * (RET_HEAD_DIM ** -0.5))
    return jnp.concatenate(qs, axis=1), jnp.concatenate(ks, axis=1)


def _ret_post_fn(y, gp, gn_w):
    outs = []
    for h in range(RET_HEADS):
        sl = slice(h * RET_HEAD_DIM, (h + 1) * RET_HEAD_DIM)
        yh = y[:, sl]
        mu = jnp.mean(yh, axis=-1, keepdims=True)
        yc = yh - mu
        var = jnp.mean(yc * yc, axis=-1, keepdims=True)
        outs.append(yc * lax.rsqrt(var + RET_GN_EPS) * gn_w[:, sl])
    yn = jnp.concatenate(outs, axis=1)
    return (gp * _sigmoid(gp) * yn,)


def _tok_axis(a):
    return 1 if a.ndim == 3 else 0


def _blk_spec(a, tb, rev_nb=None):
    nd, ax = a.ndim, _tok_axis(a)
    shape = a.shape[:ax] + (tb,) + a.shape[ax + 1:]

    def imap(i):
        idx = [0] * nd
        idx[ax] = i if rev_nb is None else rev_nb - 1 - i
        return tuple(idx)

    return pl.BlockSpec(shape, imap)


def _full_spec(a):
    nd = a.ndim
    return pl.BlockSpec(a.shape, lambda i: (0,) * nd)


def _tok_fwd(name, fn, toks, consts, out_tails, tb=TOK_BLOCK):
    n_in = len(toks) + len(consts)
    tn = toks[0].shape[_tok_axis(toks[0])]

    def body(*refs):
        outs = fn(*[r[...] for r in refs[:n_in]])
        for r, o in zip(refs[n_in:], outs):
            r[...] = o

    out_shape = [jax.ShapeDtypeStruct((tn,) + tuple(s) if len(s) == 1 else (s[0], tn, s[1]), F32) for s in out_tails]
    return pl.pallas_call(
        body, name=name, grid=(tn // tb,),
        in_specs=[_blk_spec(a, tb) for a in toks] + [_full_spec(c) for c in consts],
        out_specs=[_blk_spec(o, tb) for o in out_shape], out_shape=out_shape,
        compiler_params=_cparams(1))(*toks, *consts)


def _tok_bwd(name, fn, aux, toks, consts, cts, add=None, tb=TOK_BLOCK):
    n_aux, n_tok, n_c = len(aux), len(toks), len(consts)
    ct_groups = [c if isinstance(c, (tuple, list)) else (c,) for c in cts]
    ct_flat = [a for grp in ct_groups for a in grp]
    n_ct = len(ct_flat)
    n_add = 0 if add is None else 1
    tn = toks[0].shape[_tok_axis(toks[0])]

    def body(*refs):
        pos = 0
        aux_v = [r[...] for r in refs[pos:pos + n_aux]]; pos += n_aux
        tok_v = [r[...] for r in refs[pos:pos + n_tok]]; pos += n_tok
        const_v = [r[...] for r in refs[pos:pos + n_c]]; pos += n_c
        ct_refs = refs[pos:pos + n_ct]; pos += n_ct
        add_refs = refs[pos:pos + n_add]; pos += n_add
        dtok_refs = refs[pos:pos + n_tok]; pos += n_tok
        dconst_refs = refs[pos:pos + n_c]
        ct_v, q = [], 0
        for grp in ct_groups:
            s = ct_refs[q][...]
            for r in ct_refs[q + 1:q + len(grp)]:
                s = s + r[...]
            q += len(grp)
            ct_v.append(s)
        _, vjp = jax.vjp(lambda *tc: fn(*aux_v, *tc), *tok_v, *const_v)
        grads = vjp(tuple(ct_v))
        for j, r in enumerate(dtok_refs):
            gj = grads[j]
            if j == 0 and n_add:
                gj = gj + add_refs[0][...]
            r[...] = gj

        @pl.when(pl.program_id(0) == 0)
        def _():
            for r in dconst_refs:
                r[...] = jnp.zeros(r.shape, F32)

        for j, r in enumerate(dconst_refs):
            r[...] += grads[n_tok + j]

    ins = list(aux) + list(toks) + list(consts) + ct_flat + ([add] if n_add else [])
    in_specs = ([_blk_spec(a, tb) for a in aux] + [_blk_spec(a, tb) for a in toks] + [_full_spec(c) for c in consts]
                + [_blk_spec(a, tb) for a in ct_flat] + ([_blk_spec(add, tb)] if n_add else []))
    out_shape = [jax.ShapeDtypeStruct(a.shape, F32) for a in toks] + [jax.ShapeDtypeStruct(c.shape, F32) for c in consts]
    out_specs = [_blk_spec(a, tb) for a in toks] + [_full_spec(c) for c in consts]
    return pl.pallas_call(body, name=name, grid=(tn // tb,), in_specs=in_specs, out_specs=out_specs,
                          out_shape=out_shape, compiler_params=_cparams(1))(*ins)


MM_VMEM_BUDGET = 40 * 1024 * 1024
MM_STEP_SECONDS = 0.4e-6
MM_HBM_BYTES_PER_SECOND = 2.5e12


def _mm_tiles(m, n, kd, a_bytes, b_bytes, o_bytes, has_add):
    divs = lambda d: [t for t in range(128, d + 1, 128) if d % t == 0]
    best = None
    for tm in divs(m):
        for tn in divs(n):
            for tk in divs(kd):
                ni, nj, nk = m // tm, n // tn, kd // tk
                vmem = (2 * tm * tk * a_bytes + 2 * tk * tn * b_bytes + tm * tn * 4 + 2 * tm * tn * o_bytes
                        + (2 * tm * tn * 4 if has_add else 0) + 2 * (tm * tk + tk * tn) + tm * tn * 4)
                if vmem > MM_VMEM_BUDGET:
                    continue
                a_traffic = m * kd * a_bytes * (nj if nk > 1 else 1)
                b_traffic = kd * n * b_bytes * (ni if nj * nk > 1 else 1)
                cost = ni * nj * nk * MM_STEP_SECONDS + (a_traffic + b_traffic) / MM_HBM_BYTES_PER_SECOND
                if best is None or cost < best[0]:
                    best = (cost, tm, tn, tk)
    return best[1:]


def _mm(name, a, b, ta=False, tb=False, add=None, out_dtype=F32):
    if ta:
        kd, m = a.shape
    else:
        m, kd = a.shape
    if tb:
        n, kb = b.shape
    else:
        kb, n = b.shape
    assert kd == kb, (a.shape, b.shape)
    tm, tn, tk = _mm_tiles(m, n, kd, a.dtype.itemsize, b.dtype.itemsize, jnp.dtype(out_dtype).itemsize,
                           add is not None)
    nk = kd // tk
    has_add = add is not None
    dims = (((0 if ta else 1,), (1 if tb else 0,)), ((), ()))

    def body(*refs):
        a_ref, b_ref = refs[0], refs[1]
        o_ref, acc_ref = refs[-2], refs[-1]
        k = pl.program_id(2)

        @pl.when(k == 0)
        def _():
            acc_ref[...] = refs[2][...] if has_add else jnp.zeros(acc_ref.shape, F32)

        acc_ref[...] += lax.dot_general(a_ref[...].astype(BF16), b_ref[...].astype(BF16), dims,
                                        preferred_element_type=F32)

        @pl.when(k == nk - 1)
        def _():
            o_ref[...] = acc_ref[...].astype(out_dtype)

    a_spec = pl.BlockSpec((tk, tm), lambda i, j, k: (k, i)) if ta else pl.BlockSpec((tm, tk), lambda i, j, k: (i, k))
    b_spec = pl.BlockSpec((tn, tk), lambda i, j, k: (j, k)) if tb else pl.BlockSpec((tk, tn), lambda i, j, k: (k, j))
    o_spec = pl.BlockSpec((tm, tn), lambda i, j, k: (i, j))
    ins = [a, b] + ([add] if has_add else [])
    in_specs = [a_spec, b_spec] + ([o_spec] if has_add else [])
    return pl.pallas_call(body, name=name, grid=(m // tm, n // tn, nk), in_specs=in_specs, out_specs=o_spec,
                          out_shape=jax.ShapeDtypeStruct((m, n), out_dtype),
                          scratch_shapes=[pltpu.VMEM((tm, tn), F32)], compiler_params=_cparams(3))(*ins)


def _prev8_spec(a, tb, rev_nb=None):
    r = tb // 8
    if rev_nb is None:
        return pl.BlockSpec((8, a.shape[1]), lambda i: (jnp.maximum(i * r - 1, 0), 0))
    return pl.BlockSpec((8, a.shape[1]), lambda i: (jnp.maximum((rev_nb - 1 - i) * r - 1, 0), 0))


def _pre_a_fwd(h1, p, consts, tb=TOK_BLOCK):
    tn = h1.shape[0]

    def body(h1_ref, h1h_ref, p_ref, ph_ref, *rest):
        c_refs, o_refs = rest[:len(consts)], rest[len(consts):]
        first = pl.program_id(0) == 0
        h1p = jnp.where(first, 0.0, h1h_ref[7:8, :])
        pp = jnp.where(first, 0.0, ph_ref[7:8, :])
        outs = _pre_a_fn(h1_ref[...], h1p, p_ref[...], pp, *[c[...] for c in c_refs])
        for r, o in zip(o_refs, outs):
            r[...] = o

    out_shape = [jax.ShapeDtypeStruct((tn, RWKV_WIDTH), F32) for _ in range(6)]
    return pl.pallas_call(
        body, name="rwkv_pre_a_fwd", grid=(tn // tb,),
        in_specs=[_blk_spec(h1, tb), _prev8_spec(h1, tb), _blk_spec(p, tb), _prev8_spec(p, tb)]
        + [_full_spec(c) for c in consts],
        out_specs=[_blk_spec(o, tb) for o in out_shape], out_shape=out_shape,
        compiler_params=_cparams(1))(h1, h1, p, p, *consts)


def _pre_a_bwd(h1, p, consts, cts, tb=TOK_BLOCK):
    tn = h1.shape[0]
    nb = tn // tb
    n_c = len(consts)
    ct_groups = [c if isinstance(c, (tuple, list)) else (c,) for c in cts]
    ct_flat = [a for grp in ct_groups for a in grp]
    n_ct = len(ct_flat)

    def body(*refs):
        h1_ref, h1h_ref, p_ref, ph_ref = refs[:4]
        c_refs = refs[4:4 + n_c]
        ct_refs = refs[4 + n_c:4 + n_c + n_ct]
        dh1_ref, dp_ref = refs[4 + n_c + n_ct:6 + n_c + n_ct]
        dc_refs = refs[6 + n_c + n_ct:6 + 2 * n_c + n_ct]
        ch_ref, cp_ref = refs[-2], refs[-1]
        i = pl.program_id(0)
        first_block = i == nb - 1
        h1p = jnp.where(first_block, 0.0, h1h_ref[7:8, :])
        pp = jnp.where(first_block, 0.0, ph_ref[7:8, :])
        ct_v, q = [], 0
        for grp in ct_groups:
            s = ct_refs[q][...]
            for r in ct_refs[q + 1:q + len(grp)]:
                s = s + r[...]
            q += len(grp)
            ct_v.append(s)
        _, vjp = jax.vjp(_pre_a_fn, h1_ref[...], h1p, p_ref[...], pp, *[c[...] for c in c_refs])
        grads = vjp(tuple(ct_v))

        @pl.when(i == 0)
        def _():
            ch_ref[...] = jnp.zeros(ch_ref.shape, F32)
            cp_ref[...] = jnp.zeros(cp_ref.shape, F32)
            for r in dc_refs:
                r[...] = jnp.zeros(r.shape, F32)

        rowh = lax.broadcasted_iota(jnp.int32, (tb, h1.shape[1]), 0)
        rowp = lax.broadcasted_iota(jnp.int32, (tb, p.shape[1]), 0)
        dh1_ref[...] = grads[0] + jnp.where(rowh == tb - 1, jnp.broadcast_to(ch_ref[0:1, :], rowh.shape), 0.0)
        dp_ref[...] = grads[2] + jnp.where(rowp == tb - 1, jnp.broadcast_to(cp_ref[0:1, :], rowp.shape), 0.0)
        ch_ref[0:1, :] = grads[1]
        cp_ref[0:1, :] = grads[3]
        for j, r in enumerate(dc_refs):
            r[...] += grads[4 + j]

    ins = [h1, h1, p, p] + list(consts) + ct_flat
    in_specs = ([_blk_spec(h1, tb, nb), _prev8_spec(h1, tb, nb), _blk_spec(p, tb, nb), _prev8_spec(p, tb, nb)]
                + [_full_spec(c) for c in consts] + [_blk_spec(a, tb, nb) for a in ct_flat])
    out_shape = ([jax.ShapeDtypeStruct(h1.shape, F32), jax.ShapeDtypeStruct(p.shape, F32)]
                 + [jax.ShapeDtypeStruct(c.shape, F32) for c in consts])
    out_specs = [_blk_spec(h1, tb, nb), _blk_spec(p, tb, nb)] + [_full_spec(c) for c in consts]
    return pl.pallas_call(body, name="rwkv_pre_a_bwd", grid=(nb,), in_specs=in_specs, out_specs=out_specs,
                          out_shape=out_shape,
                          scratch_shapes=[pltpu.VMEM((8, h1.shape[1]), F32), pltpu.VMEM((8, p.shape[1]), F32)],
                          compiler_params=_cparams(1))(*ins)


def _my_index():
    return 4 * lax.axis_index("x") + 2 * lax.axis_index("y") + lax.axis_index("c")


def _peer(k):
    x, y, c = lax.axis_index("x"), lax.axis_index("y"), lax.axis_index("c")
    px = 1 - x if k & 4 else x
    py = 1 - y if k & 2 else y
    pc = 1 - c if k & 1 else c
    return (px, py, pc), 4 * px + 2 * py + pc


def _xchg_sems(n):
    return [pltpu.SemaphoreType.DMA((n * (N_DEV - 1),)), pltpu.SemaphoreType.DMA((n * (N_DEV - 1),)),
            pltpu.SemaphoreType.DMA((n,))]


def _xchg_copies(srcs, dsts, sems, scatter, incoming=False):
    send_sems, recv_sems, local_sems = sems
    me = _my_index()
    local, remote = [], []
    for i, (s, d) in enumerate(zip(srcs, dsts)):
        if not incoming:
            local.append(pltpu.make_async_copy(s.at[me] if scatter else s, d.at[me], local_sems.at[i]))
        for k in range(1, N_DEV):
            peer, plin = _peer(k)
            j = i * (N_DEV - 1) + k - 1
            s_slot, d_slot = (me, plin) if incoming else (plin, me)
            remote.append(pltpu.make_async_remote_copy(
                src_ref=s.at[s_slot] if scatter else s, dst_ref=d.at[d_slot], send_sem=send_sems.at[j],
                recv_sem=recv_sems.at[j], device_id=peer, device_id_type=pl.DeviceIdType.MESH))
    return local, remote


def _xchg_start(srcs, dsts, sems, scatter):
    local, out = _xchg_copies(srcs, dsts, sems, scatter)
    for cp in local + out:
        cp.start()


def _xchg_wait(srcs, dsts, sems, scatter):
    for cp in _xchg_copies(srcs, dsts, sems, scatter, incoming=True)[1]:
        cp.wait_recv()
    local, out = _xchg_copies(srcs, dsts, sems, scatter)
    for cp in out:
        cp.wait_send()
    for cp in local:
        cp.wait()


def _xchg_out_shapes(srcs, scatter):
    return [jax.ShapeDtypeStruct(s.shape if scatter else (N_DEV,) + s.shape, s.dtype) for s in srcs]


_ANY = pl.BlockSpec(memory_space=pl.ANY)


def _exchange(name, srcs, scatter):
    n = len(srcs)

    def body(*refs):
        s, d, sems = refs[:n], refs[n:2 * n], refs[2 * n:]
        _xchg_start(s, d, sems, scatter)
        _xchg_wait(s, d, sems, scatter)

    return pl.pallas_call(body, name=name, in_specs=[_ANY] * n, out_specs=[_ANY] * n,
                          out_shape=_xchg_out_shapes(srcs, scatter), scratch_shapes=_xchg_sems(n))(*srcs)


_MM_DIMS = {'nn': (((1,), (0,)), ((), ())), 'nt': (((1,), (1,)), ((), ())), 'tn': (((0,), (0,)), ((), ()))}


def _cmm_raw(x, y, kind, split):
    dot = functools.partial(lax.dot_general, dimension_numbers=_MM_DIMS[kind], preferred_element_type=F32)
    xh, yh = x.astype(BF16), y.astype(BF16)
    out = dot(xh, yh)
    if split:
        xl = (x - xh.astype(F32)).astype(BF16)
        yl = (y - yh.astype(F32)).astype(BF16)
        out = out + (dot(xh, yl) + dot(xl, yh))
    return out


@functools.partial(jax.custom_vjp, nondiff_argnums=(2, 3))
def _cmm(x, y, kind, split=False):
    return _cmm_raw(x, y, kind, split)


def _cmm_fwd(x, y, kind, split):
    return _cmm_raw(x, y, kind, split), (x, y)


def _cmm_bwd(kind, split, res, g):
    x, y = res
    if kind == 'nn':
        return _cmm_raw(g, y, 'nt', split), _cmm_raw(x, g, 'tn', split)
    if kind == 'nt':
        return _cmm_raw(g, y, 'nn', split), _cmm_raw(g, x, 'tn', split)
    return _cmm_raw(y, g, 'nt', split), _cmm_raw(x, g, 'nn', split)


_cmm.defvjp(_cmm_fwd, _cmm_bwd)


def _tri_sum_raw(tri, x, kind):
    dot = functools.partial(lax.dot_general, dimension_numbers=_MM_DIMS[kind], preferred_element_type=F32)
    tb = tri.astype(BF16)
    hi, mid, lo = _split3(x)
    return (dot(tb, hi) + dot(tb, mid)) + dot(tb, lo)


@functools.partial(jax.custom_vjp, nondiff_argnums=(2,))
def _tri_sum(tri, x, kind):
    return _tri_sum_raw(tri, x, kind)


def _tri_sum_fwd(tri, x, kind):
    return _tri_sum_raw(tri, x, kind), tri


def _tri_sum_bwd(kind, tri, g):
    return jnp.zeros_like(tri), _tri_sum_raw(tri, g, 'tn' if kind == 'nn' else 'nn')


_tri_sum.defvjp(_tri_sum_fwd, _tri_sum_bwd)


def _chunk_fn(S0, r, lw, k, v, a, b):
    hs = range(len(r))
    C = r[0].shape[0]
    ii = lax.broadcasted_iota(jnp.int32, (C, C), 0)
    jj = lax.broadcasted_iota(jnp.int32, (C, C), 1)
    incl, strict = ii >= jj, ii > jj
    eye = (ii == jj).astype(F32)
    inclf = incl.astype(F32)
    cum = [_tri_sum(inclf, lw[h], 'nn') for h in hs]
    e_inv = [jnp.exp(-cum[h]) for h in hs]
    At = [a[h] * jnp.exp(cum[h] - lw[h]) for h in hs]
    Rt = [r[h] * jnp.exp(cum[h]) for h in hs]
    Kh = [k[h] * e_inv[h] for h in hs]
    Bh = [b[h] * e_inv[h] for h in hs]
    Mab = [jnp.where(strict, _cmm(At[h], Bh[h], 'nt'), 0.0) for h in hs]
    Mak = [jnp.where(strict, _cmm(At[h], Kh[h], 'nt'), 0.0) for h in hs]
    Mrk = [jnp.where(incl, _cmm(Rt[h], Kh[h], 'nt'), 0.0) for h in hs]
    Mrb = [jnp.where(incl, _cmm(Rt[h], Bh[h], 'nt'), 0.0) for h in hs]
    rhs = [_cmm(At[h], S0[h], 'nt') + _cmm(Mak[h], v[h], 'nn') for h in hs]
    P = Mab
    Tm = [eye + P[h] for h in hs]
    n = 1
    while 2 * n < C:
        P = [_cmm(P[h], P[h], 'nn', True) for h in hs]
        Tm = [_cmm(Tm[h], eye + P[h], 'nn', True) for h in hs]
        n *= 2
    U = [_cmm(Tm[h], rhs[h], 'nn', True) for h in hs]
    Y = [_cmm(Rt[h], S0[h], 'nt') + _cmm(Mrk[h], v[h], 'nn') + _cmm(Mrb[h], U[h], 'nn') for h in hs]
    gC = [jnp.exp(jnp.sum(lw[h], axis=0, keepdims=True)) for h in hs]
    SC = [S0[h] * gC[h] + _cmm(v[h], Kh[h] * gC[h], 'tn') + _cmm(U[h], Bh[h] * gC[h], 'tn') for h in hs]
    return tuple(Y), tuple(SC)


def _cscan_fwd(r, lw, k, v, a, b, xs):
    n_x = len(xs)
    tn = r.shape[0]
    H, Dh = RWKV_HEADS, RWKV_HEAD_DIM
    nc = tn // SCAN_CHUNK
    lanes = lambda h: slice(h * Dh, (h + 1) * Dh)
    heads = lambda ref: tuple(ref[:, lanes(h)] for h in range(H))
    mats = lambda ref: tuple(ref[h] for h in range(H))

    def body(r_ref, lw_ref, k_ref, v_ref, a_ref, b_ref, *rest):
        x_src, (y_ref, ck_ref) = rest[:n_x], rest[n_x:n_x + 2]
        x_dst, s_ref, sems = rest[n_x + 2:2 * n_x + 2], rest[2 * n_x + 2], rest[2 * n_x + 3:]

        @pl.when(pl.program_id(0) == 0)
        def _():
            s_ref[...] = jnp.zeros(s_ref.shape, F32)
            _xchg_start(x_src, x_dst, sems, False)

        ck_ref[0] = s_ref[...]
        y, sc = _chunk_fn(mats(s_ref), heads(r_ref), heads(lw_ref), heads(k_ref), heads(v_ref), heads(a_ref),
                          heads(b_ref))
        for h in range(H):
            y_ref[:, lanes(h)] = y[h]
            s_ref[h] = sc[h]

        @pl.when(pl.program_id(0) == nc - 1)
        def _():
            _xchg_wait(x_src, x_dst, sems, False)

    hm = pl.BlockSpec((SCAN_CHUNK, H * Dh), lambda c: (c, 0))
    res = pl.pallas_call(
        body, name="rwkv_scan_fwd", grid=(nc,), in_specs=[hm] * 6 + [_ANY] * n_x,
        out_specs=[hm, pl.BlockSpec((1, H, Dh, Dh), lambda c: (c, 0, 0, 0))] + [_ANY] * n_x,
        out_shape=[jax.ShapeDtypeStruct((tn, H * Dh), F32), jax.ShapeDtypeStruct((nc, H, Dh, Dh), F32)]
        + _xchg_out_shapes(xs, False),
        scratch_shapes=[pltpu.VMEM((H, Dh, Dh), F32)] + _xchg_sems(n_x),
        compiler_params=_cparams(1))(r, lw, k, v, a, b, *xs)
    return res[0], res[1], res[2:]


def _cscan_bwd(r, lw, k, v, a, b, dy, ck, xs):
    n_x = len(xs)
    tn = r.shape[0]
    H, Dh = RWKV_HEADS, RWKV_HEAD_DIM
    nc = tn // SCAN_CHUNK
    lanes = lambda h: slice(h * Dh, (h + 1) * Dh)
    heads = lambda ref: tuple(ref[:, lanes(h)] for h in range(H))
    mats = lambda ref: tuple(ref[h] for h in range(H))

    def body(r_ref, lw_ref, k_ref, v_ref, a_ref, b_ref, dy_ref, ck_ref, *rest):
        x_src = rest[:n_x]
        d_refs = rest[n_x:n_x + 6]
        x_dst = rest[n_x + 6:2 * n_x + 6]
        g_ref = rest[2 * n_x + 6]
        sems = rest[2 * n_x + 7:]

        @pl.when(pl.program_id(0) == 0)
        def _():
            g_ref[...] = jnp.zeros(g_ref.shape, F32)
            _xchg_start(x_src, x_dst, sems, True)

        s0 = tuple(ck_ref[0, h] for h in range(H))
        _, vjp = jax.vjp(_chunk_fn, s0, heads(r_ref), heads(lw_ref), heads(k_ref), heads(v_ref), heads(a_ref),
                         heads(b_ref))
        grads = vjp((heads(dy_ref), mats(g_ref)))
        for h in range(H):
            g_ref[h] = grads[0][h]
            for d_ref, gz in zip(d_refs, grads[1:]):
                d_ref[:, lanes(h)] = gz[h]

        @pl.when(pl.program_id(0) == nc - 1)
        def _():
            _xchg_wait(x_src, x_dst, sems, True)

    hm = pl.BlockSpec((SCAN_CHUNK, H * Dh), lambda c: (nc - 1 - c, 0))
    hshape = jax.ShapeDtypeStruct((tn, H * Dh), F32)
    res = pl.pallas_call(
        body, name="rwkv_scan_bwd", grid=(nc,),
        in_specs=[hm] * 7 + [pl.BlockSpec((1, H, Dh, Dh), lambda c: (nc - 1 - c, 0, 0, 0))] + [_ANY] * n_x,
        out_specs=[hm] * 6 + [_ANY] * n_x, out_shape=[hshape] * 6 + _xchg_out_shapes(xs, True),
        scratch_shapes=[pltpu.VMEM((H, Dh, Dh), F32)] + _xchg_sems(n_x),
        compiler_params=_cparams(1))(r, lw, k, v, a, b, dy, ck, *xs)
    return res[:6], res[6:]


def _decay_mask(lg, i, j, blk):
    rows = lax.broadcasted_iota(jnp.int32, (blk, blk), 0)
    cols = lax.broadcasted_iota(jnp.int32, (blk, blk), 1)
    dd = (rows - cols + (i - j) * blk).astype(F32)
    return jnp.where(dd >= 0.0, jnp.exp(lg * jnp.maximum(dd, 0.0)), 0.0)


_NT = (((1,), (1,)), ((), ()))
_TN = (((0,), (0,)), ((), ()))


def _ret_attn_fwd(lg, q, k, v, blk=ATT_BLOCK):
    tn = q.shape[0]
    Dh = RET_HEAD_DIM

    def body(lg_ref, q_ref, k_ref, v_ref, o_ref):
        i = pl.program_id(1)
        lgv = lg_ref[0][:, 0:1]
        qb = q_ref[...].astype(BF16)

        def jb(j, acc):
            ks = pl.ds(pl.multiple_of(j * blk, blk), blk)
            s = lax.dot_general(qb, k_ref[ks, :].astype(BF16), _NT, preferred_element_type=F32)
            s = s * _decay_mask(lgv, i, j, blk)
            return acc + jnp.dot(s.astype(BF16), v_ref[ks, :].astype(BF16), preferred_element_type=F32)

        o_ref[...] = lax.fori_loop(0, i + 1, jb, jnp.zeros((blk, Dh), F32))

    full = pl.BlockSpec((tn, Dh), lambda h, i: (0, h))
    qs = pl.BlockSpec((blk, Dh), lambda h, i: (i, h))
    return pl.pallas_call(
        body, name="ret_attn_fwd", grid=(RET_HEADS, tn // blk),
        in_specs=[pl.BlockSpec((1, 1, 128), lambda h, i: (h, 0, 0)), qs, full, full],
        out_specs=qs, out_shape=jax.ShapeDtypeStruct(q.shape, F32), compiler_params=_cparams(2))(lg, q, k, v)


def _ret_attn_bwd(lg, q, k, v, do, blk=ATT_BLOCK):
    tn = q.shape[0]
    nb = tn // blk
    Dh = RET_HEAD_DIM

    def body(lg_ref, q_ref, k_ref, v_ref, do_ref, dq_ref, dk_ref, dv_ref):
        lgv = lg_ref[0][:, 0:1]
        dk_ref[...] = jnp.zeros(dk_ref.shape, F32)
        dv_ref[...] = jnp.zeros(dv_ref.shape, F32)

        def ib(i, carry):
            qs = pl.ds(pl.multiple_of(i * blk, blk), blk)
            qb = q_ref[qs, :].astype(BF16)
            dob = do_ref[qs, :].astype(BF16)

            def jb(j, dq):
                ks = pl.ds(pl.multiple_of(j * blk, blk), blk)
                kb = k_ref[ks, :].astype(BF16)
                vb = v_ref[ks, :].astype(BF16)
                dm = _decay_mask(lgv, i, j, blk)
                s = lax.dot_general(qb, kb, _NT, preferred_element_type=F32) * dm
                ds = lax.dot_general(dob, vb, _NT, preferred_element_type=F32) * dm
                sb, dsb = s.astype(BF16), ds.astype(BF16)
                dv_ref[ks, :] += lax.dot_general(sb, dob, _TN, preferred_element_type=F32)
                dk_ref[ks, :] += lax.dot_general(dsb, qb, _TN, preferred_element_type=F32)
                return dq + jnp.dot(dsb, kb, preferred_element_type=F32)

            dq_ref[qs, :] = lax.fori_loop(0, i + 1, jb, jnp.zeros((blk, Dh), F32))
            return carry

        lax.fori_loop(0, nb, ib, 0)

    full = pl.BlockSpec((tn, Dh), lambda h: (0, h))
    sh = jax.ShapeDtypeStruct(q.shape, F32)
    return pl.pallas_call(
        body, name="ret_attn_bwd", grid=(RET_HEADS,),
        in_specs=[pl.BlockSpec((1, 1, 128), lambda h: (h, 0, 0)), full, full, full, full],
        out_specs=[full, full, full], out_shape=[sh, sh, sh], compiler_params=_cparams(1))(lg, q, k, v, do)


def _next8_spec(a, tb):
    r = tb // 8
    last = a.shape[0] // 8 - 1
    return pl.BlockSpec((8, a.shape[1]), lambda i: (jnp.minimum((i + 1) * r, last), 0))


def _conv_taps(g_ext, cw_ref, cb_ref):
    return (cw_ref[2:3, :] * g_ext + cw_ref[1:2, :] * pltpu.roll(g_ext, 1, 0)
            + cw_ref[0:1, :] * pltpu.roll(g_ext, 2, 0) + cb_ref[...])


def _glu_fwd(gate, up, cw, cb, tb=TOK_BLOCK):
    tn = gate.shape[0]

    def body(g_ref, gh_ref, u_ref, cw_ref, cb_ref, o_ref):
        halo = jnp.where(pl.program_id(0) == 0, 0.0, gh_ref[...])
        g_ext = jnp.concatenate([halo, g_ref[...]], axis=0)
        gc = _conv_taps(g_ext, cw_ref, cb_ref)[8:, :]
        o_ref[...] = gc * _sigmoid(gc) * u_ref[...]

    return pl.pallas_call(
        body, name="glu_fwd", grid=(tn // tb,),
        in_specs=[_blk_spec(gate, tb), _prev8_spec(gate, tb), _blk_spec(up, tb), _full_spec(cw), _full_spec(cb)],
        out_specs=_blk_spec(gate, tb), out_shape=jax.ShapeDtypeStruct(gate.shape, F32),
        compiler_params=_cparams(1))(gate, gate, up, cw, cb)


def _glu_bwd(gate, up, dact, cw, cb, tb=TOK_BLOCK):
    tn = gate.shape[0]
    nb = tn // tb

    def body(g_ref, gp_ref, gn_ref, u_ref, un_ref, d_ref, dn_ref, cw_ref, cb_ref, dg_ref, du_ref, dcw_ref, dcb_ref):
        i = pl.program_id(0)
        gprev = jnp.where(i == 0, 0.0, gp_ref[...])
        dnext = jnp.where(i == nb - 1, 0.0, dn_ref[...])
        g_ext = jnp.concatenate([gprev, g_ref[...], gn_ref[...]], axis=0)
        gc = _conv_taps(g_ext, cw_ref, cb_ref)[8:, :]
        u_e = jnp.concatenate([u_ref[...], un_ref[...]], axis=0)
        d_e = jnp.concatenate([d_ref[...], dnext], axis=0)
        s = _sigmoid(gc)
        dgc = d_e * u_e * (s * (1.0 + gc * (1.0 - s)))
        du_ref[...] = d_ref[...] * (gc * s)[:tb, :]
        n_e = tb + 8
        dg_ref[...] = (cw_ref[2:3, :] * dgc + cw_ref[1:2, :] * pltpu.roll(dgc, n_e - 1, 0)
                       + cw_ref[0:1, :] * pltpu.roll(dgc, n_e - 2, 0))[:tb, :]

        @pl.when(i == 0)
        def _():
            dcw_ref[...] = jnp.zeros(dcw_ref.shape, F32)
            dcb_ref[...] = jnp.zeros(dcb_ref.shape, F32)

        dgc_b = dgc[:tb, :]
        g0 = g_ext[8:8 + tb, :]
        g1 = pltpu.roll(g_ext, 1, 0)[8:8 + tb, :]
        g2 = pltpu.roll(g_ext, 2, 0)[8:8 + tb, :]
        dcw_ref[2:3, :] += jnp.sum(dgc_b * g0, axis=0, keepdims=True)
        dcw_ref[1:2, :] += jnp.sum(dgc_b * g1, axis=0, keepdims=True)
        dcw_ref[0:1, :] += jnp.sum(dgc_b * g2, axis=0, keepdims=True)
        dcb_ref[...] += jnp.sum(dgc_b, axis=0, keepdims=True)

    sh = jax.ShapeDtypeStruct(gate.shape, F32)
    return pl.pallas_call(
        body, name="glu_bwd", grid=(nb,),
        in_specs=[_blk_spec(gate, tb), _prev8_spec(gate, tb), _next8_spec(gate, tb), _blk_spec(up, tb),
                  _next8_spec(up, tb), _blk_spec(dact, tb), _next8_spec(dact, tb), _full_spec(cw), _full_spec(cb)],
        out_specs=[_blk_spec(gate, tb), _blk_spec(gate, tb), _full_spec(cw), _full_spec(cb)],
        out_shape=[sh, sh, jax.ShapeDtypeStruct(cw.shape, F32), jax.ShapeDtypeStruct(cb.shape, F32)],
        compiler_params=_cparams(1))(gate, gate, gate, up, up, dact, dact, cw, cb)


def _final_loss(x2, tgt, g, tb=TOK_BLOCK):
    tn, dm = x2.shape

    def body(x_ref, t_ref, g_ref, l_ref, dx_ref, dg_ref):
        y, vjp = jax.vjp(_rms_fn, x_ref[...], g_ref[...])
        err = y - t_ref[...]
        dx, dg = vjp(err * (1.0 / dm))

        @pl.when(pl.program_id(0) == 0)
        def _():
            l_ref[...] = jnp.zeros(l_ref.shape, F32)
            dg_ref[...] = jnp.zeros(dg_ref.shape, F32)

        part = 0.5 * jnp.sum(jnp.mean(err * err, axis=-1, keepdims=True), axis=0, keepdims=True)
        l_ref[...] += jnp.broadcast_to(part, l_ref.shape)
        dx_ref[...] = dx
        dg_ref[...] += dg

    return pl.pallas_call(
        body, name="final_loss", grid=(tn // tb,),
        in_specs=[_blk_spec(x2, tb), _blk_spec(tgt, tb), _full_spec(g)],
        out_specs=[pl.BlockSpec((8, 128), lambda i: (0, 0)), _blk_spec(x2, tb), _full_spec(g)],
        out_shape=[jax.ShapeDtypeStruct((8, 128), F32), jax.ShapeDtypeStruct(x2.shape, F32),
                   jax.ShapeDtypeStruct(g.shape, F32)],
        compiler_params=_cparams(1))(x2, tgt, g)


def _pad_cols(w, n):
    return jnp.pad(w, ((0, 0), (0, n - w.shape[1])))


def _pad_rows(w, n):
    return jnp.pad(w, ((0, n - w.shape[0]), (0, 0)))


def _local_step(x, tgt, W, late):
    tn = x.shape[0]
    Wd = RWKV_WIDTH
    row = lambda z: z.reshape(1, -1)
    g_mix, g_ffn, g_fin = row(W['norm_mix_g']), row(W['norm_ffn_g']), row(W['norm_final_g'])

    (h1,) = _tok_fwd("norm_mix_fwd", lambda a, g: (_rms_fn(a, g),), [x], [g_mix], [(D_MODEL,)])
    proj = _mm("proj_fwd", h1, W['w_in_t'], tb=True)
    p_rkv = proj[:, :3 * Wd]
    pre_consts = [row(W['rwkv_mu_w']), row(W['rwkv_mu_a']), row(W['rwkv_mu_g']), row(W['rwkv_mu_r']),
                  row(W['rwkv_mu_k']), row(W['rwkv_mu_v']), row(W['rwkv_w0']),
                  _pad_cols(W['rwkv_w1'], LORA_PAD), _pad_rows(W['rwkv_w2'], LORA_PAD), row(W['rwkv_a0']),
                  _pad_cols(W['rwkv_a1'], LORA_PAD), _pad_rows(W['rwkv_a2'], LORA_PAD),
                  W['rwkv_g1'], W['rwkv_g2']]
    r, k0, v, lw, a, g = _pre_a_fwd(h1, p_rkv, pre_consts)
    k_k, k_a = row(W['rwkv_k_k']), row(W['rwkv_k_a'])
    nkk, k, b = _tok_fwd("rwkv_pre_b_fwd", _pre_b_fn, [k0, a], [k_k, k_a], [(Wd,)] * 3)
    y_scan, ck, gathered = _cscan_fwd(r, lw, k, v, nkk, b, late)
    w_out, w_gate_t, w_up_t, w_down = [g_.reshape(-1, D_MODEL) for g_ in gathered]
    post_consts = [row(W['rwkv_lnx_w']), row(W['rwkv_lnx_b']), row(W['rwkv_r_k'])]
    (y_rwkv,) = _tok_fwd("rwkv_post_fwd", _rwkv_post_fn, [y_scan, r, k, v, g], post_consts, [(Wd,)])

    pos = jnp.arange(tn, dtype=F32)
    half = RET_HEAD_DIM // 2
    inv_freq = ROPE_BASE ** (-jnp.arange(half, dtype=F32) / half)
    ang = pos[:, None] * inv_freq[None, :]
    cos2 = jnp.concatenate([jnp.cos(ang), jnp.cos(ang)], axis=1)
    sin2 = jnp.concatenate([-jnp.sin(ang), jnp.sin(ang)], axis=1)
    lg = jnp.log(1.0 - 2.0 ** (-5.0 - jnp.arange(RET_HEADS, dtype=F32)))
    lg = jnp.broadcast_to(lg[:, None, None], (RET_HEADS, 1, 128))
    q_p, k_p = proj[:, 3 * Wd:4 * Wd], proj[:, 4 * Wd:5 * Wd]
    v_ret, g_ret = proj[:, 5 * Wd:6 * Wd], proj[:, 6 * Wd:7 * Wd]
    q_rot, k_rot = _tok_fwd("ret_rotary_fwd", _rotary_fn, [cos2, sin2, q_p, k_p], [], [(RET_WIDTH,)] * 2)
    y_ret_raw = _ret_attn_fwd(lg, q_rot, k_rot, v_ret)
    gn_w = row(W['ret_gn_w'])
    (y_ret,) = _tok_fwd("ret_post_fwd", _ret_post_fn, [y_ret_raw, g_ret], [gn_w], [(RET_WIDTH,)])

    ycat = jnp.concatenate([y_rwkv, y_ret], axis=1)
    x1 = _mm("out_proj_fwd", ycat, w_out, add=x)
    (h2,) = _tok_fwd("norm_ffn_fwd", lambda a_, g_: (_rms_fn(a_, g_),), [x1], [g_ffn], [(D_MODEL,)])
    gate = _mm("ffn_gate_fwd", h2, w_gate_t, tb=True)
    up = _mm("ffn_up_fwd", h2, w_up_t, tb=True)
    cw = W['ffn_conv_w']
    cb = row(W['ffn_conv_b'])
    act = _glu_fwd(gate, up, cw, cb)
    x2 = _mm("ffn_down_fwd", act, w_down, add=x1)
    loss8, dx2, dg_fin = _final_loss(x2, tgt, g_fin)

    G = {'norm_final_g': dg_fin}
    dact = _mm("ffn_down_dx", dx2, w_down, tb=True)
    d_down = _mm("ffn_down_dw", act, dx2, ta=True, out_dtype=BF16)
    dgate, dup, dcw, dcb = _glu_bwd(gate, up, dact, cw, cb)
    G['ffn_conv_w'], G['ffn_conv_b'] = dcw, dcb
    dh2 = _mm("ffn_gate_dx", dgate, w_gate_t)
    dh2 = _mm("ffn_up_dx", dup, w_up_t, add=dh2)
    d_gate_t = _mm("ffn_gate_dw", dgate, h2, ta=True, out_dtype=BF16)
    d_up_t = _mm("ffn_up_dw", dup, h2, ta=True, out_dtype=BF16)
    dx1, G['norm_ffn_g'] = _tok_bwd("norm_ffn_bwd", lambda a_, g_: (_rms_fn(a_, g_),), [], [x1], [g_ffn], [dh2], add=dx2)
    dycat = _mm("out_proj_dx", dx1, w_out, tb=True)
    d_out = _mm("out_proj_dw", ycat, dx1, ta=True, out_dtype=BF16)
    late_grads = [z.reshape(N_DEV, -1, D_MODEL) for z in (d_out, d_gate_t, d_up_t, d_down)]
    dy_rwkv, dy_ret = dycat[:, :Wd], dycat[:, Wd:]

    dyr_raw, dg_ret, G['ret_gn_w'] = _tok_bwd("ret_post_bwd", _ret_post_fn, [], [y_ret_raw, g_ret], [gn_w], [dy_ret])
    dq_rot, dk_rot, dv_ret = _ret_attn_bwd(lg, q_rot, k_rot, v_ret, dyr_raw)
    dq_p, dk_p = _tok_bwd("ret_rotary_bwd", _rotary_fn, [cos2, sin2], [q_p, k_p], [], [dq_rot, dk_rot])

    dy_scan, dr1, dk1, dv1, dg, G['rwkv_lnx_w'], G['rwkv_lnx_b'], G['rwkv_r_k'] = _tok_bwd(
        "rwkv_post_bwd", _rwkv_post_fn, [], [y_scan, r, k, v, g], post_consts, [dy_rwkv])
    (dr2, dlw, dk2, dv2, dnkk, db), late_parts = _cscan_bwd(r, lw, k, v, nkk, b, dy_scan, ck, late_grads)
    dk0, da, G['rwkv_k_k'], G['rwkv_k_a'] = _tok_bwd(
        "rwkv_pre_b_bwd", _pre_b_fn, [], [k0, a], [k_k, k_a], [dnkk, (dk1, dk2), db])
    pre_cts = [(dr1, dr2), dk0, (dv1, dv2), dlw, da, dg]
    pre_out = _pre_a_bwd(h1, p_rkv, pre_consts, pre_cts)
    dh1_a, dp_rkv = pre_out[0], pre_out[1]
    (G['rwkv_mu_w'], G['rwkv_mu_a'], G['rwkv_mu_g'], G['rwkv_mu_r'], G['rwkv_mu_k'], G['rwkv_mu_v'], G['rwkv_w0'],
     dw1, dw2, G['rwkv_a0'], da1, da2, G['rwkv_g1'], G['rwkv_g2']) = pre_out[2:]
    G['rwkv_w1'], G['rwkv_w2'] = dw1[:, :64], dw2[:64, :]
    G['rwkv_a1'], G['rwkv_a2'] = da1[:, :64], da2[:64, :]

    dproj = jnp.concatenate([dp_rkv, dq_p, dk_p, dv_ret, dg_ret], axis=1)
    dh1 = _mm("proj_dx", dproj, W['w_in_t'], add=dh1_a)
    G['w_in_t'] = _mm("proj_dw", dproj, h1, ta=True, out_dtype=BF16)
    dx, G['norm_mix_g'] = _tok_bwd("norm_mix_bwd", lambda a_, g_: (_rms_fn(a_, g_),), [], [x], [g_mix], [dh1], add=dx1)
    return loss8[0, 0], dx, G, late_parts


def _adamw(name, parts, w, m, v):
    rows, cols = w.shape
    sub = 8 * 4 // parts.dtype.itemsize
    tb = max(t for t in range(sub, 65, sub) if rows % t == 0) if rows > 64 else rows
    c1 = 1.0 - ADAM_B1 ** ADAM_STEP
    c2 = 1.0 - ADAM_B2 ** ADAM_STEP

    def body(p_ref, w_ref, m_ref, v_ref, g_ref, d_ref, nm_ref, nv_ref):
        g = p_ref[0].astype(F32)
        for d in range(1, N_DEV):
            g = g + p_ref[d].astype(F32)
        mn = ADAM_B1 * m_ref[...] + (1.0 - ADAM_B1) * g
        vn = ADAM_B2 * v_ref[...] + (1.0 - ADAM_B2) * (g * g)
        m_hat = mn / c1
        v_hat = vn / c2
        g_ref[...] = g
        d_ref[...] = -ADAM_LR * (m_hat / (jnp.sqrt(v_hat) + ADAM_EPS) + ADAM_WD * w_ref[...])
        nm_ref[...] = mn
        nv_ref[...] = vn

    spec = pl.BlockSpec((tb, cols), lambda i: (i, 0))
    sh = jax.ShapeDtypeStruct((rows, cols), F32)
    return pl.pallas_call(
        body, name=name, grid=(rows // tb,),
        in_specs=[pl.BlockSpec((N_DEV, tb, cols), lambda i: (0, i, 0)), spec, spec, spec],
        out_specs=[spec] * 4, out_shape=[sh] * 4, compiler_params=_cparams(1))(parts, w, m, v)


def _local_shape(name):
    gs, ax = SHARDED[name]
    ls = list(gs)
    ls[ax] //= N_DEV
    return tuple(ls)


def _seg(flat, seg):
    n = flat.shape[-1]
    pad = _round_up(n, seg) - n
    if pad:
        flat = jnp.pad(flat, [(0, 0)] * (flat.ndim - 1) + [(0, pad)])
    return flat


def _split3(w):
    hi = w.astype(BF16)
    r1 = w - hi.astype(F32)
    mid = r1.astype(BF16)
    lo = (r1 - mid.astype(F32)).astype(BF16)
    return hi, mid, lo


def _pack_small_shards(shards):
    pieces = []
    for name in SMALL_NAMES:
        flat = shards[name].reshape(-1)
        if name == 'ffn_conv_w':
            pieces += [_seg(p, BF16_SEG) for p in _split3(flat)]
        else:
            pieces.append(flat.astype(BF16))
    return jnp.concatenate(pieces).reshape(-1, 128)


def _unpack_small(gathered):
    flat = gathered.reshape(N_DEV, -1)
    out, off = {}, 0
    for name in SMALL_NAMES:
        gs, ax = SHARDED[name]
        ls = _local_shape(name)
        n = int(np.prod(ls))
        if name == 'ffn_conv_w':
            nseg = _round_up(n, BF16_SEG)
            hi, mid, lo = (flat[:, off + j * nseg: off + j * nseg + n].astype(F32) for j in range(3))
            sh = ((hi + mid) + lo).reshape(N_DEV, 3, -1)
            out[name] = jnp.swapaxes(sh, 0, 1).reshape(3, D_FF)
            off += 3 * nseg
        else:
            sh = flat[:, off:off + n].reshape((N_DEV,) + ls[1:])
            out[name] = sh.reshape(gs[1:]) if ax == 1 else jnp.swapaxes(sh, 0, 1).reshape(gs[1:])
            off += n
    return out


def _small_pieces(sharded, repl):
    return [sharded[n].reshape(-1) for n in SMALL_NAMES] + [repl[n].reshape(-1) for n in REPL_NAMES]


def _pack_small_local(d):
    flat = jnp.concatenate(_small_pieces(d, d))
    return _seg(flat, F32_SEG).reshape(-1, 128)


def _pack_small_grads(G):
    pieces = []
    for name in SMALL_NAMES:
        gs, ax = SHARDED[name]
        g = G[name]
        if name == 'ffn_conv_w':
            sh = jnp.swapaxes(g.reshape(3, N_DEV, -1), 0, 1)
        elif ax == 1:
            sh = g
        else:
            sh = jnp.swapaxes(g.reshape(g.shape[0], N_DEV, -1), 0, 1)
        pieces.append(sh.reshape(N_DEV, -1))
    rep = jnp.concatenate([G[n].reshape(-1) for n in REPL_NAMES])
    pieces.append(jnp.broadcast_to(rep[None, :], (N_DEV, rep.shape[0])))
    flat = _seg(jnp.concatenate(pieces, axis=1), F32_SEG)
    return flat.reshape(N_DEV, -1, 128)


def _unpack_small_local(packed, local_shapes):
    flat = packed.reshape(-1)
    out, off = {}, 0
    for name in SMALL_NAMES + REPL_NAMES:
        n = int(np.prod(local_shapes[name]))
        out[name] = flat[off:off + n].reshape(local_shapes[name])
        off += n
    return out


def kernel(x, *rest):
    nw = len(WEIGHT_NAMES)
    assert len(rest) == 3 * nw + 1
    weights = dict(zip(WEIGHT_NAMES, rest[:nw]))
    loss_target = rest[nw]
    moms = dict(zip(WEIGHT_NAMES, rest[nw + 1:2 * nw + 1]))
    vars_ = dict(zip(WEIGHT_NAMES, rest[2 * nw + 1:]))
    local_shapes = {n: weights[n].shape for n in WEIGHT_NAMES}

    def native2d(name, a):
        a2 = a.reshape(a.shape[-2], a.shape[-1])
        return a2.T if name in BIG_T else a2

    def from2d(name, a2):
        return (a2.T if name in BIG_T else a2).reshape(local_shapes[name])

    big_w = {n: native2d(n, weights[n]) for n in BIG_NAMES}
    w_in_t_sh = big_w['w_in'].astype(BF16)
    late = [big_w[n].astype(BF16) for n in LATE_NAMES]
    small_sh = _pack_small_shards({n: weights[n] for n in SMALL_NAMES})
    w_in_g, small_g = _exchange("weights_all_gather", [w_in_t_sh, small_sh], False)
    W = _unpack_small(small_g)
    W['w_in_t'] = w_in_g.reshape(-1, D_MODEL)
    for n in REPL_NAMES:
        W[n] = weights[n][0] if n != 'norm_final_g' else weights[n]

    loss, dx, G, late_parts = _local_step(x[0], loss_target[0], W, late)

    w_in_parts, small_parts = _exchange(
        "grads_all_to_all", [G['w_in_t'].reshape(N_DEV, -1, D_MODEL), _pack_small_grads(G)], True)
    results = {}
    for n, parts in zip(['w_in'] + LATE_NAMES, [w_in_parts] + list(late_parts)):
        res = _adamw("adamw_" + n, parts, big_w[n], native2d(n, moms[n]), native2d(n, vars_[n]))
        results[n] = [from2d(n, r) for r in res]
    small_res = _adamw("adamw_small", small_parts, _pack_small_local(weights), _pack_small_local(moms),
                       _pack_small_local(vars_))
    small_out = [_unpack_small_local(p, local_shapes) for p in small_res]

    loss = lax.psum(loss, ("x", "y", "c"))
    outs = [loss, dx[None]]
    for j in range(4):
        outs += [results[n][j] if n in results else small_out[j][n] for n in WEIGHT_NAMES]
    return tuple(outs)
```

```python
import functools
import math

import numpy as np
import jax
import jax.numpy as jnp
from jax import lax
from jax.experimental import pallas as pl
from jax.experimental.pallas import tpu as pltpu

F32 = jnp.float32
BF16 = jnp.bfloat16

N_DEV = 8
D_MODEL = 1024
RWKV_HEADS = 8
RWKV_HEAD_DIM = 64
RWKV_WIDTH = 512
RET_HEADS = 4
RET_HEAD_DIM = 128
RET_WIDTH = 512
LORA_PAD = 128
D_FF = 2816
NORM_EPS = 1e-6
RWKV_GN_EPS = 64e-5
RET_GN_EPS = 1e-5
ROPE_BASE = 10000.0
ADAM_LR, ADAM_B1, ADAM_B2, ADAM_EPS, ADAM_WD, ADAM_STEP = 0.001, 0.9, 0.999, 1e-08, 0.01, 10

VMEM_LIMIT = 56 * 1024 * 1024
TOK_BLOCK = 256
SCAN_CHUNK = 64
ATT_BLOCK = 512
BF16_SEG = 2048
F32_SEG = 1024

WEIGHT_NAMES = ['norm_mix_g', 'w_in', 'rwkv_mu_r', 'rwkv_mu_k', 'rwkv_mu_v', 'rwkv_mu_w', 'rwkv_mu_a',
                'rwkv_mu_g', 'rwkv_w0', 'rwkv_w1', 'rwkv_w2', 'rwkv_a0', 'rwkv_a1', 'rwkv_a2', 'rwkv_g1',
                'rwkv_g2', 'rwkv_k_k', 'rwkv_k_a', 'rwkv_r_k', 'rwkv_lnx_w', 'rwkv_lnx_b', 'ret_gn_w',
                'w_out', 'norm_ffn_g', 'ffn_w_gate', 'ffn_w_up', 'ffn_conv_w', 'ffn_conv_b', 'ffn_w_down',
                'norm_final_g']
SHARDED = {
    'w_in': ((1, 1024, 3584), 2), 'rwkv_w1': ((1, 1024, 64), 1), 'rwkv_w2': ((1, 64, 512), 2),
    'rwkv_a1': ((1, 1024, 64), 1), 'rwkv_a2': ((1, 64, 512), 2), 'rwkv_g1': ((1, 1024, 128), 1),
    'rwkv_g2': ((1, 128, 512), 2), 'w_out': ((1, 1024, 1024), 1), 'ffn_w_gate': ((1, 1024, 2816), 2),
    'ffn_w_up': ((1, 1024, 2816), 2), 'ffn_conv_w': ((1, 3, 1, 2816), 3), 'ffn_w_down': ((1, 2816, 1024), 1),
}
REPL_NAMES = [n for n in WEIGHT_NAMES if n not in SHARDED]
BIG_NAMES = ['w_in', 'w_out', 'ffn_w_gate', 'ffn_w_up', 'ffn_w_down']
BIG_T = ('w_in', 'ffn_w_gate', 'ffn_w_up')
LATE_NAMES = ['w_out', 'ffn_w_gate', 'ffn_w_up', 'ffn_w_down']
SMALL_NAMES = [n for n in WEIGHT_NAMES if n in SHARDED and n not in BIG_NAMES]


def _cparams(n_grid):
    return pltpu.CompilerParams(dimension_semantics=("arbitrary",) * n_grid, vmem_limit_bytes=VMEM_LIMIT)


def _round_up(n, m):
    return (n + m - 1) // m * m


@jax.custom_vjp
def _bdot(x, w):
    return jnp.dot(x.astype(BF16), w.astype(BF16), preferred_element_type=F32)


def _bdot_fwd(x, w):
    return _bdot(x, w), (x, w)


def _bdot_bwd(res, g):
    x, w = res
    gb = g.astype(BF16)
    dx = lax.dot_general(gb, w.astype(BF16), (((1,), (1,)), ((), ())), preferred_element_type=F32)
    dw = lax.dot_general(x.astype(BF16), gb, (((0,), (0,)), ((), ())), preferred_element_type=F32)
    return dx, dw.astype(w.dtype)


_bdot.defvjp(_bdot_fwd, _bdot_bwd)


@jax.custom_vjp
def _shift_rows(x, prev):
    rolled = pltpu.roll(x, 1, 0)
    row = lax.broadcasted_iota(jnp.int32, x.shape, 0)
    return jnp.where(row == 0, jnp.broadcast_to(prev, x.shape), rolled)


def _shift_rows_fwd(x, prev):
    return _shift_rows(x, prev), None


def _shift_rows_bwd(_, g):
    n = g.shape[0]
    rolled = pltpu.roll(g, n - 1, 0)
    row = lax.broadcasted_iota(jnp.int32, g.shape, 0)
    return jnp.where(row == n - 1, 0.0, rolled), g[0:1, :]


_shift_rows.defvjp(_shift_rows_fwd, _shift_rows_bwd)


@jax.custom_vjp
def _swap_halves(x):
    return pltpu.roll(x, 64, 1)


_swap_halves.defvjp(lambda x: (_swap_halves(x), None), lambda _, g: (pltpu.roll(g, 64, 1),))


def _sigmoid(x):
    return 1.0 / (1.0 + jnp.exp(-x))


def _softplus(x):
    return jnp.maximum(x, 0.0) + jnp.log(1.0 + jnp.exp(-jnp.abs(x)))


def _rms_fn(x, g):
    return x * lax.rsqrt(jnp.mean(x * x, axis=-1, keepdims=True) + NORM_EPS) * g


def _pre_a_fn(h1, h1p, p, pp, mu_w, mu_a, mu_g, mu_r, mu_k, mu_v, w0, w1, w2, a0, a1, a2, g1, g2):
    W = RWKV_WIDTH
    h1s = _shift_rows(h1, h1p)
    ps = _shift_rows(p, pp)
    dx = h1s - h1
    xw = h1 + dx * mu_w
    xa = h1 + dx * mu_a
    xg = h1 + dx * mu_g
    dp = ps - p
    r = p[:, 0:W] + dp[:, 0:W] * mu_r
    k0 = p[:, W:2 * W] + dp[:, W:2 * W] * mu_k
    v = p[:, 2 * W:3 * W] + dp[:, 2 * W:3 * W] * mu_v
    wl = w0 + _bdot(jnp.tanh(_bdot(xw, w1)), w2)
    w_log = -_softplus(-wl) - 0.5
    lw = -jnp.exp(w_log)
    a = _sigmoid(a0 + _bdot(_bdot(xa, a1), a2))
    g = _bdot(_sigmoid(_bdot(xg, g1)), g2)
    return r, k0, v, lw, a, g


def _head_sum_raw(x):
    n = x.shape[1]
    ii = lax.broadcasted_iota(jnp.int32, (n, n), 0) // RWKV_HEAD_DIM
    jj = lax.broadcasted_iota(jnp.int32, (n, n), 1) // RWKV_HEAD_DIM
    ones = (ii == jj).astype(BF16)
    xh = x.astype(BF16)
    xl = (x - xh.astype(F32)).astype(BF16)
    return jnp.dot(xh, ones, preferred_element_type=F32) + jnp.dot(xl, ones, preferred_element_type=F32)


@jax.custom_vjp
def _head_sum(x):
    return _head_sum_raw(x)


_head_sum.defvjp(lambda x: (_head_sum_raw(x), None), lambda _, g: (_head_sum_raw(g),))


def _pre_b_fn(k0, a, k_k, k_a):
    kkr = k0 * k_k
    nrm = jnp.sqrt(_head_sum(kkr * kkr))
    kk = kkr / jnp.maximum(nrm, 1e-12)
    k = k0 * (1.0 + (a - 1.0) * k_a)
    return -kk, k, kk * a


def _rwkv_post_fn(y, r, k, v, g, lnx_w, lnx_b, r_k):
    inv = 1.0 / RWKV_HEAD_DIM
    mu = _head_sum(y) * inv
    yc = y - mu
    var = _head_sum(yc * yc) * inv
    yn = yc * lax.rsqrt(var + RWKV_GN_EPS) * lnx_w + lnx_b
    bonus = _head_sum(r * k * r_k) * v
    return ((yn + bonus) * g,)


def _rotary_fn(cos2, sin2, q, k):
    qs, ks = [], []
    for h in range(RET_HEADS):
        sl = slice(h * RET_HEAD_DIM, (h + 1) * RET_HEAD_DIM)
        qh, kh = q[:, sl], k[:, sl]
        qs.append(qh * cos2 + _swap_halves(qh) * sin2)
        ks.append((kh * cos2 + _swap_halves(kh) * sin2) * (RET_HEAD_DIM ** -0.5))
    return jnp.concatenate(qs, axis=1), jnp.concatenate(ks, axis=1)


def _ret_post_fn(y, gp, gn_w):
    outs = []
    for h in range(RET_HEADS):
        sl = slice(h * RET_HEAD_DIM, (h + 1) * RET_HEAD_DIM)
        yh = y[:, sl]
        mu = jnp.mean(yh, axis=-1, keepdims=True)
        yc = yh - mu
        var = jnp.mean(yc * yc, axis=-1, keepdims=True)
        outs.append(yc * lax.rsqrt(var + RET_GN_EPS) * gn_w[:, sl])
    yn = jnp.concatenate(outs, axis=1)
    return (gp * _sigmoid(gp) * yn,)


def _tok_axis(a):
    return 1 if a.ndim == 3 else 0


def _blk_spec(a, tb, rev_nb=None):
    nd, ax = a.ndim, _tok_axis(a)
    shape = a.shape[:ax] + (tb,) + a.shape[ax + 1:]

    def imap(i):
        idx = [0] * nd
        idx[ax] = i if rev_nb is None else rev_nb - 1 - i
        return tuple(idx)

    return pl.BlockSpec(shape, imap)


def _full_spec(a):
    nd = a.ndim
    return pl.BlockSpec(a.shape, lambda i: (0,) * nd)


def _tok_fwd(name, fn, toks, consts, out_tails, tb=TOK_BLOCK):
    n_in = len(toks) + len(consts)
    tn = toks[0].shape[_tok_axis(toks[0])]

    def body(*refs):
        outs = fn(*[r[...] for r in refs[:n_in]])
        for r, o in zip(refs[n_in:], outs):
            r[...] = o

    out_shape = [jax.ShapeDtypeStruct((tn,) + tuple(s) if len(s) == 1 else (s[0], tn, s[1]), F32) for s in out_tails]
    return pl.pallas_call(
        body, name=name, grid=(tn // tb,),
        in_specs=[_blk_spec(a, tb) for a in toks] + [_full_spec(c) for c in consts],
        out_specs=[_blk_spec(o, tb) for o in out_shape], out_shape=out_shape,
        compiler_params=_cparams(1))(*toks, *consts)


def _tok_bwd(name, fn, aux, toks, consts, cts, add=None, tb=TOK_BLOCK):
    n_aux, n_tok, n_c = len(aux), len(toks), len(consts)
    ct_groups = [c if isinstance(c, (tuple, list)) else (c,) for c in cts]
    ct_flat = [a for grp in ct_groups for a in grp]
    n_ct = len(ct_flat)
    n_add = 0 if add is None else 1
    tn = toks[0].shape[_tok_axis(toks[0])]

    def body(*refs):
        pos = 0
        aux_v = [r[...] for r in refs[pos:pos + n_aux]]; pos += n_aux
        tok_v = [r[...] for r in refs[pos:pos + n_tok]]; pos += n_tok
        const_v = [r[...] for r in refs[pos:pos + n_c]]; pos += n_c
        ct_refs = refs[pos:pos + n_ct]; pos += n_ct
        add_refs = refs[pos:pos + n_add]; pos += n_add
        dtok_refs = refs[pos:pos + n_tok]; pos += n_tok
        dconst_refs = refs[pos:pos + n_c]
        ct_v, q = [], 0
        for grp in ct_groups:
            s = ct_refs[q][...]
            for r in ct_refs[q + 1:q + len(grp)]:
                s = s + r[...]
            q += len(grp)
            ct_v.append(s)
        _, vjp = jax.vjp(lambda *tc: fn(*aux_v, *tc), *tok_v, *const_v)
        grads = vjp(tuple(ct_v))
        for j, r in enumerate(dtok_refs):
            gj = grads[j]
            if j == 0 and n_add:
                gj = gj + add_refs[0][...]
            r[...] = gj

        @pl.when(pl.program_id(0) == 0)
        def _():
            for r in dconst_refs:
                r[...] = jnp.zeros(r.shape, F32)

        for j, r in enumerate(dconst_refs):
            r[...] += grads[n_tok + j]

    ins = list(aux) + list(toks) + list(consts) + ct_flat + ([add] if n_add else [])
    in_specs = ([_blk_spec(a, tb) for a in aux] + [_blk_spec(a, tb) for a in toks] + [_full_spec(c) for c in consts]
                + [_blk_spec(a, tb) for a in ct_flat] + ([_blk_spec(add, tb)] if n_add else []))
    out_shape = [jax.ShapeDtypeStruct(a.shape, F32) for a in toks] + [jax.ShapeDtypeStruct(c.shape, F32) for c in consts]
    out_specs = [_blk_spec(a, tb) for a in toks] + [_full_spec(c) for c in consts]
    return pl.pallas_call(body, name=name, grid=(tn // tb,), in_specs=in_specs, out_specs=out_specs,
                          out_shape=out_shape, compiler_params=_cparams(1))(*ins)


MM_VMEM_BUDGET = 40 * 1024 * 1024
MM_STEP_SECONDS = 0.4e-6
MM_HBM_BYTES_PER_SECOND = 2.5e12


def _mm_tiles(m, n, kd, a_bytes, b_bytes, o_bytes, has_add):
    divs = lambda d: [t for t in range(128, d + 1, 128) if d % t == 0]
    best = None
    for tm in divs(m):
        for tn in divs(n):
            for tk in divs(kd):
                ni, nj, nk = m // tm, n // tn, kd // tk
                vmem = (2 * tm * tk * a_bytes + 2 * tk * tn * b_bytes + tm * tn * 4 + 2 * tm * tn * o_bytes
                        + (2 * tm * tn * 4 if has_add else 0) + 2 * (tm * tk + tk * tn) + tm * tn * 4)
                if vmem > MM_VMEM_BUDGET:
                    continue
                a_traffic = m * kd * a_bytes * (nj if nk > 1 else 1)
                b_traffic = kd * n * b_bytes * (ni if nj * nk > 1 else 1)
                cost = ni * nj * nk * MM_STEP_SECONDS + (a_traffic + b_traffic) / MM_HBM_BYTES_PER_SECOND
                if best is None or cost < best[0]:
                    best = (cost, tm, tn, tk)
    return best[1:]


def _mm(name, a, b, ta=False, tb=False, add=None, out_dtype=F32):
    if ta:
        kd, m = a.shape
    else:
        m, kd = a.shape
    if tb:
        n, kb = b.shape
    else:
        kb, n = b.shape
    assert kd == kb, (a.shape, b.shape)
    tm, tn, tk = _mm_tiles(m, n, kd, a.dtype.itemsize, b.dtype.itemsize, jnp.dtype(out_dtype).itemsize,
                           add is not None)
    nk = kd // tk
    has_add = add is not None
    dims = (((0 if ta else 1,), (1 if tb else 0,)), ((), ()))

    def body(*refs):
        a_ref, b_ref = refs[0], refs[1]
        o_ref, acc_ref = refs[-2], refs[-1]
        k = pl.program_id(2)

        @pl.when(k == 0)
        def _():
            acc_ref[...] = refs[2][...] if has_add else jnp.zeros(acc_ref.shape, F32)

        acc_ref[...] += lax.dot_general(a_ref[...].astype(BF16), b_ref[...].astype(BF16), dims,
                                        preferred_element_type=F32)

        @pl.when(k == nk - 1)
        def _():
            o_ref[...] = acc_ref[...].astype(out_dtype)

    a_spec = pl.BlockSpec((tk, tm), lambda i, j, k: (k, i)) if ta else pl.BlockSpec((tm, tk), lambda i, j, k: (i, k))
    b_spec = pl.BlockSpec((tn, tk), lambda i, j, k: (j, k)) if tb else pl.BlockSpec((tk, tn), lambda i, j, k: (k, j))
    o_spec = pl.BlockSpec((tm, tn), lambda i, j, k: (i, j))
    ins = [a, b] + ([add] if has_add else [])
    in_specs = [a_spec, b_spec] + ([o_spec] if has_add else [])
    return pl.pallas_call(body, name=name, grid=(m // tm, n // tn, nk), in_specs=in_specs, out_specs=o_spec,
                          out_shape=jax.ShapeDtypeStruct((m, n), out_dtype),
                          scratch_shapes=[pltpu.VMEM((tm, tn), F32)], compiler_params=_cparams(3))(*ins)


def _prev8_spec(a, tb, rev_nb=None):
    r = tb // 8
    if rev_nb is None:
        return pl.BlockSpec((8, a.shape[1]), lambda i: (jnp.maximum(i * r - 1, 0), 0))
    return pl.BlockSpec((8, a.shape[1]), lambda i: (jnp.maximum((rev_nb - 1 - i) * r - 1, 0), 0))


def _pre_a_fwd(h1, p, consts, tb=TOK_BLOCK):
    tn = h1.shape[0]

    def body(h1_ref, h1h_ref, p_ref, ph_ref, *rest):
        c_refs, o_refs = rest[:len(consts)], rest[len(consts):]
        first = pl.program_id(0) == 0
        h1p = jnp.where(first, 0.0, h1h_ref[7:8, :])
        pp = jnp.where(first, 0.0, ph_ref[7:8, :])
        outs = _pre_a_fn(h1_ref[...], h1p, p_ref[...], pp, *[c[...] for c in c_refs])
        for r, o in zip(o_refs, outs):
            r[...] = o

    out_shape = [jax.ShapeDtypeStruct((tn, RWKV_WIDTH), F32) for _ in range(6)]
    return pl.pallas_call(
        body, name="rwkv_pre_a_fwd", grid=(tn // tb,),
        in_specs=[_blk_spec(h1, tb), _prev8_spec(h1, tb), _blk_spec(p, tb), _prev8_spec(p, tb)]
        + [_full_spec(c) for c in consts],
        out_specs=[_blk_spec(o, tb) for o in out_shape], out_shape=out_shape,
        compiler_params=_cparams(1))(h1, h1, p, p, *consts)


def _pre_a_bwd(h1, p, consts, cts, tb=TOK_BLOCK):
    tn = h1.shape[0]
    nb = tn // tb
    n_c = len(consts)
    ct_groups = [c if isinstance(c, (tuple, list)) else (c,) for c in cts]
    ct_flat = [a for grp in ct_groups for a in grp]
    n_ct = len(ct_flat)

    def body(*refs):
        h1_ref, h1h_ref, p_ref, ph_ref = refs[:4]
        c_refs = refs[4:4 + n_c]
        ct_refs = refs[4 + n_c:4 + n_c + n_ct]
        dh1_ref, dp_ref = refs[4 + n_c + n_ct:6 + n_c + n_ct]
        dc_refs = refs[6 + n_c + n_ct:6 + 2 * n_c + n_ct]
        ch_ref, cp_ref = refs[-2], refs[-1]
        i = pl.program_id(0)
        first_block = i == nb - 1
        h1p = jnp.where(first_block, 0.0, h1h_ref[7:8, :])
        pp = jnp.where(first_block, 0.0, ph_ref[7:8, :])
        ct_v, q = [], 0
        for grp in ct_groups:
            s = ct_refs[q][...]
            for r in ct_refs[q + 1:q + len(grp)]:
                s = s + r[...]
            q += len(grp)
            ct_v.append(s)
        _, vjp = jax.vjp(_pre_a_fn, h1_ref[...], h1p, p_ref[...], pp, *[c[...] for c in c_refs])
        grads = vjp(tuple(ct_v))

        @pl.when(i == 0)
        def _():
            ch_ref[...] = jnp.zeros(ch_ref.shape, F32)
            cp_ref[...] = jnp.zeros(cp_ref.shape, F32)
            for r in dc_refs:
                r[...] = jnp.zeros(r.shape, F32)

        rowh = lax.broadcasted_iota(jnp.int32, (tb, h1.shape[1]), 0)
        rowp = lax.broadcasted_iota(jnp.int32, (tb, p.shape[1]), 0)
        dh1_ref[...] = grads[0] + jnp.where(rowh == tb - 1, jnp.broadcast_to(ch_ref[0:1, :], rowh.shape), 0.0)
        dp_ref[...] = grads[2] + jnp.where(rowp == tb - 1, jnp.broadcast_to(cp_ref[0:1, :], rowp.shape), 0.0)
        ch_ref[0:1, :] = grads[1]
        cp_ref[0:1, :] = grads[3]
        for j, r in enumerate(dc_refs):
            r[...] += grads[4 + j]

    ins = [h1, h1, p, p] + list(consts) + ct_flat
    in_specs = ([_blk_spec(h1, tb, nb), _prev8_spec(h1, tb, nb), _blk_spec(p, tb, nb), _prev8_spec(p, tb, nb)]
                + [_full_spec(c) for c in consts] + [_blk_spec(a, tb, nb) for a in ct_flat])
    out_shape = ([jax.ShapeDtypeStruct(h1.shape, F32), jax.ShapeDtypeStruct(p.shape, F32)]
                 + [jax.ShapeDtypeStruct(c.shape, F32) for c in consts])
    out_specs = [_blk_spec(h1, tb, nb), _blk_spec(p, tb, nb)] + [_full_spec(c) for c in consts]
    return pl.pallas_call(body, name="rwkv_pre_a_bwd", grid=(nb,), in_specs=in_specs, out_specs=out_specs,
                          out_shape=out_shape,
                          scratch_shapes=[pltpu.VMEM((8, h1.shape[1]), F32), pltpu.VMEM((8, p.shape[1]), F32)],
                          compiler_params=_cparams(1))(*ins)


def _my_index():
    return 4 * lax.axis_index("x") + 2 * lax.axis_index("y") + lax.axis_index("c")


def _peer(k):
    x, y, c = lax.axis_index("x"), lax.axis_index("y"), lax.axis_index("c")
    px = 1 - x if k & 4 else x
    py = 1 - y if k & 2 else y
    pc = 1 - c if k & 1 else c
    return (px, py, pc), 4 * px + 2 * py + pc


def _xchg_sems(n):
    return [pltpu.SemaphoreType.DMA((n * (N_DEV - 1),)), pltpu.SemaphoreType.DMA((n * (N_DEV - 1),)),
            pltpu.SemaphoreType.DMA((n,))]


def _scatter_copies(srcs, dsts, sems, incoming=False):
    send_sems, recv_sems, local_sems = sems
    me = _my_index()
    local, remote = [], []
    for i, (s, d) in enumerate(zip(srcs, dsts)):
        if not incoming:
            local.append(pltpu.make_async_copy(s.at[me], d.at[me], local_sems.at[i]))
        for k in range(1, N_DEV):
            peer, plin = _peer(k)
            j = i * (N_DEV - 1) + k - 1
            s_slot, d_slot = (me, plin) if incoming else (plin, me)
            remote.append(pltpu.make_async_remote_copy(
                src_ref=s.at[s_slot], dst_ref=d.at[d_slot], send_sem=send_sems.at[j],
                recv_sem=recv_sems.at[j], device_id=peer, device_id_type=pl.DeviceIdType.MESH))
    return local, remote


def _scatter_start(srcs, dsts, sems):
    local, out = _scatter_copies(srcs, dsts, sems)
    for cp in local + out:
        cp.start()


def _scatter_wait(srcs, dsts, sems):
    for cp in _scatter_copies(srcs, dsts, sems, incoming=True)[1]:
        cp.wait_recv()
    local, out = _scatter_copies(srcs, dsts, sems)
    for cp in out:
        cp.wait_send()
    for cp in local:
        cp.wait()


_ICI_PEERS = (2, 4, 6)


def _gather_copies(srcs, dsts, sems, group):
    send_sems, recv_sems, local_sems = sems
    me = _my_index()
    sib, sib_lin = _peer(1)
    out = []
    for i, (s, d) in enumerate(zip(srcs, dsts)):
        def mk(q, src, dst, dev):
            j = i * (N_DEV - 1) + q
            return pltpu.make_async_remote_copy(src_ref=src, dst_ref=dst, send_sem=send_sems.at[j],
                                                recv_sem=recv_sems.at[j], device_id=dev,
                                                device_id_type=pl.DeviceIdType.MESH)
        if group == 'local':
            out.append(pltpu.make_async_copy(s, d.at[me], local_sems.at[i]))
        elif group == 'own':
            out.append(mk(0, s, d.at[me], sib))
        elif group == 'in_d2d':
            out.append(mk(0, s, d.at[sib_lin], sib))
        for jj, k in enumerate(_ICI_PEERS):
            peer, plin = _peer(k)
            plin_other = _peer(k + 1)[1]
            if group == 'own':
                out.append(mk(1 + jj, s, d.at[me], peer))
            elif group == 'in_ici':
                out.append(mk(1 + jj, s, d.at[plin], peer))
            elif group == 'pass_on':
                out.append(mk(4 + jj, d.at[plin], d.at[plin], sib))
            elif group == 'in_d2d':
                out.append(mk(4 + jj, d.at[plin_other], d.at[plin_other], sib))
    return out


def _gather_start(srcs, dsts, sems):
    for cp in _gather_copies(srcs, dsts, sems, 'local') + _gather_copies(srcs, dsts, sems, 'own'):
        cp.start()


def _gather_pass_on(srcs, dsts, sems):
    for cp in _gather_copies(srcs, dsts, sems, 'in_ici'):
        cp.wait_recv()
    for cp in _gather_copies(srcs, dsts, sems, 'pass_on'):
        cp.start()


def _gather_finish(srcs, dsts, sems):
    for cp in _gather_copies(srcs, dsts, sems, 'in_d2d'):
        cp.wait_recv()
    for cp in _gather_copies(srcs, dsts, sems, 'own') + _gather_copies(srcs, dsts, sems, 'pass_on'):
        cp.wait_send()
    for cp in _gather_copies(srcs, dsts, sems, 'local'):
        cp.wait()


def _xchg_out_shapes(srcs, scatter):
    return [jax.ShapeDtypeStruct(s.shape if scatter else (N_DEV,) + s.shape, s.dtype) for s in srcs]


_ANY = pl.BlockSpec(memory_space=pl.ANY)


def _exchange(name, srcs, scatter):
    n = len(srcs)

    def body(*refs):
        s, d, sems = refs[:n], refs[n:2 * n], refs[2 * n:]
        if scatter:
            _scatter_start(s, d, sems)
            _scatter_wait(s, d, sems)
        else:
            _gather_start(s, d, sems)
            _gather_pass_on(s, d, sems)
            _gather_finish(s, d, sems)

    return pl.pallas_call(body, name=name, in_specs=[_ANY] * n, out_specs=[_ANY] * n,
                          out_shape=_xchg_out_shapes(srcs, scatter), scratch_shapes=_xchg_sems(n))(*srcs)


_MM_DIMS = {'nn': (((1,), (0,)), ((), ())), 'nt': (((1,), (1,)), ((), ())), 'tn': (((0,), (0,)), ((), ()))}


def _cmm_raw(x, y, kind, split):
    dot = functools.partial(lax.dot_general, dimension_numbers=_MM_DIMS[kind], preferred_element_type=F32)
    xh, yh = x.astype(BF16), y.astype(BF16)
    out = dot(xh, yh)
    if split:
        xl = (x - xh.astype(F32)).astype(BF16)
        yl = (y - yh.astype(F32)).astype(BF16)
        out = out + (dot(xh, yl) + dot(xl, yh))
    return out


@functools.partial(jax.custom_vjp, nondiff_argnums=(2, 3))
def _cmm(x, y, kind, split=False):
    return _cmm_raw(x, y, kind, split)


def _cmm_fwd(x, y, kind, split):
    return _cmm_raw(x, y, kind, split), (x, y)


def _cmm_bwd(kind, split, res, g):
    x, y = res
    if kind == 'nn':
        return _cmm_raw(g, y, 'nt', split), _cmm_raw(x, g, 'tn', split)
    if kind == 'nt':
        return _cmm_raw(g, y, 'nn', split), _cmm_raw(g, x, 'tn', split)
    return _cmm_raw(y, g, 'nt', split), _cmm_raw(x, g, 'nn', split)


_cmm.defvjp(_cmm_fwd, _cmm_bwd)


def _tri_sum_raw(tri, x, kind):
    dot = functools.partial(lax.dot_general, dimension_numbers=_MM_DIMS[kind], preferred_element_type=F32)
    tb = tri.astype(BF16)
    hi, mid, lo = _split3(x)
    return (dot(tb, hi) + dot(tb, mid)) + dot(tb, lo)


@functools.partial(jax.custom_vjp, nondiff_argnums=(2,))
def _tri_sum(tri, x, kind):
    return _tri_sum_raw(tri, x, kind)


def _tri_sum_fwd(tri, x, kind):
    return _tri_sum_raw(tri, x, kind), tri


def _tri_sum_bwd(kind, tri, g):
    return jnp.zeros_like(tri), _tri_sum_raw(tri, g, 'tn' if kind == 'nn' else 'nn')


_tri_sum.defvjp(_tri_sum_fwd, _tri_sum_bwd)


def _chunk_fn(S0, r, lw, k, v, a, b):
    hs = range(len(r))
    C = r[0].shape[0]
    ii = lax.broadcasted_iota(jnp.int32, (C, C), 0)
    jj = lax.broadcasted_iota(jnp.int32, (C, C), 1)
    incl, strict = ii >= jj, ii > jj
    eye = (ii == jj).astype(F32)
    inclf = incl.astype(F32)
    cum = [_tri_sum(inclf, lw[h], 'nn') for h in hs]
    e_inv = [jnp.exp(-cum[h]) for h in hs]
    At = [a[h] * jnp.exp(cum[h] - lw[h]) for h in hs]
    Rt = [r[h] * jnp.exp(cum[h]) for h in hs]
    Kh = [k[h] * e_inv[h] for h in hs]
    Bh = [b[h] * e_inv[h] for h in hs]
    Mab = [jnp.where(strict, _cmm(At[h], Bh[h], 'nt'), 0.0) for h in hs]
    Mak = [jnp.where(strict, _cmm(At[h], Kh[h], 'nt'), 0.0) for h in hs]
    Mrk = [jnp.where(incl, _cmm(Rt[h], Kh[h], 'nt'), 0.0) for h in hs]
    Mrb = [jnp.where(incl, _cmm(Rt[h], Bh[h], 'nt'), 0.0) for h in hs]
    rhs = [_cmm(At[h], S0[h], 'nt') + _cmm(Mak[h], v[h], 'nn') for h in hs]
    P = Mab
    Tm = [eye + P[h] for h in hs]
    n = 1
    while 2 * n < C:
        P = [_cmm(P[h], P[h], 'nn', True) for h in hs]
        Tm = [_cmm(Tm[h], eye + P[h], 'nn', True) for h in hs]
        n *= 2
    U = [_cmm(Tm[h], rhs[h], 'nn', True) for h in hs]
    Y = [_cmm(Rt[h], S0[h], 'nt') + _cmm(Mrk[h], v[h], 'nn') + _cmm(Mrb[h], U[h], 'nn') for h in hs]
    gC = [jnp.exp(jnp.sum(lw[h], axis=0, keepdims=True)) for h in hs]
    SC = [S0[h] * gC[h] + _cmm(v[h], Kh[h] * gC[h], 'tn') + _cmm(U[h], Bh[h] * gC[h], 'tn') for h in hs]
    return tuple(Y), tuple(SC)


def _cscan_fwd(r, lw, k, v, a, b, xs):
    n_x = len(xs)
    tn = r.shape[0]
    H, Dh = RWKV_HEADS, RWKV_HEAD_DIM
    nc = tn // SCAN_CHUNK
    lanes = lambda h: slice(h * Dh, (h + 1) * Dh)
    heads = lambda ref: tuple(ref[:, lanes(h)] for h in range(H))
    mats = lambda ref: tuple(ref[h] for h in range(H))

    def body(r_ref, lw_ref, k_ref, v_ref, a_ref, b_ref, *rest):
        x_src, (y_ref, ck_ref) = rest[:n_x], rest[n_x:n_x + 2]
        x_dst, s_ref, sems = rest[n_x + 2:2 * n_x + 2], rest[2 * n_x + 2], rest[2 * n_x + 3:]

        @pl.when(pl.program_id(0) == 0)
        def _():
            s_ref[...] = jnp.zeros(s_ref.shape, F32)
            _gather_start(x_src, x_dst, sems)

        ck_ref[0] = s_ref[...]
        y, sc = _chunk_fn(mats(s_ref), heads(r_ref), heads(lw_ref), heads(k_ref), heads(v_ref), heads(a_ref),
                          heads(b_ref))
        for h in range(H):
            y_ref[:, lanes(h)] = y[h]
            s_ref[h] = sc[h]

        @pl.when(pl.program_id(0) == max(nc - 4, 0))
        def _():
            _gather_pass_on(x_src, x_dst, sems)

        @pl.when(pl.program_id(0) == nc - 1)
        def _():
            _gather_finish(x_src, x_dst, sems)

    hm = pl.BlockSpec((SCAN_CHUNK, H * Dh), lambda c: (c, 0))
    res = pl.pallas_call(
        body, name="rwkv_scan_fwd", grid=(nc,), in_specs=[hm] * 6 + [_ANY] * n_x,
        out_specs=[hm, pl.BlockSpec((1, H, Dh, Dh), lambda c: (c, 0, 0, 0))] + [_ANY] * n_x,
        out_shape=[jax.ShapeDtypeStruct((tn, H * Dh), F32), jax.ShapeDtypeStruct((nc, H, Dh, Dh), F32)]
        + _xchg_out_shapes(xs, False),
        scratch_shapes=[pltpu.VMEM((H, Dh, Dh), F32)] + _xchg_sems(n_x),
        compiler_params=_cparams(1))(r, lw, k, v, a, b, *xs)
    return res[0], res[1], res[2:]


def _cscan_bwd(r, lw, k, v, a, b, dy, ck, xs):
    n_x = len(xs)
    tn = r.shape[0]
    H, Dh = RWKV_HEADS, RWKV_HEAD_DIM
    nc = tn // SCAN_CHUNK
    lanes = lambda h: slice(h * Dh, (h + 1) * Dh)
    heads = lambda ref: tuple(ref[:, lanes(h)] for h in range(H))
    mats = lambda ref: tuple(ref[h] for h in range(H))

    def body(r_ref, lw_ref, k_ref, v_ref, a_ref, b_ref, dy_ref, ck_ref, *rest):
        x_src = rest[:n_x]
        d_refs = rest[n_x:n_x + 6]
        x_dst = rest[n_x + 6:2 * n_x + 6]
        g_ref = rest[2 * n_x + 6]
        sems = rest[2 * n_x + 7:]

        @pl.when(pl.program_id(0) == 0)
        def _():
            g_ref[...] = jnp.zeros(g_ref.shape, F32)
            _scatter_start(x_src, x_dst, sems)

        s0 = tuple(ck_ref[0, h] for h in range(H))
        _, vjp = jax.vjp(_chunk_fn, s0, heads(r_ref), heads(lw_ref), heads(k_ref), heads(v_ref), heads(a_ref),
                         heads(b_ref))
        grads = vjp((heads(dy_ref), mats(g_ref)))
        for h in range(H):
            g_ref[h] = grads[0][h]
            for d_ref, gz in zip(d_refs, grads[1:]):
                d_ref[:, lanes(h)] = gz[h]

        @pl.when(pl.program_id(0) == nc - 1)
        def _():
            _scatter_wait(x_src, x_dst, sems)

    hm = pl.BlockSpec((SCAN_CHUNK, H * Dh), lambda c: (nc - 1 - c, 0))
    hshape = jax.ShapeDtypeStruct((tn, H * Dh), F32)
    res = pl.pallas_call(
        body, name="rwkv_scan_bwd", grid=(nc,),
        in_specs=[hm] * 7 + [pl.BlockSpec((1, H, Dh, Dh), lambda c: (nc - 1 - c, 0, 0, 0))] + [_ANY] * n_x,
        out_specs=[hm] * 6 + [_ANY] * n_x, out_shape=[hshape] * 6 + _xchg_out_shapes(xs, True),
        scratch_shapes=[pltpu.VMEM((H, Dh, Dh), F32)] + _xchg_sems(n_x),
        compiler_params=_cparams(1))(r, lw, k, v, a, b, dy, ck, *xs)
    return res[:6], res[6:]


def _decay_mask(lg, i, j, blk):
    rows = lax.broadcasted_iota(jnp.int32, (blk, blk), 0)
    cols = lax.broadcasted_iota(jnp.int32, (blk, blk), 1)
    dd = (rows - cols + (i - j) * blk).astype(F32)
    return jnp.where(dd >= 0.0, jnp.exp(lg * jnp.maximum(dd, 0.0)), 0.0)


_NT = (((1,), (1,)), ((), ()))
_TN = (((0,), (0,)), ((), ()))


def _ret_attn_fwd(lg, q, k, v, blk=ATT_BLOCK):
    tn = q.shape[0]
    Dh = RET_HEAD_DIM

    def body(lg_ref, q_ref, k_ref, v_ref, o_ref):
        i = pl.program_id(1)
        lgv = lg_ref[0][:, 0:1]
        qb = q_ref[...].astype(BF16)

        def jb(j, acc):
            ks = pl.ds(pl.multiple_of(j * blk, blk), blk)
            s = lax.dot_general(qb, k_ref[ks, :].astype(BF16), _NT, preferred_element_type=F32)
            s = s * _decay_mask(lgv, i, j, blk)
            return acc + jnp.dot(s.astype(BF16), v_ref[ks, :].astype(BF16), preferred_element_type=F32)

        o_ref[...] = lax.fori_loop(0, i + 1, jb, jnp.zeros((blk, Dh), F32))

    full = pl.BlockSpec((tn, Dh), lambda h, i: (0, h))
    qs = pl.BlockSpec((blk, Dh), lambda h, i: (i, h))
    return pl.pallas_call(
        body, name="ret_attn_fwd", grid=(RET_HEADS, tn // blk),
        in_specs=[pl.BlockSpec((1, 1, 128), lambda h, i: (h, 0, 0)), qs, full, full],
        out_specs=qs, out_shape=jax.ShapeDtypeStruct(q.shape, F32), compiler_params=_cparams(2))(lg, q, k, v)


def _ret_attn_bwd(lg, q, k, v, do, blk=ATT_BLOCK):
    tn = q.shape[0]
    nb = tn // blk
    Dh = RET_HEAD_DIM

    def body(lg_ref, q_ref, k_ref, v_ref, do_ref, dq_ref, dk_ref, dv_ref):
        lgv = lg_ref[0][:, 0:1]
        dk_ref[...] = jnp.zeros(dk_ref.shape, F32)
        dv_ref[...] = jnp.zeros(dv_ref.shape, F32)

        def ib(i, carry):
            qs = pl.ds(pl.multiple_of(i * blk, blk), blk)
            qb = q_ref[qs, :].astype(BF16)
            dob = do_ref[qs, :].astype(BF16)

            def jb(j, dq):
                ks = pl.ds(pl.multiple_of(j * blk, blk), blk)
                kb = k_ref[ks, :].astype(BF16)
                vb = v_ref[ks, :].astype(BF16)
                dm = _decay_mask(lgv, i, j, blk)
                s = lax.dot_general(qb, kb, _NT, preferred_element_type=F32) * dm
                ds = lax.dot_general(dob, vb, _NT, preferred_element_type=F32) * dm
                sb, dsb = s.astype(BF16), ds.astype(BF16)
                dv_ref[ks, :] += lax.dot_general(sb, dob, _TN, preferred_element_type=F32)
                dk_ref[ks, :] += lax.dot_general(dsb, qb, _TN, preferred_element_type=F32)
                return dq + jnp.dot(dsb, kb, preferred_element_type=F32)

            dq_ref[qs, :] = lax.fori_loop(0, i + 1, jb, jnp.zeros((blk, Dh), F32))
            return carry

        lax.fori_loop(0, nb, ib, 0)

    full = pl.BlockSpec((tn, Dh), lambda h: (0, h))
    sh = jax.ShapeDtypeStruct(q.shape, F32)
    return pl.pallas_call(
        body, name="ret_attn_bwd", grid=(RET_HEADS,),
        in_specs=[pl.BlockSpec((1, 1, 128), lambda h: (h, 0, 0)), full, full, full, full],
        out_specs=[full, full, full], out_shape=[sh, sh, sh], compiler_params=_cparams(1))(lg, q, k, v, do)


def _next8_spec(a, tb):
    r = tb // 8
    last = a.shape[0] // 8 - 1
    return pl.BlockSpec((8, a.shape[1]), lambda i: (jnp.minimum((i + 1) * r, last), 0))


def _conv_taps(g_ext, cw_ref, cb_ref):
    return (cw_ref[2:3, :] * g_ext + cw_ref[1:2, :] * pltpu.roll(g_ext, 1, 0)
            + cw_ref[0:1, :] * pltpu.roll(g_ext, 2, 0) + cb_ref[...])


def _glu_fwd(gate, up, cw, cb, tb=TOK_BLOCK):
    tn = gate.shape[0]

    def body(g_ref, gh_ref, u_ref, cw_ref, cb_ref, o_ref):
        halo = jnp.where(pl.program_id(0) == 0, 0.0, gh_ref[...])
        g_ext = jnp.concatenate([halo, g_ref[...]], axis=0)
        gc = _conv_taps(g_ext, cw_ref, cb_ref)[8:, :]
        o_ref[...] = gc * _sigmoid(gc) * u_ref[...]

    return pl.pallas_call(
        body, name="glu_fwd", grid=(tn // tb,),
        in_specs=[_blk_spec(gate, tb), _prev8_spec(gate, tb), _blk_spec(up, tb), _full_spec(cw), _full_spec(cb)],
        out_specs=_blk_spec(gate, tb), out_shape=jax.ShapeDtypeStruct(gate.shape, F32),
        compiler_params=_cparams(1))(gate, gate, up, cw, cb)


def _glu_bwd(gate, up, dact, cw, cb, tb=TOK_BLOCK):
    tn = gate.shape[0]
    nb = tn // tb

    def body(g_ref, gp_ref, gn_ref, u_ref, un_ref, d_ref, dn_ref, cw_ref, cb_ref, dg_ref, du_ref, dcw_ref, dcb_ref):
        i = pl.program_id(0)
        gprev = jnp.where(i == 0, 0.0, gp_ref[...])
        dnext = jnp.where(i == nb - 1, 0.0, dn_ref[...])
        g_ext = jnp.concatenate([gprev, g_ref[...], gn_ref[...]], axis=0)
        gc = _conv_taps(g_ext, cw_ref, cb_ref)[8:, :]
        u_e = jnp.concatenate([u_ref[...], un_ref[...]], axis=0)
        d_e = jnp.concatenate([d_ref[...], dnext], axis=0)
        s = _sigmoid(gc)
        dgc = d_e * u_e * (s * (1.0 + gc * (1.0 - s)))
        du_ref[...] = d_ref[...] * (gc * s)[:tb, :]
        n_e = tb + 8
        dg_ref[...] = (cw_ref[2:3, :] * dgc + cw_ref[1:2, :] * pltpu.roll(dgc, n_e - 1, 0)
                       + cw_ref[0:1, :] * pltpu.roll(dgc, n_e - 2, 0))[:tb, :]

        @pl.when(i == 0)
        def _():
            dcw_ref[...] = jnp.zeros(dcw_ref.shape, F32)
            dcb_ref[...] = jnp.zeros(dcb_ref.shape, F32)

        dgc_b = dgc[:tb, :]
        g0 = g_ext[8:8 + tb, :]
        g1 = pltpu.roll(g_ext, 1, 0)[8:8 + tb, :]
        g2 = pltpu.roll(g_ext, 2, 0)[8:8 + tb, :]
        dcw_ref[2:3, :] += jnp.sum(dgc_b * g0, axis=0, keepdims=True)
        dcw_ref[1:2, :] += jnp.sum(dgc_b * g1, axis=0, keepdims=True)
        dcw_ref[0:1, :] += jnp.sum(dgc_b * g2, axis=0, keepdims=True)
        dcb_ref[...] += jnp.sum(dgc_b, axis=0, keepdims=True)

    sh = jax.ShapeDtypeStruct(gate.shape, F32)
    return pl.pallas_call(
        body, name="glu_bwd", grid=(nb,),
        in_specs=[_blk_spec(gate, tb), _prev8_spec(gate, tb), _next8_spec(gate, tb), _blk_spec(up, tb),
                  _next8_spec(up, tb), _blk_spec(dact, tb), _next8_spec(dact, tb), _full_spec(cw), _full_spec(cb)],
        out_specs=[_blk_spec(gate, tb), _blk_spec(gate, tb), _full_spec(cw), _full_spec(cb)],
        out_shape=[sh, sh, jax.ShapeDtypeStruct(cw.shape, F32), jax.ShapeDtypeStruct(cb.shape, F32)],
        compiler_params=_cparams(1))(gate, gate, gate, up, up, dact, dact, cw, cb)


def _final_loss(x2, tgt, g, tb=TOK_BLOCK):
    tn, dm = x2.shape

    def body(x_ref, t_ref, g_ref, l_ref, dx_ref, dg_ref):
        y, vjp = jax.vjp(_rms_fn, x_ref[...], g_ref[...])
        err = y - t_ref[...]
        dx, dg = vjp(err * (1.0 / dm))

        @pl.when(pl.program_id(0) == 0)
        def _():
            l_ref[...] = jnp.zeros(l_ref.shape, F32)
            dg_ref[...] = jnp.zeros(dg_ref.shape, F32)

        part = 0.5 * jnp.sum(jnp.mean(err * err, axis=-1, keepdims=True), axis=0, keepdims=True)
        l_ref[...] += jnp.broadcast_to(part, l_ref.shape)
        dx_ref[...] = dx
        dg_ref[...] += dg

    return pl.pallas_call(
        body, name="final_loss", grid=(tn // tb,),
        in_specs=[_blk_spec(x2, tb), _blk_spec(tgt, tb), _full_spec(g)],
        out_specs=[pl.BlockSpec((8, 128), lambda i: (0, 0)), _blk_spec(x2, tb), _full_spec(g)],
        out_shape=[jax.ShapeDtypeStruct((8, 128), F32), jax.ShapeDtypeStruct(x2.shape, F32),
                   jax.ShapeDtypeStruct(g.shape, F32)],
        compiler_params=_cparams(1))(x2, tgt, g)


def _pad_cols(w, n):
    return jnp.pad(w, ((0, 0), (0, n - w.shape[1])))


def _pad_rows(w, n):
    return jnp.pad(w, ((0, n - w.shape[0]), (0, 0)))


def _local_step(x, tgt, W, late):
    tn = x.shape[0]
    Wd = RWKV_WIDTH
    row = lambda z: z.reshape(1, -1)
    g_mix, g_ffn, g_fin = row(W['norm_mix_g']), row(W['norm_ffn_g']), row(W['norm_final_g'])

    (h1,) = _tok_fwd("norm_mix_fwd", lambda a, g: (_rms_fn(a, g),), [x], [g_mix], [(D_MODEL,)])
    proj = _mm("proj_fwd", h1, W['w_in_t'], tb=True)
    p_rkv = proj[:, :3 * Wd]
    pre_consts = [row(W['rwkv_mu_w']), row(W['rwkv_mu_a']), row(W['rwkv_mu_g']), row(W['rwkv_mu_r']),
                  row(W['rwkv_mu_k']), row(W['rwkv_mu_v']), row(W['rwkv_w0']),
                  _pad_cols(W['rwkv_w1'], LORA_PAD), _pad_rows(W['rwkv_w2'], LORA_PAD), row(W['rwkv_a0']),
                  _pad_cols(W['rwkv_a1'], LORA_PAD), _pad_rows(W['rwkv_a2'], LORA_PAD),
                  W['rwkv_g1'], W['rwkv_g2']]
    r, k0, v, lw, a, g = _pre_a_fwd(h1, p_rkv, pre_consts)
    k_k, k_a = row(W['rwkv_k_k']), row(W['rwkv_k_a'])
    nkk, k, b = _tok_fwd("rwkv_pre_b_fwd", _pre_b_fn, [k0, a], [k_k, k_a], [(Wd,)] * 3)
    y_scan, ck, gathered = _cscan_fwd(r, lw, k, v, nkk, b, late)
    w_out, w_gate_t, w_up_t, w_down = [g_.reshape(-1, D_MODEL) for g_ in gathered]
    post_consts = [row(W['rwkv_lnx_w']), row(W['rwkv_lnx_b']), row(W['rwkv_r_k'])]
    (y_rwkv,) = _tok_fwd("rwkv_post_fwd", _rwkv_post_fn, [y_scan, r, k, v, g], post_consts, [(Wd,)])

    pos = jnp.arange(tn, dtype=F32)
    half = RET_HEAD_DIM // 2
    inv_freq = ROPE_BASE ** (-jnp.arange(half, dtype=F32) / half)
    ang = pos[:, None] * inv_freq[None, :]
    cos2 = jnp.concatenate([jnp.cos(ang), jnp.cos(ang)], axis=1)
    sin2 = jnp.concatenate([-jnp.sin(ang), jnp.sin(ang)], axis=1)
    lg = jnp.log(1.0 - 2.0 ** (-5.0 - jnp.arange(RET_HEADS, dtype=F32)))
    lg = jnp.broadcast_to(lg[:, None, None], (RET_HEADS, 1, 128))
    q_p, k_p = proj[:, 3 * Wd:4 * Wd], proj[:, 4 * Wd:5 * Wd]
    v_ret, g_ret = proj[:, 5 * Wd:6 * Wd], proj[:, 6 * Wd:7 * Wd]
    q_rot, k_rot = _tok_fwd("ret_rotary_fwd", _rotary_fn, [cos2, sin2, q_p, k_p], [], [(RET_WIDTH,)] * 2)
    y_ret_raw = _ret_attn_fwd(lg, q_rot, k_rot, v_ret)
    gn_w = row(W['ret_gn_w'])
    (y_ret,) = _tok_fwd("ret_post_fwd", _ret_post_fn, [y_ret_raw, g_ret], [gn_w], [(RET_WIDTH,)])

    ycat = jnp.concatenate([y_rwkv, y_ret], axis=1)
    x1 = _mm("out_proj_fwd", ycat, w_out, add=x)
    (h2,) = _tok_fwd("norm_ffn_fwd", lambda a_, g_: (_rms_fn(a_, g_),), [x1], [g_ffn], [(D_MODEL,)])
    gate = _mm("ffn_gate_fwd", h2, w_gate_t, tb=True)
    up = _mm("ffn_up_fwd", h2, w_up_t, tb=True)
    cw = W['ffn_conv_w']
    cb = row(W['ffn_conv_b'])
    act = _glu_fwd(gate, up, cw, cb)
    x2 = _mm("ffn_down_fwd", act, w_down, add=x1)
    loss8, dx2, dg_fin = _final_loss(x2, tgt, g_fin)

    G = {'norm_final_g': dg_fin}
    dact = _mm("ffn_down_dx", dx2, w_down, tb=True)
    d_down = _mm("ffn_down_dw", act, dx2, ta=True, out_dtype=BF16)
    dgate, dup, dcw, dcb = _glu_bwd(gate, up, dact, cw, cb)
    G['ffn_conv_w'], G['ffn_conv_b'] = dcw, dcb
    dh2 = _mm("ffn_gate_dx", dgate, w_gate_t)
    dh2 = _mm("ffn_up_dx", dup, w_up_t, add=dh2)
    d_gate_t = _mm("ffn_gate_dw", dgate, h2, ta=True, out_dtype=BF16)
    d_up_t = _mm("ffn_up_dw", dup, h2, ta=True, out_dtype=BF16)
    dx1, G['norm_ffn_g'] = _tok_bwd("norm_ffn_bwd", lambda a_, g_: (_rms_fn(a_, g_),), [], [x1], [g_ffn], [dh2], add=dx2)
    dycat = _mm("out_proj_dx", dx1, w_out, tb=True)
    d_out = _mm("out_proj_dw", ycat, dx1, ta=True, out_dtype=BF16)
    late_grads = [z.reshape(N_DEV, -1, D_MODEL) for z in (d_out, d_gate_t, d_up_t, d_down)]
    dy_rwkv, dy_ret = dycat[:, :Wd], dycat[:, Wd:]

    dyr_raw, dg_ret, G['ret_gn_w'] = _tok_bwd("ret_post_bwd", _ret_post_fn, [], [y_ret_raw, g_ret], [gn_w], [dy_ret])
    dq_rot, dk_rot, dv_ret = _ret_attn_bwd(lg, q_rot, k_rot, v_ret, dyr_raw)
    dq_p, dk_p = _tok_bwd("ret_rotary_bwd", _rotary_fn, [cos2, sin2], [q_p, k_p], [], [dq_rot, dk_rot])

    dy_scan, dr1, dk1, dv1, dg, G['rwkv_lnx_w'], G['rwkv_lnx_b'], G['rwkv_r_k'] = _tok_bwd(
        "rwkv_post_bwd", _rwkv_post_fn, [], [y_scan, r, k, v, g], post_consts, [dy_rwkv])
    (dr2, dlw, dk2, dv2, dnkk, db), late_parts = _cscan_bwd(r, lw, k, v, nkk, b, dy_scan, ck, late_grads)
    dk0, da, G['rwkv_k_k'], G['rwkv_k_a'] = _tok_bwd(
        "rwkv_pre_b_bwd", _pre_b_fn, [], [k0, a], [k_k, k_a], [dnkk, (dk1, dk2), db])
    pre_cts = [(dr1, dr2), dk0, (dv1, dv2), dlw, da, dg]
    pre_out = _pre_a_bwd(h1, p_rkv, pre_consts, pre_cts)
    dh1_a, dp_rkv = pre_out[0], pre_out[1]
    (G['rwkv_mu_w'], G['rwkv_mu_a'], G['rwkv_mu_g'], G['rwkv_mu_r'], G['rwkv_mu_k'], G['rwkv_mu_v'], G['rwkv_w0'],
     dw1, dw2, G['rwkv_a0'], da1, da2, G['rwkv_g1'], G['rwkv_g2']) = pre_out[2:]
    G['rwkv_w1'], G['rwkv_w2'] = dw1[:, :64], dw2[:64, :]
    G['rwkv_a1'], G['rwkv_a2'] = da1[:, :64], da2[:64, :]

    dproj = jnp.concatenate([dp_rkv, dq_p, dk_p, dv_ret, dg_ret], axis=1)
    dh1 = _mm("proj_dx", dproj, W['w_in_t'], add=dh1_a)
    G['w_in_t'] = _mm("proj_dw", dproj, h1, ta=True, out_dtype=BF16)
    dx, G['norm_mix_g'] = _tok_bwd("norm_mix_bwd", lambda a_, g_: (_rms_fn(a_, g_),), [], [x], [g_mix], [dh1], add=dx1)
    return loss8[0, 0], dx, G, late_parts


def _adamw(name, parts, w, m, v):
    rows, cols = w.shape
    sub = 8 * 4 // parts.dtype.itemsize
    tb = max(t for t in range(sub, 65, sub) if rows % t == 0) if rows > 64 else rows
    c1 = 1.0 - ADAM_B1 ** ADAM_STEP
    c2 = 1.0 - ADAM_B2 ** ADAM_STEP

    def body(p_ref, w_ref, m_ref, v_ref, g_ref, d_ref, nm_ref, nv_ref):
        g = p_ref[0].astype(F32)
        for d in range(1, N_DEV):
            g = g + p_ref[d].astype(F32)
        mn = ADAM_B1 * m_ref[...] + (1.0 - ADAM_B1) * g
        vn = ADAM_B2 * v_ref[...] + (1.0 - ADAM_B2) * (g * g)
        m_hat = mn / c1
        v_hat = vn / c2
        g_ref[...] = g
        d_ref[...] = -ADAM_LR * (m_hat / (jnp.sqrt(v_hat) + ADAM_EPS) + ADAM_WD * w_ref[...])
        nm_ref[...] = mn
        nv_ref[...] = vn

    spec = pl.BlockSpec((tb, cols), lambda i: (i, 0))
    sh = jax.ShapeDtypeStruct((rows, cols), F32)
    return pl.pallas_call(
        body, name=name, grid=(rows // tb,),
        in_specs=[pl.BlockSpec((N_DEV, tb, cols), lambda i: (0, i, 0)), spec, spec, spec],
        out_specs=[spec] * 4, out_shape=[sh] * 4, compiler_params=_cparams(1))(parts, w, m, v)


def _local_shape(name):
    gs, ax = SHARDED[name]
    ls = list(gs)
    ls[ax] //= N_DEV
    return tuple(ls)


def _seg(flat, seg):
    n = flat.shape[-1]
    pad = _round_up(n, seg) - n
    if pad:
        flat = jnp.pad(flat, [(0, 0)] * (flat.ndim - 1) + [(0, pad)])
    return flat


def _split3(w):
    hi = w.astype(BF16)
    r1 = w - hi.astype(F32)
    mid = r1.astype(BF16)
    lo = (r1 - mid.astype(F32)).astype(BF16)
    return hi, mid, lo


def _pack_small_shards(shards):
    pieces = []
    for name in SMALL_NAMES:
        flat = shards[name].reshape(-1)
        if name == 'ffn_conv_w':
            pieces += [_seg(p, BF16_SEG) for p in _split3(flat)]
        else:
            pieces.append(flat.astype(BF16))
    return jnp.concatenate(pieces).reshape(-1, 128)


def _unpack_small(gathered):
    flat = gathered.reshape(N_DEV, -1)
    out, off = {}, 0
    for name in SMALL_NAMES:
        gs, ax = SHARDED[name]
        ls = _local_shape(name)
        n = int(np.prod(ls))
        if name == 'ffn_conv_w':
            nseg = _round_up(n, BF16_SEG)
            hi, mid, lo = (flat[:, off + j * nseg: off + j * nseg + n].astype(F32) for j in range(3))
            sh = ((hi + mid) + lo).reshape(N_DEV, 3, -1)
            out[name] = jnp.swapaxes(sh, 0, 1).reshape(3, D_FF)
            off += 3 * nseg
        else:
            sh = flat[:, off:off + n].reshape((N_DEV,) + ls[1:])
            out[name] = sh.reshape(gs[1:]) if ax == 1 else jnp.swapaxes(sh, 0, 1).reshape(gs[1:])
            off += n
    return out


def _small_pieces(sharded, repl):
    return [sharded[n].reshape(-1) for n in SMALL_NAMES] + [repl[n].reshape(-1) for n in REPL_NAMES]


def _pack_small_local(d):
    flat = jnp.concatenate(_small_pieces(d, d))
    return _seg(flat, F32_SEG).reshape(-1, 128)


def _pack_small_grads(G):
    pieces = []
    for name in SMALL_NAMES:
        gs, ax = SHARDED[name]
        g = G[name]
        if name == 'ffn_conv_w':
            sh = jnp.swapaxes(g.reshape(3, N_DEV, -1), 0, 1)
        elif ax == 1:
            sh = g
        else:
            sh = jnp.swapaxes(g.reshape(g.shape[0], N_DEV, -1), 0, 1)
        pieces.append(sh.reshape(N_DEV, -1))
    rep = jnp.concatenate([G[n].reshape(-1) for n in REPL_NAMES])
    pieces.append(jnp.broadcast_to(rep[None, :], (N_DEV, rep.shape[0])))
    flat = _seg(jnp.concatenate(pieces, axis=1), F32_SEG)
    return flat.reshape(N_DEV, -1, 128)


def _unpack_small_local(packed, local_shapes):
    flat = packed.reshape(-1)
    out, off = {}, 0
    for name in SMALL_NAMES + REPL_NAMES:
        n = int(np.prod(local_shapes[name]))
        out[name] = flat[off:off + n].reshape(local_shapes[name])
        off += n
    return out


def kernel(x, *rest):
    nw = len(WEIGHT_NAMES)
    assert len(rest) == 3 * nw + 1
    weights = dict(zip(WEIGHT_NAMES, rest[:nw]))
    loss_target = rest[nw]
    moms = dict(zip(WEIGHT_NAMES, rest[nw + 1:2 * nw + 1]))
    vars_ = dict(zip(WEIGHT_NAMES, rest[2 * nw + 1:]))
    local_shapes = {n: weights[n].shape for n in WEIGHT_NAMES}

    def native2d(name, a):
        a2 = a.reshape(a.shape[-2], a.shape[-1])
        return a2.T if name in BIG_T else a2

    def from2d(name, a2):
        return (a2.T if name in BIG_T else a2).reshape(local_shapes[name])

    big_w = {n: native2d(n, weights[n]) for n in BIG_NAMES}
    w_in_t_sh = big_w['w_in'].astype(BF16)
    late = [big_w[n].astype(BF16) for n in LATE_NAMES]
    small_sh = _pack_small_shards({n: weights[n] for n in SMALL_NAMES})
    w_in_g, small_g = _exchange("weights_all_gather", [w_in_t_sh, small_sh], False)
    W = _unpack_small(small_g)
    W['w_in_t'] = w_in_g.reshape(-1, D_MODEL)
    for n in REPL_NAMES:
        W[n] = weights[n][0] if n != 'norm_final_g' else weights[n]

    loss, dx, G, late_parts = _local_step(x[0], loss_target[0], W, late)

    w_in_parts, small_parts = _exchange(
        "grads_all_to_all", [G['w_in_t'].reshape(N_DEV, -1, D_MODEL), _pack_small_grads(G)], True)
    results = {}
    for n, parts in zip(['w_in'] + LATE_NAMES, [w_in_parts] + list(late_parts)):
        res = _adamw("adamw_" + n, parts, big_w[n], native2d(n, moms[n]), native2d(n, vars_[n]))
        results[n] = [from2d(n, r) for r in res]
    small_res = _adamw("adamw_small", small_parts, _pack_small_local(weights), _pack_small_local(moms),
                       _pack_small_local(vars_))
    small_out = [_unpack_small_local(p, local_shapes) for p in small_res]

    loss = lax.psum(loss, ("x", "y", "c"))
    outs = [loss, dx[None]]
    for j in range(4):
        outs += [results[n][j] if n in results else small_out[j][n] for n in WEIGHT_NAMES]
    return tuple(outs)
```

```python
import functools
import math

import numpy as np
import jax
import jax.numpy as jnp
from jax import lax
from jax.experimental import pallas as pl
from jax.experimental.pallas import tpu as pltpu

F32 = jnp.float32
BF16 = jnp.bfloat16

N_DEV = 8
D_MODEL = 1024
RWKV_HEADS = 8
RWKV_HEAD_DIM = 64
RWKV_WIDTH = 512
RET_HEADS = 4
RET_HEAD_DIM = 128
RET_WIDTH = 512
LORA_PAD = 128
D_FF = 2816
NORM_EPS = 1e-6
RWKV_GN_EPS = 64e-5
RET_GN_EPS = 1e-5
ROPE_BASE = 10000.0
ADAM_LR, ADAM_B1, ADAM_B2, ADAM_EPS, ADAM_WD, ADAM_STEP = 0.001, 0.9, 0.999, 1e-08, 0.01, 10

VMEM_LIMIT = 56 * 1024 * 1024
TOK_BLOCK = 256
SCAN_CHUNK = 64
ATT_BLOCK = 512
BF16_SEG = 2048
F32_SEG = 1024

WEIGHT_NAMES = ['norm_mix_g', 'w_in', 'rwkv_mu_r', 'rwkv_mu_k', 'rwkv_mu_v', 'rwkv_mu_w', 'rwkv_mu_a',
                'rwkv_mu_g', 'rwkv_w0', 'rwkv_w1', 'rwkv_w2', 'rwkv_a0', 'rwkv_a1', 'rwkv_a2', 'rwkv_g1',
                'rwkv_g2', 'rwkv_k_k', 'rwkv_k_a', 'rwkv_r_k', 'rwkv_lnx_w', 'rwkv_lnx_b', 'ret_gn_w',
                'w_out', 'norm_ffn_g', 'ffn_w_gate', 'ffn_w_up', 'ffn_conv_w', 'ffn_conv_b', 'ffn_w_down',
                'norm_final_g']
SHARDED = {
    'w_in': ((1, 1024, 3584), 2), 'rwkv_w1': ((1, 1024, 64), 1), 'rwkv_w2': ((1, 64, 512), 2),
    'rwkv_a1': ((1, 1024, 64), 1), 'rwkv_a2': ((1, 64, 512), 2), 'rwkv_g1': ((1, 1024, 128), 1),
    'rwkv_g2': ((1, 128, 512), 2), 'w_out': ((1, 1024, 1024), 1), 'ffn_w_gate': ((1, 1024, 2816), 2),
    'ffn_w_up': ((1, 1024, 2816), 2), 'ffn_conv_w': ((1, 3, 1, 2816), 3), 'ffn_w_down': ((1, 2816, 1024), 1),
}
REPL_NAMES = [n for n in WEIGHT_NAMES if n not in SHARDED]
BIG_NAMES = ['w_in', 'w_out', 'ffn_w_gate', 'ffn_w_up', 'ffn_w_down']
BIG_T = ('w_in', 'ffn_w_gate', 'ffn_w_up')
LATE_NAMES = ['w_out', 'ffn_w_gate', 'ffn_w_up', 'ffn_w_down']
SMALL_NAMES = [n for n in WEIGHT_NAMES if n in SHARDED and n not in BIG_NAMES]


def _cparams(n_grid):
    return pltpu.CompilerParams(dimension_semantics=("arbitrary",) * n_grid, vmem_limit_bytes=VMEM_LIMIT)


def _round_up(n, m):
    return (n + m - 1) // m * m


@jax.custom_vjp
def _bdot(x, w):
    return jnp.dot(x.astype(BF16), w.astype(BF16), preferred_element_type=F32)


def _bdot_fwd(x, w):
    return _bdot(x, w), (x, w)


def _bdot_bwd(res, g):
    x, w = res
    gb = g.astype(BF16)
    dx = lax.dot_general(gb, w.astype(BF16), (((1,), (1,)), ((), ())), preferred_element_type=F32)
    dw = lax.dot_general(x.astype(BF16), gb, (((0,), (0,)), ((), ())), preferred_element_type=F32)
    return dx, dw.astype(w.dtype)


_bdot.defvjp(_bdot_fwd, _bdot_bwd)


@jax.custom_vjp
def _shift_rows(x, prev):
    rolled = pltpu.roll(x, 1, 0)
    row = lax.broadcasted_iota(jnp.int32, x.shape, 0)
    return jnp.where(row == 0, jnp.broadcast_to(prev, x.shape), rolled)


def _shift_rows_fwd(x, prev):
    return _shift_rows(x, prev), None


def _shift_rows_bwd(_, g):
    n = g.shape[0]
    rolled = pltpu.roll(g, n - 1, 0)
    row = lax.broadcasted_iota(jnp.int32, g.shape, 0)
    return jnp.where(row == n - 1, 0.0, rolled), g[0:1, :]


_shift_rows.defvjp(_shift_rows_fwd, _shift_rows_bwd)


@jax.custom_vjp
def _swap_halves(x):
    return pltpu.roll(x, 64, 1)


_swap_halves.defvjp(lambda x: (_swap_halves(x), None), lambda _, g: (pltpu.roll(g, 64, 1),))


def _sigmoid(x):
    return 1.0 / (1.0 + jnp.exp(-x))


def _softplus(x):
    return jnp.maximum(x, 0.0) + jnp.log(1.0 + jnp.exp(-jnp.abs(x)))


def _rms_fn(x, g):
    return x * lax.rsqrt(jnp.mean(x * x, axis=-1, keepdims=True) + NORM_EPS) * g


def _pre_a_fn(h1, h1p, p, pp, mu_w, mu_a, mu_g, mu_r, mu_k, mu_v, w0, w1, w2, a0, a1, a2, g1, g2):
    W = RWKV_WIDTH
    h1s = _shift_rows(h1, h1p)
    ps = _shift_rows(p, pp)
    dx = h1s - h1
    xw = h1 + dx * mu_w
    xa = h1 + dx * mu_a
    xg = h1 + dx * mu_g
    dp = ps - p
    r = p[:, 0:W] + dp[:, 0:W] * mu_r
    k0 = p[:, W:2 * W] + dp[:, W:2 * W] * mu_k
    v = p[:, 2 * W:3 * W] + dp[:, 2 * W:3 * W] * mu_v
    wl = w0 + _bdot(jnp.tanh(_bdot(xw, w1)), w2)
    w_log = -_softplus(-wl) - 0.5
    lw = -jnp.exp(w_log)
    a = _sigmoid(a0 + _bdot(_bdot(xa, a1), a2))
    g = _bdot(_sigmoid(_bdot(xg, g1)), g2)
    return r, k0, v, lw, a, g


def _head_sum_raw(x):
    n = x.shape[1]
    ii = lax.broadcasted_iota(jnp.int32, (n, n), 0) // RWKV_HEAD_DIM
    jj = lax.broadcasted_iota(jnp.int32, (n, n), 1) // RWKV_HEAD_DIM
    ones = (ii == jj).astype(BF16)
    xh = x.astype(BF16)
    xl = (x - xh.astype(F32)).astype(BF16)
    return jnp.dot(xh, ones, preferred_element_type=F32) + jnp.dot(xl, ones, preferred_element_type=F32)


@jax.custom_vjp
def _head_sum(x):
    return _head_sum_raw(x)


_head_sum.defvjp(lambda x: (_head_sum_raw(x), None), lambda _, g: (_head_sum_raw(g),))


def _pre_b_fn(k0, a, k_k, k_a):
    kkr = k0 * k_k
    nrm = jnp.sqrt(_head_sum(kkr * kkr))
    kk = kkr / jnp.maximum(nrm, 1e-12)
    k = k0 * (1.0 + (a - 1.0) * k_a)
    return -kk, k, kk * a


def _rwkv_post_fn(y, r, k, v, g, lnx_w, lnx_b, r_k):
    inv = 1.0 / RWKV_HEAD_DIM
    mu = _head_sum(y) * inv
    yc = y - mu
    var = _head_sum(yc * yc) * inv
    yn = yc * lax.rsqrt(var + RWKV_GN_EPS) * lnx_w + lnx_b
    bonus = _head_sum(r * k * r_k) * v
    return ((yn + bonus) * g,)


def _rotary_fn(cos2, sin2, q, k):
    qs, ks = [], []
    for h in range(RET_HEADS):
        sl = slice(h * RET_HEAD_DIM, (h + 1) * RET_HEAD_DIM)
        qh, kh = q[:, sl], k[:, sl]
        qs.append(qh * cos2 + _swap_halves(qh) * sin2)
        ks.append((kh * cos2 + _swap_halves(kh) * sin2) * (RET_HEAD_DIM ** -0.5))
    return jnp.concatenate(qs, axis=1), jnp.concatenate(ks, axis=1)


def _ret_post_fn(y, gp, gn_w):
    outs = []
    for h in range(RET_HEADS):
        sl = slice(h * RET_HEAD_DIM, (h + 1) * RET_HEAD_DIM)
        yh = y[:, sl]
        mu = jnp.mean(yh, axis=-1, keepdims=True)
        yc = yh - mu
        var = jnp.mean(yc * yc, axis=-1, keepdims=True)
        outs.append(yc * lax.rsqrt(var + RET_GN_EPS) * gn_w[:, sl])
    yn = jnp.concatenate(outs, axis=1)
    return (gp * _sigmoid(gp) * yn,)


def _blk_spec(a, tb, rev_nb=None):
    nd = a.ndim
    if rev_nb is None:
        return pl.BlockSpec((tb,) + a.shape[1:], lambda i: (i,) + (0,) * (nd - 1))
    return pl.BlockSpec((tb,) + a.shape[1:], lambda i: (rev_nb - 1 - i,) + (0,) * (nd - 1))


def _full_spec(a):
    nd = a.ndim
    return pl.BlockSpec(a.shape, lambda i: (0,) * nd)


def _tok_fwd(name, fn, toks, consts, out_tails, tb=TOK_BLOCK, out_dtypes=None):
    out_dtypes = out_dtypes or [F32] * len(out_tails)
    n_in = len(toks) + len(consts)
    tn = toks[0].shape[0]

    def body(*refs):
        outs = fn(*[r[...] for r in refs[:n_in]])
        for r, o in zip(refs[n_in:], outs):
            r[...] = o.astype(r.dtype)

    out_shape = [jax.ShapeDtypeStruct((tn,) + tuple(s), dt) for s, dt in zip(out_tails, out_dtypes)]
    return pl.pallas_call(
        body, name=name, grid=(tn // tb,),
        in_specs=[_blk_spec(a, tb) for a in toks] + [_full_spec(c) for c in consts],
        out_specs=[_blk_spec(o, tb) for o in out_shape], out_shape=out_shape,
        compiler_params=_cparams(1))(*toks, *consts)


def _tok_bwd(name, fn, aux, toks, consts, cts, add=None, tb=TOK_BLOCK, tok_dtypes=None):
    n_aux, n_tok, n_c = len(aux), len(toks), len(consts)
    ct_groups = [c if isinstance(c, (tuple, list)) else (c,) for c in cts]
    ct_flat = [a for grp in ct_groups for a in grp]
    n_ct = len(ct_flat)
    n_add = 0 if add is None else 1
    tn = toks[0].shape[0]

    def body(*refs):
        pos = 0
        aux_v = [r[...] for r in refs[pos:pos + n_aux]]; pos += n_aux
        tok_v = [r[...] for r in refs[pos:pos + n_tok]]; pos += n_tok
        const_v = [r[...] for r in refs[pos:pos + n_c]]; pos += n_c
        ct_refs = refs[pos:pos + n_ct]; pos += n_ct
        add_refs = refs[pos:pos + n_add]; pos += n_add
        dtok_refs = refs[pos:pos + n_tok]; pos += n_tok
        dconst_refs = refs[pos:pos + n_c]
        ct_v, q = [], 0
        for grp in ct_groups:
            s = ct_refs[q][...]
            for r in ct_refs[q + 1:q + len(grp)]:
                s = s + r[...]
            q += len(grp)
            ct_v.append(s)
        _, vjp = jax.vjp(lambda *tc: fn(*aux_v, *tc), *tok_v, *const_v)
        grads = vjp(tuple(ct_v))
        for j, r in enumerate(dtok_refs):
            gj = grads[j]
            if j == 0 and n_add:
                gj = gj + add_refs[0][...]
            r[...] = gj.astype(r.dtype)

        @pl.when(pl.program_id(0) == 0)
        def _():
            for r in dconst_refs:
                r[...] = jnp.zeros(r.shape, F32)

        for j, r in enumerate(dconst_refs):
            r[...] += grads[n_tok + j]

    ins = list(aux) + list(toks) + list(consts) + ct_flat + ([add] if n_add else [])
    in_specs = ([_blk_spec(a, tb) for a in aux] + [_blk_spec(a, tb) for a in toks] + [_full_spec(c) for c in consts]
                + [_blk_spec(a, tb) for a in ct_flat] + ([_blk_spec(add, tb)] if n_add else []))
    tok_dtypes = tok_dtypes or [F32] * n_tok
    out_shape = ([jax.ShapeDtypeStruct(a.shape, dt) for a, dt in zip(toks, tok_dtypes)]
                 + [jax.ShapeDtypeStruct(c.shape, F32) for c in consts])
    out_specs = [_blk_spec(a, tb) for a in toks] + [_full_spec(c) for c in consts]
    return pl.pallas_call(body, name=name, grid=(tn // tb,), in_specs=in_specs, out_specs=out_specs,
                          out_shape=out_shape, compiler_params=_cparams(1))(*ins)


MM_VMEM_BUDGET = 40 * 1024 * 1024
MM_STEP_SECONDS = 0.4e-6
MM_HBM_BYTES_PER_SECOND = 2.5e12


def _mm_tiles(m, n, kd, a_bytes, b_bytes, o_bytes, has_add):
    divs = lambda d: [t for t in range(128, d + 1, 128) if d % t == 0]
    best = None
    for tm in divs(m):
        for tn in divs(n):
            for tk in divs(kd):
                ni, nj, nk = m // tm, n // tn, kd // tk
                vmem = (2 * tm * tk * a_bytes + 2 * tk * tn * b_bytes + tm * tn * 4 + 2 * tm * tn * o_bytes
                        + (2 * tm * tn * 4 if has_add else 0) + 2 * (tm * tk + tk * tn) + tm * tn * 4)
                if vmem > MM_VMEM_BUDGET:
                    continue
                a_traffic = m * kd * a_bytes * (nj if nk > 1 else 1)
                b_traffic = kd * n * b_bytes * (ni if nj * nk > 1 else 1)
                cost = ni * nj * nk * MM_STEP_SECONDS + (a_traffic + b_traffic) / MM_HBM_BYTES_PER_SECOND
                if best is None or cost < best[0]:
                    best = (cost, tm, tn, tk)
    return best[1:]


def _mm(name, a, b, ta=False, tb=False, add=None, out_dtype=F32):
    if ta:
        kd, m = a.shape
    else:
        m, kd = a.shape
    if tb:
        n, kb = b.shape
    else:
        kb, n = b.shape
    assert kd == kb, (a.shape, b.shape)
    tm, tn, tk = _mm_tiles(m, n, kd, a.dtype.itemsize, b.dtype.itemsize, jnp.dtype(out_dtype).itemsize,
                           add is not None)
    nk = kd // tk
    has_add = add is not None
    dims = (((0 if ta else 1,), (1 if tb else 0,)), ((), ()))

    def body(*refs):
        a_ref, b_ref = refs[0], refs[1]
        o_ref, acc_ref = refs[-2], refs[-1]
        k = pl.program_id(2)

        @pl.when(k == 0)
        def _():
            acc_ref[...] = refs[2][...] if has_add else jnp.zeros(acc_ref.shape, F32)

        acc_ref[...] += lax.dot_general(a_ref[...].astype(BF16), b_ref[...].astype(BF16), dims,
                                        preferred_element_type=F32)

        @pl.when(k == nk - 1)
        def _():
            o_ref[...] = acc_ref[...].astype(out_dtype)

    a_spec = pl.BlockSpec((tk, tm), lambda i, j, k: (k, i)) if ta else pl.BlockSpec((tm, tk), lambda i, j, k: (i, k))
    b_spec = pl.BlockSpec((tn, tk), lambda i, j, k: (j, k)) if tb else pl.BlockSpec((tk, tn), lambda i, j, k: (k, j))
    o_spec = pl.BlockSpec((tm, tn), lambda i, j, k: (i, j))
    ins = [a, b] + ([add] if has_add else [])
    in_specs = [a_spec, b_spec] + ([o_spec] if has_add else [])
    return pl.pallas_call(body, name=name, grid=(m // tm, n // tn, nk), in_specs=in_specs, out_specs=o_spec,
                          out_shape=jax.ShapeDtypeStruct((m, n), out_dtype),
                          scratch_shapes=[pltpu.VMEM((tm, tn), F32)], compiler_params=_cparams(3))(*ins)


def _prev8_spec(a, tb, rev_nb=None):
    r = tb // 8
    if rev_nb is None:
        return pl.BlockSpec((8, a.shape[1]), lambda i: (jnp.maximum(i * r - 1, 0), 0))
    return pl.BlockSpec((8, a.shape[1]), lambda i: (jnp.maximum((rev_nb - 1 - i) * r - 1, 0), 0))


def _pre_a_fwd(h1, p, consts, tb=TOK_BLOCK):
    tn = h1.shape[0]

    def body(h1_ref, h1h_ref, p_ref, ph_ref, *rest):
        c_refs, o_refs = rest[:len(consts)], rest[len(consts):]
        first = pl.program_id(0) == 0
        h1p = jnp.where(first, 0.0, h1h_ref[7:8, :])
        pp = jnp.where(first, 0.0, ph_ref[7:8, :])
        outs = _pre_a_fn(h1_ref[...], h1p, p_ref[...], pp, *[c[...] for c in c_refs])
        for r, o in zip(o_refs, outs):
            r[...] = o

    out_shape = [jax.ShapeDtypeStruct((tn, RWKV_WIDTH), F32) for _ in range(6)]
    return pl.pallas_call(
        body, name="rwkv_pre_a_fwd", grid=(tn // tb,),
        in_specs=[_blk_spec(h1, tb), _prev8_spec(h1, tb), _blk_spec(p, tb), _prev8_spec(p, tb)]
        + [_full_spec(c) for c in consts],
        out_specs=[_blk_spec(o, tb) for o in out_shape], out_shape=out_shape,
        compiler_params=_cparams(1))(h1, h1, p, p, *consts)


def _pre_a_bwd(h1, p, consts, cts, tb=TOK_BLOCK):
    tn = h1.shape[0]
    nb = tn // tb
    n_c = len(consts)
    ct_groups = [c if isinstance(c, (tuple, list)) else (c,) for c in cts]
    ct_flat = [a for grp in ct_groups for a in grp]
    n_ct = len(ct_flat)

    def body(*refs):
        h1_ref, h1h_ref, p_ref, ph_ref = refs[:4]
        c_refs = refs[4:4 + n_c]
        ct_refs = refs[4 + n_c:4 + n_c + n_ct]
        dh1_ref, dp_ref = refs[4 + n_c + n_ct:6 + n_c + n_ct]
        dc_refs = refs[6 + n_c + n_ct:6 + 2 * n_c + n_ct]
        ch_ref, cp_ref = refs[-2], refs[-1]
        i = pl.program_id(0)
        first_block = i == nb - 1
        h1p = jnp.where(first_block, 0.0, h1h_ref[7:8, :])
        pp = jnp.where(first_block, 0.0, ph_ref[7:8, :])
        ct_v, q = [], 0
        for grp in ct_groups:
            s = ct_refs[q][...]
            for r in ct_refs[q + 1:q + len(grp)]:
                s = s + r[...]
            q += len(grp)
            ct_v.append(s)
        _, vjp = jax.vjp(_pre_a_fn, h1_ref[...], h1p, p_ref[...], pp, *[c[...] for c in c_refs])
        grads = vjp(tuple(ct_v))

        @pl.when(i == 0)
        def _():
            ch_ref[...] = jnp.zeros(ch_ref.shape, F32)
            cp_ref[...] = jnp.zeros(cp_ref.shape, F32)
            for r in dc_refs:
                r[...] = jnp.zeros(r.shape, F32)

        rowh = lax.broadcasted_iota(jnp.int32, (tb, h1.shape[1]), 0)
        rowp = lax.broadcasted_iota(jnp.int32, (tb, p.shape[1]), 0)
        dh1_ref[...] = grads[0] + jnp.where(rowh == tb - 1, jnp.broadcast_to(ch_ref[0:1, :], rowh.shape), 0.0)
        dp_ref[...] = (grads[2] + jnp.where(rowp == tb - 1, jnp.broadcast_to(cp_ref[0:1, :], rowp.shape), 0.0)
                       ).astype(dp_ref.dtype)
        ch_ref[0:1, :] = grads[1]
        cp_ref[0:1, :] = grads[3]
        for j, r in enumerate(dc_refs):
            r[...] += grads[4 + j]

    ins = [h1, h1, p, p] + list(consts) + ct_flat
    in_specs = ([_blk_spec(h1, tb, nb), _prev8_spec(h1, tb, nb), _blk_spec(p, tb, nb), _prev8_spec(p, tb, nb)]
                + [_full_spec(c) for c in consts] + [_blk_spec(a, tb, nb) for a in ct_flat])
    out_shape = ([jax.ShapeDtypeStruct(h1.shape, F32), jax.ShapeDtypeStruct(p.shape, BF16)]
                 + [jax.ShapeDtypeStruct(c.shape, F32) for c in consts])
    out_specs = [_blk_spec(h1, tb, nb), _blk_spec(p, tb, nb)] + [_full_spec(c) for c in consts]
    return pl.pallas_call(body, name="rwkv_pre_a_bwd", grid=(nb,), in_specs=in_specs, out_specs=out_specs,
                          out_shape=out_shape,
                          scratch_shapes=[pltpu.VMEM((8, h1.shape[1]), F32), pltpu.VMEM((8, p.shape[1]), F32)],
                          compiler_params=_cparams(1))(*ins)


def _my_index():
    return 4 * lax.axis_index("x") + 2 * lax.axis_index("y") + lax.axis_index("c")


def _peer(k):
    x, y, c = lax.axis_index("x"), lax.axis_index("y"), lax.axis_index("c")
    px = 1 - x if k & 4 else x
    py = 1 - y if k & 2 else y
    pc = 1 - c if k & 1 else c
    return (px, py, pc), 4 * px + 2 * py + pc


def _xchg_sems(n):
    return [pltpu.SemaphoreType.DMA((n * (N_DEV - 1),)), pltpu.SemaphoreType.DMA((n * (N_DEV - 1),)),
            pltpu.SemaphoreType.DMA((n,))]


def _scatter_copies(srcs, dsts, sems, incoming=False):
    send_sems, recv_sems, local_sems = sems
    me = _my_index()
    local, remote = [], []
    for i, (s, d) in enumerate(zip(srcs, dsts)):
        if not incoming:
            local.append(pltpu.make_async_copy(s.at[me], d.at[me], local_sems.at[i]))
        for k in range(1, N_DEV):
            peer, plin = _peer(k)
            j = i * (N_DEV - 1) + k - 1
            s_slot, d_slot = (me, plin) if incoming else (plin, me)
            remote.append(pltpu.make_async_remote_copy(
                src_ref=s.at[s_slot], dst_ref=d.at[d_slot], send_sem=send_sems.at[j],
                recv_sem=recv_sems.at[j], device_id=peer, device_id_type=pl.DeviceIdType.MESH))
    return local, remote


def _scatter_start(srcs, dsts, sems):
    local, out = _scatter_copies(srcs, dsts, sems)
    for cp in local + out:
        cp.start()


def _scatter_wait(srcs, dsts, sems):
    for cp in _scatter_copies(srcs, dsts, sems, incoming=True)[1]:
        cp.wait_recv()
    local, out = _scatter_copies(srcs, dsts, sems)
    for cp in out:
        cp.wait_send()
    for cp in local:
        cp.wait()


_ICI_PEERS = (2, 4, 6)


def _gather_copies(srcs, dsts, sems, group):
    send_sems, recv_sems, local_sems = sems
    me = _my_index()
    sib, sib_lin = _peer(1)
    out = []
    for i, (s, d) in enumerate(zip(srcs, dsts)):
        def mk(q, src, dst, dev):
            j = i * (N_DEV - 1) + q
            return pltpu.make_async_remote_copy(src_ref=src, dst_ref=dst, send_sem=send_sems.at[j],
                                                recv_sem=recv_sems.at[j], device_id=dev,
                                                device_id_type=pl.DeviceIdType.MESH)
        if group == 'local':
            out.append(pltpu.make_async_copy(s, d.at[me], local_sems.at[i]))
        elif group == 'own':
            out.append(mk(0, s, d.at[me], sib))
        elif group == 'in_d2d':
            out.append(mk(0, s, d.at[sib_lin], sib))
        for jj, k in enumerate(_ICI_PEERS):
            peer, plin = _peer(k)
            plin_other = _peer(k + 1)[1]
            if group == 'own':
                out.append(mk(1 + jj, s, d.at[me], peer))
            elif group == 'in_ici':
                out.append(mk(1 + jj, s, d.at[plin], peer))
            elif group == 'pass_on':
                out.append(mk(4 + jj, d.at[plin], d.at[plin], sib))
            elif group == 'in_d2d':
                out.append(mk(4 + jj, d.at[plin_other], d.at[plin_other], sib))
    return out


def _gather_start(srcs, dsts, sems):
    for cp in _gather_copies(srcs, dsts, sems, 'local') + _gather_copies(srcs, dsts, sems, 'own'):
        cp.start()


def _gather_pass_on(srcs, dsts, sems):
    for cp in _gather_copies(srcs, dsts, sems, 'in_ici'):
        cp.wait_recv()
    for cp in _gather_copies(srcs, dsts, sems, 'pass_on'):
        cp.start()


def _gather_finish(srcs, dsts, sems):
    for cp in _gather_copies(srcs, dsts, sems, 'in_d2d'):
        cp.wait_recv()
    for cp in _gather_copies(srcs, dsts, sems, 'own') + _gather_copies(srcs, dsts, sems, 'pass_on'):
        cp.wait_send()
    for cp in _gather_copies(srcs, dsts, sems, 'local'):
        cp.wait()


def _xchg_out_shapes(srcs, scatter):
    return [jax.ShapeDtypeStruct(s.shape if scatter else (N_DEV,) + s.shape, s.dtype) for s in srcs]


_ANY = pl.BlockSpec(memory_space=pl.ANY)


def _exchange(name, srcs, scatter):
    n = len(srcs)

    def body(*refs):
        s, d, sems = refs[:n], refs[n:2 * n], refs[2 * n:]
        if scatter:
            _scatter_start(s, d, sems)
            _scatter_wait(s, d, sems)
        else:
            _gather_start(s, d, sems)
            _gather_pass_on(s, d, sems)
            _gather_finish(s, d, sems)

    return pl.pallas_call(body, name=name, in_specs=[_ANY] * n, out_specs=[_ANY] * n,
                          out_shape=_xchg_out_shapes(srcs, scatter), scratch_shapes=_xchg_sems(n))(*srcs)


_MM_DIMS = {'nn': (((1,), (0,)), ((), ())), 'nt': (((1,), (1,)), ((), ())), 'tn': (((0,), (0,)), ((), ()))}


def _cmm_raw(x, y, kind, split):
    dot = functools.partial(lax.dot_general, dimension_numbers=_MM_DIMS[kind], preferred_element_type=F32)
    xh, yh = x.astype(BF16), y.astype(BF16)
    out = dot(xh, yh)
    if split:
        xl = (x - xh.astype(F32)).astype(BF16)
        yl = (y - yh.astype(F32)).astype(BF16)
        out = out + (dot(xh, yl) + dot(xl, yh))
    return out


@functools.partial(jax.custom_vjp, nondiff_argnums=(2, 3))
def _cmm(x, y, kind, split=False):
    return _cmm_raw(x, y, kind, split)


def _cmm_fwd(x, y, kind, split):
    return _cmm_raw(x, y, kind, split), (x, y)


def _cmm_bwd(kind, split, res, g):
    x, y = res
    if kind == 'nn':
        return _cmm_raw(g, y, 'nt', split), _cmm_raw(x, g, 'tn', split)
    if kind == 'nt':
        return _cmm_raw(g, y, 'nn', split), _cmm_raw(g, x, 'tn', split)
    return _cmm_raw(y, g, 'nt', split), _cmm_raw(x, g, 'nn', split)


_cmm.defvjp(_cmm_fwd, _cmm_bwd)


def _tri_sum_raw(tri, x, kind):
    dot = functools.partial(lax.dot_general, dimension_numbers=_MM_DIMS[kind], preferred_element_type=F32)
    tb = tri.astype(BF16)
    hi, mid, lo = _split3(x)
    return (dot(tb, hi) + dot(tb, mid)) + dot(tb, lo)


@functools.partial(jax.custom_vjp, nondiff_argnums=(2,))
def _tri_sum(tri, x, kind):
    return _tri_sum_raw(tri, x, kind)


def _tri_sum_fwd(tri, x, kind):
    return _tri_sum_raw(tri, x, kind), tri


def _tri_sum_bwd(kind, tri, g):
    return jnp.zeros_like(tri), _tri_sum_raw(tri, g, 'tn' if kind == 'nn' else 'nn')


_tri_sum.defvjp(_tri_sum_fwd, _tri_sum_bwd)


def _chunk_fn(S0, r, lw, k, v, a, b):
    hs = range(len(r))
    C = r[0].shape[0]
    ii = lax.broadcasted_iota(jnp.int32, (C, C), 0)
    jj = lax.broadcasted_iota(jnp.int32, (C, C), 1)
    incl, strict = ii >= jj, ii > jj
    eye = (ii == jj).astype(F32)
    inclf = incl.astype(F32)
    cum = [_tri_sum(inclf, lw[h], 'nn') for h in hs]
    e_inv = [jnp.exp(-cum[h]) for h in hs]
    At = [a[h] * jnp.exp(cum[h] - lw[h]) for h in hs]
    Rt = [r[h] * jnp.exp(cum[h]) for h in hs]
    Kh = [k[h] * e_inv[h] for h in hs]
    Bh = [b[h] * e_inv[h] for h in hs]
    Mab = [jnp.where(strict, _cmm(At[h], Bh[h], 'nt'), 0.0) for h in hs]
    Mak = [jnp.where(strict, _cmm(At[h], Kh[h], 'nt'), 0.0) for h in hs]
    Mrk = [jnp.where(incl, _cmm(Rt[h], Kh[h], 'nt'), 0.0) for h in hs]
    Mrb = [jnp.where(incl, _cmm(Rt[h], Bh[h], 'nt'), 0.0) for h in hs]
    rhs = [_cmm(At[h], S0[h], 'nt') + _cmm(Mak[h], v[h], 'nn') for h in hs]
    P = Mab
    Tm = [eye + P[h] for h in hs]
    n = 1
    while 2 * n < C:
        P = [_cmm(P[h], P[h], 'nn', True) for h in hs]
        Tm = [_cmm(Tm[h], eye + P[h], 'nn', True) for h in hs]
        n *= 2
    U = [_cmm(Tm[h], rhs[h], 'nn', True) for h in hs]
    Y = [_cmm(Rt[h], S0[h], 'nt') + _cmm(Mrk[h], v[h], 'nn') + _cmm(Mrb[h], U[h], 'nn') for h in hs]
    gC = [jnp.exp(jnp.sum(lw[h], axis=0, keepdims=True)) for h in hs]
    SC = [S0[h] * gC[h] + _cmm(v[h], Kh[h] * gC[h], 'tn') + _cmm(U[h], Bh[h] * gC[h], 'tn') for h in hs]
    return tuple(Y), tuple(SC)


def _cscan_fwd(r, lw, k, v, a, b, xs):
    n_x = len(xs)
    tn = r.shape[0]
    H, Dh = RWKV_HEADS, RWKV_HEAD_DIM
    nc = tn // SCAN_CHUNK
    lanes = lambda h: slice(h * Dh, (h + 1) * Dh)
    heads = lambda ref: tuple(ref[:, lanes(h)] for h in range(H))
    mats = lambda ref: tuple(ref[h] for h in range(H))

    def body(r_ref, lw_ref, k_ref, v_ref, a_ref, b_ref, *rest):
        x_src, (y_ref, ck_ref) = rest[:n_x], rest[n_x:n_x + 2]
        x_dst, s_ref, sems = rest[n_x + 2:2 * n_x + 2], rest[2 * n_x + 2], rest[2 * n_x + 3:]

        @pl.when(pl.program_id(0) == 0)
        def _():
            s_ref[...] = jnp.zeros(s_ref.shape, F32)
            _gather_start(x_src, x_dst, sems)

        ck_ref[0] = s_ref[...]
        y, sc = _chunk_fn(mats(s_ref), heads(r_ref), heads(lw_ref), heads(k_ref), heads(v_ref), heads(a_ref),
                          heads(b_ref))
        for h in range(H):
            y_ref[:, lanes(h)] = y[h]
            s_ref[h] = sc[h]

        @pl.when(pl.program_id(0) == max(nc - 4, 0))
        def _():
            _gather_pass_on(x_src, x_dst, sems)

        @pl.when(pl.program_id(0) == nc - 1)
        def _():
            _gather_finish(x_src, x_dst, sems)

    hm = pl.BlockSpec((SCAN_CHUNK, H * Dh), lambda c: (c, 0))
    res = pl.pallas_call(
        body, name="rwkv_scan_fwd", grid=(nc,), in_specs=[hm] * 6 + [_ANY] * n_x,
        out_specs=[hm, pl.BlockSpec((1, H, Dh, Dh), lambda c: (c, 0, 0, 0))] + [_ANY] * n_x,
        out_shape=[jax.ShapeDtypeStruct((tn, H * Dh), F32), jax.ShapeDtypeStruct((nc, H, Dh, Dh), F32)]
        + _xchg_out_shapes(xs, False),
        scratch_shapes=[pltpu.VMEM((H, Dh, Dh), F32)] + _xchg_sems(n_x),
        compiler_params=_cparams(1))(r, lw, k, v, a, b, *xs)
    return res[0], res[1], res[2:]


def _cscan_bwd(r, lw, k, v, a, b, dy, ck, xs):
    n_x = len(xs)
    tn = r.shape[0]
    H, Dh = RWKV_HEADS, RWKV_HEAD_DIM
    nc = tn // SCAN_CHUNK
    lanes = lambda h: slice(h * Dh, (h + 1) * Dh)
    heads = lambda ref: tuple(ref[:, lanes(h)] for h in range(H))
    mats = lambda ref: tuple(ref[h] for h in range(H))

    def body(r_ref, lw_ref, k_ref, v_ref, a_ref, b_ref, dy_ref, ck_ref, *rest):
        x_src = rest[:n_x]
        d_refs = rest[n_x:n_x + 6]
        x_dst = rest[n_x + 6:2 * n_x + 6]
        g_ref = rest[2 * n_x + 6]
        sems = rest[2 * n_x + 7:]

        @pl.when(pl.program_id(0) == 0)
        def _():
            g_ref[...] = jnp.zeros(g_ref.shape, F32)
            _scatter_start(x_src, x_dst, sems)

        s0 = tuple(ck_ref[0, h] for h in range(H))
        _, vjp = jax.vjp(_chunk_fn, s0, heads(r_ref), heads(lw_ref), heads(k_ref), heads(v_ref), heads(a_ref),
                         heads(b_ref))
        grads = vjp((heads(dy_ref), mats(g_ref)))
        for h in range(H):
            g_ref[h] = grads[0][h]
            for d_ref, gz in zip(d_refs, grads[1:]):
                d_ref[:, lanes(h)] = gz[h]

        @pl.when(pl.program_id(0) == nc - 1)
        def _():
            _scatter_wait(x_src, x_dst, sems)

    hm = pl.BlockSpec((SCAN_CHUNK, H * Dh), lambda c: (nc - 1 - c, 0))
    hshape = jax.ShapeDtypeStruct((tn, H * Dh), F32)
    res = pl.pallas_call(
        body, name="rwkv_scan_bwd", grid=(nc,),
        in_specs=[hm] * 7 + [pl.BlockSpec((1, H, Dh, Dh), lambda c: (nc - 1 - c, 0, 0, 0))] + [_ANY] * n_x,
        out_specs=[hm] * 6 + [_ANY] * n_x, out_shape=[hshape] * 6 + _xchg_out_shapes(xs, True),
        scratch_shapes=[pltpu.VMEM((H, Dh, Dh), F32)] + _xchg_sems(n_x),
        compiler_params=_cparams(1))(r, lw, k, v, a, b, dy, ck, *xs)
    return res[:6], res[6:]


def _decay_mask(lg, i, j, blk):
    rows = lax.broadcasted_iota(jnp.int32, (blk, blk), 0)
    cols = lax.broadcasted_iota(jnp.int32, (blk, blk), 1)
    dd = (rows - cols + (i - j) * blk).astype(F32)
    return jnp.where(dd >= 0.0, jnp.exp(lg * jnp.maximum(dd, 0.0)), 0.0)


_NT = (((1,), (1,)), ((), ()))
_TN = (((0,), (0,)), ((), ()))


def _ret_attn_fwd(lg, q, k, v, blk=ATT_BLOCK):
    tn = q.shape[0]
    Dh = RET_HEAD_DIM

    def body(lg_ref, q_ref, k_ref, v_ref, o_ref):
        i = pl.program_id(1)
        lgv = lg_ref[0][:, 0:1]
        qb = q_ref[...].astype(BF16)

        def jb(j, acc):
            ks = pl.ds(pl.multiple_of(j * blk, blk), blk)
            s = lax.dot_general(qb, k_ref[ks, :].astype(BF16), _NT, preferred_element_type=F32)
            s = s * _decay_mask(lgv, i, j, blk)
            return acc + jnp.dot(s.astype(BF16), v_ref[ks, :].astype(BF16), preferred_element_type=F32)

        o_ref[...] = lax.fori_loop(0, i + 1, jb, jnp.zeros((blk, Dh), F32))

    full = pl.BlockSpec((tn, Dh), lambda h, i: (0, h))
    qs = pl.BlockSpec((blk, Dh), lambda h, i: (i, h))
    return pl.pallas_call(
        body, name="ret_attn_fwd", grid=(RET_HEADS, tn // blk),
        in_specs=[pl.BlockSpec((1, 1, 128), lambda h, i: (h, 0, 0)), qs, full, full],
        out_specs=qs, out_shape=jax.ShapeDtypeStruct(q.shape, F32), compiler_params=_cparams(2))(lg, q, k, v)


def _ret_attn_bwd(lg, q, k, v, do, blk=ATT_BLOCK):
    tn = q.shape[0]
    nb = tn // blk
    Dh = RET_HEAD_DIM

    def body(lg_ref, q_ref, k_ref, v_ref, do_ref, dq_ref, dk_ref, dv_ref):
        lgv = lg_ref[0][:, 0:1]
        dk_ref[...] = jnp.zeros(dk_ref.shape, F32)
        dv_ref[...] = jnp.zeros(dv_ref.shape, F32)

        def ib(i, carry):
            qs = pl.ds(pl.multiple_of(i * blk, blk), blk)
            qb = q_ref[qs, :].astype(BF16)
            dob = do_ref[qs, :].astype(BF16)

            def jb(j, dq):
                ks = pl.ds(pl.multiple_of(j * blk, blk), blk)
                kb = k_ref[ks, :].astype(BF16)
                vb = v_ref[ks, :].astype(BF16)
                dm = _decay_mask(lgv, i, j, blk)
                s = lax.dot_general(qb, kb, _NT, preferred_element_type=F32) * dm
                ds = lax.dot_general(dob, vb, _NT, preferred_element_type=F32) * dm
                sb, dsb = s.astype(BF16), ds.astype(BF16)
                dv_ref[ks, :] += lax.dot_general(sb, dob, _TN, preferred_element_type=F32)
                dk_ref[ks, :] += lax.dot_general(dsb, qb, _TN, preferred_element_type=F32)
                return dq + jnp.dot(dsb, kb, preferred_element_type=F32)

            dq_ref[qs, :] = lax.fori_loop(0, i + 1, jb, jnp.zeros((blk, Dh), F32))
            return carry

        lax.fori_loop(0, nb, ib, 0)

    full = pl.BlockSpec((tn, Dh), lambda h: (0, h))
    sh = jax.ShapeDtypeStruct(q.shape, F32)
    return pl.pallas_call(
        body, name="ret_attn_bwd", grid=(RET_HEADS,),
        in_specs=[pl.BlockSpec((1, 1, 128), lambda h: (h, 0, 0)), full, full, full, full],
        out_specs=[full, full, full], out_shape=[sh, sh, sh], compiler_params=_cparams(1))(lg, q, k, v, do)


def _next8_spec(a, tb):
    r = tb // 8
    last = a.shape[0] // 8 - 1
    return pl.BlockSpec((8, a.shape[1]), lambda i: (jnp.minimum((i + 1) * r, last), 0))


def _conv_taps(g_ext, cw_ref, cb_ref):
    return (cw_ref[2:3, :] * g_ext + cw_ref[1:2, :] * pltpu.roll(g_ext, 1, 0)
            + cw_ref[0:1, :] * pltpu.roll(g_ext, 2, 0) + cb_ref[...])


def _glu_fwd(gate, up, cw, cb, tb=TOK_BLOCK):
    tn = gate.shape[0]

    def body(g_ref, gh_ref, u_ref, cw_ref, cb_ref, o_ref):
        halo = jnp.where(pl.program_id(0) == 0, 0.0, gh_ref[...])
        g_ext = jnp.concatenate([halo, g_ref[...]], axis=0)
        gc = _conv_taps(g_ext, cw_ref, cb_ref)[8:, :]
        o_ref[...] = (gc * _sigmoid(gc) * u_ref[...]).astype(o_ref.dtype)

    return pl.pallas_call(
        body, name="glu_fwd", grid=(tn // tb,),
        in_specs=[_blk_spec(gate, tb), _prev8_spec(gate, tb), _blk_spec(up, tb), _full_spec(cw), _full_spec(cb)],
        out_specs=_blk_spec(gate, tb), out_shape=jax.ShapeDtypeStruct(gate.shape, BF16),
        compiler_params=_cparams(1))(gate, gate, up, cw, cb)


def _glu_bwd(gate, up, dact, cw, cb, tb=TOK_BLOCK):
    tn = gate.shape[0]
    nb = tn // tb

    def body(g_ref, gp_ref, gn_ref, u_ref, un_ref, d_ref, dn_ref, cw_ref, cb_ref, dg_ref, du_ref, dcw_ref, dcb_ref):
        i = pl.program_id(0)
        gprev = jnp.where(i == 0, 0.0, gp_ref[...])
        dnext = jnp.where(i == nb - 1, 0.0, dn_ref[...])
        g_ext = jnp.concatenate([gprev, g_ref[...], gn_ref[...]], axis=0)
        gc = _conv_taps(g_ext, cw_ref, cb_ref)[8:, :]
        u_e = jnp.concatenate([u_ref[...], un_ref[...]], axis=0)
        d_e = jnp.concatenate([d_ref[...], dnext], axis=0)
        s = _sigmoid(gc)
        dgc = d_e * u_e * (s * (1.0 + gc * (1.0 - s)))
        du_ref[...] = (d_ref[...] * (gc * s)[:tb, :]).astype(du_ref.dtype)
        n_e = tb + 8
        dg_ref[...] = (cw_ref[2:3, :] * dgc + cw_ref[1:2, :] * pltpu.roll(dgc, n_e - 1, 0)
                       + cw_ref[0:1, :] * pltpu.roll(dgc, n_e - 2, 0))[:tb, :].astype(dg_ref.dtype)

        @pl.when(i == 0)
        def _():
            dcw_ref[...] = jnp.zeros(dcw_ref.shape, F32)
            dcb_ref[...] = jnp.zeros(dcb_ref.shape, F32)

        dgc_b = dgc[:tb, :]
        g0 = g_ext[8:8 + tb, :]
        g1 = pltpu.roll(g_ext, 1, 0)[8:8 + tb, :]
        g2 = pltpu.roll(g_ext, 2, 0)[8:8 + tb, :]
        dcw_ref[2:3, :] += jnp.sum(dgc_b * g0, axis=0, keepdims=True)
        dcw_ref[1:2, :] += jnp.sum(dgc_b * g1, axis=0, keepdims=True)
        dcw_ref[0:1, :] += jnp.sum(dgc_b * g2, axis=0, keepdims=True)
        dcb_ref[...] += jnp.sum(dgc_b, axis=0, keepdims=True)

    sh = jax.ShapeDtypeStruct(gate.shape, BF16)
    return pl.pallas_call(
        body, name="glu_bwd", grid=(nb,),
        in_specs=[_blk_spec(gate, tb), _prev8_spec(gate, tb), _next8_spec(gate, tb), _blk_spec(up, tb),
                  _next8_spec(up, tb), _blk_spec(dact, tb), _next8_spec(dact, tb), _full_spec(cw), _full_spec(cb)],
        out_specs=[_blk_spec(gate, tb), _blk_spec(gate, tb), _full_spec(cw), _full_spec(cb)],
        out_shape=[sh, sh, jax.ShapeDtypeStruct(cw.shape, F32), jax.ShapeDtypeStruct(cb.shape, F32)],
        compiler_params=_cparams(1))(gate, gate, gate, up, up, dact, dact, cw, cb)


def _final_loss(x2, tgt, g, tb=TOK_BLOCK):
    tn, dm = x2.shape

    def body(x_ref, t_ref, g_ref, l_ref, dx_ref, dg_ref):
        y, vjp = jax.vjp(_rms_fn, x_ref[...], g_ref[...])
        err = y - t_ref[...]
        dx, dg = vjp(err * (1.0 / dm))

        @pl.when(pl.program_id(0) == 0)
        def _():
            l_ref[...] = jnp.zeros(l_ref.shape, F32)
            dg_ref[...] = jnp.zeros(dg_ref.shape, F32)

        part = 0.5 * jnp.sum(jnp.mean(err * err, axis=-1, keepdims=True), axis=0, keepdims=True)
        l_ref[...] += jnp.broadcast_to(part, l_ref.shape)
        dx_ref[...] = dx
        dg_ref[...] += dg

    return pl.pallas_call(
        body, name="final_loss", grid=(tn // tb,),
        in_specs=[_blk_spec(x2, tb), _blk_spec(tgt, tb), _full_spec(g)],
        out_specs=[pl.BlockSpec((8, 128), lambda i: (0, 0)), _blk_spec(x2, tb), _full_spec(g)],
        out_shape=[jax.ShapeDtypeStruct((8, 128), F32), jax.ShapeDtypeStruct(x2.shape, F32),
                   jax.ShapeDtypeStruct(g.shape, F32)],
        compiler_params=_cparams(1))(x2, tgt, g)


def _pad_cols(w, n):
    return jnp.pad(w, ((0, 0), (0, n - w.shape[1])))


def _pad_rows(w, n):
    return jnp.pad(w, ((0, n - w.shape[0]), (0, 0)))


def _local_step(x, tgt, W, late):
    tn = x.shape[0]
    Wd = RWKV_WIDTH
    row = lambda z: z.reshape(1, -1)
    g_mix, g_ffn, g_fin = row(W['norm_mix_g']), row(W['norm_ffn_g']), row(W['norm_final_g'])

    (h1,) = _tok_fwd("norm_mix_fwd", lambda a, g: (_rms_fn(a, g),), [x], [g_mix], [(D_MODEL,)])
    proj = _mm("proj_fwd", h1, W['w_in_t'], tb=True)
    p_rkv = proj[:, :3 * Wd]
    pre_consts = [row(W['rwkv_mu_w']), row(W['rwkv_mu_a']), row(W['rwkv_mu_g']), row(W['rwkv_mu_r']),
                  row(W['rwkv_mu_k']), row(W['rwkv_mu_v']), row(W['rwkv_w0']),
                  _pad_cols(W['rwkv_w1'], LORA_PAD), _pad_rows(W['rwkv_w2'], LORA_PAD), row(W['rwkv_a0']),
                  _pad_cols(W['rwkv_a1'], LORA_PAD), _pad_rows(W['rwkv_a2'], LORA_PAD),
                  W['rwkv_g1'], W['rwkv_g2']]
    r, k0, v, lw, a, g = _pre_a_fwd(h1, p_rkv, pre_consts)
    k_k, k_a = row(W['rwkv_k_k']), row(W['rwkv_k_a'])
    nkk, k, b = _tok_fwd("rwkv_pre_b_fwd", _pre_b_fn, [k0, a], [k_k, k_a], [(Wd,)] * 3)
    y_scan, ck, gathered = _cscan_fwd(r, lw, k, v, nkk, b, late)
    w_out, w_gate_t, w_up_t, w_down = [g_.reshape(-1, D_MODEL) for g_ in gathered]
    post_consts = [row(W['rwkv_lnx_w']), row(W['rwkv_lnx_b']), row(W['rwkv_r_k'])]
    (y_rwkv,) = _tok_fwd("rwkv_post_fwd", _rwkv_post_fn, [y_scan, r, k, v, g], post_consts, [(Wd,)],
                         out_dtypes=[BF16])

    pos = jnp.arange(tn, dtype=F32)
    half = RET_HEAD_DIM // 2
    inv_freq = ROPE_BASE ** (-jnp.arange(half, dtype=F32) / half)
    ang = pos[:, None] * inv_freq[None, :]
    cos2 = jnp.concatenate([jnp.cos(ang), jnp.cos(ang)], axis=1)
    sin2 = jnp.concatenate([-jnp.sin(ang), jnp.sin(ang)], axis=1)
    lg = jnp.log(1.0 - 2.0 ** (-5.0 - jnp.arange(RET_HEADS, dtype=F32)))
    lg = jnp.broadcast_to(lg[:, None, None], (RET_HEADS, 1, 128))
    q_p, k_p = proj[:, 3 * Wd:4 * Wd], proj[:, 4 * Wd:5 * Wd]
    v_ret, g_ret = proj[:, 5 * Wd:6 * Wd], proj[:, 6 * Wd:7 * Wd]
    q_rot, k_rot = _tok_fwd("ret_rotary_fwd", _rotary_fn, [cos2, sin2, q_p, k_p], [], [(RET_WIDTH,)] * 2)
    y_ret_raw = _ret_attn_fwd(lg, q_rot, k_rot, v_ret)
    gn_w = row(W['ret_gn_w'])
    (y_ret,) = _tok_fwd("ret_post_fwd", _ret_post_fn, [y_ret_raw, g_ret], [gn_w], [(RET_WIDTH,)],
                        out_dtypes=[BF16])

    ycat = jnp.concatenate([y_rwkv, y_ret], axis=1)
    x1 = _mm("out_proj_fwd", ycat, w_out, add=x)
    (h2,) = _tok_fwd("norm_ffn_fwd", lambda a_, g_: (_rms_fn(a_, g_),), [x1], [g_ffn], [(D_MODEL,)],
                     out_dtypes=[BF16])
    gate = _mm("ffn_gate_fwd", h2, w_gate_t, tb=True)
    up = _mm("ffn_up_fwd", h2, w_up_t, tb=True)
    cw = W['ffn_conv_w']
    cb = row(W['ffn_conv_b'])
    act = _glu_fwd(gate, up, cw, cb)
    x2 = _mm("ffn_down_fwd", act, w_down, add=x1)
    loss8, dx2, dg_fin = _final_loss(x2, tgt, g_fin)

    G = {'norm_final_g': dg_fin}
    dact = _mm("ffn_down_dx", dx2, w_down, tb=True)
    d_down = _mm("ffn_down_dw", act, dx2, ta=True, out_dtype=BF16)
    dgate, dup, dcw, dcb = _glu_bwd(gate, up, dact, cw, cb)
    G['ffn_conv_w'], G['ffn_conv_b'] = dcw, dcb
    dh2 = _mm("ffn_gate_dx", dgate, w_gate_t)
    dh2 = _mm("ffn_up_dx", dup, w_up_t, add=dh2)
    d_gate_t = _mm("ffn_gate_dw", dgate, h2, ta=True, out_dtype=BF16)
    d_up_t = _mm("ffn_up_dw", dup, h2, ta=True, out_dtype=BF16)
    dx1, G['norm_ffn_g'] = _tok_bwd("norm_ffn_bwd", lambda a_, g_: (_rms_fn(a_, g_),), [], [x1], [g_ffn], [dh2], add=dx2)
    dycat = _mm("out_proj_dx", dx1, w_out, tb=True)
    d_out = _mm("out_proj_dw", ycat, dx1, ta=True, out_dtype=BF16)
    late_grads = [z.reshape(N_DEV, -1, D_MODEL) for z in (d_out, d_gate_t, d_up_t, d_down)]
    dy_rwkv, dy_ret = dycat[:, :Wd], dycat[:, Wd:]

    dyr_raw, dg_ret, G['ret_gn_w'] = _tok_bwd("ret_post_bwd", _ret_post_fn, [], [y_ret_raw, g_ret], [gn_w], [dy_ret],
                                              tok_dtypes=[F32, BF16])
    dq_rot, dk_rot, dv_ret = _ret_attn_bwd(lg, q_rot, k_rot, v_ret, dyr_raw)
    dq_p, dk_p = _tok_bwd("ret_rotary_bwd", _rotary_fn, [cos2, sin2], [q_p, k_p], [], [dq_rot, dk_rot],
                          tok_dtypes=[BF16, BF16])

    dy_scan, dr1, dk1, dv1, dg, G['rwkv_lnx_w'], G['rwkv_lnx_b'], G['rwkv_r_k'] = _tok_bwd(
        "rwkv_post_bwd", _rwkv_post_fn, [], [y_scan, r, k, v, g], post_consts, [dy_rwkv])
    (dr2, dlw, dk2, dv2, dnkk, db), late_parts = _cscan_bwd(r, lw, k, v, nkk, b, dy_scan, ck, late_grads)
    dk0, da, G['rwkv_k_k'], G['rwkv_k_a'] = _tok_bwd(
        "rwkv_pre_b_bwd", _pre_b_fn, [], [k0, a], [k_k, k_a], [dnkk, (dk1, dk2), db])
    pre_cts = [(dr1, dr2), dk0, (dv1, dv2), dlw, da, dg]
    pre_out = _pre_a_bwd(h1, p_rkv, pre_consts, pre_cts)
    dh1_a, dp_rkv = pre_out[0], pre_out[1]
    (G['rwkv_mu_w'], G['rwkv_mu_a'], G['rwkv_mu_g'], G['rwkv_mu_r'], G['rwkv_mu_k'], G['rwkv_mu_v'], G['rwkv_w0'],
     dw1, dw2, G['rwkv_a0'], da1, da2, G['rwkv_g1'], G['rwkv_g2']) = pre_out[2:]
    G['rwkv_w1'], G['rwkv_w2'] = dw1[:, :64], dw2[:64, :]
    G['rwkv_a1'], G['rwkv_a2'] = da1[:, :64], da2[:64, :]

    dproj = jnp.concatenate([dp_rkv, dq_p, dk_p, dv_ret.astype(BF16), dg_ret], axis=1)
    dh1 = _mm("proj_dx", dproj, W['w_in_t'], add=dh1_a)
    G['w_in_t'] = _mm("proj_dw", dproj, h1, ta=True, out_dtype=BF16)
    dx, G['norm_mix_g'] = _tok_bwd("norm_mix_bwd", lambda a_, g_: (_rms_fn(a_, g_),), [], [x], [g_mix], [dh1], add=dx1)
    return loss8[0, 0], dx, G, late_parts


def _adamw(name, parts, w, m, v):
    rows, cols = w.shape
    sub = 8 * 4 // parts.dtype.itemsize
    tb = max(t for t in range(sub, 65, sub) if rows % t == 0) if rows > 64 else rows
    c1 = 1.0 - ADAM_B1 ** ADAM_STEP
    c2 = 1.0 - ADAM_B2 ** ADAM_STEP

    def body(p_ref, w_ref, m_ref, v_ref, g_ref, d_ref, nm_ref, nv_ref):
        g = p_ref[0].astype(F32)
        for d in range(1, N_DEV):
            g = g + p_ref[d].astype(F32)
        mn = ADAM_B1 * m_ref[...] + (1.0 - ADAM_B1) * g
        vn = ADAM_B2 * v_ref[...] + (1.0 - ADAM_B2) * (g * g)
        m_hat = mn / c1
        v_hat = vn / c2
        g_ref[...] = g
        d_ref[...] = -ADAM_LR * (m_hat / (jnp.sqrt(v_hat) + ADAM_EPS) + ADAM_WD * w_ref[...])
        nm_ref[...] = mn
        nv_ref[...] = vn

    spec = pl.BlockSpec((tb, cols), lambda i: (i, 0))
    sh = jax.ShapeDtypeStruct((rows, cols), F32)
    return pl.pallas_call(
        body, name=name, grid=(rows // tb,),
        in_specs=[pl.BlockSpec((N_DEV, tb, cols), lambda i: (0, i, 0)), spec, spec, spec],
        out_specs=[spec] * 4, out_shape=[sh] * 4, compiler_params=_cparams(1))(parts, w, m, v)


def _local_shape(name):
    gs, ax = SHARDED[name]
    ls = list(gs)
    ls[ax] //= N_DEV
    return tuple(ls)


def _seg(flat, seg):
    n = flat.shape[-1]
    pad = _round_up(n, seg) - n
    if pad:
        flat = jnp.pad(flat, [(0, 0)] * (flat.ndim - 1) + [(0, pad)])
    return flat


def _split3(w):
    hi = w.astype(BF16)
    r1 = w - hi.astype(F32)
    mid = r1.astype(BF16)
    lo = (r1 - mid.astype(F32)).astype(BF16)
    return hi, mid, lo


def _pack_small_shards(shards):
    pieces = []
    for name in SMALL_NAMES:
        flat = shards[name].reshape(-1)
        if name == 'ffn_conv_w':
            pieces += [_seg(p, BF16_SEG) for p in _split3(flat)]
        else:
            pieces.append(flat.astype(BF16))
    return jnp.concatenate(pieces).reshape(-1, 128)


def _unpack_small(gathered):
    flat = gathered.reshape(N_DEV, -1)
    out, off = {}, 0
    for name in SMALL_NAMES:
        gs, ax = SHARDED[name]
        ls = _local_shape(name)
        n = int(np.prod(ls))
        if name == 'ffn_conv_w':
            nseg = _round_up(n, BF16_SEG)
            hi, mid, lo = (flat[:, off + j * nseg: off + j * nseg + n].astype(F32) for j in range(3))
            sh = ((hi + mid) + lo).reshape(N_DEV, 3, -1)
            out[name] = jnp.swapaxes(sh, 0, 1).reshape(3, D_FF)
            off += 3 * nseg
        else:
            sh = flat[:, off:off + n].reshape((N_DEV,) + ls[1:])
            out[name] = sh.reshape(gs[1:]) if ax == 1 else jnp.swapaxes(sh, 0, 1).reshape(gs[1:])
            off += n
    return out


def _small_pieces(sharded, repl):
    return [sharded[n].reshape(-1) for n in SMALL_NAMES] + [repl[n].reshape(-1) for n in REPL_NAMES]


def _pack_small_local(d):
    flat = jnp.concatenate(_small_pieces(d, d))
    return _seg(flat, F32_SEG).reshape(-1, 128)


def _pack_small_grads(G):
    pieces = []
    for name in SMALL_NAMES:
        gs, ax = SHARDED[name]
        g = G[name]
        if name == 'ffn_conv_w':
            sh = jnp.swapaxes(g.reshape(3, N_DEV, -1), 0, 1)
        elif ax == 1:
            sh = g
        else:
            sh = jnp.swapaxes(g.reshape(g.shape[0], N_DEV, -1), 0, 1)
        pieces.append(sh.reshape(N_DEV, -1))
    rep = jnp.concatenate([G[n].reshape(-1) for n in REPL_NAMES])
    pieces.append(jnp.broadcast_to(rep[None, :], (N_DEV, rep.shape[0])))
    flat = _seg(jnp.concatenate(pieces, axis=1), F32_SEG)
    return flat.reshape(N_DEV, -1, 128)


def _unpack_small_local(packed, local_shapes):
    flat = packed.reshape(-1)
    out, off = {}, 0
    for name in SMALL_NAMES + REPL_NAMES:
        n = int(np.prod(local_shapes[name]))
        out[name] = flat[off:off + n].reshape(local_shapes[name])
        off += n
    return out


def kernel(x, *rest):
    nw = len(WEIGHT_NAMES)
    assert len(rest) == 3 * nw + 1
    weights = dict(zip(WEIGHT_NAMES, rest[:nw]))
    loss_target = rest[nw]
    moms = dict(zip(WEIGHT_NAMES, rest[nw + 1:2 * nw + 1]))
    vars_ = dict(zip(WEIGHT_NAMES, rest[2 * nw + 1:]))
    local_shapes = {n: weights[n].shape for n in WEIGHT_NAMES}

    def native2d(name, a):
        a2 = a.reshape(a.shape[-2], a.shape[-1])
        return a2.T if name in BIG_T else a2

    def from2d(name, a2):
        return (a2.T if name in BIG_T else a2).reshape(local_shapes[name])

    big_w = {n: native2d(n, weights[n]) for n in BIG_NAMES}
    w_in_t_sh = big_w['w_in'].astype(BF16)
    late = [big_w[n].astype(BF16) for n in LATE_NAMES]
    small_sh = _pack_small_shards({n: weights[n] for n in SMALL_NAMES})
    w_in_g, small_g = _exchange("weights_all_gather", [w_in_t_sh, small_sh], False)
    W = _unpack_small(small_g)
    W['w_in_t'] = w_in_g.reshape(-1, D_MODEL)
    for n in REPL_NAMES:
        W[n] = weights[n][0] if n != 'norm_final_g' else weights[n]

    loss, dx, G, late_parts = _local_step(x[0], loss_target[0], W, late)

    w_in_parts, small_parts = _exchange(
        "grads_all_to_all", [G['w_in_t'].reshape(N_DEV, -1, D_MODEL), _pack_small_grads(G)], True)
    results = {}
    for n, parts in zip(['w_in'] + LATE_NAMES, [w_in_parts] + list(late_parts)):
        res = _adamw("adamw_" + n, parts, big_w[n], native2d(n, moms[n]), native2d(n, vars_[n]))
        results[n] = [from2d(n, r) for r in res]
    small_res = _adamw("adamw_small", small_parts, _pack_small_local(weights), _pack_small_local(moms),
                       _pack_small_local(vars_))
    small_out = [_unpack_small_local(p, local_shapes) for p in small_res]

    loss = lax.psum(loss, ("x", "y", "c"))
    outs = [loss, dx[None]]
    for j in range(4):
        outs += [results[n][j] if n in results else small_out[j][n] for n in WEIGHT_NAMES]
    return tuple(outs)
```

```python
import functools
import math

import numpy as np
import jax
import jax.numpy as jnp
from jax import lax
from jax.experimental import pallas as pl
from jax.experimental.pallas import tpu as pltpu

F32 = jnp.float32
BF16 = jnp.bfloat16

N_DEV = 8
D_MODEL = 1024
RWKV_HEADS = 8
RWKV_HEAD_DIM = 64
RWKV_WIDTH = 512
RET_HEADS = 4
RET_HEAD_DIM = 128
RET_WIDTH = 512
LORA_PAD = 128
D_FF = 2816
NORM_EPS = 1e-6
RWKV_GN_EPS = 64e-5
RET_GN_EPS = 1e-5
ROPE_BASE = 10000.0
ADAM_LR, ADAM_B1, ADAM_B2, ADAM_EPS, ADAM_WD, ADAM_STEP = 0.001, 0.9, 0.999, 1e-08, 0.01, 10

VMEM_LIMIT = 56 * 1024 * 1024
TOK_BLOCK = 256
SCAN_CHUNK = 64
SCAN_SUB = 1
ATT_BLOCK = 512
BF16_SEG = 2048
F32_SEG = 1024

WEIGHT_NAMES = ['norm_mix_g', 'w_in', 'rwkv_mu_r', 'rwkv_mu_k', 'rwkv_mu_v', 'rwkv_mu_w', 'rwkv_mu_a',
                'rwkv_mu_g', 'rwkv_w0', 'rwkv_w1', 'rwkv_w2', 'rwkv_a0', 'rwkv_a1', 'rwkv_a2', 'rwkv_g1',
                'rwkv_g2', 'rwkv_k_k', 'rwkv_k_a', 'rwkv_r_k', 'rwkv_lnx_w', 'rwkv_lnx_b', 'ret_gn_w',
                'w_out', 'norm_ffn_g', 'ffn_w_gate', 'ffn_w_up', 'ffn_conv_w', 'ffn_conv_b', 'ffn_w_down',
                'norm_final_g']
SHARDED = {
    'w_in': ((1, 1024, 3584), 2), 'rwkv_w1': ((1, 1024, 64), 1), 'rwkv_w2': ((1, 64, 512), 2),
    'rwkv_a1': ((1, 1024, 64), 1), 'rwkv_a2': ((1, 64, 512), 2), 'rwkv_g1': ((1, 1024, 128), 1),
    'rwkv_g2': ((1, 128, 512), 2), 'w_out': ((1, 1024, 1024), 1), 'ffn_w_gate': ((1, 1024, 2816), 2),
    'ffn_w_up': ((1, 1024, 2816), 2), 'ffn_conv_w': ((1, 3, 1, 2816), 3), 'ffn_w_down': ((1, 2816, 1024), 1),
}
REPL_NAMES = [n for n in WEIGHT_NAMES if n not in SHARDED]
BIG_NAMES = ['w_in', 'w_out', 'ffn_w_gate', 'ffn_w_up', 'ffn_w_down']
BIG_T = ('w_in', 'ffn_w_gate', 'ffn_w_up')
LATE_NAMES = ['w_out', 'ffn_w_gate', 'ffn_w_up', 'ffn_w_down']
SMALL_NAMES = [n for n in WEIGHT_NAMES if n in SHARDED and n not in BIG_NAMES]


def _cparams(n_grid):
    return pltpu.CompilerParams(dimension_semantics=("arbitrary",) * n_grid, vmem_limit_bytes=VMEM_LIMIT)


def _round_up(n, m):
    return (n + m - 1) // m * m


@jax.custom_vjp
def _bdot(x, w):
    return jnp.dot(x.astype(BF16), w.astype(BF16), preferred_element_type=F32)


def _bdot_fwd(x, w):
    return _bdot(x, w), (x, w)


def _bdot_bwd(res, g):
    x, w = res
    gb = g.astype(BF16)
    dx = lax.dot_general(gb, w.astype(BF16), (((1,), (1,)), ((), ())), preferred_element_type=F32)
    dw = lax.dot_general(x.astype(BF16), gb, (((0,), (0,)), ((), ())), preferred_element_type=F32)
    return dx, dw.astype(w.dtype)


_bdot.defvjp(_bdot_fwd, _bdot_bwd)


@jax.custom_vjp
def _shift_rows(x, prev):
    rolled = pltpu.roll(x, 1, 0)
    row = lax.broadcasted_iota(jnp.int32, x.shape, 0)
    return jnp.where(row == 0, jnp.broadcast_to(prev, x.shape), rolled)


def _shift_rows_fwd(x, prev):
    return _shift_rows(x, prev), None


def _shift_rows_bwd(_, g):
    n = g.shape[0]
    rolled = pltpu.roll(g, n - 1, 0)
    row = lax.broadcasted_iota(jnp.int32, g.shape, 0)
    return jnp.where(row == n - 1, 0.0, rolled), g[0:1, :]


_shift_rows.defvjp(_shift_rows_fwd, _shift_rows_bwd)


@jax.custom_vjp
def _swap_halves(x):
    return pltpu.roll(x, 64, 1)


_swap_halves.defvjp(lambda x: (_swap_halves(x), None), lambda _, g: (pltpu.roll(g, 64, 1),))


def _sigmoid(x):
    return 1.0 / (1.0 + jnp.exp(-x))


def _softplus(x):
    return jnp.maximum(x, 0.0) + jnp.log(1.0 + jnp.exp(-jnp.abs(x)))


def _rms_fn(x, g):
    return x * lax.rsqrt(jnp.mean(x * x, axis=-1, keepdims=True) + NORM_EPS) * g


def _pre_a_fn(h1, h1p, p, pp, mu_w, mu_a, mu_g, mu_r, mu_k, mu_v, w0, w1, w2, a0, a1, a2, g1, g2):
    W = RWKV_WIDTH
    h1s = _shift_rows(h1, h1p)
    ps = _shift_rows(p, pp)
    dx = h1s - h1
    xw = h1 + dx * mu_w
    xa = h1 + dx * mu_a
    xg = h1 + dx * mu_g
    dp = ps - p
    r = p[:, 0:W] + dp[:, 0:W] * mu_r
    k0 = p[:, W:2 * W] + dp[:, W:2 * W] * mu_k
    v = p[:, 2 * W:3 * W] + dp[:, 2 * W:3 * W] * mu_v
    wl = w0 + _bdot(jnp.tanh(_bdot(xw, w1)), w2)
    w_log = -_softplus(-wl) - 0.5
    lw = -jnp.exp(w_log)
    a = _sigmoid(a0 + _bdot(_bdot(xa, a1), a2))
    g = _bdot(_sigmoid(_bdot(xg, g1)), g2)
    return r, k0, v, lw, a, g


def _head_sum_raw(x):
    n = x.shape[1]
    ii = lax.broadcasted_iota(jnp.int32, (n, n), 0) // RWKV_HEAD_DIM
    jj = lax.broadcasted_iota(jnp.int32, (n, n), 1) // RWKV_HEAD_DIM
    ones = (ii == jj).astype(BF16)
    xh = x.astype(BF16)
    xl = (x - xh.astype(F32)).astype(BF16)
    return jnp.dot(xh, ones, preferred_element_type=F32) + jnp.dot(xl, ones, preferred_element_type=F32)


@jax.custom_vjp
def _head_sum(x):
    return _head_sum_raw(x)


_head_sum.defvjp(lambda x: (_head_sum_raw(x), None), lambda _, g: (_head_sum_raw(g),))


def _pre_b_fn(k0, a, k_k, k_a):
    kkr = k0 * k_k
    nrm = jnp.sqrt(_head_sum(kkr * kkr))
    kk = kkr / jnp.maximum(nrm, 1e-12)
    k = k0 * (1.0 + (a - 1.0) * k_a)
    return -kk, k, kk * a


def _rwkv_post_fn(y, r, k, v, g, lnx_w, lnx_b, r_k):
    inv = 1.0 / RWKV_HEAD_DIM
    mu = _head_sum(y) * inv
    yc = y - mu
    var = _head_sum(yc * yc) * inv
    yn = yc * lax.rsqrt(var + RWKV_GN_EPS) * lnx_w + lnx_b
    bonus = _head_sum(r * k * r_k) * v
    return ((yn + bonus) * g,)


def _rotary_fn(cos2, sin2, q, k):
    qs, ks = [], []
    for h in range(RET_HEADS):
        sl = slice(h * RET_HEAD_DIM, (h + 1) * RET_HEAD_DIM)
        qh, kh = q[:, sl], k[:, sl]
        qs.append(qh * cos2 + _swap_halves(qh) * sin2)
        ks.append((kh * cos2 + _swap_halves(kh) * sin2) * (RET_HEAD_DIM ** -0.5))
    return jnp.concatenate(qs, axis=1), jnp.concatenate(ks, axis=1)


def _ret_post_fn(y, gp, gn_w):
    outs = []
    for h in range(RET_HEADS):
        sl = slice(h * RET_HEAD_DIM, (h + 1) * RET_HEAD_DIM)
        yh = y[:, sl]
        mu = jnp.mean(yh, axis=-1, keepdims=True)
        yc = yh - mu
        var = jnp.mean(yc * yc, axis=-1, keepdims=True)
        outs.append(yc * lax.rsqrt(var + RET_GN_EPS) * gn_w[:, sl])
    yn = jnp.concatenate(outs, axis=1)
    return (gp * _sigmoid(gp) * yn,)


def _blk_spec(a, tb, rev_nb=None):
    nd = a.ndim
    if rev_nb is None:
        return pl.BlockSpec((tb,) + a.shape[1:], lambda i: (i,) + (0,) * (nd - 1))
    return pl.BlockSpec((tb,) + a.shape[1:], lambda i: (rev_nb - 1 - i,) + (0,) * (nd - 1))


def _full_spec(a):
    nd = a.ndim
    return pl.BlockSpec(a.shape, lambda i: (0,) * nd)


def _tok_fwd(name, fn, toks, consts, out_tails, tb=TOK_BLOCK, out_dtypes=None):
    out_dtypes = out_dtypes or [F32] * len(out_tails)
    n_in = len(toks) + len(consts)
    tn = toks[0].shape[0]

    def body(*refs):
        outs = fn(*[r[...] for r in refs[:n_in]])
        for r, o in zip(refs[n_in:], outs):
            r[...] = o.astype(r.dtype)

    out_shape = [jax.ShapeDtypeStruct((tn,) + tuple(s), dt) for s, dt in zip(out_tails, out_dtypes)]
    return pl.pallas_call(
        body, name=name, grid=(tn // tb,),
        in_specs=[_blk_spec(a, tb) for a in toks] + [_full_spec(c) for c in consts],
        out_specs=[_blk_spec(o, tb) for o in out_shape], out_shape=out_shape,
        compiler_params=_cparams(1))(*toks, *consts)


def _tok_bwd(name, fn, aux, toks, consts, cts, add=None, tb=TOK_BLOCK, tok_dtypes=None):
    n_aux, n_tok, n_c = len(aux), len(toks), len(consts)
    ct_groups = [c if isinstance(c, (tuple, list)) else (c,) for c in cts]
    ct_flat = [a for grp in ct_groups for a in grp]
    n_ct = len(ct_flat)
    n_add = 0 if add is None else 1
    tn = toks[0].shape[0]

    def body(*refs):
        pos = 0
        aux_v = [r[...] for r in refs[pos:pos + n_aux]]; pos += n_aux
        tok_v = [r[...] for r in refs[pos:pos + n_tok]]; pos += n_tok
        const_v = [r[...] for r in refs[pos:pos + n_c]]; pos += n_c
        ct_refs = refs[pos:pos + n_ct]; pos += n_ct
        add_refs = refs[pos:pos + n_add]; pos += n_add
        dtok_refs = refs[pos:pos + n_tok]; pos += n_tok
        dconst_refs = refs[pos:pos + n_c]
        ct_v, q = [], 0
        for grp in ct_groups:
            s = ct_refs[q][...]
            for r in ct_refs[q + 1:q + len(grp)]:
                s = s + r[...]
            q += len(grp)
            ct_v.append(s)
        _, vjp = jax.vjp(lambda *tc: fn(*aux_v, *tc), *tok_v, *const_v)
        grads = vjp(tuple(ct_v))
        for j, r in enumerate(dtok_refs):
            gj = grads[j]
            if j == 0 and n_add:
                gj = gj + add_refs[0][...]
            r[...] = gj.astype(r.dtype)

        @pl.when(pl.program_id(0) == 0)
        def _():
            for r in dconst_refs:
                r[...] = jnp.zeros(r.shape, F32)

        for j, r in enumerate(dconst_refs):
            r[...] += grads[n_tok + j]

    ins = list(aux) + list(toks) + list(consts) + ct_flat + ([add] if n_add else [])
    in_specs = ([_blk_spec(a, tb) for a in aux] + [_blk_spec(a, tb) for a in toks] + [_full_spec(c) for c in consts]
                + [_blk_spec(a, tb) for a in ct_flat] + ([_blk_spec(add, tb)] if n_add else []))
    tok_dtypes = tok_dtypes or [F32] * n_tok
    out_shape = ([jax.ShapeDtypeStruct(a.shape, dt) for a, dt in zip(toks, tok_dtypes)]
                 + [jax.ShapeDtypeStruct(c.shape, F32) for c in consts])
    out_specs = [_blk_spec(a, tb) for a in toks] + [_full_spec(c) for c in consts]
    return pl.pallas_call(body, name=name, grid=(tn // tb,), in_specs=in_specs, out_specs=out_specs,
                          out_shape=out_shape, compiler_params=_cparams(1))(*ins)


MM_VMEM_BUDGET = 40 * 1024 * 1024
MM_STEP_SECONDS = 0.4e-6
MM_HBM_BYTES_PER_SECOND = 2.5e12
MM_XPOSE_SECONDS_PER_ELEM = 2e-12
MM_MXU_COLUMNS = 256
MM_MXU_FLOPS = 9e14


def _mm_tiles(m, n, kd, a_bytes, b_bytes, o_bytes, has_add, ta):
    divs = lambda d: [t for t in range(128, d + 1, 128) if d % t == 0]
    best = None
    for tm in divs(m):
        for tn in divs(n):
            for tk in divs(kd):
                ni, nj, nk = m // tm, n // tn, kd // tk
                vmem = (2 * tm * tk * a_bytes + 2 * tk * tn * b_bytes + tm * tn * 4 + 2 * tm * tn * o_bytes
                        + (2 * tm * tn * 4 if has_add else 0) + 2 * (tm * tk + tk * tn) + tm * tn * 4)
                if vmem > MM_VMEM_BUDGET:
                    continue
                a_traffic = m * kd * a_bytes * (nj if nk > 1 else 1)
                b_traffic = kd * n * b_bytes * (ni if nj * nk > 1 else 1)
                cost = ni * nj * nk * MM_STEP_SECONDS + (a_traffic + b_traffic) / MM_HBM_BYTES_PER_SECOND
                cost += 2.0 * m * kd * nj * max(tn, MM_MXU_COLUMNS) / MM_MXU_FLOPS
                if ta:
                    cost += m * kd * nj * MM_XPOSE_SECONDS_PER_ELEM
                if best is None or cost < best[0]:
                    best = (cost, tm, tn, tk)
    return best[1:]


def _mm(name, a, b, ta=False, tb=False, add=None, out_dtype=F32):
    if ta:
        kd, m = a.shape
    else:
        m, kd = a.shape
    if tb:
        n, kb = b.shape
    else:
        kb, n = b.shape
    assert kd == kb, (a.shape, b.shape)
    tm, tn, tk = _mm_tiles(m, n, kd, a.dtype.itemsize, b.dtype.itemsize, jnp.dtype(out_dtype).itemsize,
                           add is not None, ta)
    nk = kd // tk
    has_add = add is not None
    dims = (((0 if ta else 1,), (1 if tb else 0,)), ((), ()))

    def body(*refs):
        a_ref, b_ref = refs[0], refs[1]
        o_ref, acc_ref = refs[-2], refs[-1]
        k = pl.program_id(2)

        @pl.when(k == 0)
        def _():
            acc_ref[...] = refs[2][...] if has_add else jnp.zeros(acc_ref.shape, F32)

        acc_ref[...] += lax.dot_general(a_ref[...].astype(BF16), b_ref[...].astype(BF16), dims,
                                        preferred_element_type=F32)

        @pl.when(k == nk - 1)
        def _():
            o_ref[...] = acc_ref[...].astype(out_dtype)

    a_spec = pl.BlockSpec((tk, tm), lambda i, j, k: (k, i)) if ta else pl.BlockSpec((tm, tk), lambda i, j, k: (i, k))
    b_spec = pl.BlockSpec((tn, tk), lambda i, j, k: (j, k)) if tb else pl.BlockSpec((tk, tn), lambda i, j, k: (k, j))
    o_spec = pl.BlockSpec((tm, tn), lambda i, j, k: (i, j))
    ins = [a, b] + ([add] if has_add else [])
    in_specs = [a_spec, b_spec] + ([o_spec] if has_add else [])
    return pl.pallas_call(body, name=name, grid=(m // tm, n // tn, nk), in_specs=in_specs, out_specs=o_spec,
                          out_shape=jax.ShapeDtypeStruct((m, n), out_dtype),
                          scratch_shapes=[pltpu.VMEM((tm, tn), F32)], compiler_params=_cparams(3))(*ins)


def _prev8_spec(a, tb, rev_nb=None):
    r = tb // 8
    if rev_nb is None:
        return pl.BlockSpec((8, a.shape[1]), lambda i: (jnp.maximum(i * r - 1, 0), 0))
    return pl.BlockSpec((8, a.shape[1]), lambda i: (jnp.maximum((rev_nb - 1 - i) * r - 1, 0), 0))


def _pre_a_fwd(h1, p, consts, tb=TOK_BLOCK):
    tn = h1.shape[0]

    def body(h1_ref, h1h_ref, p_ref, ph_ref, *rest):
        c_refs, o_refs = rest[:len(consts)], rest[len(consts):]
        first = pl.program_id(0) == 0
        h1p = jnp.where(first, 0.0, h1h_ref[7:8, :])
        pp = jnp.where(first, 0.0, ph_ref[7:8, :])
        outs = _pre_a_fn(h1_ref[...], h1p, p_ref[...], pp, *[c[...] for c in c_refs])
        for r, o in zip(o_refs, outs):
            r[...] = o

    out_shape = [jax.ShapeDtypeStruct((tn, RWKV_WIDTH), F32) for _ in range(6)]
    return pl.pallas_call(
        body, name="rwkv_pre_a_fwd", grid=(tn // tb,),
        in_specs=[_blk_spec(h1, tb), _prev8_spec(h1, tb), _blk_spec(p, tb), _prev8_spec(p, tb)]
        + [_full_spec(c) for c in consts],
        out_specs=[_blk_spec(o, tb) for o in out_shape], out_shape=out_shape,
        compiler_params=_cparams(1))(h1, h1, p, p, *consts)


def _pre_a_bwd(h1, p, consts, cts, tb=TOK_BLOCK):
    tn = h1.shape[0]
    nb = tn // tb
    n_c = len(consts)
    ct_groups = [c if isinstance(c, (tuple, list)) else (c,) for c in cts]
    ct_flat = [a for grp in ct_groups for a in grp]
    n_ct = len(ct_flat)

    def body(*refs):
        h1_ref, h1h_ref, p_ref, ph_ref = refs[:4]
        c_refs = refs[4:4 + n_c]
        ct_refs = refs[4 + n_c:4 + n_c + n_ct]
        dh1_ref, dp_ref = refs[4 + n_c + n_ct:6 + n_c + n_ct]
        dc_refs = refs[6 + n_c + n_ct:6 + 2 * n_c + n_ct]
        ch_ref, cp_ref = refs[-2], refs[-1]
        i = pl.program_id(0)
        first_block = i == nb - 1
        h1p = jnp.where(first_block, 0.0, h1h_ref[7:8, :])
        pp = jnp.where(first_block, 0.0, ph_ref[7:8, :])
        ct_v, q = [], 0
        for grp in ct_groups:
            s = ct_refs[q][...]
            for r in ct_refs[q + 1:q + len(grp)]:
                s = s + r[...]
            q += len(grp)
            ct_v.append(s)
        _, vjp = jax.vjp(_pre_a_fn, h1_ref[...], h1p, p_ref[...], pp, *[c[...] for c in c_refs])
        grads = vjp(tuple(ct_v))

        @pl.when(i == 0)
        def _():
            ch_ref[...] = jnp.zeros(ch_ref.shape, F32)
            cp_ref[...] = jnp.zeros(cp_ref.shape, F32)
            for r in dc_refs:
                r[...] = jnp.zeros(r.shape, F32)

        rowh = lax.broadcasted_iota(jnp.int32, (tb, h1.shape[1]), 0)
        rowp = lax.broadcasted_iota(jnp.int32, (tb, p.shape[1]), 0)
        dh1_ref[...] = grads[0] + jnp.where(rowh == tb - 1, jnp.broadcast_to(ch_ref[0:1, :], rowh.shape), 0.0)
        dp_ref[...] = (grads[2] + jnp.where(rowp == tb - 1, jnp.broadcast_to(cp_ref[0:1, :], rowp.shape), 0.0)
                       ).astype(dp_ref.dtype)
        ch_ref[0:1, :] = grads[1]
        cp_ref[0:1, :] = grads[3]
        for j, r in enumerate(dc_refs):
            r[...] += grads[4 + j]

    ins = [h1, h1, p, p] + list(consts) + ct_flat
    in_specs = ([_blk_spec(h1, tb, nb), _prev8_spec(h1, tb, nb), _blk_spec(p, tb, nb), _prev8_spec(p, tb, nb)]
                + [_full_spec(c) for c in consts] + [_blk_spec(a, tb, nb) for a in ct_flat])
    out_shape = ([jax.ShapeDtypeStruct(h1.shape, F32), jax.ShapeDtypeStruct(p.shape, BF16)]
                 + [jax.ShapeDtypeStruct(c.shape, F32) for c in consts])
    out_specs = [_blk_spec(h1, tb, nb), _blk_spec(p, tb, nb)] + [_full_spec(c) for c in consts]
    return pl.pallas_call(body, name="rwkv_pre_a_bwd", grid=(nb,), in_specs=in_specs, out_specs=out_specs,
                          out_shape=out_shape,
                          scratch_shapes=[pltpu.VMEM((8, h1.shape[1]), F32), pltpu.VMEM((8, p.shape[1]), F32)],
                          compiler_params=_cparams(1))(*ins)


def _my_index():
    return 4 * lax.axis_index("x") + 2 * lax.axis_index("y") + lax.axis_index("c")


def _peer(k):
    x, y, c = lax.axis_index("x"), lax.axis_index("y"), lax.axis_index("c")
    px = 1 - x if k & 4 else x
    py = 1 - y if k & 2 else y
    pc = 1 - c if k & 1 else c
    return (px, py, pc), 4 * px + 2 * py + pc


def _xchg_sems(n):
    return [pltpu.SemaphoreType.DMA((n * (N_DEV - 1),)), pltpu.SemaphoreType.DMA((n * (N_DEV - 1),)),
            pltpu.SemaphoreType.DMA((n,))]


def _scatter_copies(srcs, dsts, sems, incoming=False):
    send_sems, recv_sems, local_sems = sems
    me = _my_index()
    local, remote = [], []
    for i, (s, d) in enumerate(zip(srcs, dsts)):
        if not incoming:
            local.append(pltpu.make_async_copy(s.at[me], d.at[me], local_sems.at[i]))
        for k in range(1, N_DEV):
            peer, plin = _peer(k)
            j = i * (N_DEV - 1) + k - 1
            s_slot, d_slot = (me, plin) if incoming else (plin, me)
            remote.append(pltpu.make_async_remote_copy(
                src_ref=s.at[s_slot], dst_ref=d.at[d_slot], send_sem=send_sems.at[j],
                recv_sem=recv_sems.at[j], device_id=peer, device_id_type=pl.DeviceIdType.MESH))
    return local, remote


def _scatter_start(srcs, dsts, sems):
    local, out = _scatter_copies(srcs, dsts, sems)
    for cp in local + out:
        cp.start()


def _scatter_wait(srcs, dsts, sems):
    for cp in _scatter_copies(srcs, dsts, sems, incoming=True)[1]:
        cp.wait_recv()
    local, out = _scatter_copies(srcs, dsts, sems)
    for cp in out:
        cp.wait_send()
    for cp in local:
        cp.wait()


_ICI_PEERS = (2, 4, 6)


def _gather_copies(srcs, dsts, sems, group):
    send_sems, recv_sems, local_sems = sems
    me = _my_index()
    sib, sib_lin = _peer(1)
    out = []
    for i, (s, d) in enumerate(zip(srcs, dsts)):
        def mk(q, src, dst, dev):
            j = i * (N_DEV - 1) + q
            return pltpu.make_async_remote_copy(src_ref=src, dst_ref=dst, send_sem=send_sems.at[j],
                                                recv_sem=recv_sems.at[j], device_id=dev,
                                                device_id_type=pl.DeviceIdType.MESH)
        if group == 'local':
            out.append(pltpu.make_async_copy(s, d.at[me], local_sems.at[i]))
        elif group == 'own':
            out.append(mk(0, s, d.at[me], sib))
        elif group == 'in_d2d':
            out.append(mk(0, s, d.at[sib_lin], sib))
        for jj, k in enumerate(_ICI_PEERS):
            peer, plin = _peer(k)
            plin_other = _peer(k + 1)[1]
            if group == 'own':
                out.append(mk(1 + jj, s, d.at[me], peer))
            elif group == 'in_ici':
                out.append(mk(1 + jj, s, d.at[plin], peer))
            elif group == 'pass_on':
                out.append(mk(4 + jj, d.at[plin], d.at[plin], sib))
            elif group == 'in_d2d':
                out.append(mk(4 + jj, d.at[plin_other], d.at[plin_other], sib))
    return out


def _gather_start(srcs, dsts, sems):
    for cp in _gather_copies(srcs, dsts, sems, 'local') + _gather_copies(srcs, dsts, sems, 'own'):
        cp.start()


def _gather_pass_on(srcs, dsts, sems):
    for cp in _gather_copies(srcs, dsts, sems, 'in_ici'):
        cp.wait_recv()
    for cp in _gather_copies(srcs, dsts, sems, 'pass_on'):
        cp.start()


def _gather_finish(srcs, dsts, sems):
    for cp in _gather_copies(srcs, dsts, sems, 'in_d2d'):
        cp.wait_recv()
    for cp in _gather_copies(srcs, dsts, sems, 'own') + _gather_copies(srcs, dsts, sems, 'pass_on'):
        cp.wait_send()
    for cp in _gather_copies(srcs, dsts, sems, 'local'):
        cp.wait()


def _xchg_out_shapes(srcs, scatter):
    return [jax.ShapeDtypeStruct(s.shape if scatter else (N_DEV,) + s.shape, s.dtype) for s in srcs]


_ANY = pl.BlockSpec(memory_space=pl.ANY)


def _exchange(name, srcs, scatter):
    n = len(srcs)

    def body(*refs):
        s, d, sems = refs[:n], refs[n:2 * n], refs[2 * n:]
        if scatter:
            _scatter_start(s, d, sems)
            _scatter_wait(s, d, sems)
        else:
            _gather_start(s, d, sems)
            _gather_pass_on(s, d, sems)
            _gather_finish(s, d, sems)

    return pl.pallas_call(body, name=name, in_specs=[_ANY] * n, out_specs=[_ANY] * n,
                          out_shape=_xchg_out_shapes(srcs, scatter), scratch_shapes=_xchg_sems(n))(*srcs)


_MM_DIMS = {'nn': (((1,), (0,)), ((), ())), 'nt': (((1,), (1,)), ((), ())), 'tn': (((0,), (0,)), ((), ()))}


def _cmm_raw(x, y, kind, split):
    dot = functools.partial(lax.dot_general, dimension_numbers=_MM_DIMS[kind], preferred_element_type=F32)
    xh, yh = x.astype(BF16), y.astype(BF16)
    out = dot(xh, yh)
    if split:
        xl = (x - xh.astype(F32)).astype(BF16)
        yl = (y - yh.astype(F32)).astype(BF16)
        out = out + (dot(xh, yl) + dot(xl, yh))
    return out


@functools.partial(jax.custom_vjp, nondiff_argnums=(2, 3))
def _cmm(x, y, kind, split=False):
    return _cmm_raw(x, y, kind, split)


def _cmm_fwd(x, y, kind, split):
    return _cmm_raw(x, y, kind, split), (x, y)


def _cmm_bwd(kind, split, res, g):
    x, y = res
    if kind == 'nn':
        return _cmm_raw(g, y, 'nt', split), _cmm_raw(x, g, 'tn', split)
    if kind == 'nt':
        return _cmm_raw(g, y, 'nn', split), _cmm_raw(g, x, 'tn', split)
    return _cmm_raw(y, g, 'nt', split), _cmm_raw(x, g, 'nn', split)


_cmm.defvjp(_cmm_fwd, _cmm_bwd)


def _tri_sum_raw(tri, x, kind):
    dot = functools.partial(lax.dot_general, dimension_numbers=_MM_DIMS[kind], preferred_element_type=F32)
    tb = tri.astype(BF16)
    hi, mid, lo = _split3(x)
    return (dot(tb, hi) + dot(tb, mid)) + dot(tb, lo)


@functools.partial(jax.custom_vjp, nondiff_argnums=(2,))
def _tri_sum(tri, x, kind):
    return _tri_sum_raw(tri, x, kind)


def _tri_sum_fwd(tri, x, kind):
    return _tri_sum_raw(tri, x, kind), tri


def _tri_sum_bwd(kind, tri, g):
    return jnp.zeros_like(tri), _tri_sum_raw(tri, g, 'tn' if kind == 'nn' else 'nn')


_tri_sum.defvjp(_tri_sum_fwd, _tri_sum_bwd)


SCAN_GROUP = 2


def _lane_group(shape, width):
    return lax.broadcasted_iota(jnp.int32, shape, 1) // width


def _block_rows(y):
    grp = _lane_group(y.shape, y.shape[1] // SCAN_GROUP)
    return jnp.concatenate([jnp.where(grp == g, y, 0.0) for g in range(SCAN_GROUP)], axis=0)


def _gmm(x, y, kind, split=False):
    if kind != 'tn':
        return _cmm(x, _block_rows(y), kind, split)
    full = _cmm(x, y, 'tn', split)
    rows = full.shape[0] // SCAN_GROUP
    grp = _lane_group((rows, full.shape[1]), full.shape[1] // SCAN_GROUP)
    out = jnp.zeros((rows, full.shape[1]), F32)
    for g in range(SCAN_GROUP):
        out = out + jnp.where(grp == g, full[g * rows:(g + 1) * rows, :], 0.0)
    return out


def _chunk_fn(S0, r, lw, k, v, a, b):
    groups = range(len(r))
    C = SCAN_CHUNK
    n_sub = r[0].shape[0] // C
    us = [(s, g) for s in range(n_sub) for g in groups]
    rows = lambda z: {(s, g): z[g][s * C:(s + 1) * C, :] for s, g in us}
    r, lw, k, v, a, b = rows(r), rows(lw), rows(k), rows(v), rows(a), rows(b)
    ii = lax.broadcasted_iota(jnp.int32, (C, SCAN_GROUP * C), 0)
    jj = lax.broadcasted_iota(jnp.int32, (C, SCAN_GROUP * C), 1) % C
    incl, strict = ii >= jj, ii > jj
    eye = (ii == jj).astype(F32)
    inclf = incl[:, :C].astype(F32)
    cum = {u: _tri_sum(inclf, lw[u], 'nn') for u in us}
    e_inv = {u: jnp.exp(-cum[u]) for u in us}
    At = {u: a[u] * jnp.exp(cum[u] - lw[u]) for u in us}
    Rt = {u: r[u] * jnp.exp(cum[u]) for u in us}
    Kh = {u: k[u] * e_inv[u] for u in us}
    Bh = {u: b[u] * e_inv[u] for u in us}
    Mab = {u: jnp.where(strict, _gmm(At[u], Bh[u], 'nt'), 0.0) for u in us}
    Mak = {u: jnp.where(strict, _gmm(At[u], Kh[u], 'nt'), 0.0) for u in us}
    Mrk = {u: jnp.where(incl, _gmm(Rt[u], Kh[u], 'nt'), 0.0) for u in us}
    Mrb = {u: jnp.where(incl, _gmm(Rt[u], Bh[u], 'nt'), 0.0) for u in us}
    mv = {u: _gmm(Mak[u], v[u], 'nn') for u in us}
    yv = {u: _gmm(Mrk[u], v[u], 'nn') for u in us}
    gC = {u: jnp.exp(jnp.sum(lw[u], axis=0, keepdims=True)) for u in us}
    P = Mab
    Tm = {u: eye + P[u] for u in us}
    n = 1
    while 2 * n < C:
        P = {u: _gmm(P[u], P[u], 'nn', True) for u in us}
        Tm = {u: _gmm(Tm[u], eye + P[u], 'nn', True) for u in us}
        n *= 2
    S = list(S0)
    ys = {}
    for s in range(n_sub):
        U = {g: _gmm(Tm[s, g], _gmm(At[s, g], S[g], 'nt') + mv[s, g], 'nn', True) for g in groups}
        for g in groups:
            ys[s, g] = _gmm(Rt[s, g], S[g], 'nt') + yv[s, g] + _gmm(Mrb[s, g], U[g], 'nn')
        S = [S[g] * gC[s, g] + _gmm(v[s, g], Kh[s, g] * gC[s, g], 'tn') + _gmm(U[g], Bh[s, g] * gC[s, g], 'tn')
             for g in groups]
    Y = tuple(jnp.concatenate([ys[s, g] for s in range(n_sub)], axis=0) for g in groups)
    return Y, tuple(S)


def _cscan_fwd(r, lw, k, v, a, b, xs):
    n_x = len(xs)
    tn = r.shape[0]
    H, Dh, Dv = RWKV_HEADS // SCAN_GROUP, RWKV_HEAD_DIM * SCAN_GROUP, RWKV_HEAD_DIM
    nc = tn // (SCAN_CHUNK * SCAN_SUB)
    lanes = lambda h: slice(h * Dh, (h + 1) * Dh)
    heads = lambda ref: tuple(ref[:, lanes(h)] for h in range(H))
    mats = lambda ref: tuple(ref[h] for h in range(H))

    def body(r_ref, lw_ref, k_ref, v_ref, a_ref, b_ref, *rest):
        x_src, (y_ref, ck_ref) = rest[:n_x], rest[n_x:n_x + 2]
        x_dst, s_ref, sems = rest[n_x + 2:2 * n_x + 2], rest[2 * n_x + 2], rest[2 * n_x + 3:]

        @pl.when(pl.program_id(0) == 0)
        def _():
            s_ref[...] = jnp.zeros(s_ref.shape, F32)
            _gather_start(x_src, x_dst, sems)

        ck_ref[0] = s_ref[...]
        y, sc = _chunk_fn(mats(s_ref), heads(r_ref), heads(lw_ref), heads(k_ref), heads(v_ref), heads(a_ref),
                          heads(b_ref))
        for h in range(H):
            y_ref[:, lanes(h)] = y[h]
            s_ref[h] = sc[h]

        @pl.when(pl.program_id(0) == max(nc - 4, 0))
        def _():
            _gather_pass_on(x_src, x_dst, sems)

        @pl.when(pl.program_id(0) == nc - 1)
        def _():
            _gather_finish(x_src, x_dst, sems)

    hm = pl.BlockSpec((SCAN_CHUNK * SCAN_SUB, H * Dh), lambda c: (c, 0))
    res = pl.pallas_call(
        body, name="rwkv_scan_fwd", grid=(nc,), in_specs=[hm] * 6 + [_ANY] * n_x,
        out_specs=[hm, pl.BlockSpec((1, H, Dv, Dh), lambda c: (c, 0, 0, 0))] + [_ANY] * n_x,
        out_shape=[jax.ShapeDtypeStruct((tn, H * Dh), F32), jax.ShapeDtypeStruct((nc, H, Dv, Dh), F32)]
        + _xchg_out_shapes(xs, False),
        scratch_shapes=[pltpu.VMEM((H, Dv, Dh), F32)] + _xchg_sems(n_x),
        compiler_params=_cparams(1))(r, lw, k, v, a, b, *xs)
    return res[0], res[1], res[2:]


def _cscan_bwd(r, lw, k, v, a, b, dy, ck, xs):
    n_x = len(xs)
    tn = r.shape[0]
    H, Dh, Dv = RWKV_HEADS // SCAN_GROUP, RWKV_HEAD_DIM * SCAN_GROUP, RWKV_HEAD_DIM
    nc = tn // (SCAN_CHUNK * SCAN_SUB)
    lanes = lambda h: slice(h * Dh, (h + 1) * Dh)
    heads = lambda ref: tuple(ref[:, lanes(h)] for h in range(H))
    mats = lambda ref: tuple(ref[h] for h in range(H))

    def body(r_ref, lw_ref, k_ref, v_ref, a_ref, b_ref, dy_ref, ck_ref, *rest):
        x_src = rest[:n_x]
        d_refs = rest[n_x:n_x + 6]
        x_dst = rest[n_x + 6:2 * n_x + 6]
        g_ref = rest[2 * n_x + 6]
        sems = rest[2 * n_x + 7:]

        @pl.when(pl.program_id(0) == 0)
        def _():
            g_ref[...] = jnp.zeros(g_ref.shape, F32)
            _scatter_start(x_src, x_dst, sems)

        s0 = tuple(ck_ref[0, h] for h in range(H))
        _, vjp = jax.vjp(_chunk_fn, s0, heads(r_ref), heads(lw_ref), heads(k_ref), heads(v_ref), heads(a_ref),
                         heads(b_ref))
        grads = vjp((heads(dy_ref), mats(g_ref)))
        for h in range(H):
            g_ref[h] = grads[0][h]
            for d_ref, gz in zip(d_refs, grads[1:]):
                d_ref[:, lanes(h)] = gz[h]

        @pl.when(pl.program_id(0) == nc - 1)
        def _():
            _scatter_wait(x_src, x_dst, sems)

    hm = pl.BlockSpec((SCAN_CHUNK * SCAN_SUB, H * Dh), lambda c: (nc - 1 - c, 0))
    hshape = jax.ShapeDtypeStruct((tn, H * Dh), F32)
    res = pl.pallas_call(
        body, name="rwkv_scan_bwd", grid=(nc,),
        in_specs=[hm] * 7 + [pl.BlockSpec((1, H, Dv, Dh), lambda c: (nc - 1 - c, 0, 0, 0))] + [_ANY] * n_x,
        out_specs=[hm] * 6 + [_ANY] * n_x, out_shape=[hshape] * 6 + _xchg_out_shapes(xs, True),
        scratch_shapes=[pltpu.VMEM((H, Dv, Dh), F32)] + _xchg_sems(n_x),
        compiler_params=_cparams(1))(r, lw, k, v, a, b, dy, ck, *xs)
    return res[:6], res[6:]


def _decay_mask(lg, i, j, blk):
    rows = lax.broadcasted_iota(jnp.int32, (blk, blk), 0)
    cols = lax.broadcasted_iota(jnp.int32, (blk, blk), 1)
    dd = (rows - cols + (i - j) * blk).astype(F32)
    return jnp.where(dd >= 0.0, jnp.exp(lg * jnp.maximum(dd, 0.0)), 0.0)


_NT = (((1,), (1,)), ((), ()))
_TN = (((0,), (0,)), ((), ()))


def _ret_attn_fwd(lg, q, k, v, blk=ATT_BLOCK):
    tn = q.shape[0]
    Dh = RET_HEAD_DIM

    def body(lg_ref, q_ref, k_ref, v_ref, o_ref):
        i = pl.program_id(1)
        lgv = lg_ref[0][:, 0:1]
        qb = q_ref[...].astype(BF16)

        def jb(j, acc):
            ks = pl.ds(pl.multiple_of(j * blk, blk), blk)
            s = lax.dot_general(qb, k_ref[ks, :].astype(BF16), _NT, preferred_element_type=F32)
            s = s * _decay_mask(lgv, i, j, blk)
            return acc + jnp.dot(s.astype(BF16), v_ref[ks, :].astype(BF16), preferred_element_type=F32)

        o_ref[...] = lax.fori_loop(0, i + 1, jb, jnp.zeros((blk, Dh), F32))

    full = pl.BlockSpec((tn, Dh), lambda h, i: (0, h))
    qs = pl.BlockSpec((blk, Dh), lambda h, i: (i, h))
    return pl.pallas_call(
        body, name="ret_attn_fwd", grid=(RET_HEADS, tn // blk),
        in_specs=[pl.BlockSpec((1, 1, 128), lambda h, i: (h, 0, 0)), qs, full, full],
        out_specs=qs, out_shape=jax.ShapeDtypeStruct(q.shape, F32), compiler_params=_cparams(2))(lg, q, k, v)


def _ret_attn_bwd(lg, q, k, v, do, blk=ATT_BLOCK):
    tn = q.shape[0]
    nb = tn // blk
    Dh = RET_HEAD_DIM

    def body(lg_ref, q_ref, k_ref, v_ref, do_ref, dq_ref, dk_ref, dv_ref):
        lgv = lg_ref[0][:, 0:1]
        dk_ref[...] = jnp.zeros(dk_ref.shape, F32)
        dv_ref[...] = jnp.zeros(dv_ref.shape, F32)

        def ib(i, carry):
            qs = pl.ds(pl.multiple_of(i * blk, blk), blk)
            qb = q_ref[qs, :].astype(BF16)
            dob = do_ref[qs, :].astype(BF16)

            def jb(j, dq):
                ks = pl.ds(pl.multiple_of(j * blk, blk), blk)
                kb = k_ref[ks, :].astype(BF16)
                vb = v_ref[ks, :].astype(BF16)
                dm = _decay_mask(lgv, i, j, blk)
                s = lax.dot_general(qb, kb, _NT, preferred_element_type=F32) * dm
                ds = lax.dot_general(dob, vb, _NT, preferred_element_type=F32) * dm
                sb, dsb = s.astype(BF16), ds.astype(BF16)
                dv_ref[ks, :] += lax.dot_general(sb, dob, _TN, preferred_element_type=F32)
                dk_ref[ks, :] += lax.dot_general(dsb, qb, _TN, preferred_element_type=F32)
                return dq + jnp.dot(dsb, kb, preferred_element_type=F32)

            dq_ref[qs, :] = lax.fori_loop(0, i + 1, jb, jnp.zeros((blk, Dh), F32))
            return carry

        lax.fori_loop(0, nb, ib, 0)

    full = pl.BlockSpec((tn, Dh), lambda h: (0, h))
    sh = jax.ShapeDtypeStruct(q.shape, F32)
    return pl.pallas_call(
        body, name="ret_attn_bwd", grid=(RET_HEADS,),
        in_specs=[pl.BlockSpec((1, 1, 128), lambda h: (h, 0, 0)), full, full, full, full],
        out_specs=[full, full, full], out_shape=[sh, sh, sh], compiler_params=_cparams(1))(lg, q, k, v, do)


def _next8_spec(a, tb):
    r = tb // 8
    last = a.shape[0] // 8 - 1
    return pl.BlockSpec((8, a.shape[1]), lambda i: (jnp.minimum((i + 1) * r, last), 0))


def _conv_taps(g_ext, cw_ref, cb_ref):
    return (cw_ref[2:3, :] * g_ext + cw_ref[1:2, :] * pltpu.roll(g_ext, 1, 0)
            + cw_ref[0:1, :] * pltpu.roll(g_ext, 2, 0) + cb_ref[...])


def _glu_fwd(gate, up, cw, cb, tb=TOK_BLOCK):
    tn = gate.shape[0]

    def body(g_ref, gh_ref, u_ref, cw_ref, cb_ref, o_ref):
        halo = jnp.where(pl.program_id(0) == 0, 0.0, gh_ref[...])
        g_ext = jnp.concatenate([halo, g_ref[...]], axis=0)
        gc = _conv_taps(g_ext, cw_ref, cb_ref)[8:, :]
        o_ref[...] = (gc * _sigmoid(gc) * u_ref[...]).astype(o_ref.dtype)

    return pl.pallas_call(
        body, name="glu_fwd", grid=(tn // tb,),
        in_specs=[_blk_spec(gate, tb), _prev8_spec(gate, tb), _blk_spec(up, tb), _full_spec(cw), _full_spec(cb)],
        out_specs=_blk_spec(gate, tb), out_shape=jax.ShapeDtypeStruct(gate.shape, BF16),
        compiler_params=_cparams(1))(gate, gate, up, cw, cb)


def _glu_bwd(gate, up, dact, cw, cb, tb=TOK_BLOCK):
    tn = gate.shape[0]
    nb = tn // tb

    def body(g_ref, gp_ref, gn_ref, u_ref, un_ref, d_ref, dn_ref, cw_ref, cb_ref, dg_ref, du_ref, dcw_ref, dcb_ref):
        i = pl.program_id(0)
        gprev = jnp.where(i == 0, 0.0, gp_ref[...])
        dnext = jnp.where(i == nb - 1, 0.0, dn_ref[...])
        g_ext = jnp.concatenate([gprev, g_ref[...], gn_ref[...]], axis=0)
        gc = _conv_taps(g_ext, cw_ref, cb_ref)[8:, :]
        u_e = jnp.concatenate([u_ref[...], un_ref[...]], axis=0)
        d_e = jnp.concatenate([d_ref[...], dnext], axis=0)
        s = _sigmoid(gc)
        dgc = d_e * u_e * (s * (1.0 + gc * (1.0 - s)))
        du_ref[...] = (d_ref[...] * (gc * s)[:tb, :]).astype(du_ref.dtype)
        n_e = tb + 8
        dg_ref[...] = (cw_ref[2:3, :] * dgc + cw_ref[1:2, :] * pltpu.roll(dgc, n_e - 1, 0)
                       + cw_ref[0:1, :] * pltpu.roll(dgc, n_e - 2, 0))[:tb, :].astype(dg_ref.dtype)

        @pl.when(i == 0)
        def _():
            dcw_ref[...] = jnp.zeros(dcw_ref.shape, F32)
            dcb_ref[...] = jnp.zeros(dcb_ref.shape, F32)

        dgc_b = dgc[:tb, :]
        g0 = g_ext[8:8 + tb, :]
        g1 = pltpu.roll(g_ext, 1, 0)[8:8 + tb, :]
        g2 = pltpu.roll(g_ext, 2, 0)[8:8 + tb, :]
        dcw_ref[2:3, :] += jnp.sum(dgc_b * g0, axis=0, keepdims=True)
        dcw_ref[1:2, :] += jnp.sum(dgc_b * g1, axis=0, keepdims=True)
        dcw_ref[0:1, :] += jnp.sum(dgc_b * g2, axis=0, keepdims=True)
        dcb_ref[...] += jnp.sum(dgc_b, axis=0, keepdims=True)

    sh = jax.ShapeDtypeStruct(gate.shape, BF16)
    return pl.pallas_call(
        body, name="glu_bwd", grid=(nb,),
        in_specs=[_blk_spec(gate, tb), _prev8_spec(gate, tb), _next8_spec(gate, tb), _blk_spec(up, tb),
                  _next8_spec(up, tb), _blk_spec(dact, tb), _next8_spec(dact, tb), _full_spec(cw), _full_spec(cb)],
        out_specs=[_blk_spec(gate, tb), _blk_spec(gate, tb), _full_spec(cw), _full_spec(cb)],
        out_shape=[sh, sh, jax.ShapeDtypeStruct(cw.shape, F32), jax.ShapeDtypeStruct(cb.shape, F32)],
        compiler_params=_cparams(1))(gate, gate, gate, up, up, dact, dact, cw, cb)


def _final_loss(x2, tgt, g, tb=TOK_BLOCK):
    tn, dm = x2.shape

    def body(x_ref, t_ref, g_ref, l_ref, dx_ref, dg_ref):
        y, vjp = jax.vjp(_rms_fn, x_ref[...], g_ref[...])
        err = y - t_ref[...]
        dx, dg = vjp(err * (1.0 / dm))

        @pl.when(pl.program_id(0) == 0)
        def _():
            l_ref[...] = jnp.zeros(l_ref.shape, F32)
            dg_ref[...] = jnp.zeros(dg_ref.shape, F32)

        part = 0.5 * jnp.sum(jnp.mean(err * err, axis=-1, keepdims=True), axis=0, keepdims=True)
        l_ref[...] += jnp.broadcast_to(part, l_ref.shape)
        dx_ref[...] = dx
        dg_ref[...] += dg

    return pl.pallas_call(
        body, name="final_loss", grid=(tn // tb,),
        in_specs=[_blk_spec(x2, tb), _blk_spec(tgt, tb), _full_spec(g)],
        out_specs=[pl.BlockSpec((8, 128), lambda i: (0, 0)), _blk_spec(x2, tb), _full_spec(g)],
        out_shape=[jax.ShapeDtypeStruct((8, 128), F32), jax.ShapeDtypeStruct(x2.shape, F32),
                   jax.ShapeDtypeStruct(g.shape, F32)],
        compiler_params=_cparams(1))(x2, tgt, g)


def _pad_cols(w, n):
    return jnp.pad(w, ((0, 0), (0, n - w.shape[1])))


def _pad_rows(w, n):
    return jnp.pad(w, ((0, n - w.shape[0]), (0, 0)))


def _local_step(x, tgt, W, late):
    tn = x.shape[0]
    Wd = RWKV_WIDTH
    row = lambda z: z.reshape(1, -1)
    g_mix, g_ffn, g_fin = row(W['norm_mix_g']), row(W['norm_ffn_g']), row(W['norm_final_g'])

    (h1,) = _tok_fwd("norm_mix_fwd", lambda a, g: (_rms_fn(a, g),), [x], [g_mix], [(D_MODEL,)])
    proj = _mm("proj_fwd", h1, W['w_in_t'], tb=True)
    p_rkv = proj[:, :3 * Wd]
    pre_consts = [row(W['rwkv_mu_w']), row(W['rwkv_mu_a']), row(W['rwkv_mu_g']), row(W['rwkv_mu_r']),
                  row(W['rwkv_mu_k']), row(W['rwkv_mu_v']), row(W['rwkv_w0']),
                  _pad_cols(W['rwkv_w1'], LORA_PAD), _pad_rows(W['rwkv_w2'], LORA_PAD), row(W['rwkv_a0']),
                  _pad_cols(W['rwkv_a1'], LORA_PAD), _pad_rows(W['rwkv_a2'], LORA_PAD),
                  W['rwkv_g1'], W['rwkv_g2']]
    r, k0, v, lw, a, g = _pre_a_fwd(h1, p_rkv, pre_consts)
    k_k, k_a = row(W['rwkv_k_k']), row(W['rwkv_k_a'])
    nkk, k, b = _tok_fwd("rwkv_pre_b_fwd", _pre_b_fn, [k0, a], [k_k, k_a], [(Wd,)] * 3)
    y_scan, ck, gathered = _cscan_fwd(r, lw, k, v, nkk, b, late)
    w_out, w_gate_t, w_up_t, w_down = [g_.reshape(-1, D_MODEL) for g_ in gathered]
    post_consts = [row(W['rwkv_lnx_w']), row(W['rwkv_lnx_b']), row(W['rwkv_r_k'])]
    (y_rwkv,) = _tok_fwd("rwkv_post_fwd", _rwkv_post_fn, [y_scan, r, k, v, g], post_consts, [(Wd,)],
                         out_dtypes=[BF16])

    pos = jnp.arange(tn, dtype=F32)
    half = RET_HEAD_DIM // 2
    inv_freq = ROPE_BASE ** (-jnp.arange(half, dtype=F32) / half)
    ang = pos[:, None] * inv_freq[None, :]
    cos2 = jnp.concatenate([jnp.cos(ang), jnp.cos(ang)], axis=1)
    sin2 = jnp.concatenate([-jnp.sin(ang), jnp.sin(ang)], axis=1)
    lg = jnp.log(1.0 - 2.0 ** (-5.0 - jnp.arange(RET_HEADS, dtype=F32)))
    lg = jnp.broadcast_to(lg[:, None, None], (RET_HEADS, 1, 128))
    q_p, k_p = proj[:, 3 * Wd:4 * Wd], proj[:, 4 * Wd:5 * Wd]
    v_ret, g_ret = proj[:, 5 * Wd:6 * Wd], proj[:, 6 * Wd:7 * Wd]
    q_rot, k_rot = _tok_fwd("ret_rotary_fwd", _rotary_fn, [cos2, sin2, q_p, k_p], [], [(RET_WIDTH,)] * 2)
    y_ret_raw = _ret_attn_fwd(lg, q_rot, k_rot, v_ret)
    gn_w = row(W['ret_gn_w'])
    (y_ret,) = _tok_fwd("ret_post_fwd", _ret_post_fn, [y_ret_raw, g_ret], [gn_w], [(RET_WIDTH,)],
                        out_dtypes=[BF16])

    ycat = jnp.concatenate([y_rwkv, y_ret], axis=1)
    x1 = _mm("out_proj_fwd", ycat, w_out, add=x)
    (h2,) = _tok_fwd("norm_ffn_fwd", lambda a_, g_: (_rms_fn(a_, g_),), [x1], [g_ffn], [(D_MODEL,)],
                     out_dtypes=[BF16])
    gate = _mm("ffn_gate_fwd", h2, w_gate_t, tb=True)
    up = _mm("ffn_up_fwd", h2, w_up_t, tb=True)
    cw = W['ffn_conv_w']
    cb = row(W['ffn_conv_b'])
    act = _glu_fwd(gate, up, cw, cb)
    x2 = _mm("ffn_down_fwd", act, w_down, add=x1)
    loss8, dx2, dg_fin = _final_loss(x2, tgt, g_fin)

    G = {'norm_final_g': dg_fin}
    dact = _mm("ffn_down_dx", dx2, w_down, tb=True)
    d_down = _mm("ffn_down_dw", act, dx2, ta=True, out_dtype=BF16)
    dgate, dup, dcw, dcb = _glu_bwd(gate, up, dact, cw, cb)
    G['ffn_conv_w'], G['ffn_conv_b'] = dcw, dcb
    dh2 = _mm("ffn_gate_dx", dgate, w_gate_t)
    dh2 = _mm("ffn_up_dx", dup, w_up_t, add=dh2)
    d_gate_t = _mm("ffn_gate_dw", dgate, h2, ta=True, out_dtype=BF16)
    d_up_t = _mm("ffn_up_dw", dup, h2, ta=True, out_dtype=BF16)
    dx1, G['norm_ffn_g'] = _tok_bwd("norm_ffn_bwd", lambda a_, g_: (_rms_fn(a_, g_),), [], [x1], [g_ffn], [dh2], add=dx2)
    dycat = _mm("out_proj_dx", dx1, w_out, tb=True)
    d_out = _mm("out_proj_dw", ycat, dx1, ta=True, out_dtype=BF16)
    late_grads = [z.reshape(N_DEV, -1, D_MODEL) for z in (d_out, d_gate_t, d_up_t, d_down)]
    dy_rwkv, dy_ret = dycat[:, :Wd], dycat[:, Wd:]

    dyr_raw, dg_ret, G['ret_gn_w'] = _tok_bwd("ret_post_bwd", _ret_post_fn, [], [y_ret_raw, g_ret], [gn_w], [dy_ret],
                                              tok_dtypes=[F32, BF16])
    dq_rot, dk_rot, dv_ret = _ret_attn_bwd(lg, q_rot, k_rot, v_ret, dyr_raw)
    dq_p, dk_p = _tok_bwd("ret_rotary_bwd", _rotary_fn, [cos2, sin2], [q_p, k_p], [], [dq_rot, dk_rot],
                          tok_dtypes=[BF16, BF16])

    dy_scan, dr1, dk1, dv1, dg, G['rwkv_lnx_w'], G['rwkv_lnx_b'], G['rwkv_r_k'] = _tok_bwd(
        "rwkv_post_bwd", _rwkv_post_fn, [], [y_scan, r, k, v, g], post_consts, [dy_rwkv])
    (dr2, dlw, dk2, dv2, dnkk, db), late_parts = _cscan_bwd(r, lw, k, v, nkk, b, dy_scan, ck, late_grads)
    dk0, da, G['rwkv_k_k'], G['rwkv_k_a'] = _tok_bwd(
        "rwkv_pre_b_bwd", _pre_b_fn, [], [k0, a], [k_k, k_a], [dnkk, (dk1, dk2), db])
    pre_cts = [(dr1, dr2), dk0, (dv1, dv2), dlw, da, dg]
    pre_out = _pre_a_bwd(h1, p_rkv, pre_consts, pre_cts)
    dh1_a, dp_rkv = pre_out[0], pre_out[1]
    (G['rwkv_mu_w'], G['rwkv_mu_a'], G['rwkv_mu_g'], G['rwkv_mu_r'], G['rwkv_mu_k'], G['rwkv_mu_v'], G['rwkv_w0'],
     dw1, dw2, G['rwkv_a0'], da1, da2, G['rwkv_g1'], G['rwkv_g2']) = pre_out[2:]
    G['rwkv_w1'], G['rwkv_w2'] = dw1[:, :64], dw2[:64, :]
    G['rwkv_a1'], G['rwkv_a2'] = da1[:, :64], da2[:64, :]

    dproj = jnp.concatenate([dp_rkv, dq_p, dk_p, dv_ret.astype(BF16), dg_ret], axis=1)
    dh1 = _mm("proj_dx", dproj, W['w_in_t'], add=dh1_a)
    G['w_in_t'] = _mm("proj_dw", dproj, h1, ta=True, out_dtype=BF16)
    dx, G['norm_mix_g'] = _tok_bwd("norm_mix_bwd", lambda a_, g_: (_rms_fn(a_, g_),), [], [x], [g_mix], [dh1], add=dx1)
    return loss8[0, 0], dx, G, late_parts


def _adamw(name, parts, w, m, v):
    rows, cols = w.shape
    sub = 8 * 4 // parts.dtype.itemsize
    tb = max(t for t in range(sub, 65, sub) if rows % t == 0) if rows > 64 else rows
    c1 = 1.0 - ADAM_B1 ** ADAM_STEP
    c2 = 1.0 - ADAM_B2 ** ADAM_STEP

    def body(p_ref, w_ref, m_ref, v_ref, g_ref, d_ref, nm_ref, nv_ref):
        g = p_ref[0].astype(F32)
        for d in range(1, N_DEV):
            g = g + p_ref[d].astype(F32)
        mn = ADAM_B1 * m_ref[...] + (1.0 - ADAM_B1) * g
        vn = ADAM_B2 * v_ref[...] + (1.0 - ADAM_B2) * (g * g)
        m_hat = mn / c1
        v_hat = vn / c2
        g_ref[...] = g
        d_ref[...] = -ADAM_LR * (m_hat / (jnp.sqrt(v_hat) + ADAM_EPS) + ADAM_WD * w_ref[...])
        nm_ref[...] = mn
        nv_ref[...] = vn

    spec = pl.BlockSpec((tb, cols), lambda i: (i, 0))
    sh = jax.ShapeDtypeStruct((rows, cols), F32)
    return pl.pallas_call(
        body, name=name, grid=(rows // tb,),
        in_specs=[pl.BlockSpec((N_DEV, tb, cols), lambda i: (0, i, 0)), spec, spec, spec],
        out_specs=[spec] * 4, out_shape=[sh] * 4, compiler_params=_cparams(1))(parts, w, m, v)


def _local_shape(name):
    gs, ax = SHARDED[name]
    ls = list(gs)
    ls[ax] //= N_DEV
    return tuple(ls)


def _seg(flat, seg):
    n = flat.shape[-1]
    pad = _round_up(n, seg) - n
    if pad:
        flat = jnp.pad(flat, [(0, 0)] * (flat.ndim - 1) + [(0, pad)])
    return flat


def _split3(w):
    hi = w.astype(BF16)
    r1 = w - hi.astype(F32)
    mid = r1.astype(BF16)
    lo = (r1 - mid.astype(F32)).astype(BF16)
    return hi, mid, lo


def _pack_small_shards(shards):
    pieces = []
    for name in SMALL_NAMES:
        flat = shards[name].reshape(-1)
        if name == 'ffn_conv_w':
            pieces += [_seg(p, BF16_SEG) for p in _split3(flat)]
        else:
            pieces.append(flat.astype(BF16))
    return jnp.concatenate(pieces).reshape(-1, 128)


def _unpack_small(gathered):
    flat = gathered.reshape(N_DEV, -1)
    out, off = {}, 0
    for name in SMALL_NAMES:
        gs, ax = SHARDED[name]
        ls = _local_shape(name)
        n = int(np.prod(ls))
        if name == 'ffn_conv_w':
            nseg = _round_up(n, BF16_SEG)
            hi, mid, lo = (flat[:, off + j * nseg: off + j * nseg + n].astype(F32) for j in range(3))
            sh = ((hi + mid) + lo).reshape(N_DEV, 3, -1)
            out[name] = jnp.swapaxes(sh, 0, 1).reshape(3, D_FF)
            off += 3 * nseg
        else:
            sh = flat[:, off:off + n].reshape((N_DEV,) + ls[1:])
            out[name] = sh.reshape(gs[1:]) if ax == 1 else jnp.swapaxes(sh, 0, 1).reshape(gs[1:])
            off += n
    return out


def _small_pieces(sharded, repl):
    return [sharded[n].reshape(-1) for n in SMALL_NAMES] + [repl[n].reshape(-1) for n in REPL_NAMES]


def _pack_small_local(d):
    flat = jnp.concatenate(_small_pieces(d, d))
    return _seg(flat, F32_SEG).reshape(-1, 128)


def _pack_small_grads(G):
    pieces = []
    for name in SMALL_NAMES:
        gs, ax = SHARDED[name]
        g = G[name]
        if name == 'ffn_conv_w':
            sh = jnp.swapaxes(g.reshape(3, N_DEV, -1), 0, 1)
        elif ax == 1:
            sh = g
        else:
            sh = jnp.swapaxes(g.reshape(g.shape[0], N_DEV, -1), 0, 1)
        pieces.append(sh.reshape(N_DEV, -1))
    rep = jnp.concatenate([G[n].reshape(-1) for n in REPL_NAMES])
    pieces.append(jnp.broadcast_to(rep[None, :], (N_DEV, rep.shape[0])))
    flat = _seg(jnp.concatenate(pieces, axis=1), F32_SEG)
    return flat.reshape(N_DEV, -1, 128)


def _unpack_small_local(packed, local_shapes):
    flat = packed.reshape(-1)
    out, off = {}, 0
    for name in SMALL_NAMES + REPL_NAMES:
        n = int(np.prod(local_shapes[name]))
        out[name] = flat[off:off + n].reshape(local_shapes[name])
        off += n
    return out


def kernel(x, *rest):
    nw = len(WEIGHT_NAMES)
    assert len(rest) == 3 * nw + 1
    weights = dict(zip(WEIGHT_NAMES, rest[:nw]))
    loss_target = rest[nw]
    moms = dict(zip(WEIGHT_NAMES, rest[nw + 1:2 * nw + 1]))
    vars_ = dict(zip(WEIGHT_NAMES, rest[2 * nw + 1:]))
    local_shapes = {n: weights[n].shape for n in WEIGHT_NAMES}

    def native2d(name, a):
        a2 = a.reshape(a.shape[-2], a.shape[-1])
        return a2.T if name in BIG_T else a2

    def from2d(name, a2):
        return (a2.T if name in BIG_T else a2).reshape(local_shapes[name])

    big_w = {n: native2d(n, weights[n]) for n in BIG_NAMES}
    w_in_t_sh = big_w['w_in'].astype(BF16)
    late = [big_w[n].astype(BF16) for n in LATE_NAMES]
    small_sh = _pack_small_shards({n: weights[n] for n in SMALL_NAMES})
    w_in_g, small_g = _exchange("weights_all_gather", [w_in_t_sh, small_sh], False)
    W = _unpack_small(small_g)
    W['w_in_t'] = w_in_g.reshape(-1, D_MODEL)
    for n in REPL_NAMES:
        W[n] = weights[n][0] if n != 'norm_final_g' else weights[n]

    loss, dx, G, late_parts = _local_step(x[0], loss_target[0], W, late)

    w_in_parts, small_parts = _exchange(
        "grads_all_to_all", [G['w_in_t'].reshape(N_DEV, -1, D_MODEL), _pack_small_grads(G)], True)
    results = {}
    for n, parts in zip(['w_in'] + LATE_NAMES, [w_in_parts] + list(late_parts)):
        res = _adamw("adamw_" + n, parts, big_w[n], native2d(n, moms[n]), native2d(n, vars_[n]))
        results[n] = [from2d(n, r) for r in res]
    small_res = _adamw("adamw_small", small_parts, _pack_small_local(weights), _pack_small_local(moms),
                       _pack_small_local(vars_))
    small_out = [_unpack_small_local(p, local_shapes) for p in small_res]

    loss = lax.psum(loss, ("x", "y", "c"))
    outs = [loss, dx[None]]
    for j in range(4):
        outs += [results[n][j] if n in results else small_out[j][n] for n in WEIGHT_NAMES]
    return tuple(outs)
```

```python
import functools
import math

import numpy as np
import jax
import jax.numpy as jnp
from jax import lax
from jax.experimental import pallas as pl
from jax.experimental.pallas import tpu as pltpu

F32 = jnp.float32
BF16 = jnp.bfloat16

N_DEV = 8
D_MODEL = 1024
RWKV_HEADS = 8
RWKV_HEAD_DIM = 64
RWKV_WIDTH = 512
RET_HEADS = 4
RET_HEAD_DIM = 128
RET_WIDTH = 512
LORA_PAD = 128
D_FF = 2816
NORM_EPS = 1e-6
RWKV_GN_EPS = 64e-5
RET_GN_EPS = 1e-5
ROPE_BASE = 10000.0
ADAM_LR, ADAM_B1, ADAM_B2, ADAM_EPS, ADAM_WD, ADAM_STEP = 0.001, 0.9, 0.999, 1e-08, 0.01, 10

VMEM_LIMIT = 56 * 1024 * 1024
TOK_BLOCK = 256
SCAN_CHUNK = 64
SCAN_SUB = 1
ATT_BLOCK = 512
BF16_SEG = 2048
F32_SEG = 1024

WEIGHT_NAMES = ['norm_mix_g', 'w_in', 'rwkv_mu_r', 'rwkv_mu_k', 'rwkv_mu_v', 'rwkv_mu_w', 'rwkv_mu_a',
                'rwkv_mu_g', 'rwkv_w0', 'rwkv_w1', 'rwkv_w2', 'rwkv_a0', 'rwkv_a1', 'rwkv_a2', 'rwkv_g1',
                'rwkv_g2', 'rwkv_k_k', 'rwkv_k_a', 'rwkv_r_k', 'rwkv_lnx_w', 'rwkv_lnx_b', 'ret_gn_w',
                'w_out', 'norm_ffn_g', 'ffn_w_gate', 'ffn_w_up', 'ffn_conv_w', 'ffn_conv_b', 'ffn_w_down',
                'norm_final_g']
SHARDED = {
    'w_in': ((1, 1024, 3584), 2), 'rwkv_w1': ((1, 1024, 64), 1), 'rwkv_w2': ((1, 64, 512), 2),
    'rwkv_a1': ((1, 1024, 64), 1), 'rwkv_a2': ((1, 64, 512), 2), 'rwkv_g1': ((1, 1024, 128), 1),
    'rwkv_g2': ((1, 128, 512), 2), 'w_out': ((1, 1024, 1024), 1), 'ffn_w_gate': ((1, 1024, 2816), 2),
    'ffn_w_up': ((1, 1024, 2816), 2), 'ffn_conv_w': ((1, 3, 1, 2816), 3), 'ffn_w_down': ((1, 2816, 1024), 1),
}
REPL_NAMES = [n for n in WEIGHT_NAMES if n not in SHARDED]
BIG_NAMES = ['w_in', 'w_out', 'ffn_w_gate', 'ffn_w_up', 'ffn_w_down']
BIG_T = ('w_in', 'ffn_w_gate', 'ffn_w_up')
LATE_NAMES = ['w_out', 'ffn_w_gate', 'ffn_w_up', 'ffn_w_down']
SMALL_NAMES = [n for n in WEIGHT_NAMES if n in SHARDED and n not in BIG_NAMES]


def _cparams(n_grid):
    return pltpu.CompilerParams(dimension_semantics=("arbitrary",) * n_grid, vmem_limit_bytes=VMEM_LIMIT)


def _round_up(n, m):
    return (n + m - 1) // m * m


@jax.custom_vjp
def _bdot(x, w):
    return jnp.dot(x.astype(BF16), w.astype(BF16), preferred_element_type=F32)


def _bdot_fwd(x, w):
    return _bdot(x, w), (x, w)


def _bdot_bwd(res, g):
    x, w = res
    gb = g.astype(BF16)
    dx = lax.dot_general(gb, w.astype(BF16), (((1,), (1,)), ((), ())), preferred_element_type=F32)
    dw = lax.dot_general(x.astype(BF16), gb, (((0,), (0,)), ((), ())), preferred_element_type=F32)
    return dx, dw.astype(w.dtype)


_bdot.defvjp(_bdot_fwd, _bdot_bwd)


@jax.custom_vjp
def _shift_rows(x, prev):
    rolled = pltpu.roll(x, 1, 0)
    row = lax.broadcasted_iota(jnp.int32, x.shape, 0)
    return jnp.where(row == 0, jnp.broadcast_to(prev, x.shape), rolled)


def _shift_rows_fwd(x, prev):
    return _shift_rows(x, prev), None


def _shift_rows_bwd(_, g):
    n = g.shape[0]
    rolled = pltpu.roll(g, n - 1, 0)
    row = lax.broadcasted_iota(jnp.int32, g.shape, 0)
    return jnp.where(row == n - 1, 0.0, rolled), g[0:1, :]


_shift_rows.defvjp(_shift_rows_fwd, _shift_rows_bwd)


@jax.custom_vjp
def _swap_halves(x):
    return pltpu.roll(x, 64, 1)


_swap_halves.defvjp(lambda x: (_swap_halves(x), None), lambda _, g: (pltpu.roll(g, 64, 1),))


def _sigmoid(x):
    return 1.0 / (1.0 + jnp.exp(-x))


def _softplus(x):
    return jnp.maximum(x, 0.0) + jnp.log(1.0 + jnp.exp(-jnp.abs(x)))


def _rms_fn(x, g):
    return x * lax.rsqrt(jnp.mean(x * x, axis=-1, keepdims=True) + NORM_EPS) * g


def _pre_a_fn(h1, h1p, p, pp, mu_w, mu_a, mu_g, mu_r, mu_k, mu_v, w0, w1, w2, a0, a1, a2, g1, g2, k_k, k_a):
    W = RWKV_WIDTH
    h1s = _shift_rows(h1, h1p)
    ps = _shift_rows(p, pp)
    dx = h1s - h1
    xw = h1 + dx * mu_w
    xa = h1 + dx * mu_a
    xg = h1 + dx * mu_g
    dp = ps - p
    r = p[:, 0:W] + dp[:, 0:W] * mu_r
    k0 = p[:, W:2 * W] + dp[:, W:2 * W] * mu_k
    v = p[:, 2 * W:3 * W] + dp[:, 2 * W:3 * W] * mu_v
    wl = w0 + _bdot(jnp.tanh(_bdot(xw, w1)), w2)
    w_log = -_softplus(-wl) - 0.5
    lw = -jnp.exp(w_log)
    a = _sigmoid(a0 + _bdot(_bdot(xa, a1), a2))
    g = _bdot(_sigmoid(_bdot(xg, g1)), g2)
    nkk, k, b = _pre_b_fn(k0, a, k_k, k_a)
    return r, k, v, lw, nkk, b, g


def _head_sum_raw(x):
    n = x.shape[1]
    ii = lax.broadcasted_iota(jnp.int32, (n, n), 0) // RWKV_HEAD_DIM
    jj = lax.broadcasted_iota(jnp.int32, (n, n), 1) // RWKV_HEAD_DIM
    ones = (ii == jj).astype(BF16)
    xh = x.astype(BF16)
    xl = (x - xh.astype(F32)).astype(BF16)
    return jnp.dot(xh, ones, preferred_element_type=F32) + jnp.dot(xl, ones, preferred_element_type=F32)


@jax.custom_vjp
def _head_sum(x):
    return _head_sum_raw(x)


_head_sum.defvjp(lambda x: (_head_sum_raw(x), None), lambda _, g: (_head_sum_raw(g),))


def _pre_b_fn(k0, a, k_k, k_a):
    kkr = k0 * k_k
    nrm = jnp.sqrt(_head_sum(kkr * kkr))
    kk = kkr / jnp.maximum(nrm, 1e-12)
    k = k0 * (1.0 + (a - 1.0) * k_a)
    return -kk, k, kk * a


def _rwkv_post_fn(y, r, k, v, g, lnx_w, lnx_b, r_k):
    inv = 1.0 / RWKV_HEAD_DIM
    mu = _head_sum(y) * inv
    yc = y - mu
    var = _head_sum(yc * yc) * inv
    yn = yc * lax.rsqrt(var + RWKV_GN_EPS) * lnx_w + lnx_b
    bonus = _head_sum(r * k * r_k) * v
    return ((yn + bonus) * g,)


def _rotary_fn(cos2, sin2, q, k):
    qs, ks = [], []
    for h in range(RET_HEADS):
        sl = slice(h * RET_HEAD_DIM, (h + 1) * RET_HEAD_DIM)
        qh, kh = q[:, sl], k[:, sl]
        qs.append(qh * cos2 + _swap_halves(qh) * sin2)
        ks.append((kh * cos2 + _swap_halves(kh) * sin2) * (RET_HEAD_DIM ** -0.5))
    return jnp.concatenate(qs, axis=1), jnp.concatenate(ks, axis=1)


def _ret_post_fn(y, gp, gn_w):
    outs = []
    for h in range(RET_HEADS):
        sl = slice(h * RET_HEAD_DIM, (h + 1) * RET_HEAD_DIM)
        yh = y[:, sl]
        mu = jnp.mean(yh, axis=-1, keepdims=True)
        yc = yh - mu
        var = jnp.mean(yc * yc, axis=-1, keepdims=True)
        outs.append(yc * lax.rsqrt(var + RET_GN_EPS) * gn_w[:, sl])
    yn = jnp.concatenate(outs, axis=1)
    return (gp * _sigmoid(gp) * yn,)


def _blk_spec(a, tb, rev_nb=None):
    nd = a.ndim
    if rev_nb is None:
        return pl.BlockSpec((tb,) + a.shape[1:], lambda i: (i,) + (0,) * (nd - 1))
    return pl.BlockSpec((tb,) + a.shape[1:], lambda i: (rev_nb - 1 - i,) + (0,) * (nd - 1))


def _full_spec(a):
    nd = a.ndim
    return pl.BlockSpec(a.shape, lambda i: (0,) * nd)


def _tok_fwd(name, fn, toks, consts, out_tails, tb=TOK_BLOCK, out_dtypes=None):
    out_dtypes = out_dtypes or [F32] * len(out_tails)
    n_in = len(toks) + len(consts)
    tn = toks[0].shape[0]

    def body(*refs):
        outs = fn(*[r[...] for r in refs[:n_in]])
        for r, o in zip(refs[n_in:], outs):
            r[...] = o.astype(r.dtype)

    out_shape = [jax.ShapeDtypeStruct((tn,) + tuple(s), dt) for s, dt in zip(out_tails, out_dtypes)]
    return pl.pallas_call(
        body, name=name, grid=(tn // tb,),
        in_specs=[_blk_spec(a, tb) for a in toks] + [_full_spec(c) for c in consts],
        out_specs=[_blk_spec(o, tb) for o in out_shape], out_shape=out_shape,
        compiler_params=_cparams(1))(*toks, *consts)


def _tok_bwd(name, fn, aux, toks, consts, cts, add=None, tb=TOK_BLOCK, tok_dtypes=None):
    n_aux, n_tok, n_c = len(aux), len(toks), len(consts)
    ct_groups = [c if isinstance(c, (tuple, list)) else (c,) for c in cts]
    ct_flat = [a for grp in ct_groups for a in grp]
    n_ct = len(ct_flat)
    n_add = 0 if add is None else 1
    tn = toks[0].shape[0]

    def body(*refs):
        pos = 0
        aux_v = [r[...] for r in refs[pos:pos + n_aux]]; pos += n_aux
        tok_v = [r[...] for r in refs[pos:pos + n_tok]]; pos += n_tok
        const_v = [r[...] for r in refs[pos:pos + n_c]]; pos += n_c
        ct_refs = refs[pos:pos + n_ct]; pos += n_ct
        add_refs = refs[pos:pos + n_add]; pos += n_add
        dtok_refs = refs[pos:pos + n_tok]; pos += n_tok
        dconst_refs = refs[pos:pos + n_c]
        ct_v, q = [], 0
        for grp in ct_groups:
            s = ct_refs[q][...]
            for r in ct_refs[q + 1:q + len(grp)]:
                s = s + r[...]
            q += len(grp)
            ct_v.append(s)
        _, vjp = jax.vjp(lambda *tc: fn(*aux_v, *tc), *tok_v, *const_v)
        grads = vjp(tuple(ct_v))
        for j, r in enumerate(dtok_refs):
            gj = grads[j]
            if j == 0 and n_add:
                gj = gj + add_refs[0][...]
            r[...] = gj.astype(r.dtype)

        @pl.when(pl.program_id(0) == 0)
        def _():
            for r in dconst_refs:
                r[...] = jnp.zeros(r.shape, F32)

        for j, r in enumerate(dconst_refs):
            r[...] += grads[n_tok + j]

    ins = list(aux) + list(toks) + list(consts) + ct_flat + ([add] if n_add else [])
    in_specs = ([_blk_spec(a, tb) for a in aux] + [_blk_spec(a, tb) for a in toks] + [_full_spec(c) for c in consts]
                + [_blk_spec(a, tb) for a in ct_flat] + ([_blk_spec(add, tb)] if n_add else []))
    tok_dtypes = tok_dtypes or [F32] * n_tok
    out_shape = ([jax.ShapeDtypeStruct(a.shape, dt) for a, dt in zip(toks, tok_dtypes)]
                 + [jax.ShapeDtypeStruct(c.shape, F32) for c in consts])
    out_specs = [_blk_spec(a, tb) for a in toks] + [_full_spec(c) for c in consts]
    return pl.pallas_call(body, name=name, grid=(tn // tb,), in_specs=in_specs, out_specs=out_specs,
                          out_shape=out_shape, compiler_params=_cparams(1))(*ins)


MM_VMEM_BUDGET = 40 * 1024 * 1024
MM_STEP_SECONDS = 0.4e-6
MM_HBM_BYTES_PER_SECOND = 2.5e12
MM_XPOSE_SECONDS_PER_ELEM = 2e-12
MM_MXU_COLUMNS = 256
MM_MXU_FLOPS = 9e14


def _mm_tiles(m, n, kd, a_bytes, b_bytes, o_bytes, has_add, ta):
    divs = lambda d: [t for t in range(128, d + 1, 128) if d % t == 0]
    best = None
    for tm in divs(m):
        for tn in divs(n):
            for tk in divs(kd):
                ni, nj, nk = m // tm, n // tn, kd // tk
                vmem = (2 * tm * tk * a_bytes + 2 * tk * tn * b_bytes + tm * tn * 4 + 2 * tm * tn * o_bytes
                        + (2 * tm * tn * 4 if has_add else 0) + 2 * (tm * tk + tk * tn) + tm * tn * 4)
                if vmem > MM_VMEM_BUDGET:
                    continue
                a_traffic = m * kd * a_bytes * (nj if nk > 1 else 1)
                b_traffic = kd * n * b_bytes * (ni if nj * nk > 1 else 1)
                cost = ni * nj * nk * MM_STEP_SECONDS + (a_traffic + b_traffic) / MM_HBM_BYTES_PER_SECOND
                cost += 2.0 * m * kd * nj * max(tn, MM_MXU_COLUMNS) / MM_MXU_FLOPS
                if ta:
                    cost += m * kd * nj * MM_XPOSE_SECONDS_PER_ELEM
                if best is None or cost < best[0]:
                    best = (cost, tm, tn, tk)
    return best[1:]


def _mm(name, a, b, ta=False, tb=False, add=None, out_dtype=F32):
    if ta:
        kd, m = a.shape
    else:
        m, kd = a.shape
    if tb:
        n, kb = b.shape
    else:
        kb, n = b.shape
    assert kd == kb, (a.shape, b.shape)
    tm, tn, tk = _mm_tiles(m, n, kd, a.dtype.itemsize, b.dtype.itemsize, jnp.dtype(out_dtype).itemsize,
                           add is not None, ta)
    nk = kd // tk
    has_add = add is not None
    dims = (((0 if ta else 1,), (1 if tb else 0,)), ((), ()))

    def body(*refs):
        a_ref, b_ref = refs[0], refs[1]
        o_ref, acc_ref = refs[-2], refs[-1]
        k = pl.program_id(2)

        @pl.when(k == 0)
        def _():
            acc_ref[...] = refs[2][...] if has_add else jnp.zeros(acc_ref.shape, F32)

        acc_ref[...] += lax.dot_general(a_ref[...].astype(BF16), b_ref[...].astype(BF16), dims,
                                        preferred_element_type=F32)

        @pl.when(k == nk - 1)
        def _():
            o_ref[...] = acc_ref[...].astype(out_dtype)

    a_spec = pl.BlockSpec((tk, tm), lambda i, j, k: (k, i)) if ta else pl.BlockSpec((tm, tk), lambda i, j, k: (i, k))
    b_spec = pl.BlockSpec((tn, tk), lambda i, j, k: (j, k)) if tb else pl.BlockSpec((tk, tn), lambda i, j, k: (k, j))
    o_spec = pl.BlockSpec((tm, tn), lambda i, j, k: (i, j))
    ins = [a, b] + ([add] if has_add else [])
    in_specs = [a_spec, b_spec] + ([o_spec] if has_add else [])
    return pl.pallas_call(body, name=name, grid=(m // tm, n // tn, nk), in_specs=in_specs, out_specs=o_spec,
                          out_shape=jax.ShapeDtypeStruct((m, n), out_dtype),
                          scratch_shapes=[pltpu.VMEM((tm, tn), F32)], compiler_params=_cparams(3))(*ins)


def _prev8_spec(a, tb, rev_nb=None):
    r = tb // 8
    if rev_nb is None:
        return pl.BlockSpec((8, a.shape[1]), lambda i: (jnp.maximum(i * r - 1, 0), 0))
    return pl.BlockSpec((8, a.shape[1]), lambda i: (jnp.maximum((rev_nb - 1 - i) * r - 1, 0), 0))


def _pre_a_fwd(h1, p, consts, tb=TOK_BLOCK):
    tn = h1.shape[0]

    def body(h1_ref, h1h_ref, p_ref, ph_ref, *rest):
        c_refs, o_refs = rest[:len(consts)], rest[len(consts):]
        first = pl.program_id(0) == 0
        h1p = jnp.where(first, 0.0, h1h_ref[7:8, :])
        pp = jnp.where(first, 0.0, ph_ref[7:8, :])
        outs = _pre_a_fn(h1_ref[...], h1p, p_ref[...], pp, *[c[...] for c in c_refs])
        for r, o in zip(o_refs, outs):
            r[...] = o

    out_shape = [jax.ShapeDtypeStruct((tn, RWKV_WIDTH), F32) for _ in range(7)]
    return pl.pallas_call(
        body, name="rwkv_pre_a_fwd", grid=(tn // tb,),
        in_specs=[_blk_spec(h1, tb), _prev8_spec(h1, tb), _blk_spec(p, tb), _prev8_spec(p, tb)]
        + [_full_spec(c) for c in consts],
        out_specs=[_blk_spec(o, tb) for o in out_shape], out_shape=out_shape,
        compiler_params=_cparams(1))(h1, h1, p, p, *consts)


def _pre_a_bwd(h1, p, consts, cts, tb=TOK_BLOCK):
    tn = h1.shape[0]
    nb = tn // tb
    n_c = len(consts)
    ct_groups = [c if isinstance(c, (tuple, list)) else (c,) for c in cts]
    ct_flat = [a for grp in ct_groups for a in grp]
    n_ct = len(ct_flat)

    def body(*refs):
        h1_ref, h1h_ref, p_ref, ph_ref = refs[:4]
        c_refs = refs[4:4 + n_c]
        ct_refs = refs[4 + n_c:4 + n_c + n_ct]
        dh1_ref, dp_ref = refs[4 + n_c + n_ct:6 + n_c + n_ct]
        dc_refs = refs[6 + n_c + n_ct:6 + 2 * n_c + n_ct]
        ch_ref, cp_ref = refs[-2], refs[-1]
        i = pl.program_id(0)
        first_block = i == nb - 1
        h1p = jnp.where(first_block, 0.0, h1h_ref[7:8, :])
        pp = jnp.where(first_block, 0.0, ph_ref[7:8, :])
        ct_v, q = [], 0
        for grp in ct_groups:
            s = ct_refs[q][...]
            for r in ct_refs[q + 1:q + len(grp)]:
                s = s + r[...]
            q += len(grp)
            ct_v.append(s)
        _, vjp = jax.vjp(_pre_a_fn, h1_ref[...], h1p, p_ref[...], pp, *[c[...] for c in c_refs])
        grads = vjp(tuple(ct_v))

        @pl.when(i == 0)
        def _():
            ch_ref[...] = jnp.zeros(ch_ref.shape, F32)
            cp_ref[...] = jnp.zeros(cp_ref.shape, F32)
            for r in dc_refs:
                r[...] = jnp.zeros(r.shape, F32)

        rowh = lax.broadcasted_iota(jnp.int32, (tb, h1.shape[1]), 0)
        rowp = lax.broadcasted_iota(jnp.int32, (tb, p.shape[1]), 0)
        dh1_ref[...] = grads[0] + jnp.where(rowh == tb - 1, jnp.broadcast_to(ch_ref[0:1, :], rowh.shape), 0.0)
        dp_ref[...] = (grads[2] + jnp.where(rowp == tb - 1, jnp.broadcast_to(cp_ref[0:1, :], rowp.shape), 0.0)
                       ).astype(dp_ref.dtype)
        ch_ref[0:1, :] = grads[1]
        cp_ref[0:1, :] = grads[3]
        for j, r in enumerate(dc_refs):
            r[...] += grads[4 + j]

    ins = [h1, h1, p, p] + list(consts) + ct_flat
    in_specs = ([_blk_spec(h1, tb, nb), _prev8_spec(h1, tb, nb), _blk_spec(p, tb, nb), _prev8_spec(p, tb, nb)]
                + [_full_spec(c) for c in consts] + [_blk_spec(a, tb, nb) for a in ct_flat])
    out_shape = ([jax.ShapeDtypeStruct(h1.shape, F32), jax.ShapeDtypeStruct(p.shape, BF16)]
                 + [jax.ShapeDtypeStruct(c.shape, F32) for c in consts])
    out_specs = [_blk_spec(h1, tb, nb), _blk_spec(p, tb, nb)] + [_full_spec(c) for c in consts]
    return pl.pallas_call(body, name="rwkv_pre_a_bwd", grid=(nb,), in_specs=in_specs, out_specs=out_specs,
                          out_shape=out_shape,
                          scratch_shapes=[pltpu.VMEM((8, h1.shape[1]), F32), pltpu.VMEM((8, p.shape[1]), F32)],
                          compiler_params=_cparams(1))(*ins)


def _my_index():
    return 4 * lax.axis_index("x") + 2 * lax.axis_index("y") + lax.axis_index("c")


def _peer(k):
    x, y, c = lax.axis_index("x"), lax.axis_index("y"), lax.axis_index("c")
    px = 1 - x if k & 4 else x
    py = 1 - y if k & 2 else y
    pc = 1 - c if k & 1 else c
    return (px, py, pc), 4 * px + 2 * py + pc


def _xchg_sems(n):
    return [pltpu.SemaphoreType.DMA((n * (N_DEV - 1),)), pltpu.SemaphoreType.DMA((n * (N_DEV - 1),)),
            pltpu.SemaphoreType.DMA((n,))]


def _scatter_copies(srcs, dsts, sems, incoming=False):
    send_sems, recv_sems, local_sems = sems
    me = _my_index()
    local, remote = [], []
    for i, (s, d) in enumerate(zip(srcs, dsts)):
        if not incoming:
            local.append(pltpu.make_async_copy(s.at[me], d.at[me], local_sems.at[i]))
        for k in range(1, N_DEV):
            peer, plin = _peer(k)
            j = i * (N_DEV - 1) + k - 1
            s_slot, d_slot = (me, plin) if incoming else (plin, me)
            remote.append(pltpu.make_async_remote_copy(
                src_ref=s.at[s_slot], dst_ref=d.at[d_slot], send_sem=send_sems.at[j],
                recv_sem=recv_sems.at[j], device_id=peer, device_id_type=pl.DeviceIdType.MESH))
    return local, remote


def _scatter_start(srcs, dsts, sems):
    local, out = _scatter_copies(srcs, dsts, sems)
    for cp in local + out:
        cp.start()


def _scatter_wait(srcs, dsts, sems):
    for cp in _scatter_copies(srcs, dsts, sems, incoming=True)[1]:
        cp.wait_recv()
    local, out = _scatter_copies(srcs, dsts, sems)
    for cp in out:
        cp.wait_send()
    for cp in local:
        cp.wait()


_ICI_PEERS = (2, 4, 6)


def _gather_copies(srcs, dsts, sems, group):
    send_sems, recv_sems, local_sems = sems
    me = _my_index()
    sib, sib_lin = _peer(1)
    out = []
    for i, (s, d) in enumerate(zip(srcs, dsts)):
        def mk(q, src, dst, dev):
            j = i * (N_DEV - 1) + q
            return pltpu.make_async_remote_copy(src_ref=src, dst_ref=dst, send_sem=send_sems.at[j],
                                                recv_sem=recv_sems.at[j], device_id=dev,
                                                device_id_type=pl.DeviceIdType.MESH)
        if group == 'local':
            out.append(pltpu.make_async_copy(s, d.at[me], local_sems.at[i]))
        elif group == 'own':
            out.append(mk(0, s, d.at[me], sib))
        elif group == 'in_d2d':
            out.append(mk(0, s, d.at[sib_lin], sib))
        for jj, k in enumerate(_ICI_PEERS):
            peer, plin = _peer(k)
            plin_other = _peer(k + 1)[1]
            if group == 'own':
                out.append(mk(1 + jj, s, d.at[me], peer))
            elif group == 'in_ici':
                out.append(mk(1 + jj, s, d.at[plin], peer))
            elif group == 'pass_on':
                out.append(mk(4 + jj, d.at[plin], d.at[plin], sib))
            elif group == 'in_d2d':
                out.append(mk(4 + jj, d.at[plin_other], d.at[plin_other], sib))
    return out


def _gather_start(srcs, dsts, sems):
    for cp in _gather_copies(srcs, dsts, sems, 'local') + _gather_copies(srcs, dsts, sems, 'own'):
        cp.start()


def _gather_pass_on(srcs, dsts, sems):
    for cp in _gather_copies(srcs, dsts, sems, 'in_ici'):
        cp.wait_recv()
    for cp in _gather_copies(srcs, dsts, sems, 'pass_on'):
        cp.start()


def _gather_finish(srcs, dsts, sems):
    for cp in _gather_copies(srcs, dsts, sems, 'in_d2d'):
        cp.wait_recv()
    for cp in _gather_copies(srcs, dsts, sems, 'own') + _gather_copies(srcs, dsts, sems, 'pass_on'):
        cp.wait_send()
    for cp in _gather_copies(srcs, dsts, sems, 'local'):
        cp.wait()


def _xchg_out_shapes(srcs, scatter):
    return [jax.ShapeDtypeStruct(s.shape if scatter else (N_DEV,) + s.shape, s.dtype) for s in srcs]


_ANY = pl.BlockSpec(memory_space=pl.ANY)


def _exchange(name, srcs, scatter):
    n = len(srcs)

    def body(*refs):
        s, d, sems = refs[:n], refs[n:2 * n], refs[2 * n:]
        if scatter:
            _scatter_start(s, d, sems)
            _scatter_wait(s, d, sems)
        else:
            _gather_start(s, d, sems)
            _gather_pass_on(s, d, sems)
            _gather_finish(s, d, sems)

    return pl.pallas_call(body, name=name, in_specs=[_ANY] * n, out_specs=[_ANY] * n,
                          out_shape=_xchg_out_shapes(srcs, scatter), scratch_shapes=_xchg_sems(n))(*srcs)


_MM_DIMS = {'nn': (((1,), (0,)), ((), ())), 'nt': (((1,), (1,)), ((), ())), 'tn': (((0,), (0,)), ((), ()))}


def _cmm_raw(x, y, kind, split):
    dot = functools.partial(lax.dot_general, dimension_numbers=_MM_DIMS[kind], preferred_element_type=F32)
    xh, yh = x.astype(BF16), y.astype(BF16)
    out = dot(xh, yh)
    if split:
        xl = (x - xh.astype(F32)).astype(BF16)
        yl = (y - yh.astype(F32)).astype(BF16)
        out = out + (dot(xh, yl) + dot(xl, yh))
    return out


@functools.partial(jax.custom_vjp, nondiff_argnums=(2, 3))
def _cmm(x, y, kind, split=False):
    return _cmm_raw(x, y, kind, split)


def _cmm_fwd(x, y, kind, split):
    return _cmm_raw(x, y, kind, split), (x, y)


def _cmm_bwd(kind, split, res, g):
    x, y = res
    if kind == 'nn':
        return _cmm_raw(g, y, 'nt', split), _cmm_raw(x, g, 'tn', split)
    if kind == 'nt':
        return _cmm_raw(g, y, 'nn', split), _cmm_raw(g, x, 'tn', split)
    return _cmm_raw(y, g, 'nt', split), _cmm_raw(x, g, 'nn', split)


_cmm.defvjp(_cmm_fwd, _cmm_bwd)


def _tri_sum_raw(tri, x, kind):
    dot = functools.partial(lax.dot_general, dimension_numbers=_MM_DIMS[kind], preferred_element_type=F32)
    tb = tri.astype(BF16)
    hi, mid, lo = _split3(x)
    return (dot(tb, hi) + dot(tb, mid)) + dot(tb, lo)


@functools.partial(jax.custom_vjp, nondiff_argnums=(2,))
def _tri_sum(tri, x, kind):
    return _tri_sum_raw(tri, x, kind)


def _tri_sum_fwd(tri, x, kind):
    return _tri_sum_raw(tri, x, kind), tri


def _tri_sum_bwd(kind, tri, g):
    return jnp.zeros_like(tri), _tri_sum_raw(tri, g, 'tn' if kind == 'nn' else 'nn')


_tri_sum.defvjp(_tri_sum_fwd, _tri_sum_bwd)


SCAN_GROUP = 1


def _lane_group(shape, width):
    return lax.broadcasted_iota(jnp.int32, shape, 1) // width


def _block_rows(y):
    grp = _lane_group(y.shape, y.shape[1] // SCAN_GROUP)
    return jnp.concatenate([jnp.where(grp == g, y, 0.0) for g in range(SCAN_GROUP)], axis=0)


def _gmm(x, y, kind, split=False):
    if SCAN_GROUP == 1:
        return _cmm(x, y, kind, split)
    if kind != 'tn':
        return _cmm(x, _block_rows(y), kind, split)
    full = _cmm(x, y, 'tn', split)
    rows = full.shape[0] // SCAN_GROUP
    grp = _lane_group((rows, full.shape[1]), full.shape[1] // SCAN_GROUP)
    out = jnp.zeros((rows, full.shape[1]), F32)
    for g in range(SCAN_GROUP):
        out = out + jnp.where(grp == g, full[g * rows:(g + 1) * rows, :], 0.0)
    return out


def _chunk_fn(S0, r, lw, k, v, a, b):
    groups = range(len(r))
    C = SCAN_CHUNK
    n_sub = r[0].shape[0] // C
    us = [(s, g) for s in range(n_sub) for g in groups]
    rows = lambda z: {(s, g): z[g][s * C:(s + 1) * C, :] for s, g in us}
    r, lw, k, v, a, b = rows(r), rows(lw), rows(k), rows(v), rows(a), rows(b)
    ii = lax.broadcasted_iota(jnp.int32, (C, SCAN_GROUP * C), 0)
    jj = lax.broadcasted_iota(jnp.int32, (C, SCAN_GROUP * C), 1) % C
    incl, strict = ii >= jj, ii > jj
    eye = (ii == jj).astype(F32)
    inclf = incl[:, :C].astype(F32)
    cum = {u: _tri_sum(inclf, lw[u], 'nn') for u in us}
    first_half = lax.broadcasted_iota(jnp.int32, (C, 1), 0) < C // 2
    cm = {u: jnp.sum(jnp.where(first_half, lw[u], 0.0), axis=0, keepdims=True) for u in us}
    cC = {u: jnp.sum(lw[u], axis=0, keepdims=True) for u in us}
    At0 = {u: a[u] * jnp.exp(cum[u] - lw[u]) for u in us}
    Rt0 = {u: r[u] * jnp.exp(cum[u]) for u in us}
    At = {u: a[u] * jnp.exp(cum[u] - lw[u] - cm[u]) for u in us}
    Rt = {u: r[u] * jnp.exp(cum[u] - cm[u]) for u in us}
    e_inv = {u: jnp.exp(cm[u] - cum[u]) for u in us}
    Kh = {u: k[u] * e_inv[u] for u in us}
    Bh = {u: b[u] * e_inv[u] for u in us}
    e_end = {u: jnp.exp(cC[u] - cum[u]) for u in us}
    Ke = {u: k[u] * e_end[u] for u in us}
    Be = {u: b[u] * e_end[u] for u in us}
    Mab = {u: jnp.where(strict, _gmm(At[u], Bh[u], 'nt'), 0.0) for u in us}
    Mak = {u: jnp.where(strict, _gmm(At[u], Kh[u], 'nt'), 0.0) for u in us}
    Mrk = {u: jnp.where(incl, _gmm(Rt[u], Kh[u], 'nt'), 0.0) for u in us}
    Mrb = {u: jnp.where(incl, _gmm(Rt[u], Bh[u], 'nt'), 0.0) for u in us}
    mv = {u: _gmm(Mak[u], v[u], 'nn') for u in us}
    yv = {u: _gmm(Mrk[u], v[u], 'nn') for u in us}
    gC = {u: jnp.exp(cC[u]) for u in us}
    P = Mab
    Tm = {u: eye + P[u] for u in us}
    n = 1
    while 2 * n < C:
        P = {u: _gmm(P[u], P[u], 'nn', True) for u in us}
        Tm = {u: _gmm(Tm[u], eye + P[u], 'nn', True) for u in us}
        n *= 2
    S = list(S0)
    ys = {}
    for s in range(n_sub):
        U = {g: _gmm(Tm[s, g], _gmm(At0[s, g], S[g], 'nt') + mv[s, g], 'nn', True) for g in groups}
        for g in groups:
            ys[s, g] = _gmm(Rt0[s, g], S[g], 'nt') + yv[s, g] + _gmm(Mrb[s, g], U[g], 'nn')
        S = [S[g] * gC[s, g] + _gmm(v[s, g], Ke[s, g], 'tn') + _gmm(U[g], Be[s, g], 'tn') for g in groups]
    Y = tuple(jnp.concatenate([ys[s, g] for s in range(n_sub)], axis=0) for g in groups)
    return Y, tuple(S)


def _cscan_fwd(r, lw, k, v, a, b, xs):
    n_x = len(xs)
    tn = r.shape[0]
    H, Dh, Dv = RWKV_HEADS // SCAN_GROUP, RWKV_HEAD_DIM * SCAN_GROUP, RWKV_HEAD_DIM
    nc = tn // (SCAN_CHUNK * SCAN_SUB)
    lanes = lambda h: slice(h * Dh, (h + 1) * Dh)
    heads = lambda ref: tuple(ref[:, lanes(h)] for h in range(H))
    mats = lambda ref: tuple(ref[h] for h in range(H))

    def body(r_ref, lw_ref, k_ref, v_ref, a_ref, b_ref, *rest):
        x_src, (y_ref, ck_ref) = rest[:n_x], rest[n_x:n_x + 2]
        x_dst, s_ref, sems = rest[n_x + 2:2 * n_x + 2], rest[2 * n_x + 2], rest[2 * n_x + 3:]

        @pl.when(pl.program_id(0) == 0)
        def _():
            s_ref[...] = jnp.zeros(s_ref.shape, F32)
            _gather_start(x_src, x_dst, sems)

        ck_ref[0] = s_ref[...]
        y, sc = _chunk_fn(mats(s_ref), heads(r_ref), heads(lw_ref), heads(k_ref), heads(v_ref), heads(a_ref),
                          heads(b_ref))
        for h in range(H):
            y_ref[:, lanes(h)] = y[h]
            s_ref[h] = sc[h]

        @pl.when(pl.program_id(0) == max(nc - 4, 0))
        def _():
            _gather_pass_on(x_src, x_dst, sems)

        @pl.when(pl.program_id(0) == nc - 1)
        def _():
            _gather_finish(x_src, x_dst, sems)

    hm = pl.BlockSpec((SCAN_CHUNK * SCAN_SUB, H * Dh), lambda c: (c, 0))
    res = pl.pallas_call(
        body, name="rwkv_scan_fwd", grid=(nc,), in_specs=[hm] * 6 + [_ANY] * n_x,
        out_specs=[hm, pl.BlockSpec((1, H, Dv, Dh), lambda c: (c, 0, 0, 0))] + [_ANY] * n_x,
        out_shape=[jax.ShapeDtypeStruct((tn, H * Dh), F32), jax.ShapeDtypeStruct((nc, H, Dv, Dh), F32)]
        + _xchg_out_shapes(xs, False),
        scratch_shapes=[pltpu.VMEM((H, Dv, Dh), F32)] + _xchg_sems(n_x),
        compiler_params=_cparams(1))(r, lw, k, v, a, b, *xs)
    return res[0], res[1], res[2:]


def _cscan_bwd(r, lw, k, v, a, b, dy, ck, xs):
    n_x = len(xs)
    tn = r.shape[0]
    H, Dh, Dv = RWKV_HEADS // SCAN_GROUP, RWKV_HEAD_DIM * SCAN_GROUP, RWKV_HEAD_DIM
    nc = tn // (SCAN_CHUNK * SCAN_SUB)
    lanes = lambda h: slice(h * Dh, (h + 1) * Dh)
    heads = lambda ref: tuple(ref[:, lanes(h)] for h in range(H))
    mats = lambda ref: tuple(ref[h] for h in range(H))

    def body(r_ref, lw_ref, k_ref, v_ref, a_ref, b_ref, dy_ref, ck_ref, *rest):
        x_src = rest[:n_x]
        d_refs = rest[n_x:n_x + 6]
        x_dst = rest[n_x + 6:2 * n_x + 6]
        g_ref = rest[2 * n_x + 6]
        sems = rest[2 * n_x + 7:]

        @pl.when(pl.program_id(0) == 0)
        def _():
            g_ref[...] = jnp.zeros(g_ref.shape, F32)
            _scatter_start(x_src, x_dst, sems)

        s0 = tuple(ck_ref[0, h] for h in range(H))
        _, vjp = jax.vjp(_chunk_fn, s0, heads(r_ref), heads(lw_ref), heads(k_ref), heads(v_ref), heads(a_ref),
                         heads(b_ref))
        grads = vjp((heads(dy_ref), mats(g_ref)))
        for h in range(H):
            g_ref[h] = grads[0][h]
            for d_ref, gz in zip(d_refs, grads[1:]):
                d_ref[:, lanes(h)] = gz[h]

        @pl.when(pl.program_id(0) == nc - 1)
        def _():
            _scatter_wait(x_src, x_dst, sems)

    hm = pl.BlockSpec((SCAN_CHUNK * SCAN_SUB, H * Dh), lambda c: (nc - 1 - c, 0))
    hshape = jax.ShapeDtypeStruct((tn, H * Dh), F32)
    res = pl.pallas_call(
        body, name="rwkv_scan_bwd", grid=(nc,),
        in_specs=[hm] * 7 + [pl.BlockSpec((1, H, Dv, Dh), lambda c: (nc - 1 - c, 0, 0, 0))] + [_ANY] * n_x,
        out_specs=[hm] * 6 + [_ANY] * n_x, out_shape=[hshape] * 6 + _xchg_out_shapes(xs, True),
        scratch_shapes=[pltpu.VMEM((H, Dv, Dh), F32)] + _xchg_sems(n_x),
        compiler_params=_cparams(1))(r, lw, k, v, a, b, dy, ck, *xs)
    return res[:6], res[6:]


def _decay_mask(lg, i, j, blk):
    rows = lax.broadcasted_iota(jnp.int32, (blk, blk), 0)
    cols = lax.broadcasted_iota(jnp.int32, (blk, blk), 1)
    dd = (rows - cols + (i - j) * blk).astype(F32)
    return jnp.where(dd >= 0.0, jnp.exp(lg * jnp.maximum(dd, 0.0)), 0.0)


_NT = (((1,), (1,)), ((), ()))
_TN = (((0,), (0,)), ((), ()))


def _ret_attn_fwd(lg, q, k, v, blk=ATT_BLOCK):
    tn = q.shape[0]
    Dh = RET_HEAD_DIM

    def body(lg_ref, q_ref, k_ref, v_ref, o_ref):
        i = pl.program_id(1)
        lgv = lg_ref[0][:, 0:1]
        qb = q_ref[...].astype(BF16)

        def jb(j, acc):
            ks = pl.ds(pl.multiple_of(j * blk, blk), blk)
            s = lax.dot_general(qb, k_ref[ks, :].astype(BF16), _NT, preferred_element_type=F32)
            s = s * _decay_mask(lgv, i, j, blk)
            return acc + jnp.dot(s.astype(BF16), v_ref[ks, :].astype(BF16), preferred_element_type=F32)

        o_ref[...] = lax.fori_loop(0, i + 1, jb, jnp.zeros((blk, Dh), F32))

    full = pl.BlockSpec((tn, Dh), lambda h, i: (0, h))
    qs = pl.BlockSpec((blk, Dh), lambda h, i: (i, h))
    return pl.pallas_call(
        body, name="ret_attn_fwd", grid=(RET_HEADS, tn // blk),
        in_specs=[pl.BlockSpec((1, 1, 128), lambda h, i: (h, 0, 0)), qs, full, full],
        out_specs=qs, out_shape=jax.ShapeDtypeStruct(q.shape, F32), compiler_params=_cparams(2))(lg, q, k, v)


def _ret_attn_bwd(lg, q, k, v, do, blk=ATT_BLOCK):
    tn = q.shape[0]
    nb = tn // blk
    Dh = RET_HEAD_DIM

    def body(lg_ref, q_ref, k_ref, v_ref, do_ref, dq_ref, dk_ref, dv_ref):
        lgv = lg_ref[0][:, 0:1]
        dk_ref[...] = jnp.zeros(dk_ref.shape, F32)
        dv_ref[...] = jnp.zeros(dv_ref.shape, F32)

        def ib(i, carry):
            qs = pl.ds(pl.multiple_of(i * blk, blk), blk)
            qb = q_ref[qs, :].astype(BF16)
            dob = do_ref[qs, :].astype(BF16)

            def jb(j, dq):
                ks = pl.ds(pl.multiple_of(j * blk, blk), blk)
                kb = k_ref[ks, :].astype(BF16)
                vb = v_ref[ks, :].astype(BF16)
                dm = _decay_mask(lgv, i, j, blk)
                s = lax.dot_general(qb, kb, _NT, preferred_element_type=F32) * dm
                ds = lax.dot_general(dob, vb, _NT, preferred_element_type=F32) * dm
                sb, dsb = s.astype(BF16), ds.astype(BF16)
                dv_ref[ks, :] += lax.dot_general(sb, dob, _TN, preferred_element_type=F32)
                dk_ref[ks, :] += lax.dot_general(dsb, qb, _TN, preferred_element_type=F32)
                return dq + jnp.dot(dsb, kb, preferred_element_type=F32)

            dq_ref[qs, :] = lax.fori_loop(0, i + 1, jb, jnp.zeros((blk, Dh), F32))
            return carry

        lax.fori_loop(0, nb, ib, 0)

    full = pl.BlockSpec((tn, Dh), lambda h: (0, h))
    sh = jax.ShapeDtypeStruct(q.shape, F32)
    return pl.pallas_call(
        body, name="ret_attn_bwd", grid=(RET_HEADS,),
        in_specs=[pl.BlockSpec((1, 1, 128), lambda h: (h, 0, 0)), full, full, full, full],
        out_specs=[full, full, full], out_shape=[sh, sh, sh], compiler_params=_cparams(1))(lg, q, k, v, do)


def _next8_spec(a, tb):
    r = tb // 8
    last = a.shape[0] // 8 - 1
    return pl.BlockSpec((8, a.shape[1]), lambda i: (jnp.minimum((i + 1) * r, last), 0))


def _conv_taps(g_ext, cw_ref, cb_ref):
    return (cw_ref[2:3, :] * g_ext + cw_ref[1:2, :] * pltpu.roll(g_ext, 1, 0)
            + cw_ref[0:1, :] * pltpu.roll(g_ext, 2, 0) + cb_ref[...])


def _glu_fwd(gate, up, cw, cb, tb=TOK_BLOCK):
    tn = gate.shape[0]

    def body(g_ref, gh_ref, u_ref, cw_ref, cb_ref, o_ref):
        halo = jnp.where(pl.program_id(0) == 0, 0.0, gh_ref[...])
        g_ext = jnp.concatenate([halo, g_ref[...]], axis=0)
        gc = _conv_taps(g_ext, cw_ref, cb_ref)[8:, :]
        o_ref[...] = (gc * _sigmoid(gc) * u_ref[...]).astype(o_ref.dtype)

    return pl.pallas_call(
        body, name="glu_fwd", grid=(tn // tb,),
        in_specs=[_blk_spec(gate, tb), _prev8_spec(gate, tb), _blk_spec(up, tb), _full_spec(cw), _full_spec(cb)],
        out_specs=_blk_spec(gate, tb), out_shape=jax.ShapeDtypeStruct(gate.shape, BF16),
        compiler_params=_cparams(1))(gate, gate, up, cw, cb)


def _glu_bwd(gate, up, dact, cw, cb, tb=TOK_BLOCK):
    tn = gate.shape[0]
    nb = tn // tb

    def body(g_ref, gp_ref, gn_ref, u_ref, un_ref, d_ref, dn_ref, cw_ref, cb_ref, dg_ref, du_ref, dcw_ref, dcb_ref):
        i = pl.program_id(0)
        gprev = jnp.where(i == 0, 0.0, gp_ref[...])
        dnext = jnp.where(i == nb - 1, 0.0, dn_ref[...])
        g_ext = jnp.concatenate([gprev, g_ref[...], gn_ref[...]], axis=0)
        gc = _conv_taps(g_ext, cw_ref, cb_ref)[8:, :]
        u_e = jnp.concatenate([u_ref[...], un_ref[...]], axis=0)
        d_e = jnp.concatenate([d_ref[...], dnext], axis=0)
        s = _sigmoid(gc)
        dgc = d_e * u_e * (s * (1.0 + gc * (1.0 - s)))
        du_ref[...] = (d_ref[...] * (gc * s)[:tb, :]).astype(du_ref.dtype)
        n_e = tb + 8
        dg_ref[...] = (cw_ref[2:3, :] * dgc + cw_ref[1:2, :] * pltpu.roll(dgc, n_e - 1, 0)
                       + cw_ref[0:1, :] * pltpu.roll(dgc, n_e - 2, 0))[:tb, :].astype(dg_ref.dtype)

        @pl.when(i == 0)
        def _():
            dcw_ref[...] = jnp.zeros(dcw_ref.shape, F32)
            dcb_ref[...] = jnp.zeros(dcb_ref.shape, F32)

        dgc_b = dgc[:tb, :]
        g0 = g_ext[8:8 + tb, :]
        g1 = pltpu.roll(g_ext, 1, 0)[8:8 + tb, :]
        g2 = pltpu.roll(g_ext, 2, 0)[8:8 + tb, :]
        dcw_ref[2:3, :] += jnp.sum(dgc_b * g0, axis=0, keepdims=True)
        dcw_ref[1:2, :] += jnp.sum(dgc_b * g1, axis=0, keepdims=True)
        dcw_ref[0:1, :] += jnp.sum(dgc_b * g2, axis=0, keepdims=True)
        dcb_ref[...] += jnp.sum(dgc_b, axis=0, keepdims=True)

    sh = jax.ShapeDtypeStruct(gate.shape, BF16)
    return pl.pallas_call(
        body, name="glu_bwd", grid=(nb,),
        in_specs=[_blk_spec(gate, tb), _prev8_spec(gate, tb), _next8_spec(gate, tb), _blk_spec(up, tb),
                  _next8_spec(up, tb), _blk_spec(dact, tb), _next8_spec(dact, tb), _full_spec(cw), _full_spec(cb)],
        out_specs=[_blk_spec(gate, tb), _blk_spec(gate, tb), _full_spec(cw), _full_spec(cb)],
        out_shape=[sh, sh, jax.ShapeDtypeStruct(cw.shape, F32), jax.ShapeDtypeStruct(cb.shape, F32)],
        compiler_params=_cparams(1))(gate, gate, gate, up, up, dact, dact, cw, cb)


def _final_loss(x2, tgt, g, tb=TOK_BLOCK):
    tn, dm = x2.shape

    def body(x_ref, t_ref, g_ref, l_ref, dx_ref, dg_ref):
        y, vjp = jax.vjp(_rms_fn, x_ref[...], g_ref[...])
        err = y - t_ref[...]
        dx, dg = vjp(err * (1.0 / dm))

        @pl.when(pl.program_id(0) == 0)
        def _():
            l_ref[...] = jnp.zeros(l_ref.shape, F32)
            dg_ref[...] = jnp.zeros(dg_ref.shape, F32)

        part = 0.5 * jnp.sum(jnp.mean(err * err, axis=-1, keepdims=True), axis=0, keepdims=True)
        l_ref[...] += jnp.broadcast_to(part, l_ref.shape)
        dx_ref[...] = dx
        dg_ref[...] += dg

    return pl.pallas_call(
        body, name="final_loss", grid=(tn // tb,),
        in_specs=[_blk_spec(x2, tb), _blk_spec(tgt, tb), _full_spec(g)],
        out_specs=[pl.BlockSpec((8, 128), lambda i: (0, 0)), _blk_spec(x2, tb), _full_spec(g)],
        out_shape=[jax.ShapeDtypeStruct((8, 128), F32), jax.ShapeDtypeStruct(x2.shape, F32),
                   jax.ShapeDtypeStruct(g.shape, F32)],
        compiler_params=_cparams(1))(x2, tgt, g)


def _pad_cols(w, n):
    return jnp.pad(w, ((0, 0), (0, n - w.shape[1])))


def _pad_rows(w, n):
    return jnp.pad(w, ((0, n - w.shape[0]), (0, 0)))


def _local_step(x, tgt, W, late):
    tn = x.shape[0]
    Wd = RWKV_WIDTH
    row = lambda z: z.reshape(1, -1)
    g_mix, g_ffn, g_fin = row(W['norm_mix_g']), row(W['norm_ffn_g']), row(W['norm_final_g'])

    (h1,) = _tok_fwd("norm_mix_fwd", lambda a, g: (_rms_fn(a, g),), [x], [g_mix], [(D_MODEL,)])
    proj = _mm("proj_fwd", h1, W['w_in_t'], tb=True)
    p_rkv = proj[:, :3 * Wd]
    pre_consts = [row(W['rwkv_mu_w']), row(W['rwkv_mu_a']), row(W['rwkv_mu_g']), row(W['rwkv_mu_r']),
                  row(W['rwkv_mu_k']), row(W['rwkv_mu_v']), row(W['rwkv_w0']),
                  _pad_cols(W['rwkv_w1'], LORA_PAD), _pad_rows(W['rwkv_w2'], LORA_PAD), row(W['rwkv_a0']),
                  _pad_cols(W['rwkv_a1'], LORA_PAD), _pad_rows(W['rwkv_a2'], LORA_PAD),
                  W['rwkv_g1'], W['rwkv_g2'], row(W['rwkv_k_k']), row(W['rwkv_k_a'])]
    r, k, v, lw, nkk, b, g = _pre_a_fwd(h1, p_rkv, pre_consts)
    y_scan, ck, gathered = _cscan_fwd(r, lw, k, v, nkk, b, late)
    w_out, w_gate_t, w_up_t, w_down = [g_.reshape(-1, D_MODEL) for g_ in gathered]
    post_consts = [row(W['rwkv_lnx_w']), row(W['rwkv_lnx_b']), row(W['rwkv_r_k'])]
    (y_rwkv,) = _tok_fwd("rwkv_post_fwd", _rwkv_post_fn, [y_scan, r, k, v, g], post_consts, [(Wd,)],
                         out_dtypes=[BF16])

    pos = jnp.arange(tn, dtype=F32)
    half = RET_HEAD_DIM // 2
    inv_freq = ROPE_BASE ** (-jnp.arange(half, dtype=F32) / half)
    ang = pos[:, None] * inv_freq[None, :]
    cos2 = jnp.concatenate([jnp.cos(ang), jnp.cos(ang)], axis=1)
    sin2 = jnp.concatenate([-jnp.sin(ang), jnp.sin(ang)], axis=1)
    lg = jnp.log(1.0 - 2.0 ** (-5.0 - jnp.arange(RET_HEADS, dtype=F32)))
    lg = jnp.broadcast_to(lg[:, None, None], (RET_HEADS, 1, 128))
    q_p, k_p = proj[:, 3 * Wd:4 * Wd], proj[:, 4 * Wd:5 * Wd]
    v_ret, g_ret = proj[:, 5 * Wd:6 * Wd], proj[:, 6 * Wd:7 * Wd]
    q_rot, k_rot = _tok_fwd("ret_rotary_fwd", _rotary_fn, [cos2, sin2, q_p, k_p], [], [(RET_WIDTH,)] * 2)
    y_ret_raw = _ret_attn_fwd(lg, q_rot, k_rot, v_ret)
    gn_w = row(W['ret_gn_w'])
    (y_ret,) = _tok_fwd("ret_post_fwd", _ret_post_fn, [y_ret_raw, g_ret], [gn_w], [(RET_WIDTH,)],
                        out_dtypes=[BF16])

    ycat = jnp.concatenate([y_rwkv, y_ret], axis=1)
    x1 = _mm("out_proj_fwd", ycat, w_out, add=x)
    (h2,) = _tok_fwd("norm_ffn_fwd", lambda a_, g_: (_rms_fn(a_, g_),), [x1], [g_ffn], [(D_MODEL,)],
                     out_dtypes=[BF16])
    gate = _mm("ffn_gate_fwd", h2, w_gate_t, tb=True)
    up = _mm("ffn_up_fwd", h2, w_up_t, tb=True)
    cw = W['ffn_conv_w']
    cb = row(W['ffn_conv_b'])
    act = _glu_fwd(gate, up, cw, cb)
    x2 = _mm("ffn_down_fwd", act, w_down, add=x1)
    loss8, dx2, dg_fin = _final_loss(x2, tgt, g_fin)

    G = {'norm_final_g': dg_fin}
    dact = _mm("ffn_down_dx", dx2, w_down, tb=True)
    d_down = _mm("ffn_down_dw", act, dx2, ta=True, out_dtype=BF16)
    dgate, dup, dcw, dcb = _glu_bwd(gate, up, dact, cw, cb)
    G['ffn_conv_w'], G['ffn_conv_b'] = dcw, dcb
    dh2 = _mm("ffn_gate_dx", dgate, w_gate_t)
    dh2 = _mm("ffn_up_dx", dup, w_up_t, add=dh2)
    d_gate_t = _mm("ffn_gate_dw", dgate, h2, ta=True, out_dtype=BF16)
    d_up_t = _mm("ffn_up_dw", dup, h2, ta=True, out_dtype=BF16)
    dx1, G['norm_ffn_g'] = _tok_bwd("norm_ffn_bwd", lambda a_, g_: (_rms_fn(a_, g_),), [], [x1], [g_ffn], [dh2], add=dx2)
    dycat = _mm("out_proj_dx", dx1, w_out, tb=True)
    d_out = _mm("out_proj_dw", ycat, dx1, ta=True, out_dtype=BF16)
    late_grads = [z.reshape(N_DEV, -1, D_MODEL) for z in (d_out, d_gate_t, d_up_t, d_down)]
    dy_rwkv, dy_ret = dycat[:, :Wd], dycat[:, Wd:]

    dyr_raw, dg_ret, G['ret_gn_w'] = _tok_bwd("ret_post_bwd", _ret_post_fn, [], [y_ret_raw, g_ret], [gn_w], [dy_ret],
                                              tok_dtypes=[F32, BF16])
    dq_rot, dk_rot, dv_ret = _ret_attn_bwd(lg, q_rot, k_rot, v_ret, dyr_raw)
    dq_p, dk_p = _tok_bwd("ret_rotary_bwd", _rotary_fn, [cos2, sin2], [q_p, k_p], [], [dq_rot, dk_rot],
                          tok_dtypes=[BF16, BF16])

    dy_scan, dr1, dk1, dv1, dg, G['rwkv_lnx_w'], G['rwkv_lnx_b'], G['rwkv_r_k'] = _tok_bwd(
        "rwkv_post_bwd", _rwkv_post_fn, [], [y_scan, r, k, v, g], post_consts, [dy_rwkv])
    (dr2, dlw, dk2, dv2, dnkk, db), late_parts = _cscan_bwd(r, lw, k, v, nkk, b, dy_scan, ck, late_grads)
    pre_cts = [(dr1, dr2), (dk1, dk2), (dv1, dv2), dlw, dnkk, db, dg]
    pre_out = _pre_a_bwd(h1, p_rkv, pre_consts, pre_cts)
    dh1_a, dp_rkv = pre_out[0], pre_out[1]
    (G['rwkv_mu_w'], G['rwkv_mu_a'], G['rwkv_mu_g'], G['rwkv_mu_r'], G['rwkv_mu_k'], G['rwkv_mu_v'], G['rwkv_w0'],
     dw1, dw2, G['rwkv_a0'], da1, da2, G['rwkv_g1'], G['rwkv_g2'], G['rwkv_k_k'], G['rwkv_k_a']) = pre_out[2:]
    G['rwkv_w1'], G['rwkv_w2'] = dw1[:, :64], dw2[:64, :]
    G['rwkv_a1'], G['rwkv_a2'] = da1[:, :64], da2[:64, :]

    dproj = jnp.concatenate([dp_rkv, dq_p, dk_p, dv_ret.astype(BF16), dg_ret], axis=1)
    dh1 = _mm("proj_dx", dproj, W['w_in_t'], add=dh1_a)
    G['w_in_t'] = _mm("proj_dw", dproj, h1, ta=True, out_dtype=BF16)
    dx, G['norm_mix_g'] = _tok_bwd("norm_mix_bwd", lambda a_, g_: (_rms_fn(a_, g_),), [], [x], [g_mix], [dh1], add=dx1)
    return loss8[0, 0], dx, G, late_parts


def _adamw(name, parts, w, m, v):
    rows, cols = w.shape
    sub = 8 * 4 // parts.dtype.itemsize
    tb = max(t for t in range(sub, 65, sub) if rows % t == 0) if rows > 64 else rows
    c1 = 1.0 - ADAM_B1 ** ADAM_STEP
    c2 = 1.0 - ADAM_B2 ** ADAM_STEP

    def body(p_ref, w_ref, m_ref, v_ref, g_ref, d_ref, nm_ref, nv_ref):
        g = p_ref[0].astype(F32)
        for d in range(1, N_DEV):
            g = g + p_ref[d].astype(F32)
        mn = ADAM_B1 * m_ref[...] + (1.0 - ADAM_B1) * g
        vn = ADAM_B2 * v_ref[...] + (1.0 - ADAM_B2) * (g * g)
        m_hat = mn / c1
        v_hat = vn / c2
        g_ref[...] = g
        d_ref[...] = -ADAM_LR * (m_hat / (jnp.sqrt(v_hat) + ADAM_EPS) + ADAM_WD * w_ref[...])
        nm_ref[...] = mn
        nv_ref[...] = vn

    spec = pl.BlockSpec((tb, cols), lambda i: (i, 0))
    sh = jax.ShapeDtypeStruct((rows, cols), F32)
    return pl.pallas_call(
        body, name=name, grid=(rows // tb,),
        in_specs=[pl.BlockSpec((N_DEV, tb, cols), lambda i: (0, i, 0)), spec, spec, spec],
        out_specs=[spec] * 4, out_shape=[sh] * 4, compiler_params=_cparams(1))(parts, w, m, v)


def _local_shape(name):
    gs, ax = SHARDED[name]
    ls = list(gs)
    ls[ax] //= N_DEV
    return tuple(ls)


def _seg(flat, seg):
    n = flat.shape[-1]
    pad = _round_up(n, seg) - n
    if pad:
        flat = jnp.pad(flat, [(0, 0)] * (flat.ndim - 1) + [(0, pad)])
    return flat


def _split3(w):
    hi = w.astype(BF16)
    r1 = w - hi.astype(F32)
    mid = r1.astype(BF16)
    lo = (r1 - mid.astype(F32)).astype(BF16)
    return hi, mid, lo


def _pack_small_shards(shards):
    pieces = []
    for name in SMALL_NAMES:
        flat = shards[name].reshape(-1)
        if name == 'ffn_conv_w':
            pieces += [_seg(p, BF16_SEG) for p in _split3(flat)]
        else:
            pieces.append(flat.astype(BF16))
    return jnp.concatenate(pieces).reshape(-1, 128)


def _unpack_small(gathered):
    flat = gathered.reshape(N_DEV, -1)
    out, off = {}, 0
    for name in SMALL_NAMES:
        gs, ax = SHARDED[name]
        ls = _local_shape(name)
        n = int(np.prod(ls))
        if name == 'ffn_conv_w':
            nseg = _round_up(n, BF16_SEG)
            hi, mid, lo = (flat[:, off + j * nseg: off + j * nseg + n].astype(F32) for j in range(3))
            sh = ((hi + mid) + lo).reshape(N_DEV, 3, -1)
            out[name] = jnp.swapaxes(sh, 0, 1).reshape(3, D_FF)
            off += 3 * nseg
        else:
            sh = flat[:, off:off + n].reshape((N_DEV,) + ls[1:])
            out[name] = sh.reshape(gs[1:]) if ax == 1 else jnp.swapaxes(sh, 0, 1).reshape(gs[1:])
            off += n
    return out


def _small_pieces(sharded, repl):
    return [sharded[n].reshape(-1) for n in SMALL_NAMES] + [repl[n].reshape(-1) for n in REPL_NAMES]


def _pack_small_local(d):
    flat = jnp.concatenate(_small_pieces(d, d))
    return _seg(flat, F32_SEG).reshape(-1, 128)


def _pack_small_grads(G):
    pieces = []
    for name in SMALL_NAMES:
        gs, ax = SHARDED[name]
        g = G[name]
        if name == 'ffn_conv_w':
            sh = jnp.swapaxes(g.reshape(3, N_DEV, -1), 0, 1)
        elif ax == 1:
            sh = g
        else:
            sh = jnp.swapaxes(g.reshape(g.shape[0], N_DEV, -1), 0, 1)
        pieces.append(sh.reshape(N_DEV, -1))
    rep = jnp.concatenate([G[n].reshape(-1) for n in REPL_NAMES])
    pieces.append(jnp.broadcast_to(rep[None, :], (N_DEV, rep.shape[0])))
    flat = _seg(jnp.concatenate(pieces, axis=1), F32_SEG)
    return flat.reshape(N_DEV, -1, 128)


def _unpack_small_local(packed, local_shapes):
    flat = packed.reshape(-1)
    out, off = {}, 0
    for name in SMALL_NAMES + REPL_NAMES:
        n = int(np.prod(local_shapes[name]))
        out[name] = flat[off:off + n].reshape(local_shapes[name])
        off += n
    return out


def kernel(x, *rest):
    nw = len(WEIGHT_NAMES)
    assert len(rest) == 3 * nw + 1
    weights = dict(zip(WEIGHT_NAMES, rest[:nw]))
    loss_target = rest[nw]
    moms = dict(zip(WEIGHT_NAMES, rest[nw + 1:2 * nw + 1]))
    vars_ = dict(zip(WEIGHT_NAMES, rest[2 * nw + 1:]))
    local_shapes = {n: weights[n].shape for n in WEIGHT_NAMES}

    def native2d(name, a):
        a2 = a.reshape(a.shape[-2], a.shape[-1])
        return a2.T if name in BIG_T else a2

    def from2d(name, a2):
        return (a2.T if name in BIG_T else a2).reshape(local_shapes[name])

    big_w = {n: native2d(n, weights[n]) for n in BIG_NAMES}
    w_in_t_sh = big_w['w_in'].astype(BF16)
    late = [big_w[n].astype(BF16) for n in LATE_NAMES]
    small_sh = _pack_small_shards({n: weights[n] for n in SMALL_NAMES})
    w_in_g, small_g = _exchange("weights_all_gather", [w_in_t_sh, small_sh], False)
    W = _unpack_small(small_g)
    W['w_in_t'] = w_in_g.reshape(-1, D_MODEL)
    for n in REPL_NAMES:
        W[n] = weights[n][0] if n != 'norm_final_g' else weights[n]

    loss, dx, G, late_parts = _local_step(x[0], loss_target[0], W, late)

    w_in_parts, small_parts = _exchange(
        "grads_all_to_all", [G['w_in_t'].reshape(N_DEV, -1, D_MODEL), _pack_small_grads(G)], True)
    results = {}
    for n, parts in zip(['w_in'] + LATE_NAMES, [w_in_parts] + list(late_parts)):
        res = _adamw("adamw_" + n, parts, big_w[n], native2d(n, moms[n]), native2d(n, vars_[n]))
        results[n] = [from2d(n, r) for r in res]
    small_res = _adamw("adamw_small", small_parts, _pack_small_local(weights), _pack_small_local(moms),
                       _pack_small_local(vars_))
    small_out = [_unpack_small_local(p, local_shapes) for p in small_res]

    loss = lax.psum(loss, ("x", "y", "c"))
    outs = [loss, dx[None]]
    for j in range(4):
        outs += [results[n][j] if n in results else small_out[j][n] for n in WEIGHT_NAMES]
    return tuple(outs)
```

```python
import functools
import math

import numpy as np
import jax
import jax.numpy as jnp
from jax import lax
from jax.experimental import pallas as pl
from jax.experimental.pallas import tpu as pltpu

F32 = jnp.float32
BF16 = jnp.bfloat16

N_DEV = 8
D_MODEL = 1024
RWKV_HEADS = 8
RWKV_HEAD_DIM = 64
RWKV_WIDTH = 512
RET_HEADS = 4
RET_HEAD_DIM = 128
RET_WIDTH = 512
LORA_PAD = 128
D_FF = 2816
NORM_EPS = 1e-6
RWKV_GN_EPS = 64e-5
RET_GN_EPS = 1e-5
ROPE_BASE = 10000.0
ADAM_LR, ADAM_B1, ADAM_B2, ADAM_EPS, ADAM_WD, ADAM_STEP = 0.001, 0.9, 0.999, 1e-08, 0.01, 10

VMEM_LIMIT = 56 * 1024 * 1024
TOK_BLOCK = 256
SCAN_CHUNK = 64
SCAN_SUB = 1
ATT_BLOCK = 512
BF16_SEG = 2048
F32_SEG = 1024

WEIGHT_NAMES = ['norm_mix_g', 'w_in', 'rwkv_mu_r', 'rwkv_mu_k', 'rwkv_mu_v', 'rwkv_mu_w', 'rwkv_mu_a',
                'rwkv_mu_g', 'rwkv_w0', 'rwkv_w1', 'rwkv_w2', 'rwkv_a0', 'rwkv_a1', 'rwkv_a2', 'rwkv_g1',
                'rwkv_g2', 'rwkv_k_k', 'rwkv_k_a', 'rwkv_r_k', 'rwkv_lnx_w', 'rwkv_lnx_b', 'ret_gn_w',
                'w_out', 'norm_ffn_g', 'ffn_w_gate', 'ffn_w_up', 'ffn_conv_w', 'ffn_conv_b', 'ffn_w_down',
                'norm_final_g']
SHARDED = {
    'w_in': ((1, 1024, 3584), 2), 'rwkv_w1': ((1, 1024, 64), 1), 'rwkv_w2': ((1, 64, 512), 2),
    'rwkv_a1': ((1, 1024, 64), 1), 'rwkv_a2': ((1, 64, 512), 2), 'rwkv_g1': ((1, 1024, 128), 1),
    'rwkv_g2': ((1, 128, 512), 2), 'w_out': ((1, 1024, 1024), 1), 'ffn_w_gate': ((1, 1024, 2816), 2),
    'ffn_w_up': ((1, 1024, 2816), 2), 'ffn_conv_w': ((1, 3, 1, 2816), 3), 'ffn_w_down': ((1, 2816, 1024), 1),
}
REPL_NAMES = [n for n in WEIGHT_NAMES if n not in SHARDED]
BIG_NAMES = ['w_in', 'w_out', 'ffn_w_gate', 'ffn_w_up', 'ffn_w_down']
BIG_T = ('w_in', 'ffn_w_gate', 'ffn_w_up')
LATE_NAMES = ['w_out', 'ffn_w_gate', 'ffn_w_up', 'ffn_w_down']
SMALL_NAMES = [n for n in WEIGHT_NAMES if n in SHARDED and n not in BIG_NAMES]


def _cparams(n_grid):
    return pltpu.CompilerParams(dimension_semantics=("arbitrary",) * n_grid, vmem_limit_bytes=VMEM_LIMIT)


def _round_up(n, m):
    return (n + m - 1) // m * m


@jax.custom_vjp
def _bdot(x, w):
    return jnp.dot(x.astype(BF16), w.astype(BF16), preferred_element_type=F32)


def _bdot_fwd(x, w):
    return _bdot(x, w), (x, w)


def _bdot_bwd(res, g):
    x, w = res
    gb = g.astype(BF16)
    dx = lax.dot_general(gb, w.astype(BF16), (((1,), (1,)), ((), ())), preferred_element_type=F32)
    dw = lax.dot_general(x.astype(BF16), gb, (((0,), (0,)), ((), ())), preferred_element_type=F32)
    return dx, dw.astype(w.dtype)


_bdot.defvjp(_bdot_fwd, _bdot_bwd)


@jax.custom_vjp
def _shift_rows(x, prev):
    rolled = pltpu.roll(x, 1, 0)
    row = lax.broadcasted_iota(jnp.int32, x.shape, 0)
    return jnp.where(row == 0, jnp.broadcast_to(prev, x.shape), rolled)


def _shift_rows_fwd(x, prev):
    return _shift_rows(x, prev), None


def _shift_rows_bwd(_, g):
    n = g.shape[0]
    rolled = pltpu.roll(g, n - 1, 0)
    row = lax.broadcasted_iota(jnp.int32, g.shape, 0)
    return jnp.where(row == n - 1, 0.0, rolled), g[0:1, :]


_shift_rows.defvjp(_shift_rows_fwd, _shift_rows_bwd)


@jax.custom_vjp
def _swap_halves(x):
    return pltpu.roll(x, 64, 1)


_swap_halves.defvjp(lambda x: (_swap_halves(x), None), lambda _, g: (pltpu.roll(g, 64, 1),))


def _sigmoid(x):
    return 1.0 / (1.0 + jnp.exp(-x))


def _softplus(x):
    return jnp.maximum(x, 0.0) + jnp.log(1.0 + jnp.exp(-jnp.abs(x)))


def _rms_fn(x, g):
    return x * lax.rsqrt(jnp.mean(x * x, axis=-1, keepdims=True) + NORM_EPS) * g


def _pre_a_fn(h1, h1p, p, pp, mu_w, mu_a, mu_g, mu_r, mu_k, mu_v, w0, w1, w2, a0, a1, a2, g1, g2, k_k, k_a):
    W = RWKV_WIDTH
    h1s = _shift_rows(h1, h1p)
    ps = _shift_rows(p, pp)
    dx = h1s - h1
    xw = h1 + dx * mu_w
    xa = h1 + dx * mu_a
    xg = h1 + dx * mu_g
    dp = ps - p
    r = p[:, 0:W] + dp[:, 0:W] * mu_r
    k0 = p[:, W:2 * W] + dp[:, W:2 * W] * mu_k
    v = p[:, 2 * W:3 * W] + dp[:, 2 * W:3 * W] * mu_v
    wl = w0 + _bdot(jnp.tanh(_bdot(xw, w1)), w2)
    w_log = -_softplus(-wl) - 0.5
    lw = -jnp.exp(w_log)
    a = _sigmoid(a0 + _bdot(_bdot(xa, a1), a2))
    g = _bdot(_sigmoid(_bdot(xg, g1)), g2)
    nkk, k, b = _pre_b_fn(k0, a, k_k, k_a)
    return r, k, v, lw, nkk, b, g


def _head_sum_raw(x):
    n = x.shape[1]
    ii = lax.broadcasted_iota(jnp.int32, (n, n), 0) // RWKV_HEAD_DIM
    jj = lax.broadcasted_iota(jnp.int32, (n, n), 1) // RWKV_HEAD_DIM
    ones = (ii == jj).astype(BF16)
    xh = x.astype(BF16)
    xl = (x - xh.astype(F32)).astype(BF16)
    return jnp.dot(xh, ones, preferred_element_type=F32) + jnp.dot(xl, ones, preferred_element_type=F32)


@jax.custom_vjp
def _head_sum(x):
    return _head_sum_raw(x)


_head_sum.defvjp(lambda x: (_head_sum_raw(x), None), lambda _, g: (_head_sum_raw(g),))


def _pre_b_fn(k0, a, k_k, k_a):
    kkr = k0 * k_k
    nrm = jnp.sqrt(_head_sum(kkr * kkr))
    kk = kkr / jnp.maximum(nrm, 1e-12)
    k = k0 * (1.0 + (a - 1.0) * k_a)
    return -kk, k, kk * a


def _rwkv_post_fn(y, r, k, v, g, lnx_w, lnx_b, r_k):
    inv = 1.0 / RWKV_HEAD_DIM
    mu = _head_sum(y) * inv
    yc = y - mu
    var = _head_sum(yc * yc) * inv
    yn = yc * lax.rsqrt(var + RWKV_GN_EPS) * lnx_w + lnx_b
    bonus = _head_sum(r * k * r_k) * v
    return ((yn + bonus) * g,)


def _rotary_fn(cos2, sin2, q, k):
    qs, ks = [], []
    for h in range(RET_HEADS):
        sl = slice(h * RET_HEAD_DIM, (h + 1) * RET_HEAD_DIM)
        qh, kh = q[:, sl], k[:, sl]
        qs.append(qh * cos2 + _swap_halves(qh) * sin2)
        ks.append((kh * cos2 + _swap_halves(kh) * sin2) * (RET_HEAD_DIM ** -0.5))
    return jnp.concatenate(qs, axis=1), jnp.concatenate(ks, axis=1)


def _ret_post_fn(y, gp, gn_w):
    outs = []
    for h in range(RET_HEADS):
        sl = slice(h * RET_HEAD_DIM, (h + 1) * RET_HEAD_DIM)
        yh = y[:, sl]
        mu = jnp.mean(yh, axis=-1, keepdims=True)
        yc = yh - mu
        var = jnp.mean(yc * yc, axis=-1, keepdims=True)
        outs.append(yc * lax.rsqrt(var + RET_GN_EPS) * gn_w[:, sl])
    yn = jnp.concatenate(outs, axis=1)
    return (gp * _sigmoid(gp) * yn,)


def _blk_spec(a, tb, rev_nb=None):
    nd = a.ndim
    if rev_nb is None:
        return pl.BlockSpec((tb,) + a.shape[1:], lambda i: (i,) + (0,) * (nd - 1))
    return pl.BlockSpec((tb,) + a.shape[1:], lambda i: (rev_nb - 1 - i,) + (0,) * (nd - 1))


def _full_spec(a):
    nd = a.ndim
    return pl.BlockSpec(a.shape, lambda i: (0,) * nd)


def _tok_fwd(name, fn, toks, consts, out_tails, tb=TOK_BLOCK, out_dtypes=None):
    out_dtypes = out_dtypes or [F32] * len(out_tails)
    n_in = len(toks) + len(consts)
    tn = toks[0].shape[0]

    def body(*refs):
        outs = fn(*[r[...] for r in refs[:n_in]])
        for r, o in zip(refs[n_in:], outs):
            r[...] = o.astype(r.dtype)

    out_shape = [jax.ShapeDtypeStruct((tn,) + tuple(s), dt) for s, dt in zip(out_tails, out_dtypes)]
    return pl.pallas_call(
        body, name=name, grid=(tn // tb,),
        in_specs=[_blk_spec(a, tb) for a in toks] + [_full_spec(c) for c in consts],
        out_specs=[_blk_spec(o, tb) for o in out_shape], out_shape=out_shape,
        compiler_params=_cparams(1))(*toks, *consts)


def _tok_bwd(name, fn, aux, toks, consts, cts, add=None, tb=TOK_BLOCK, tok_dtypes=None):
    n_aux, n_tok, n_c = len(aux), len(toks), len(consts)
    ct_groups = [c if isinstance(c, (tuple, list)) else (c,) for c in cts]
    ct_flat = [a for grp in ct_groups for a in grp]
    n_ct = len(ct_flat)
    n_add = 0 if add is None else 1
    tn = toks[0].shape[0]

    def body(*refs):
        pos = 0
        aux_v = [r[...] for r in refs[pos:pos + n_aux]]; pos += n_aux
        tok_v = [r[...] for r in refs[pos:pos + n_tok]]; pos += n_tok
        const_v = [r[...] for r in refs[pos:pos + n_c]]; pos += n_c
        ct_refs = refs[pos:pos + n_ct]; pos += n_ct
        add_refs = refs[pos:pos + n_add]; pos += n_add
        dtok_refs = refs[pos:pos + n_tok]; pos += n_tok
        dconst_refs = refs[pos:pos + n_c]
        ct_v, q = [], 0
        for grp in ct_groups:
            s = ct_refs[q][...]
            for r in ct_refs[q + 1:q + len(grp)]:
                s = s + r[...]
            q += len(grp)
            ct_v.append(s)
        _, vjp = jax.vjp(lambda *tc: fn(*aux_v, *tc), *tok_v, *const_v)
        grads = vjp(tuple(ct_v))
        for j, r in enumerate(dtok_refs):
            gj = grads[j]
            if j == 0 and n_add:
                gj = gj + add_refs[0][...]
            r[...] = gj.astype(r.dtype)

        @pl.when(pl.program_id(0) == 0)
        def _():
            for r in dconst_refs:
                r[...] = jnp.zeros(r.shape, F32)

        for j, r in enumerate(dconst_refs):
            r[...] += grads[n_tok + j]

    ins = list(aux) + list(toks) + list(consts) + ct_flat + ([add] if n_add else [])
    in_specs = ([_blk_spec(a, tb) for a in aux] + [_blk_spec(a, tb) for a in toks] + [_full_spec(c) for c in consts]
                + [_blk_spec(a, tb) for a in ct_flat] + ([_blk_spec(add, tb)] if n_add else []))
    tok_dtypes = tok_dtypes or [F32] * n_tok
    out_shape = ([jax.ShapeDtypeStruct(a.shape, dt) for a, dt in zip(toks, tok_dtypes)]
                 + [jax.ShapeDtypeStruct(c.shape, F32) for c in consts])
    out_specs = [_blk_spec(a, tb) for a in toks] + [_full_spec(c) for c in consts]
    return pl.pallas_call(body, name=name, grid=(tn // tb,), in_specs=in_specs, out_specs=out_specs,
                          out_shape=out_shape, compiler_params=_cparams(1))(*ins)


MM_VMEM_BUDGET = 40 * 1024 * 1024
MM_STEP_SECONDS = 0.4e-6
MM_HBM_BYTES_PER_SECOND = 2.5e12
MM_XPOSE_SECONDS_PER_ELEM = 2e-12
MM_MXU_COLUMNS = 256
MM_MXU_FLOPS = 9e14


def _mm_tiles(m, n, kd, a_bytes, b_bytes, o_bytes, has_add, ta):
    divs = lambda d: [t for t in range(128, d + 1, 128) if d % t == 0]
    best = None
    for tm in divs(m):
        for tn in divs(n):
            for tk in divs(kd):
                ni, nj, nk = m // tm, n // tn, kd // tk
                vmem = (2 * tm * tk * a_bytes + 2 * tk * tn * b_bytes + tm * tn * 4 + 2 * tm * tn * o_bytes
                        + (2 * tm * tn * 4 if has_add else 0) + 2 * (tm * tk + tk * tn) + tm * tn * 4)
                if vmem > MM_VMEM_BUDGET:
                    continue
                a_traffic = m * kd * a_bytes * (nj if nk > 1 else 1)
                b_traffic = kd * n * b_bytes * (ni if nj * nk > 1 else 1)
                cost = ni * nj * nk * MM_STEP_SECONDS + (a_traffic + b_traffic) / MM_HBM_BYTES_PER_SECOND
                cost += 2.0 * m * kd * nj * max(tn, MM_MXU_COLUMNS) / MM_MXU_FLOPS
                if ta:
                    cost += m * kd * nj * MM_XPOSE_SECONDS_PER_ELEM
                if best is None or cost < best[0]:
                    best = (cost, tm, tn, tk)
    return best[1:]


def _mm(name, a, b, ta=False, tb=False, add=None, out_dtype=F32):
    if ta:
        kd, m = a.shape
    else:
        m, kd = a.shape
    if tb:
        n, kb = b.shape
    else:
        kb, n = b.shape
    assert kd == kb, (a.shape, b.shape)
    tm, tn, tk = _mm_tiles(m, n, kd, a.dtype.itemsize, b.dtype.itemsize, jnp.dtype(out_dtype).itemsize,
                           add is not None, ta)
    nk = kd // tk
    has_add = add is not None
    dims = (((0 if ta else 1,), (1 if tb else 0,)), ((), ()))

    def body(*refs):
        a_ref, b_ref = refs[0], refs[1]
        o_ref, acc_ref = refs[-2], refs[-1]
        k = pl.program_id(2)

        @pl.when(k == 0)
        def _():
            acc_ref[...] = refs[2][...] if has_add else jnp.zeros(acc_ref.shape, F32)

        acc_ref[...] += lax.dot_general(a_ref[...].astype(BF16), b_ref[...].astype(BF16), dims,
                                        preferred_element_type=F32)

        @pl.when(k == nk - 1)
        def _():
            o_ref[...] = acc_ref[...].astype(out_dtype)

    a_spec = pl.BlockSpec((tk, tm), lambda i, j, k: (k, i)) if ta else pl.BlockSpec((tm, tk), lambda i, j, k: (i, k))
    b_spec = pl.BlockSpec((tn, tk), lambda i, j, k: (j, k)) if tb else pl.BlockSpec((tk, tn), lambda i, j, k: (k, j))
    o_spec = pl.BlockSpec((tm, tn), lambda i, j, k: (i, j))
    ins = [a, b] + ([add] if has_add else [])
    in_specs = [a_spec, b_spec] + ([o_spec] if has_add else [])
    return pl.pallas_call(body, name=name, grid=(m // tm, n // tn, nk), in_specs=in_specs, out_specs=o_spec,
                          out_shape=jax.ShapeDtypeStruct((m, n), out_dtype),
                          scratch_shapes=[pltpu.VMEM((tm, tn), F32)], compiler_params=_cparams(3))(*ins)


def _prev8_spec(a, tb, rev_nb=None):
    r = tb // 8
    if rev_nb is None:
        return pl.BlockSpec((8, a.shape[1]), lambda i: (jnp.maximum(i * r - 1, 0), 0))
    return pl.BlockSpec((8, a.shape[1]), lambda i: (jnp.maximum((rev_nb - 1 - i) * r - 1, 0), 0))


def _pre_a_fwd(h1, p, consts, tb=TOK_BLOCK):
    tn = h1.shape[0]

    def body(h1_ref, h1h_ref, p_ref, ph_ref, *rest):
        c_refs, o_refs = rest[:len(consts)], rest[len(consts):]
        first = pl.program_id(0) == 0
        h1p = jnp.where(first, 0.0, h1h_ref[7:8, :])
        pp = jnp.where(first, 0.0, ph_ref[7:8, :])
        outs = _pre_a_fn(h1_ref[...], h1p, p_ref[...], pp, *[c[...] for c in c_refs])
        for r, o in zip(o_refs, outs):
            r[...] = o

    out_shape = [jax.ShapeDtypeStruct((tn, RWKV_WIDTH), F32) for _ in range(7)]
    return pl.pallas_call(
        body, name="rwkv_pre_a_fwd", grid=(tn // tb,),
        in_specs=[_blk_spec(h1, tb), _prev8_spec(h1, tb), _blk_spec(p, tb), _prev8_spec(p, tb)]
        + [_full_spec(c) for c in consts],
        out_specs=[_blk_spec(o, tb) for o in out_shape], out_shape=out_shape,
        compiler_params=_cparams(1))(h1, h1, p, p, *consts)


def _pre_a_bwd(h1, p, consts, cts, tb=TOK_BLOCK):
    tn = h1.shape[0]
    nb = tn // tb
    n_c = len(consts)
    ct_groups = [c if isinstance(c, (tuple, list)) else (c,) for c in cts]
    ct_flat = [a for grp in ct_groups for a in grp]
    n_ct = len(ct_flat)

    def body(*refs):
        h1_ref, h1h_ref, p_ref, ph_ref = refs[:4]
        c_refs = refs[4:4 + n_c]
        ct_refs = refs[4 + n_c:4 + n_c + n_ct]
        dh1_ref, dp_ref = refs[4 + n_c + n_ct:6 + n_c + n_ct]
        dc_refs = refs[6 + n_c + n_ct:6 + 2 * n_c + n_ct]
        ch_ref, cp_ref = refs[-2], refs[-1]
        i = pl.program_id(0)
        first_block = i == nb - 1
        h1p = jnp.where(first_block, 0.0, h1h_ref[7:8, :])
        pp = jnp.where(first_block, 0.0, ph_ref[7:8, :])
        ct_v, q = [], 0
        for grp in ct_groups:
            s = ct_refs[q][...]
            for r in ct_refs[q + 1:q + len(grp)]:
                s = s + r[...]
            q += len(grp)
            ct_v.append(s)
        _, vjp = jax.vjp(_pre_a_fn, h1_ref[...], h1p, p_ref[...], pp, *[c[...] for c in c_refs])
        grads = vjp(tuple(ct_v))

        @pl.when(i == 0)
        def _():
            ch_ref[...] = jnp.zeros(ch_ref.shape, F32)
            cp_ref[...] = jnp.zeros(cp_ref.shape, F32)
            for r in dc_refs:
                r[...] = jnp.zeros(r.shape, F32)

        rowh = lax.broadcasted_iota(jnp.int32, (tb, h1.shape[1]), 0)
        rowp = lax.broadcasted_iota(jnp.int32, (tb, p.shape[1]), 0)
        dh1_ref[...] = grads[0] + jnp.where(rowh == tb - 1, jnp.broadcast_to(ch_ref[0:1, :], rowh.shape), 0.0)
        dp_ref[...] = (grads[2] + jnp.where(rowp == tb - 1, jnp.broadcast_to(cp_ref[0:1, :], rowp.shape), 0.0)
                       ).astype(dp_ref.dtype)
        ch_ref[0:1, :] = grads[1]
        cp_ref[0:1, :] = grads[3]
        for j, r in enumerate(dc_refs):
            r[...] += grads[4 + j]

    ins = [h1, h1, p, p] + list(consts) + ct_flat
    in_specs = ([_blk_spec(h1, tb, nb), _prev8_spec(h1, tb, nb), _blk_spec(p, tb, nb), _prev8_spec(p, tb, nb)]
                + [_full_spec(c) for c in consts] + [_blk_spec(a, tb, nb) for a in ct_flat])
    out_shape = ([jax.ShapeDtypeStruct(h1.shape, F32), jax.ShapeDtypeStruct(p.shape, BF16)]
                 + [jax.ShapeDtypeStruct(c.shape, F32) for c in consts])
    out_specs = [_blk_spec(h1, tb, nb), _blk_spec(p, tb, nb)] + [_full_spec(c) for c in consts]
    return pl.pallas_call(body, name="rwkv_pre_a_bwd", grid=(nb,), in_specs=in_specs, out_specs=out_specs,
                          out_shape=out_shape,
                          scratch_shapes=[pltpu.VMEM((8, h1.shape[1]), F32), pltpu.VMEM((8, p.shape[1]), F32)],
                          compiler_params=_cparams(1))(*ins)


def _my_index():
    return 4 * lax.axis_index("x") + 2 * lax.axis_index("y") + lax.axis_index("c")


def _peer(k):
    x, y, c = lax.axis_index("x"), lax.axis_index("y"), lax.axis_index("c")
    px = 1 - x if k & 4 else x
    py = 1 - y if k & 2 else y
    pc = 1 - c if k & 1 else c
    return (px, py, pc), 4 * px + 2 * py + pc


def _xchg_sems(n):
    return [pltpu.SemaphoreType.DMA((n * (N_DEV - 1),)), pltpu.SemaphoreType.DMA((n * (N_DEV - 1),)),
            pltpu.SemaphoreType.DMA((n,))]


def _scatter_copies(srcs, dsts, sems, incoming=False):
    send_sems, recv_sems, local_sems = sems
    me = _my_index()
    local, remote = [], []
    for i, (s, d) in enumerate(zip(srcs, dsts)):
        if not incoming:
            local.append(pltpu.make_async_copy(s.at[me], d.at[me], local_sems.at[i]))
        for k in range(1, N_DEV):
            peer, plin = _peer(k)
            j = i * (N_DEV - 1) + k - 1
            s_slot, d_slot = (me, plin) if incoming else (plin, me)
            remote.append(pltpu.make_async_remote_copy(
                src_ref=s.at[s_slot], dst_ref=d.at[d_slot], send_sem=send_sems.at[j],
                recv_sem=recv_sems.at[j], device_id=peer, device_id_type=pl.DeviceIdType.MESH))
    return local, remote


def _scatter_start(srcs, dsts, sems):
    local, out = _scatter_copies(srcs, dsts, sems)
    for cp in local + out:
        cp.start()


def _scatter_wait(srcs, dsts, sems):
    for cp in _scatter_copies(srcs, dsts, sems, incoming=True)[1]:
        cp.wait_recv()
    local, out = _scatter_copies(srcs, dsts, sems)
    for cp in out:
        cp.wait_send()
    for cp in local:
        cp.wait()


_ICI_PEERS = (2, 4, 6)


def _gather_copies(srcs, dsts, sems, group):
    send_sems, recv_sems, local_sems = sems
    me = _my_index()
    sib, sib_lin = _peer(1)
    out = []
    for i, (s, d) in enumerate(zip(srcs, dsts)):
        def mk(q, src, dst, dev):
            j = i * (N_DEV - 1) + q
            return pltpu.make_async_remote_copy(src_ref=src, dst_ref=dst, send_sem=send_sems.at[j],
                                                recv_sem=recv_sems.at[j], device_id=dev,
                                                device_id_type=pl.DeviceIdType.MESH)
        if group == 'local':
            out.append(pltpu.make_async_copy(s, d.at[me], local_sems.at[i]))
        elif group == 'own':
            out.append(mk(0, s, d.at[me], sib))
        elif group == 'in_d2d':
            out.append(mk(0, s, d.at[sib_lin], sib))
        for jj, k in enumerate(_ICI_PEERS):
            peer, plin = _peer(k)
            plin_other = _peer(k + 1)[1]
            if group == 'own':
                out.append(mk(1 + jj, s, d.at[me], peer))
            elif group == 'in_ici':
                out.append(mk(1 + jj, s, d.at[plin], peer))
            elif group == 'pass_on':
                out.append(mk(4 + jj, d.at[plin], d.at[plin], sib))
            elif group == 'in_d2d':
                out.append(mk(4 + jj, d.at[plin_other], d.at[plin_other], sib))
    return out


def _gather_start(srcs, dsts, sems):
    for cp in _gather_copies(srcs, dsts, sems, 'local') + _gather_copies(srcs, dsts, sems, 'own'):
        cp.start()


def _gather_pass_on(srcs, dsts, sems):
    for cp in _gather_copies(srcs, dsts, sems, 'in_ici'):
        cp.wait_recv()
    for cp in _gather_copies(srcs, dsts, sems, 'pass_on'):
        cp.start()


def _gather_finish(srcs, dsts, sems):
    for cp in _gather_copies(srcs, dsts, sems, 'in_d2d'):
        cp.wait_recv()
    for cp in _gather_copies(srcs, dsts, sems, 'own') + _gather_copies(srcs, dsts, sems, 'pass_on'):
        cp.wait_send()
    for cp in _gather_copies(srcs, dsts, sems, 'local'):
        cp.wait()


def _xchg_out_shapes(srcs, scatter):
    return [jax.ShapeDtypeStruct(s.shape if scatter else (N_DEV,) + s.shape, s.dtype) for s in srcs]


_ANY = pl.BlockSpec(memory_space=pl.ANY)


def _exchange(name, srcs, scatter):
    n = len(srcs)

    def body(*refs):
        s, d, sems = refs[:n], refs[n:2 * n], refs[2 * n:]
        if scatter:
            _scatter_start(s, d, sems)
            _scatter_wait(s, d, sems)
        else:
            _gather_start(s, d, sems)
            _gather_pass_on(s, d, sems)
            _gather_finish(s, d, sems)

    return pl.pallas_call(body, name=name, in_specs=[_ANY] * n, out_specs=[_ANY] * n,
                          out_shape=_xchg_out_shapes(srcs, scatter), scratch_shapes=_xchg_sems(n))(*srcs)


_MM_DIMS = {'nn': (((1,), (0,)), ((), ())), 'nt': (((1,), (1,)), ((), ())), 'tn': (((0,), (0,)), ((), ()))}


def _cmm_raw(x, y, kind, split):
    dot = functools.partial(lax.dot_general, dimension_numbers=_MM_DIMS[kind], preferred_element_type=F32)
    xh, yh = x.astype(BF16), y.astype(BF16)
    out = dot(xh, yh)
    if split:
        xl = (x - xh.astype(F32)).astype(BF16)
        yl = (y - yh.astype(F32)).astype(BF16)
        out = out + (dot(xh, yl) + dot(xl, yh))
    return out


@functools.partial(jax.custom_vjp, nondiff_argnums=(2, 3))
def _cmm(x, y, kind, split=False):
    return _cmm_raw(x, y, kind, split)


def _cmm_fwd(x, y, kind, split):
    return _cmm_raw(x, y, kind, split), (x, y)


def _cmm_bwd(kind, split, res, g):
    x, y = res
    if kind == 'nn':
        return _cmm_raw(g, y, 'nt', split), _cmm_raw(x, g, 'tn', split)
    if kind == 'nt':
        return _cmm_raw(g, y, 'nn', split), _cmm_raw(g, x, 'tn', split)
    return _cmm_raw(y, g, 'nt', split), _cmm_raw(x, g, 'nn', split)


_cmm.defvjp(_cmm_fwd, _cmm_bwd)


def _tri_sum_raw(tri, x, kind):
    dot = functools.partial(lax.dot_general, dimension_numbers=_MM_DIMS[kind], preferred_element_type=F32)
    tb = tri.astype(BF16)
    hi, mid, lo = _split3(x)
    return (dot(tb, hi) + dot(tb, mid)) + dot(tb, lo)


@functools.partial(jax.custom_vjp, nondiff_argnums=(2,))
def _tri_sum(tri, x, kind):
    return _tri_sum_raw(tri, x, kind)


def _tri_sum_fwd(tri, x, kind):
    return _tri_sum_raw(tri, x, kind), tri


def _tri_sum_bwd(kind, tri, g):
    return jnp.zeros_like(tri), _tri_sum_raw(tri, g, 'tn' if kind == 'nn' else 'nn')


_tri_sum.defvjp(_tri_sum_fwd, _tri_sum_bwd)


SCAN_GROUP = 1


def _lane_group(shape, width):
    return lax.broadcasted_iota(jnp.int32, shape, 1) // width


def _block_rows(y):
    grp = _lane_group(y.shape, y.shape[1] // SCAN_GROUP)
    return jnp.concatenate([jnp.where(grp == g, y, 0.0) for g in range(SCAN_GROUP)], axis=0)


def _gmm(x, y, kind, split=False):
    if SCAN_GROUP == 1:
        return _cmm(x, y, kind, split)
    if kind != 'tn':
        return _cmm(x, _block_rows(y), kind, split)
    full = _cmm(x, y, 'tn', split)
    rows = full.shape[0] // SCAN_GROUP
    grp = _lane_group((rows, full.shape[1]), full.shape[1] // SCAN_GROUP)
    out = jnp.zeros((rows, full.shape[1]), F32)
    for g in range(SCAN_GROUP):
        out = out + jnp.where(grp == g, full[g * rows:(g + 1) * rows, :], 0.0)
    return out


def _chunk_fn(S0, r, lw, k, v, a, b):
    groups = range(len(r))
    C = SCAN_CHUNK
    n_sub = r[0].shape[0] // C
    us = [(s, g) for s in range(n_sub) for g in groups]
    rows = lambda z: {(s, g): z[g][s * C:(s + 1) * C, :] for s, g in us}
    r, lw, k, v, a, b = rows(r), rows(lw), rows(k), rows(v), rows(a), rows(b)
    ii = lax.broadcasted_iota(jnp.int32, (C, SCAN_GROUP * C), 0)
    jj = lax.broadcasted_iota(jnp.int32, (C, SCAN_GROUP * C), 1) % C
    incl, strict = ii >= jj, ii > jj
    eye = (ii == jj).astype(F32)
    inclf = incl[:, :C].astype(F32)
    cum = {u: _tri_sum(inclf, lw[u], 'nn') for u in us}
    e_inv = {u: jnp.exp(-cum[u]) for u in us}
    At = {u: a[u] * jnp.exp(cum[u] - lw[u]) for u in us}
    Rt = {u: r[u] * jnp.exp(cum[u]) for u in us}
    Kh = {u: k[u] * e_inv[u] for u in us}
    Bh = {u: b[u] * e_inv[u] for u in us}
    Mab = {u: jnp.where(strict, _gmm(At[u], Bh[u], 'nt'), 0.0) for u in us}
    Mak = {u: jnp.where(strict, _gmm(At[u], Kh[u], 'nt'), 0.0) for u in us}
    Mrk = {u: jnp.where(incl, _gmm(Rt[u], Kh[u], 'nt'), 0.0) for u in us}
    Mrb = {u: jnp.where(incl, _gmm(Rt[u], Bh[u], 'nt'), 0.0) for u in us}
    mv = {u: _gmm(Mak[u], v[u], 'nn') for u in us}
    yv = {u: _gmm(Mrk[u], v[u], 'nn') for u in us}
    gC = {u: jnp.exp(jnp.sum(lw[u], axis=0, keepdims=True)) for u in us}
    P = Mab
    Tm = {u: eye + P[u] for u in us}
    n = 1
    while 2 * n < C:
        P = {u: _gmm(P[u], P[u], 'nn', True) for u in us}
        Tm = {u: _gmm(Tm[u], eye + P[u], 'nn', True) for u in us}
        n *= 2
    S = list(S0)
    ys = {}
    for s in range(n_sub):
        U = {g: _gmm(Tm[s, g], _gmm(At[s, g], S[g], 'nt') + mv[s, g], 'nn', True) for g in groups}
        for g in groups:
            ys[s, g] = _gmm(Rt[s, g], S[g], 'nt') + yv[s, g] + _gmm(Mrb[s, g], U[g], 'nn')
        S = [S[g] * gC[s, g] + _gmm(v[s, g], Kh[s, g] * gC[s, g], 'tn') + _gmm(U[g], Bh[s, g] * gC[s, g], 'tn')
             for g in groups]
    Y = tuple(jnp.concatenate([ys[s, g] for s in range(n_sub)], axis=0) for g in groups)
    return Y, tuple(S)


def _cscan_fwd(r, lw, k, v, a, b, xs):
    n_x = len(xs)
    tn = r.shape[0]
    H, Dh, Dv = RWKV_HEADS // SCAN_GROUP, RWKV_HEAD_DIM * SCAN_GROUP, RWKV_HEAD_DIM
    nc = tn // (SCAN_CHUNK * SCAN_SUB)
    lanes = lambda h: slice(h * Dh, (h + 1) * Dh)
    heads = lambda ref: tuple(ref[:, lanes(h)] for h in range(H))
    mats = lambda ref: tuple(ref[h] for h in range(H))

    def body(r_ref, lw_ref, k_ref, v_ref, a_ref, b_ref, *rest):
        x_src, (y_ref, ck_ref) = rest[:n_x], rest[n_x:n_x + 2]
        x_dst, s_ref, sems = rest[n_x + 2:2 * n_x + 2], rest[2 * n_x + 2], rest[2 * n_x + 3:]

        @pl.when(pl.program_id(0) == 0)
        def _():
            s_ref[...] = jnp.zeros(s_ref.shape, F32)
            _gather_start(x_src, x_dst, sems)

        ck_ref[0] = s_ref[...]
        y, sc = _chunk_fn(mats(s_ref), heads(r_ref), heads(lw_ref), heads(k_ref), heads(v_ref), heads(a_ref),
                          heads(b_ref))
        for h in range(H):
            y_ref[:, lanes(h)] = y[h]
            s_ref[h] = sc[h]

        @pl.when(pl.program_id(0) == max(nc - 4, 0))
        def _():
            _gather_pass_on(x_src, x_dst, sems)

        @pl.when(pl.program_id(0) == nc - 1)
        def _():
            _gather_finish(x_src, x_dst, sems)

    hm = pl.BlockSpec((SCAN_CHUNK * SCAN_SUB, H * Dh), lambda c: (c, 0))
    res = pl.pallas_call(
        body, name="rwkv_scan_fwd", grid=(nc,), in_specs=[hm] * 6 + [_ANY] * n_x,
        out_specs=[hm, pl.BlockSpec((1, H, Dv, Dh), lambda c: (c, 0, 0, 0))] + [_ANY] * n_x,
        out_shape=[jax.ShapeDtypeStruct((tn, H * Dh), F32), jax.ShapeDtypeStruct((nc, H, Dv, Dh), F32)]
        + _xchg_out_shapes(xs, False),
        scratch_shapes=[pltpu.VMEM((H, Dv, Dh), F32)] + _xchg_sems(n_x),
        compiler_params=_cparams(1))(r, lw, k, v, a, b, *xs)
    return res[0], res[1], res[2:]


def _cscan_bwd(r, lw, k, v, a, b, dy, ck, xs):
    n_x = len(xs)
    tn = r.shape[0]
    H, Dh, Dv = RWKV_HEADS // SCAN_GROUP, RWKV_HEAD_DIM * SCAN_GROUP, RWKV_HEAD_DIM
    nc = tn // (SCAN_CHUNK * SCAN_SUB)
    lanes = lambda h: slice(h * Dh, (h + 1) * Dh)
    heads = lambda ref: tuple(ref[:, lanes(h)] for h in range(H))
    mats = lambda ref: tuple(ref[h] for h in range(H))

    def body(r_ref, lw_ref, k_ref, v_ref, a_ref, b_ref, dy_ref, ck_ref, *rest):
        x_src = rest[:n_x]
        d_refs = rest[n_x:n_x + 6]
        x_dst = rest[n_x + 6:2 * n_x + 6]
        g_ref = rest[2 * n_x + 6]
        sems = rest[2 * n_x + 7:]

        @pl.when(pl.program_id(0) == 0)
        def _():
            g_ref[...] = jnp.zeros(g_ref.shape, F32)
            _scatter_start(x_src, x_dst, sems)

        s0 = tuple(ck_ref[0, h] for h in range(H))
        _, vjp = jax.vjp(_chunk_fn, s0, heads(r_ref), heads(lw_ref), heads(k_ref), heads(v_ref), heads(a_ref),
                         heads(b_ref))
        grads = vjp((heads(dy_ref), mats(g_ref)))
        for h in range(H):
            g_ref[h] = grads[0][h]
            for d_ref, gz in zip(d_refs, grads[1:]):
                d_ref[:, lanes(h)] = gz[h]

        @pl.when(pl.program_id(0) == nc - 1)
        def _():
            _scatter_wait(x_src, x_dst, sems)

    hm = pl.BlockSpec((SCAN_CHUNK * SCAN_SUB, H * Dh), lambda c: (nc - 1 - c, 0))
    hshape = jax.ShapeDtypeStruct((tn, H * Dh), F32)
    res = pl.pallas_call(
        body, name="rwkv_scan_bwd", grid=(nc,),
        in_specs=[hm] * 7 + [pl.BlockSpec((1, H, Dv, Dh), lambda c: (nc - 1 - c, 0, 0, 0))] + [_ANY] * n_x,
        out_specs=[hm] * 6 + [_ANY] * n_x, out_shape=[hshape] * 6 + _xchg_out_shapes(xs, True),
        scratch_shapes=[pltpu.VMEM((H, Dv, Dh), F32)] + _xchg_sems(n_x),
        compiler_params=_cparams(1))(r, lw, k, v, a, b, dy, ck, *xs)
    return res[:6], res[6:]


def _decay_mask(lg, i, j, blk):
    rows = lax.broadcasted_iota(jnp.int32, (blk, blk), 0)
    cols = lax.broadcasted_iota(jnp.int32, (blk, blk), 1)
    dd = (rows - cols + (i - j) * blk).astype(F32)
    return jnp.where(dd >= 0.0, jnp.exp(lg * jnp.maximum(dd, 0.0)), 0.0)


_NT = (((1,), (1,)), ((), ()))
_TN = (((0,), (0,)), ((), ()))


def _ret_attn_fwd(lg, q, k, v, blk=ATT_BLOCK):
    tn = q.shape[0]
    Dh = RET_HEAD_DIM

    def body(lg_ref, q_ref, k_ref, v_ref, o_ref):
        i = pl.program_id(1)
        lgv = lg_ref[0][:, 0:1]
        qb = q_ref[...].astype(BF16)

        def jb(j, acc):
            ks = pl.ds(pl.multiple_of(j * blk, blk), blk)
            s = lax.dot_general(qb, k_ref[ks, :].astype(BF16), _NT, preferred_element_type=F32)
            s = s * _decay_mask(lgv, i, j, blk)
            return acc + jnp.dot(s.astype(BF16), v_ref[ks, :].astype(BF16), preferred_element_type=F32)

        o_ref[...] = lax.fori_loop(0, i + 1, jb, jnp.zeros((blk, Dh), F32))

    full = pl.BlockSpec((tn, Dh), lambda h, i: (0, h))
    qs = pl.BlockSpec((blk, Dh), lambda h, i: (i, h))
    return pl.pallas_call(
        body, name="ret_attn_fwd", grid=(RET_HEADS, tn // blk),
        in_specs=[pl.BlockSpec((1, 1, 128), lambda h, i: (h, 0, 0)), qs, full, full],
        out_specs=qs, out_shape=jax.ShapeDtypeStruct(q.shape, F32), compiler_params=_cparams(2))(lg, q, k, v)


def _ret_attn_bwd(lg, q, k, v, do, blk=ATT_BLOCK):
    tn = q.shape[0]
    nb = tn // blk
    Dh = RET_HEAD_DIM

    def body(lg_ref, q_ref, k_ref, v_ref, do_ref, dq_ref, dk_ref, dv_ref):
        lgv = lg_ref[0][:, 0:1]
        dk_ref[...] = jnp.zeros(dk_ref.shape, F32)
        dv_ref[...] = jnp.zeros(dv_ref.shape, F32)

        def ib(i, carry):
            qs = pl.ds(pl.multiple_of(i * blk, blk), blk)
            qb = q_ref[qs, :].astype(BF16)
            dob = do_ref[qs, :].astype(BF16)

            def jb(j, dq):
                ks = pl.ds(pl.multiple_of(j * blk, blk), blk)
                kb = k_ref[ks, :].astype(BF16)
                vb = v_ref[ks, :].astype(BF16)
                dm = _decay_mask(lgv, i, j, blk)
                s = lax.dot_general(qb, kb, _NT, preferred_element_type=F32) * dm
                ds = lax.dot_general(dob, vb, _NT, preferred_element_type=F32) * dm
                sb, dsb = s.astype(BF16), ds.astype(BF16)
                dv_ref[ks, :] += lax.dot_general(sb, dob, _TN, preferred_element_type=F32)
                dk_ref[ks, :] += lax.dot_general(dsb, qb, _TN, preferred_element_type=F32)
                return dq + jnp.dot(dsb, kb, preferred_element_type=F32)

            dq_ref[qs, :] = lax.fori_loop(0, i + 1, jb, jnp.zeros((blk, Dh), F32))
            return carry

        lax.fori_loop(0, nb, ib, 0)

    full = pl.BlockSpec((tn, Dh), lambda h: (0, h))
    sh = jax.ShapeDtypeStruct(q.shape, F32)
    return pl.pallas_call(
        body, name="ret_attn_bwd", grid=(RET_HEADS,),
        in_specs=[pl.BlockSpec((1, 1, 128), lambda h: (h, 0, 0)), full, full, full, full],
        out_specs=[full, full, full], out_shape=[sh, sh, sh], compiler_params=_cparams(1))(lg, q, k, v, do)


def _next8_spec(a, tb):
    r = tb // 8
    last = a.shape[0] // 8 - 1
    return pl.BlockSpec((8, a.shape[1]), lambda i: (jnp.minimum((i + 1) * r, last), 0))


def _conv_taps(g_ext, cw_ref, cb_ref):
    return (cw_ref[2:3, :] * g_ext + cw_ref[1:2, :] * pltpu.roll(g_ext, 1, 0)
            + cw_ref[0:1, :] * pltpu.roll(g_ext, 2, 0) + cb_ref[...])


def _glu_fwd(gate, up, cw, cb, tb=TOK_BLOCK):
    tn = gate.shape[0]

    def body(g_ref, gh_ref, u_ref, cw_ref, cb_ref, o_ref):
        halo = jnp.where(pl.program_id(0) == 0, 0.0, gh_ref[...])
        g_ext = jnp.concatenate([halo, g_ref[...]], axis=0)
        gc = _conv_taps(g_ext, cw_ref, cb_ref)[8:, :]
        o_ref[...] = (gc * _sigmoid(gc) * u_ref[...]).astype(o_ref.dtype)

    return pl.pallas_call(
        body, name="glu_fwd", grid=(tn // tb,),
        in_specs=[_blk_spec(gate, tb), _prev8_spec(gate, tb), _blk_spec(up, tb), _full_spec(cw), _full_spec(cb)],
        out_specs=_blk_spec(gate, tb), out_shape=jax.ShapeDtypeStruct(gate.shape, BF16),
        compiler_params=_cparams(1))(gate, gate, up, cw, cb)


def _glu_bwd(gate, up, dact, cw, cb, tb=TOK_BLOCK):
    tn = gate.shape[0]
    nb = tn // tb

    def body(g_ref, gp_ref, gn_ref, u_ref, un_ref, d_ref, dn_ref, cw_ref, cb_ref, dg_ref, du_ref, dcw_ref, dcb_ref):
        i = pl.program_id(0)
        gprev = jnp.where(i == 0, 0.0, gp_ref[...])
        dnext = jnp.where(i == nb - 1, 0.0, dn_ref[...])
        g_ext = jnp.concatenate([gprev, g_ref[...], gn_ref[...]], axis=0)
        gc = _conv_taps(g_ext, cw_ref, cb_ref)[8:, :]
        u_e = jnp.concatenate([u_ref[...], un_ref[...]], axis=0)
        d_e = jnp.concatenate([d_ref[...], dnext], axis=0)
        s = _sigmoid(gc)
        dgc = d_e * u_e * (s * (1.0 + gc * (1.0 - s)))
        du_ref[...] = (d_ref[...] * (gc * s)[:tb, :]).astype(du_ref.dtype)
        n_e = tb + 8
        dg_ref[...] = (cw_ref[2:3, :] * dgc + cw_ref[1:2, :] * pltpu.roll(dgc, n_e - 1, 0)
                       + cw_ref[0:1, :] * pltpu.roll(dgc, n_e - 2, 0))[:tb, :].astype(dg_ref.dtype)

        @pl.when(i == 0)
        def _():
            dcw_ref[...] = jnp.zeros(dcw_ref.shape, F32)
            dcb_ref[...] = jnp.zeros(dcb_ref.shape, F32)

        dgc_b = dgc[:tb, :]
        g0 = g_ext[8:8 + tb, :]
        g1 = pltpu.roll(g_ext, 1, 0)[8:8 + tb, :]
        g2 = pltpu.roll(g_ext, 2, 0)[8:8 + tb, :]
        dcw_ref[2:3, :] += jnp.sum(dgc_b * g0, axis=0, keepdims=True)
        dcw_ref[1:2, :] += jnp.sum(dgc_b * g1, axis=0, keepdims=True)
        dcw_ref[0:1, :] += jnp.sum(dgc_b * g2, axis=0, keepdims=True)
        dcb_ref[...] += jnp.sum(dgc_b, axis=0, keepdims=True)

    sh = jax.ShapeDtypeStruct(gate.shape, BF16)
    return pl.pallas_call(
        body, name="glu_bwd", grid=(nb,),
        in_specs=[_blk_spec(gate, tb), _prev8_spec(gate, tb), _next8_spec(gate, tb), _blk_spec(up, tb),
                  _next8_spec(up, tb), _blk_spec(dact, tb), _next8_spec(dact, tb), _full_spec(cw), _full_spec(cb)],
        out_specs=[_blk_spec(gate, tb), _blk_spec(gate, tb), _full_spec(cw), _full_spec(cb)],
        out_shape=[sh, sh, jax.ShapeDtypeStruct(cw.shape, F32), jax.ShapeDtypeStruct(cb.shape, F32)],
        compiler_params=_cparams(1))(gate, gate, gate, up, up, dact, dact, cw, cb)


def _final_loss(x2, tgt, g, tb=TOK_BLOCK):
    tn, dm = x2.shape

    def body(x_ref, t_ref, g_ref, l_ref, dx_ref, dg_ref):
        y, vjp = jax.vjp(_rms_fn, x_ref[...], g_ref[...])
        err = y - t_ref[...]
        dx, dg = vjp(err * (1.0 / dm))

        @pl.when(pl.program_id(0) == 0)
        def _():
            l_ref[...] = jnp.zeros(l_ref.shape, F32)
            dg_ref[...] = jnp.zeros(dg_ref.shape, F32)

        part = 0.5 * jnp.sum(jnp.mean(err * err, axis=-1, keepdims=True), axis=0, keepdims=True)
        l_ref[...] += jnp.broadcast_to(part, l_ref.shape)
        dx_ref[...] = dx
        dg_ref[...] += dg

    return pl.pallas_call(
        body, name="final_loss", grid=(tn // tb,),
        in_specs=[_blk_spec(x2, tb), _blk_spec(tgt, tb), _full_spec(g)],
        out_specs=[pl.BlockSpec((8, 128), lambda i: (0, 0)), _blk_spec(x2, tb), _full_spec(g)],
        out_shape=[jax.ShapeDtypeStruct((8, 128), F32), jax.ShapeDtypeStruct(x2.shape, F32),
                   jax.ShapeDtypeStruct(g.shape, F32)],
        compiler_params=_cparams(1))(x2, tgt, g)


def _pad_cols(w, n):
    return jnp.pad(w, ((0, 0), (0, n - w.shape[1])))


def _pad_rows(w, n):
    return jnp.pad(w, ((0, n - w.shape[0]), (0, 0)))


def _local_step(x, tgt, W, late):
    tn = x.shape[0]
    Wd = RWKV_WIDTH
    row = lambda z: z.reshape(1, -1)
    g_mix, g_ffn, g_fin = row(W['norm_mix_g']), row(W['norm_ffn_g']), row(W['norm_final_g'])

    (h1,) = _tok_fwd("norm_mix_fwd", lambda a, g: (_rms_fn(a, g),), [x], [g_mix], [(D_MODEL,)])
    proj = _mm("proj_fwd", h1, W['w_in_t'], tb=True)
    p_rkv = proj[:, :3 * Wd]
    pre_consts = [row(W['rwkv_mu_w']), row(W['rwkv_mu_a']), row(W['rwkv_mu_g']), row(W['rwkv_mu_r']),
                  row(W['rwkv_mu_k']), row(W['rwkv_mu_v']), row(W['rwkv_w0']),
                  _pad_cols(W['rwkv_w1'], LORA_PAD), _pad_rows(W['rwkv_w2'], LORA_PAD), row(W['rwkv_a0']),
                  _pad_cols(W['rwkv_a1'], LORA_PAD), _pad_rows(W['rwkv_a2'], LORA_PAD),
                  W['rwkv_g1'], W['rwkv_g2'], row(W['rwkv_k_k']), row(W['rwkv_k_a'])]
    r, k, v, lw, nkk, b, g = _pre_a_fwd(h1, p_rkv, pre_consts)
    y_scan, ck, gathered = _cscan_fwd(r, lw, k, v, nkk, b, late)
    w_out, w_gate_t, w_up_t, w_down = [g_.reshape(-1, D_MODEL) for g_ in gathered]
    post_consts = [row(W['rwkv_lnx_w']), row(W['rwkv_lnx_b']), row(W['rwkv_r_k'])]
    (y_rwkv,) = _tok_fwd("rwkv_post_fwd", _rwkv_post_fn, [y_scan, r, k, v, g], post_consts, [(Wd,)],
                         out_dtypes=[BF16])

    pos = jnp.arange(tn, dtype=F32)
    half = RET_HEAD_DIM // 2
    inv_freq = ROPE_BASE ** (-jnp.arange(half, dtype=F32) / half)
    ang = pos[:, None] * inv_freq[None, :]
    cos2 = jnp.concatenate([jnp.cos(ang), jnp.cos(ang)], axis=1)
    sin2 = jnp.concatenate([-jnp.sin(ang), jnp.sin(ang)], axis=1)
    lg = jnp.log(1.0 - 2.0 ** (-5.0 - jnp.arange(RET_HEADS, dtype=F32)))
    lg = jnp.broadcast_to(lg[:, None, None], (RET_HEADS, 1, 128))
    q_p, k_p = proj[:, 3 * Wd:4 * Wd], proj[:, 4 * Wd:5 * Wd]
    v_ret, g_ret = proj[:, 5 * Wd:6 * Wd], proj[:, 6 * Wd:7 * Wd]
    q_rot, k_rot = _tok_fwd("ret_rotary_fwd", _rotary_fn, [cos2, sin2, q_p, k_p], [], [(RET_WIDTH,)] * 2)
    y_ret_raw = _ret_attn_fwd(lg, q_rot, k_rot, v_ret)
    gn_w = row(W['ret_gn_w'])
    (y_ret,) = _tok_fwd("ret_post_fwd", _ret_post_fn, [y_ret_raw, g_ret], [gn_w], [(RET_WIDTH,)],
                        out_dtypes=[BF16])

    ycat = jnp.concatenate([y_rwkv, y_ret], axis=1)
    x1 = _mm("out_proj_fwd", ycat, w_out, add=x)
    (h2,) = _tok_fwd("norm_ffn_fwd", lambda a_, g_: (_rms_fn(a_, g_),), [x1], [g_ffn], [(D_MODEL,)],
                     out_dtypes=[BF16])
    gate = _mm("ffn_gate_fwd", h2, w_gate_t, tb=True)
    up = _mm("ffn_up_fwd", h2, w_up_t, tb=True)
    cw = W['ffn_conv_w']
    cb = row(W['ffn_conv_b'])
    act = _glu_fwd(gate, up, cw, cb)
    x2 = _mm("ffn_down_fwd", act, w_down, add=x1)
    loss8, dx2, dg_fin = _final_loss(x2, tgt, g_fin)

    G = {'norm_final_g': dg_fin}
    dact = _mm("ffn_down_dx", dx2, w_down, tb=True)
    d_down = _mm("ffn_down_dw", act, dx2, ta=True, out_dtype=BF16)
    dgate, dup, dcw, dcb = _glu_bwd(gate, up, dact, cw, cb)
    G['ffn_conv_w'], G['ffn_conv_b'] = dcw, dcb
    dh2 = _mm("ffn_gate_dx", dgate, w_gate_t)
    dh2 = _mm("ffn_up_dx", dup, w_up_t, add=dh2)
    d_gate_t = _mm("ffn_gate_dw", dgate, h2, ta=True, out_dtype=BF16)
    d_up_t = _mm("ffn_up_dw", dup, h2, ta=True, out_dtype=BF16)
    dx1, G['norm_ffn_g'] = _tok_bwd("norm_ffn_bwd", lambda a_, g_: (_rms_fn(a_, g_),), [], [x1], [g_ffn], [dh2], add=dx2)
    dycat = _mm("out_proj_dx", dx1, w_out, tb=True)
    d_out = _mm("out_proj_dw", ycat, dx1, ta=True, out_dtype=BF16)
    late_grads = [z.reshape(N_DEV, -1, D_MODEL) for z in (d_out, d_gate_t, d_up_t, d_down)]
    dy_rwkv, dy_ret = dycat[:, :Wd], dycat[:, Wd:]

    dyr_raw, dg_ret, G['ret_gn_w'] = _tok_bwd("ret_post_bwd", _ret_post_fn, [], [y_ret_raw, g_ret], [gn_w], [dy_ret],
                                              tok_dtypes=[F32, BF16])
    dq_rot, dk_rot, dv_ret = _ret_attn_bwd(lg, q_rot, k_rot, v_ret, dyr_raw)
    dq_p, dk_p = _tok_bwd("ret_rotary_bwd", _rotary_fn, [cos2, sin2], [q_p, k_p], [], [dq_rot, dk_rot],
                          tok_dtypes=[BF16, BF16])

    dy_scan, dr1, dk1, dv1, dg, G['rwkv_lnx_w'], G['rwkv_lnx_b'], G['rwkv_r_k'] = _tok_bwd(
        "rwkv_post_bwd", _rwkv_post_fn, [], [y_scan, r, k, v, g], post_consts, [dy_rwkv])
    (dr2, dlw, dk2, dv2, dnkk, db), late_parts = _cscan_bwd(r, lw, k, v, nkk, b, dy_scan, ck, late_grads)
    pre_cts = [(dr1, dr2), (dk1, dk2), (dv1, dv2), dlw, dnkk, db, dg]
    pre_out = _pre_a_bwd(h1, p_rkv, pre_consts, pre_cts)
    dh1_a, dp_rkv = pre_out[0], pre_out[1]
    (G['rwkv_mu_w'], G['rwkv_mu_a'], G['rwkv_mu_g'], G['rwkv_mu_r'], G['rwkv_mu_k'], G['rwkv_mu_v'], G['rwkv_w0'],
     dw1, dw2, G['rwkv_a0'], da1, da2, G['rwkv_g1'], G['rwkv_g2'], G['rwkv_k_k'], G['rwkv_k_a']) = pre_out[2:]
    G['rwkv_w1'], G['rwkv_w2'] = dw1[:, :64], dw2[:64, :]
    G['rwkv_a1'], G['rwkv_a2'] = da1[:, :64], da2[:64, :]

    dproj = jnp.concatenate([dp_rkv, dq_p, dk_p, dv_ret.astype(BF16), dg_ret], axis=1)
    dh1 = _mm("proj_dx", dproj, W['w_in_t'], add=dh1_a)
    G['w_in_t'] = _mm("proj_dw", dproj, h1, ta=True, out_dtype=BF16)
    dx, G['norm_mix_g'] = _tok_bwd("norm_mix_bwd", lambda a_, g_: (_rms_fn(a_, g_),), [], [x], [g_mix], [dh1], add=dx1)
    return loss8[0, 0], dx, G, late_parts


def _adamw(name, parts, w, m, v):
    rows, cols = w.shape
    sub = 8 * 4 // parts.dtype.itemsize
    tb = max(t for t in range(sub, 65, sub) if rows % t == 0) if rows > 64 else rows
    c1 = 1.0 - ADAM_B1 ** ADAM_STEP
    c2 = 1.0 - ADAM_B2 ** ADAM_STEP

    def body(p_ref, w_ref, m_ref, v_ref, g_ref, d_ref, nm_ref, nv_ref):
        g = p_ref[0].astype(F32)
        for d in range(1, N_DEV):
            g = g + p_ref[d].astype(F32)
        mn = ADAM_B1 * m_ref[...] + (1.0 - ADAM_B1) * g
        vn = ADAM_B2 * v_ref[...] + (1.0 - ADAM_B2) * (g * g)
        m_hat = mn / c1
        v_hat = vn / c2
        g_ref[...] = g
        d_ref[...] = -ADAM_LR * (m_hat / (jnp.sqrt(v_hat) + ADAM_EPS) + ADAM_WD * w_ref[...])
        nm_ref[...] = mn
        nv_ref[...] = vn

    spec = pl.BlockSpec((tb, cols), lambda i: (i, 0))
    sh = jax.ShapeDtypeStruct((rows, cols), F32)
    return pl.pallas_call(
        body, name=name, grid=(rows // tb,),
        in_specs=[pl.BlockSpec((N_DEV, tb, cols), lambda i: (0, i, 0)), spec, spec, spec],
        out_specs=[spec] * 4, out_shape=[sh] * 4, compiler_params=_cparams(1))(parts, w, m, v)


def _local_shape(name):
    gs, ax = SHARDED[name]
    ls = list(gs)
    ls[ax] //= N_DEV
    return tuple(ls)


def _seg(flat, seg):
    n = flat.shape[-1]
    pad = _round_up(n, seg) - n
    if pad:
        flat = jnp.pad(flat, [(0, 0)] * (flat.ndim - 1) + [(0, pad)])
    return flat


def _split3(w):
    hi = w.astype(BF16)
    r1 = w - hi.astype(F32)
    mid = r1.astype(BF16)
    lo = (r1 - mid.astype(F32)).astype(BF16)
    return hi, mid, lo


def _pack_small_shards(shards):
    pieces = []
    for name in SMALL_NAMES:
        flat = shards[name].reshape(-1)
        if name == 'ffn_conv_w':
            pieces += [_seg(p, BF16_SEG) for p in _split3(flat)]
        else:
            pieces.append(flat.astype(BF16))
    return jnp.concatenate(pieces).reshape(-1, 128)


def _unpack_small(gathered):
    flat = gathered.reshape(N_DEV, -1)
    out, off = {}, 0
    for name in SMALL_NAMES:
        gs, ax = SHARDED[name]
        ls = _local_shape(name)
        n = int(np.prod(ls))
        if name == 'ffn_conv_w':
            nseg = _round_up(n, BF16_SEG)
            hi, mid, lo = (flat[:, off + j * nseg: off + j * nseg + n].astype(F32) for j in range(3))
            sh = ((hi + mid) + lo).reshape(N_DEV, 3, -1)
            out[name] = jnp.swapaxes(sh, 0, 1).reshape(3, D_FF)
            off += 3 * nseg
        else:
            sh = flat[:, off:off + n].reshape((N_DEV,) + ls[1:])
            out[name] = sh.reshape(gs[1:]) if ax == 1 else jnp.swapaxes(sh, 0, 1).reshape(gs[1:])
            off += n
    return out


def _small_pieces(sharded, repl):
    return [sharded[n].reshape(-1) for n in SMALL_NAMES] + [repl[n].reshape(-1) for n in REPL_NAMES]


def _pack_small_local(d):
    flat = jnp.concatenate(_small_pieces(d, d))
    return _seg(flat, F32_SEG).reshape(-1, 128)


def _pack_small_grads(G):
    pieces = []
    for name in SMALL_NAMES:
        gs, ax = SHARDED[name]
        g = G[name]
        if name == 'ffn_conv_w':
            sh = jnp.swapaxes(g.reshape(3, N_DEV, -1), 0, 1)
        elif ax == 1:
            sh = g
        else:
            sh = jnp.swapaxes(g.reshape(g.shape[0], N_DEV, -1), 0, 1)
        pieces.append(sh.reshape(N_DEV, -1))
    rep = jnp.concatenate([G[n].reshape(-1) for n in REPL_NAMES])
    pieces.append(jnp.broadcast_to(rep[None, :], (N_DEV, rep.shape[0])))
    flat = _seg(jnp.concatenate(pieces, axis=1), F32_SEG)
    return flat.reshape(N_DEV, -1, 128)


def _unpack_small_local(packed, local_shapes):
    flat = packed.reshape(-1)
    out, off = {}, 0
    for name in SMALL_NAMES + REPL_NAMES:
        n = int(np.prod(local_shapes[name]))
        out[name] = flat[off:off + n].reshape(local_shapes[name])
        off += n
    return out


def kernel(x, *rest):
    nw = len(WEIGHT_NAMES)
    assert len(rest) == 3 * nw + 1
    weights = dict(zip(WEIGHT_NAMES, rest[:nw]))
    loss_target = rest[nw]
    moms = dict(zip(WEIGHT_NAMES, rest[nw + 1:2 * nw + 1]))
    vars_ = dict(zip(WEIGHT_NAMES, rest[2 * nw + 1:]))
    local_shapes = {n: weights[n].shape for n in WEIGHT_NAMES}

    def native2d(name, a):
        a2 = a.reshape(a.shape[-2], a.shape[-1])
        return a2.T if name in BIG_T else a2

    def from2d(name, a2):
        return (a2.T if name in BIG_T else a2).reshape(local_shapes[name])

    big_w = {n: native2d(n, weights[n]) for n in BIG_NAMES}
    w_in_t_sh = big_w['w_in'].astype(BF16)
    late = [big_w[n].astype(BF16) for n in LATE_NAMES]
    small_sh = _pack_small_shards({n: weights[n] for n in SMALL_NAMES})
    w_in_g, small_g = _exchange("weights_all_gather", [w_in_t_sh, small_sh], False)
    W = _unpack_small(small_g)
    W['w_in_t'] = w_in_g.reshape(-1, D_MODEL)
    for n in REPL_NAMES:
        W[n] = weights[n][0] if n != 'norm_final_g' else weights[n]

    loss, dx, G, late_parts = _local_step(x[0], loss_target[0], W, late)

    w_in_parts, small_parts = _exchange(
        "grads_all_to_all", [G['w_in_t'].reshape(N_DEV, -1, D_MODEL), _pack_small_grads(G)], True)
    results = {}
    for n, parts in zip(['w_in'] + LATE_NAMES, [w_in_parts] + list(late_parts)):
        res = _adamw("adamw_" + n, parts, big_w[n], native2d(n, moms[n]), native2d(n, vars_[n]))
        results[n] = [from2d(n, r) for r in res]
    small_res = _adamw("adamw_small", small_parts, _pack_small_local(weights), _pack_small_local(moms),
                       _pack_small_local(vars_))
    small_out = [_unpack_small_local(p, local_shapes) for p in small_res]

    loss = lax.psum(loss, ("x", "y", "c"))
    outs = [loss, dx[None]]
    for j in range(4):
        outs += [results[n][j] if n in results else small_out[j][n] for n in WEIGHT_NAMES]
    return tuple(outs)
```

```python
import functools
import math

import numpy as np
import jax
import jax.numpy as jnp
from jax import lax
from jax.experimental import pallas as pl
from jax.experimental.pallas import tpu as pltpu

F32 = jnp.float32
BF16 = jnp.bfloat16

N_DEV = 8
D_MODEL = 1024
RWKV_HEADS = 8
RWKV_HEAD_DIM = 64
RWKV_WIDTH = 512
RET_HEADS = 4
RET_HEAD_DIM = 128
RET_WIDTH = 512
LORA_PAD = 128
D_FF = 2816
NORM_EPS = 1e-6
RWKV_GN_EPS = 64e-5
RET_GN_EPS = 1e-5
ROPE_BASE = 10000.0
ADAM_LR, ADAM_B1, ADAM_B2, ADAM_EPS, ADAM_WD, ADAM_STEP = 0.001, 0.9, 0.999, 1e-08, 0.01, 10

VMEM_LIMIT = 56 * 1024 * 1024
TOK_BLOCK = 256
SCAN_CHUNK = 64
ATT_BLOCK = 512
BF16_SEG = 2048
F32_SEG = 1024

WEIGHT_NAMES = ['norm_mix_g', 'w_in', 'rwkv_mu_r', 'rwkv_mu_k', 'rwkv_mu_v', 'rwkv_mu_w', 'rwkv_mu_a',
                'rwkv_mu_g', 'rwkv_w0', 'rwkv_w1', 'rwkv_w2', 'rwkv_a0', 'rwkv_a1', 'rwkv_a2', 'rwkv_g1',
                'rwkv_g2', 'rwkv_k_k', 'rwkv_k_a', 'rwkv_r_k', 'rwkv_lnx_w', 'rwkv_lnx_b', 'ret_gn_w',
                'w_out', 'norm_ffn_g', 'ffn_w_gate', 'ffn_w_up', 'ffn_conv_w', 'ffn_conv_b', 'ffn_w_down',
                'norm_final_g']
SHARDED = {
    'w_in': ((1, 1024, 3584), 2), 'rwkv_w1': ((1, 1024, 64), 1), 'rwkv_w2': ((1, 64, 512), 2),
    'rwkv_a1': ((1, 1024, 64), 1), 'rwkv_a2': ((1, 64, 512), 2), 'rwkv_g1': ((1, 1024, 128), 1),
    'rwkv_g2': ((1, 128, 512), 2), 'w_out': ((1, 1024, 1024), 1), 'ffn_w_gate': ((1, 1024, 2816), 2),
    'ffn_w_up': ((1, 1024, 2816), 2), 'ffn_conv_w': ((1, 3, 1, 2816), 3), 'ffn_w_down': ((1, 2816, 1024), 1),
}
REPL_NAMES = [n for n in WEIGHT_NAMES if n not in SHARDED]
BIG_NAMES = ['w_in', 'w_out', 'ffn_w_gate', 'ffn_w_up', 'ffn_w_down']
BIG_T = ('w_in', 'ffn_w_gate', 'ffn_w_up')
LATE_NAMES = ['w_out', 'ffn_w_gate', 'ffn_w_up', 'ffn_w_down']
SMALL_NAMES = [n for n in WEIGHT_NAMES if n in SHARDED and n not in BIG_NAMES]


def _cparams(n_grid):
    return pltpu.CompilerParams(dimension_semantics=("arbitrary",) * n_grid, vmem_limit_bytes=VMEM_LIMIT)


def _round_up(n, m):
    return (n + m - 1) // m * m


@jax.custom_vjp
def _bdot(x, w):
    return jnp.dot(x.astype(BF16), w.astype(BF16), preferred_element_type=F32)


def _bdot_fwd(x, w):
    return _bdot(x, w), (x, w)


def _bdot_bwd(res, g):
    x, w = res
    gb = g.astype(BF16)
    dx = lax.dot_general(gb, w.astype(BF16), (((1,), (1,)), ((), ())), preferred_element_type=F32)
    dw = lax.dot_general(x.astype(BF16), gb, (((0,), (0,)), ((), ())), preferred_element_type=F32)
    return dx, dw.astype(w.dtype)


_bdot.defvjp(_bdot_fwd, _bdot_bwd)


@jax.custom_vjp
def _shift_rows(x, prev):
    rolled = pltpu.roll(x, 1, 0)
    row = lax.broadcasted_iota(jnp.int32, x.shape, 0)
    return jnp.where(row == 0, jnp.broadcast_to(prev, x.shape), rolled)


def _shift_rows_fwd(x, prev):
    return _shift_rows(x, prev), None


def _shift_rows_bwd(_, g):
    n = g.shape[0]
    rolled = pltpu.roll(g, n - 1, 0)
    row = lax.broadcasted_iota(jnp.int32, g.shape, 0)
    return jnp.where(row == n - 1, 0.0, rolled), g[0:1, :]


_shift_rows.defvjp(_shift_rows_fwd, _shift_rows_bwd)


@jax.custom_vjp
def _swap_halves(x):
    return pltpu.roll(x, 64, 1)


_swap_halves.defvjp(lambda x: (_swap_halves(x), None), lambda _, g: (pltpu.roll(g, 64, 1),))


def _sigmoid(x):
    return 1.0 / (1.0 + jnp.exp(-x))


def _softplus(x):
    return jnp.maximum(x, 0.0) + jnp.log(1.0 + jnp.exp(-jnp.abs(x)))


def _rms_fn(x, g):
    return x * lax.rsqrt(jnp.mean(x * x, axis=-1, keepdims=True) + NORM_EPS) * g


def _pre_a_fn(h1, h1p, p, pp, mu_w, mu_a, mu_g, mu_r, mu_k, mu_v, w0, w1, w2, a0, a1, a2, g1, g2, k_k, k_a):
    W = RWKV_WIDTH
    h1s = _shift_rows(h1, h1p)
    ps = _shift_rows(p, pp)
    dx = h1s - h1
    xw = h1 + dx * mu_w
    xa = h1 + dx * mu_a
    xg = h1 + dx * mu_g
    dp = ps - p
    r = p[:, 0:W] + dp[:, 0:W] * mu_r
    k0 = p[:, W:2 * W] + dp[:, W:2 * W] * mu_k
    v = p[:, 2 * W:3 * W] + dp[:, 2 * W:3 * W] * mu_v
    wl = w0 + _bdot(jnp.tanh(_bdot(xw, w1)), w2)
    w_log = -_softplus(-wl) - 0.5
    lw = -jnp.exp(w_log)
    a = _sigmoid(a0 + _bdot(_bdot(xa, a1), a2))
    g = _bdot(_sigmoid(_bdot(xg, g1)), g2)
    nkk, k, b = _pre_b_fn(k0, a, k_k, k_a)
    return r, k, v, lw, nkk, b, g


def _head_sum_raw(x):
    n = x.shape[1]
    ii = lax.broadcasted_iota(jnp.int32, (n, n), 0) // RWKV_HEAD_DIM
    jj = lax.broadcasted_iota(jnp.int32, (n, n), 1) // RWKV_HEAD_DIM
    ones = (ii == jj).astype(BF16)
    xh = x.astype(BF16)
    xl = (x - xh.astype(F32)).astype(BF16)
    return jnp.dot(xh, ones, preferred_element_type=F32) + jnp.dot(xl, ones, preferred_element_type=F32)


@jax.custom_vjp
def _head_sum(x):
    return _head_sum_raw(x)


_head_sum.defvjp(lambda x: (_head_sum_raw(x), None), lambda _, g: (_head_sum_raw(g),))


def _pre_b_fn(k0, a, k_k, k_a):
    kkr = k0 * k_k
    nrm = jnp.sqrt(_head_sum(kkr * kkr))
    kk = kkr / jnp.maximum(nrm, 1e-12)
    k = k0 * (1.0 + (a - 1.0) * k_a)
    return -kk, k, kk * a


def _rwkv_post_fn(y, r, k, v, g, lnx_w, lnx_b, r_k):
    inv = 1.0 / RWKV_HEAD_DIM
    mu = _head_sum(y) * inv
    yc = y - mu
    var = _head_sum(yc * yc) * inv
    yn = yc * lax.rsqrt(var + RWKV_GN_EPS) * lnx_w + lnx_b
    bonus = _head_sum(r * k * r_k) * v
    return ((yn + bonus) * g,)


def _rotary_fn(cos2, sin2, q, k):
    qs, ks = [], []
    for h in range(RET_HEADS):
        sl = slice(h * RET_HEAD_DIM, (h + 1) * RET_HEAD_DIM)
        qh, kh = q[:, sl], k[:, sl]
        qs.append(qh * cos2 + _swap_halves(qh) * sin2)
        ks.append((kh * cos2 + _swap_halves(kh) * sin2) * (RET_HEAD_DIM ** -0.5))
    return jnp.concatenate(qs, axis=1), jnp.concatenate(ks, axis=1)


def _ret_post_fn(y, gp, gn_w):
    outs = []
    for h in range(RET_HEADS):
        sl = slice(h * RET_HEAD_DIM, (h + 1) * RET_HEAD_DIM)
        yh = y[:, sl]
        mu = jnp.mean(yh, axis=-1, keepdims=True)
        yc = yh - mu
        var = jnp.mean(yc * yc, axis=-1, keepdims=True)
        outs.append(yc * lax.rsqrt(var + RET_GN_EPS) * gn_w[:, sl])
    yn = jnp.concatenate(outs, axis=1)
    return (gp * _sigmoid(gp) * yn,)


def _blk_spec(a, tb, rev_nb=None):
    nd = a.ndim
    if rev_nb is None:
        return pl.BlockSpec((tb,) + a.shape[1:], lambda i: (i,) + (0,) * (nd - 1))
    return pl.BlockSpec((tb,) + a.shape[1:], lambda i: (rev_nb - 1 - i,) + (0,) * (nd - 1))


def _full_spec(a):
    nd = a.ndim
    return pl.BlockSpec(a.shape, lambda i: (0,) * nd)


def _tok_fwd(name, fn, toks, consts, out_tails, tb=TOK_BLOCK, out_dtypes=None):
    out_dtypes = out_dtypes or [F32] * len(out_tails)
    n_in = len(toks) + len(consts)
    tn = toks[0].shape[0]

    def body(*refs):
        outs = fn(*[r[...] for r in refs[:n_in]])
        for r, o in zip(refs[n_in:], outs):
            r[...] = o.astype(r.dtype)

    out_shape = [jax.ShapeDtypeStruct((tn,) + tuple(s), dt) for s, dt in zip(out_tails, out_dtypes)]
    return pl.pallas_call(
        body, name=name, grid=(tn // tb,),
        in_specs=[_blk_spec(a, tb) for a in toks] + [_full_spec(c) for c in consts],
        out_specs=[_blk_spec(o, tb) for o in out_shape], out_shape=out_shape,
        compiler_params=_cparams(1))(*toks, *consts)


def _tok_bwd(name, fn, aux, toks, consts, cts, add=None, tb=TOK_BLOCK, tok_dtypes=None):
    n_aux, n_tok, n_c = len(aux), len(toks), len(consts)
    ct_groups = [c if isinstance(c, (tuple, list)) else (c,) for c in cts]
    ct_flat = [a for grp in ct_groups for a in grp]
    n_ct = len(ct_flat)
    n_add = 0 if add is None else 1
    tn = toks[0].shape[0]

    def body(*refs):
        pos = 0
        aux_v = [r[...] for r in refs[pos:pos + n_aux]]; pos += n_aux
        tok_v = [r[...] for r in refs[pos:pos + n_tok]]; pos += n_tok
        const_v = [r[...] for r in refs[pos:pos + n_c]]; pos += n_c
        ct_refs = refs[pos:pos + n_ct]; pos += n_ct
        add_refs = refs[pos:pos + n_add]; pos += n_add
        dtok_refs = refs[pos:pos + n_tok]; pos += n_tok
        dconst_refs = refs[pos:pos + n_c]
        ct_v, q = [], 0
        for grp in ct_groups:
            s = ct_refs[q][...]
            for r in ct_refs[q + 1:q + len(grp)]:
                s = s + r[...]
            q += len(grp)
            ct_v.append(s)
        _, vjp = jax.vjp(lambda *tc: fn(*aux_v, *tc), *tok_v, *const_v)
        grads = vjp(tuple(ct_v))
        for j, r in enumerate(dtok_refs):
            gj = grads[j]
            if j == 0 and n_add:
                gj = gj + add_refs[0][...]
            r[...] = gj.astype(r.dtype)

        @pl.when(pl.program_id(0) == 0)
        def _():
            for r in dconst_refs:
                r[...] = jnp.zeros(r.shape, F32)

        for j, r in enumerate(dconst_refs):
            r[...] += grads[n_tok + j]

    ins = list(aux) + list(toks) + list(consts) + ct_flat + ([add] if n_add else [])
    in_specs = ([_blk_spec(a, tb) for a in aux] + [_blk_spec(a, tb) for a in toks] + [_full_spec(c) for c in consts]
                + [_blk_spec(a, tb) for a in ct_flat] + ([_blk_spec(add, tb)] if n_add else []))
    tok_dtypes = tok_dtypes or [F32] * n_tok
    out_shape = ([jax.ShapeDtypeStruct(a.shape, dt) for a, dt in zip(toks, tok_dtypes)]
                 + [jax.ShapeDtypeStruct(c.shape, F32) for c in consts])
    out_specs = [_blk_spec(a, tb) for a in toks] + [_full_spec(c) for c in consts]
    return pl.pallas_call(body, name=name, grid=(tn // tb,), in_specs=in_specs, out_specs=out_specs,
                          out_shape=out_shape, compiler_params=_cparams(1))(*ins)


MM_VMEM_BUDGET = 40 * 1024 * 1024
MM_STEP_SECONDS = 0.4e-6
MM_HBM_BYTES_PER_SECOND = 2.5e12
MM_XPOSE_SECONDS_PER_ELEM = 2e-12
MM_MXU_COLUMNS = 256
MM_MXU_FLOPS = 9e14


def _mm_tiles(m, n, kd, a_bytes, b_bytes, o_bytes, has_add, ta):
    divs = lambda d: [t for t in range(128, d + 1, 128) if d % t == 0]
    best = None
    for tm in divs(m):
        for tn in divs(n):
            for tk in divs(kd):
                ni, nj, nk = m // tm, n // tn, kd // tk
                vmem = (2 * tm * tk * a_bytes + 2 * tk * tn * b_bytes + tm * tn * 4 + 2 * tm * tn * o_bytes
                        + (2 * tm * tn * 4 if has_add else 0) + 2 * (tm * tk + tk * tn) + tm * tn * 4)
                if vmem > MM_VMEM_BUDGET:
                    continue
                a_traffic = m * kd * a_bytes * (nj if nk > 1 else 1)
                b_traffic = kd * n * b_bytes * (ni if nj * nk > 1 else 1)
                cost = ni * nj * nk * MM_STEP_SECONDS + (a_traffic + b_traffic) / MM_HBM_BYTES_PER_SECOND
                cost += 2.0 * m * kd * nj * max(tn, MM_MXU_COLUMNS) / MM_MXU_FLOPS
                if ta:
                    cost += m * kd * nj * MM_XPOSE_SECONDS_PER_ELEM
                if best is None or cost < best[0]:
                    best = (cost, tm, tn, tk)
    return best[1:]


def _mm(name, a, b, ta=False, tb=False, add=None, out_dtype=F32):
    if ta:
        kd, m = a.shape
    else:
        m, kd = a.shape
    if tb:
        n, kb = b.shape
    else:
        kb, n = b.shape
    assert kd == kb, (a.shape, b.shape)
    tm, tn, tk = _mm_tiles(m, n, kd, a.dtype.itemsize, b.dtype.itemsize, jnp.dtype(out_dtype).itemsize,
                           add is not None, ta)
    nk = kd // tk
    has_add = add is not None
    dims = (((0 if ta else 1,), (1 if tb else 0,)), ((), ()))

    def body(*refs):
        a_ref, b_ref = refs[0], refs[1]
        o_ref, acc_ref = refs[-2], refs[-1]
        k = pl.program_id(2)

        @pl.when(k == 0)
        def _():
            acc_ref[...] = refs[2][...] if has_add else jnp.zeros(acc_ref.shape, F32)

        acc_ref[...] += lax.dot_general(a_ref[...].astype(BF16), b_ref[...].astype(BF16), dims,
                                        preferred_element_type=F32)

        @pl.when(k == nk - 1)
        def _():
            o_ref[...] = acc_ref[...].astype(out_dtype)

    a_spec = pl.BlockSpec((tk, tm), lambda i, j, k: (k, i)) if ta else pl.BlockSpec((tm, tk), lambda i, j, k: (i, k))
    b_spec = pl.BlockSpec((tn, tk), lambda i, j, k: (j, k)) if tb else pl.BlockSpec((tk, tn), lambda i, j, k: (k, j))
    o_spec = pl.BlockSpec((tm, tn), lambda i, j, k: (i, j))
    ins = [a, b] + ([add] if has_add else [])
    in_specs = [a_spec, b_spec] + ([o_spec] if has_add else [])
    return pl.pallas_call(body, name=name, grid=(m // tm, n // tn, nk), in_specs=in_specs, out_specs=o_spec,
                          out_shape=jax.ShapeDtypeStruct((m, n), out_dtype),
                          scratch_shapes=[pltpu.VMEM((tm, tn), F32)], compiler_params=_cparams(3))(*ins)


def _prev8_spec(a, tb, rev_nb=None):
    r = tb // 8
    if rev_nb is None:
        return pl.BlockSpec((8, a.shape[1]), lambda i: (jnp.maximum(i * r - 1, 0), 0))
    return pl.BlockSpec((8, a.shape[1]), lambda i: (jnp.maximum((rev_nb - 1 - i) * r - 1, 0), 0))


def _pre_a_fwd(h1, p, consts, tb=TOK_BLOCK):
    tn = h1.shape[0]

    def body(h1_ref, h1h_ref, p_ref, ph_ref, *rest):
        c_refs, o_refs = rest[:len(consts)], rest[len(consts):]
        first = pl.program_id(0) == 0
        h1p = jnp.where(first, 0.0, h1h_ref[7:8, :])
        pp = jnp.where(first, 0.0, ph_ref[7:8, :])
        outs = _pre_a_fn(h1_ref[...], h1p, p_ref[...], pp, *[c[...] for c in c_refs])
        for r, o in zip(o_refs, outs):
            r[...] = o

    out_shape = [jax.ShapeDtypeStruct((tn, RWKV_WIDTH), F32) for _ in range(7)]
    return pl.pallas_call(
        body, name="rwkv_pre_a_fwd", grid=(tn // tb,),
        in_specs=[_blk_spec(h1, tb), _prev8_spec(h1, tb), _blk_spec(p, tb), _prev8_spec(p, tb)]
        + [_full_spec(c) for c in consts],
        out_specs=[_blk_spec(o, tb) for o in out_shape], out_shape=out_shape,
        compiler_params=_cparams(1))(h1, h1, p, p, *consts)


def _pre_a_bwd(h1, p, consts, cts, tb=TOK_BLOCK):
    tn = h1.shape[0]
    nb = tn // tb
    n_c = len(consts)
    ct_groups = [c if isinstance(c, (tuple, list)) else (c,) for c in cts]
    ct_flat = [a for grp in ct_groups for a in grp]
    n_ct = len(ct_flat)

    def body(*refs):
        h1_ref, h1h_ref, p_ref, ph_ref = refs[:4]
        c_refs = refs[4:4 + n_c]
        ct_refs = refs[4 + n_c:4 + n_c + n_ct]
        dh1_ref, dp_ref = refs[4 + n_c + n_ct:6 + n_c + n_ct]
        dc_refs = refs[6 + n_c + n_ct:6 + 2 * n_c + n_ct]
        ch_ref, cp_ref = refs[-2], refs[-1]
        i = pl.program_id(0)
        first_block = i == nb - 1
        h1p = jnp.where(first_block, 0.0, h1h_ref[7:8, :])
        pp = jnp.where(first_block, 0.0, ph_ref[7:8, :])
        ct_v, q = [], 0
        for grp in ct_groups:
            s = ct_refs[q][...]
            for r in ct_refs[q + 1:q + len(grp)]:
                s = s + r[...]
            q += len(grp)
            ct_v.append(s)
        _, vjp = jax.vjp(_pre_a_fn, h1_ref[...], h1p, p_ref[...], pp, *[c[...] for c in c_refs])
        grads = vjp(tuple(ct_v))

        @pl.when(i == 0)
        def _():
            ch_ref[...] = jnp.zeros(ch_ref.shape, F32)
            cp_ref[...] = jnp.zeros(cp_ref.shape, F32)
            for r in dc_refs:
                r[...] = jnp.zeros(r.shape, F32)

        rowh = lax.broadcasted_iota(jnp.int32, (tb, h1.shape[1]), 0)
        rowp = lax.broadcasted_iota(jnp.int32, (tb, p.shape[1]), 0)
        dh1_ref[...] = grads[0] + jnp.where(rowh == tb - 1, jnp.broadcast_to(ch_ref[0:1, :], rowh.shape), 0.0)
        dp_ref[...] = (grads[2] + jnp.where(rowp == tb - 1, jnp.broadcast_to(cp_ref[0:1, :], rowp.shape), 0.0)
                       ).astype(dp_ref.dtype)
        ch_ref[0:1, :] = grads[1]
        cp_ref[0:1, :] = grads[3]
        for j, r in enumerate(dc_refs):
            r[...] += grads[4 + j]

    ins = [h1, h1, p, p] + list(consts) + ct_flat
    in_specs = ([_blk_spec(h1, tb, nb), _prev8_spec(h1, tb, nb), _blk_spec(p, tb, nb), _prev8_spec(p, tb, nb)]
                + [_full_spec(c) for c in consts] + [_blk_spec(a, tb, nb) for a in ct_flat])
    out_shape = ([jax.ShapeDtypeStruct(h1.shape, F32), jax.ShapeDtypeStruct(p.shape, BF16)]
                 + [jax.ShapeDtypeStruct(c.shape, F32) for c in consts])
    out_specs = [_blk_spec(h1, tb, nb), _blk_spec(p, tb, nb)] + [_full_spec(c) for c in consts]
    return pl.pallas_call(body, name="rwkv_pre_a_bwd", grid=(nb,), in_specs=in_specs, out_specs=out_specs,
                          out_shape=out_shape,
                          scratch_shapes=[pltpu.VMEM((8, h1.shape[1]), F32), pltpu.VMEM((8, p.shape[1]), F32)],
                          compiler_params=_cparams(1))(*ins)


def _my_index():
    return 4 * lax.axis_index("x") + 2 * lax.axis_index("y") + lax.axis_index("c")


def _peer(k):
    x, y, c = lax.axis_index("x"), lax.axis_index("y"), lax.axis_index("c")
    px = 1 - x if k & 4 else x
    py = 1 - y if k & 2 else y
    pc = 1 - c if k & 1 else c
    return (px, py, pc), 4 * px + 2 * py + pc


def _xchg_sems(n):
    return [pltpu.SemaphoreType.DMA((n * (N_DEV - 1),)), pltpu.SemaphoreType.DMA((n * (N_DEV - 1),)),
            pltpu.SemaphoreType.DMA((n,))]


def _scatter_copies(srcs, dsts, sems, incoming=False):
    send_sems, recv_sems, local_sems = sems
    me = _my_index()
    local, remote = [], []
    for i, (s, d) in enumerate(zip(srcs, dsts)):
        if not incoming:
            local.append(pltpu.make_async_copy(s.at[me], d.at[me], local_sems.at[i]))
        for k in range(1, N_DEV):
            peer, plin = _peer(k)
            j = i * (N_DEV - 1) + k - 1
            s_slot, d_slot = (me, plin) if incoming else (plin, me)
            remote.append(pltpu.make_async_remote_copy(
                src_ref=s.at[s_slot], dst_ref=d.at[d_slot], send_sem=send_sems.at[j],
                recv_sem=recv_sems.at[j], device_id=peer, device_id_type=pl.DeviceIdType.MESH))
    return local, remote


def _scatter_start(srcs, dsts, sems):
    local, out = _scatter_copies(srcs, dsts, sems)
    for cp in local + out:
        cp.start()


def _scatter_wait(srcs, dsts, sems):
    for cp in _scatter_copies(srcs, dsts, sems, incoming=True)[1]:
        cp.wait_recv()
    local, out = _scatter_copies(srcs, dsts, sems)
    for cp in out:
        cp.wait_send()
    for cp in local:
        cp.wait()


_ICI_PEERS = (2, 4, 6)


def _gather_copies(srcs, dsts, sems, group):
    send_sems, recv_sems, local_sems = sems
    me = _my_index()
    sib, sib_lin = _peer(1)
    out = []
    for i, (s, d) in enumerate(zip(srcs, dsts)):
        def mk(q, src, dst, dev):
            j = i * (N_DEV - 1) + q
            return pltpu.make_async_remote_copy(src_ref=src, dst_ref=dst, send_sem=send_sems.at[j],
                                                recv_sem=recv_sems.at[j], device_id=dev,
                                                device_id_type=pl.DeviceIdType.MESH)
        if group == 'local':
            out.append(pltpu.make_async_copy(s, d.at[me], local_sems.at[i]))
        elif group == 'own':
            out.append(mk(0, s, d.at[me], sib))
        elif group == 'in_d2d':
            out.append(mk(0, s, d.at[sib_lin], sib))
        for jj, k in enumerate(_ICI_PEERS):
            peer, plin = _peer(k)
            plin_other = _peer(k + 1)[1]
            if group == 'own':
                out.append(mk(1 + jj, s, d.at[me], peer))
            elif group == 'in_ici':
                out.append(mk(1 + jj, s, d.at[plin], peer))
            elif group == 'pass_on':
                out.append(mk(4 + jj, d.at[plin], d.at[plin], sib))
            elif group == 'in_d2d':
                out.append(mk(4 + jj, d.at[plin_other], d.at[plin_other], sib))
    return out


def _gather_start(srcs, dsts, sems):
    for cp in _gather_copies(srcs, dsts, sems, 'local') + _gather_copies(srcs, dsts, sems, 'own'):
        cp.start()


def _gather_pass_on(srcs, dsts, sems):
    for cp in _gather_copies(srcs, dsts, sems, 'in_ici'):
        cp.wait_recv()
    for cp in _gather_copies(srcs, dsts, sems, 'pass_on'):
        cp.start()


def _gather_finish(srcs, dsts, sems):
    for cp in _gather_copies(srcs, dsts, sems, 'in_d2d'):
        cp.wait_recv()
    for cp in _gather_copies(srcs, dsts, sems, 'own') + _gather_copies(srcs, dsts, sems, 'pass_on'):
        cp.wait_send()
    for cp in _gather_copies(srcs, dsts, sems, 'local'):
        cp.wait()


def _xchg_out_shapes(srcs, scatter):
    return [jax.ShapeDtypeStruct(s.shape if scatter else (N_DEV,) + s.shape, s.dtype) for s in srcs]


_ANY = pl.BlockSpec(memory_space=pl.ANY)


def _exchange(name, srcs, scatter):
    n = len(srcs)

    def body(*refs):
        s, d, sems = refs[:n], refs[n:2 * n], refs[2 * n:]
        if scatter:
            _scatter_start(s, d, sems)
            _scatter_wait(s, d, sems)
        else:
            _gather_start(s, d, sems)
            _gather_pass_on(s, d, sems)
            _gather_finish(s, d, sems)

    return pl.pallas_call(body, name=name, in_specs=[_ANY] * n, out_specs=[_ANY] * n,
                          out_shape=_xchg_out_shapes(srcs, scatter), scratch_shapes=_xchg_sems(n))(*srcs)


_MM_DIMS = {'nn': (((1,), (0,)), ((), ())), 'nt': (((1,), (1,)), ((), ())), 'tn': (((0,), (0,)), ((), ()))}


def _cmm_raw(x, y, kind, split):
    dot = functools.partial(lax.dot_general, dimension_numbers=_MM_DIMS[kind], preferred_element_type=F32)
    xh, yh = x.astype(BF16), y.astype(BF16)
    out = dot(xh, yh)
    if split:
        xl = (x - xh.astype(F32)).astype(BF16)
        yl = (y - yh.astype(F32)).astype(BF16)
        out = out + (dot(xh, yl) + dot(xl, yh))
    return out


@functools.partial(jax.custom_vjp, nondiff_argnums=(2, 3))
def _cmm(x, y, kind, split=False):
    return _cmm_raw(x, y, kind, split)


def _cmm_fwd(x, y, kind, split):
    return _cmm_raw(x, y, kind, split), (x, y)


def _cmm_bwd(kind, split, res, g):
    x, y = res
    if kind == 'nn':
        return _cmm_raw(g, y, 'nt', split), _cmm_raw(x, g, 'tn', split)
    if kind == 'nt':
        return _cmm_raw(g, y, 'nn', split), _cmm_raw(g, x, 'tn', split)
    return _cmm_raw(y, g, 'nt', split), _cmm_raw(x, g, 'nn', split)


_cmm.defvjp(_cmm_fwd, _cmm_bwd)


def _tri_sum_raw(tri, x, kind):
    dot = functools.partial(lax.dot_general, dimension_numbers=_MM_DIMS[kind], preferred_element_type=F32)
    tb = tri.astype(BF16)
    hi, mid, lo = _split3(x)
    return (dot(tb, hi) + dot(tb, mid)) + dot(tb, lo)


@functools.partial(jax.custom_vjp, nondiff_argnums=(2,))
def _tri_sum(tri, x, kind):
    return _tri_sum_raw(tri, x, kind)


def _tri_sum_fwd(tri, x, kind):
    return _tri_sum_raw(tri, x, kind), tri


def _tri_sum_bwd(kind, tri, g):
    return jnp.zeros_like(tri), _tri_sum_raw(tri, g, 'tn' if kind == 'nn' else 'nn')


_tri_sum.defvjp(_tri_sum_fwd, _tri_sum_bwd)


def _chunk_fn(S0, r, lw, k, v, a, b):
    hs = range(len(r))
    C = r[0].shape[0]
    ii = lax.broadcasted_iota(jnp.int32, (C, C), 0)
    jj = lax.broadcasted_iota(jnp.int32, (C, C), 1)
    incl, strict = ii >= jj, ii > jj
    eye = (ii == jj).astype(F32)
    inclf = incl.astype(F32)
    cum = [_tri_sum(inclf, lw[h], 'nn') for h in hs]
    e_inv = [jnp.exp(-cum[h]) for h in hs]
    At = [a[h] * jnp.exp(cum[h] - lw[h]) for h in hs]
    Rt = [r[h] * jnp.exp(cum[h]) for h in hs]
    Kh = [k[h] * e_inv[h] for h in hs]
    Bh = [b[h] * e_inv[h] for h in hs]
    Mab = [jnp.where(strict, _cmm(At[h], Bh[h], 'nt'), 0.0) for h in hs]
    Mak = [jnp.where(strict, _cmm(At[h], Kh[h], 'nt'), 0.0) for h in hs]
    Mrk = [jnp.where(incl, _cmm(Rt[h], Kh[h], 'nt'), 0.0) for h in hs]
    Mrb = [jnp.where(incl, _cmm(Rt[h], Bh[h], 'nt'), 0.0) for h in hs]
    rhs = [_cmm(At[h], S0[h], 'nt') + _cmm(Mak[h], v[h], 'nn') for h in hs]
    P = Mab
    Tm = [eye + P[h] for h in hs]
    n = 1
    while 2 * n < C:
        P = [_cmm(P[h], P[h], 'nn', True) for h in hs]
        Tm = [_cmm(Tm[h], eye + P[h], 'nn', True) for h in hs]
        n *= 2
    U = [_cmm(Tm[h], rhs[h], 'nn', True) for h in hs]
    Y = [_cmm(Rt[h], S0[h], 'nt') + _cmm(Mrk[h], v[h], 'nn') + _cmm(Mrb[h], U[h], 'nn') for h in hs]
    gC = [jnp.exp(jnp.sum(lw[h], axis=0, keepdims=True)) for h in hs]
    SC = [S0[h] * gC[h] + _cmm(v[h], Kh[h] * gC[h], 'tn') + _cmm(U[h], Bh[h] * gC[h], 'tn') for h in hs]
    return tuple(Y), tuple(SC)


def _cscan_fwd(r, lw, k, v, a, b, xs):
    n_x = len(xs)
    tn = r.shape[0]
    H, Dh, Dv = RWKV_HEADS, RWKV_HEAD_DIM, RWKV_HEAD_DIM
    nc = tn // SCAN_CHUNK
    lanes = lambda h: slice(h * Dh, (h + 1) * Dh)
    heads = lambda ref: tuple(ref[:, lanes(h)] for h in range(H))
    mats = lambda ref: tuple(ref[h] for h in range(H))

    def body(r_ref, lw_ref, k_ref, v_ref, a_ref, b_ref, *rest):
        x_src, (y_ref, ck_ref) = rest[:n_x], rest[n_x:n_x + 2]
        x_dst, s_ref, sems = rest[n_x + 2:2 * n_x + 2], rest[2 * n_x + 2], rest[2 * n_x + 3:]

        @pl.when(pl.program_id(0) == 0)
        def _():
            s_ref[...] = jnp.zeros(s_ref.shape, F32)
            _gather_start(x_src, x_dst, sems)

        ck_ref[0] = s_ref[...]
        y, sc = _chunk_fn(mats(s_ref), heads(r_ref), heads(lw_ref), heads(k_ref), heads(v_ref), heads(a_ref),
                          heads(b_ref))
        for h in range(H):
            y_ref[:, lanes(h)] = y[h]
            s_ref[h] = sc[h]

        @pl.when(pl.program_id(0) == max(nc - 4, 0))
        def _():
            _gather_pass_on(x_src, x_dst, sems)

        @pl.when(pl.program_id(0) == nc - 1)
        def _():
            _gather_finish(x_src, x_dst, sems)

    hm = pl.BlockSpec((SCAN_CHUNK, H * Dh), lambda c: (c, 0))
    res = pl.pallas_call(
        body, name="rwkv_scan_fwd", grid=(nc,), in_specs=[hm] * 6 + [_ANY] * n_x,
        out_specs=[hm, pl.BlockSpec((1, H, Dv, Dh), lambda c: (c, 0, 0, 0))] + [_ANY] * n_x,
        out_shape=[jax.ShapeDtypeStruct((tn, H * Dh), F32), jax.ShapeDtypeStruct((nc, H, Dv, Dh), F32)]
        + _xchg_out_shapes(xs, False),
        scratch_shapes=[pltpu.VMEM((H, Dv, Dh), F32)] + _xchg_sems(n_x),
        compiler_params=_cparams(1))(r, lw, k, v, a, b, *xs)
    return res[0], res[1], res[2:]


def _cscan_bwd(r, lw, k, v, a, b, dy, ck, xs):
    n_x = len(xs)
    tn = r.shape[0]
    H, Dh, Dv = RWKV_HEADS, RWKV_HEAD_DIM, RWKV_HEAD_DIM
    nc = tn // SCAN_CHUNK
    lanes = lambda h: slice(h * Dh, (h + 1) * Dh)
    heads = lambda ref: tuple(ref[:, lanes(h)] for h in range(H))
    mats = lambda ref: tuple(ref[h] for h in range(H))

    def body(r_ref, lw_ref, k_ref, v_ref, a_ref, b_ref, dy_ref, ck_ref, *rest):
        x_src = rest[:n_x]
        d_refs = rest[n_x:n_x + 6]
        x_dst = rest[n_x + 6:2 * n_x + 6]
        g_ref = rest[2 * n_x + 6]
        sems = rest[2 * n_x + 7:]

        @pl.when(pl.program_id(0) == 0)
        def _():
            g_ref[...] = jnp.zeros(g_ref.shape, F32)
            _scatter_start(x_src, x_dst, sems)

        s0 = tuple(ck_ref[0, h] for h in range(H))
        _, vjp = jax.vjp(_chunk_fn, s0, heads(r_ref), heads(lw_ref), heads(k_ref), heads(v_ref), heads(a_ref),
                         heads(b_ref))
        grads = vjp((heads(dy_ref), mats(g_ref)))
        for h in range(H):
            g_ref[h] = grads[0][h]
            for d_ref, gz in zip(d_refs, grads[1:]):
                d_ref[:, lanes(h)] = gz[h]

        @pl.when(pl.program_id(0) == nc - 1)
        def _():
            _scatter_wait(x_src, x_dst, sems)

    hm = pl.BlockSpec((SCAN_CHUNK, H * Dh), lambda c: (nc - 1 - c, 0))
    hshape = jax.ShapeDtypeStruct((tn, H * Dh), F32)
    res = pl.pallas_call(
        body, name="rwkv_scan_bwd", grid=(nc,),
        in_specs=[hm] * 7 + [pl.BlockSpec((1, H, Dv, Dh), lambda c: (nc - 1 - c, 0, 0, 0))] + [_ANY] * n_x,
        out_specs=[hm] * 6 + [_ANY] * n_x, out_shape=[hshape] * 6 + _xchg_out_shapes(xs, True),
        scratch_shapes=[pltpu.VMEM((H, Dv, Dh), F32)] + _xchg_sems(n_x),
        compiler_params=_cparams(1))(r, lw, k, v, a, b, dy, ck, *xs)
    return res[:6], res[6:]


def _decay_mask(lg, i, j, blk):
    rows = lax.broadcasted_iota(jnp.int32, (blk, blk), 0)
    cols = lax.broadcasted_iota(jnp.int32, (blk, blk), 1)
    dd = (rows - cols + (i - j) * blk).astype(F32)
    return jnp.where(dd >= 0.0, jnp.exp(lg * jnp.maximum(dd, 0.0)), 0.0)


_NT = (((1,), (1,)), ((), ()))
_TN = (((0,), (0,)), ((), ()))


def _ret_attn_fwd(lg, q, k, v, blk=ATT_BLOCK):
    tn = q.shape[0]
    Dh = RET_HEAD_DIM

    def body(lg_ref, q_ref, k_ref, v_ref, o_ref):
        i = pl.program_id(1)
        lgv = lg_ref[0][:, 0:1]
        qb = q_ref[...].astype(BF16)

        def jb(j, acc):
            ks = pl.ds(pl.multiple_of(j * blk, blk), blk)
            s = lax.dot_general(qb, k_ref[ks, :].astype(BF16), _NT, preferred_element_type=F32)
            s = s * _decay_mask(lgv, i, j, blk)
            return acc + jnp.dot(s.astype(BF16), v_ref[ks, :].astype(BF16), preferred_element_type=F32)

        o_ref[...] = lax.fori_loop(0, i + 1, jb, jnp.zeros((blk, Dh), F32))

    full = pl.BlockSpec((tn, Dh), lambda h, i: (0, h))
    qs = pl.BlockSpec((blk, Dh), lambda h, i: (i, h))
    return pl.pallas_call(
        body, name="ret_attn_fwd", grid=(RET_HEADS, tn // blk),
        in_specs=[pl.BlockSpec((1, 1, 128), lambda h, i: (h, 0, 0)), qs, full, full],
        out_specs=qs, out_shape=jax.ShapeDtypeStruct(q.shape, F32), compiler_params=_cparams(2))(lg, q, k, v)


def _ret_attn_bwd(lg, q, k, v, do, blk=ATT_BLOCK):
    tn = q.shape[0]
    nb = tn // blk
    Dh = RET_HEAD_DIM

    def body(lg_ref, q_ref, k_ref, v_ref, do_ref, dq_ref, dk_ref, dv_ref):
        lgv = lg_ref[0][:, 0:1]
        dk_ref[...] = jnp.zeros(dk_ref.shape, F32)
        dv_ref[...] = jnp.zeros(dv_ref.shape, F32)

        def ib(i, carry):
            qs = pl.ds(pl.multiple_of(i * blk, blk), blk)
            qb = q_ref[qs, :].astype(BF16)
            dob = do_ref[qs, :].astype(BF16)

            def jb(j, dq):
                ks = pl.ds(pl.multiple_of(j * blk, blk), blk)
                kb = k_ref[ks, :].astype(BF16)
                vb = v_ref[ks, :].astype(BF16)
                dm = _decay_mask(lgv, i, j, blk)
                s = lax.dot_general(qb, kb, _NT, preferred_element_type=F32) * dm
                ds = lax.dot_general(dob, vb, _NT, preferred_element_type=F32) * dm
                sb, dsb = s.astype(BF16), ds.astype(BF16)
                dv_ref[ks, :] += lax.dot_general(sb, dob, _TN, preferred_element_type=F32)
                dk_ref[ks, :] += lax.dot_general(dsb, qb, _TN, preferred_element_type=F32)
                return dq + jnp.dot(dsb, kb, preferred_element_type=F32)

            dq_ref[qs, :] = lax.fori_loop(0, i + 1, jb, jnp.zeros((blk, Dh), F32))
            return carry

        lax.fori_loop(0, nb, ib, 0)

    full = pl.BlockSpec((tn, Dh), lambda h: (0, h))
    sh = jax.ShapeDtypeStruct(q.shape, F32)
    return pl.pallas_call(
        body, name="ret_attn_bwd", grid=(RET_HEADS,),
        in_specs=[pl.BlockSpec((1, 1, 128), lambda h: (h, 0, 0)), full, full, full, full],
        out_specs=[full, full, full], out_shape=[sh, sh, sh], compiler_params=_cparams(1))(lg, q, k, v, do)


def _next8_spec(a, tb):
    r = tb // 8
    last = a.shape[0] // 8 - 1
    return pl.BlockSpec((8, a.shape[1]), lambda i: (jnp.minimum((i + 1) * r, last), 0))


def _conv_taps(g_ext, cw_ref, cb_ref):
    return (cw_ref[2:3, :] * g_ext + cw_ref[1:2, :] * pltpu.roll(g_ext, 1, 0)
            + cw_ref[0:1, :] * pltpu.roll(g_ext, 2, 0) + cb_ref[...])


def _glu_fwd(gate, up, cw, cb, tb=TOK_BLOCK):
    tn = gate.shape[0]

    def body(g_ref, gh_ref, u_ref, cw_ref, cb_ref, o_ref):
        halo = jnp.where(pl.program_id(0) == 0, 0.0, gh_ref[...])
        g_ext = jnp.concatenate([halo, g_ref[...]], axis=0)
        gc = _conv_taps(g_ext, cw_ref, cb_ref)[8:, :]
        o_ref[...] = (gc * _sigmoid(gc) * u_ref[...]).astype(o_ref.dtype)

    return pl.pallas_call(
        body, name="glu_fwd", grid=(tn // tb,),
        in_specs=[_blk_spec(gate, tb), _prev8_spec(gate, tb), _blk_spec(up, tb), _full_spec(cw), _full_spec(cb)],
        out_specs=_blk_spec(gate, tb), out_shape=jax.ShapeDtypeStruct(gate.shape, BF16),
        compiler_params=_cparams(1))(gate, gate, up, cw, cb)


def _glu_bwd(gate, up, dact, cw, cb, tb=TOK_BLOCK):
    tn = gate.shape[0]
    nb = tn // tb

    def body(g_ref, gp_ref, gn_ref, u_ref, un_ref, d_ref, dn_ref, cw_ref, cb_ref, dg_ref, du_ref, dcw_ref, dcb_ref):
        i = pl.program_id(0)
        gprev = jnp.where(i == 0, 0.0, gp_ref[...])
        dnext = jnp.where(i == nb - 1, 0.0, dn_ref[...])
        g_ext = jnp.concatenate([gprev, g_ref[...], gn_ref[...]], axis=0)
        gc = _conv_taps(g_ext, cw_ref, cb_ref)[8:, :]
        u_e = jnp.concatenate([u_ref[...], un_ref[...]], axis=0)
        d_e = jnp.concatenate([d_ref[...], dnext], axis=0)
        s = _sigmoid(gc)
        dgc = d_e * u_e * (s * (1.0 + gc * (1.0 - s)))
        du_ref[...] = (d_ref[...] * (gc * s)[:tb, :]).astype(du_ref.dtype)
        n_e = tb + 8
        dg_ref[...] = (cw_ref[2:3, :] * dgc + cw_ref[1:2, :] * pltpu.roll(dgc, n_e - 1, 0)
                       + cw_ref[0:1, :] * pltpu.roll(dgc, n_e - 2, 0))[:tb, :].astype(dg_ref.dtype)

        @pl.when(i == 0)
        def _():
            dcw_ref[...] = jnp.zeros(dcw_ref.shape, F32)
            dcb_ref[...] = jnp.zeros(dcb_ref.shape, F32)

        dgc_b = dgc[:tb, :]
        g0 = g_ext[8:8 + tb, :]
        g1 = pltpu.roll(g_ext, 1, 0)[8:8 + tb, :]
        g2 = pltpu.roll(g_ext, 2, 0)[8:8 + tb, :]
        dcw_ref[2:3, :] += jnp.sum(dgc_b * g0, axis=0, keepdims=True)
        dcw_ref[1:2, :] += jnp.sum(dgc_b * g1, axis=0, keepdims=True)
        dcw_ref[0:1, :] += jnp.sum(dgc_b * g2, axis=0, keepdims=True)
        dcb_ref[...] += jnp.sum(dgc_b, axis=0, keepdims=True)

    sh = jax.ShapeDtypeStruct(gate.shape, BF16)
    return pl.pallas_call(
        body, name="glu_bwd", grid=(nb,),
        in_specs=[_blk_spec(gate, tb), _prev8_spec(gate, tb), _next8_spec(gate, tb), _blk_spec(up, tb),
                  _next8_spec(up, tb), _blk_spec(dact, tb), _next8_spec(dact, tb), _full_spec(cw), _full_spec(cb)],
        out_specs=[_blk_spec(gate, tb), _blk_spec(gate, tb), _full_spec(cw), _full_spec(cb)],
        out_shape=[sh, sh, jax.ShapeDtypeStruct(cw.shape, F32), jax.ShapeDtypeStruct(cb.shape, F32)],
        compiler_params=_cparams(1))(gate, gate, gate, up, up, dact, dact, cw, cb)


def _final_loss(x2, tgt, g, tb=TOK_BLOCK):
    tn, dm = x2.shape

    def body(x_ref, t_ref, g_ref, l_ref, dx_ref, dg_ref):
        y, vjp = jax.vjp(_rms_fn, x_ref[...], g_ref[...])
        err = y - t_ref[...]
        dx, dg = vjp(err * (1.0 / dm))

        @pl.when(pl.program_id(0) == 0)
        def _():
            l_ref[...] = jnp.zeros(l_ref.shape, F32)
            dg_ref[...] = jnp.zeros(dg_ref.shape, F32)

        part = 0.5 * jnp.sum(jnp.mean(err * err, axis=-1, keepdims=True), axis=0, keepdims=True)
        l_ref[...] += jnp.broadcast_to(part, l_ref.shape)
        dx_ref[...] = dx
        dg_ref[...] += dg

    return pl.pallas_call(
        body, name="final_loss", grid=(tn // tb,),
        in_specs=[_blk_spec(x2, tb), _blk_spec(tgt, tb), _full_spec(g)],
        out_specs=[pl.BlockSpec((8, 128), lambda i: (0, 0)), _blk_spec(x2, tb), _full_spec(g)],
        out_shape=[jax.ShapeDtypeStruct((8, 128), F32), jax.ShapeDtypeStruct(x2.shape, F32),
                   jax.ShapeDtypeStruct(g.shape, F32)],
        compiler_params=_cparams(1))(x2, tgt, g)


def _pad_cols(w, n):
    return jnp.pad(w, ((0, 0), (0, n - w.shape[1])))


def _pad_rows(w, n):
    return jnp.pad(w, ((0, n - w.shape[0]), (0, 0)))


def _local_step(x, tgt, W, late):
    tn = x.shape[0]
    Wd = RWKV_WIDTH
    row = lambda z: z.reshape(1, -1)
    g_mix, g_ffn, g_fin = row(W['norm_mix_g']), row(W['norm_ffn_g']), row(W['norm_final_g'])

    (h1,) = _tok_fwd("norm_mix_fwd", lambda a, g: (_rms_fn(a, g),), [x], [g_mix], [(D_MODEL,)])
    proj = _mm("proj_fwd", h1, W['w_in_t'], tb=True)
    p_rkv = proj[:, :3 * Wd]
    pre_consts = [row(W['rwkv_mu_w']), row(W['rwkv_mu_a']), row(W['rwkv_mu_g']), row(W['rwkv_mu_r']),
                  row(W['rwkv_mu_k']), row(W['rwkv_mu_v']), row(W['rwkv_w0']),
                  _pad_cols(W['rwkv_w1'], LORA_PAD), _pad_rows(W['rwkv_w2'], LORA_PAD), row(W['rwkv_a0']),
                  _pad_cols(W['rwkv_a1'], LORA_PAD), _pad_rows(W['rwkv_a2'], LORA_PAD),
                  W['rwkv_g1'], W['rwkv_g2'], row(W['rwkv_k_k']), row(W['rwkv_k_a'])]
    r, k, v, lw, nkk, b, g = _pre_a_fwd(h1, p_rkv, pre_consts)
    y_scan, ck, gathered = _cscan_fwd(r, lw, k, v, nkk, b, late)
    w_out, w_gate_t, w_up_t, w_down = [g_.reshape(-1, D_MODEL) for g_ in gathered]
    post_consts = [row(W['rwkv_lnx_w']), row(W['rwkv_lnx_b']), row(W['rwkv_r_k'])]
    (y_rwkv,) = _tok_fwd("rwkv_post_fwd", _rwkv_post_fn, [y_scan, r, k, v, g], post_consts, [(Wd,)],
                         out_dtypes=[BF16])

    pos = jnp.arange(tn, dtype=F32)
    half = RET_HEAD_DIM // 2
    inv_freq = ROPE_BASE ** (-jnp.arange(half, dtype=F32) / half)
    ang = pos[:, None] * inv_freq[None, :]
    cos2 = jnp.concatenate([jnp.cos(ang), jnp.cos(ang)], axis=1)
    sin2 = jnp.concatenate([-jnp.sin(ang), jnp.sin(ang)], axis=1)
    lg = jnp.log(1.0 - 2.0 ** (-5.0 - jnp.arange(RET_HEADS, dtype=F32)))
    lg = jnp.broadcast_to(lg[:, None, None], (RET_HEADS, 1, 128))
    q_p, k_p = proj[:, 3 * Wd:4 * Wd], proj[:, 4 * Wd:5 * Wd]
    v_ret, g_ret = proj[:, 5 * Wd:6 * Wd], proj[:, 6 * Wd:7 * Wd]
    q_rot, k_rot = _tok_fwd("ret_rotary_fwd", _rotary_fn, [cos2, sin2, q_p, k_p], [], [(RET_WIDTH,)] * 2)
    y_ret_raw = _ret_attn_fwd(lg, q_rot, k_rot, v_ret)
    gn_w = row(W['ret_gn_w'])
    (y_ret,) = _tok_fwd("ret_post_fwd", _ret_post_fn, [y_ret_raw, g_ret], [gn_w], [(RET_WIDTH,)],
                        out_dtypes=[BF16])

    ycat = jnp.concatenate([y_rwkv, y_ret], axis=1)
    x1 = _mm("out_proj_fwd", ycat, w_out, add=x)
    (h2,) = _tok_fwd("norm_ffn_fwd", lambda a_, g_: (_rms_fn(a_, g_),), [x1], [g_ffn], [(D_MODEL,)],
                     out_dtypes=[BF16])
    gate = _mm("ffn_gate_fwd", h2, w_gate_t, tb=True)
    up = _mm("ffn_up_fwd", h2, w_up_t, tb=True)
    cw = W['ffn_conv_w']
    cb = row(W['ffn_conv_b'])
    act = _glu_fwd(gate, up, cw, cb)
    x2 = _mm("ffn_down_fwd", act, w_down, add=x1)
    loss8, dx2, dg_fin = _final_loss(x2, tgt, g_fin)

    G = {'norm_final_g': dg_fin}
    dact = _mm("ffn_down_dx", dx2, w_down, tb=True)
    d_down = _mm("ffn_down_dw", act, dx2, ta=True, out_dtype=BF16)
    dgate, dup, dcw, dcb = _glu_bwd(gate, up, dact, cw, cb)
    G['ffn_conv_w'], G['ffn_conv_b'] = dcw, dcb
    dh2 = _mm("ffn_gate_dx", dgate, w_gate_t)
    dh2 = _mm("ffn_up_dx", dup, w_up_t, add=dh2)
    d_gate_t = _mm("ffn_gate_dw", dgate, h2, ta=True, out_dtype=BF16)
    d_up_t = _mm("ffn_up_dw", dup, h2, ta=True, out_dtype=BF16)
    dx1, G['norm_ffn_g'] = _tok_bwd("norm_ffn_bwd", lambda a_, g_: (_rms_fn(a_, g_),), [], [x1], [g_ffn], [dh2], add=dx2)
    dycat = _mm("out_proj_dx", dx1, w_out, tb=True)
    d_out = _mm("out_proj_dw", ycat, dx1, ta=True, out_dtype=BF16)
    late_grads = [z.reshape(N_DEV, -1, D_MODEL) for z in (d_out, d_gate_t, d_up_t, d_down)]
    dy_rwkv, dy_ret = dycat[:, :Wd], dycat[:, Wd:]

    dyr_raw, dg_ret, G['ret_gn_w'] = _tok_bwd("ret_post_bwd", _ret_post_fn, [], [y_ret_raw, g_ret], [gn_w], [dy_ret],
                                              tok_dtypes=[F32, BF16])
    dq_rot, dk_rot, dv_ret = _ret_attn_bwd(lg, q_rot, k_rot, v_ret, dyr_raw)
    dq_p, dk_p = _tok_bwd("ret_rotary_bwd", _rotary_fn, [cos2, sin2], [q_p, k_p], [], [dq_rot, dk_rot],
                          tok_dtypes=[BF16, BF16])

    dy_scan, dr1, dk1, dv1, dg, G['rwkv_lnx_w'], G['rwkv_lnx_b'], G['rwkv_r_k'] = _tok_bwd(
        "rwkv_post_bwd", _rwkv_post_fn, [], [y_scan, r, k, v, g], post_consts, [dy_rwkv])
    (dr2, dlw, dk2, dv2, dnkk, db), late_parts = _cscan_bwd(r, lw, k, v, nkk, b, dy_scan, ck, late_grads)
    pre_cts = [(dr1, dr2), (dk1, dk2), (dv1, dv2), dlw, dnkk, db, dg]
    pre_out = _pre_a_bwd(h1, p_rkv, pre_consts, pre_cts)
    dh1_a, dp_rkv = pre_out[0], pre_out[1]
    (G['rwkv_mu_w'], G['rwkv_mu_a'], G['rwkv_mu_g'], G['rwkv_mu_r'], G['rwkv_mu_k'], G['rwkv_mu_v'], G['rwkv_w0'],
     dw1, dw2, G['rwkv_a0'], da1, da2, G['rwkv_g1'], G['rwkv_g2'], G['rwkv_k_k'], G['rwkv_k_a']) = pre_out[2:]
    G['rwkv_w1'], G['rwkv_w2'] = dw1[:, :64], dw2[:64, :]
    G['rwkv_a1'], G['rwkv_a2'] = da1[:, :64], da2[:64, :]

    dproj = jnp.concatenate([dp_rkv, dq_p, dk_p, dv_ret.astype(BF16), dg_ret], axis=1)
    dh1 = _mm("proj_dx", dproj, W['w_in_t'], add=dh1_a)
    G['w_in_t'] = _mm("proj_dw", dproj, h1, ta=True, out_dtype=BF16)
    dx, G['norm_mix_g'] = _tok_bwd("norm_mix_bwd", lambda a_, g_: (_rms_fn(a_, g_),), [], [x], [g_mix], [dh1], add=dx1)
    return loss8[0, 0], dx, G, late_parts


def _adamw(name, parts, w, m, v):
    rows, cols = w.shape
    sub = 8 * 4 // parts.dtype.itemsize
    tb = max(t for t in range(sub, 65, sub) if rows % t == 0) if rows > 64 else rows
    c1 = 1.0 - ADAM_B1 ** ADAM_STEP
    c2 = 1.0 - ADAM_B2 ** ADAM_STEP

    def body(p_ref, w_ref, m_ref, v_ref, g_ref, d_ref, nm_ref, nv_ref):
        g = p_ref[0].astype(F32)
        for d in range(1, N_DEV):
            g = g + p_ref[d].astype(F32)
        mn = ADAM_B1 * m_ref[...] + (1.0 - ADAM_B1) * g
        vn = ADAM_B2 * v_ref[...] + (1.0 - ADAM_B2) * (g * g)
        m_hat = mn / c1
        v_hat = vn / c2
        g_ref[...] = g
        d_ref[...] = -ADAM_LR * (m_hat / (jnp.sqrt(v_hat) + ADAM_EPS) + ADAM_WD * w_ref[...])
        nm_ref[...] = mn
        nv_ref[...] = vn

    spec = pl.BlockSpec((tb, cols), lambda i: (i, 0))
    sh = jax.ShapeDtypeStruct((rows, cols), F32)
    return pl.pallas_call(
        body, name=name, grid=(rows // tb,),
        in_specs=[pl.BlockSpec((N_DEV, tb, cols), lambda i: (0, i, 0)), spec, spec, spec],
        out_specs=[spec] * 4, out_shape=[sh] * 4, compiler_params=_cparams(1))(parts, w, m, v)


def _local_shape(name):
    gs, ax = SHARDED[name]
    ls = list(gs)
    ls[ax] //= N_DEV
    return tuple(ls)


def _seg(flat, seg):
    n = flat.shape[-1]
    pad = _round_up(n, seg) - n
    if pad:
        flat = jnp.pad(flat, [(0, 0)] * (flat.ndim - 1) + [(0, pad)])
    return flat


def _split3(w):
    hi = w.astype(BF16)
    r1 = w - hi.astype(F32)
    mid = r1.astype(BF16)
    lo = (r1 - mid.astype(F32)).astype(BF16)
    return hi, mid, lo


def _pack_small_shards(shards):
    pieces = []
    for name in SMALL_NAMES:
        flat = shards[name].reshape(-1)
        if name == 'ffn_conv_w':
            pieces += [_seg(p, BF16_SEG) for p in _split3(flat)]
        else:
            pieces.append(flat.astype(BF16))
    return jnp.concatenate(pieces).reshape(-1, 128)


def _unpack_small(gathered):
    flat = gathered.reshape(N_DEV, -1)
    out, off = {}, 0
    for name in SMALL_NAMES:
        gs, ax = SHARDED[name]
        ls = _local_shape(name)
        n = int(np.prod(ls))
        if name == 'ffn_conv_w':
            nseg = _round_up(n, BF16_SEG)
            hi, mid, lo = (flat[:, off + j * nseg: off + j * nseg + n].astype(F32) for j in range(3))
            sh = ((hi + mid) + lo).reshape(N_DEV, 3, -1)
            out[name] = jnp.swapaxes(sh, 0, 1).reshape(3, D_FF)
            off += 3 * nseg
        else:
            sh = flat[:, off:off + n].reshape((N_DEV,) + ls[1:])
            out[name] = sh.reshape(gs[1:]) if ax == 1 else jnp.swapaxes(sh, 0, 1).reshape(gs[1:])
            off += n
    return out


def _small_pieces(sharded, repl):
    return [sharded[n].reshape(-1) for n in SMALL_NAMES] + [repl[n].reshape(-1) for n in REPL_NAMES]


def _pack_small_local(d):
    flat = jnp.concatenate(_small_pieces(d, d))
    return _seg(flat, F32_SEG).reshape(-1, 128)


def _pack_small_grads(G):
    pieces = []
    for name in SMALL_NAMES:
        gs, ax = SHARDED[name]
        g = G[name]
        if name == 'ffn_conv_w':
            sh = jnp.swapaxes(g.reshape(3, N_DEV, -1), 0, 1)
        elif ax == 1:
            sh = g
        else:
            sh = jnp.swapaxes(g.reshape(g.shape[0], N_DEV, -1), 0, 1)
        pieces.append(sh.reshape(N_DEV, -1))
    rep = jnp.concatenate([G[n].reshape(-1) for n in REPL_NAMES])
    pieces.append(jnp.broadcast_to(rep[None, :], (N_DEV, rep.shape[0])))
    flat = _seg(jnp.concatenate(pieces, axis=1), F32_SEG)
    return flat.reshape(N_DEV, -1, 128)


def _unpack_small_local(packed, local_shapes):
    flat = packed.reshape(-1)
    out, off = {}, 0
    for name in SMALL_NAMES + REPL_NAMES:
        n = int(np.prod(local_shapes[name]))
        out[name] = flat[off:off + n].reshape(local_shapes[name])
        off += n
    return out


def kernel(x, *rest):
    nw = len(WEIGHT_NAMES)
    assert len(rest) == 3 * nw + 1
    weights = dict(zip(WEIGHT_NAMES, rest[:nw]))
    loss_target = rest[nw]
    moms = dict(zip(WEIGHT_NAMES, rest[nw + 1:2 * nw + 1]))
    vars_ = dict(zip(WEIGHT_NAMES, rest[2 * nw + 1:]))
    local_shapes = {n: weights[n].shape for n in WEIGHT_NAMES}

    def native2d(name, a):
        a2 = a.reshape(a.shape[-2], a.shape[-1])
        return a2.T if name in BIG_T else a2

    def from2d(name, a2):
        return (a2.T if name in BIG_T else a2).reshape(local_shapes[name])

    big_w = {n: native2d(n, weights[n]) for n in BIG_NAMES}
    w_in_t_sh = big_w['w_in'].astype(BF16)
    late = [big_w[n].astype(BF16) for n in LATE_NAMES]
    small_sh = _pack_small_shards({n: weights[n] for n in SMALL_NAMES})
    w_in_g, small_g = _exchange("weights_all_gather", [w_in_t_sh, small_sh], False)
    W = _unpack_small(small_g)
    W['w_in_t'] = w_in_g.reshape(-1, D_MODEL)
    for n in REPL_NAMES:
        W[n] = weights[n][0] if n != 'norm_final_g' else weights[n]

    loss, dx, G, late_parts = _local_step(x[0], loss_target[0], W, late)

    w_in_parts, small_parts = _exchange(
        "grads_all_to_all", [G['w_in_t'].reshape(N_DEV, -1, D_MODEL), _pack_small_grads(G)], True)
    results = {}
    for n, parts in zip(['w_in'] + LATE_NAMES, [w_in_parts] + list(late_parts)):
        res = _adamw("adamw_" + n, parts, big_w[n], native2d(n, moms[n]), native2d(n, vars_[n]))
        results[n] = [from2d(n, r) for r in res]
    small_res = _adamw("adamw_small", small_parts, _pack_small_local(weights), _pack_small_local(moms),
                       _pack_small_local(vars_))
    small_out = [_unpack_small_local(p, local_shapes) for p in small_res]

    loss = lax.psum(loss, ("x", "y", "c"))
    outs = [loss, dx[None]]
    for j in range(4):
        outs += [results[n][j] if n in results else small_out[j][n] for n in WEIGHT_NAMES]
    return tuple(outs)
```

```python
import functools
import math

import numpy as np
import jax
import jax.numpy as jnp
from jax import lax
from jax.experimental import pallas as pl
from jax.experimental.pallas import tpu as pltpu

F32 = jnp.float32
BF16 = jnp.bfloat16

N_DEV = 8
D_MODEL = 1024
RWKV_HEADS = 8
RWKV_HEAD_DIM = 64
RWKV_WIDTH = 512
RET_HEADS = 4
RET_HEAD_DIM = 128
RET_WIDTH = 512
LORA_PAD = 128
D_FF = 2816
NORM_EPS = 1e-6
RWKV_GN_EPS = 64e-5
RET_GN_EPS = 1e-5
ROPE_BASE = 10000.0
ADAM_LR, ADAM_B1, ADAM_B2, ADAM_EPS, ADAM_WD, ADAM_STEP = 0.001, 0.9, 0.999, 1e-08, 0.01, 10

VMEM_LIMIT = 56 * 1024 * 1024
TOK_BLOCK = 256
SCAN_CHUNK = 64
ATT_BLOCK = 512
BF16_SEG = 2048
F32_SEG = 1024

WEIGHT_NAMES = ['norm_mix_g', 'w_in', 'rwkv_mu_r', 'rwkv_mu_k', 'rwkv_mu_v', 'rwkv_mu_w', 'rwkv_mu_a',
                'rwkv_mu_g', 'rwkv_w0', 'rwkv_w1', 'rwkv_w2', 'rwkv_a0', 'rwkv_a1', 'rwkv_a2', 'rwkv_g1',
                'rwkv_g2', 'rwkv_k_k', 'rwkv_k_a', 'rwkv_r_k', 'rwkv_lnx_w', 'rwkv_lnx_b', 'ret_gn_w',
                'w_out', 'norm_ffn_g', 'ffn_w_gate', 'ffn_w_up', 'ffn_conv_w', 'ffn_conv_b', 'ffn_w_down',
                'norm_final_g']
SHARDED = {
    'w_in': ((1, 1024, 3584), 2), 'rwkv_w1': ((1, 1024, 64), 1), 'rwkv_w2': ((1, 64, 512), 2),
    'rwkv_a1': ((1, 1024, 64), 1), 'rwkv_a2': ((1, 64, 512), 2), 'rwkv_g1': ((1, 1024, 128), 1),
    'rwkv_g2': ((1, 128, 512), 2), 'w_out': ((1, 1024, 1024), 1), 'ffn_w_gate': ((1, 1024, 2816), 2),
    'ffn_w_up': ((1, 1024, 2816), 2), 'ffn_conv_w': ((1, 3, 1, 2816), 3), 'ffn_w_down': ((1, 2816, 1024), 1),
}
REPL_NAMES = [n for n in WEIGHT_NAMES if n not in SHARDED]
BIG_NAMES = ['w_in', 'w_out', 'ffn_w_gate', 'ffn_w_up', 'ffn_w_down']
BIG_T = ('w_in', 'ffn_w_gate', 'ffn_w_up')
LATE_NAMES = ['w_out', 'ffn_w_gate', 'ffn_w_up', 'ffn_w_down']
SMALL_NAMES = [n for n in WEIGHT_NAMES if n in SHARDED and n not in BIG_NAMES]


def _cparams(n_grid):
    return pltpu.CompilerParams(dimension_semantics=("arbitrary",) * n_grid, vmem_limit_bytes=VMEM_LIMIT)


def _round_up(n, m):
    return (n + m - 1) // m * m


@jax.custom_vjp
def _bdot(x, w):
    return jnp.dot(x.astype(BF16), w.astype(BF16), preferred_element_type=F32)


def _bdot_fwd(x, w):
    return _bdot(x, w), (x, w)


def _bdot_bwd(res, g):
    x, w = res
    gb = g.astype(BF16)
    dx = lax.dot_general(gb, w.astype(BF16), (((1,), (1,)), ((), ())), preferred_element_type=F32)
    dw = lax.dot_general(x.astype(BF16), gb, (((0,), (0,)), ((), ())), preferred_element_type=F32)
    return dx, dw.astype(w.dtype)


_bdot.defvjp(_bdot_fwd, _bdot_bwd)


@jax.custom_vjp
def _shift_rows(x, prev):
    rolled = pltpu.roll(x, 1, 0)
    row = lax.broadcasted_iota(jnp.int32, x.shape, 0)
    return jnp.where(row == 0, jnp.broadcast_to(prev, x.shape), rolled)


def _shift_rows_fwd(x, prev):
    return _shift_rows(x, prev), None


def _shift_rows_bwd(_, g):
    n = g.shape[0]
    rolled = pltpu.roll(g, n - 1, 0)
    row = lax.broadcasted_iota(jnp.int32, g.shape, 0)
    return jnp.where(row == n - 1, 0.0, rolled), g[0:1, :]


_shift_rows.defvjp(_shift_rows_fwd, _shift_rows_bwd)


@jax.custom_vjp
def _swap_halves(x):
    return pltpu.roll(x, 64, 1)


_swap_halves.defvjp(lambda x: (_swap_halves(x), None), lambda _, g: (pltpu.roll(g, 64, 1),))


def _sigmoid(x):
    return 1.0 / (1.0 + jnp.exp(-x))


def _softplus(x):
    return jnp.maximum(x, 0.0) + jnp.log(1.0 + jnp.exp(-jnp.abs(x)))


def _rms_fn(x, g):
    return x * lax.rsqrt(jnp.mean(x * x, axis=-1, keepdims=True) + NORM_EPS) * g


def _pre_a_fn(h1, h1p, p, pp, mu_w, mu_a, mu_g, mu_r, mu_k, mu_v, w0, w1, w2, a0, a1, a2, g1, g2, k_k, k_a):
    W = RWKV_WIDTH
    h1s = _shift_rows(h1, h1p)
    ps = _shift_rows(p, pp)
    dx = h1s - h1
    xw = h1 + dx * mu_w
    xa = h1 + dx * mu_a
    xg = h1 + dx * mu_g
    dp = ps - p
    r = p[:, 0:W] + dp[:, 0:W] * mu_r
    k0 = p[:, W:2 * W] + dp[:, W:2 * W] * mu_k
    v = p[:, 2 * W:3 * W] + dp[:, 2 * W:3 * W] * mu_v
    wl = w0 + _bdot(jnp.tanh(_bdot(xw, w1)), w2)
    w_log = -_softplus(-wl) - 0.5
    lw = -jnp.exp(w_log)
    a = _sigmoid(a0 + _bdot(_bdot(xa, a1), a2))
    g = _bdot(_sigmoid(_bdot(xg, g1)), g2)
    nkk, k, b = _pre_b_fn(k0, a, k_k, k_a)
    return r, k, v, lw, nkk, b, g


def _head_sum_raw(x):
    n = x.shape[1]
    ii = lax.broadcasted_iota(jnp.int32, (n, n), 0) // RWKV_HEAD_DIM
    jj = lax.broadcasted_iota(jnp.int32, (n, n), 1) // RWKV_HEAD_DIM
    ones = (ii == jj).astype(BF16)
    xh = x.astype(BF16)
    xl = (x - xh.astype(F32)).astype(BF16)
    return jnp.dot(xh, ones, preferred_element_type=F32) + jnp.dot(xl, ones, preferred_element_type=F32)


@jax.custom_vjp
def _head_sum(x):
    return _head_sum_raw(x)


_head_sum.defvjp(lambda x: (_head_sum_raw(x), None), lambda _, g: (_head_sum_raw(g),))


def _pre_b_fn(k0, a, k_k, k_a):
    kkr = k0 * k_k
    nrm = jnp.sqrt(_head_sum(kkr * kkr))
    kk = kkr / jnp.maximum(nrm, 1e-12)
    k = k0 * (1.0 + (a - 1.0) * k_a)
    return -kk, k, kk * a


def _rwkv_post_fn(y, r, k, v, g, lnx_w, lnx_b, r_k):
    inv = 1.0 / RWKV_HEAD_DIM
    mu = _head_sum(y) * inv
    yc = y - mu
    var = _head_sum(yc * yc) * inv
    yn = yc * lax.rsqrt(var + RWKV_GN_EPS) * lnx_w + lnx_b
    bonus = _head_sum(r * k * r_k) * v
    return ((yn + bonus) * g,)


def _rotary_fn(cos2, sin2, q, k):
    qs, ks = [], []
    for h in range(RET_HEADS):
        sl = slice(h * RET_HEAD_DIM, (h + 1) * RET_HEAD_DIM)
        qh, kh = q[:, sl], k[:, sl]
        qs.append(qh * cos2 + _swap_halves(qh) * sin2)
        ks.append((kh * cos2 + _swap_halves(kh) * sin2) * (RET_HEAD_DIM ** -0.5))
    return jnp.concatenate(qs, axis=1), jnp.concatenate(ks, axis=1)


def _ret_post_fn(y, gp, gn_w):
    outs = []
    for h in range(RET_HEADS):
        sl = slice(h * RET_HEAD_DIM, (h + 1) * RET_HEAD_DIM)
        yh = y[:, sl]
        mu = jnp.mean(yh, axis=-1, keepdims=True)
        yc = yh - mu
        var = jnp.mean(yc * yc, axis=-1, keepdims=True)
        outs.append(yc * lax.rsqrt(var + RET_GN_EPS) * gn_w[:, sl])
    yn = jnp.concatenate(outs, axis=1)
    return (gp * _sigmoid(gp) * yn,)


class _Cols:
    def __init__(self, array, width, block):
        self.array, self.width, self.block = array, width, block
        self.shape, self.ndim, self.dtype = (array.shape[0], width), 2, array.dtype


def _arr(a):
    return a.array if isinstance(a, _Cols) else a


def _blk_spec(a, tb, rev_nb=None):
    nd = a.ndim
    tail = (a.block,) if isinstance(a, _Cols) else (0,) * (nd - 1)
    if rev_nb is None:
        return pl.BlockSpec((tb,) + a.shape[1:], lambda i: (i,) + tail)
    return pl.BlockSpec((tb,) + a.shape[1:], lambda i: (rev_nb - 1 - i,) + tail)


def _full_spec(a):
    nd = a.ndim
    return pl.BlockSpec(a.shape, lambda i: (0,) * nd)


def _tok_fwd(name, fn, toks, consts, out_tails, tb=TOK_BLOCK, out_dtypes=None):
    out_dtypes = out_dtypes or [F32] * len(out_tails)
    n_in = len(toks) + len(consts)
    tn = toks[0].shape[0]

    def body(*refs):
        outs = fn(*[r[...] for r in refs[:n_in]])
        for r, o in zip(refs[n_in:], outs):
            r[...] = o.astype(r.dtype)

    out_shape = [jax.ShapeDtypeStruct((tn,) + tuple(s), dt) for s, dt in zip(out_tails, out_dtypes)]
    return pl.pallas_call(
        body, name=name, grid=(tn // tb,),
        in_specs=[_blk_spec(a, tb) for a in toks] + [_full_spec(c) for c in consts],
        out_specs=[_blk_spec(o, tb) for o in out_shape], out_shape=out_shape,
        compiler_params=_cparams(1))(*[_arr(a) for a in toks], *consts)


def _tok_bwd(name, fn, aux, toks, consts, cts, add=None, tb=TOK_BLOCK, tok_dtypes=None):
    n_aux, n_tok, n_c = len(aux), len(toks), len(consts)
    ct_groups = [c if isinstance(c, (tuple, list)) else (c,) for c in cts]
    ct_flat = [a for grp in ct_groups for a in grp]
    n_ct = len(ct_flat)
    n_add = 0 if add is None else 1
    tn = toks[0].shape[0]

    def body(*refs):
        pos = 0
        aux_v = [r[...] for r in refs[pos:pos + n_aux]]; pos += n_aux
        tok_v = [r[...] for r in refs[pos:pos + n_tok]]; pos += n_tok
        const_v = [r[...] for r in refs[pos:pos + n_c]]; pos += n_c
        ct_refs = refs[pos:pos + n_ct]; pos += n_ct
        add_refs = refs[pos:pos + n_add]; pos += n_add
        dtok_refs = refs[pos:pos + n_tok]; pos += n_tok
        dconst_refs = refs[pos:pos + n_c]
        ct_v, q = [], 0
        for grp in ct_groups:
            s = ct_refs[q][...]
            for r in ct_refs[q + 1:q + len(grp)]:
                s = s + r[...]
            q += len(grp)
            ct_v.append(s)
        _, vjp = jax.vjp(lambda *tc: fn(*aux_v, *tc), *tok_v, *const_v)
        grads = vjp(tuple(ct_v))
        for j, r in enumerate(dtok_refs):
            gj = grads[j]
            if j == 0 and n_add:
                gj = gj + add_refs[0][...]
            r[...] = gj.astype(r.dtype)

        @pl.when(pl.program_id(0) == 0)
        def _():
            for r in dconst_refs:
                r[...] = jnp.zeros(r.shape, F32)

        for j, r in enumerate(dconst_refs):
            r[...] += grads[n_tok + j]

    ins = list(aux) + list(toks) + list(consts) + ct_flat + ([add] if n_add else [])
    in_specs = ([_blk_spec(a, tb) for a in aux] + [_blk_spec(a, tb) for a in toks] + [_full_spec(c) for c in consts]
                + [_blk_spec(a, tb) for a in ct_flat] + ([_blk_spec(add, tb)] if n_add else []))
    tok_dtypes = tok_dtypes or [F32] * n_tok
    out_shape = ([jax.ShapeDtypeStruct(a.shape, dt) for a, dt in zip(toks, tok_dtypes)]
                 + [jax.ShapeDtypeStruct(c.shape, F32) for c in consts])
    out_specs = [_blk_spec(o, tb) for o in out_shape[:n_tok]] + [_full_spec(c) for c in consts]
    return pl.pallas_call(body, name=name, grid=(tn // tb,), in_specs=in_specs, out_specs=out_specs,
                          out_shape=out_shape, compiler_params=_cparams(1))(*[_arr(a) for a in ins])


MM_VMEM_BUDGET = 40 * 1024 * 1024
MM_STEP_SECONDS = 0.4e-6
MM_HBM_BYTES_PER_SECOND = 2.5e12
MM_XPOSE_SECONDS_PER_ELEM = 2e-12
MM_MXU_COLUMNS = 256
MM_MXU_FLOPS = 9e14


def _mm_tiles(m, n, kd, a_bytes, b_bytes, o_bytes, has_add, ta):
    divs = lambda d: [t for t in range(128, d + 1, 128) if d % t == 0]
    best = None
    for tm in divs(m):
        for tn in divs(n):
            for tk in divs(kd):
                ni, nj, nk = m // tm, n // tn, kd // tk
                vmem = (2 * tm * tk * a_bytes + 2 * tk * tn * b_bytes + tm * tn * 4 + 2 * tm * tn * o_bytes
                        + (2 * tm * tn * 4 if has_add else 0) + 2 * (tm * tk + tk * tn) + tm * tn * 4)
                if vmem > MM_VMEM_BUDGET:
                    continue
                a_traffic = m * kd * a_bytes * (nj if nk > 1 else 1)
                b_traffic = kd * n * b_bytes * (ni if nj * nk > 1 else 1)
                cost = ni * nj * nk * MM_STEP_SECONDS + (a_traffic + b_traffic) / MM_HBM_BYTES_PER_SECOND
                cost += 2.0 * m * kd * nj * max(tn, MM_MXU_COLUMNS) / MM_MXU_FLOPS
                if ta:
                    cost += m * kd * nj * MM_XPOSE_SECONDS_PER_ELEM
                if best is None or cost < best[0]:
                    best = (cost, tm, tn, tk)
    return best[1:]


def _mm(name, a, b, ta=False, tb=False, add=None, out_dtype=F32):
    if ta:
        kd, m = a.shape
    else:
        m, kd = a.shape
    if tb:
        n, kb = b.shape
    else:
        kb, n = b.shape
    assert kd == kb, (a.shape, b.shape)
    tm, tn, tk = _mm_tiles(m, n, kd, a.dtype.itemsize, b.dtype.itemsize, jnp.dtype(out_dtype).itemsize,
                           add is not None, ta)
    nk = kd // tk
    has_add = add is not None
    dims = (((0 if ta else 1,), (1 if tb else 0,)), ((), ()))

    def body(*refs):
        a_ref, b_ref = refs[0], refs[1]
        o_ref, acc_ref = refs[-2], refs[-1]
        k = pl.program_id(2)

        @pl.when(k == 0)
        def _():
            acc_ref[...] = refs[2][...] if has_add else jnp.zeros(acc_ref.shape, F32)

        acc_ref[...] += lax.dot_general(a_ref[...].astype(BF16), b_ref[...].astype(BF16), dims,
                                        preferred_element_type=F32)

        @pl.when(k == nk - 1)
        def _():
            o_ref[...] = acc_ref[...].astype(out_dtype)

    a_spec = pl.BlockSpec((tk, tm), lambda i, j, k: (k, i)) if ta else pl.BlockSpec((tm, tk), lambda i, j, k: (i, k))
    b_spec = pl.BlockSpec((tn, tk), lambda i, j, k: (j, k)) if tb else pl.BlockSpec((tk, tn), lambda i, j, k: (k, j))
    o_spec = pl.BlockSpec((tm, tn), lambda i, j, k: (i, j))
    ins = [a, b] + ([add] if has_add else [])
    in_specs = [a_spec, b_spec] + ([o_spec] if has_add else [])
    return pl.pallas_call(body, name=name, grid=(m // tm, n // tn, nk), in_specs=in_specs, out_specs=o_spec,
                          out_shape=jax.ShapeDtypeStruct((m, n), out_dtype),
                          scratch_shapes=[pltpu.VMEM((tm, tn), F32)], compiler_params=_cparams(3))(*ins)


def _prev8_spec(a, tb, rev_nb=None):
    r = tb // 8
    if rev_nb is None:
        return pl.BlockSpec((8, a.shape[1]), lambda i: (jnp.maximum(i * r - 1, 0), 0))
    return pl.BlockSpec((8, a.shape[1]), lambda i: (jnp.maximum((rev_nb - 1 - i) * r - 1, 0), 0))


def _pre_a_fwd(h1, p, consts, tb=TOK_BLOCK):
    tn = h1.shape[0]

    def body(h1_ref, h1h_ref, p_ref, ph_ref, *rest):
        c_refs, o_refs = rest[:len(consts)], rest[len(consts):]
        first = pl.program_id(0) == 0
        h1p = jnp.where(first, 0.0, h1h_ref[7:8, :])
        pp = jnp.where(first, 0.0, ph_ref[7:8, :])
        outs = _pre_a_fn(h1_ref[...], h1p, p_ref[...], pp, *[c[...] for c in c_refs])
        for r, o in zip(o_refs, outs):
            r[...] = o

    out_shape = [jax.ShapeDtypeStruct((tn, RWKV_WIDTH), F32) for _ in range(7)]
    return pl.pallas_call(
        body, name="rwkv_pre_a_fwd", grid=(tn // tb,),
        in_specs=[_blk_spec(h1, tb), _prev8_spec(h1, tb), _blk_spec(p, tb), _prev8_spec(p, tb)]
        + [_full_spec(c) for c in consts],
        out_specs=[_blk_spec(o, tb) for o in out_shape], out_shape=out_shape,
        compiler_params=_cparams(1))(h1, h1, _arr(p), _arr(p), *consts)


def _pre_a_bwd(h1, p, consts, cts, tb=TOK_BLOCK):
    tn = h1.shape[0]
    nb = tn // tb
    n_c = len(consts)
    ct_groups = [c if isinstance(c, (tuple, list)) else (c,) for c in cts]
    ct_flat = [a for grp in ct_groups for a in grp]
    n_ct = len(ct_flat)

    def body(*refs):
        h1_ref, h1h_ref, p_ref, ph_ref = refs[:4]
        c_refs = refs[4:4 + n_c]
        ct_refs = refs[4 + n_c:4 + n_c + n_ct]
        dh1_ref, dp_ref = refs[4 + n_c + n_ct:6 + n_c + n_ct]
        dc_refs = refs[6 + n_c + n_ct:6 + 2 * n_c + n_ct]
        ch_ref, cp_ref = refs[-2], refs[-1]
        i = pl.program_id(0)
        first_block = i == nb - 1
        h1p = jnp.where(first_block, 0.0, h1h_ref[7:8, :])
        pp = jnp.where(first_block, 0.0, ph_ref[7:8, :])
        ct_v, q = [], 0
        for grp in ct_groups:
            s = ct_refs[q][...]
            for r in ct_refs[q + 1:q + len(grp)]:
                s = s + r[...]
            q += len(grp)
            ct_v.append(s)
        _, vjp = jax.vjp(_pre_a_fn, h1_ref[...], h1p, p_ref[...], pp, *[c[...] for c in c_refs])
        grads = vjp(tuple(ct_v))

        @pl.when(i == 0)
        def _():
            ch_ref[...] = jnp.zeros(ch_ref.shape, F32)
            cp_ref[...] = jnp.zeros(cp_ref.shape, F32)
            for r in dc_refs:
                r[...] = jnp.zeros(r.shape, F32)

        rowh = lax.broadcasted_iota(jnp.int32, (tb, h1.shape[1]), 0)
        rowp = lax.broadcasted_iota(jnp.int32, (tb, p.shape[1]), 0)
        dh1_ref[...] = grads[0] + jnp.where(rowh == tb - 1, jnp.broadcast_to(ch_ref[0:1, :], rowh.shape), 0.0)
        dp_ref[...] = (grads[2] + jnp.where(rowp == tb - 1, jnp.broadcast_to(cp_ref[0:1, :], rowp.shape), 0.0)
                       ).astype(dp_ref.dtype)
        ch_ref[0:1, :] = grads[1]
        cp_ref[0:1, :] = grads[3]
        for j, r in enumerate(dc_refs):
            r[...] += grads[4 + j]

    ins = [h1, h1, _arr(p), _arr(p)] + list(consts) + ct_flat
    in_specs = ([_blk_spec(h1, tb, nb), _prev8_spec(h1, tb, nb), _blk_spec(p, tb, nb), _prev8_spec(p, tb, nb)]
                + [_full_spec(c) for c in consts] + [_blk_spec(a, tb, nb) for a in ct_flat])
    out_shape = ([jax.ShapeDtypeStruct(h1.shape, F32), jax.ShapeDtypeStruct(p.shape, BF16)]
                 + [jax.ShapeDtypeStruct(c.shape, F32) for c in consts])
    out_specs = [_blk_spec(h1, tb, nb), _blk_spec(p, tb, nb)] + [_full_spec(c) for c in consts]
    return pl.pallas_call(body, name="rwkv_pre_a_bwd", grid=(nb,), in_specs=in_specs, out_specs=out_specs,
                          out_shape=out_shape,
                          scratch_shapes=[pltpu.VMEM((8, h1.shape[1]), F32), pltpu.VMEM((8, p.shape[1]), F32)],
                          compiler_params=_cparams(1))(*ins)


def _my_index():
    return 4 * lax.axis_index("x") + 2 * lax.axis_index("y") + lax.axis_index("c")


def _peer(k):
    x, y, c = lax.axis_index("x"), lax.axis_index("y"), lax.axis_index("c")
    px = 1 - x if k & 4 else x
    py = 1 - y if k & 2 else y
    pc = 1 - c if k & 1 else c
    return (px, py, pc), 4 * px + 2 * py + pc


def _xchg_sems(n):
    return [pltpu.SemaphoreType.DMA((n * (N_DEV - 1),)), pltpu.SemaphoreType.DMA((n * (N_DEV - 1),)),
            pltpu.SemaphoreType.DMA((n,))]


def _scatter_copies(srcs, dsts, sems, incoming=False):
    send_sems, recv_sems, local_sems = sems
    me = _my_index()
    local, remote = [], []
    for i, (s, d) in enumerate(zip(srcs, dsts)):
        if not incoming:
            local.append(pltpu.make_async_copy(s.at[me], d.at[me], local_sems.at[i]))
        for k in range(1, N_DEV):
            peer, plin = _peer(k)
            j = i * (N_DEV - 1) + k - 1
            s_slot, d_slot = (me, plin) if incoming else (plin, me)
            remote.append(pltpu.make_async_remote_copy(
                src_ref=s.at[s_slot], dst_ref=d.at[d_slot], send_sem=send_sems.at[j],
                recv_sem=recv_sems.at[j], device_id=peer, device_id_type=pl.DeviceIdType.MESH))
    return local, remote


def _scatter_start(srcs, dsts, sems):
    local, out = _scatter_copies(srcs, dsts, sems)
    for cp in local + out:
        cp.start()


def _scatter_wait(srcs, dsts, sems):
    for cp in _scatter_copies(srcs, dsts, sems, incoming=True)[1]:
        cp.wait_recv()
    local, out = _scatter_copies(srcs, dsts, sems)
    for cp in out:
        cp.wait_send()
    for cp in local:
        cp.wait()


_ICI_PEERS = (2, 4, 6)


def _gather_copies(srcs, dsts, sems, group):
    send_sems, recv_sems, local_sems = sems
    me = _my_index()
    sib, sib_lin = _peer(1)
    out = []
    for i, (s, d) in enumerate(zip(srcs, dsts)):
        def mk(q, src, dst, dev):
            j = i * (N_DEV - 1) + q
            return pltpu.make_async_remote_copy(src_ref=src, dst_ref=dst, send_sem=send_sems.at[j],
                                                recv_sem=recv_sems.at[j], device_id=dev,
                                                device_id_type=pl.DeviceIdType.MESH)
        if group == 'local':
            out.append(pltpu.make_async_copy(s, d.at[me], local_sems.at[i]))
        elif group == 'own':
            out.append(mk(0, s, d.at[me], sib))
        elif group == 'in_d2d':
            out.append(mk(0, s, d.at[sib_lin], sib))
        for jj, k in enumerate(_ICI_PEERS):
            peer, plin = _peer(k)
            plin_other = _peer(k + 1)[1]
            if group == 'own':
                out.append(mk(1 + jj, s, d.at[me], peer))
            elif group == 'in_ici':
                out.append(mk(1 + jj, s, d.at[plin], peer))
            elif group == 'pass_on':
                out.append(mk(4 + jj, d.at[plin], d.at[plin], sib))
            elif group == 'in_d2d':
                out.append(mk(4 + jj, d.at[plin_other], d.at[plin_other], sib))
    return out


def _gather_start(srcs, dsts, sems):
    for cp in _gather_copies(srcs, dsts, sems, 'local') + _gather_copies(srcs, dsts, sems, 'own'):
        cp.start()


def _gather_pass_on(srcs, dsts, sems):
    for cp in _gather_copies(srcs, dsts, sems, 'in_ici'):
        cp.wait_recv()
    for cp in _gather_copies(srcs, dsts, sems, 'pass_on'):
        cp.start()


def _gather_finish(srcs, dsts, sems):
    for cp in _gather_copies(srcs, dsts, sems, 'in_d2d'):
        cp.wait_recv()
    for cp in _gather_copies(srcs, dsts, sems, 'own') + _gather_copies(srcs, dsts, sems, 'pass_on'):
        cp.wait_send()
    for cp in _gather_copies(srcs, dsts, sems, 'local'):
        cp.wait()


def _xchg_out_shapes(srcs, scatter):
    return [jax.ShapeDtypeStruct(s.shape if scatter else (N_DEV,) + s.shape, s.dtype) for s in srcs]


_ANY = pl.BlockSpec(memory_space=pl.ANY)


def _exchange(name, srcs, scatter):
    n = len(srcs)

    def body(*refs):
        s, d, sems = refs[:n], refs[n:2 * n], refs[2 * n:]
        if scatter:
            _scatter_start(s, d, sems)
            _scatter_wait(s, d, sems)
        else:
            _gather_start(s, d, sems)
            _gather_pass_on(s, d, sems)
            _gather_finish(s, d, sems)

    return pl.pallas_call(body, name=name, in_specs=[_ANY] * n, out_specs=[_ANY] * n,
                          out_shape=_xchg_out_shapes(srcs, scatter), scratch_shapes=_xchg_sems(n))(*srcs)


_MM_DIMS = {'nn': (((1,), (0,)), ((), ())), 'nt': (((1,), (1,)), ((), ())), 'tn': (((0,), (0,)), ((), ()))}


def _cmm_raw(x, y, kind, split):
    dot = functools.partial(lax.dot_general, dimension_numbers=_MM_DIMS[kind], preferred_element_type=F32)
    xh, yh = x.astype(BF16), y.astype(BF16)
    out = dot(xh, yh)
    if split:
        xl = (x - xh.astype(F32)).astype(BF16)
        yl = (y - yh.astype(F32)).astype(BF16)
        out = out + (dot(xh, yl) + dot(xl, yh))
    return out


@functools.partial(jax.custom_vjp, nondiff_argnums=(2, 3))
def _cmm(x, y, kind, split=False):
    return _cmm_raw(x, y, kind, split)


def _cmm_fwd(x, y, kind, split):
    return _cmm_raw(x, y, kind, split), (x, y)


def _cmm_bwd(kind, split, res, g):
    x, y = res
    if kind == 'nn':
        return _cmm_raw(g, y, 'nt', split), _cmm_raw(x, g, 'tn', split)
    if kind == 'nt':
        return _cmm_raw(g, y, 'nn', split), _cmm_raw(g, x, 'tn', split)
    return _cmm_raw(y, g, 'nt', split), _cmm_raw(x, g, 'nn', split)


_cmm.defvjp(_cmm_fwd, _cmm_bwd)


def _tri_sum_raw(tri, x, kind):
    dot = functools.partial(lax.dot_general, dimension_numbers=_MM_DIMS[kind], preferred_element_type=F32)
    tb = tri.astype(BF16)
    hi, mid, lo = _split3(x)
    return (dot(tb, hi) + dot(tb, mid)) + dot(tb, lo)


@functools.partial(jax.custom_vjp, nondiff_argnums=(2,))
def _tri_sum(tri, x, kind):
    return _tri_sum_raw(tri, x, kind)


def _tri_sum_fwd(tri, x, kind):
    return _tri_sum_raw(tri, x, kind), tri


def _tri_sum_bwd(kind, tri, g):
    return jnp.zeros_like(tri), _tri_sum_raw(tri, g, 'tn' if kind == 'nn' else 'nn')


_tri_sum.defvjp(_tri_sum_fwd, _tri_sum_bwd)


def _chunk_fn(S0, r, lw, k, v, a, b):
    hs = range(len(r))
    C = r[0].shape[0]
    ii = lax.broadcasted_iota(jnp.int32, (C, C), 0)
    jj = lax.broadcasted_iota(jnp.int32, (C, C), 1)
    incl, strict = ii >= jj, ii > jj
    eye = (ii == jj).astype(F32)
    inclf = incl.astype(F32)
    cum = [_tri_sum(inclf, lw[h], 'nn') for h in hs]
    e_inv = [jnp.exp(-cum[h]) for h in hs]
    At = [a[h] * jnp.exp(cum[h] - lw[h]) for h in hs]
    Rt = [r[h] * jnp.exp(cum[h]) for h in hs]
    Kh = [k[h] * e_inv[h] for h in hs]
    Bh = [b[h] * e_inv[h] for h in hs]
    Mab = [jnp.where(strict, _cmm(At[h], Bh[h], 'nt'), 0.0) for h in hs]
    Mak = [jnp.where(strict, _cmm(At[h], Kh[h], 'nt'), 0.0) for h in hs]
    Mrk = [jnp.where(incl, _cmm(Rt[h], Kh[h], 'nt'), 0.0) for h in hs]
    Mrb = [jnp.where(incl, _cmm(Rt[h], Bh[h], 'nt'), 0.0) for h in hs]
    rhs = [_cmm(At[h], S0[h], 'nt') + _cmm(Mak[h], v[h], 'nn') for h in hs]
    P = Mab
    Tm = [eye + P[h] for h in hs]
    n = 1
    while 2 * n < C:
        P = [_cmm(P[h], P[h], 'nn', True) for h in hs]
        Tm = [_cmm(Tm[h], eye + P[h], 'nn', True) for h in hs]
        n *= 2
    U = [_cmm(Tm[h], rhs[h], 'nn', True) for h in hs]
    Y = [_cmm(Rt[h], S0[h], 'nt') + _cmm(Mrk[h], v[h], 'nn') + _cmm(Mrb[h], U[h], 'nn') for h in hs]
    gC = [jnp.exp(jnp.sum(lw[h], axis=0, keepdims=True)) for h in hs]
    SC = [S0[h] * gC[h] + _cmm(v[h], Kh[h] * gC[h], 'tn') + _cmm(U[h], Bh[h] * gC[h], 'tn') for h in hs]
    return tuple(Y), tuple(SC)


def _cscan_fwd(r, lw, k, v, a, b, xs):
    n_x = len(xs)
    tn = r.shape[0]
    H, Dh, Dv = RWKV_HEADS, RWKV_HEAD_DIM, RWKV_HEAD_DIM
    nc = tn // SCAN_CHUNK
    lanes = lambda h: slice(h * Dh, (h + 1) * Dh)
    heads = lambda ref: tuple(ref[:, lanes(h)] for h in range(H))
    mats = lambda ref: tuple(ref[h] for h in range(H))

    def body(r_ref, lw_ref, k_ref, v_ref, a_ref, b_ref, *rest):
        x_src, (y_ref, ck_ref) = rest[:n_x], rest[n_x:n_x + 2]
        x_dst, s_ref, sems = rest[n_x + 2:2 * n_x + 2], rest[2 * n_x + 2], rest[2 * n_x + 3:]

        @pl.when(pl.program_id(0) == 0)
        def _():
            s_ref[...] = jnp.zeros(s_ref.shape, F32)
            _gather_start(x_src, x_dst, sems)

        ck_ref[0] = s_ref[...]
        y, sc = _chunk_fn(mats(s_ref), heads(r_ref), heads(lw_ref), heads(k_ref), heads(v_ref), heads(a_ref),
                          heads(b_ref))
        for h in range(H):
            y_ref[:, lanes(h)] = y[h]
            s_ref[h] = sc[h]

        @pl.when(pl.program_id(0) == max(nc - 4, 0))
        def _():
            _gather_pass_on(x_src, x_dst, sems)

        @pl.when(pl.program_id(0) == nc - 1)
        def _():
            _gather_finish(x_src, x_dst, sems)

    hm = pl.BlockSpec((SCAN_CHUNK, H * Dh), lambda c: (c, 0))
    res = pl.pallas_call(
        body, name="rwkv_scan_fwd", grid=(nc,), in_specs=[hm] * 6 + [_ANY] * n_x,
        out_specs=[hm, pl.BlockSpec((1, H, Dv, Dh), lambda c: (c, 0, 0, 0))] + [_ANY] * n_x,
        out_shape=[jax.ShapeDtypeStruct((tn, H * Dh), F32), jax.ShapeDtypeStruct((nc, H, Dv, Dh), F32)]
        + _xchg_out_shapes(xs, False),
        scratch_shapes=[pltpu.VMEM((H, Dv, Dh), F32)] + _xchg_sems(n_x),
        compiler_params=_cparams(1))(r, lw, k, v, a, b, *xs)
    return res[0], res[1], res[2:]


def _cscan_bwd(r, lw, k, v, a, b, dy, ck, xs):
    n_x = len(xs)
    tn = r.shape[0]
    H, Dh, Dv = RWKV_HEADS, RWKV_HEAD_DIM, RWKV_HEAD_DIM
    nc = tn // SCAN_CHUNK
    lanes = lambda h: slice(h * Dh, (h + 1) * Dh)
    heads = lambda ref: tuple(ref[:, lanes(h)] for h in range(H))
    mats = lambda ref: tuple(ref[h] for h in range(H))

    def body(r_ref, lw_ref, k_ref, v_ref, a_ref, b_ref, dy_ref, ck_ref, *rest):
        x_src = rest[:n_x]
        d_refs = rest[n_x:n_x + 6]
        x_dst = rest[n_x + 6:2 * n_x + 6]
        g_ref = rest[2 * n_x + 6]
        sems = rest[2 * n_x + 7:]

        @pl.when(pl.program_id(0) == 0)
        def _():
            g_ref[...] = jnp.zeros(g_ref.shape, F32)
            _scatter_start(x_src, x_dst, sems)

        s0 = tuple(ck_ref[0, h] for h in range(H))
        _, vjp = jax.vjp(_chunk_fn, s0, heads(r_ref), heads(lw_ref), heads(k_ref), heads(v_ref), heads(a_ref),
                         heads(b_ref))
        grads = vjp((heads(dy_ref), mats(g_ref)))
        for h in range(H):
            g_ref[h] = grads[0][h]
            for d_ref, gz in zip(d_refs, grads[1:]):
                d_ref[:, lanes(h)] = gz[h]

        @pl.when(pl.program_id(0) == nc - 1)
        def _():
            _scatter_wait(x_src, x_dst, sems)

    hm = pl.BlockSpec((SCAN_CHUNK, H * Dh), lambda c: (nc - 1 - c, 0))
    hshape = jax.ShapeDtypeStruct((tn, H * Dh), F32)
    res = pl.pallas_call(
        body, name="rwkv_scan_bwd", grid=(nc,),
        in_specs=[hm] * 7 + [pl.BlockSpec((1, H, Dv, Dh), lambda c: (nc - 1 - c, 0, 0, 0))] + [_ANY] * n_x,
        out_specs=[hm] * 6 + [_ANY] * n_x, out_shape=[hshape] * 6 + _xchg_out_shapes(xs, True),
        scratch_shapes=[pltpu.VMEM((H, Dv, Dh), F32)] + _xchg_sems(n_x),
        compiler_params=_cparams(1))(r, lw, k, v, a, b, dy, ck, *xs)
    return res[:6], res[6:]


def _decay_mask(lg, i, j, blk):
    rows = lax.broadcasted_iota(jnp.int32, (blk, blk), 0)
    cols = lax.broadcasted_iota(jnp.int32, (blk, blk), 1)
    dd = (rows - cols + (i - j) * blk).astype(F32)
    return jnp.where(dd >= 0.0, jnp.exp(lg * jnp.maximum(dd, 0.0)), 0.0)


_NT = (((1,), (1,)), ((), ()))
_TN = (((0,), (0,)), ((), ()))


def _ret_attn_fwd(lg, q, k, v, v_col0=0, blk=ATT_BLOCK):
    tn = q.shape[0]
    Dh = RET_HEAD_DIM

    def body(lg_ref, q_ref, k_ref, v_ref, o_ref):
        i = pl.program_id(1)
        lgv = lg_ref[0][:, 0:1]
        qb = q_ref[...].astype(BF16)

        def jb(j, acc):
            ks = pl.ds(pl.multiple_of(j * blk, blk), blk)
            s = lax.dot_general(qb, k_ref[ks, :].astype(BF16), _NT, preferred_element_type=F32)
            s = s * _decay_mask(lgv, i, j, blk)
            return acc + jnp.dot(s.astype(BF16), v_ref[ks, :].astype(BF16), preferred_element_type=F32)

        o_ref[...] = lax.fori_loop(0, i + 1, jb, jnp.zeros((blk, Dh), F32))

    full = pl.BlockSpec((tn, Dh), lambda h, i: (0, h))
    qs = pl.BlockSpec((blk, Dh), lambda h, i: (i, h))
    return pl.pallas_call(
        body, name="ret_attn_fwd", grid=(RET_HEADS, tn // blk),
        in_specs=[pl.BlockSpec((1, 1, 128), lambda h, i: (h, 0, 0)), qs, full,
                  pl.BlockSpec((tn, Dh), lambda h, i: (0, v_col0 + h))],
        out_specs=qs, out_shape=jax.ShapeDtypeStruct(q.shape, F32), compiler_params=_cparams(2))(lg, q, k, v)


def _ret_attn_bwd(lg, q, k, v, do, v_col0=0, blk=ATT_BLOCK):
    tn = q.shape[0]
    nb = tn // blk
    Dh = RET_HEAD_DIM

    def body(lg_ref, q_ref, k_ref, v_ref, do_ref, dq_ref, dk_ref, dv_ref):
        lgv = lg_ref[0][:, 0:1]
        dk_ref[...] = jnp.zeros(dk_ref.shape, F32)
        dv_ref[...] = jnp.zeros(dv_ref.shape, F32)

        def ib(i, carry):
            qs = pl.ds(pl.multiple_of(i * blk, blk), blk)
            qb = q_ref[qs, :].astype(BF16)
            dob = do_ref[qs, :].astype(BF16)

            def jb(j, dq):
                ks = pl.ds(pl.multiple_of(j * blk, blk), blk)
                kb = k_ref[ks, :].astype(BF16)
                vb = v_ref[ks, :].astype(BF16)
                dm = _decay_mask(lgv, i, j, blk)
                s = lax.dot_general(qb, kb, _NT, preferred_element_type=F32) * dm
                ds = lax.dot_general(dob, vb, _NT, preferred_element_type=F32) * dm
                sb, dsb = s.astype(BF16), ds.astype(BF16)
                dv_ref[ks, :] += lax.dot_general(sb, dob, _TN, preferred_element_type=F32)
                dk_ref[ks, :] += lax.dot_general(dsb, qb, _TN, preferred_element_type=F32)
                return dq + jnp.dot(dsb, kb, preferred_element_type=F32)

            dq_ref[qs, :] = lax.fori_loop(0, i + 1, jb, jnp.zeros((blk, Dh), F32))
            return carry

        lax.fori_loop(0, nb, ib, 0)

    full = pl.BlockSpec((tn, Dh), lambda h: (0, h))
    sh = jax.ShapeDtypeStruct(q.shape, F32)
    return pl.pallas_call(
        body, name="ret_attn_bwd", grid=(RET_HEADS,),
        in_specs=[pl.BlockSpec((1, 1, 128), lambda h: (h, 0, 0)), full, full,
                  pl.BlockSpec((tn, Dh), lambda h: (0, v_col0 + h)), full],
        out_specs=[full, full, full], out_shape=[sh, sh, sh], compiler_params=_cparams(1))(lg, q, k, v, do)


def _next8_spec(a, tb):
    r = tb // 8
    last = a.shape[0] // 8 - 1
    return pl.BlockSpec((8, a.shape[1]), lambda i: (jnp.minimum((i + 1) * r, last), 0))


def _conv_taps(g_ext, cw_ref, cb_ref):
    return (cw_ref[2:3, :] * g_ext + cw_ref[1:2, :] * pltpu.roll(g_ext, 1, 0)
            + cw_ref[0:1, :] * pltpu.roll(g_ext, 2, 0) + cb_ref[...])


def _glu_fwd(gate, up, cw, cb, tb=TOK_BLOCK):
    tn = gate.shape[0]

    def body(g_ref, gh_ref, u_ref, cw_ref, cb_ref, o_ref):
        halo = jnp.where(pl.program_id(0) == 0, 0.0, gh_ref[...])
        g_ext = jnp.concatenate([halo, g_ref[...]], axis=0)
        gc = _conv_taps(g_ext, cw_ref, cb_ref)[8:, :]
        o_ref[...] = (gc * _sigmoid(gc) * u_ref[...]).astype(o_ref.dtype)

    return pl.pallas_call(
        body, name="glu_fwd", grid=(tn // tb,),
        in_specs=[_blk_spec(gate, tb), _prev8_spec(gate, tb), _blk_spec(up, tb), _full_spec(cw), _full_spec(cb)],
        out_specs=_blk_spec(gate, tb), out_shape=jax.ShapeDtypeStruct(gate.shape, BF16),
        compiler_params=_cparams(1))(gate, gate, up, cw, cb)


def _glu_bwd(gate, up, dact, cw, cb, tb=TOK_BLOCK):
    tn = gate.shape[0]
    nb = tn // tb

    def body(g_ref, gp_ref, gn_ref, u_ref, un_ref, d_ref, dn_ref, cw_ref, cb_ref, dg_ref, du_ref, dcw_ref, dcb_ref):
        i = pl.program_id(0)
        gprev = jnp.where(i == 0, 0.0, gp_ref[...])
        dnext = jnp.where(i == nb - 1, 0.0, dn_ref[...])
        g_ext = jnp.concatenate([gprev, g_ref[...], gn_ref[...]], axis=0)
        gc = _conv_taps(g_ext, cw_ref, cb_ref)[8:, :]
        u_e = jnp.concatenate([u_ref[...], un_ref[...]], axis=0)
        d_e = jnp.concatenate([d_ref[...], dnext], axis=0)
        s = _sigmoid(gc)
        dgc = d_e * u_e * (s * (1.0 + gc * (1.0 - s)))
        du_ref[...] = (d_ref[...] * (gc * s)[:tb, :]).astype(du_ref.dtype)
        n_e = tb + 8
        dg_ref[...] = (cw_ref[2:3, :] * dgc + cw_ref[1:2, :] * pltpu.roll(dgc, n_e - 1, 0)
                       + cw_ref[0:1, :] * pltpu.roll(dgc, n_e - 2, 0))[:tb, :].astype(dg_ref.dtype)

        @pl.when(i == 0)
        def _():
            dcw_ref[...] = jnp.zeros(dcw_ref.shape, F32)
            dcb_ref[...] = jnp.zeros(dcb_ref.shape, F32)

        dgc_b = dgc[:tb, :]
        g0 = g_ext[8:8 + tb, :]
        g1 = pltpu.roll(g_ext, 1, 0)[8:8 + tb, :]
        g2 = pltpu.roll(g_ext, 2, 0)[8:8 + tb, :]
        dcw_ref[2:3, :] += jnp.sum(dgc_b * g0, axis=0, keepdims=True)
        dcw_ref[1:2, :] += jnp.sum(dgc_b * g1, axis=0, keepdims=True)
        dcw_ref[0:1, :] += jnp.sum(dgc_b * g2, axis=0, keepdims=True)
        dcb_ref[...] += jnp.sum(dgc_b, axis=0, keepdims=True)

    sh = jax.ShapeDtypeStruct(gate.shape, BF16)
    return pl.pallas_call(
        body, name="glu_bwd", grid=(nb,),
        in_specs=[_blk_spec(gate, tb), _prev8_spec(gate, tb), _next8_spec(gate, tb), _blk_spec(up, tb),
                  _next8_spec(up, tb), _blk_spec(dact, tb), _next8_spec(dact, tb), _full_spec(cw), _full_spec(cb)],
        out_specs=[_blk_spec(gate, tb), _blk_spec(gate, tb), _full_spec(cw), _full_spec(cb)],
        out_shape=[sh, sh, jax.ShapeDtypeStruct(cw.shape, F32), jax.ShapeDtypeStruct(cb.shape, F32)],
        compiler_params=_cparams(1))(gate, gate, gate, up, up, dact, dact, cw, cb)


def _final_loss(x2, tgt, g, tb=TOK_BLOCK):
    tn, dm = x2.shape

    def body(x_ref, t_ref, g_ref, l_ref, dx_ref, dg_ref):
        y, vjp = jax.vjp(_rms_fn, x_ref[...], g_ref[...])
        err = y - t_ref[...]
        dx, dg = vjp(err * (1.0 / dm))

        @pl.when(pl.program_id(0) == 0)
        def _():
            l_ref[...] = jnp.zeros(l_ref.shape, F32)
            dg_ref[...] = jnp.zeros(dg_ref.shape, F32)

        part = 0.5 * jnp.sum(jnp.mean(err * err, axis=-1, keepdims=True), axis=0, keepdims=True)
        l_ref[...] += jnp.broadcast_to(part, l_ref.shape)
        dx_ref[...] = dx
        dg_ref[...] += dg

    return pl.pallas_call(
        body, name="final_loss", grid=(tn // tb,),
        in_specs=[_blk_spec(x2, tb), _blk_spec(tgt, tb), _full_spec(g)],
        out_specs=[pl.BlockSpec((8, 128), lambda i: (0, 0)), _blk_spec(x2, tb), _full_spec(g)],
        out_shape=[jax.ShapeDtypeStruct((8, 128), F32), jax.ShapeDtypeStruct(x2.shape, F32),
                   jax.ShapeDtypeStruct(g.shape, F32)],
        compiler_params=_cparams(1))(x2, tgt, g)


def _pad_cols(w, n):
    return jnp.pad(w, ((0, 0), (0, n - w.shape[1])))


def _pad_rows(w, n):
    return jnp.pad(w, ((0, n - w.shape[0]), (0, 0)))


def _local_step(x, tgt, W, late):
    tn = x.shape[0]
    Wd = RWKV_WIDTH
    row = lambda z: z.reshape(1, -1)
    g_mix, g_ffn, g_fin = row(W['norm_mix_g']), row(W['norm_ffn_g']), row(W['norm_final_g'])

    (h1,) = _tok_fwd("norm_mix_fwd", lambda a, g: (_rms_fn(a, g),), [x], [g_mix], [(D_MODEL,)])
    proj = _mm("proj_fwd", h1, W['w_in_t'], tb=True)
    p_rkv = _Cols(proj, 3 * Wd, 0)
    pre_consts = [row(W['rwkv_mu_w']), row(W['rwkv_mu_a']), row(W['rwkv_mu_g']), row(W['rwkv_mu_r']),
                  row(W['rwkv_mu_k']), row(W['rwkv_mu_v']), row(W['rwkv_w0']),
                  _pad_cols(W['rwkv_w1'], LORA_PAD), _pad_rows(W['rwkv_w2'], LORA_PAD), row(W['rwkv_a0']),
                  _pad_cols(W['rwkv_a1'], LORA_PAD), _pad_rows(W['rwkv_a2'], LORA_PAD),
                  W['rwkv_g1'], W['rwkv_g2'], row(W['rwkv_k_k']), row(W['rwkv_k_a'])]
    r, k, v, lw, nkk, b, g = _pre_a_fwd(h1, p_rkv, pre_consts)
    y_scan, ck, gathered = _cscan_fwd(r, lw, k, v, nkk, b, late)
    w_out, w_gate_t, w_up_t, w_down = [g_.reshape(-1, D_MODEL) for g_ in gathered]
    post_consts = [row(W['rwkv_lnx_w']), row(W['rwkv_lnx_b']), row(W['rwkv_r_k'])]
    (y_rwkv,) = _tok_fwd("rwkv_post_fwd", _rwkv_post_fn, [y_scan, r, k, v, g], post_consts, [(Wd,)],
                         out_dtypes=[BF16])

    pos = jnp.arange(tn, dtype=F32)
    half = RET_HEAD_DIM // 2
    inv_freq = ROPE_BASE ** (-jnp.arange(half, dtype=F32) / half)
    ang = pos[:, None] * inv_freq[None, :]
    cos2 = jnp.concatenate([jnp.cos(ang), jnp.cos(ang)], axis=1)
    sin2 = jnp.concatenate([-jnp.sin(ang), jnp.sin(ang)], axis=1)
    lg = jnp.log(1.0 - 2.0 ** (-5.0 - jnp.arange(RET_HEADS, dtype=F32)))
    lg = jnp.broadcast_to(lg[:, None, None], (RET_HEADS, 1, 128))
    q_p, k_p, g_ret = _Cols(proj, Wd, 3), _Cols(proj, Wd, 4), _Cols(proj, Wd, 6)
    v_col0 = 5 * Wd // RET_HEAD_DIM
    q_rot, k_rot = _tok_fwd("ret_rotary_fwd", _rotary_fn, [cos2, sin2, q_p, k_p], [], [(RET_WIDTH,)] * 2)
    y_ret_raw = _ret_attn_fwd(lg, q_rot, k_rot, proj, v_col0)
    gn_w = row(W['ret_gn_w'])
    (y_ret,) = _tok_fwd("ret_post_fwd", _ret_post_fn, [y_ret_raw, g_ret], [gn_w], [(RET_WIDTH,)],
                        out_dtypes=[BF16])

    ycat = jnp.concatenate([y_rwkv, y_ret], axis=1)
    x1 = _mm("out_proj_fwd", ycat, w_out, add=x)
    (h2,) = _tok_fwd("norm_ffn_fwd", lambda a_, g_: (_rms_fn(a_, g_),), [x1], [g_ffn], [(D_MODEL,)],
                     out_dtypes=[BF16])
    gate = _mm("ffn_gate_fwd", h2, w_gate_t, tb=True)
    up = _mm("ffn_up_fwd", h2, w_up_t, tb=True)
    cw = W['ffn_conv_w']
    cb = row(W['ffn_conv_b'])
    act = _glu_fwd(gate, up, cw, cb)
    x2 = _mm("ffn_down_fwd", act, w_down, add=x1)
    loss8, dx2, dg_fin = _final_loss(x2, tgt, g_fin)

    G = {'norm_final_g': dg_fin}
    dact = _mm("ffn_down_dx", dx2, w_down, tb=True)
    d_down = _mm("ffn_down_dw", act, dx2, ta=True, out_dtype=BF16)
    dgate, dup, dcw, dcb = _glu_bwd(gate, up, dact, cw, cb)
    G['ffn_conv_w'], G['ffn_conv_b'] = dcw, dcb
    dh2 = _mm("ffn_gate_dx", dgate, w_gate_t)
    dh2 = _mm("ffn_up_dx", dup, w_up_t, add=dh2)
    d_gate_t = _mm("ffn_gate_dw", dgate, h2, ta=True, out_dtype=BF16)
    d_up_t = _mm("ffn_up_dw", dup, h2, ta=True, out_dtype=BF16)
    dx1, G['norm_ffn_g'] = _tok_bwd("norm_ffn_bwd", lambda a_, g_: (_rms_fn(a_, g_),), [], [x1], [g_ffn], [dh2], add=dx2)
    dycat = _mm("out_proj_dx", dx1, w_out, tb=True)
    d_out = _mm("out_proj_dw", ycat, dx1, ta=True, out_dtype=BF16)
    late_grads = [z.reshape(N_DEV, -1, D_MODEL) for z in (d_out, d_gate_t, d_up_t, d_down)]
    dy_rwkv, dy_ret = _Cols(dycat, Wd, 0), _Cols(dycat, Wd, 1)

    dyr_raw, dg_ret, G['ret_gn_w'] = _tok_bwd("ret_post_bwd", _ret_post_fn, [], [y_ret_raw, g_ret], [gn_w], [dy_ret],
                                              tok_dtypes=[F32, BF16])
    dq_rot, dk_rot, dv_ret = _ret_attn_bwd(lg, q_rot, k_rot, proj, dyr_raw, v_col0)
    dq_p, dk_p = _tok_bwd("ret_rotary_bwd", _rotary_fn, [cos2, sin2], [q_p, k_p], [], [dq_rot, dk_rot],
                          tok_dtypes=[BF16, BF16])

    dy_scan, dr1, dk1, dv1, dg, G['rwkv_lnx_w'], G['rwkv_lnx_b'], G['rwkv_r_k'] = _tok_bwd(
        "rwkv_post_bwd", _rwkv_post_fn, [], [y_scan, r, k, v, g], post_consts, [dy_rwkv])
    (dr2, dlw, dk2, dv2, dnkk, db), late_parts = _cscan_bwd(r, lw, k, v, nkk, b, dy_scan, ck, late_grads)
    pre_cts = [(dr1, dr2), (dk1, dk2), (dv1, dv2), dlw, dnkk, db, dg]
    pre_out = _pre_a_bwd(h1, p_rkv, pre_consts, pre_cts)
    dh1_a, dp_rkv = pre_out[0], pre_out[1]
    (G['rwkv_mu_w'], G['rwkv_mu_a'], G['rwkv_mu_g'], G['rwkv_mu_r'], G['rwkv_mu_k'], G['rwkv_mu_v'], G['rwkv_w0'],
     dw1, dw2, G['rwkv_a0'], da1, da2, G['rwkv_g1'], G['rwkv_g2'], G['rwkv_k_k'], G['rwkv_k_a']) = pre_out[2:]
    G['rwkv_w1'], G['rwkv_w2'] = dw1[:, :64], dw2[:64, :]
    G['rwkv_a1'], G['rwkv_a2'] = da1[:, :64], da2[:64, :]

    dproj = jnp.concatenate([dp_rkv, dq_p, dk_p, dv_ret.astype(BF16), dg_ret], axis=1)
    dh1 = _mm("proj_dx", dproj, W['w_in_t'], add=dh1_a)
    G['w_in_t'] = _mm("proj_dw", dproj, h1, ta=True, out_dtype=BF16)
    dx, G['norm_mix_g'] = _tok_bwd("norm_mix_bwd", lambda a_, g_: (_rms_fn(a_, g_),), [], [x], [g_mix], [dh1], add=dx1)
    return loss8[0, 0], dx, G, late_parts


def _adamw(name, parts, w, m, v):
    rows, cols = w.shape
    sub = 8 * 4 // parts.dtype.itemsize
    tb = max(t for t in range(sub, 65, sub) if rows % t == 0) if rows > 64 else rows
    c1 = 1.0 - ADAM_B1 ** ADAM_STEP
    c2 = 1.0 - ADAM_B2 ** ADAM_STEP

    def body(p_ref, w_ref, m_ref, v_ref, g_ref, d_ref, nm_ref, nv_ref):
        g = p_ref[0].astype(F32)
        for d in range(1, N_DEV):
            g = g + p_ref[d].astype(F32)
        mn = ADAM_B1 * m_ref[...] + (1.0 - ADAM_B1) * g
        vn = ADAM_B2 * v_ref[...] + (1.0 - ADAM_B2) * (g * g)
        m_hat = mn / c1
        v_hat = vn / c2
        g_ref[...] = g
        d_ref[...] = -ADAM_LR * (m_hat / (jnp.sqrt(v_hat) + ADAM_EPS) + ADAM_WD * w_ref[...])
        nm_ref[...] = mn
        nv_ref[...] = vn

    spec = pl.BlockSpec((tb, cols), lambda i: (i, 0))
    sh = jax.ShapeDtypeStruct((rows, cols), F32)
    return pl.pallas_call(
        body, name=name, grid=(rows // tb,),
        in_specs=[pl.BlockSpec((N_DEV, tb, cols), lambda i: (0, i, 0)), spec, spec, spec],
        out_specs=[spec] * 4, out_shape=[sh] * 4, compiler_params=_cparams(1))(parts, w, m, v)


def _local_shape(name):
    gs, ax = SHARDED[name]
    ls = list(gs)
    ls[ax] //= N_DEV
    return tuple(ls)


def _seg(flat, seg):
    n = flat.shape[-1]
    pad = _round_up(n, seg) - n
    if pad:
        flat = jnp.pad(flat, [(0, 0)] * (flat.ndim - 1) + [(0, pad)])
    return flat


def _split3(w):
    hi = w.astype(BF16)
    r1 = w - hi.astype(F32)
    mid = r1.astype(BF16)
    lo = (r1 - mid.astype(F32)).astype(BF16)
    return hi, mid, lo


def _pack_small_shards(shards):
    pieces = []
    for name in SMALL_NAMES:
        flat = shards[name].reshape(-1)
        if name == 'ffn_conv_w':
            pieces += [_seg(p, BF16_SEG) for p in _split3(flat)]
        else:
            pieces.append(flat.astype(BF16))
    return jnp.concatenate(pieces).reshape(-1, 128)


def _unpack_small(gathered):
    flat = gathered.reshape(N_DEV, -1)
    out, off = {}, 0
    for name in SMALL_NAMES:
        gs, ax = SHARDED[name]
        ls = _local_shape(name)
        n = int(np.prod(ls))
        if name == 'ffn_conv_w':
            nseg = _round_up(n, BF16_SEG)
            hi, mid, lo = (flat[:, off + j * nseg: off + j * nseg + n].astype(F32) for j in range(3))
            sh = ((hi + mid) + lo).reshape(N_DEV, 3, -1)
            out[name] = jnp.swapaxes(sh, 0, 1).reshape(3, D_FF)
            off += 3 * nseg
        else:
            sh = flat[:, off:off + n].reshape((N_DEV,) + ls[1:])
            out[name] = sh.reshape(gs[1:]) if ax == 1 else jnp.swapaxes(sh, 0, 1).reshape(gs[1:])
            off += n
    return out


def _small_pieces(sharded, repl):
    return [sharded[n].reshape(-1) for n in SMALL_NAMES] + [repl[n].reshape(-1) for n in REPL_NAMES]


def _pack_small_local(d):
    flat = jnp.concatenate(_small_pieces(d, d))
    return _seg(flat, F32_SEG).reshape(-1, 128)


def _pack_small_grads(G):
    pieces = []
    for name in SMALL_NAMES:
        gs, ax = SHARDED[name]
        g = G[name]
        if name == 'ffn_conv_w':
            sh = jnp.swapaxes(g.reshape(3, N_DEV, -1), 0, 1)
        elif ax == 1:
            sh = g
        else:
            sh = jnp.swapaxes(g.reshape(g.shape[0], N_DEV, -1), 0, 1)
        pieces.append(sh.reshape(N_DEV, -1))
    rep = jnp.concatenate([G[n].reshape(-1) for n in REPL_NAMES])
    pieces.append(jnp.broadcast_to(rep[None, :], (N_DEV, rep.shape[0])))
    flat = _seg(jnp.concatenate(pieces, axis=1), F32_SEG)
    return flat.reshape(N_DEV, -1, 128)


def _unpack_small_local(packed, local_shapes):
    flat = packed.reshape(-1)
    out, off = {}, 0
    for name in SMALL_NAMES + REPL_NAMES:
        n = int(np.prod(local_shapes[name]))
        out[name] = flat[off:off + n].reshape(local_shapes[name])
        off += n
    return out


def kernel(x, *rest):
    nw = len(WEIGHT_NAMES)
    assert len(rest) == 3 * nw + 1
    weights = dict(zip(WEIGHT_NAMES, rest[:nw]))
    loss_target = rest[nw]
    moms = dict(zip(WEIGHT_NAMES, rest[nw + 1:2 * nw + 1]))
    vars_ = dict(zip(WEIGHT_NAMES, rest[2 * nw + 1:]))
    local_shapes = {n: weights[n].shape for n in WEIGHT_NAMES}

    def native2d(name, a):
        a2 = a.reshape(a.shape[-2], a.shape[-1])
        return a2.T if name in BIG_T else a2

    def from2d(name, a2):
        return (a2.T if name in BIG_T else a2).reshape(local_shapes[name])

    big_w = {n: native2d(n, weights[n]) for n in BIG_NAMES}
    w_in_t_sh = big_w['w_in'].astype(BF16)
    late = [big_w[n].astype(BF16) for n in LATE_NAMES]
    small_sh = _pack_small_shards({n: weights[n] for n in SMALL_NAMES})
    w_in_g, small_g = _exchange("weights_all_gather", [w_in_t_sh, small_sh], False)
    W = _unpack_small(small_g)
    W['w_in_t'] = w_in_g.reshape(-1, D_MODEL)
    for n in REPL_NAMES:
        W[n] = weights[n][0] if n != 'norm_final_g' else weights[n]

    loss, dx, G, late_parts = _local_step(x[0], loss_target[0], W, late)

    w_in_parts, small_parts = _exchange(
        "grads_all_to_all", [G['w_in_t'].reshape(N_DEV, -1, D_MODEL), _pack_small_grads(G)], True)
    results = {}
    for n, parts in zip(['w_in'] + LATE_NAMES, [w_in_parts] + list(late_parts)):
        res = _adamw("adamw_" + n, parts, big_w[n], native2d(n, moms[n]), native2d(n, vars_[n]))
        results[n] = [from2d(n, r) for r in res]
    small_res = _adamw("adamw_small", small_parts, _pack_small_local(weights), _pack_small_local(moms),
                       _pack_small_local(vars_))
    small_out = [_unpack_small_local(p, local_shapes) for p in small_res]

    loss = lax.psum(loss, ("x", "y", "c"))
    outs = [loss, dx[None]]
    for j in range(4):
        outs += [results[n][j] if n in results else small_out[j][n] for n in WEIGHT_NAMES]
    return tuple(outs)
```

```python
import functools
import math

import numpy as np
import jax
import jax.numpy as jnp
from jax import lax
from jax.experimental import pallas as pl
from jax.experimental.pallas import tpu as pltpu

F32 = jnp.float32
BF16 = jnp.bfloat16

N_DEV = 8
D_MODEL = 1024
RWKV_HEADS = 8
RWKV_HEAD_DIM = 64
RWKV_WIDTH = 512
RET_HEADS = 4
RET_HEAD_DIM = 128
RET_WIDTH = 512
LORA_PAD = 128
D_FF = 2816
NORM_EPS = 1e-6
RWKV_GN_EPS = 64e-5
RET_GN_EPS = 1e-5
ROPE_BASE = 10000.0
ADAM_LR, ADAM_B1, ADAM_B2, ADAM_EPS, ADAM_WD, ADAM_STEP = 0.001, 0.9, 0.999, 1e-08, 0.01, 10

VMEM_LIMIT = 56 * 1024 * 1024
TOK_BLOCK = 256
SCAN_CHUNK = 64
ATT_BLOCK = 512
BF16_SEG = 2048
F32_SEG = 1024

WEIGHT_NAMES = ['norm_mix_g', 'w_in', 'rwkv_mu_r', 'rwkv_mu_k', 'rwkv_mu_v', 'rwkv_mu_w', 'rwkv_mu_a',
                'rwkv_mu_g', 'rwkv_w0', 'rwkv_w1', 'rwkv_w2', 'rwkv_a0', 'rwkv_a1', 'rwkv_a2', 'rwkv_g1',
                'rwkv_g2', 'rwkv_k_k', 'rwkv_k_a', 'rwkv_r_k', 'rwkv_lnx_w', 'rwkv_lnx_b', 'ret_gn_w',
                'w_out', 'norm_ffn_g', 'ffn_w_gate', 'ffn_w_up', 'ffn_conv_w', 'ffn_conv_b', 'ffn_w_down',
                'norm_final_g']
SHARDED = {
    'w_in': ((1, 1024, 3584), 2), 'rwkv_w1': ((1, 1024, 64), 1), 'rwkv_w2': ((1, 64, 512), 2),
    'rwkv_a1': ((1, 1024, 64), 1), 'rwkv_a2': ((1, 64, 512), 2), 'rwkv_g1': ((1, 1024, 128), 1),
    'rwkv_g2': ((1, 128, 512), 2), 'w_out': ((1, 1024, 1024), 1), 'ffn_w_gate': ((1, 1024, 2816), 2),
    'ffn_w_up': ((1, 1024, 2816), 2), 'ffn_conv_w': ((1, 3, 1, 2816), 3), 'ffn_w_down': ((1, 2816, 1024), 1),
}
REPL_NAMES = [n for n in WEIGHT_NAMES if n not in SHARDED]
BIG_NAMES = ['w_in', 'w_out', 'ffn_w_gate', 'ffn_w_up', 'ffn_w_down']
BIG_T = ('w_in', 'ffn_w_gate', 'ffn_w_up')
LATE_NAMES = ['w_out', 'ffn_w_gate', 'ffn_w_up', 'ffn_w_down']
SMALL_NAMES = [n for n in WEIGHT_NAMES if n in SHARDED and n not in BIG_NAMES]


def _cparams(n_grid):
    return pltpu.CompilerParams(dimension_semantics=("arbitrary",) * n_grid, vmem_limit_bytes=VMEM_LIMIT)


def _round_up(n, m):
    return (n + m - 1) // m * m


@jax.custom_vjp
def _bdot(x, w):
    return jnp.dot(x.astype(BF16), w.astype(BF16), preferred_element_type=F32)


def _bdot_fwd(x, w):
    return _bdot(x, w), (x, w)


def _bdot_bwd(res, g):
    x, w = res
    gb = g.astype(BF16)
    dx = lax.dot_general(gb, w.astype(BF16), (((1,), (1,)), ((), ())), preferred_element_type=F32)
    dw = lax.dot_general(x.astype(BF16), gb, (((0,), (0,)), ((), ())), preferred_element_type=F32)
    return dx, dw.astype(w.dtype)


_bdot.defvjp(_bdot_fwd, _bdot_bwd)


@jax.custom_vjp
def _shift_rows(x, prev):
    rolled = pltpu.roll(x, 1, 0)
    row = lax.broadcasted_iota(jnp.int32, x.shape, 0)
    return jnp.where(row == 0, jnp.broadcast_to(prev, x.shape), rolled)


def _shift_rows_fwd(x, prev):
    return _shift_rows(x, prev), None


def _shift_rows_bwd(_, g):
    n = g.shape[0]
    rolled = pltpu.roll(g, n - 1, 0)
    row = lax.broadcasted_iota(jnp.int32, g.shape, 0)
    return jnp.where(row == n - 1, 0.0, rolled), g[0:1, :]


_shift_rows.defvjp(_shift_rows_fwd, _shift_rows_bwd)


@jax.custom_vjp
def _swap_halves(x):
    return pltpu.roll(x, 64, 1)


_swap_halves.defvjp(lambda x: (_swap_halves(x), None), lambda _, g: (pltpu.roll(g, 64, 1),))


def _sigmoid(x):
    return 1.0 / (1.0 + jnp.exp(-x))


def _softplus(x):
    return jnp.maximum(x, 0.0) + jnp.log(1.0 + jnp.exp(-jnp.abs(x)))


def _rms_fn(x, g):
    return x * lax.rsqrt(jnp.mean(x * x, axis=-1, keepdims=True) + NORM_EPS) * g


def _pre_a_fn(h1, h1p, p, pp, mu_w, mu_a, mu_g, mu_r, mu_k, mu_v, w0, w1, w2, a0, a1, a2, g1, g2, k_k, k_a):
    W = RWKV_WIDTH
    h1s = _shift_rows(h1, h1p)
    ps = _shift_rows(p, pp)
    dx = h1s - h1
    xw = h1 + dx * mu_w
    xa = h1 + dx * mu_a
    xg = h1 + dx * mu_g
    dp = ps - p
    r = p[:, 0:W] + dp[:, 0:W] * mu_r
    k0 = p[:, W:2 * W] + dp[:, W:2 * W] * mu_k
    v = p[:, 2 * W:3 * W] + dp[:, 2 * W:3 * W] * mu_v
    wl = w0 + _bdot(jnp.tanh(_bdot(xw, w1)), w2)
    w_log = -_softplus(-wl) - 0.5
    lw = -jnp.exp(w_log)
    a = _sigmoid(a0 + _bdot(_bdot(xa, a1), a2))
    g = _bdot(_sigmoid(_bdot(xg, g1)), g2)
    nkk, k, b = _pre_b_fn(k0, a, k_k, k_a)
    return r, k, v, lw, nkk, b, g


def _head_sum_raw(x):
    n = x.shape[1]
    ii = lax.broadcasted_iota(jnp.int32, (n, n), 0) // RWKV_HEAD_DIM
    jj = lax.broadcasted_iota(jnp.int32, (n, n), 1) // RWKV_HEAD_DIM
    ones = (ii == jj).astype(BF16)
    xh = x.astype(BF16)
    xl = (x - xh.astype(F32)).astype(BF16)
    return jnp.dot(xh, ones, preferred_element_type=F32) + jnp.dot(xl, ones, preferred_element_type=F32)


@jax.custom_vjp
def _head_sum(x):
    return _head_sum_raw(x)


_head_sum.defvjp(lambda x: (_head_sum_raw(x), None), lambda _, g: (_head_sum_raw(g),))


def _pre_b_fn(k0, a, k_k, k_a):
    kkr = k0 * k_k
    nrm = jnp.sqrt(_head_sum(kkr * kkr))
    kk = kkr / jnp.maximum(nrm, 1e-12)
    k = k0 * (1.0 + (a - 1.0) * k_a)
    return -kk, k, kk * a


def _rwkv_post_fn(y, r, k, v, g, lnx_w, lnx_b, r_k):
    inv = 1.0 / RWKV_HEAD_DIM
    mu = _head_sum(y) * inv
    yc = y - mu
    var = _head_sum(yc * yc) * inv
    yn = yc * lax.rsqrt(var + RWKV_GN_EPS) * lnx_w + lnx_b
    bonus = _head_sum(r * k * r_k) * v
    return ((yn + bonus) * g,)


def _rotary_fn(cos2, sin2, q, k):
    qs, ks = [], []
    for h in range(RET_HEADS):
        sl = slice(h * RET_HEAD_DIM, (h + 1) * RET_HEAD_DIM)
        qh, kh = q[:, sl], k[:, sl]
        qs.append(qh * cos2 + _swap_halves(qh) * sin2)
        ks.append((kh * cos2 + _swap_halves(kh) * sin2) * (RET_HEAD_DIM ** -0.5))
    return jnp.concatenate(qs, axis=1), jnp.concatenate(ks, axis=1)


def _ret_post_fn(y, gp, gn_w):
    outs = []
    for h in range(RET_HEADS):
        sl = slice(h * RET_HEAD_DIM, (h + 1) * RET_HEAD_DIM)
        yh = y[:, sl]
        mu = jnp.mean(yh, axis=-1, keepdims=True)
        yc = yh - mu
        var = jnp.mean(yc * yc, axis=-1, keepdims=True)
        outs.append(yc * lax.rsqrt(var + RET_GN_EPS) * gn_w[:, sl])
    yn = jnp.concatenate(outs, axis=1)
    return (gp * _sigmoid(gp) * yn,)


class _Cols:
    def __init__(self, array, width, block):
        self.array, self.width, self.block = array, width, block
        self.shape, self.ndim, self.dtype = (array.shape[0], width), 2, array.dtype


def _arr(a):
    return a.array if isinstance(a, _Cols) else a


def _blk_spec(a, tb, rev_nb=None):
    nd = a.ndim
    tail = (a.block,) if isinstance(a, _Cols) else (0,) * (nd - 1)
    if rev_nb is None:
        return pl.BlockSpec((tb,) + a.shape[1:], lambda i: (i,) + tail)
    return pl.BlockSpec((tb,) + a.shape[1:], lambda i: (rev_nb - 1 - i,) + tail)


def _full_spec(a):
    nd = a.ndim
    return pl.BlockSpec(a.shape, lambda i: (0,) * nd)


def _tok_fwd(name, fn, toks, consts, out_tails, tb=TOK_BLOCK, out_dtypes=None):
    out_dtypes = out_dtypes or [F32] * len(out_tails)
    n_in = len(toks) + len(consts)
    tn = toks[0].shape[0]

    def body(*refs):
        outs = fn(*[r[...] for r in refs[:n_in]])
        for r, o in zip(refs[n_in:], outs):
            r[...] = o.astype(r.dtype)

    out_shape = [jax.ShapeDtypeStruct((tn,) + tuple(s), dt) for s, dt in zip(out_tails, out_dtypes)]
    return pl.pallas_call(
        body, name=name, grid=(tn // tb,),
        in_specs=[_blk_spec(a, tb) for a in toks] + [_full_spec(c) for c in consts],
        out_specs=[_blk_spec(o, tb) for o in out_shape], out_shape=out_shape,
        compiler_params=_cparams(1))(*[_arr(a) for a in toks], *consts)


def _tok_bwd(name, fn, aux, toks, consts, cts, add=None, tb=TOK_BLOCK, tok_dtypes=None):
    n_aux, n_tok, n_c = len(aux), len(toks), len(consts)
    ct_groups = [c if isinstance(c, (tuple, list)) else (c,) for c in cts]
    ct_flat = [a for grp in ct_groups for a in grp]
    n_ct = len(ct_flat)
    n_add = 0 if add is None else 1
    tn = toks[0].shape[0]

    def body(*refs):
        pos = 0
        aux_v = [r[...] for r in refs[pos:pos + n_aux]]; pos += n_aux
        tok_v = [r[...] for r in refs[pos:pos + n_tok]]; pos += n_tok
        const_v = [r[...] for r in refs[pos:pos + n_c]]; pos += n_c
        ct_refs = refs[pos:pos + n_ct]; pos += n_ct
        add_refs = refs[pos:pos + n_add]; pos += n_add
        dtok_refs = refs[pos:pos + n_tok]; pos += n_tok
        dconst_refs = refs[pos:pos + n_c]
        ct_v, q = [], 0
        for grp in ct_groups:
            s = ct_refs[q][...]
            for r in ct_refs[q + 1:q + len(grp)]:
                s = s + r[...]
            q += len(grp)
            ct_v.append(s)
        _, vjp = jax.vjp(lambda *tc: fn(*aux_v, *tc), *tok_v, *const_v)
        grads = vjp(tuple(ct_v))
        for j, r in enumerate(dtok_refs):
            gj = grads[j]
            if j == 0 and n_add:
                gj = gj + add_refs[0][...]
            r[...] = gj.astype(r.dtype)

        @pl.when(pl.program_id(0) == 0)
        def _():
            for r in dconst_refs:
                r[...] = jnp.zeros(r.shape, F32)

        for j, r in enumerate(dconst_refs):
            r[...] += grads[n_tok + j]

    ins = list(aux) + list(toks) + list(consts) + ct_flat + ([add] if n_add else [])
    in_specs = ([_blk_spec(a, tb) for a in aux] + [_blk_spec(a, tb) for a in toks] + [_full_spec(c) for c in consts]
                + [_blk_spec(a, tb) for a in ct_flat] + ([_blk_spec(add, tb)] if n_add else []))
    tok_dtypes = tok_dtypes or [F32] * n_tok
    out_shape = ([jax.ShapeDtypeStruct(a.shape, dt) for a, dt in zip(toks, tok_dtypes)]
                 + [jax.ShapeDtypeStruct(c.shape, F32) for c in consts])
    out_specs = [_blk_spec(o, tb) for o in out_shape[:n_tok]] + [_full_spec(c) for c in consts]
    return pl.pallas_call(body, name=name, grid=(tn // tb,), in_specs=in_specs, out_specs=out_specs,
                          out_shape=out_shape, compiler_params=_cparams(1))(*[_arr(a) for a in ins])


MM_VMEM_BUDGET = 40 * 1024 * 1024
MM_STEP_SECONDS = 0.4e-6
MM_HBM_BYTES_PER_SECOND = 2.5e12
MM_XPOSE_SECONDS_PER_ELEM = 2e-12
MM_MXU_COLUMNS = 256
MM_MXU_FLOPS = 9e14


def _mm_tiles(m, n, kd, a_bytes, b_bytes, o_bytes, has_add, ta):
    divs = lambda d: [t for t in range(128, d + 1, 128) if d % t == 0]
    best = None
    for tm in divs(m):
        for tn in divs(n):
            for tk in divs(kd):
                ni, nj, nk = m // tm, n // tn, kd // tk
                vmem = (2 * tm * tk * a_bytes + 2 * tk * tn * b_bytes + tm * tn * 4 + 2 * tm * tn * o_bytes
                        + (2 * tm * tn * 4 if has_add else 0) + 2 * (tm * tk + tk * tn) + tm * tn * 4)
                if vmem > MM_VMEM_BUDGET:
                    continue
                a_traffic = m * kd * a_bytes * (nj if nk > 1 else 1)
                b_traffic = kd * n * b_bytes * (ni if nj * nk > 1 else 1)
                cost = ni * nj * nk * MM_STEP_SECONDS + (a_traffic + b_traffic) / MM_HBM_BYTES_PER_SECOND
                cost += 2.0 * m * kd * nj * max(tn, MM_MXU_COLUMNS) / MM_MXU_FLOPS
                if ta:
                    cost += m * kd * nj * MM_XPOSE_SECONDS_PER_ELEM
                if best is None or cost < best[0]:
                    best = (cost, tm, tn, tk)
    return best[1:]


def _mm(name, a, b, ta=False, tb=False, add=None, out_dtype=F32):
    if ta:
        kd, m = a.shape
    else:
        m, kd = a.shape
    if tb:
        n, kb = b.shape
    else:
        kb, n = b.shape
    assert kd == kb, (a.shape, b.shape)
    tm, tn, tk = _mm_tiles(m, n, kd, a.dtype.itemsize, b.dtype.itemsize, jnp.dtype(out_dtype).itemsize,
                           add is not None, ta)
    nk = kd // tk
    has_add = add is not None
    dims = (((0 if ta else 1,), (1 if tb else 0,)), ((), ()))

    def body(*refs):
        a_ref, b_ref = refs[0], refs[1]
        o_ref, acc_ref = refs[-2], refs[-1]
        k = pl.program_id(2)

        @pl.when(k == 0)
        def _():
            acc_ref[...] = refs[2][...] if has_add else jnp.zeros(acc_ref.shape, F32)

        acc_ref[...] += lax.dot_general(a_ref[...].astype(BF16), b_ref[...].astype(BF16), dims,
                                        preferred_element_type=F32)

        @pl.when(k == nk - 1)
        def _():
            o_ref[...] = acc_ref[...].astype(out_dtype)

    a_spec = pl.BlockSpec((tk, tm), lambda i, j, k: (k, i)) if ta else pl.BlockSpec((tm, tk), lambda i, j, k: (i, k))
    b_spec = pl.BlockSpec((tn, tk), lambda i, j, k: (j, k)) if tb else pl.BlockSpec((tk, tn), lambda i, j, k: (k, j))
    o_spec = pl.BlockSpec((tm, tn), lambda i, j, k: (i, j))
    ins = [a, b] + ([add] if has_add else [])
    in_specs = [a_spec, b_spec] + ([o_spec] if has_add else [])
    return pl.pallas_call(body, name=name, grid=(m // tm, n // tn, nk), in_specs=in_specs, out_specs=o_spec,
                          out_shape=jax.ShapeDtypeStruct((m, n), out_dtype),
                          scratch_shapes=[pltpu.VMEM((tm, tn), F32)], compiler_params=_cparams(3))(*ins)


def _prev8_spec(a, tb, rev_nb=None):
    r = tb // 8
    if rev_nb is None:
        return pl.BlockSpec((8, a.shape[1]), lambda i: (jnp.maximum(i * r - 1, 0), 0))
    return pl.BlockSpec((8, a.shape[1]), lambda i: (jnp.maximum((rev_nb - 1 - i) * r - 1, 0), 0))


def _pre_a_fwd(h1, p, consts, tb=TOK_BLOCK):
    tn = h1.shape[0]

    def body(h1_ref, h1h_ref, p_ref, ph_ref, *rest):
        c_refs, o_refs = rest[:len(consts)], rest[len(consts):]
        first = pl.program_id(0) == 0
        h1p = jnp.where(first, 0.0, h1h_ref[7:8, :])
        pp = jnp.where(first, 0.0, ph_ref[7:8, :])
        outs = _pre_a_fn(h1_ref[...], h1p, p_ref[...], pp, *[c[...] for c in c_refs])
        for r, o in zip(o_refs, outs):
            r[...] = o

    out_shape = [jax.ShapeDtypeStruct((tn, RWKV_WIDTH), F32) for _ in range(7)]
    return pl.pallas_call(
        body, name="rwkv_pre_a_fwd", grid=(tn // tb,),
        in_specs=[_blk_spec(h1, tb), _prev8_spec(h1, tb), _blk_spec(p, tb), _prev8_spec(p, tb)]
        + [_full_spec(c) for c in consts],
        out_specs=[_blk_spec(o, tb) for o in out_shape], out_shape=out_shape,
        compiler_params=_cparams(1))(h1, h1, _arr(p), _arr(p), *consts)


def _pre_a_bwd(h1, p, consts, cts, tb=TOK_BLOCK):
    tn = h1.shape[0]
    nb = tn // tb
    n_c = len(consts)
    ct_groups = [c if isinstance(c, (tuple, list)) else (c,) for c in cts]
    ct_flat = [a for grp in ct_groups for a in grp]
    n_ct = len(ct_flat)

    def body(*refs):
        h1_ref, h1h_ref, p_ref, ph_ref = refs[:4]
        c_refs = refs[4:4 + n_c]
        ct_refs = refs[4 + n_c:4 + n_c + n_ct]
        dh1_ref, dp_ref = refs[4 + n_c + n_ct:6 + n_c + n_ct]
        dc_refs = refs[6 + n_c + n_ct:6 + 2 * n_c + n_ct]
        ch_ref, cp_ref = refs[-2], refs[-1]
        i = pl.program_id(0)
        first_block = i == nb - 1
        h1p = jnp.where(first_block, 0.0, h1h_ref[7:8, :])
        pp = jnp.where(first_block, 0.0, ph_ref[7:8, :])
        ct_v, q = [], 0
        for grp in ct_groups:
            s = ct_refs[q][...]
            for r in ct_refs[q + 1:q + len(grp)]:
                s = s + r[...]
            q += len(grp)
            ct_v.append(s)
        _, vjp = jax.vjp(_pre_a_fn, h1_ref[...], h1p, p_ref[...], pp, *[c[...] for c in c_refs])
        grads = vjp(tuple(ct_v))

        @pl.when(i == 0)
        def _():
            ch_ref[...] = jnp.zeros(ch_ref.shape, F32)
            cp_ref[...] = jnp.zeros(cp_ref.shape, F32)
            for r in dc_refs:
                r[...] = jnp.zeros(r.shape, F32)

        rowh = lax.broadcasted_iota(jnp.int32, (tb, h1.shape[1]), 0)
        rowp = lax.broadcasted_iota(jnp.int32, (tb, p.shape[1]), 0)
        dh1_ref[...] = grads[0] + jnp.where(rowh == tb - 1, jnp.broadcast_to(ch_ref[0:1, :], rowh.shape), 0.0)
        dp_ref[...] = (grads[2] + jnp.where(rowp == tb - 1, jnp.broadcast_to(cp_ref[0:1, :], rowp.shape), 0.0)
                       ).astype(dp_ref.dtype)
        ch_ref[0:1, :] = grads[1]
        cp_ref[0:1, :] = grads[3]
        for j, r in enumerate(dc_refs):
            r[...] += grads[4 + j]

    ins = [h1, h1, _arr(p), _arr(p)] + list(consts) + ct_flat
    in_specs = ([_blk_spec(h1, tb, nb), _prev8_spec(h1, tb, nb), _blk_spec(p, tb, nb), _prev8_spec(p, tb, nb)]
                + [_full_spec(c) for c in consts] + [_blk_spec(a, tb, nb) for a in ct_flat])
    out_shape = ([jax.ShapeDtypeStruct(h1.shape, F32), jax.ShapeDtypeStruct(p.shape, BF16)]
                 + [jax.ShapeDtypeStruct(c.shape, F32) for c in consts])
    out_specs = [_blk_spec(h1, tb, nb), _blk_spec(p, tb, nb)] + [_full_spec(c) for c in consts]
    return pl.pallas_call(body, name="rwkv_pre_a_bwd", grid=(nb,), in_specs=in_specs, out_specs=out_specs,
                          out_shape=out_shape,
                          scratch_shapes=[pltpu.VMEM((8, h1.shape[1]), F32), pltpu.VMEM((8, p.shape[1]), F32)],
                          compiler_params=_cparams(1))(*ins)


def _my_index():
    return 4 * lax.axis_index("x") + 2 * lax.axis_index("y") + lax.axis_index("c")


def _peer(k):
    x, y, c = lax.axis_index("x"), lax.axis_index("y"), lax.axis_index("c")
    px = 1 - x if k & 4 else x
    py = 1 - y if k & 2 else y
    pc = 1 - c if k & 1 else c
    return (px, py, pc), 4 * px + 2 * py + pc


def _xchg_sems(n):
    return [pltpu.SemaphoreType.DMA((n * (N_DEV - 1),)), pltpu.SemaphoreType.DMA((n * (N_DEV - 1),)),
            pltpu.SemaphoreType.DMA((n,))]


def _scatter_copies(srcs, dsts, sems, incoming=False):
    send_sems, recv_sems, local_sems = sems
    me = _my_index()
    local, remote = [], []
    for i, (s, d) in enumerate(zip(srcs, dsts)):
        if not incoming:
            local.append(pltpu.make_async_copy(s.at[me], d.at[me], local_sems.at[i]))
        for k in range(1, N_DEV):
            peer, plin = _peer(k)
            j = i * (N_DEV - 1) + k - 1
            s_slot, d_slot = (me, plin) if incoming else (plin, me)
            remote.append(pltpu.make_async_remote_copy(
                src_ref=s.at[s_slot], dst_ref=d.at[d_slot], send_sem=send_sems.at[j],
                recv_sem=recv_sems.at[j], device_id=peer, device_id_type=pl.DeviceIdType.MESH))
    return local, remote


def _scatter_start(srcs, dsts, sems):
    local, out = _scatter_copies(srcs, dsts, sems)
    for cp in local + out:
        cp.start()


def _scatter_wait(srcs, dsts, sems):
    for cp in _scatter_copies(srcs, dsts, sems, incoming=True)[1]:
        cp.wait_recv()
    local, out = _scatter_copies(srcs, dsts, sems)
    for cp in out:
        cp.wait_send()
    for cp in local:
        cp.wait()


_ICI_PEERS = (2, 4, 6)


def _gather_copies(srcs, dsts, sems, group):
    send_sems, recv_sems, local_sems = sems
    me = _my_index()
    sib, sib_lin = _peer(1)
    out = []
    for i, (s, d) in enumerate(zip(srcs, dsts)):
        def mk(q, src, dst, dev):
            j = i * (N_DEV - 1) + q
            return pltpu.make_async_remote_copy(src_ref=src, dst_ref=dst, send_sem=send_sems.at[j],
                                                recv_sem=recv_sems.at[j], device_id=dev,
                                                device_id_type=pl.DeviceIdType.MESH)
        if group == 'local':
            out.append(pltpu.make_async_copy(s, d.at[me], local_sems.at[i]))
        elif group == 'own':
            out.append(mk(0, s, d.at[me], sib))
        elif group == 'in_d2d':
            out.append(mk(0, s, d.at[sib_lin], sib))
        for jj, k in enumerate(_ICI_PEERS):
            peer, plin = _peer(k)
            plin_other = _peer(k + 1)[1]
            if group == 'own':
                out.append(mk(1 + jj, s, d.at[me], peer))
            elif group == 'in_ici':
                out.append(mk(1 + jj, s, d.at[plin], peer))
            elif group == 'pass_on':
                out.append(mk(4 + jj, d.at[plin], d.at[plin], sib))
            elif group == 'in_d2d':
                out.append(mk(4 + jj, d.at[plin_other], d.at[plin_other], sib))
    return out


def _gather_start(srcs, dsts, sems):
    for cp in _gather_copies(srcs, dsts, sems, 'local') + _gather_copies(srcs, dsts, sems, 'own'):
        cp.start()


def _gather_pass_on(srcs, dsts, sems):
    for cp in _gather_copies(srcs, dsts, sems, 'in_ici'):
        cp.wait_recv()
    for cp in _gather_copies(srcs, dsts, sems, 'pass_on'):
        cp.start()


def _gather_finish(srcs, dsts, sems):
    for cp in _gather_copies(srcs, dsts, sems, 'in_d2d'):
        cp.wait_recv()
    for cp in _gather_copies(srcs, dsts, sems, 'own') + _gather_copies(srcs, dsts, sems, 'pass_on'):
        cp.wait_send()
    for cp in _gather_copies(srcs, dsts, sems, 'local'):
        cp.wait()


def _xchg_out_shapes(srcs, scatter):
    return [jax.ShapeDtypeStruct(s.shape if scatter else (N_DEV,) + s.shape, s.dtype) for s in srcs]


_ANY = pl.BlockSpec(memory_space=pl.ANY)


def _exchange(name, srcs, scatter):
    n = len(srcs)

    def body(*refs):
        s, d, sems = refs[:n], refs[n:2 * n], refs[2 * n:]
        if scatter:
            _scatter_start(s, d, sems)
            _scatter_wait(s, d, sems)
        else:
            _gather_start(s, d, sems)
            _gather_pass_on(s, d, sems)
            _gather_finish(s, d, sems)

    return pl.pallas_call(body, name=name, in_specs=[_ANY] * n, out_specs=[_ANY] * n,
                          out_shape=_xchg_out_shapes(srcs, scatter), scratch_shapes=_xchg_sems(n))(*srcs)


_MM_DIMS = {'nn': (((1,), (0,)), ((), ())), 'nt': (((1,), (1,)), ((), ())), 'tn': (((0,), (0,)), ((), ()))}


def _cmm_raw(x, y, kind, split):
    dot = functools.partial(lax.dot_general, dimension_numbers=_MM_DIMS[kind], preferred_element_type=F32)
    xh, yh = x.astype(BF16), y.astype(BF16)
    out = dot(xh, yh)
    if split or True:
        xl = (x - xh.astype(F32)).astype(BF16)
        yl = (y - yh.astype(F32)).astype(BF16)
        out = out + (dot(xh, yl) + dot(xl, yh))
    return out


@functools.partial(jax.custom_vjp, nondiff_argnums=(2, 3))
def _cmm(x, y, kind, split=False):
    return _cmm_raw(x, y, kind, split)


def _cmm_fwd(x, y, kind, split):
    return _cmm_raw(x, y, kind, split), (x, y)


def _cmm_bwd(kind, split, res, g):
    x, y = res
    if kind == 'nn':
        return _cmm_raw(g, y, 'nt', split), _cmm_raw(x, g, 'tn', split)
    if kind == 'nt':
        return _cmm_raw(g, y, 'nn', split), _cmm_raw(g, x, 'tn', split)
    return _cmm_raw(y, g, 'nt', split), _cmm_raw(x, g, 'nn', split)


_cmm.defvjp(_cmm_fwd, _cmm_bwd)


def _tri_sum_raw(tri, x, kind):
    dot = functools.partial(lax.dot_general, dimension_numbers=_MM_DIMS[kind], preferred_element_type=F32)
    tb = tri.astype(BF16)
    hi, mid, lo = _split3(x)
    return (dot(tb, hi) + dot(tb, mid)) + dot(tb, lo)


@functools.partial(jax.custom_vjp, nondiff_argnums=(2,))
def _tri_sum(tri, x, kind):
    return _tri_sum_raw(tri, x, kind)


def _tri_sum_fwd(tri, x, kind):
    return _tri_sum_raw(tri, x, kind), tri


def _tri_sum_bwd(kind, tri, g):
    return jnp.zeros_like(tri), _tri_sum_raw(tri, g, 'tn' if kind == 'nn' else 'nn')


_tri_sum.defvjp(_tri_sum_fwd, _tri_sum_bwd)


def _chunk_fn(S0, r, lw, k, v, a, b):
    hs = range(len(r))
    C = r[0].shape[0]
    ii = lax.broadcasted_iota(jnp.int32, (C, C), 0)
    jj = lax.broadcasted_iota(jnp.int32, (C, C), 1)
    incl, strict = ii >= jj, ii > jj
    eye = (ii == jj).astype(F32)
    inclf = incl.astype(F32)
    cum = [_tri_sum(inclf, lw[h], 'nn') for h in hs]
    e_inv = [jnp.exp(-cum[h]) for h in hs]
    At = [a[h] * jnp.exp(cum[h] - lw[h]) for h in hs]
    Rt = [r[h] * jnp.exp(cum[h]) for h in hs]
    Kh = [k[h] * e_inv[h] for h in hs]
    Bh = [b[h] * e_inv[h] for h in hs]
    Mab = [jnp.where(strict, _cmm(At[h], Bh[h], 'nt'), 0.0) for h in hs]
    Mak = [jnp.where(strict, _cmm(At[h], Kh[h], 'nt'), 0.0) for h in hs]
    Mrk = [jnp.where(incl, _cmm(Rt[h], Kh[h], 'nt'), 0.0) for h in hs]
    Mrb = [jnp.where(incl, _cmm(Rt[h], Bh[h], 'nt'), 0.0) for h in hs]
    rhs = [_cmm(At[h], S0[h], 'nt') + _cmm(Mak[h], v[h], 'nn') for h in hs]
    P = Mab
    Tm = [eye + P[h] for h in hs]
    n = 1
    while 2 * n < C:
        P = [_cmm(P[h], P[h], 'nn', True) for h in hs]
        Tm = [_cmm(Tm[h], eye + P[h], 'nn', True) for h in hs]
        n *= 2
    U = [_cmm(Tm[h], rhs[h], 'nn', True) for h in hs]
    Y = [_cmm(Rt[h], S0[h], 'nt') + _cmm(Mrk[h], v[h], 'nn') + _cmm(Mrb[h], U[h], 'nn') for h in hs]
    gC = [jnp.exp(jnp.sum(lw[h], axis=0, keepdims=True)) for h in hs]
    SC = [S0[h] * gC[h] + _cmm(v[h], Kh[h] * gC[h], 'tn') + _cmm(U[h], Bh[h] * gC[h], 'tn') for h in hs]
    return tuple(Y), tuple(SC)


def _cscan_fwd(r, lw, k, v, a, b, xs):
    n_x = len(xs)
    tn = r.shape[0]
    H, Dh, Dv = RWKV_HEADS, RWKV_HEAD_DIM, RWKV_HEAD_DIM
    nc = tn // SCAN_CHUNK
    lanes = lambda h: slice(h * Dh, (h + 1) * Dh)
    heads = lambda ref: tuple(ref[:, lanes(h)] for h in range(H))
    mats = lambda ref: tuple(ref[h] for h in range(H))

    def body(r_ref, lw_ref, k_ref, v_ref, a_ref, b_ref, *rest):
        x_src, (y_ref, ck_ref) = rest[:n_x], rest[n_x:n_x + 2]
        x_dst, s_ref, sems = rest[n_x + 2:2 * n_x + 2], rest[2 * n_x + 2], rest[2 * n_x + 3:]

        @pl.when(pl.program_id(0) == 0)
        def _():
            s_ref[...] = jnp.zeros(s_ref.shape, F32)
            _gather_start(x_src, x_dst, sems)

        ck_ref[0] = s_ref[...]
        y, sc = _chunk_fn(mats(s_ref), heads(r_ref), heads(lw_ref), heads(k_ref), heads(v_ref), heads(a_ref),
                          heads(b_ref))
        for h in range(H):
            y_ref[:, lanes(h)] = y[h]
            s_ref[h] = sc[h]

        @pl.when(pl.program_id(0) == max(nc - 4, 0))
        def _():
            _gather_pass_on(x_src, x_dst, sems)

        @pl.when(pl.program_id(0) == nc - 1)
        def _():
            _gather_finish(x_src, x_dst, sems)

    hm = pl.BlockSpec((SCAN_CHUNK, H * Dh), lambda c: (c, 0))
    res = pl.pallas_call(
        body, name="rwkv_scan_fwd", grid=(nc,), in_specs=[hm] * 6 + [_ANY] * n_x,
        out_specs=[hm, pl.BlockSpec((1, H, Dv, Dh), lambda c: (c, 0, 0, 0))] + [_ANY] * n_x,
        out_shape=[jax.ShapeDtypeStruct((tn, H * Dh), F32), jax.ShapeDtypeStruct((nc, H, Dv, Dh), F32)]
        + _xchg_out_shapes(xs, False),
        scratch_shapes=[pltpu.VMEM((H, Dv, Dh), F32)] + _xchg_sems(n_x),
        compiler_params=_cparams(1))(r, lw, k, v, a, b, *xs)
    return res[0], res[1], res[2:]


def _cscan_bwd(r, lw, k, v, a, b, dy, ck, xs):
    n_x = len(xs)
    tn = r.shape[0]
    H, Dh, Dv = RWKV_HEADS, RWKV_HEAD_DIM, RWKV_HEAD_DIM
    nc = tn // SCAN_CHUNK
    lanes = lambda h: slice(h * Dh, (h + 1) * Dh)
    heads = lambda ref: tuple(ref[:, lanes(h)] for h in range(H))
    mats = lambda ref: tuple(ref[h] for h in range(H))

    def body(r_ref, lw_ref, k_ref, v_ref, a_ref, b_ref, dy_ref, ck_ref, *rest):
        x_src = rest[:n_x]
        d_refs = rest[n_x:n_x + 6]
        x_dst = rest[n_x + 6:2 * n_x + 6]
        g_ref = rest[2 * n_x + 6]
        sems = rest[2 * n_x + 7:]

        @pl.when(pl.program_id(0) == 0)
        def _():
            g_ref[...] = jnp.zeros(g_ref.shape, F32)
            _scatter_start(x_src, x_dst, sems)

        s0 = tuple(ck_ref[0, h] for h in range(H))
        _, vjp = jax.vjp(_chunk_fn, s0, heads(r_ref), heads(lw_ref), heads(k_ref), heads(v_ref), heads(a_ref),
                         heads(b_ref))
        grads = vjp((heads(dy_ref), mats(g_ref)))
        for h in range(H):
            g_ref[h] = grads[0][h]
            for d_ref, gz in zip(d_refs, grads[1:]):
                d_ref[:, lanes(h)] = gz[h]

        @pl.when(pl.program_id(0) == nc - 1)
        def _():
            _scatter_wait(x_src, x_dst, sems)

    hm = pl.BlockSpec((SCAN_CHUNK, H * Dh), lambda c: (nc - 1 - c, 0))
    hshape = jax.ShapeDtypeStruct((tn, H * Dh), F32)
    res = pl.pallas_call(
        body, name="rwkv_scan_bwd", grid=(nc,),
        in_specs=[hm] * 7 + [pl.BlockSpec((1, H, Dv, Dh), lambda c: (nc - 1 - c, 0, 0, 0))] + [_ANY] * n_x,
        out_specs=[hm] * 6 + [_ANY] * n_x, out_shape=[hshape] * 6 + _xchg_out_shapes(xs, True),
        scratch_shapes=[pltpu.VMEM((H, Dv, Dh), F32)] + _xchg_sems(n_x),
        compiler_params=_cparams(1))(r, lw, k, v, a, b, dy, ck, *xs)
    return res[:6], res[6:]


def _decay_mask(lg, i, j, blk):
    rows = lax.broadcasted_iota(jnp.int32, (blk, blk), 0)
    cols = lax.broadcasted_iota(jnp.int32, (blk, blk), 1)
    dd = (rows - cols + (i - j) * blk).astype(F32)
    return jnp.where(dd >= 0.0, jnp.exp(lg * jnp.maximum(dd, 0.0)), 0.0)


_NT = (((1,), (1,)), ((), ()))
_TN = (((0,), (0,)), ((), ()))


def _ret_attn_fwd(lg, q, k, v, v_col0=0, blk=ATT_BLOCK):
    tn = q.shape[0]
    Dh = RET_HEAD_DIM

    def body(lg_ref, q_ref, k_ref, v_ref, o_ref):
        i = pl.program_id(1)
        lgv = lg_ref[0][:, 0:1]
        qb = q_ref[...].astype(BF16)

        def jb(j, acc):
            ks = pl.ds(pl.multiple_of(j * blk, blk), blk)
            s = lax.dot_general(qb, k_ref[ks, :].astype(BF16), _NT, preferred_element_type=F32)
            s = s * _decay_mask(lgv, i, j, blk)
            return acc + jnp.dot(s.astype(BF16), v_ref[ks, :].astype(BF16), preferred_element_type=F32)

        o_ref[...] = lax.fori_loop(0, i + 1, jb, jnp.zeros((blk, Dh), F32))

    full = pl.BlockSpec((tn, Dh), lambda h, i: (0, h))
    qs = pl.BlockSpec((blk, Dh), lambda h, i: (i, h))
    return pl.pallas_call(
        body, name="ret_attn_fwd", grid=(RET_HEADS, tn // blk),
        in_specs=[pl.BlockSpec((1, 1, 128), lambda h, i: (h, 0, 0)), qs, full,
                  pl.BlockSpec((tn, Dh), lambda h, i: (0, v_col0 + h))],
        out_specs=qs, out_shape=jax.ShapeDtypeStruct(q.shape, F32), compiler_params=_cparams(2))(lg, q, k, v)


def _ret_attn_bwd(lg, q, k, v, do, v_col0=0, blk=ATT_BLOCK):
    tn = q.shape[0]
    nb = tn // blk
    Dh = RET_HEAD_DIM

    def body(lg_ref, q_ref, k_ref, v_ref, do_ref, dq_ref, dk_ref, dv_ref):
        lgv = lg_ref[0][:, 0:1]
        dk_ref[...] = jnp.zeros(dk_ref.shape, F32)
        dv_ref[...] = jnp.zeros(dv_ref.shape, F32)

        def ib(i, carry):
            qs = pl.ds(pl.multiple_of(i * blk, blk), blk)
            qb = q_ref[qs, :].astype(BF16)
            dob = do_ref[qs, :].astype(BF16)

            def jb(j, dq):
                ks = pl.ds(pl.multiple_of(j * blk, blk), blk)
                kb = k_ref[ks, :].astype(BF16)
                vb = v_ref[ks, :].astype(BF16)
                dm = _decay_mask(lgv, i, j, blk)
                s = lax.dot_general(qb, kb, _NT, preferred_element_type=F32) * dm
                ds = lax.dot_general(dob, vb, _NT, preferred_element_type=F32) * dm
                sb, dsb = s.astype(BF16), ds.astype(BF16)
                dv_ref[ks, :] += lax.dot_general(sb, dob, _TN, preferred_element_type=F32)
                dk_ref[ks, :] += lax.dot_general(dsb, qb, _TN, preferred_element_type=F32)
                return dq + jnp.dot(dsb, kb, preferred_element_type=F32)

            dq_ref[qs, :] = lax.fori_loop(0, i + 1, jb, jnp.zeros((blk, Dh), F32))
            return carry

        lax.fori_loop(0, nb, ib, 0)

    full = pl.BlockSpec((tn, Dh), lambda h: (0, h))
    sh = jax.ShapeDtypeStruct(q.shape, F32)
    return pl.pallas_call(
        body, name="ret_attn_bwd", grid=(RET_HEADS,),
        in_specs=[pl.BlockSpec((1, 1, 128), lambda h: (h, 0, 0)), full, full,
                  pl.BlockSpec((tn, Dh), lambda h: (0, v_col0 + h)), full],
        out_specs=[full, full, full], out_shape=[sh, sh, sh], compiler_params=_cparams(1))(lg, q, k, v, do)


def _next8_spec(a, tb):
    r = tb // 8
    last = a.shape[0] // 8 - 1
    return pl.BlockSpec((8, a.shape[1]), lambda i: (jnp.minimum((i + 1) * r, last), 0))


def _conv_taps(g_ext, cw_ref, cb_ref):
    return (cw_ref[2:3, :] * g_ext + cw_ref[1:2, :] * pltpu.roll(g_ext, 1, 0)
            + cw_ref[0:1, :] * pltpu.roll(g_ext, 2, 0) + cb_ref[...])


def _glu_fwd(gate, up, cw, cb, tb=TOK_BLOCK):
    tn = gate.shape[0]

    def body(g_ref, gh_ref, u_ref, cw_ref, cb_ref, o_ref):
        halo = jnp.where(pl.program_id(0) == 0, 0.0, gh_ref[...])
        g_ext = jnp.concatenate([halo, g_ref[...]], axis=0)
        gc = _conv_taps(g_ext, cw_ref, cb_ref)[8:, :]
        o_ref[...] = (gc * _sigmoid(gc) * u_ref[...]).astype(o_ref.dtype)

    return pl.pallas_call(
        body, name="glu_fwd", grid=(tn // tb,),
        in_specs=[_blk_spec(gate, tb), _prev8_spec(gate, tb), _blk_spec(up, tb), _full_spec(cw), _full_spec(cb)],
        out_specs=_blk_spec(gate, tb), out_shape=jax.ShapeDtypeStruct(gate.shape, BF16),
        compiler_params=_cparams(1))(gate, gate, up, cw, cb)


def _glu_bwd(gate, up, dact, cw, cb, tb=TOK_BLOCK):
    tn = gate.shape[0]
    nb = tn // tb

    def body(g_ref, gp_ref, gn_ref, u_ref, un_ref, d_ref, dn_ref, cw_ref, cb_ref, dg_ref, du_ref, dcw_ref, dcb_ref):
        i = pl.program_id(0)
        gprev = jnp.where(i == 0, 0.0, gp_ref[...])
        dnext = jnp.where(i == nb - 1, 0.0, dn_ref[...])
        g_ext = jnp.concatenate([gprev, g_ref[...], gn_ref[...]], axis=0)
        gc = _conv_taps(g_ext, cw_ref, cb_ref)[8:, :]
        u_e = jnp.concatenate([u_ref[...], un_ref[...]], axis=0)
        d_e = jnp.concatenate([d_ref[...], dnext], axis=0)
        s = _sigmoid(gc)
        dgc = d_e * u_e * (s * (1.0 + gc * (1.0 - s)))
        du_ref[...] = (d_ref[...] * (gc * s)[:tb, :]).astype(du_ref.dtype)
        n_e = tb + 8
        dg_ref[...] = (cw_ref[2:3, :] * dgc + cw_ref[1:2, :] * pltpu.roll(dgc, n_e - 1, 0)
                       + cw_ref[0:1, :] * pltpu.roll(dgc, n_e - 2, 0))[:tb, :].astype(dg_ref.dtype)

        @pl.when(i == 0)
        def _():
            dcw_ref[...] = jnp.zeros(dcw_ref.shape, F32)
            dcb_ref[...] = jnp.zeros(dcb_ref.shape, F32)

        dgc_b = dgc[:tb, :]
        g0 = g_ext[8:8 + tb, :]
        g1 = pltpu.roll(g_ext, 1, 0)[8:8 + tb, :]
        g2 = pltpu.roll(g_ext, 2, 0)[8:8 + tb, :]
        dcw_ref[2:3, :] += jnp.sum(dgc_b * g0, axis=0, keepdims=True)
        dcw_ref[1:2, :] += jnp.sum(dgc_b * g1, axis=0, keepdims=True)
        dcw_ref[0:1, :] += jnp.sum(dgc_b * g2, axis=0, keepdims=True)
        dcb_ref[...] += jnp.sum(dgc_b, axis=0, keepdims=True)

    sh = jax.ShapeDtypeStruct(gate.shape, BF16)
    return pl.pallas_call(
        body, name="glu_bwd", grid=(nb,),
        in_specs=[_blk_spec(gate, tb), _prev8_spec(gate, tb), _next8_spec(gate, tb), _blk_spec(up, tb),
                  _next8_spec(up, tb), _blk_spec(dact, tb), _next8_spec(dact, tb), _full_spec(cw), _full_spec(cb)],
        out_specs=[_blk_spec(gate, tb), _blk_spec(gate, tb), _full_spec(cw), _full_spec(cb)],
        out_shape=[sh, sh, jax.ShapeDtypeStruct(cw.shape, F32), jax.ShapeDtypeStruct(cb.shape, F32)],
        compiler_params=_cparams(1))(gate, gate, gate, up, up, dact, dact, cw, cb)


def _final_loss(x2, tgt, g, tb=TOK_BLOCK):
    tn, dm = x2.shape

    def body(x_ref, t_ref, g_ref, l_ref, dx_ref, dg_ref):
        y, vjp = jax.vjp(_rms_fn, x_ref[...], g_ref[...])
        err = y - t_ref[...]
        dx, dg = vjp(err * (1.0 / dm))

        @pl.when(pl.program_id(0) == 0)
        def _():
            l_ref[...] = jnp.zeros(l_ref.shape, F32)
            dg_ref[...] = jnp.zeros(dg_ref.shape, F32)

        part = 0.5 * jnp.sum(jnp.mean(err * err, axis=-1, keepdims=True), axis=0, keepdims=True)
        l_ref[...] += jnp.broadcast_to(part, l_ref.shape)
        dx_ref[...] = dx
        dg_ref[...] += dg

    return pl.pallas_call(
        body, name="final_loss", grid=(tn // tb,),
        in_specs=[_blk_spec(x2, tb), _blk_spec(tgt, tb), _full_spec(g)],
        out_specs=[pl.BlockSpec((8, 128), lambda i: (0, 0)), _blk_spec(x2, tb), _full_spec(g)],
        out_shape=[jax.ShapeDtypeStruct((8, 128), F32), jax.ShapeDtypeStruct(x2.shape, F32),
                   jax.ShapeDtypeStruct(g.shape, F32)],
        compiler_params=_cparams(1))(x2, tgt, g)


def _pad_cols(w, n):
    return jnp.pad(w, ((0, 0), (0, n - w.shape[1])))


def _pad_rows(w, n):
    return jnp.pad(w, ((0, n - w.shape[0]), (0, 0)))


def _local_step(x, tgt, W, late):
    tn = x.shape[0]
    Wd = RWKV_WIDTH
    row = lambda z: z.reshape(1, -1)
    g_mix, g_ffn, g_fin = row(W['norm_mix_g']), row(W['norm_ffn_g']), row(W['norm_final_g'])

    (h1,) = _tok_fwd("norm_mix_fwd", lambda a, g: (_rms_fn(a, g),), [x], [g_mix], [(D_MODEL,)])
    proj = _mm("proj_fwd", h1, W['w_in_t'], tb=True)
    p_rkv = _Cols(proj, 3 * Wd, 0)
    pre_consts = [row(W['rwkv_mu_w']), row(W['rwkv_mu_a']), row(W['rwkv_mu_g']), row(W['rwkv_mu_r']),
                  row(W['rwkv_mu_k']), row(W['rwkv_mu_v']), row(W['rwkv_w0']),
                  _pad_cols(W['rwkv_w1'], LORA_PAD), _pad_rows(W['rwkv_w2'], LORA_PAD), row(W['rwkv_a0']),
                  _pad_cols(W['rwkv_a1'], LORA_PAD), _pad_rows(W['rwkv_a2'], LORA_PAD),
                  W['rwkv_g1'], W['rwkv_g2'], row(W['rwkv_k_k']), row(W['rwkv_k_a'])]
    r, k, v, lw, nkk, b, g = _pre_a_fwd(h1, p_rkv, pre_consts)
    y_scan, ck, gathered = _cscan_fwd(r, lw, k, v, nkk, b, late)
    w_out, w_gate_t, w_up_t, w_down = [g_.reshape(-1, D_MODEL) for g_ in gathered]
    post_consts = [row(W['rwkv_lnx_w']), row(W['rwkv_lnx_b']), row(W['rwkv_r_k'])]
    (y_rwkv,) = _tok_fwd("rwkv_post_fwd", _rwkv_post_fn, [y_scan, r, k, v, g], post_consts, [(Wd,)],
                         out_dtypes=[BF16])

    pos = jnp.arange(tn, dtype=F32)
    half = RET_HEAD_DIM // 2
    inv_freq = ROPE_BASE ** (-jnp.arange(half, dtype=F32) / half)
    ang = pos[:, None] * inv_freq[None, :]
    cos2 = jnp.concatenate([jnp.cos(ang), jnp.cos(ang)], axis=1)
    sin2 = jnp.concatenate([-jnp.sin(ang), jnp.sin(ang)], axis=1)
    lg = jnp.log(1.0 - 2.0 ** (-5.0 - jnp.arange(RET_HEADS, dtype=F32)))
    lg = jnp.broadcast_to(lg[:, None, None], (RET_HEADS, 1, 128))
    q_p, k_p, g_ret = _Cols(proj, Wd, 3), _Cols(proj, Wd, 4), _Cols(proj, Wd, 6)
    v_col0 = 5 * Wd // RET_HEAD_DIM
    q_rot, k_rot = _tok_fwd("ret_rotary_fwd", _rotary_fn, [cos2, sin2, q_p, k_p], [], [(RET_WIDTH,)] * 2)
    y_ret_raw = _ret_attn_fwd(lg, q_rot, k_rot, proj, v_col0)
    gn_w = row(W['ret_gn_w'])
    (y_ret,) = _tok_fwd("ret_post_fwd", _ret_post_fn, [y_ret_raw, g_ret], [gn_w], [(RET_WIDTH,)],
                        out_dtypes=[BF16])

    ycat = jnp.concatenate([y_rwkv, y_ret], axis=1)
    x1 = _mm("out_proj_fwd", ycat, w_out, add=x)
    (h2,) = _tok_fwd("norm_ffn_fwd", lambda a_, g_: (_rms_fn(a_, g_),), [x1], [g_ffn], [(D_MODEL,)],
                     out_dtypes=[BF16])
    gate = _mm("ffn_gate_fwd", h2, w_gate_t, tb=True)
    up = _mm("ffn_up_fwd", h2, w_up_t, tb=True)
    cw = W['ffn_conv_w']
    cb = row(W['ffn_conv_b'])
    act = _glu_fwd(gate, up, cw, cb)
    x2 = _mm("ffn_down_fwd", act, w_down, add=x1)
    loss8, dx2, dg_fin = _final_loss(x2, tgt, g_fin)

    G = {'norm_final_g': dg_fin}
    dact = _mm("ffn_down_dx", dx2, w_down, tb=True)
    d_down = _mm("ffn_down_dw", act, dx2, ta=True, out_dtype=BF16)
    dgate, dup, dcw, dcb = _glu_bwd(gate, up, dact, cw, cb)
    G['ffn_conv_w'], G['ffn_conv_b'] = dcw, dcb
    dh2 = _mm("ffn_gate_dx", dgate, w_gate_t)
    dh2 = _mm("ffn_up_dx", dup, w_up_t, add=dh2)
    d_gate_t = _mm("ffn_gate_dw", dgate, h2, ta=True, out_dtype=BF16)
    d_up_t = _mm("ffn_up_dw", dup, h2, ta=True, out_dtype=BF16)
    dx1, G['norm_ffn_g'] = _tok_bwd("norm_ffn_bwd", lambda a_, g_: (_rms_fn(a_, g_),), [], [x1], [g_ffn], [dh2], add=dx2)
    dycat = _mm("out_proj_dx", dx1, w_out, tb=True)
    d_out = _mm("out_proj_dw", ycat, dx1, ta=True, out_dtype=BF16)
    late_grads = [z.reshape(N_DEV, -1, D_MODEL) for z in (d_out, d_gate_t, d_up_t, d_down)]
    dy_rwkv, dy_ret = _Cols(dycat, Wd, 0), _Cols(dycat, Wd, 1)

    dyr_raw, dg_ret, G['ret_gn_w'] = _tok_bwd("ret_post_bwd", _ret_post_fn, [], [y_ret_raw, g_ret], [gn_w], [dy_ret],
                                              tok_dtypes=[F32, BF16])
    dq_rot, dk_rot, dv_ret = _ret_attn_bwd(lg, q_rot, k_rot, proj, dyr_raw, v_col0)
    dq_p, dk_p = _tok_bwd("ret_rotary_bwd", _rotary_fn, [cos2, sin2], [q_p, k_p], [], [dq_rot, dk_rot],
                          tok_dtypes=[BF16, BF16])

    dy_scan, dr1, dk1, dv1, dg, G['rwkv_lnx_w'], G['rwkv_lnx_b'], G['rwkv_r_k'] = _tok_bwd(
        "rwkv_post_bwd", _rwkv_post_fn, [], [y_scan, r, k, v, g], post_consts, [dy_rwkv])
    (dr2, dlw, dk2, dv2, dnkk, db), late_parts = _cscan_bwd(r, lw, k, v, nkk, b, dy_scan, ck, late_grads)
    pre_cts = [(dr1, dr2), (dk1, dk2), (dv1, dv2), dlw, dnkk, db, dg]
    pre_out = _pre_a_bwd(h1, p_rkv, pre_consts, pre_cts)
    dh1_a, dp_rkv = pre_out[0], pre_out[1]
    (G['rwkv_mu_w'], G['rwkv_mu_a'], G['rwkv_mu_g'], G['rwkv_mu_r'], G['rwkv_mu_k'], G['rwkv_mu_v'], G['rwkv_w0'],
     dw1, dw2, G['rwkv_a0'], da1, da2, G['rwkv_g1'], G['rwkv_g2'], G['rwkv_k_k'], G['rwkv_k_a']) = pre_out[2:]
    G['rwkv_w1'], G['rwkv_w2'] = dw1[:, :64], dw2[:64, :]
    G['rwkv_a1'], G['rwkv_a2'] = da1[:, :64], da2[:64, :]

    dproj = jnp.concatenate([dp_rkv, dq_p, dk_p, dv_ret.astype(BF16), dg_ret], axis=1)
    dh1 = _mm("proj_dx", dproj, W['w_in_t'], add=dh1_a)
    G['w_in_t'] = _mm("proj_dw", dproj, h1, ta=True, out_dtype=BF16)
    dx, G['norm_mix_g'] = _tok_bwd("norm_mix_bwd", lambda a_, g_: (_rms_fn(a_, g_),), [], [x], [g_mix], [dh1], add=dx1)
    return loss8[0, 0], dx, G, late_parts


def _adamw(name, parts, w, m, v):
    rows, cols = w.shape
    sub = 8 * 4 // parts.dtype.itemsize
    tb = max(t for t in range(sub, 65, sub) if rows % t == 0) if rows > 64 else rows
    c1 = 1.0 - ADAM_B1 ** ADAM_STEP
    c2 = 1.0 - ADAM_B2 ** ADAM_STEP

    def body(p_ref, w_ref, m_ref, v_ref, g_ref, d_ref, nm_ref, nv_ref):
        g = p_ref[0].astype(F32)
        for d in range(1, N_DEV):
            g = g + p_ref[d].astype(F32)
        mn = ADAM_B1 * m_ref[...] + (1.0 - ADAM_B1) * g
        vn = ADAM_B2 * v_ref[...] + (1.0 - ADAM_B2) * (g * g)
        m_hat = mn / c1
        v_hat = vn / c2
        g_ref[...] = g
        d_ref[...] = -ADAM_LR * (m_hat / (jnp.sqrt(v_hat) + ADAM_EPS) + ADAM_WD * w_ref[...])
        nm_ref[...] = mn
        nv_ref[...] = vn

    spec = pl.BlockSpec((tb, cols), lambda i: (i, 0))
    sh = jax.ShapeDtypeStruct((rows, cols), F32)
    return pl.pallas_call(
        body, name=name, grid=(rows // tb,),
        in_specs=[pl.BlockSpec((N_DEV, tb, cols), lambda i: (0, i, 0)), spec, spec, spec],
        out_specs=[spec] * 4, out_shape=[sh] * 4, compiler_params=_cparams(1))(parts, w, m, v)


def _local_shape(name):
    gs, ax = SHARDED[name]
    ls = list(gs)
    ls[ax] //= N_DEV
    return tuple(ls)


def _seg(flat, seg):
    n = flat.shape[-1]
    pad = _round_up(n, seg) - n
    if pad:
        flat = jnp.pad(flat, [(0, 0)] * (flat.ndim - 1) + [(0, pad)])
    return flat


def _split3(w):
    hi = w.astype(BF16)
    r1 = w - hi.astype(F32)
    mid = r1.astype(BF16)
    lo = (r1 - mid.astype(F32)).astype(BF16)
    return hi, mid, lo


def _pack_small_shards(shards):
    pieces = []
    for name in SMALL_NAMES:
        flat = shards[name].reshape(-1)
        if name == 'ffn_conv_w':
            pieces += [_seg(p, BF16_SEG) for p in _split3(flat)]
        else:
            pieces.append(flat.astype(BF16))
    return jnp.concatenate(pieces).reshape(-1, 128)


def _unpack_small(gathered):
    flat = gathered.reshape(N_DEV, -1)
    out, off = {}, 0
    for name in SMALL_NAMES:
        gs, ax = SHARDED[name]
        ls = _local_shape(name)
        n = int(np.prod(ls))
        if name == 'ffn_conv_w':
            nseg = _round_up(n, BF16_SEG)
            hi, mid, lo = (flat[:, off + j * nseg: off + j * nseg + n].astype(F32) for j in range(3))
            sh = ((hi + mid) + lo).reshape(N_DEV, 3, -1)
            out[name] = jnp.swapaxes(sh, 0, 1).reshape(3, D_FF)
            off += 3 * nseg
        else:
            sh = flat[:, off:off + n].reshape((N_DEV,) + ls[1:])
            out[name] = sh.reshape(gs[1:]) if ax == 1 else jnp.swapaxes(sh, 0, 1).reshape(gs[1:])
            off += n
    return out


def _small_pieces(sharded, repl):
    return [sharded[n].reshape(-1) for n in SMALL_NAMES] + [repl[n].reshape(-1) for n in REPL_NAMES]


def _pack_small_local(d):
    flat = jnp.concatenate(_small_pieces(d, d))
    return _seg(flat, F32_SEG).reshape(-1, 128)


def _pack_small_grads(G):
    pieces = []
    for name in SMALL_NAMES:
        gs, ax = SHARDED[name]
        g = G[name]
        if name == 'ffn_conv_w':
            sh = jnp.swapaxes(g.reshape(3, N_DEV, -1), 0, 1)
        elif ax == 1:
            sh = g
        else:
            sh = jnp.swapaxes(g.reshape(g.shape[0], N_DEV, -1), 0, 1)
        pieces.append(sh.reshape(N_DEV, -1))
    rep = jnp.concatenate([G[n].reshape(-1) for n in REPL_NAMES])
    pieces.append(jnp.broadcast_to(rep[None, :], (N_DEV, rep.shape[0])))
    flat = _seg(jnp.concatenate(pieces, axis=1), F32_SEG)
    return flat.reshape(N_DEV, -1, 128)


def _unpack_small_local(packed, local_shapes):
    flat = packed.reshape(-1)
    out, off = {}, 0
    for name in SMALL_NAMES + REPL_NAMES:
        n = int(np.prod(local_shapes[name]))
        out[name] = flat[off:off + n].reshape(local_shapes[name])
        off += n
    return out


def kernel(x, *rest):
    nw = len(WEIGHT_NAMES)
    assert len(rest) == 3 * nw + 1
    weights = dict(zip(WEIGHT_NAMES, rest[:nw]))
    loss_target = rest[nw]
    moms = dict(zip(WEIGHT_NAMES, rest[nw + 1:2 * nw + 1]))
    vars_ = dict(zip(WEIGHT_NAMES, rest[2 * nw + 1:]))
    local_shapes = {n: weights[n].shape for n in WEIGHT_NAMES}

    def native2d(name, a):
        a2 = a.reshape(a.shape[-2], a.shape[-1])
        return a2.T if name in BIG_T else a2

    def from2d(name, a2):
        return (a2.T if name in BIG_T else a2).reshape(local_shapes[name])

    big_w = {n: native2d(n, weights[n]) for n in BIG_NAMES}
    w_in_t_sh = big_w['w_in'].astype(BF16)
    late = [big_w[n].astype(BF16) for n in LATE_NAMES]
    small_sh = _pack_small_shards({n: weights[n] for n in SMALL_NAMES})
    w_in_g, small_g = _exchange("weights_all_gather", [w_in_t_sh, small_sh], False)
    W = _unpack_small(small_g)
    W['w_in_t'] = w_in_g.reshape(-1, D_MODEL)
    for n in REPL_NAMES:
        W[n] = weights[n][0] if n != 'norm_final_g' else weights[n]

    loss, dx, G, late_parts = _local_step(x[0], loss_target[0], W, late)

    w_in_parts, small_parts = _exchange(
        "grads_all_to_all", [G['w_in_t'].reshape(N_DEV, -1, D_MODEL), _pack_small_grads(G)], True)
    results = {}
    for n, parts in zip(['w_in'] + LATE_NAMES, [w_in_parts] + list(late_parts)):
        res = _adamw("adamw_" + n, parts, big_w[n], native2d(n, moms[n]), native2d(n, vars_[n]))
        results[n] = [from2d(n, r) for r in res]
    small_res = _adamw("adamw_small", small_parts, _pack_small_local(weights), _pack_small_local(moms),
                       _pack_small_local(vars_))
    small_out = [_unpack_small_local(p, local_shapes) for p in small_res]

    loss = lax.psum(loss, ("x", "y", "c"))
    outs = [loss, dx[None]]
    for j in range(4):
        outs += [results[n][j] if n in results else small_out[j][n] for n in WEIGHT_NAMES]
    return tuple(outs)
```

```python
import functools
import math

import numpy as np
import jax
import jax.numpy as jnp
from jax import lax
from jax.experimental import pallas as pl
from jax.experimental.pallas import tpu as pltpu

F32 = jnp.float32
BF16 = jnp.bfloat16

N_DEV = 8
D_MODEL = 1024
RWKV_HEADS = 8
RWKV_HEAD_DIM = 64
RWKV_WIDTH = 512
RET_HEADS = 4
RET_HEAD_DIM = 128
RET_WIDTH = 512
LORA_PAD = 128
D_FF = 2816
NORM_EPS = 1e-6
RWKV_GN_EPS = 64e-5
RET_GN_EPS = 1e-5
ROPE_BASE = 10000.0
ADAM_LR, ADAM_B1, ADAM_B2, ADAM_EPS, ADAM_WD, ADAM_STEP = 0.001, 0.9, 0.999, 1e-08, 0.01, 10

VMEM_LIMIT = 56 * 1024 * 1024
TOK_BLOCK = 256
SCAN_CHUNK = 64
ATT_BLOCK = 512
BF16_SEG = 2048
F32_SEG = 1024

WEIGHT_NAMES = ['norm_mix_g', 'w_in', 'rwkv_mu_r', 'rwkv_mu_k', 'rwkv_mu_v', 'rwkv_mu_w', 'rwkv_mu_a',
                'rwkv_mu_g', 'rwkv_w0', 'rwkv_w1', 'rwkv_w2', 'rwkv_a0', 'rwkv_a1', 'rwkv_a2', 'rwkv_g1',
                'rwkv_g2', 'rwkv_k_k', 'rwkv_k_a', 'rwkv_r_k', 'rwkv_lnx_w', 'rwkv_lnx_b', 'ret_gn_w',
                'w_out', 'norm_ffn_g', 'ffn_w_gate', 'ffn_w_up', 'ffn_conv_w', 'ffn_conv_b', 'ffn_w_down',
                'norm_final_g']
SHARDED = {
    'w_in': ((1, 1024, 3584), 2), 'rwkv_w1': ((1, 1024, 64), 1), 'rwkv_w2': ((1, 64, 512), 2),
    'rwkv_a1': ((1, 1024, 64), 1), 'rwkv_a2': ((1, 64, 512), 2), 'rwkv_g1': ((1, 1024, 128), 1),
    'rwkv_g2': ((1, 128, 512), 2), 'w_out': ((1, 1024, 1024), 1), 'ffn_w_gate': ((1, 1024, 2816), 2),
    'ffn_w_up': ((1, 1024, 2816), 2), 'ffn_conv_w': ((1, 3, 1, 2816), 3), 'ffn_w_down': ((1, 2816, 1024), 1),
}
REPL_NAMES = [n for n in WEIGHT_NAMES if n not in SHARDED]
BIG_NAMES = ['w_in', 'w_out', 'ffn_w_gate', 'ffn_w_up', 'ffn_w_down']
BIG_T = ('w_in', 'ffn_w_gate', 'ffn_w_up')
LATE_NAMES = ['w_out', 'ffn_w_gate', 'ffn_w_up', 'ffn_w_down']
SMALL_NAMES = [n for n in WEIGHT_NAMES if n in SHARDED and n not in BIG_NAMES]


def _cparams(n_grid):
    return pltpu.CompilerParams(dimension_semantics=("arbitrary",) * n_grid, vmem_limit_bytes=VMEM_LIMIT)


def _round_up(n, m):
    return (n + m - 1) // m * m


@jax.custom_vjp
def _bdot(x, w):
    return jnp.dot(x.astype(BF16), w.astype(BF16), preferred_element_type=F32)


def _bdot_fwd(x, w):
    return _bdot(x, w), (x, w)


def _bdot_bwd(res, g):
    x, w = res
    gb = g.astype(BF16)
    dx = lax.dot_general(gb, w.astype(BF16), (((1,), (1,)), ((), ())), preferred_element_type=F32)
    dw = lax.dot_general(x.astype(BF16), gb, (((0,), (0,)), ((), ())), preferred_element_type=F32)
    return dx, dw.astype(w.dtype)


_bdot.defvjp(_bdot_fwd, _bdot_bwd)


@jax.custom_vjp
def _shift_rows(x, prev):
    rolled = pltpu.roll(x, 1, 0)
    row = lax.broadcasted_iota(jnp.int32, x.shape, 0)
    return jnp.where(row == 0, jnp.broadcast_to(prev, x.shape), rolled)


def _shift_rows_fwd(x, prev):
    return _shift_rows(x, prev), None


def _shift_rows_bwd(_, g):
    n = g.shape[0]
    rolled = pltpu.roll(g, n - 1, 0)
    row = lax.broadcasted_iota(jnp.int32, g.shape, 0)
    return jnp.where(row == n - 1, 0.0, rolled), g[0:1, :]


_shift_rows.defvjp(_shift_rows_fwd, _shift_rows_bwd)


@jax.custom_vjp
def _swap_halves(x):
    return pltpu.roll(x, 64, 1)


_swap_halves.defvjp(lambda x: (_swap_halves(x), None), lambda _, g: (pltpu.roll(g, 64, 1),))


def _sigmoid(x):
    return 1.0 / (1.0 + jnp.exp(-x))


def _softplus(x):
    return jnp.maximum(x, 0.0) + jnp.log(1.0 + jnp.exp(-jnp.abs(x)))


def _rms_fn(x, g):
    return x * lax.rsqrt(jnp.mean(x * x, axis=-1, keepdims=True) + NORM_EPS) * g


def _pre_a_fn(h1, h1p, p, pp, mu_w, mu_a, mu_g, mu_r, mu_k, mu_v, w0, w1, w2, a0, a1, a2, g1, g2, k_k, k_a):
    W = RWKV_WIDTH
    h1s = _shift_rows(h1, h1p)
    ps = _shift_rows(p, pp)
    dx = h1s - h1
    xw = h1 + dx * mu_w
    xa = h1 + dx * mu_a
    xg = h1 + dx * mu_g
    dp = ps - p
    r = p[:, 0:W] + dp[:, 0:W] * mu_r
    k0 = p[:, W:2 * W] + dp[:, W:2 * W] * mu_k
    v = p[:, 2 * W:3 * W] + dp[:, 2 * W:3 * W] * mu_v
    wl = w0 + _bdot(jnp.tanh(_bdot(xw, w1)), w2)
    w_log = -_softplus(-wl) - 0.5
    lw = -jnp.exp(w_log)
    a = _sigmoid(a0 + _bdot(_bdot(xa, a1), a2))
    g = _bdot(_sigmoid(_bdot(xg, g1)), g2)
    nkk, k, b = _pre_b_fn(k0, a, k_k, k_a)
    return r, k, v, lw, nkk, b, g


def _head_sum_raw(x):
    n = x.shape[1]
    ii = lax.broadcasted_iota(jnp.int32, (n, n), 0) // RWKV_HEAD_DIM
    jj = lax.broadcasted_iota(jnp.int32, (n, n), 1) // RWKV_HEAD_DIM
    ones = (ii == jj).astype(BF16)
    xh = x.astype(BF16)
    xl = (x - xh.astype(F32)).astype(BF16)
    return jnp.dot(xh, ones, preferred_element_type=F32) + jnp.dot(xl, ones, preferred_element_type=F32)


@jax.custom_vjp
def _head_sum(x):
    return _head_sum_raw(x)


_head_sum.defvjp(lambda x: (_head_sum_raw(x), None), lambda _, g: (_head_sum_raw(g),))


def _pre_b_fn(k0, a, k_k, k_a):
    kkr = k0 * k_k
    nrm = jnp.sqrt(_head_sum(kkr * kkr))
    kk = kkr / jnp.maximum(nrm, 1e-12)
    k = k0 * (1.0 + (a - 1.0) * k_a)
    return -kk, k, kk * a


def _rwkv_post_fn(y, r, k, v, g, lnx_w, lnx_b, r_k):
    inv = 1.0 / RWKV_HEAD_DIM
    mu = _head_sum(y) * inv
    yc = y - mu
    var = _head_sum(yc * yc) * inv
    yn = yc * lax.rsqrt(var + RWKV_GN_EPS) * lnx_w + lnx_b
    bonus = _head_sum(r * k * r_k) * v
    return ((yn + bonus) * g,)


def _rotary_fn(cos2, sin2, q, k):
    qs, ks = [], []
    for h in range(RET_HEADS):
        sl = slice(h * RET_HEAD_DIM, (h + 1) * RET_HEAD_DIM)
        qh, kh = q[:, sl], k[:, sl]
        qs.append(qh * cos2 + _swap_halves(qh) * sin2)
        ks.append((kh * cos2 + _swap_halves(kh) * sin2) * (RET_HEAD_DIM ** -0.5))
    return jnp.concatenate(qs, axis=1), jnp.concatenate(ks, axis=1)


def _ret_post_fn(y, gp, gn_w):
    outs = []
    for h in range(RET_HEADS):
        sl = slice(h * RET_HEAD_DIM, (h + 1) * RET_HEAD_DIM)
        yh = y[:, sl]
        mu = jnp.mean(yh, axis=-1, keepdims=True)
        yc = yh - mu
        var = jnp.mean(yc * yc, axis=-1, keepdims=True)
        outs.append(yc * lax.rsqrt(var + RET_GN_EPS) * gn_w[:, sl])
    yn = jnp.concatenate(outs, axis=1)
    return (gp * _sigmoid(gp) * yn,)


class _Cols:
    def __init__(self, array, width, block):
        self.array, self.width, self.block = array, width, block
        self.shape, self.ndim, self.dtype = (array.shape[0], width), 2, array.dtype


def _arr(a):
    return a.array if isinstance(a, _Cols) else a


def _blk_spec(a, tb, rev_nb=None):
    nd = a.ndim
    tail = (a.block,) if isinstance(a, _Cols) else (0,) * (nd - 1)
    if rev_nb is None:
        return pl.BlockSpec((tb,) + a.shape[1:], lambda i: (i,) + tail)
    return pl.BlockSpec((tb,) + a.shape[1:], lambda i: (rev_nb - 1 - i,) + tail)


def _full_spec(a):
    nd = a.ndim
    return pl.BlockSpec(a.shape, lambda i: (0,) * nd)


def _tok_fwd(name, fn, toks, consts, out_tails, tb=TOK_BLOCK, out_dtypes=None):
    out_dtypes = out_dtypes or [F32] * len(out_tails)
    n_in = len(toks) + len(consts)
    tn = toks[0].shape[0]

    def body(*refs):
        outs = fn(*[r[...] for r in refs[:n_in]])
        for r, o in zip(refs[n_in:], outs):
            r[...] = o.astype(r.dtype)

    out_shape = [jax.ShapeDtypeStruct((tn,) + tuple(s), dt) for s, dt in zip(out_tails, out_dtypes)]
    return pl.pallas_call(
        body, name=name, grid=(tn // tb,),
        in_specs=[_blk_spec(a, tb) for a in toks] + [_full_spec(c) for c in consts],
        out_specs=[_blk_spec(o, tb) for o in out_shape], out_shape=out_shape,
        compiler_params=_cparams(1))(*[_arr(a) for a in toks], *consts)


def _tok_bwd(name, fn, aux, toks, consts, cts, add=None, tb=TOK_BLOCK, tok_dtypes=None):
    n_aux, n_tok, n_c = len(aux), len(toks), len(consts)
    ct_groups = [c if isinstance(c, (tuple, list)) else (c,) for c in cts]
    ct_flat = [a for grp in ct_groups for a in grp]
    n_ct = len(ct_flat)
    n_add = 0 if add is None else 1
    tn = toks[0].shape[0]

    def body(*refs):
        pos = 0
        aux_v = [r[...] for r in refs[pos:pos + n_aux]]; pos += n_aux
        tok_v = [r[...] for r in refs[pos:pos + n_tok]]; pos += n_tok
        const_v = [r[...] for r in refs[pos:pos + n_c]]; pos += n_c
        ct_refs = refs[pos:pos + n_ct]; pos += n_ct
        add_refs = refs[pos:pos + n_add]; pos += n_add
        dtok_refs = refs[pos:pos + n_tok]; pos += n_tok
        dconst_refs = refs[pos:pos + n_c]
        ct_v, q = [], 0
        for grp in ct_groups:
            s = ct_refs[q][...]
            for r in ct_refs[q + 1:q + len(grp)]:
                s = s + r[...]
            q += len(grp)
            ct_v.append(s)
        _, vjp = jax.vjp(lambda *tc: fn(*aux_v, *tc), *tok_v, *const_v)
        grads = vjp(tuple(ct_v))
        for j, r in enumerate(dtok_refs):
            gj = grads[j]
            if j == 0 and n_add:
                gj = gj + add_refs[0][...]
            r[...] = gj.astype(r.dtype)

        @pl.when(pl.program_id(0) == 0)
        def _():
            for r in dconst_refs:
                r[...] = jnp.zeros(r.shape, F32)

        for j, r in enumerate(dconst_refs):
            r[...] += grads[n_tok + j]

    ins = list(aux) + list(toks) + list(consts) + ct_flat + ([add] if n_add else [])
    in_specs = ([_blk_spec(a, tb) for a in aux] + [_blk_spec(a, tb) for a in toks] + [_full_spec(c) for c in consts]
                + [_blk_spec(a, tb) for a in ct_flat] + ([_blk_spec(add, tb)] if n_add else []))
    tok_dtypes = tok_dtypes or [F32] * n_tok
    out_shape = ([jax.ShapeDtypeStruct(a.shape, dt) for a, dt in zip(toks, tok_dtypes)]
                 + [jax.ShapeDtypeStruct(c.shape, F32) for c in consts])
    out_specs = [_blk_spec(o, tb) for o in out_shape[:n_tok]] + [_full_spec(c) for c in consts]
    return pl.pallas_call(body, name=name, grid=(tn // tb,), in_specs=in_specs, out_specs=out_specs,
                          out_shape=out_shape, compiler_params=_cparams(1))(*[_arr(a) for a in ins])


MM_VMEM_BUDGET = 40 * 1024 * 1024
MM_STEP_SECONDS = 0.4e-6
MM_HBM_BYTES_PER_SECOND = 2.5e12
MM_XPOSE_SECONDS_PER_ELEM = 2e-12
MM_MXU_COLUMNS = 256
MM_MXU_FLOPS = 9e14


def _mm_tiles(m, n, kd, a_bytes, b_bytes, o_bytes, has_add, ta):
    divs = lambda d: [t for t in range(128, d + 1, 128) if d % t == 0]
    best = None
    for tm in divs(m):
        for tn in divs(n):
            for tk in divs(kd):
                ni, nj, nk = m // tm, n // tn, kd // tk
                vmem = (2 * tm * tk * a_bytes + 2 * tk * tn * b_bytes + tm * tn * 4 + 2 * tm * tn * o_bytes
                        + (2 * tm * tn * 4 if has_add else 0) + 2 * (tm * tk + tk * tn) + tm * tn * 4)
                if vmem > MM_VMEM_BUDGET:
                    continue
                a_traffic = m * kd * a_bytes * (nj if nk > 1 else 1)
                b_traffic = kd * n * b_bytes * (ni if nj * nk > 1 else 1)
                cost = ni * nj * nk * MM_STEP_SECONDS + (a_traffic + b_traffic) / MM_HBM_BYTES_PER_SECOND
                cost += 2.0 * m * kd * nj * max(tn, MM_MXU_COLUMNS) / MM_MXU_FLOPS
                if ta:
                    cost += m * kd * nj * MM_XPOSE_SECONDS_PER_ELEM
                if best is None or cost < best[0]:
                    best = (cost, tm, tn, tk)
    return best[1:]


def _mm(name, a, b, ta=False, tb=False, add=None, out_dtype=F32):
    if ta:
        kd, m = a.shape
    else:
        m, kd = a.shape
    if tb:
        n, kb = b.shape
    else:
        kb, n = b.shape
    assert kd == kb, (a.shape, b.shape)
    tm, tn, tk = _mm_tiles(m, n, kd, a.dtype.itemsize, b.dtype.itemsize, jnp.dtype(out_dtype).itemsize,
                           add is not None, ta)
    nk = kd // tk
    has_add = add is not None
    dims = (((0 if ta else 1,), (1 if tb else 0,)), ((), ()))

    def body(*refs):
        a_ref, b_ref = refs[0], refs[1]
        o_ref, acc_ref = refs[-2], refs[-1]
        k = pl.program_id(2)

        @pl.when(k == 0)
        def _():
            acc_ref[...] = refs[2][...] if has_add else jnp.zeros(acc_ref.shape, F32)

        acc_ref[...] += lax.dot_general(a_ref[...].astype(BF16), b_ref[...].astype(BF16), dims,
                                        preferred_element_type=F32)

        @pl.when(k == nk - 1)
        def _():
            o_ref[...] = acc_ref[...].astype(out_dtype)

    a_spec = pl.BlockSpec((tk, tm), lambda i, j, k: (k, i)) if ta else pl.BlockSpec((tm, tk), lambda i, j, k: (i, k))
    b_spec = pl.BlockSpec((tn, tk), lambda i, j, k: (j, k)) if tb else pl.BlockSpec((tk, tn), lambda i, j, k: (k, j))
    o_spec = pl.BlockSpec((tm, tn), lambda i, j, k: (i, j))
    ins = [a, b] + ([add] if has_add else [])
    in_specs = [a_spec, b_spec] + ([o_spec] if has_add else [])
    return pl.pallas_call(body, name=name, grid=(m // tm, n // tn, nk), in_specs=in_specs, out_specs=o_spec,
                          out_shape=jax.ShapeDtypeStruct((m, n), out_dtype),
                          scratch_shapes=[pltpu.VMEM((tm, tn), F32)], compiler_params=_cparams(3))(*ins)


def _prev8_spec(a, tb, rev_nb=None):
    r = tb // 8
    if rev_nb is None:
        return pl.BlockSpec((8, a.shape[1]), lambda i: (jnp.maximum(i * r - 1, 0), 0))
    return pl.BlockSpec((8, a.shape[1]), lambda i: (jnp.maximum((rev_nb - 1 - i) * r - 1, 0), 0))


def _pre_a_fwd(h1, p, consts, tb=TOK_BLOCK):
    tn = h1.shape[0]

    def body(h1_ref, h1h_ref, p_ref, ph_ref, *rest):
        c_refs, o_refs = rest[:len(consts)], rest[len(consts):]
        first = pl.program_id(0) == 0
        h1p = jnp.where(first, 0.0, h1h_ref[7:8, :])
        pp = jnp.where(first, 0.0, ph_ref[7:8, :])
        outs = _pre_a_fn(h1_ref[...], h1p, p_ref[...], pp, *[c[...] for c in c_refs])
        for r, o in zip(o_refs, outs):
            r[...] = o

    out_shape = [jax.ShapeDtypeStruct((tn, RWKV_WIDTH), F32) for _ in range(7)]
    return pl.pallas_call(
        body, name="rwkv_pre_a_fwd", grid=(tn // tb,),
        in_specs=[_blk_spec(h1, tb), _prev8_spec(h1, tb), _blk_spec(p, tb), _prev8_spec(p, tb)]
        + [_full_spec(c) for c in consts],
        out_specs=[_blk_spec(o, tb) for o in out_shape], out_shape=out_shape,
        compiler_params=_cparams(1))(h1, h1, _arr(p), _arr(p), *consts)


def _pre_a_bwd(h1, p, consts, cts, tb=TOK_BLOCK):
    tn = h1.shape[0]
    nb = tn // tb
    n_c = len(consts)
    ct_groups = [c if isinstance(c, (tuple, list)) else (c,) for c in cts]
    ct_flat = [a for grp in ct_groups for a in grp]
    n_ct = len(ct_flat)

    def body(*refs):
        h1_ref, h1h_ref, p_ref, ph_ref = refs[:4]
        c_refs = refs[4:4 + n_c]
        ct_refs = refs[4 + n_c:4 + n_c + n_ct]
        dh1_ref, dp_ref = refs[4 + n_c + n_ct:6 + n_c + n_ct]
        dc_refs = refs[6 + n_c + n_ct:6 + 2 * n_c + n_ct]
        ch_ref, cp_ref = refs[-2], refs[-1]
        i = pl.program_id(0)
        first_block = i == nb - 1
        h1p = jnp.where(first_block, 0.0, h1h_ref[7:8, :])
        pp = jnp.where(first_block, 0.0, ph_ref[7:8, :])
        ct_v, q = [], 0
        for grp in ct_groups:
            s = ct_refs[q][...]
            for r in ct_refs[q + 1:q + len(grp)]:
                s = s + r[...]
            q += len(grp)
            ct_v.append(s)
        _, vjp = jax.vjp(_pre_a_fn, h1_ref[...], h1p, p_ref[...], pp, *[c[...] for c in c_refs])
        grads = vjp(tuple(ct_v))

        @pl.when(i == 0)
        def _():
            ch_ref[...] = jnp.zeros(ch_ref.shape, F32)
            cp_ref[...] = jnp.zeros(cp_ref.shape, F32)
            for r in dc_refs:
                r[...] = jnp.zeros(r.shape, F32)

        rowh = lax.broadcasted_iota(jnp.int32, (tb, h1.shape[1]), 0)
        rowp = lax.broadcasted_iota(jnp.int32, (tb, p.shape[1]), 0)
        dh1_ref[...] = grads[0] + jnp.where(rowh == tb - 1, jnp.broadcast_to(ch_ref[0:1, :], rowh.shape), 0.0)
        dp_ref[...] = (grads[2] + jnp.where(rowp == tb - 1, jnp.broadcast_to(cp_ref[0:1, :], rowp.shape), 0.0)
                       ).astype(dp_ref.dtype)
        ch_ref[0:1, :] = grads[1]
        cp_ref[0:1, :] = grads[3]
        for j, r in enumerate(dc_refs):
            r[...] += grads[4 + j]

    ins = [h1, h1, _arr(p), _arr(p)] + list(consts) + ct_flat
    in_specs = ([_blk_spec(h1, tb, nb), _prev8_spec(h1, tb, nb), _blk_spec(p, tb, nb), _prev8_spec(p, tb, nb)]
                + [_full_spec(c) for c in consts] + [_blk_spec(a, tb, nb) for a in ct_flat])
    out_shape = ([jax.ShapeDtypeStruct(h1.shape, F32), jax.ShapeDtypeStruct(p.shape, BF16)]
                 + [jax.ShapeDtypeStruct(c.shape, F32) for c in consts])
    out_specs = [_blk_spec(h1, tb, nb), _blk_spec(p, tb, nb)] + [_full_spec(c) for c in consts]
    return pl.pallas_call(body, name="rwkv_pre_a_bwd", grid=(nb,), in_specs=in_specs, out_specs=out_specs,
                          out_shape=out_shape,
                          scratch_shapes=[pltpu.VMEM((8, h1.shape[1]), F32), pltpu.VMEM((8, p.shape[1]), F32)],
                          compiler_params=_cparams(1))(*ins)


def _my_index():
    return 4 * lax.axis_index("x") + 2 * lax.axis_index("y") + lax.axis_index("c")


def _peer(k):
    x, y, c = lax.axis_index("x"), lax.axis_index("y"), lax.axis_index("c")
    px = 1 - x if k & 4 else x
    py = 1 - y if k & 2 else y
    pc = 1 - c if k & 1 else c
    return (px, py, pc), 4 * px + 2 * py + pc


def _xchg_sems(n):
    return [pltpu.SemaphoreType.DMA((n * (N_DEV - 1),)), pltpu.SemaphoreType.DMA((n * (N_DEV - 1),)),
            pltpu.SemaphoreType.DMA((n,))]


def _scatter_copies(srcs, dsts, sems, incoming=False):
    send_sems, recv_sems, local_sems = sems
    me = _my_index()
    local, remote = [], []
    for i, (s, d) in enumerate(zip(srcs, dsts)):
        if not incoming:
            local.append(pltpu.make_async_copy(s.at[me], d.at[me], local_sems.at[i]))
        for k in range(1, N_DEV):
            peer, plin = _peer(k)
            j = i * (N_DEV - 1) + k - 1
            s_slot, d_slot = (me, plin) if incoming else (plin, me)
            remote.append(pltpu.make_async_remote_copy(
                src_ref=s.at[s_slot], dst_ref=d.at[d_slot], send_sem=send_sems.at[j],
                recv_sem=recv_sems.at[j], device_id=peer, device_id_type=pl.DeviceIdType.MESH))
    return local, remote


def _scatter_start(srcs, dsts, sems):
    local, out = _scatter_copies(srcs, dsts, sems)
    for cp in local + out:
        cp.start()


def _scatter_wait(srcs, dsts, sems):
    for cp in _scatter_copies(srcs, dsts, sems, incoming=True)[1]:
        cp.wait_recv()
    local, out = _scatter_copies(srcs, dsts, sems)
    for cp in out:
        cp.wait_send()
    for cp in local:
        cp.wait()


_ICI_PEERS = (2, 4, 6)


def _gather_copies(srcs, dsts, sems, group):
    send_sems, recv_sems, local_sems = sems
    me = _my_index()
    sib, sib_lin = _peer(1)
    out = []
    for i, (s, d) in enumerate(zip(srcs, dsts)):
        def mk(q, src, dst, dev):
            j = i * (N_DEV - 1) + q
            return pltpu.make_async_remote_copy(src_ref=src, dst_ref=dst, send_sem=send_sems.at[j],
                                                recv_sem=recv_sems.at[j], device_id=dev,
                                                device_id_type=pl.DeviceIdType.MESH)
        if group == 'local':
            out.append(pltpu.make_async_copy(s, d.at[me], local_sems.at[i]))
        elif group == 'own':
            out.append(mk(0, s, d.at[me], sib))
        elif group == 'in_d2d':
            out.append(mk(0, s, d.at[sib_lin], sib))
        for jj, k in enumerate(_ICI_PEERS):
            peer, plin = _peer(k)
            plin_other = _peer(k + 1)[1]
            if group == 'own':
                out.append(mk(1 + jj, s, d.at[me], peer))
            elif group == 'in_ici':
                out.append(mk(1 + jj, s, d.at[plin], peer))
            elif group == 'pass_on':
                out.append(mk(4 + jj, d.at[plin], d.at[plin], sib))
            elif group == 'in_d2d':
                out.append(mk(4 + jj, d.at[plin_other], d.at[plin_other], sib))
    return out


def _gather_start(srcs, dsts, sems):
    for cp in _gather_copies(srcs, dsts, sems, 'local') + _gather_copies(srcs, dsts, sems, 'own'):
        cp.start()


def _gather_pass_on(srcs, dsts, sems):
    for cp in _gather_copies(srcs, dsts, sems, 'in_ici'):
        cp.wait_recv()
    for cp in _gather_copies(srcs, dsts, sems, 'pass_on'):
        cp.start()


def _gather_finish(srcs, dsts, sems):
    for cp in _gather_copies(srcs, dsts, sems, 'in_d2d'):
        cp.wait_recv()
    for cp in _gather_copies(srcs, dsts, sems, 'own') + _gather_copies(srcs, dsts, sems, 'pass_on'):
        cp.wait_send()
    for cp in _gather_copies(srcs, dsts, sems, 'local'):
        cp.wait()


def _xchg_out_shapes(srcs, scatter):
    return [jax.ShapeDtypeStruct(s.shape if scatter else (N_DEV,) + s.shape, s.dtype) for s in srcs]


_ANY = pl.BlockSpec(memory_space=pl.ANY)


def _exchange(name, srcs, scatter):
    n = len(srcs)

    def body(*refs):
        s, d, sems = refs[:n], refs[n:2 * n], refs[2 * n:]
        if scatter:
            _scatter_start(s, d, sems)
            _scatter_wait(s, d, sems)
        else:
            _gather_start(s, d, sems)
            _gather_pass_on(s, d, sems)
            _gather_finish(s, d, sems)

    return pl.pallas_call(body, name=name, in_specs=[_ANY] * n, out_specs=[_ANY] * n,
                          out_shape=_xchg_out_shapes(srcs, scatter), scratch_shapes=_xchg_sems(n))(*srcs)


_MM_DIMS = {'nn': (((1,), (0,)), ((), ())), 'nt': (((1,), (1,)), ((), ())), 'tn': (((0,), (0,)), ((), ()))}


def _cmm_raw(x, y, kind, split):
    dot = functools.partial(lax.dot_general, dimension_numbers=_MM_DIMS[kind], preferred_element_type=F32)
    xh, yh = x.astype(BF16), y.astype(BF16)
    out = dot(xh, yh)
    if split:
        xl = (x - xh.astype(F32)).astype(BF16)
        yl = (y - yh.astype(F32)).astype(BF16)
        out = out + (dot(xh, yl) + dot(xl, yh))
    return out


@functools.partial(jax.custom_vjp, nondiff_argnums=(2, 3))
def _cmm(x, y, kind, split=False):
    return _cmm_raw(x, y, kind, split)


def _cmm_fwd(x, y, kind, split):
    return _cmm_raw(x, y, kind, split), (x, y)


def _cmm_bwd(kind, split, res, g):
    x, y = res
    if kind == 'nn':
        return _cmm_raw(g, y, 'nt', split), _cmm_raw(x, g, 'tn', split)
    if kind == 'nt':
        return _cmm_raw(g, y, 'nn', split), _cmm_raw(g, x, 'tn', split)
    return _cmm_raw(y, g, 'nt', split), _cmm_raw(x, g, 'nn', split)


_cmm.defvjp(_cmm_fwd, _cmm_bwd)


def _tri_sum_raw(tri, x, kind):
    dot = functools.partial(lax.dot_general, dimension_numbers=_MM_DIMS[kind], preferred_element_type=F32)
    tb = tri.astype(BF16)
    hi, mid, lo = _split3(x)
    return (dot(tb, hi) + dot(tb, mid)) + dot(tb, lo)


@functools.partial(jax.custom_vjp, nondiff_argnums=(2,))
def _tri_sum(tri, x, kind):
    return _tri_sum_raw(tri, x, kind)


def _tri_sum_fwd(tri, x, kind):
    return _tri_sum_raw(tri, x, kind), tri


def _tri_sum_bwd(kind, tri, g):
    return jnp.zeros_like(tri), _tri_sum_raw(tri, g, 'tn' if kind == 'nn' else 'nn')


_tri_sum.defvjp(_tri_sum_fwd, _tri_sum_bwd)


def _chunk_fn(S0, r, lw, k, v, a, b):
    hs = range(len(r))
    C = r[0].shape[0]
    ii = lax.broadcasted_iota(jnp.int32, (C, C), 0)
    jj = lax.broadcasted_iota(jnp.int32, (C, C), 1)
    incl, strict = ii >= jj, ii > jj
    eye = (ii == jj).astype(F32)
    inclf = incl.astype(F32)
    cum = [_tri_sum(inclf, lw[h], 'nn') for h in hs]
    e_inv = [jnp.exp(-cum[h]) for h in hs]
    At = [a[h] * jnp.exp(cum[h] - lw[h]) for h in hs]
    Rt = [r[h] * jnp.exp(cum[h]) for h in hs]
    Kh = [k[h] * e_inv[h] for h in hs]
    Bh = [b[h] * e_inv[h] for h in hs]
    Mab = [jnp.where(strict, _cmm(At[h], Bh[h], 'nt', True), 0.0) for h in hs]
    Mak = [jnp.where(strict, _cmm(At[h], Kh[h], 'nt', True), 0.0) for h in hs]
    Mrk = [jnp.where(incl, _cmm(Rt[h], Kh[h], 'nt', True), 0.0) for h in hs]
    Mrb = [jnp.where(incl, _cmm(Rt[h], Bh[h], 'nt', True), 0.0) for h in hs]
    rhs = [_cmm(At[h], S0[h], 'nt') + _cmm(Mak[h], v[h], 'nn') for h in hs]
    P = Mab
    Tm = [eye + P[h] for h in hs]
    n = 1
    while 2 * n < C:
        P = [_cmm(P[h], P[h], 'nn', True) for h in hs]
        Tm = [_cmm(Tm[h], eye + P[h], 'nn', True) for h in hs]
        n *= 2
    U = [_cmm(Tm[h], rhs[h], 'nn', True) for h in hs]
    Y = [_cmm(Rt[h], S0[h], 'nt') + _cmm(Mrk[h], v[h], 'nn') + _cmm(Mrb[h], U[h], 'nn') for h in hs]
    gC = [jnp.exp(jnp.sum(lw[h], axis=0, keepdims=True)) for h in hs]
    SC = [S0[h] * gC[h] + _cmm(v[h], Kh[h] * gC[h], 'tn') + _cmm(U[h], Bh[h] * gC[h], 'tn') for h in hs]
    return tuple(Y), tuple(SC)


def _cscan_fwd(r, lw, k, v, a, b, xs):
    n_x = len(xs)
    tn = r.shape[0]
    H, Dh, Dv = RWKV_HEADS, RWKV_HEAD_DIM, RWKV_HEAD_DIM
    nc = tn // SCAN_CHUNK
    lanes = lambda h: slice(h * Dh, (h + 1) * Dh)
    heads = lambda ref: tuple(ref[:, lanes(h)] for h in range(H))
    mats = lambda ref: tuple(ref[h] for h in range(H))

    def body(r_ref, lw_ref, k_ref, v_ref, a_ref, b_ref, *rest):
        x_src, (y_ref, ck_ref) = rest[:n_x], rest[n_x:n_x + 2]
        x_dst, s_ref, sems = rest[n_x + 2:2 * n_x + 2], rest[2 * n_x + 2], rest[2 * n_x + 3:]

        @pl.when(pl.program_id(0) == 0)
        def _():
            s_ref[...] = jnp.zeros(s_ref.shape, F32)
            _gather_start(x_src, x_dst, sems)

        ck_ref[0] = s_ref[...]
        y, sc = _chunk_fn(mats(s_ref), heads(r_ref), heads(lw_ref), heads(k_ref), heads(v_ref), heads(a_ref),
                          heads(b_ref))
        for h in range(H):
            y_ref[:, lanes(h)] = y[h]
            s_ref[h] = sc[h]

        @pl.when(pl.program_id(0) == max(nc - 4, 0))
        def _():
            _gather_pass_on(x_src, x_dst, sems)

        @pl.when(pl.program_id(0) == nc - 1)
        def _():
            _gather_finish(x_src, x_dst, sems)

    hm = pl.BlockSpec((SCAN_CHUNK, H * Dh), lambda c: (c, 0))
    res = pl.pallas_call(
        body, name="rwkv_scan_fwd", grid=(nc,), in_specs=[hm] * 6 + [_ANY] * n_x,
        out_specs=[hm, pl.BlockSpec((1, H, Dv, Dh), lambda c: (c, 0, 0, 0))] + [_ANY] * n_x,
        out_shape=[jax.ShapeDtypeStruct((tn, H * Dh), F32), jax.ShapeDtypeStruct((nc, H, Dv, Dh), F32)]
        + _xchg_out_shapes(xs, False),
        scratch_shapes=[pltpu.VMEM((H, Dv, Dh), F32)] + _xchg_sems(n_x),
        compiler_params=_cparams(1))(r, lw, k, v, a, b, *xs)
    return res[0], res[1], res[2:]


def _cscan_bwd(r, lw, k, v, a, b, dy, ck, xs):
    n_x = len(xs)
    tn = r.shape[0]
    H, Dh, Dv = RWKV_HEADS, RWKV_HEAD_DIM, RWKV_HEAD_DIM
    nc = tn // SCAN_CHUNK
    lanes = lambda h: slice(h * Dh, (h + 1) * Dh)
    heads = lambda ref: tuple(ref[:, lanes(h)] for h in range(H))
    mats = lambda ref: tuple(ref[h] for h in range(H))

    def body(r_ref, lw_ref, k_ref, v_ref, a_ref, b_ref, dy_ref, ck_ref, *rest):
        x_src = rest[:n_x]
        d_refs = rest[n_x:n_x + 6]
        x_dst = rest[n_x + 6:2 * n_x + 6]
        g_ref = rest[2 * n_x + 6]
        sems = rest[2 * n_x + 7:]

        @pl.when(pl.program_id(0) == 0)
        def _():
            g_ref[...] = jnp.zeros(g_ref.shape, F32)
            _scatter_start(x_src, x_dst, sems)

        s0 = tuple(ck_ref[0, h] for h in range(H))
        _, vjp = jax.vjp(_chunk_fn, s0, heads(r_ref), heads(lw_ref), heads(k_ref), heads(v_ref), heads(a_ref),
                         heads(b_ref))
        grads = vjp((heads(dy_ref), mats(g_ref)))
        for h in range(H):
            g_ref[h] = grads[0][h]
            for d_ref, gz in zip(d_refs, grads[1:]):
                d_ref[:, lanes(h)] = gz[h]

        @pl.when(pl.program_id(0) == nc - 1)
        def _():
            _scatter_wait(x_src, x_dst, sems)

    hm = pl.BlockSpec((SCAN_CHUNK, H * Dh), lambda c: (nc - 1 - c, 0))
    hshape = jax.ShapeDtypeStruct((tn, H * Dh), F32)
    res = pl.pallas_call(
        body, name="rwkv_scan_bwd", grid=(nc,),
        in_specs=[hm] * 7 + [pl.BlockSpec((1, H, Dv, Dh), lambda c: (nc - 1 - c, 0, 0, 0))] + [_ANY] * n_x,
        out_specs=[hm] * 6 + [_ANY] * n_x, out_shape=[hshape] * 6 + _xchg_out_shapes(xs, True),
        scratch_shapes=[pltpu.VMEM((H, Dv, Dh), F32)] + _xchg_sems(n_x),
        compiler_params=_cparams(1))(r, lw, k, v, a, b, dy, ck, *xs)
    return res[:6], res[6:]


def _decay_mask(lg, i, j, blk):
    rows = lax.broadcasted_iota(jnp.int32, (blk, blk), 0)
    cols = lax.broadcasted_iota(jnp.int32, (blk, blk), 1)
    dd = (rows - cols + (i - j) * blk).astype(F32)
    return jnp.where(dd >= 0.0, jnp.exp(lg * jnp.maximum(dd, 0.0)), 0.0)


_NT = (((1,), (1,)), ((), ()))
_TN = (((0,), (0,)), ((), ()))


def _ret_attn_fwd(lg, q, k, v, v_col0=0, blk=ATT_BLOCK):
    tn = q.shape[0]
    Dh = RET_HEAD_DIM

    def body(lg_ref, q_ref, k_ref, v_ref, o_ref):
        i = pl.program_id(1)
        lgv = lg_ref[0][:, 0:1]
        qb = q_ref[...].astype(BF16)

        def jb(j, acc):
            ks = pl.ds(pl.multiple_of(j * blk, blk), blk)
            s = lax.dot_general(qb, k_ref[ks, :].astype(BF16), _NT, preferred_element_type=F32)
            s = s * _decay_mask(lgv, i, j, blk)
            return acc + jnp.dot(s.astype(BF16), v_ref[ks, :].astype(BF16), preferred_element_type=F32)

        o_ref[...] = lax.fori_loop(0, i + 1, jb, jnp.zeros((blk, Dh), F32))

    full = pl.BlockSpec((tn, Dh), lambda h, i: (0, h))
    qs = pl.BlockSpec((blk, Dh), lambda h, i: (i, h))
    return pl.pallas_call(
        body, name="ret_attn_fwd", grid=(RET_HEADS, tn // blk),
        in_specs=[pl.BlockSpec((1, 1, 128), lambda h, i: (h, 0, 0)), qs, full,
                  pl.BlockSpec((tn, Dh), lambda h, i: (0, v_col0 + h))],
        out_specs=qs, out_shape=jax.ShapeDtypeStruct(q.shape, F32), compiler_params=_cparams(2))(lg, q, k, v)


def _ret_attn_bwd(lg, q, k, v, do, v_col0=0, blk=ATT_BLOCK):
    tn = q.shape[0]
    nb = tn // blk
    Dh = RET_HEAD_DIM

    def body(lg_ref, q_ref, k_ref, v_ref, do_ref, dq_ref, dk_ref, dv_ref):
        lgv = lg_ref[0][:, 0:1]
        dk_ref[...] = jnp.zeros(dk_ref.shape, F32)
        dv_ref[...] = jnp.zeros(dv_ref.shape, F32)

        def ib(i, carry):
            qs = pl.ds(pl.multiple_of(i * blk, blk), blk)
            qb = q_ref[qs, :].astype(BF16)
            dob = do_ref[qs, :].astype(BF16)

            def jb(j, dq):
                ks = pl.ds(pl.multiple_of(j * blk, blk), blk)
                kb = k_ref[ks, :].astype(BF16)
                vb = v_ref[ks, :].astype(BF16)
                dm = _decay_mask(lgv, i, j, blk)
                s = lax.dot_general(qb, kb, _NT, preferred_element_type=F32) * dm
                ds = lax.dot_general(dob, vb, _NT, preferred_element_type=F32) * dm
                sb, dsb = s.astype(BF16), ds.astype(BF16)
                dv_ref[ks, :] += lax.dot_general(sb, dob, _TN, preferred_element_type=F32)
                dk_ref[ks, :] += lax.dot_general(dsb, qb, _TN, preferred_element_type=F32)
                return dq + jnp.dot(dsb, kb, preferred_element_type=F32)

            dq_ref[qs, :] = lax.fori_loop(0, i + 1, jb, jnp.zeros((blk, Dh), F32))
            return carry

        lax.fori_loop(0, nb, ib, 0)

    full = pl.BlockSpec((tn, Dh), lambda h: (0, h))
    sh = jax.ShapeDtypeStruct(q.shape, F32)
    return pl.pallas_call(
        body, name="ret_attn_bwd", grid=(RET_HEADS,),
        in_specs=[pl.BlockSpec((1, 1, 128), lambda h: (h, 0, 0)), full, full,
                  pl.BlockSpec((tn, Dh), lambda h: (0, v_col0 + h)), full],
        out_specs=[full, full, full], out_shape=[sh, sh, sh], compiler_params=_cparams(1))(lg, q, k, v, do)


def _next8_spec(a, tb):
    r = tb // 8
    last = a.shape[0] // 8 - 1
    return pl.BlockSpec((8, a.shape[1]), lambda i: (jnp.minimum((i + 1) * r, last), 0))


def _conv_taps(g_ext, cw_ref, cb_ref):
    return (cw_ref[2:3, :] * g_ext + cw_ref[1:2, :] * pltpu.roll(g_ext, 1, 0)
            + cw_ref[0:1, :] * pltpu.roll(g_ext, 2, 0) + cb_ref[...])


def _glu_fwd(gate, up, cw, cb, tb=TOK_BLOCK):
    tn = gate.shape[0]

    def body(g_ref, gh_ref, u_ref, cw_ref, cb_ref, o_ref):
        halo = jnp.where(pl.program_id(0) == 0, 0.0, gh_ref[...])
        g_ext = jnp.concatenate([halo, g_ref[...]], axis=0)
        gc = _conv_taps(g_ext, cw_ref, cb_ref)[8:, :]
        o_ref[...] = (gc * _sigmoid(gc) * u_ref[...]).astype(o_ref.dtype)

    return pl.pallas_call(
        body, name="glu_fwd", grid=(tn // tb,),
        in_specs=[_blk_spec(gate, tb), _prev8_spec(gate, tb), _blk_spec(up, tb), _full_spec(cw), _full_spec(cb)],
        out_specs=_blk_spec(gate, tb), out_shape=jax.ShapeDtypeStruct(gate.shape, BF16),
        compiler_params=_cparams(1))(gate, gate, up, cw, cb)


def _glu_bwd(gate, up, dact, cw, cb, tb=TOK_BLOCK):
    tn = gate.shape[0]
    nb = tn // tb

    def body(g_ref, gp_ref, gn_ref, u_ref, un_ref, d_ref, dn_ref, cw_ref, cb_ref, dg_ref, du_ref, dcw_ref, dcb_ref):
        i = pl.program_id(0)
        gprev = jnp.where(i == 0, 0.0, gp_ref[...])
        dnext = jnp.where(i == nb - 1, 0.0, dn_ref[...])
        g_ext = jnp.concatenate([gprev, g_ref[...], gn_ref[...]], axis=0)
        gc = _conv_taps(g_ext, cw_ref, cb_ref)[8:, :]
        u_e = jnp.concatenate([u_ref[...], un_ref[...]], axis=0)
        d_e = jnp.concatenate([d_ref[...], dnext], axis=0)
        s = _sigmoid(gc)
        dgc = d_e * u_e * (s * (1.0 + gc * (1.0 - s)))
        du_ref[...] = (d_ref[...] * (gc * s)[:tb, :]).astype(du_ref.dtype)
        n_e = tb + 8
        dg_ref[...] = (cw_ref[2:3, :] * dgc + cw_ref[1:2, :] * pltpu.roll(dgc, n_e - 1, 0)
                       + cw_ref[0:1, :] * pltpu.roll(dgc, n_e - 2, 0))[:tb, :].astype(dg_ref.dtype)

        @pl.when(i == 0)
        def _():
            dcw_ref[...] = jnp.zeros(dcw_ref.shape, F32)
            dcb_ref[...] = jnp.zeros(dcb_ref.shape, F32)

        dgc_b = dgc[:tb, :]
        g0 = g_ext[8:8 + tb, :]
        g1 = pltpu.roll(g_ext, 1, 0)[8:8 + tb, :]
        g2 = pltpu.roll(g_ext, 2, 0)[8:8 + tb, :]
        dcw_ref[2:3, :] += jnp.sum(dgc_b * g0, axis=0, keepdims=True)
        dcw_ref[1:2, :] += jnp.sum(dgc_b * g1, axis=0, keepdims=True)
        dcw_ref[0:1, :] += jnp.sum(dgc_b * g2, axis=0, keepdims=True)
        dcb_ref[...] += jnp.sum(dgc_b, axis=0, keepdims=True)

    sh = jax.ShapeDtypeStruct(gate.shape, BF16)
    return pl.pallas_call(
        body, name="glu_bwd", grid=(nb,),
        in_specs=[_blk_spec(gate, tb), _prev8_spec(gate, tb), _next8_spec(gate, tb), _blk_spec(up, tb),
                  _next8_spec(up, tb), _blk_spec(dact, tb), _next8_spec(dact, tb), _full_spec(cw), _full_spec(cb)],
        out_specs=[_blk_spec(gate, tb), _blk_spec(gate, tb), _full_spec(cw), _full_spec(cb)],
        out_shape=[sh, sh, jax.ShapeDtypeStruct(cw.shape, F32), jax.ShapeDtypeStruct(cb.shape, F32)],
        compiler_params=_cparams(1))(gate, gate, gate, up, up, dact, dact, cw, cb)


def _final_loss(x2, tgt, g, tb=TOK_BLOCK):
    tn, dm = x2.shape

    def body(x_ref, t_ref, g_ref, l_ref, dx_ref, dg_ref):
        y, vjp = jax.vjp(_rms_fn, x_ref[...], g_ref[...])
        err = y - t_ref[...]
        dx, dg = vjp(err * (1.0 / dm))

        @pl.when(pl.program_id(0) == 0)
        def _():
            l_ref[...] = jnp.zeros(l_ref.shape, F32)
            dg_ref[...] = jnp.zeros(dg_ref.shape, F32)

        part = 0.5 * jnp.sum(jnp.mean(err * err, axis=-1, keepdims=True), axis=0, keepdims=True)
        l_ref[...] += jnp.broadcast_to(part, l_ref.shape)
        dx_ref[...] = dx
        dg_ref[...] += dg

    return pl.pallas_call(
        body, name="final_loss", grid=(tn // tb,),
        in_specs=[_blk_spec(x2, tb), _blk_spec(tgt, tb), _full_spec(g)],
        out_specs=[pl.BlockSpec((8, 128), lambda i: (0, 0)), _blk_spec(x2, tb), _full_spec(g)],
        out_shape=[jax.ShapeDtypeStruct((8, 128), F32), jax.ShapeDtypeStruct(x2.shape, F32),
                   jax.ShapeDtypeStruct(g.shape, F32)],
        compiler_params=_cparams(1))(x2, tgt, g)


def _pad_cols(w, n):
    return jnp.pad(w, ((0, 0), (0, n - w.shape[1])))


def _pad_rows(w, n):
    return jnp.pad(w, ((0, n - w.shape[0]), (0, 0)))


def _local_step(x, tgt, W, late):
    tn = x.shape[0]
    Wd = RWKV_WIDTH
    row = lambda z: z.reshape(1, -1)
    g_mix, g_ffn, g_fin = row(W['norm_mix_g']), row(W['norm_ffn_g']), row(W['norm_final_g'])

    (h1,) = _tok_fwd("norm_mix_fwd", lambda a, g: (_rms_fn(a, g),), [x], [g_mix], [(D_MODEL,)])
    proj = _mm("proj_fwd", h1, W['w_in_t'], tb=True)
    p_rkv = _Cols(proj, 3 * Wd, 0)
    pre_consts = [row(W['rwkv_mu_w']), row(W['rwkv_mu_a']), row(W['rwkv_mu_g']), row(W['rwkv_mu_r']),
                  row(W['rwkv_mu_k']), row(W['rwkv_mu_v']), row(W['rwkv_w0']),
                  _pad_cols(W['rwkv_w1'], LORA_PAD), _pad_rows(W['rwkv_w2'], LORA_PAD), row(W['rwkv_a0']),
                  _pad_cols(W['rwkv_a1'], LORA_PAD), _pad_rows(W['rwkv_a2'], LORA_PAD),
                  W['rwkv_g1'], W['rwkv_g2'], row(W['rwkv_k_k']), row(W['rwkv_k_a'])]
    r, k, v, lw, nkk, b, g = _pre_a_fwd(h1, p_rkv, pre_consts)
    y_scan, ck, gathered = _cscan_fwd(r, lw, k, v, nkk, b, late)
    w_out, w_gate_t, w_up_t, w_down = [g_.reshape(-1, D_MODEL) for g_ in gathered]
    post_consts = [row(W['rwkv_lnx_w']), row(W['rwkv_lnx_b']), row(W['rwkv_r_k'])]
    (y_rwkv,) = _tok_fwd("rwkv_post_fwd", _rwkv_post_fn, [y_scan, r, k, v, g], post_consts, [(Wd,)],
                         out_dtypes=[BF16])

    pos = jnp.arange(tn, dtype=F32)
    half = RET_HEAD_DIM // 2
    inv_freq = ROPE_BASE ** (-jnp.arange(half, dtype=F32) / half)
    ang = pos[:, None] * inv_freq[None, :]
    cos2 = jnp.concatenate([jnp.cos(ang), jnp.cos(ang)], axis=1)
    sin2 = jnp.concatenate([-jnp.sin(ang), jnp.sin(ang)], axis=1)
    lg = jnp.log(1.0 - 2.0 ** (-5.0 - jnp.arange(RET_HEADS, dtype=F32)))
    lg = jnp.broadcast_to(lg[:, None, None], (RET_HEADS, 1, 128))
    q_p, k_p, g_ret = _Cols(proj, Wd, 3), _Cols(proj, Wd, 4), _Cols(proj, Wd, 6)
    v_col0 = 5 * Wd // RET_HEAD_DIM
    q_rot, k_rot = _tok_fwd("ret_rotary_fwd", _rotary_fn, [cos2, sin2, q_p, k_p], [], [(RET_WIDTH,)] * 2)
    y_ret_raw = _ret_attn_fwd(lg, q_rot, k_rot, proj, v_col0)
    gn_w = row(W['ret_gn_w'])
    (y_ret,) = _tok_fwd("ret_post_fwd", _ret_post_fn, [y_ret_raw, g_ret], [gn_w], [(RET_WIDTH,)],
                        out_dtypes=[BF16])

    ycat = jnp.concatenate([y_rwkv, y_ret], axis=1)
    x1 = _mm("out_proj_fwd", ycat, w_out, add=x)
    (h2,) = _tok_fwd("norm_ffn_fwd", lambda a_, g_: (_rms_fn(a_, g_),), [x1], [g_ffn], [(D_MODEL,)],
                     out_dtypes=[BF16])
    gate = _mm("ffn_gate_fwd", h2, w_gate_t, tb=True)
    up = _mm("ffn_up_fwd", h2, w_up_t, tb=True)
    cw = W['ffn_conv_w']
    cb = row(W['ffn_conv_b'])
    act = _glu_fwd(gate, up, cw, cb)
    x2 = _mm("ffn_down_fwd", act, w_down, add=x1)
    loss8, dx2, dg_fin = _final_loss(x2, tgt, g_fin)

    G = {'norm_final_g': dg_fin}
    dact = _mm("ffn_down_dx", dx2, w_down, tb=True)
    d_down = _mm("ffn_down_dw", act, dx2, ta=True, out_dtype=BF16)
    dgate, dup, dcw, dcb = _glu_bwd(gate, up, dact, cw, cb)
    G['ffn_conv_w'], G['ffn_conv_b'] = dcw, dcb
    dh2 = _mm("ffn_gate_dx", dgate, w_gate_t)
    dh2 = _mm("ffn_up_dx", dup, w_up_t, add=dh2)
    d_gate_t = _mm("ffn_gate_dw", dgate, h2, ta=True, out_dtype=BF16)
    d_up_t = _mm("ffn_up_dw", dup, h2, ta=True, out_dtype=BF16)
    dx1, G['norm_ffn_g'] = _tok_bwd("norm_ffn_bwd", lambda a_, g_: (_rms_fn(a_, g_),), [], [x1], [g_ffn], [dh2], add=dx2)
    dycat = _mm("out_proj_dx", dx1, w_out, tb=True)
    d_out = _mm("out_proj_dw", ycat, dx1, ta=True, out_dtype=BF16)
    late_grads = [z.reshape(N_DEV, -1, D_MODEL) for z in (d_out, d_gate_t, d_up_t, d_down)]
    dy_rwkv, dy_ret = _Cols(dycat, Wd, 0), _Cols(dycat, Wd, 1)

    dyr_raw, dg_ret, G['ret_gn_w'] = _tok_bwd("ret_post_bwd", _ret_post_fn, [], [y_ret_raw, g_ret], [gn_w], [dy_ret],
                                              tok_dtypes=[F32, BF16])
    dq_rot, dk_rot, dv_ret = _ret_attn_bwd(lg, q_rot, k_rot, proj, dyr_raw, v_col0)
    dq_p, dk_p = _tok_bwd("ret_rotary_bwd", _rotary_fn, [cos2, sin2], [q_p, k_p], [], [dq_rot, dk_rot],
                          tok_dtypes=[BF16, BF16])

    dy_scan, dr1, dk1, dv1, dg, G['rwkv_lnx_w'], G['rwkv_lnx_b'], G['rwkv_r_k'] = _tok_bwd(
        "rwkv_post_bwd", _rwkv_post_fn, [], [y_scan, r, k, v, g], post_consts, [dy_rwkv])
    (dr2, dlw, dk2, dv2, dnkk, db), late_parts = _cscan_bwd(r, lw, k, v, nkk, b, dy_scan, ck, late_grads)
    pre_cts = [(dr1, dr2), (dk1, dk2), (dv1, dv2), dlw, dnkk, db, dg]
    pre_out = _pre_a_bwd(h1, p_rkv, pre_consts, pre_cts)
    dh1_a, dp_rkv = pre_out[0], pre_out[1]
    (G['rwkv_mu_w'], G['rwkv_mu_a'], G['rwkv_mu_g'], G['rwkv_mu_r'], G['rwkv_mu_k'], G['rwkv_mu_v'], G['rwkv_w0'],
     dw1, dw2, G['rwkv_a0'], da1, da2, G['rwkv_g1'], G['rwkv_g2'], G['rwkv_k_k'], G['rwkv_k_a']) = pre_out[2:]
    G['rwkv_w1'], G['rwkv_w2'] = dw1[:, :64], dw2[:64, :]
    G['rwkv_a1'], G['rwkv_a2'] = da1[:, :64], da2[:64, :]

    dproj = jnp.concatenate([dp_rkv, dq_p, dk_p, dv_ret.astype(BF16), dg_ret], axis=1)
    dh1 = _mm("proj_dx", dproj, W['w_in_t'], add=dh1_a)
    G['w_in_t'] = _mm("proj_dw", dproj, h1, ta=True, out_dtype=BF16)
    dx, G['norm_mix_g'] = _tok_bwd("norm_mix_bwd", lambda a_, g_: (_rms_fn(a_, g_),), [], [x], [g_mix], [dh1], add=dx1)
    return loss8[0, 0], dx, G, late_parts


def _adamw(name, parts, w, m, v):
    rows, cols = w.shape
    sub = 8 * 4 // parts.dtype.itemsize
    tb = max(t for t in range(sub, 65, sub) if rows % t == 0) if rows > 64 else rows
    c1 = 1.0 - ADAM_B1 ** ADAM_STEP
    c2 = 1.0 - ADAM_B2 ** ADAM_STEP

    def body(p_ref, w_ref, m_ref, v_ref, g_ref, d_ref, nm_ref, nv_ref):
        g = p_ref[0].astype(F32)
        for d in range(1, N_DEV):
            g = g + p_ref[d].astype(F32)
        mn = ADAM_B1 * m_ref[...] + (1.0 - ADAM_B1) * g
        vn = ADAM_B2 * v_ref[...] + (1.0 - ADAM_B2) * (g * g)
        m_hat = mn / c1
        v_hat = vn / c2
        g_ref[...] = g
        d_ref[...] = -ADAM_LR * (m_hat / (jnp.sqrt(v_hat) + ADAM_EPS) + ADAM_WD * w_ref[...])
        nm_ref[...] = mn
        nv_ref[...] = vn

    spec = pl.BlockSpec((tb, cols), lambda i: (i, 0))
    sh = jax.ShapeDtypeStruct((rows, cols), F32)
    return pl.pallas_call(
        body, name=name, grid=(rows // tb,),
        in_specs=[pl.BlockSpec((N_DEV, tb, cols), lambda i: (0, i, 0)), spec, spec, spec],
        out_specs=[spec] * 4, out_shape=[sh] * 4, compiler_params=_cparams(1))(parts, w, m, v)


def _local_shape(name):
    gs, ax = SHARDED[name]
    ls = list(gs)
    ls[ax] //= N_DEV
    return tuple(ls)


def _seg(flat, seg):
    n = flat.shape[-1]
    pad = _round_up(n, seg) - n
    if pad:
        flat = jnp.pad(flat, [(0, 0)] * (flat.ndim - 1) + [(0, pad)])
    return flat


def _split3(w):
    hi = w.astype(BF16)
    r1 = w - hi.astype(F32)
    mid = r1.astype(BF16)
    lo = (r1 - mid.astype(F32)).astype(BF16)
    return hi, mid, lo


def _pack_small_shards(shards):
    pieces = []
    for name in SMALL_NAMES:
        flat = shards[name].reshape(-1)
        if name == 'ffn_conv_w':
            pieces += [_seg(p, BF16_SEG) for p in _split3(flat)]
        else:
            pieces.append(flat.astype(BF16))
    return jnp.concatenate(pieces).reshape(-1, 128)


def _unpack_small(gathered):
    flat = gathered.reshape(N_DEV, -1)
    out, off = {}, 0
    for name in SMALL_NAMES:
        gs, ax = SHARDED[name]
        ls = _local_shape(name)
        n = int(np.prod(ls))
        if name == 'ffn_conv_w':
            nseg = _round_up(n, BF16_SEG)
            hi, mid, lo = (flat[:, off + j * nseg: off + j * nseg + n].astype(F32) for j in range(3))
            sh = ((hi + mid) + lo).reshape(N_DEV, 3, -1)
            out[name] = jnp.swapaxes(sh, 0, 1).reshape(3, D_FF)
            off += 3 * nseg
        else:
            sh = flat[:, off:off + n].reshape((N_DEV,) + ls[1:])
            out[name] = sh.reshape(gs[1:]) if ax == 1 else jnp.swapaxes(sh, 0, 1).reshape(gs[1:])
            off += n
    return out


def _small_pieces(sharded, repl):
    return [sharded[n].reshape(-1) for n in SMALL_NAMES] + [repl[n].reshape(-1) for n in REPL_NAMES]


def _pack_small_local(d):
    flat = jnp.concatenate(_small_pieces(d, d))
    return _seg(flat, F32_SEG).reshape(-1, 128)


def _pack_small_grads(G):
    pieces = []
    for name in SMALL_NAMES:
        gs, ax = SHARDED[name]
        g = G[name]
        if name == 'ffn_conv_w':
            sh = jnp.swapaxes(g.reshape(3, N_DEV, -1), 0, 1)
        elif ax == 1:
            sh = g
        else:
            sh = jnp.swapaxes(g.reshape(g.shape[0], N_DEV, -1), 0, 1)
        pieces.append(sh.reshape(N_DEV, -1))
    rep = jnp.concatenate([G[n].reshape(-1) for n in REPL_NAMES])
    pieces.append(jnp.broadcast_to(rep[None, :], (N_DEV, rep.shape[0])))
    flat = _seg(jnp.concatenate(pieces, axis=1), F32_SEG)
    return flat.reshape(N_DEV, -1, 128)


def _unpack_small_local(packed, local_shapes):
    flat = packed.reshape(-1)
    out, off = {}, 0
    for name in SMALL_NAMES + REPL_NAMES:
        n = int(np.prod(local_shapes[name]))
        out[name] = flat[off:off + n].reshape(local_shapes[name])
        off += n
    return out


def kernel(x, *rest):
    nw = len(WEIGHT_NAMES)
    assert len(rest) == 3 * nw + 1
    weights = dict(zip(WEIGHT_NAMES, rest[:nw]))
    loss_target = rest[nw]
    moms = dict(zip(WEIGHT_NAMES, rest[nw + 1:2 * nw + 1]))
    vars_ = dict(zip(WEIGHT_NAMES, rest[2 * nw + 1:]))
    local_shapes = {n: weights[n].shape for n in WEIGHT_NAMES}

    def native2d(name, a):
        a2 = a.reshape(a.shape[-2], a.shape[-1])
        return a2.T if name in BIG_T else a2

    def from2d(name, a2):
        return (a2.T if name in BIG_T else a2).reshape(local_shapes[name])

    big_w = {n: native2d(n, weights[n]) for n in BIG_NAMES}
    w_in_t_sh = big_w['w_in'].astype(BF16)
    late = [big_w[n].astype(BF16) for n in LATE_NAMES]
    small_sh = _pack_small_shards({n: weights[n] for n in SMALL_NAMES})
    w_in_g, small_g = _exchange("weights_all_gather", [w_in_t_sh, small_sh], False)
    W = _unpack_small(small_g)
    W['w_in_t'] = w_in_g.reshape(-1, D_MODEL)
    for n in REPL_NAMES:
        W[n] = weights[n][0] if n != 'norm_final_g' else weights[n]

    loss, dx, G, late_parts = _local_step(x[0], loss_target[0], W, late)

    w_in_parts, small_parts = _exchange(
        "grads_all_to_all", [G['w_in_t'].reshape(N_DEV, -1, D_MODEL), _pack_small_grads(G)], True)
    results = {}
    for n, parts in zip(['w_in'] + LATE_NAMES, [w_in_parts] + list(late_parts)):
        res = _adamw("adamw_" + n, parts, big_w[n], native2d(n, moms[n]), native2d(n, vars_[n]))
        results[n] = [from2d(n, r) for r in res]
    small_res = _adamw("adamw_small", small_parts, _pack_small_local(weights), _pack_small_local(moms),
                       _pack_small_local(vars_))
    small_out = [_unpack_small_local(p, local_shapes) for p in small_res]

    loss = lax.psum(loss, ("x", "y", "c"))
    outs = [loss, dx[None]]
    for j in range(4):
        outs += [results[n][j] if n in results else small_out[j][n] for n in WEIGHT_NAMES]
    return tuple(outs)
```

```python
import functools
import math

import numpy as np
import jax
import jax.numpy as jnp
from jax import lax
from jax.experimental import pallas as pl
from jax.experimental.pallas import tpu as pltpu

F32 = jnp.float32
BF16 = jnp.bfloat16

N_DEV = 8
D_MODEL = 1024
RWKV_HEADS = 8
RWKV_HEAD_DIM = 64
RWKV_WIDTH = 512
RET_HEADS = 4
RET_HEAD_DIM = 128
RET_WIDTH = 512
LORA_PAD = 128
D_FF = 2816
NORM_EPS = 1e-6
RWKV_GN_EPS = 64e-5
RET_GN_EPS = 1e-5
ROPE_BASE = 10000.0
ADAM_LR, ADAM_B1, ADAM_B2, ADAM_EPS, ADAM_WD, ADAM_STEP = 0.001, 0.9, 0.999, 1e-08, 0.01, 10

VMEM_LIMIT = 56 * 1024 * 1024
TOK_BLOCK = 256
SCAN_CHUNK = 64
ATT_BLOCK = 512
BF16_SEG = 2048
F32_SEG = 1024

WEIGHT_NAMES = ['norm_mix_g', 'w_in', 'rwkv_mu_r', 'rwkv_mu_k', 'rwkv_mu_v', 'rwkv_mu_w', 'rwkv_mu_a',
                'rwkv_mu_g', 'rwkv_w0', 'rwkv_w1', 'rwkv_w2', 'rwkv_a0', 'rwkv_a1', 'rwkv_a2', 'rwkv_g1',
                'rwkv_g2', 'rwkv_k_k', 'rwkv_k_a', 'rwkv_r_k', 'rwkv_lnx_w', 'rwkv_lnx_b', 'ret_gn_w',
                'w_out', 'norm_ffn_g', 'ffn_w_gate', 'ffn_w_up', 'ffn_conv_w', 'ffn_conv_b', 'ffn_w_down',
                'norm_final_g']
SHARDED = {
    'w_in': ((1, 1024, 3584), 2), 'rwkv_w1': ((1, 1024, 64), 1), 'rwkv_w2': ((1, 64, 512), 2),
    'rwkv_a1': ((1, 1024, 64), 1), 'rwkv_a2': ((1, 64, 512), 2), 'rwkv_g1': ((1, 1024, 128), 1),
    'rwkv_g2': ((1, 128, 512), 2), 'w_out': ((1, 1024, 1024), 1), 'ffn_w_gate': ((1, 1024, 2816), 2),
    'ffn_w_up': ((1, 1024, 2816), 2), 'ffn_conv_w': ((1, 3, 1, 2816), 3), 'ffn_w_down': ((1, 2816, 1024), 1),
}
REPL_NAMES = [n for n in WEIGHT_NAMES if n not in SHARDED]
BIG_NAMES = ['w_in', 'w_out', 'ffn_w_gate', 'ffn_w_up', 'ffn_w_down']
BIG_T = ('w_in', 'ffn_w_gate', 'ffn_w_up')
LATE_NAMES = ['w_out', 'ffn_w_gate', 'ffn_w_up', 'ffn_w_down']
SMALL_NAMES = [n for n in WEIGHT_NAMES if n in SHARDED and n not in BIG_NAMES]


def _cparams(n_grid):
    return pltpu.CompilerParams(dimension_semantics=("arbitrary",) * n_grid, vmem_limit_bytes=VMEM_LIMIT)


def _round_up(n, m):
    return (n + m - 1) // m * m


@jax.custom_vjp
def _bdot(x, w):
    return jnp.dot(x.astype(BF16), w.astype(BF16), preferred_element_type=F32)


def _bdot_fwd(x, w):
    return _bdot(x, w), (x, w)


def _bdot_bwd(res, g):
    x, w = res
    gb = g.astype(BF16)
    dx = lax.dot_general(gb, w.astype(BF16), (((1,), (1,)), ((), ())), preferred_element_type=F32)
    dw = lax.dot_general(x.astype(BF16), gb, (((0,), (0,)), ((), ())), preferred_element_type=F32)
    return dx, dw.astype(w.dtype)


_bdot.defvjp(_bdot_fwd, _bdot_bwd)


@jax.custom_vjp
def _shift_rows(x, prev):
    rolled = pltpu.roll(x, 1, 0)
    row = lax.broadcasted_iota(jnp.int32, x.shape, 0)
    return jnp.where(row == 0, jnp.broadcast_to(prev, x.shape), rolled)


def _shift_rows_fwd(x, prev):
    return _shift_rows(x, prev), None


def _shift_rows_bwd(_, g):
    n = g.shape[0]
    rolled = pltpu.roll(g, n - 1, 0)
    row = lax.broadcasted_iota(jnp.int32, g.shape, 0)
    return jnp.where(row == n - 1, 0.0, rolled), g[0:1, :]


_shift_rows.defvjp(_shift_rows_fwd, _shift_rows_bwd)


@jax.custom_vjp
def _swap_halves(x):
    return pltpu.roll(x, 64, 1)


_swap_halves.defvjp(lambda x: (_swap_halves(x), None), lambda _, g: (pltpu.roll(g, 64, 1),))


def _sigmoid(x):
    return 1.0 / (1.0 + jnp.exp(-x))


def _softplus(x):
    return jnp.maximum(x, 0.0) + jnp.log(1.0 + jnp.exp(-jnp.abs(x)))


def _rms_fn(x, g):
    return x * lax.rsqrt(jnp.mean(x * x, axis=-1, keepdims=True) + NORM_EPS) * g


def _pre_a_fn(h1, h1p, p, pp, mu_w, mu_a, mu_g, mu_r, mu_k, mu_v, w0, w1, w2, a0, a1, a2, g1, g2, k_k, k_a):
    W = RWKV_WIDTH
    h1s = _shift_rows(h1, h1p)
    ps = _shift_rows(p, pp)
    dx = h1s - h1
    xw = h1 + dx * mu_w
    xa = h1 + dx * mu_a
    xg = h1 + dx * mu_g
    dp = ps - p
    r = p[:, 0:W] + dp[:, 0:W] * mu_r
    k0 = p[:, W:2 * W] + dp[:, W:2 * W] * mu_k
    v = p[:, 2 * W:3 * W] + dp[:, 2 * W:3 * W] * mu_v
    wl = w0 + _bdot(jnp.tanh(_bdot(xw, w1)), w2)
    w_log = -_softplus(-wl) - 0.5
    lw = -jnp.exp(w_log)
    a = _sigmoid(a0 + _bdot(_bdot(xa, a1), a2))
    g = _bdot(_sigmoid(_bdot(xg, g1)), g2)
    nkk, k, b = _pre_b_fn(k0, a, k_k, k_a)
    return r, k, v, lw, nkk, b, g


def _head_sum_raw(x):
    n = x.shape[1]
    ii = lax.broadcasted_iota(jnp.int32, (n, n), 0) // RWKV_HEAD_DIM
    jj = lax.broadcasted_iota(jnp.int32, (n, n), 1) // RWKV_HEAD_DIM
    ones = (ii == jj).astype(BF16)
    xh = x.astype(BF16)
    xl = (x - xh.astype(F32)).astype(BF16)
    return jnp.dot(xh, ones, preferred_element_type=F32) + jnp.dot(xl, ones, preferred_element_type=F32)


@jax.custom_vjp
def _head_sum(x):
    return _head_sum_raw(x)


_head_sum.defvjp(lambda x: (_head_sum_raw(x), None), lambda _, g: (_head_sum_raw(g),))


def _pre_b_fn(k0, a, k_k, k_a):
    kkr = k0 * k_k
    nrm = jnp.sqrt(_head_sum(kkr * kkr))
    kk = kkr / jnp.maximum(nrm, 1e-12)
    k = k0 * (1.0 + (a - 1.0) * k_a)
    return -kk, k, kk * a


def _rwkv_post_fn(y, r, k, v, g, lnx_w, lnx_b, r_k):
    inv = 1.0 / RWKV_HEAD_DIM
    mu = _head_sum(y) * inv
    yc = y - mu
    var = _head_sum(yc * yc) * inv
    yn = yc * lax.rsqrt(var + RWKV_GN_EPS) * lnx_w + lnx_b
    bonus = _head_sum(r * k * r_k) * v
    return ((yn + bonus) * g,)


def _rotary_fn(cos2, sin2, q, k):
    qs, ks = [], []
    for h in range(RET_HEADS):
        sl = slice(h * RET_HEAD_DIM, (h + 1) * RET_HEAD_DIM)
        qh, kh = q[:, sl], k[:, sl]
        qs.append(qh * cos2 + _swap_halves(qh) * sin2)
        ks.append((kh * cos2 + _swap_halves(kh) * sin2) * (RET_HEAD_DIM ** -0.5))
    return jnp.concatenate(qs, axis=1), jnp.concatenate(ks, axis=1)


def _ret_post_fn(y, gp, gn_w):
    outs = []
    for h in range(RET_HEADS):
        sl = slice(h * RET_HEAD_DIM, (h + 1) * RET_HEAD_DIM)
        yh = y[:, sl]
        mu = jnp.mean(yh, axis=-1, keepdims=True)
        yc = yh - mu
        var = jnp.mean(yc * yc, axis=-1, keepdims=True)
        outs.append(yc * lax.rsqrt(var + RET_GN_EPS) * gn_w[:, sl])
    yn = jnp.concatenate(outs, axis=1)
    return (gp * _sigmoid(gp) * yn,)


class _Cols:
    def __init__(self, array, width, block):
        self.array, self.width, self.block = array, width, block
        self.shape, self.ndim, self.dtype = (array.shape[0], width), 2, array.dtype


def _arr(a):
    return a.array if isinstance(a, _Cols) else a


def _blk_spec(a, tb, rev_nb=None):
    nd = a.ndim
    tail = (a.block,) if isinstance(a, _Cols) else (0,) * (nd - 1)
    if rev_nb is None:
        return pl.BlockSpec((tb,) + a.shape[1:], lambda i: (i,) + tail)
    return pl.BlockSpec((tb,) + a.shape[1:], lambda i: (rev_nb - 1 - i,) + tail)


def _full_spec(a):
    nd = a.ndim
    return pl.BlockSpec(a.shape, lambda i: (0,) * nd)


def _tok_fwd(name, fn, toks, consts, out_tails, tb=TOK_BLOCK, out_dtypes=None):
    out_dtypes = out_dtypes or [F32] * len(out_tails)
    n_in = len(toks) + len(consts)
    tn = toks[0].shape[0]

    def body(*refs):
        outs = fn(*[r[...] for r in refs[:n_in]])
        for r, o in zip(refs[n_in:], outs):
            r[...] = o.astype(r.dtype)

    out_shape = [jax.ShapeDtypeStruct((tn,) + tuple(s), dt) for s, dt in zip(out_tails, out_dtypes)]
    return pl.pallas_call(
        body, name=name, grid=(tn // tb,),
        in_specs=[_blk_spec(a, tb) for a in toks] + [_full_spec(c) for c in consts],
        out_specs=[_blk_spec(o, tb) for o in out_shape], out_shape=out_shape,
        compiler_params=_cparams(1))(*[_arr(a) for a in toks], *consts)


def _tok_bwd(name, fn, aux, toks, consts, cts, add=None, tb=TOK_BLOCK, tok_dtypes=None):
    n_aux, n_tok, n_c = len(aux), len(toks), len(consts)
    ct_groups = [c if isinstance(c, (tuple, list)) else (c,) for c in cts]
    ct_flat = [a for grp in ct_groups for a in grp]
    n_ct = len(ct_flat)
    n_add = 0 if add is None else 1
    tn = toks[0].shape[0]

    def body(*refs):
        pos = 0
        aux_v = [r[...] for r in refs[pos:pos + n_aux]]; pos += n_aux
        tok_v = [r[...] for r in refs[pos:pos + n_tok]]; pos += n_tok
        const_v = [r[...] for r in refs[pos:pos + n_c]]; pos += n_c
        ct_refs = refs[pos:pos + n_ct]; pos += n_ct
        add_refs = refs[pos:pos + n_add]; pos += n_add
        dtok_refs = refs[pos:pos + n_tok]; pos += n_tok
        dconst_refs = refs[pos:pos + n_c]
        ct_v, q = [], 0
        for grp in ct_groups:
            s = ct_refs[q][...]
            for r in ct_refs[q + 1:q + len(grp)]:
                s = s + r[...]
            q += len(grp)
            ct_v.append(s)
        _, vjp = jax.vjp(lambda *tc: fn(*aux_v, *tc), *tok_v, *const_v)
        grads = vjp(tuple(ct_v))
        for j, r in enumerate(dtok_refs):
            gj = grads[j]
            if j == 0 and n_add:
                gj = gj + add_refs[0][...]
            r[...] = gj.astype(r.dtype)

        @pl.when(pl.program_id(0) == 0)
        def _():
            for r in dconst_refs:
                r[...] = jnp.zeros(r.shape, F32)

        for j, r in enumerate(dconst_refs):
            r[...] += grads[n_tok + j]

    ins = list(aux) + list(toks) + list(consts) + ct_flat + ([add] if n_add else [])
    in_specs = ([_blk_spec(a, tb) for a in aux] + [_blk_spec(a, tb) for a in toks] + [_full_spec(c) for c in consts]
                + [_blk_spec(a, tb) for a in ct_flat] + ([_blk_spec(add, tb)] if n_add else []))
    tok_dtypes = tok_dtypes or [F32] * n_tok
    out_shape = ([jax.ShapeDtypeStruct(a.shape, dt) for a, dt in zip(toks, tok_dtypes)]
                 + [jax.ShapeDtypeStruct(c.shape, F32) for c in consts])
    out_specs = [_blk_spec(o, tb) for o in out_shape[:n_tok]] + [_full_spec(c) for c in consts]
    return pl.pallas_call(body, name=name, grid=(tn // tb,), in_specs=in_specs, out_specs=out_specs,
                          out_shape=out_shape, compiler_params=_cparams(1))(*[_arr(a) for a in ins])


MM_VMEM_BUDGET = 40 * 1024 * 1024
MM_STEP_SECONDS = 0.4e-6
MM_HBM_BYTES_PER_SECOND = 2.5e12
MM_XPOSE_SECONDS_PER_ELEM = 2e-12
MM_MXU_COLUMNS = 256
MM_MXU_FLOPS = 9e14


def _mm_tiles(m, n, kd, a_bytes, b_bytes, o_bytes, has_add, ta):
    divs = lambda d: [t for t in range(128, d + 1, 128) if d % t == 0]
    best = None
    for tm in divs(m):
        for tn in divs(n):
            for tk in divs(kd):
                ni, nj, nk = m // tm, n // tn, kd // tk
                vmem = (2 * tm * tk * a_bytes + 2 * tk * tn * b_bytes + tm * tn * 4 + 2 * tm * tn * o_bytes
                        + (2 * tm * tn * 4 if has_add else 0) + 2 * (tm * tk + tk * tn) + tm * tn * 4)
                if vmem > MM_VMEM_BUDGET:
                    continue
                a_traffic = m * kd * a_bytes * (nj if nk > 1 else 1)
                b_traffic = kd * n * b_bytes * (ni if nj * nk > 1 else 1)
                cost = ni * nj * nk * MM_STEP_SECONDS + (a_traffic + b_traffic) / MM_HBM_BYTES_PER_SECOND
                cost += 2.0 * m * kd * nj * max(tn, MM_MXU_COLUMNS) / MM_MXU_FLOPS
                if ta:
                    cost += m * kd * nj * MM_XPOSE_SECONDS_PER_ELEM
                if best is None or cost < best[0]:
                    best = (cost, tm, tn, tk)
    return best[1:]


def _mm(name, a, b, ta=False, tb=False, add=None, out_dtype=F32):
    if ta:
        kd, m = a.shape
    else:
        m, kd = a.shape
    if tb:
        n, kb = b.shape
    else:
        kb, n = b.shape
    assert kd == kb, (a.shape, b.shape)
    tm, tn, tk = _mm_tiles(m, n, kd, a.dtype.itemsize, b.dtype.itemsize, jnp.dtype(out_dtype).itemsize,
                           add is not None, ta)
    nk = kd // tk
    has_add = add is not None
    dims = (((0 if ta else 1,), (1 if tb else 0,)), ((), ()))

    def body(*refs):
        a_ref, b_ref = refs[0], refs[1]
        o_ref, acc_ref = refs[-2], refs[-1]
        k = pl.program_id(2)

        @pl.when(k == 0)
        def _():
            acc_ref[...] = refs[2][...] if has_add else jnp.zeros(acc_ref.shape, F32)

        acc_ref[...] += lax.dot_general(a_ref[...].astype(BF16), b_ref[...].astype(BF16), dims,
                                        preferred_element_type=F32)

        @pl.when(k == nk - 1)
        def _():
            o_ref[...] = acc_ref[...].astype(out_dtype)

    a_spec = pl.BlockSpec((tk, tm), lambda i, j, k: (k, i)) if ta else pl.BlockSpec((tm, tk), lambda i, j, k: (i, k))
    b_spec = pl.BlockSpec((tn, tk), lambda i, j, k: (j, k)) if tb else pl.BlockSpec((tk, tn), lambda i, j, k: (k, j))
    o_spec = pl.BlockSpec((tm, tn), lambda i, j, k: (i, j))
    ins = [a, b] + ([add] if has_add else [])
    in_specs = [a_spec, b_spec] + ([o_spec] if has_add else [])
    return pl.pallas_call(body, name=name, grid=(m // tm, n // tn, nk), in_specs=in_specs, out_specs=o_spec,
                          out_shape=jax.ShapeDtypeStruct((m, n), out_dtype),
                          scratch_shapes=[pltpu.VMEM((tm, tn), F32)], compiler_params=_cparams(3))(*ins)


def _prev8_spec(a, tb, rev_nb=None):
    r = tb // 8
    if rev_nb is None:
        return pl.BlockSpec((8, a.shape[1]), lambda i: (jnp.maximum(i * r - 1, 0), 0))
    return pl.BlockSpec((8, a.shape[1]), lambda i: (jnp.maximum((rev_nb - 1 - i) * r - 1, 0), 0))


def _pre_a_fwd(h1, p, consts, tb=TOK_BLOCK):
    tn = h1.shape[0]

    def body(h1_ref, h1h_ref, p_ref, ph_ref, *rest):
        c_refs, o_refs = rest[:len(consts)], rest[len(consts):]
        first = pl.program_id(0) == 0
        h1p = jnp.where(first, 0.0, h1h_ref[7:8, :])
        pp = jnp.where(first, 0.0, ph_ref[7:8, :])
        outs = _pre_a_fn(h1_ref[...], h1p, p_ref[...], pp, *[c[...] for c in c_refs])
        for r, o in zip(o_refs, outs):
            r[...] = o

    out_shape = [jax.ShapeDtypeStruct((tn, RWKV_WIDTH), F32) for _ in range(7)]
    return pl.pallas_call(
        body, name="rwkv_pre_a_fwd", grid=(tn // tb,),
        in_specs=[_blk_spec(h1, tb), _prev8_spec(h1, tb), _blk_spec(p, tb), _prev8_spec(p, tb)]
        + [_full_spec(c) for c in consts],
        out_specs=[_blk_spec(o, tb) for o in out_shape], out_shape=out_shape,
        compiler_params=_cparams(1))(h1, h1, _arr(p), _arr(p), *consts)


def _pre_a_bwd(h1, p, consts, cts, tb=TOK_BLOCK):
    tn = h1.shape[0]
    nb = tn // tb
    n_c = len(consts)
    ct_groups = [c if isinstance(c, (tuple, list)) else (c,) for c in cts]
    ct_flat = [a for grp in ct_groups for a in grp]
    n_ct = len(ct_flat)

    def body(*refs):
        h1_ref, h1h_ref, p_ref, ph_ref = refs[:4]
        c_refs = refs[4:4 + n_c]
        ct_refs = refs[4 + n_c:4 + n_c + n_ct]
        dh1_ref, dp_ref = refs[4 + n_c + n_ct:6 + n_c + n_ct]
        dc_refs = refs[6 + n_c + n_ct:6 + 2 * n_c + n_ct]
        ch_ref, cp_ref = refs[-2], refs[-1]
        i = pl.program_id(0)
        first_block = i == nb - 1
        h1p = jnp.where(first_block, 0.0, h1h_ref[7:8, :])
        pp = jnp.where(first_block, 0.0, ph_ref[7:8, :])
        ct_v, q = [], 0
        for grp in ct_groups:
            s = ct_refs[q][...]
            for r in ct_refs[q + 1:q + len(grp)]:
                s = s + r[...]
            q += len(grp)
            ct_v.append(s)
        _, vjp = jax.vjp(_pre_a_fn, h1_ref[...], h1p, p_ref[...], pp, *[c[...] for c in c_refs])
        grads = vjp(tuple(ct_v))

        @pl.when(i == 0)
        def _():
            ch_ref[...] = jnp.zeros(ch_ref.shape, F32)
            cp_ref[...] = jnp.zeros(cp_ref.shape, F32)
            for r in dc_refs:
                r[...] = jnp.zeros(r.shape, F32)

        rowh = lax.broadcasted_iota(jnp.int32, (tb, h1.shape[1]), 0)
        rowp = lax.broadcasted_iota(jnp.int32, (tb, p.shape[1]), 0)
        dh1_ref[...] = grads[0] + jnp.where(rowh == tb - 1, jnp.broadcast_to(ch_ref[0:1, :], rowh.shape), 0.0)
        dp_ref[...] = (grads[2] + jnp.where(rowp == tb - 1, jnp.broadcast_to(cp_ref[0:1, :], rowp.shape), 0.0)
                       ).astype(dp_ref.dtype)
        ch_ref[0:1, :] = grads[1]
        cp_ref[0:1, :] = grads[3]
        for j, r in enumerate(dc_refs):
            r[...] += grads[4 + j]

    ins = [h1, h1, _arr(p), _arr(p)] + list(consts) + ct_flat
    in_specs = ([_blk_spec(h1, tb, nb), _prev8_spec(h1, tb, nb), _blk_spec(p, tb, nb), _prev8_spec(p, tb, nb)]
                + [_full_spec(c) for c in consts] + [_blk_spec(a, tb, nb) for a in ct_flat])
    out_shape = ([jax.ShapeDtypeStruct(h1.shape, F32), jax.ShapeDtypeStruct(p.shape, BF16)]
                 + [jax.ShapeDtypeStruct(c.shape, F32) for c in consts])
    out_specs = [_blk_spec(h1, tb, nb), _blk_spec(p, tb, nb)] + [_full_spec(c) for c in consts]
    return pl.pallas_call(body, name="rwkv_pre_a_bwd", grid=(nb,), in_specs=in_specs, out_specs=out_specs,
                          out_shape=out_shape,
                          scratch_shapes=[pltpu.VMEM((8, h1.shape[1]), F32), pltpu.VMEM((8, p.shape[1]), F32)],
                          compiler_params=_cparams(1))(*ins)


def _my_index():
    return 4 * lax.axis_index("x") + 2 * lax.axis_index("y") + lax.axis_index("c")


def _peer(k):
    x, y, c = lax.axis_index("x"), lax.axis_index("y"), lax.axis_index("c")
    px = 1 - x if k & 4 else x
    py = 1 - y if k & 2 else y
    pc = 1 - c if k & 1 else c
    return (px, py, pc), 4 * px + 2 * py + pc


def _xchg_sems(n):
    return [pltpu.SemaphoreType.DMA((n * (N_DEV - 1),)), pltpu.SemaphoreType.DMA((n * (N_DEV - 1),)),
            pltpu.SemaphoreType.DMA((n,))]


def _scatter_copies(srcs, dsts, sems, incoming=False):
    send_sems, recv_sems, local_sems = sems
    me = _my_index()
    local, remote = [], []
    for i, (s, d) in enumerate(zip(srcs, dsts)):
        if not incoming:
            local.append(pltpu.make_async_copy(s.at[me], d.at[me], local_sems.at[i]))
        for k in range(1, N_DEV):
            peer, plin = _peer(k)
            j = i * (N_DEV - 1) + k - 1
            s_slot, d_slot = (me, plin) if incoming else (plin, me)
            remote.append(pltpu.make_async_remote_copy(
                src_ref=s.at[s_slot], dst_ref=d.at[d_slot], send_sem=send_sems.at[j],
                recv_sem=recv_sems.at[j], device_id=peer, device_id_type=pl.DeviceIdType.MESH))
    return local, remote


def _scatter_start(srcs, dsts, sems):
    local, out = _scatter_copies(srcs, dsts, sems)
    for cp in local + out:
        cp.start()


def _scatter_wait(srcs, dsts, sems):
    for cp in _scatter_copies(srcs, dsts, sems, incoming=True)[1]:
        cp.wait_recv()
    local, out = _scatter_copies(srcs, dsts, sems)
    for cp in out:
        cp.wait_send()
    for cp in local:
        cp.wait()


_ICI_PEERS = (2, 4, 6)


def _gather_copies(srcs, dsts, sems, group):
    send_sems, recv_sems, local_sems = sems
    me = _my_index()
    sib, sib_lin = _peer(1)
    out = []
    for i, (s, d) in enumerate(zip(srcs, dsts)):
        def mk(q, src, dst, dev):
            j = i * (N_DEV - 1) + q
            return pltpu.make_async_remote_copy(src_ref=src, dst_ref=dst, send_sem=send_sems.at[j],
                                                recv_sem=recv_sems.at[j], device_id=dev,
                                                device_id_type=pl.DeviceIdType.MESH)
        if group == 'local':
            out.append(pltpu.make_async_copy(s, d.at[me], local_sems.at[i]))
        elif group == 'own':
            out.append(mk(0, s, d.at[me], sib))
        elif group == 'in_d2d':
            out.append(mk(0, s, d.at[sib_lin], sib))
        for jj, k in enumerate(_ICI_PEERS):
            peer, plin = _peer(k)
            plin_other = _peer(k + 1)[1]
            if group == 'own':
                out.append(mk(1 + jj, s, d.at[me], peer))
            elif group == 'in_ici':
                out.append(mk(1 + jj, s, d.at[plin], peer))
            elif group == 'pass_on':
                out.append(mk(4 + jj, d.at[plin], d.at[plin], sib))
            elif group == 'in_d2d':
                out.append(mk(4 + jj, d.at[plin_other], d.at[plin_other], sib))
    return out


def _gather_start(srcs, dsts, sems):
    for cp in _gather_copies(srcs, dsts, sems, 'local') + _gather_copies(srcs, dsts, sems, 'own'):
        cp.start()


def _gather_pass_on(srcs, dsts, sems):
    for cp in _gather_copies(srcs, dsts, sems, 'in_ici'):
        cp.wait_recv()
    for cp in _gather_copies(srcs, dsts, sems, 'pass_on'):
        cp.start()


def _gather_finish(srcs, dsts, sems):
    for cp in _gather_copies(srcs, dsts, sems, 'in_d2d'):
        cp.wait_recv()
    for cp in _gather_copies(srcs, dsts, sems, 'own') + _gather_copies(srcs, dsts, sems, 'pass_on'):
        cp.wait_send()
    for cp in _gather_copies(srcs, dsts, sems, 'local'):
        cp.wait()


def _xchg_out_shapes(srcs, scatter):
    return [jax.ShapeDtypeStruct(s.shape if scatter else (N_DEV,) + s.shape, s.dtype) for s in srcs]


_ANY = pl.BlockSpec(memory_space=pl.ANY)


def _exchange(name, srcs, scatter):
    n = len(srcs)

    def body(*refs):
        s, d, sems = refs[:n], refs[n:2 * n], refs[2 * n:]
        if scatter:
            _scatter_start(s, d, sems)
            _scatter_wait(s, d, sems)
        else:
            _gather_start(s, d, sems)
            _gather_pass_on(s, d, sems)
            _gather_finish(s, d, sems)

    return pl.pallas_call(body, name=name, in_specs=[_ANY] * n, out_specs=[_ANY] * n,
                          out_shape=_xchg_out_shapes(srcs, scatter), scratch_shapes=_xchg_sems(n))(*srcs)


_MM_DIMS = {'nn': (((1,), (0,)), ((), ())), 'nt': (((1,), (1,)), ((), ())), 'tn': (((0,), (0,)), ((), ()))}


def _cmm_raw(x, y, kind, split):
    dot = functools.partial(lax.dot_general, dimension_numbers=_MM_DIMS[kind], preferred_element_type=F32)
    xh, yh = x.astype(BF16), y.astype(BF16)
    out = dot(xh, yh)
    if split:
        xl = (x - xh.astype(F32)).astype(BF16)
        yl = (y - yh.astype(F32)).astype(BF16)
        out = out + (dot(xh, yl) + dot(xl, yh))
    return out


@functools.partial(jax.custom_vjp, nondiff_argnums=(2, 3))
def _cmm(x, y, kind, split=False):
    return _cmm_raw(x, y, kind, split)


def _cmm_fwd(x, y, kind, split):
    return _cmm_raw(x, y, kind, split), (x, y)


def _cmm_bwd(kind, split, res, g):
    x, y = res
    if kind == 'nn':
        return _cmm_raw(g, y, 'nt', split), _cmm_raw(x, g, 'tn', split)
    if kind == 'nt':
        return _cmm_raw(g, y, 'nn', split), _cmm_raw(g, x, 'tn', split)
    return _cmm_raw(y, g, 'nt', split), _cmm_raw(x, g, 'nn', split)


_cmm.defvjp(_cmm_fwd, _cmm_bwd)


def _tri_sum_raw(tri, x, kind):
    dot = functools.partial(lax.dot_general, dimension_numbers=_MM_DIMS[kind], preferred_element_type=F32)
    tb = tri.astype(BF16)
    hi, mid, lo = _split3(x)
    return (dot(tb, hi) + dot(tb, mid)) + dot(tb, lo)


@functools.partial(jax.custom_vjp, nondiff_argnums=(2,))
def _tri_sum(tri, x, kind):
    return _tri_sum_raw(tri, x, kind)


def _tri_sum_fwd(tri, x, kind):
    return _tri_sum_raw(tri, x, kind), tri


def _tri_sum_bwd(kind, tri, g):
    return jnp.zeros_like(tri), _tri_sum_raw(tri, g, 'tn' if kind == 'nn' else 'nn')


_tri_sum.defvjp(_tri_sum_fwd, _tri_sum_bwd)


def _chunk_fn(S0, r, lw, k, v, a, b):
    hs = range(len(r))
    C = r[0].shape[0]
    ii = lax.broadcasted_iota(jnp.int32, (C, C), 0)
    jj = lax.broadcasted_iota(jnp.int32, (C, C), 1)
    incl, strict = ii >= jj, ii > jj
    eye = (ii == jj).astype(F32)
    inclf = incl.astype(F32)
    cum = [_tri_sum(inclf, lw[h], 'nn') for h in hs]
    e_inv = [jnp.exp(-cum[h]) for h in hs]
    At = [a[h] * jnp.exp(cum[h] - lw[h]) for h in hs]
    Rt = [r[h] * jnp.exp(cum[h]) for h in hs]
    Kh = [k[h] * e_inv[h] for h in hs]
    Bh = [b[h] * e_inv[h] for h in hs]
    Mab = [jnp.where(strict, _cmm(At[h], Bh[h], 'nt', True), 0.0) for h in hs]
    Mak = [jnp.where(strict, _cmm(At[h], Kh[h], 'nt', True), 0.0) for h in hs]
    Mrk = [jnp.where(incl, _cmm(Rt[h], Kh[h], 'nt', True), 0.0) for h in hs]
    Mrb = [jnp.where(incl, _cmm(Rt[h], Bh[h], 'nt', True), 0.0) for h in hs]
    rhs = [_cmm(At[h], S0[h], 'nt') + _cmm(Mak[h], v[h], 'nn') for h in hs]
    P = Mab
    Tm = [eye + P[h] for h in hs]
    n = 1
    while 2 * n < C:
        P = [_cmm(P[h], P[h], 'nn', True) for h in hs]
        Tm = [_cmm(Tm[h], eye + P[h], 'nn', True) for h in hs]
        n *= 2
    U = [_cmm(Tm[h], rhs[h], 'nn', True) for h in hs]
    Y = [_cmm(Rt[h], S0[h], 'nt') + _cmm(Mrk[h], v[h], 'nn') + _cmm(Mrb[h], U[h], 'nn') for h in hs]
    gC = [jnp.exp(jnp.sum(lw[h], axis=0, keepdims=True)) for h in hs]
    SC = [S0[h] * gC[h] + _cmm(v[h], Kh[h] * gC[h], 'tn') + _cmm(U[h], Bh[h] * gC[h], 'tn') for h in hs]
    return tuple(Y), tuple(SC)


def _cscan_fwd(r, lw, k, v, a, b, xs):
    n_x = len(xs)
    tn = r.shape[0]
    H, Dh, Dv = RWKV_HEADS, RWKV_HEAD_DIM, RWKV_HEAD_DIM
    nc = tn // SCAN_CHUNK
    lanes = lambda h: slice(h * Dh, (h + 1) * Dh)
    heads = lambda ref: tuple(ref[:, lanes(h)] for h in range(H))
    mats = lambda ref: tuple(ref[h] for h in range(H))

    def body(r_ref, lw_ref, k_ref, v_ref, a_ref, b_ref, *rest):
        x_src, (y_ref, ck_ref) = rest[:n_x], rest[n_x:n_x + 2]
        x_dst, s_ref, sems = rest[n_x + 2:2 * n_x + 2], rest[2 * n_x + 2], rest[2 * n_x + 3:]

        @pl.when(pl.program_id(0) == 0)
        def _():
            s_ref[...] = jnp.zeros(s_ref.shape, F32)
            _gather_start(x_src, x_dst, sems)

        ck_ref[0] = s_ref[...]
        y, sc = _chunk_fn(mats(s_ref), heads(r_ref), heads(lw_ref), heads(k_ref), heads(v_ref), heads(a_ref),
                          heads(b_ref))
        for h in range(H):
            y_ref[:, lanes(h)] = y[h]
            s_ref[h] = sc[h]

        @pl.when(pl.program_id(0) == max(nc - 4, 0))
        def _():
            _gather_pass_on(x_src, x_dst, sems)

        @pl.when(pl.program_id(0) == nc - 1)
        def _():
            _gather_finish(x_src, x_dst, sems)

    hm = pl.BlockSpec((SCAN_CHUNK, H * Dh), lambda c: (c, 0))
    res = pl.pallas_call(
        body, name="rwkv_scan_fwd", grid=(nc,), in_specs=[hm] * 6 + [_ANY] * n_x,
        out_specs=[hm, pl.BlockSpec((1, H, Dv, Dh), lambda c: (c, 0, 0, 0))] + [_ANY] * n_x,
        out_shape=[jax.ShapeDtypeStruct((tn, H * Dh), F32), jax.ShapeDtypeStruct((nc, H, Dv, Dh), F32)]
        + _xchg_out_shapes(xs, False),
        scratch_shapes=[pltpu.VMEM((H, Dv, Dh), F32)] + _xchg_sems(n_x),
        compiler_params=_cparams(1))(r, lw, k, v, a, b, *xs)
    return res[0], res[1], res[2:]


def _cscan_bwd(r, lw, k, v, a, b, dy, ck, xs):
    n_x = len(xs)
    tn = r.shape[0]
    H, Dh, Dv = RWKV_HEADS, RWKV_HEAD_DIM, RWKV_HEAD_DIM
    nc = tn // SCAN_CHUNK
    lanes = lambda h: slice(h * Dh, (h + 1) * Dh)
    heads = lambda ref: tuple(ref[:, lanes(h)] for h in range(H))
    mats = lambda ref: tuple(ref[h] for h in range(H))

    def body(r_ref, lw_ref, k_ref, v_ref, a_ref, b_ref, dy_ref, ck_ref, *rest):
        x_src = rest[:n_x]
        d_refs = rest[n_x:n_x + 6]
        x_dst = rest[n_x + 6:2 * n_x + 6]
        g_ref = rest[2 * n_x + 6]
        sems = rest[2 * n_x + 7:]

        @pl.when(pl.program_id(0) == 0)
        def _():
            g_ref[...] = jnp.zeros(g_ref.shape, F32)
            _scatter_start(x_src, x_dst, sems)

        s0 = tuple(ck_ref[0, h] for h in range(H))
        _, vjp = jax.vjp(_chunk_fn, s0, heads(r_ref), heads(lw_ref), heads(k_ref), heads(v_ref), heads(a_ref),
                         heads(b_ref))
        grads = vjp((heads(dy_ref), mats(g_ref)))
        for h in range(H):
            g_ref[h] = grads[0][h]
            for d_ref, gz in zip(d_refs, grads[1:]):
                d_ref[:, lanes(h)] = gz[h]

        @pl.when(pl.program_id(0) == nc - 1)
        def _():
            _scatter_wait(x_src, x_dst, sems)

    hm = pl.BlockSpec((SCAN_CHUNK, H * Dh), lambda c: (nc - 1 - c, 0))
    hshape = jax.ShapeDtypeStruct((tn, H * Dh), F32)
    res = pl.pallas_call(
        body, name="rwkv_scan_bwd", grid=(nc,),
        in_specs=[hm] * 7 + [pl.BlockSpec((1, H, Dv, Dh), lambda c: (nc - 1 - c, 0, 0, 0))] + [_ANY] * n_x,
        out_specs=[hm] * 6 + [_ANY] * n_x, out_shape=[hshape] * 6 + _xchg_out_shapes(xs, True),
        scratch_shapes=[pltpu.VMEM((H, Dv, Dh), F32)] + _xchg_sems(n_x),
        compiler_params=_cparams(1))(r, lw, k, v, a, b, dy, ck, *xs)
    return res[:6], res[6:]


def _decay_mask(lg, i, j, blk):
    rows = lax.broadcasted_iota(jnp.int32, (blk, blk), 0)
    cols = lax.broadcasted_iota(jnp.int32, (blk, blk), 1)
    dd = (rows - cols + (i - j) * blk).astype(F32)
    return jnp.where(dd >= 0.0, jnp.exp(lg * jnp.maximum(dd, 0.0)), 0.0)


_NT = (((1,), (1,)), ((), ()))
_TN = (((0,), (0,)), ((), ()))


def _ret_attn_fwd(lg, q, k, v, v_col0=0, blk=ATT_BLOCK):
    tn = q.shape[0]
    Dh = RET_HEAD_DIM

    def body(lg_ref, q_ref, k_ref, v_ref, o_ref):
        i = pl.program_id(1)
        lgv = lg_ref[0][:, 0:1]
        qb = q_ref[...].astype(BF16)

        def jb(j, acc):
            ks = pl.ds(pl.multiple_of(j * blk, blk), blk)
            s = lax.dot_general(qb, k_ref[ks, :].astype(BF16), _NT, preferred_element_type=F32)
            s = s * _decay_mask(lgv, i, j, blk)
            return acc + jnp.dot(s.astype(BF16), v_ref[ks, :].astype(BF16), preferred_element_type=F32)

        o_ref[...] = lax.fori_loop(0, i + 1, jb, jnp.zeros((blk, Dh), F32))

    full = pl.BlockSpec((tn, Dh), lambda h, i: (0, h))
    qs = pl.BlockSpec((blk, Dh), lambda h, i: (i, h))
    return pl.pallas_call(
        body, name="ret_attn_fwd", grid=(RET_HEADS, tn // blk),
        in_specs=[pl.BlockSpec((1, 1, 128), lambda h, i: (h, 0, 0)), qs, full,
                  pl.BlockSpec((tn, Dh), lambda h, i: (0, v_col0 + h))],
        out_specs=qs, out_shape=jax.ShapeDtypeStruct(q.shape, F32), compiler_params=_cparams(2))(lg, q, k, v)


def _ret_attn_bwd(lg, q, k, v, do, v_col0=0, blk=ATT_BLOCK):
    tn = q.shape[0]
    nb = tn // blk
    Dh = RET_HEAD_DIM

    def body(lg_ref, q_ref, k_ref, v_ref, do_ref, dq_ref, dk_ref, dv_ref):
        lgv = lg_ref[0][:, 0:1]
        dk_ref[...] = jnp.zeros(dk_ref.shape, F32)
        dv_ref[...] = jnp.zeros(dv_ref.shape, F32)

        def ib(i, carry):
            qs = pl.ds(pl.multiple_of(i * blk, blk), blk)
            qb = q_ref[qs, :].astype(BF16)
            dob = do_ref[qs, :].astype(BF16)

            def jb(j, dq):
                ks = pl.ds(pl.multiple_of(j * blk, blk), blk)
                kb = k_ref[ks, :].astype(BF16)
                vb = v_ref[ks, :].astype(BF16)
                dm = _decay_mask(lgv, i, j, blk)
                s = lax.dot_general(qb, kb, _NT, preferred_element_type=F32) * dm
                ds = lax.dot_general(dob, vb, _NT, preferred_element_type=F32) * dm
                sb, dsb = s.astype(BF16), ds.astype(BF16)
                dv_ref[ks, :] += lax.dot_general(sb, dob, _TN, preferred_element_type=F32)
                dk_ref[ks, :] += lax.dot_general(dsb, qb, _TN, preferred_element_type=F32)
                return dq + jnp.dot(dsb, kb, preferred_element_type=F32)

            dq_ref[qs, :] = lax.fori_loop(0, i + 1, jb, jnp.zeros((blk, Dh), F32))
            return carry

        lax.fori_loop(0, nb, ib, 0)

    full = pl.BlockSpec((tn, Dh), lambda h: (0, h))
    sh = jax.ShapeDtypeStruct(q.shape, F32)
    return pl.pallas_call(
        body, name="ret_attn_bwd", grid=(RET_HEADS,),
        in_specs=[pl.BlockSpec((1, 1, 128), lambda h: (h, 0, 0)), full, full,
                  pl.BlockSpec((tn, Dh), lambda h: (0, v_col0 + h)), full],
        out_specs=[full, full, full], out_shape=[sh, sh, sh], compiler_params=_cparams(1))(lg, q, k, v, do)


def _next8_spec(a, tb):
    r = tb // 8
    last = a.shape[0] // 8 - 1
    return pl.BlockSpec((8, a.shape[1]), lambda i: (jnp.minimum((i + 1) * r, last), 0))


def _conv_taps(g_ext, cw_ref, cb_ref):
    return (cw_ref[2:3, :] * g_ext + cw_ref[1:2, :] * pltpu.roll(g_ext, 1, 0)
            + cw_ref[0:1, :] * pltpu.roll(g_ext, 2, 0) + cb_ref[...])


def _glu_fwd(gate, up, cw, cb, tb=TOK_BLOCK):
    tn = gate.shape[0]

    def body(g_ref, gh_ref, u_ref, cw_ref, cb_ref, o_ref):
        halo = jnp.where(pl.program_id(0) == 0, 0.0, gh_ref[...])
        g_ext = jnp.concatenate([halo, g_ref[...]], axis=0)
        gc = _conv_taps(g_ext, cw_ref, cb_ref)[8:, :]
        o_ref[...] = (gc * _sigmoid(gc) * u_ref[...]).astype(o_ref.dtype)

    return pl.pallas_call(
        body, name="glu_fwd", grid=(tn // tb,),
        in_specs=[_blk_spec(gate, tb), _prev8_spec(gate, tb), _blk_spec(up, tb), _full_spec(cw), _full_spec(cb)],
        out_specs=_blk_spec(gate, tb), out_shape=jax.ShapeDtypeStruct(gate.shape, BF16),
        compiler_params=_cparams(1))(gate, gate, up, cw, cb)


def _glu_bwd(gate, up, dact, cw, cb, tb=TOK_BLOCK):
    tn = gate.shape[0]
    nb = tn // tb

    def body(g_ref, gp_ref, gn_ref, u_ref, un_ref, d_ref, dn_ref, cw_ref, cb_ref, dg_ref, du_ref, dcw_ref, dcb_ref):
        i = pl.program_id(0)
        gprev = jnp.where(i == 0, 0.0, gp_ref[...])
        dnext = jnp.where(i == nb - 1, 0.0, dn_ref[...])
        g_ext = jnp.concatenate([gprev, g_ref[...], gn_ref[...]], axis=0)
        gc = _conv_taps(g_ext, cw_ref, cb_ref)[8:, :]
        u_e = jnp.concatenate([u_ref[...], un_ref[...]], axis=0)
        d_e = jnp.concatenate([d_ref[...], dnext], axis=0)
        s = _sigmoid(gc)
        dgc = d_e * u_e * (s * (1.0 + gc * (1.0 - s)))
        du_ref[...] = (d_ref[...] * (gc * s)[:tb, :]).astype(du_ref.dtype)
        n_e = tb + 8
        dg_ref[...] = (cw_ref[2:3, :] * dgc + cw_ref[1:2, :] * pltpu.roll(dgc, n_e - 1, 0)
                       + cw_ref[0:1, :] * pltpu.roll(dgc, n_e - 2, 0))[:tb, :].astype(dg_ref.dtype)

        @pl.when(i == 0)
        def _():
            dcw_ref[...] = jnp.zeros(dcw_ref.shape, F32)
            dcb_ref[...] = jnp.zeros(dcb_ref.shape, F32)

        dgc_b = dgc[:tb, :]
        g0 = g_ext[8:8 + tb, :]
        g1 = pltpu.roll(g_ext, 1, 0)[8:8 + tb, :]
        g2 = pltpu.roll(g_ext, 2, 0)[8:8 + tb, :]
        dcw_ref[2:3, :] += jnp.sum(dgc_b * g0, axis=0, keepdims=True)
        dcw_ref[1:2, :] += jnp.sum(dgc_b * g1, axis=0, keepdims=True)
        dcw_ref[0:1, :] += jnp.sum(dgc_b * g2, axis=0, keepdims=True)
        dcb_ref[...] += jnp.sum(dgc_b, axis=0, keepdims=True)

    sh = jax.ShapeDtypeStruct(gate.shape, BF16)
    return pl.pallas_call(
        body, name="glu_bwd", grid=(nb,),
        in_specs=[_blk_spec(gate, tb), _prev8_spec(gate, tb), _next8_spec(gate, tb), _blk_spec(up, tb),
                  _next8_spec(up, tb), _blk_spec(dact, tb), _next8_spec(dact, tb), _full_spec(cw), _full_spec(cb)],
        out_specs=[_blk_spec(gate, tb), _blk_spec(gate, tb), _full_spec(cw), _full_spec(cb)],
        out_shape=[sh, sh, jax.ShapeDtypeStruct(cw.shape, F32), jax.ShapeDtypeStruct(cb.shape, F32)],
        compiler_params=_cparams(1))(gate, gate, gate, up, up, dact, dact, cw, cb)


def _final_loss(x2, tgt, g, tb=TOK_BLOCK):
    tn, dm = x2.shape

    def body(x_ref, t_ref, g_ref, l_ref, dx_ref, dg_ref):
        y, vjp = jax.vjp(_rms_fn, x_ref[...], g_ref[...])
        err = y - t_ref[...]
        dx, dg = vjp(err * (1.0 / dm))

        @pl.when(pl.program_id(0) == 0)
        def _():
            l_ref[...] = jnp.zeros(l_ref.shape, F32)
            dg_ref[...] = jnp.zeros(dg_ref.shape, F32)

        part = 0.5 * jnp.sum(jnp.mean(err * err, axis=-1, keepdims=True), axis=0, keepdims=True)
        l_ref[...] += jnp.broadcast_to(part, l_ref.shape)
        dx_ref[...] = dx
        dg_ref[...] += dg

    return pl.pallas_call(
        body, name="final_loss", grid=(tn // tb,),
        in_specs=[_blk_spec(x2, tb), _blk_spec(tgt, tb), _full_spec(g)],
        out_specs=[pl.BlockSpec((8, 128), lambda i: (0, 0)), _blk_spec(x2, tb), _full_spec(g)],
        out_shape=[jax.ShapeDtypeStruct((8, 128), F32), jax.ShapeDtypeStruct(x2.shape, F32),
                   jax.ShapeDtypeStruct(g.shape, F32)],
        compiler_params=_cparams(1))(x2, tgt, g)


def _pad_cols(w, n):
    return jnp.pad(w, ((0, 0), (0, n - w.shape[1])))


def _pad_rows(w, n):
    return jnp.pad(w, ((0, n - w.shape[0]), (0, 0)))


def _local_step(x, tgt, W, late):
    tn = x.shape[0]
    Wd = RWKV_WIDTH
    row = lambda z: z.reshape(1, -1)
    g_mix, g_ffn, g_fin = row(W['norm_mix_g']), row(W['norm_ffn_g']), row(W['norm_final_g'])

    (h1,) = _tok_fwd("norm_mix_fwd", lambda a, g: (_rms_fn(a, g),), [x], [g_mix], [(D_MODEL,)])
    proj = _mm("proj_fwd", h1, W['w_in_t'], tb=True)
    p_rkv = _Cols(proj, 3 * Wd, 0)
    pre_consts = [row(W['rwkv_mu_w']), row(W['rwkv_mu_a']), row(W['rwkv_mu_g']), row(W['rwkv_mu_r']),
                  row(W['rwkv_mu_k']), row(W['rwkv_mu_v']), row(W['rwkv_w0']),
                  _pad_cols(W['rwkv_w1'], LORA_PAD), _pad_rows(W['rwkv_w2'], LORA_PAD), row(W['rwkv_a0']),
                  _pad_cols(W['rwkv_a1'], LORA_PAD), _pad_rows(W['rwkv_a2'], LORA_PAD),
                  W['rwkv_g1'], W['rwkv_g2'], row(W['rwkv_k_k']), row(W['rwkv_k_a'])]
    r, k, v, lw, nkk, b, g = _pre_a_fwd(h1, p_rkv, pre_consts)
    y_scan, ck, gathered = _cscan_fwd(r, lw, k, v, nkk, b, late)
    w_out, w_gate_t, w_up_t, w_down = [g_.reshape(-1, D_MODEL) for g_ in gathered]
    post_consts = [row(W['rwkv_lnx_w']), row(W['rwkv_lnx_b']), row(W['rwkv_r_k'])]
    (y_rwkv,) = _tok_fwd("rwkv_post_fwd", _rwkv_post_fn, [y_scan, r, k, v, g], post_consts, [(Wd,)],
                         out_dtypes=[BF16])

    pos = jnp.arange(tn, dtype=F32)
    half = RET_HEAD_DIM // 2
    inv_freq = ROPE_BASE ** (-jnp.arange(half, dtype=F32) / half)
    ang = pos[:, None] * inv_freq[None, :]
    cos2 = jnp.concatenate([jnp.cos(ang), jnp.cos(ang)], axis=1)
    sin2 = jnp.concatenate([-jnp.sin(ang), jnp.sin(ang)], axis=1)
    lg = jnp.log(1.0 - 2.0 ** (-5.0 - jnp.arange(RET_HEADS, dtype=F32)))
    lg = jnp.broadcast_to(lg[:, None, None], (RET_HEADS, 1, 128))
    q_p, k_p, g_ret = _Cols(proj, Wd, 3), _Cols(proj, Wd, 4), _Cols(proj, Wd, 6)
    v_col0 = 5 * Wd // RET_HEAD_DIM
    q_rot, k_rot = _tok_fwd("ret_rotary_fwd", _rotary_fn, [cos2, sin2, q_p, k_p], [], [(RET_WIDTH,)] * 2)
    y_ret_raw = _ret_attn_fwd(lg, q_rot, k_rot, proj, v_col0)
    gn_w = row(W['ret_gn_w'])
    (y_ret,) = _tok_fwd("ret_post_fwd", _ret_post_fn, [y_ret_raw, g_ret], [gn_w], [(RET_WIDTH,)],
                        out_dtypes=[BF16])

    ycat = jnp.concatenate([y_rwkv, y_ret], axis=1)
    x1 = _mm("out_proj_fwd", ycat, w_out, add=x)
    (h2,) = _tok_fwd("norm_ffn_fwd", lambda a_, g_: (_rms_fn(a_, g_),), [x1], [g_ffn], [(D_MODEL,)],
                     out_dtypes=[BF16])
    gate = _mm("ffn_gate_fwd", h2, w_gate_t, tb=True)
    up = _mm("ffn_up_fwd", h2, w_up_t, tb=True)
    cw = W['ffn_conv_w']
    cb = row(W['ffn_conv_b'])
    act = _glu_fwd(gate, up, cw, cb)
    x2 = _mm("ffn_down_fwd", act, w_down, add=x1)
    loss8, dx2, dg_fin = _final_loss(x2, tgt, g_fin)

    G = {'norm_final_g': dg_fin}
    dact = _mm("ffn_down_dx", dx2, w_down, tb=True)
    d_down = _mm("ffn_down_dw", act, dx2, ta=True, out_dtype=BF16)
    dgate, dup, dcw, dcb = _glu_bwd(gate, up, dact, cw, cb)
    G['ffn_conv_w'], G['ffn_conv_b'] = dcw, dcb
    dh2 = _mm("ffn_gate_dx", dgate, w_gate_t)
    dh2 = _mm("ffn_up_dx", dup, w_up_t, add=dh2)
    d_gate_t = _mm("ffn_gate_dw", dgate, h2, ta=True, out_dtype=BF16)
    d_up_t = _mm("ffn_up_dw", dup, h2, ta=True, out_dtype=BF16)
    dx1, G['norm_ffn_g'] = _tok_bwd("norm_ffn_bwd", lambda a_, g_: (_rms_fn(a_, g_),), [], [x1], [g_ffn], [dh2], add=dx2)
    dycat = _mm("out_proj_dx", dx1, w_out, tb=True)
    d_out = _mm("out_proj_dw", ycat, dx1, ta=True, out_dtype=BF16)
    late_grads = [z.reshape(N_DEV, -1, D_MODEL) for z in (d_out, d_gate_t, d_up_t, d_down)]
    dy_rwkv, dy_ret = _Cols(dycat, Wd, 0), _Cols(dycat, Wd, 1)

    dyr_raw, dg_ret, G['ret_gn_w'] = _tok_bwd("ret_post_bwd", _ret_post_fn, [], [y_ret_raw, g_ret], [gn_w], [dy_ret],
                                              tok_dtypes=[F32, BF16])
    dq_rot, dk_rot, dv_ret = _ret_attn_bwd(lg, q_rot, k_rot, proj, dyr_raw, v_col0)
    dq_p, dk_p = _tok_bwd("ret_rotary_bwd", _rotary_fn, [cos2, sin2], [q_p, k_p], [], [dq_rot, dk_rot],
                          tok_dtypes=[BF16, BF16])

    dy_scan, dr1, dk1, dv1, dg, G['rwkv_lnx_w'], G['rwkv_lnx_b'], G['rwkv_r_k'] = _tok_bwd(
        "rwkv_post_bwd", _rwkv_post_fn, [], [y_scan, r, k, v, g], post_consts, [dy_rwkv])
    (dr2, dlw, dk2, dv2, dnkk, db), late_parts = _cscan_bwd(r, lw, k, v, nkk, b, dy_scan, ck, late_grads)
    pre_cts = [(dr1, dr2), (dk1, dk2), (dv1, dv2), dlw, dnkk, db, dg]
    pre_out = _pre_a_bwd(h1, p_rkv, pre_consts, pre_cts)
    dh1_a, dp_rkv = pre_out[0], pre_out[1]
    (G['rwkv_mu_w'], G['rwkv_mu_a'], G['rwkv_mu_g'], G['rwkv_mu_r'], G['rwkv_mu_k'], G['rwkv_mu_v'], G['rwkv_w0'],
     dw1, dw2, G['rwkv_a0'], da1, da2, G['rwkv_g1'], G['rwkv_g2'], G['rwkv_k_k'], G['rwkv_k_a']) = pre_out[2:]
    G['rwkv_w1'], G['rwkv_w2'] = dw1[:, :64], dw2[:64, :]
    G['rwkv_a1'], G['rwkv_a2'] = da1[:, :64], da2[:64, :]

    dproj = jnp.concatenate([dp_rkv, dq_p, dk_p, dv_ret.astype(BF16), dg_ret], axis=1)
    dh1 = _mm("proj_dx", dproj, W['w_in_t'], add=dh1_a)
    G['w_in_t'] = _mm("proj_dw", dproj, h1, ta=True, out_dtype=BF16)
    dx, G['norm_mix_g'] = _tok_bwd("norm_mix_bwd", lambda a_, g_: (_rms_fn(a_, g_),), [], [x], [g_mix], [dh1], add=dx1)
    return loss8[0, 0], dx, G, late_parts


def _adamw_block(p_ref, w_ref, m_ref, v_ref, g_ref, d_ref, nm_ref, nv_ref):
    g = p_ref[0].astype(F32)
    for d in range(1, N_DEV):
        g = g + p_ref[d].astype(F32)
    mn = ADAM_B1 * m_ref[...] + (1.0 - ADAM_B1) * g
    vn = ADAM_B2 * v_ref[...] + (1.0 - ADAM_B2) * (g * g)
    m_hat = mn / (1.0 - ADAM_B1 ** ADAM_STEP)
    v_hat = vn / (1.0 - ADAM_B2 ** ADAM_STEP)
    g_ref[...] = g
    d_ref[...] = -ADAM_LR * (m_hat / (jnp.sqrt(v_hat) + ADAM_EPS) + ADAM_WD * w_ref[...])
    nm_ref[...] = mn
    nv_ref[...] = vn


def _adamw_late(items, xs, tb=32):
    n_it, n_x = len(items), len(xs)
    nbs = [it[1].shape[0] // tb for it in items]
    steps = max(nbs)
    cols = items[0][1].shape[1]

    def body(*refs):
        ins, x_src = refs[:4 * n_it], refs[4 * n_it:4 * n_it + n_x]
        outs = refs[4 * n_it + n_x:8 * n_it + n_x]
        x_dst, sems = refs[8 * n_it + n_x:8 * n_it + 2 * n_x], refs[8 * n_it + 2 * n_x:]
        i = pl.program_id(0)

        @pl.when(i == 0)
        def _():
            _scatter_start(x_src, x_dst, sems)

        for j in range(n_it):
            @pl.when(i < nbs[j])
            def _(j=j):
                _adamw_block(*ins[4 * j:4 * j + 4], *outs[4 * j:4 * j + 4])

        @pl.when(i == steps - 1)
        def _():
            _scatter_wait(x_src, x_dst, sems)

    in_specs, out_specs, out_shape, flat = [], [], [], []
    for (parts, w, m, v), nb in zip(items, nbs):
        blk = pl.BlockSpec((tb, cols), lambda i, nb=nb: (jnp.minimum(i, nb - 1), 0))
        in_specs += [pl.BlockSpec((N_DEV, tb, cols), lambda i, nb=nb: (0, jnp.minimum(i, nb - 1), 0)), blk, blk, blk]
        out_specs += [blk] * 4
        out_shape += [jax.ShapeDtypeStruct(w.shape, F32)] * 4
        flat += [parts, w, m, v]
    res = pl.pallas_call(
        body, name="adamw_late", grid=(steps,), in_specs=in_specs + [_ANY] * n_x, out_specs=out_specs + [_ANY] * n_x,
        out_shape=out_shape + _xchg_out_shapes(xs, True), scratch_shapes=_xchg_sems(n_x),
        compiler_params=_cparams(1))(*flat, *xs)
    return [res[4 * j:4 * j + 4] for j in range(n_it)], res[4 * n_it:]


def _adamw(name, parts, w, m, v):
    rows, cols = w.shape
    sub = 8 * 4 // parts.dtype.itemsize
    tb = max(t for t in range(sub, 65, sub) if rows % t == 0) if rows > 64 else rows
    body = functools.partial(_adamw_block)
    spec = pl.BlockSpec((tb, cols), lambda i: (i, 0))
    sh = jax.ShapeDtypeStruct((rows, cols), F32)
    return pl.pallas_call(
        body, name=name, grid=(rows // tb,),
        in_specs=[pl.BlockSpec((N_DEV, tb, cols), lambda i: (0, i, 0)), spec, spec, spec],
        out_specs=[spec] * 4, out_shape=[sh] * 4, compiler_params=_cparams(1))(parts, w, m, v)


def _local_shape(name):
    gs, ax = SHARDED[name]
    ls = list(gs)
    ls[ax] //= N_DEV
    return tuple(ls)


def _seg(flat, seg):
    n = flat.shape[-1]
    pad = _round_up(n, seg) - n
    if pad:
        flat = jnp.pad(flat, [(0, 0)] * (flat.ndim - 1) + [(0, pad)])
    return flat


def _split3(w):
    hi = w.astype(BF16)
    r1 = w - hi.astype(F32)
    mid = r1.astype(BF16)
    lo = (r1 - mid.astype(F32)).astype(BF16)
    return hi, mid, lo


def _pack_small_shards(shards):
    pieces = []
    for name in SMALL_NAMES:
        flat = shards[name].reshape(-1)
        if name == 'ffn_conv_w':
            pieces += [_seg(p, BF16_SEG) for p in _split3(flat)]
        else:
            pieces.append(flat.astype(BF16))
    return jnp.concatenate(pieces).reshape(-1, 128)


def _unpack_small(gathered):
    flat = gathered.reshape(N_DEV, -1)
    out, off = {}, 0
    for name in SMALL_NAMES:
        gs, ax = SHARDED[name]
        ls = _local_shape(name)
        n = int(np.prod(ls))
        if name == 'ffn_conv_w':
            nseg = _round_up(n, BF16_SEG)
            hi, mid, lo = (flat[:, off + j * nseg: off + j * nseg + n].astype(F32) for j in range(3))
            sh = ((hi + mid) + lo).reshape(N_DEV, 3, -1)
            out[name] = jnp.swapaxes(sh, 0, 1).reshape(3, D_FF)
            off += 3 * nseg
        else:
            sh = flat[:, off:off + n].reshape((N_DEV,) + ls[1:])
            out[name] = sh.reshape(gs[1:]) if ax == 1 else jnp.swapaxes(sh, 0, 1).reshape(gs[1:])
            off += n
    return out


def _small_pieces(sharded, repl):
    return [sharded[n].reshape(-1) for n in SMALL_NAMES] + [repl[n].reshape(-1) for n in REPL_NAMES]


def _pack_small_local(d):
    flat = jnp.concatenate(_small_pieces(d, d))
    return _seg(flat, F32_SEG).reshape(-1, 128)


def _pack_small_grads(G):
    pieces = []
    for name in SMALL_NAMES:
        gs, ax = SHARDED[name]
        g = G[name]
        if name == 'ffn_conv_w':
            sh = jnp.swapaxes(g.reshape(3, N_DEV, -1), 0, 1)
        elif ax == 1:
            sh = g
        else:
            sh = jnp.swapaxes(g.reshape(g.shape[0], N_DEV, -1), 0, 1)
        pieces.append(sh.reshape(N_DEV, -1))
    rep = jnp.concatenate([G[n].reshape(-1) for n in REPL_NAMES])
    pieces.append(jnp.broadcast_to(rep[None, :], (N_DEV, rep.shape[0])))
    flat = _seg(jnp.concatenate(pieces, axis=1), F32_SEG)
    return flat.reshape(N_DEV, -1, 128)


def _unpack_small_local(packed, local_shapes):
    flat = packed.reshape(-1)
    out, off = {}, 0
    for name in SMALL_NAMES + REPL_NAMES:
        n = int(np.prod(local_shapes[name]))
        out[name] = flat[off:off + n].reshape(local_shapes[name])
        off += n
    return out


def kernel(x, *rest):
    nw = len(WEIGHT_NAMES)
    assert len(rest) == 3 * nw + 1
    weights = dict(zip(WEIGHT_NAMES, rest[:nw]))
    loss_target = rest[nw]
    moms = dict(zip(WEIGHT_NAMES, rest[nw + 1:2 * nw + 1]))
    vars_ = dict(zip(WEIGHT_NAMES, rest[2 * nw + 1:]))
    local_shapes = {n: weights[n].shape for n in WEIGHT_NAMES}

    def native2d(name, a):
        a2 = a.reshape(a.shape[-2], a.shape[-1])
        return a2.T if name in BIG_T else a2

    def from2d(name, a2):
        return (a2.T if name in BIG_T else a2).reshape(local_shapes[name])

    big_w = {n: native2d(n, weights[n]) for n in BIG_NAMES}
    w_in_t_sh = big_w['w_in'].astype(BF16)
    late = [big_w[n].astype(BF16) for n in LATE_NAMES]
    small_sh = _pack_small_shards({n: weights[n] for n in SMALL_NAMES})
    w_in_g, small_g = _exchange("weights_all_gather", [w_in_t_sh, small_sh], False)
    W = _unpack_small(small_g)
    W['w_in_t'] = w_in_g.reshape(-1, D_MODEL)
    for n in REPL_NAMES:
        W[n] = weights[n][0] if n != 'norm_final_g' else weights[n]

    loss, dx, G, late_parts = _local_step(x[0], loss_target[0], W, late)

    late_items = [(parts, big_w[n], native2d(n, moms[n]), native2d(n, vars_[n]))
                  for n, parts in zip(LATE_NAMES, late_parts)]
    late_res, (w_in_parts, small_parts) = _adamw_late(
        late_items, [G['w_in_t'].reshape(N_DEV, -1, D_MODEL), _pack_small_grads(G)])
    results = {n: [from2d(n, r) for r in res] for n, res in zip(LATE_NAMES, late_res)}
    res = _adamw("adamw_w_in", w_in_parts, big_w['w_in'], native2d('w_in', moms['w_in']), native2d('w_in', vars_['w_in']))
    results['w_in'] = [from2d('w_in', r) for r in res]
    small_res = _adamw("adamw_small", small_parts, _pack_small_local(weights), _pack_small_local(moms),
                       _pack_small_local(vars_))
    small_out = [_unpack_small_local(p, local_shapes) for p in small_res]

    loss = lax.psum(loss, ("x", "y", "c"))
    outs = [loss, dx[None]]
    for j in range(4):
        outs += [results[n][j] if n in results else small_out[j][n] for n in WEIGHT_NAMES]
    return tuple(outs)
```

```python
import functools
import math

import numpy as np
import jax
import jax.numpy as jnp
from jax import lax
from jax.experimental import pallas as pl
from jax.experimental.pallas import tpu as pltpu

F32 = jnp.float32
BF16 = jnp.bfloat16

N_DEV = 8
D_MODEL = 1024
RWKV_HEADS = 8
RWKV_HEAD_DIM = 64
RWKV_WIDTH = 512
RET_HEADS = 4
RET_HEAD_DIM = 128
RET_WIDTH = 512
LORA_PAD = 128
D_FF = 2816
NORM_EPS = 1e-6
RWKV_GN_EPS = 64e-5
RET_GN_EPS = 1e-5
ROPE_BASE = 10000.0
ADAM_LR, ADAM_B1, ADAM_B2, ADAM_EPS, ADAM_WD, ADAM_STEP = 0.001, 0.9, 0.999, 1e-08, 0.01, 10

VMEM_LIMIT = 56 * 1024 * 1024
TOK_BLOCK = 256
LIGHT_TOK_BLOCK = 512
SCAN_CHUNK = 64
ATT_BLOCK = 512
BF16_SEG = 2048
F32_SEG = 1024

WEIGHT_NAMES = ['norm_mix_g', 'w_in', 'rwkv_mu_r', 'rwkv_mu_k', 'rwkv_mu_v', 'rwkv_mu_w', 'rwkv_mu_a',
                'rwkv_mu_g', 'rwkv_w0', 'rwkv_w1', 'rwkv_w2', 'rwkv_a0', 'rwkv_a1', 'rwkv_a2', 'rwkv_g1',
                'rwkv_g2', 'rwkv_k_k', 'rwkv_k_a', 'rwkv_r_k', 'rwkv_lnx_w', 'rwkv_lnx_b', 'ret_gn_w',
                'w_out', 'norm_ffn_g', 'ffn_w_gate', 'ffn_w_up', 'ffn_conv_w', 'ffn_conv_b', 'ffn_w_down',
                'norm_final_g']
SHARDED = {
    'w_in': ((1, 1024, 3584), 2), 'rwkv_w1': ((1, 1024, 64), 1), 'rwkv_w2': ((1, 64, 512), 2),
    'rwkv_a1': ((1, 1024, 64), 1), 'rwkv_a2': ((1, 64, 512), 2), 'rwkv_g1': ((1, 1024, 128), 1),
    'rwkv_g2': ((1, 128, 512), 2), 'w_out': ((1, 1024, 1024), 1), 'ffn_w_gate': ((1, 1024, 2816), 2),
    'ffn_w_up': ((1, 1024, 2816), 2), 'ffn_conv_w': ((1, 3, 1, 2816), 3), 'ffn_w_down': ((1, 2816, 1024), 1),
}
REPL_NAMES = [n for n in WEIGHT_NAMES if n not in SHARDED]
BIG_NAMES = ['w_in', 'w_out', 'ffn_w_gate', 'ffn_w_up', 'ffn_w_down']
BIG_T = ('w_in', 'ffn_w_gate', 'ffn_w_up')
LATE_NAMES = ['w_out', 'ffn_w_gate', 'ffn_w_up', 'ffn_w_down']
SMALL_NAMES = [n for n in WEIGHT_NAMES if n in SHARDED and n not in BIG_NAMES]


def _cparams(n_grid):
    return pltpu.CompilerParams(dimension_semantics=("arbitrary",) * n_grid, vmem_limit_bytes=VMEM_LIMIT)


def _round_up(n, m):
    return (n + m - 1) // m * m


@jax.custom_vjp
def _bdot(x, w):
    return jnp.dot(x.astype(BF16), w.astype(BF16), preferred_element_type=F32)


def _bdot_fwd(x, w):
    return _bdot(x, w), (x, w)


def _bdot_bwd(res, g):
    x, w = res
    gb = g.astype(BF16)
    dx = lax.dot_general(gb, w.astype(BF16), (((1,), (1,)), ((), ())), preferred_element_type=F32)
    dw = lax.dot_general(x.astype(BF16), gb, (((0,), (0,)), ((), ())), preferred_element_type=F32)
    return dx, dw.astype(w.dtype)


_bdot.defvjp(_bdot_fwd, _bdot_bwd)


@jax.custom_vjp
def _shift_rows(x, prev):
    rolled = pltpu.roll(x, 1, 0)
    row = lax.broadcasted_iota(jnp.int32, x.shape, 0)
    return jnp.where(row == 0, jnp.broadcast_to(prev, x.shape), rolled)


def _shift_rows_fwd(x, prev):
    return _shift_rows(x, prev), None


def _shift_rows_bwd(_, g):
    n = g.shape[0]
    rolled = pltpu.roll(g, n - 1, 0)
    row = lax.broadcasted_iota(jnp.int32, g.shape, 0)
    return jnp.where(row == n - 1, 0.0, rolled), g[0:1, :]


_shift_rows.defvjp(_shift_rows_fwd, _shift_rows_bwd)


@jax.custom_vjp
def _swap_halves(x):
    return pltpu.roll(x, 64, 1)


_swap_halves.defvjp(lambda x: (_swap_halves(x), None), lambda _, g: (pltpu.roll(g, 64, 1),))


def _sigmoid(x):
    return 1.0 / (1.0 + jnp.exp(-x))


def _softplus(x):
    return jnp.maximum(x, 0.0) + jnp.log(1.0 + jnp.exp(-jnp.abs(x)))


def _rms_fn(x, g):
    return x * lax.rsqrt(jnp.mean(x * x, axis=-1, keepdims=True) + NORM_EPS) * g


def _pre_a_fn(h1, h1p, p, pp, mu_w, mu_a, mu_g, mu_r, mu_k, mu_v, w0, w1, w2, a0, a1, a2, g1, g2, k_k, k_a):
    W = RWKV_WIDTH
    h1s = _shift_rows(h1, h1p)
    ps = _shift_rows(p, pp)
    dx = h1s - h1
    xw = h1 + dx * mu_w
    xa = h1 + dx * mu_a
    xg = h1 + dx * mu_g
    dp = ps - p
    r = p[:, 0:W] + dp[:, 0:W] * mu_r
    k0 = p[:, W:2 * W] + dp[:, W:2 * W] * mu_k
    v = p[:, 2 * W:3 * W] + dp[:, 2 * W:3 * W] * mu_v
    wl = w0 + _bdot(jnp.tanh(_bdot(xw, w1)), w2)
    w_log = -_softplus(-wl) - 0.5
    lw = -jnp.exp(w_log)
    a = _sigmoid(a0 + _bdot(_bdot(xa, a1), a2))
    g = _bdot(_sigmoid(_bdot(xg, g1)), g2)
    nkk, k, b = _pre_b_fn(k0, a, k_k, k_a)
    return r, k, v, lw, nkk, b, g


def _head_sum_raw(x):
    n = x.shape[1]
    ii = lax.broadcasted_iota(jnp.int32, (n, n), 0) // RWKV_HEAD_DIM
    jj = lax.broadcasted_iota(jnp.int32, (n, n), 1) // RWKV_HEAD_DIM
    ones = (ii == jj).astype(BF16)
    xh = x.astype(BF16)
    xl = (x - xh.astype(F32)).astype(BF16)
    return jnp.dot(xh, ones, preferred_element_type=F32) + jnp.dot(xl, ones, preferred_element_type=F32)


@jax.custom_vjp
def _head_sum(x):
    return _head_sum_raw(x)


_head_sum.defvjp(lambda x: (_head_sum_raw(x), None), lambda _, g: (_head_sum_raw(g),))


def _pre_b_fn(k0, a, k_k, k_a):
    kkr = k0 * k_k
    nrm = jnp.sqrt(_head_sum(kkr * kkr))
    kk = kkr / jnp.maximum(nrm, 1e-12)
    k = k0 * (1.0 + (a - 1.0) * k_a)
    return -kk, k, kk * a


def _rwkv_post_fn(y, r, k, v, g, lnx_w, lnx_b, r_k):
    inv = 1.0 / RWKV_HEAD_DIM
    mu = _head_sum(y) * inv
    yc = y - mu
    var = _head_sum(yc * yc) * inv
    yn = yc * lax.rsqrt(var + RWKV_GN_EPS) * lnx_w + lnx_b
    bonus = _head_sum(r * k * r_k) * v
    return ((yn + bonus) * g,)


def _rotary_fn(cos2, sin2, q, k):
    qs, ks = [], []
    for h in range(RET_HEADS):
        sl = slice(h * RET_HEAD_DIM, (h + 1) * RET_HEAD_DIM)
        qh, kh = q[:, sl], k[:, sl]
        qs.append(qh * cos2 + _swap_halves(qh) * sin2)
        ks.append((kh * cos2 + _swap_halves(kh) * sin2) * (RET_HEAD_DIM ** -0.5))
    return jnp.concatenate(qs, axis=1), jnp.concatenate(ks, axis=1)


def _ret_post_fn(y, gp, gn_w):
    outs = []
    for h in range(RET_HEADS):
        sl = slice(h * RET_HEAD_DIM, (h + 1) * RET_HEAD_DIM)
        yh = y[:, sl]
        mu = jnp.mean(yh, axis=-1, keepdims=True)
        yc = yh - mu
        var = jnp.mean(yc * yc, axis=-1, keepdims=True)
        outs.append(yc * lax.rsqrt(var + RET_GN_EPS) * gn_w[:, sl])
    yn = jnp.concatenate(outs, axis=1)
    return (gp * _sigmoid(gp) * yn,)


class _Cols:
    def __init__(self, array, width, block):
        self.array, self.width, self.block = array, width, block
        self.shape, self.ndim, self.dtype = (array.shape[0], width), 2, array.dtype


def _arr(a):
    return a.array if isinstance(a, _Cols) else a


def _blk_spec(a, tb, rev_nb=None):
    nd = a.ndim
    tail = (a.block,) if isinstance(a, _Cols) else (0,) * (nd - 1)
    if rev_nb is None:
        return pl.BlockSpec((tb,) + a.shape[1:], lambda i: (i,) + tail)
    return pl.BlockSpec((tb,) + a.shape[1:], lambda i: (rev_nb - 1 - i,) + tail)


def _full_spec(a):
    nd = a.ndim
    return pl.BlockSpec(a.shape, lambda i: (0,) * nd)


def _tok_fwd(name, fn, toks, consts, out_tails, tb=LIGHT_TOK_BLOCK, out_dtypes=None):
    out_dtypes = out_dtypes or [F32] * len(out_tails)
    n_in = len(toks) + len(consts)
    tn = toks[0].shape[0]

    def body(*refs):
        outs = fn(*[r[...] for r in refs[:n_in]])
        for r, o in zip(refs[n_in:], outs):
            r[...] = o.astype(r.dtype)

    out_shape = [jax.ShapeDtypeStruct((tn,) + tuple(s), dt) for s, dt in zip(out_tails, out_dtypes)]
    return pl.pallas_call(
        body, name=name, grid=(tn // tb,),
        in_specs=[_blk_spec(a, tb) for a in toks] + [_full_spec(c) for c in consts],
        out_specs=[_blk_spec(o, tb) for o in out_shape], out_shape=out_shape,
        compiler_params=_cparams(1))(*[_arr(a) for a in toks], *consts)


def _tok_bwd(name, fn, aux, toks, consts, cts, add=None, tb=LIGHT_TOK_BLOCK, tok_dtypes=None):
    n_aux, n_tok, n_c = len(aux), len(toks), len(consts)
    ct_groups = [c if isinstance(c, (tuple, list)) else (c,) for c in cts]
    ct_flat = [a for grp in ct_groups for a in grp]
    n_ct = len(ct_flat)
    n_add = 0 if add is None else 1
    tn = toks[0].shape[0]

    def body(*refs):
        pos = 0
        aux_v = [r[...] for r in refs[pos:pos + n_aux]]; pos += n_aux
        tok_v = [r[...] for r in refs[pos:pos + n_tok]]; pos += n_tok
        const_v = [r[...] for r in refs[pos:pos + n_c]]; pos += n_c
        ct_refs = refs[pos:pos + n_ct]; pos += n_ct
        add_refs = refs[pos:pos + n_add]; pos += n_add
        dtok_refs = refs[pos:pos + n_tok]; pos += n_tok
        dconst_refs = refs[pos:pos + n_c]
        ct_v, q = [], 0
        for grp in ct_groups:
            s = ct_refs[q][...]
            for r in ct_refs[q + 1:q + len(grp)]:
                s = s + r[...]
            q += len(grp)
            ct_v.append(s)
        _, vjp = jax.vjp(lambda *tc: fn(*aux_v, *tc), *tok_v, *const_v)
        grads = vjp(tuple(ct_v))
        for j, r in enumerate(dtok_refs):
            gj = grads[j]
            if j == 0 and n_add:
                gj = gj + add_refs[0][...]
            r[...] = gj.astype(r.dtype)

        @pl.when(pl.program_id(0) == 0)
        def _():
            for r in dconst_refs:
                r[...] = jnp.zeros(r.shape, F32)

        for j, r in enumerate(dconst_refs):
            r[...] += grads[n_tok + j]

    ins = list(aux) + list(toks) + list(consts) + ct_flat + ([add] if n_add else [])
    in_specs = ([_blk_spec(a, tb) for a in aux] + [_blk_spec(a, tb) for a in toks] + [_full_spec(c) for c in consts]
                + [_blk_spec(a, tb) for a in ct_flat] + ([_blk_spec(add, tb)] if n_add else []))
    tok_dtypes = tok_dtypes or [F32] * n_tok
    out_shape = ([jax.ShapeDtypeStruct(a.shape, dt) for a, dt in zip(toks, tok_dtypes)]
                 + [jax.ShapeDtypeStruct(c.shape, F32) for c in consts])
    out_specs = [_blk_spec(o, tb) for o in out_shape[:n_tok]] + [_full_spec(c) for c in consts]
    return pl.pallas_call(body, name=name, grid=(tn // tb,), in_specs=in_specs, out_specs=out_specs,
                          out_shape=out_shape, compiler_params=_cparams(1))(*[_arr(a) for a in ins])


MM_VMEM_BUDGET = 40 * 1024 * 1024
MM_STEP_SECONDS = 0.4e-6
MM_HBM_BYTES_PER_SECOND = 2.5e12
MM_XPOSE_SECONDS_PER_ELEM = 2e-12
MM_MXU_COLUMNS = 256
MM_MXU_FLOPS = 9e14


def _mm_tiles(m, n, kd, a_bytes, b_bytes, o_bytes, has_add, ta):
    divs = lambda d: [t for t in range(128, d + 1, 128) if d % t == 0]
    best = None
    for tm in divs(m):
        for tn in divs(n):
            for tk in divs(kd):
                ni, nj, nk = m // tm, n // tn, kd // tk
                vmem = (2 * tm * tk * a_bytes + 2 * tk * tn * b_bytes + tm * tn * 4 + 2 * tm * tn * o_bytes
                        + (2 * tm * tn * 4 if has_add else 0) + 2 * (tm * tk + tk * tn) + tm * tn * 4)
                if vmem > MM_VMEM_BUDGET:
                    continue
                a_traffic = m * kd * a_bytes * (nj if nk > 1 else 1)
                b_traffic = kd * n * b_bytes * (ni if nj * nk > 1 else 1)
                cost = ni * nj * nk * MM_STEP_SECONDS + (a_traffic + b_traffic) / MM_HBM_BYTES_PER_SECOND
                cost += 2.0 * m * kd * nj * max(tn, MM_MXU_COLUMNS) / MM_MXU_FLOPS
                if ta:
                    cost += m * kd * nj * MM_XPOSE_SECONDS_PER_ELEM
                if best is None or cost < best[0]:
                    best = (cost, tm, tn, tk)
    return best[1:]


def _mm(name, a, b, ta=False, tb=False, add=None, out_dtype=F32):
    if ta:
        kd, m = a.shape
    else:
        m, kd = a.shape
    if tb:
        n, kb = b.shape
    else:
        kb, n = b.shape
    assert kd == kb, (a.shape, b.shape)
    tm, tn, tk = _mm_tiles(m, n, kd, a.dtype.itemsize, b.dtype.itemsize, jnp.dtype(out_dtype).itemsize,
                           add is not None, ta)
    nk = kd // tk
    has_add = add is not None
    dims = (((0 if ta else 1,), (1 if tb else 0,)), ((), ()))

    def body(*refs):
        a_ref, b_ref = refs[0], refs[1]
        o_ref, acc_ref = refs[-2], refs[-1]
        k = pl.program_id(2)

        @pl.when(k == 0)
        def _():
            acc_ref[...] = refs[2][...] if has_add else jnp.zeros(acc_ref.shape, F32)

        acc_ref[...] += lax.dot_general(a_ref[...].astype(BF16), b_ref[...].astype(BF16), dims,
                                        preferred_element_type=F32)

        @pl.when(k == nk - 1)
        def _():
            o_ref[...] = acc_ref[...].astype(out_dtype)

    a_spec = pl.BlockSpec((tk, tm), lambda i, j, k: (k, i)) if ta else pl.BlockSpec((tm, tk), lambda i, j, k: (i, k))
    b_spec = pl.BlockSpec((tn, tk), lambda i, j, k: (j, k)) if tb else pl.BlockSpec((tk, tn), lambda i, j, k: (k, j))
    o_spec = pl.BlockSpec((tm, tn), lambda i, j, k: (i, j))
    ins = [a, b] + ([add] if has_add else [])
    in_specs = [a_spec, b_spec] + ([o_spec] if has_add else [])
    return pl.pallas_call(body, name=name, grid=(m // tm, n // tn, nk), in_specs=in_specs, out_specs=o_spec,
                          out_shape=jax.ShapeDtypeStruct((m, n), out_dtype),
                          scratch_shapes=[pltpu.VMEM((tm, tn), F32)], compiler_params=_cparams(3))(*ins)


def _prev8_spec(a, tb, rev_nb=None):
    r = tb // 8
    if rev_nb is None:
        return pl.BlockSpec((8, a.shape[1]), lambda i: (jnp.maximum(i * r - 1, 0), 0))
    return pl.BlockSpec((8, a.shape[1]), lambda i: (jnp.maximum((rev_nb - 1 - i) * r - 1, 0), 0))


def _pre_a_fwd(h1, p, consts, tb=TOK_BLOCK):
    tn = h1.shape[0]

    def body(h1_ref, h1h_ref, p_ref, ph_ref, *rest):
        c_refs, o_refs = rest[:len(consts)], rest[len(consts):]
        first = pl.program_id(0) == 0
        h1p = jnp.where(first, 0.0, h1h_ref[7:8, :])
        pp = jnp.where(first, 0.0, ph_ref[7:8, :])
        outs = _pre_a_fn(h1_ref[...], h1p, p_ref[...], pp, *[c[...] for c in c_refs])
        for r, o in zip(o_refs, outs):
            r[...] = o

    out_shape = [jax.ShapeDtypeStruct((tn, RWKV_WIDTH), F32) for _ in range(7)]
    return pl.pallas_call(
        body, name="rwkv_pre_a_fwd", grid=(tn // tb,),
        in_specs=[_blk_spec(h1, tb), _prev8_spec(h1, tb), _blk_spec(p, tb), _prev8_spec(p, tb)]
        + [_full_spec(c) for c in consts],
        out_specs=[_blk_spec(o, tb) for o in out_shape], out_shape=out_shape,
        compiler_params=_cparams(1))(h1, h1, _arr(p), _arr(p), *consts)


def _pre_a_bwd(h1, p, consts, cts, tb=TOK_BLOCK):
    tn = h1.shape[0]
    nb = tn // tb
    n_c = len(consts)
    ct_groups = [c if isinstance(c, (tuple, list)) else (c,) for c in cts]
    ct_flat = [a for grp in ct_groups for a in grp]
    n_ct = len(ct_flat)

    def body(*refs):
        h1_ref, h1h_ref, p_ref, ph_ref = refs[:4]
        c_refs = refs[4:4 + n_c]
        ct_refs = refs[4 + n_c:4 + n_c + n_ct]
        dh1_ref, dp_ref = refs[4 + n_c + n_ct:6 + n_c + n_ct]
        dc_refs = refs[6 + n_c + n_ct:6 + 2 * n_c + n_ct]
        ch_ref, cp_ref = refs[-2], refs[-1]
        i = pl.program_id(0)
        first_block = i == nb - 1
        h1p = jnp.where(first_block, 0.0, h1h_ref[7:8, :])
        pp = jnp.where(first_block, 0.0, ph_ref[7:8, :])
        ct_v, q = [], 0
        for grp in ct_groups:
            s = ct_refs[q][...]
            for r in ct_refs[q + 1:q + len(grp)]:
                s = s + r[...]
            q += len(grp)
            ct_v.append(s)
        _, vjp = jax.vjp(_pre_a_fn, h1_ref[...], h1p, p_ref[...], pp, *[c[...] for c in c_refs])
        grads = vjp(tuple(ct_v))

        @pl.when(i == 0)
        def _():
            ch_ref[...] = jnp.zeros(ch_ref.shape, F32)
            cp_ref[...] = jnp.zeros(cp_ref.shape, F32)
            for r in dc_refs:
                r[...] = jnp.zeros(r.shape, F32)

        rowh = lax.broadcasted_iota(jnp.int32, (tb, h1.shape[1]), 0)
        rowp = lax.broadcasted_iota(jnp.int32, (tb, p.shape[1]), 0)
        dh1_ref[...] = grads[0] + jnp.where(rowh == tb - 1, jnp.broadcast_to(ch_ref[0:1, :], rowh.shape), 0.0)
        dp_ref[...] = (grads[2] + jnp.where(rowp == tb - 1, jnp.broadcast_to(cp_ref[0:1, :], rowp.shape), 0.0)
                       ).astype(dp_ref.dtype)
        ch_ref[0:1, :] = grads[1]
        cp_ref[0:1, :] = grads[3]
        for j, r in enumerate(dc_refs):
            r[...] += grads[4 + j]

    ins = [h1, h1, _arr(p), _arr(p)] + list(consts) + ct_flat
    in_specs = ([_blk_spec(h1, tb, nb), _prev8_spec(h1, tb, nb), _blk_spec(p, tb, nb), _prev8_spec(p, tb, nb)]
                + [_full_spec(c) for c in consts] + [_blk_spec(a, tb, nb) for a in ct_flat])
    out_shape = ([jax.ShapeDtypeStruct(h1.shape, F32), jax.ShapeDtypeStruct(p.shape, BF16)]
                 + [jax.ShapeDtypeStruct(c.shape, F32) for c in consts])
    out_specs = [_blk_spec(h1, tb, nb), _blk_spec(p, tb, nb)] + [_full_spec(c) for c in consts]
    return pl.pallas_call(body, name="rwkv_pre_a_bwd", grid=(nb,), in_specs=in_specs, out_specs=out_specs,
                          out_shape=out_shape,
                          scratch_shapes=[pltpu.VMEM((8, h1.shape[1]), F32), pltpu.VMEM((8, p.shape[1]), F32)],
                          compiler_params=_cparams(1))(*ins)


def _my_index():
    return 4 * lax.axis_index("x") + 2 * lax.axis_index("y") + lax.axis_index("c")


def _peer(k):
    x, y, c = lax.axis_index("x"), lax.axis_index("y"), lax.axis_index("c")
    px = 1 - x if k & 4 else x
    py = 1 - y if k & 2 else y
    pc = 1 - c if k & 1 else c
    return (px, py, pc), 4 * px + 2 * py + pc


def _xchg_sems(n):
    return [pltpu.SemaphoreType.DMA((n * (N_DEV - 1),)), pltpu.SemaphoreType.DMA((n * (N_DEV - 1),)),
            pltpu.SemaphoreType.DMA((n,))]


def _scatter_copies(srcs, dsts, sems, incoming=False):
    send_sems, recv_sems, local_sems = sems
    me = _my_index()
    local, remote = [], []
    for i, (s, d) in enumerate(zip(srcs, dsts)):
        if not incoming:
            local.append(pltpu.make_async_copy(s.at[me], d.at[me], local_sems.at[i]))
        for k in range(1, N_DEV):
            peer, plin = _peer(k)
            j = i * (N_DEV - 1) + k - 1
            s_slot, d_slot = (me, plin) if incoming else (plin, me)
            remote.append(pltpu.make_async_remote_copy(
                src_ref=s.at[s_slot], dst_ref=d.at[d_slot], send_sem=send_sems.at[j],
                recv_sem=recv_sems.at[j], device_id=peer, device_id_type=pl.DeviceIdType.MESH))
    return local, remote


def _scatter_start(srcs, dsts, sems):
    local, out = _scatter_copies(srcs, dsts, sems)
    for cp in local + out:
        cp.start()


def _scatter_wait(srcs, dsts, sems):
    for cp in _scatter_copies(srcs, dsts, sems, incoming=True)[1]:
        cp.wait_recv()
    local, out = _scatter_copies(srcs, dsts, sems)
    for cp in out:
        cp.wait_send()
    for cp in local:
        cp.wait()


_ICI_PEERS = (2, 4, 6)


def _gather_copies(srcs, dsts, sems, group):
    send_sems, recv_sems, local_sems = sems
    me = _my_index()
    sib, sib_lin = _peer(1)
    out = []
    for i, (s, d) in enumerate(zip(srcs, dsts)):
        def mk(q, src, dst, dev):
            j = i * (N_DEV - 1) + q
            return pltpu.make_async_remote_copy(src_ref=src, dst_ref=dst, send_sem=send_sems.at[j],
                                                recv_sem=recv_sems.at[j], device_id=dev,
                                                device_id_type=pl.DeviceIdType.MESH)
        if group == 'local':
            out.append(pltpu.make_async_copy(s, d.at[me], local_sems.at[i]))
        elif group == 'own':
            out.append(mk(0, s, d.at[me], sib))
        elif group == 'in_d2d':
            out.append(mk(0, s, d.at[sib_lin], sib))
        for jj, k in enumerate(_ICI_PEERS):
            peer, plin = _peer(k)
            plin_other = _peer(k + 1)[1]
            if group == 'own':
                out.append(mk(1 + jj, s, d.at[me], peer))
            elif group == 'in_ici':
                out.append(mk(1 + jj, s, d.at[plin], peer))
            elif group == 'pass_on':
                out.append(mk(4 + jj, d.at[plin], d.at[plin], sib))
            elif group == 'in_d2d':
                out.append(mk(4 + jj, d.at[plin_other], d.at[plin_other], sib))
    return out


def _gather_start(srcs, dsts, sems):
    for cp in _gather_copies(srcs, dsts, sems, 'local') + _gather_copies(srcs, dsts, sems, 'own'):
        cp.start()


def _gather_pass_on(srcs, dsts, sems):
    for cp in _gather_copies(srcs, dsts, sems, 'in_ici'):
        cp.wait_recv()
    for cp in _gather_copies(srcs, dsts, sems, 'pass_on'):
        cp.start()


def _gather_finish(srcs, dsts, sems):
    for cp in _gather_copies(srcs, dsts, sems, 'in_d2d'):
        cp.wait_recv()
    for cp in _gather_copies(srcs, dsts, sems, 'own') + _gather_copies(srcs, dsts, sems, 'pass_on'):
        cp.wait_send()
    for cp in _gather_copies(srcs, dsts, sems, 'local'):
        cp.wait()


def _xchg_out_shapes(srcs, scatter):
    return [jax.ShapeDtypeStruct(s.shape if scatter else (N_DEV,) + s.shape, s.dtype) for s in srcs]


_ANY = pl.BlockSpec(memory_space=pl.ANY)


def _exchange(name, srcs, scatter):
    n = len(srcs)

    def body(*refs):
        s, d, sems = refs[:n], refs[n:2 * n], refs[2 * n:]
        if scatter:
            _scatter_start(s, d, sems)
            _scatter_wait(s, d, sems)
        else:
            _gather_start(s, d, sems)
            _gather_pass_on(s, d, sems)
            _gather_finish(s, d, sems)

    return pl.pallas_call(body, name=name, in_specs=[_ANY] * n, out_specs=[_ANY] * n,
                          out_shape=_xchg_out_shapes(srcs, scatter), scratch_shapes=_xchg_sems(n))(*srcs)


_MM_DIMS = {'nn': (((1,), (0,)), ((), ())), 'nt': (((1,), (1,)), ((), ())), 'tn': (((0,), (0,)), ((), ()))}


def _cmm_raw(x, y, kind, split):
    dot = functools.partial(lax.dot_general, dimension_numbers=_MM_DIMS[kind], preferred_element_type=F32)
    xh, yh = x.astype(BF16), y.astype(BF16)
    out = dot(xh, yh)
    if split:
        xl = (x - xh.astype(F32)).astype(BF16)
        yl = (y - yh.astype(F32)).astype(BF16)
        out = out + (dot(xh, yl) + dot(xl, yh))
    return out


@functools.partial(jax.custom_vjp, nondiff_argnums=(2, 3))
def _cmm(x, y, kind, split=False):
    return _cmm_raw(x, y, kind, split)


def _cmm_fwd(x, y, kind, split):
    return _cmm_raw(x, y, kind, split), (x, y)


def _cmm_bwd(kind, split, res, g):
    x, y = res
    if kind == 'nn':
        return _cmm_raw(g, y, 'nt', split), _cmm_raw(x, g, 'tn', split)
    if kind == 'nt':
        return _cmm_raw(g, y, 'nn', split), _cmm_raw(g, x, 'tn', split)
    return _cmm_raw(y, g, 'nt', split), _cmm_raw(x, g, 'nn', split)


_cmm.defvjp(_cmm_fwd, _cmm_bwd)


def _tri_sum_raw(tri, x, kind):
    dot = functools.partial(lax.dot_general, dimension_numbers=_MM_DIMS[kind], preferred_element_type=F32)
    tb = tri.astype(BF16)
    hi, mid, lo = _split3(x)
    return (dot(tb, hi) + dot(tb, mid)) + dot(tb, lo)


@functools.partial(jax.custom_vjp, nondiff_argnums=(2,))
def _tri_sum(tri, x, kind):
    return _tri_sum_raw(tri, x, kind)


def _tri_sum_fwd(tri, x, kind):
    return _tri_sum_raw(tri, x, kind), tri


def _tri_sum_bwd(kind, tri, g):
    return jnp.zeros_like(tri), _tri_sum_raw(tri, g, 'tn' if kind == 'nn' else 'nn')


_tri_sum.defvjp(_tri_sum_fwd, _tri_sum_bwd)


def _chunk_fn(S0, r, lw, k, v, a, b):
    hs = range(len(r))
    C = r[0].shape[0]
    ii = lax.broadcasted_iota(jnp.int32, (C, C), 0)
    jj = lax.broadcasted_iota(jnp.int32, (C, C), 1)
    incl, strict = ii >= jj, ii > jj
    eye = (ii == jj).astype(F32)
    inclf = incl.astype(F32)
    cum = [_tri_sum(inclf, lw[h], 'nn') for h in hs]
    e_inv = [jnp.exp(-cum[h]) for h in hs]
    At = [a[h] * jnp.exp(cum[h] - lw[h]) for h in hs]
    Rt = [r[h] * jnp.exp(cum[h]) for h in hs]
    Kh = [k[h] * e_inv[h] for h in hs]
    Bh = [b[h] * e_inv[h] for h in hs]
    Mab = [jnp.where(strict, _cmm(At[h], Bh[h], 'nt', True), 0.0) for h in hs]
    Mak = [jnp.where(strict, _cmm(At[h], Kh[h], 'nt', True), 0.0) for h in hs]
    Mrk = [jnp.where(incl, _cmm(Rt[h], Kh[h], 'nt', True), 0.0) for h in hs]
    Mrb = [jnp.where(incl, _cmm(Rt[h], Bh[h], 'nt', True), 0.0) for h in hs]
    rhs = [_cmm(At[h], S0[h], 'nt') + _cmm(Mak[h], v[h], 'nn') for h in hs]
    P = Mab
    Tm = [eye + P[h] for h in hs]
    n = 1
    while 2 * n < C:
        P = [_cmm(P[h], P[h], 'nn', True) for h in hs]
        Tm = [_cmm(Tm[h], eye + P[h], 'nn', True) for h in hs]
        n *= 2
    U = [_cmm(Tm[h], rhs[h], 'nn', True) for h in hs]
    Y = [_cmm(Rt[h], S0[h], 'nt') + _cmm(Mrk[h], v[h], 'nn') + _cmm(Mrb[h], U[h], 'nn') for h in hs]
    gC = [jnp.exp(jnp.sum(lw[h], axis=0, keepdims=True)) for h in hs]
    SC = [S0[h] * gC[h] + _cmm(v[h], Kh[h] * gC[h], 'tn') + _cmm(U[h], Bh[h] * gC[h], 'tn') for h in hs]
    return tuple(Y), tuple(SC)


def _cscan_fwd(r, lw, k, v, a, b, xs):
    n_x = len(xs)
    tn = r.shape[0]
    H, Dh, Dv = RWKV_HEADS, RWKV_HEAD_DIM, RWKV_HEAD_DIM
    nc = tn // SCAN_CHUNK
    lanes = lambda h: slice(h * Dh, (h + 1) * Dh)
    heads = lambda ref: tuple(ref[:, lanes(h)] for h in range(H))
    mats = lambda ref: tuple(ref[h] for h in range(H))

    def body(r_ref, lw_ref, k_ref, v_ref, a_ref, b_ref, *rest):
        x_src, (y_ref, ck_ref) = rest[:n_x], rest[n_x:n_x + 2]
        x_dst, s_ref, sems = rest[n_x + 2:2 * n_x + 2], rest[2 * n_x + 2], rest[2 * n_x + 3:]

        @pl.when(pl.program_id(0) == 0)
        def _():
            s_ref[...] = jnp.zeros(s_ref.shape, F32)
            _gather_start(x_src, x_dst, sems)

        ck_ref[0] = s_ref[...]
        y, sc = _chunk_fn(mats(s_ref), heads(r_ref), heads(lw_ref), heads(k_ref), heads(v_ref), heads(a_ref),
                          heads(b_ref))
        for h in range(H):
            y_ref[:, lanes(h)] = y[h]
            s_ref[h] = sc[h]

        @pl.when(pl.program_id(0) == max(nc - 4, 0))
        def _():
            _gather_pass_on(x_src, x_dst, sems)

        @pl.when(pl.program_id(0) == nc - 1)
        def _():
            _gather_finish(x_src, x_dst, sems)

    hm = pl.BlockSpec((SCAN_CHUNK, H * Dh), lambda c: (c, 0))
    res = pl.pallas_call(
        body, name="rwkv_scan_fwd", grid=(nc,), in_specs=[hm] * 6 + [_ANY] * n_x,
        out_specs=[hm, pl.BlockSpec((1, H, Dv, Dh), lambda c: (c, 0, 0, 0))] + [_ANY] * n_x,
        out_shape=[jax.ShapeDtypeStruct((tn, H * Dh), F32), jax.ShapeDtypeStruct((nc, H, Dv, Dh), F32)]
        + _xchg_out_shapes(xs, False),
        scratch_shapes=[pltpu.VMEM((H, Dv, Dh), F32)] + _xchg_sems(n_x),
        compiler_params=_cparams(1))(r, lw, k, v, a, b, *xs)
    return res[0], res[1], res[2:]


def _cscan_bwd(r, lw, k, v, a, b, dy, ck, xs):
    n_x = len(xs)
    tn = r.shape[0]
    H, Dh, Dv = RWKV_HEADS, RWKV_HEAD_DIM, RWKV_HEAD_DIM
    nc = tn // SCAN_CHUNK
    lanes = lambda h: slice(h * Dh, (h + 1) * Dh)
    heads = lambda ref: tuple(ref[:, lanes(h)] for h in range(H))
    mats = lambda ref: tuple(ref[h] for h in range(H))

    def body(r_ref, lw_ref, k_ref, v_ref, a_ref, b_ref, dy_ref, ck_ref, *rest):
        x_src = rest[:n_x]
        d_refs = rest[n_x:n_x + 6]
        x_dst = rest[n_x + 6:2 * n_x + 6]
        g_ref = rest[2 * n_x + 6]
        sems = rest[2 * n_x + 7:]

        @pl.when(pl.program_id(0) == 0)
        def _():
            g_ref[...] = jnp.zeros(g_ref.shape, F32)
            _scatter_start(x_src, x_dst, sems)

        s0 = tuple(ck_ref[0, h] for h in range(H))
        _, vjp = jax.vjp(_chunk_fn, s0, heads(r_ref), heads(lw_ref), heads(k_ref), heads(v_ref), heads(a_ref),
                         heads(b_ref))
        grads = vjp((heads(dy_ref), mats(g_ref)))
        for h in range(H):
            g_ref[h] = grads[0][h]
            for d_ref, gz in zip(d_refs, grads[1:]):
                d_ref[:, lanes(h)] = gz[h]

        @pl.when(pl.program_id(0) == nc - 1)
        def _():
            _scatter_wait(x_src, x_dst, sems)

    hm = pl.BlockSpec((SCAN_CHUNK, H * Dh), lambda c: (nc - 1 - c, 0))
    hshape = jax.ShapeDtypeStruct((tn, H * Dh), F32)
    res = pl.pallas_call(
        body, name="rwkv_scan_bwd", grid=(nc,),
        in_specs=[hm] * 7 + [pl.BlockSpec((1, H, Dv, Dh), lambda c: (nc - 1 - c, 0, 0, 0))] + [_ANY] * n_x,
        out_specs=[hm] * 6 + [_ANY] * n_x, out_shape=[hshape] * 6 + _xchg_out_shapes(xs, True),
        scratch_shapes=[pltpu.VMEM((H, Dv, Dh), F32)] + _xchg_sems(n_x),
        compiler_params=_cparams(1))(r, lw, k, v, a, b, dy, ck, *xs)
    return res[:6], res[6:]


def _decay_mask(lg, i, j, blk):
    rows = lax.broadcasted_iota(jnp.int32, (blk, blk), 0)
    cols = lax.broadcasted_iota(jnp.int32, (blk, blk), 1)
    dd = (rows - cols + (i - j) * blk).astype(F32)
    return jnp.where(dd >= 0.0, jnp.exp(lg * jnp.maximum(dd, 0.0)), 0.0)


_NT = (((1,), (1,)), ((), ()))
_TN = (((0,), (0,)), ((), ()))


def _ret_attn_fwd(lg, q, k, v, v_col0=0, blk=ATT_BLOCK):
    tn = q.shape[0]
    Dh = RET_HEAD_DIM

    def body(lg_ref, q_ref, k_ref, v_ref, o_ref):
        i = pl.program_id(1)
        lgv = lg_ref[0][:, 0:1]
        qb = q_ref[...].astype(BF16)

        def jb(j, acc):
            ks = pl.ds(pl.multiple_of(j * blk, blk), blk)
            s = lax.dot_general(qb, k_ref[ks, :].astype(BF16), _NT, preferred_element_type=F32)
            s = s * _decay_mask(lgv, i, j, blk)
            return acc + jnp.dot(s.astype(BF16), v_ref[ks, :].astype(BF16), preferred_element_type=F32)

        o_ref[...] = lax.fori_loop(0, i + 1, jb, jnp.zeros((blk, Dh), F32))

    full = pl.BlockSpec((tn, Dh), lambda h, i: (0, h))
    qs = pl.BlockSpec((blk, Dh), lambda h, i: (i, h))
    return pl.pallas_call(
        body, name="ret_attn_fwd", grid=(RET_HEADS, tn // blk),
        in_specs=[pl.BlockSpec((1, 1, 128), lambda h, i: (h, 0, 0)), qs, full,
                  pl.BlockSpec((tn, Dh), lambda h, i: (0, v_col0 + h))],
        out_specs=qs, out_shape=jax.ShapeDtypeStruct(q.shape, F32), compiler_params=_cparams(2))(lg, q, k, v)


def _ret_attn_bwd(lg, q, k, v, do, v_col0=0, blk=ATT_BLOCK):
    tn = q.shape[0]
    nb = tn // blk
    Dh = RET_HEAD_DIM

    def body(lg_ref, q_ref, k_ref, v_ref, do_ref, dq_ref, dk_ref, dv_ref):
        lgv = lg_ref[0][:, 0:1]
        dk_ref[...] = jnp.zeros(dk_ref.shape, F32)
        dv_ref[...] = jnp.zeros(dv_ref.shape, F32)

        def ib(i, carry):
            qs = pl.ds(pl.multiple_of(i * blk, blk), blk)
            qb = q_ref[qs, :].astype(BF16)
            dob = do_ref[qs, :].astype(BF16)

            def jb(j, dq):
                ks = pl.ds(pl.multiple_of(j * blk, blk), blk)
                kb = k_ref[ks, :].astype(BF16)
                vb = v_ref[ks, :].astype(BF16)
                dm = _decay_mask(lgv, i, j, blk)
                s = lax.dot_general(qb, kb, _NT, preferred_element_type=F32) * dm
                ds = lax.dot_general(dob, vb, _NT, preferred_element_type=F32) * dm
                sb, dsb = s.astype(BF16), ds.astype(BF16)
                dv_ref[ks, :] += lax.dot_general(sb, dob, _TN, preferred_element_type=F32)
                dk_ref[ks, :] += lax.dot_general(dsb, qb, _TN, preferred_element_type=F32)
                return dq + jnp.dot(dsb, kb, preferred_element_type=F32)

            dq_ref[qs, :] = lax.fori_loop(0, i + 1, jb, jnp.zeros((blk, Dh), F32))
            return carry

        lax.fori_loop(0, nb, ib, 0)

    full = pl.BlockSpec((tn, Dh), lambda h: (0, h))
    sh = jax.ShapeDtypeStruct(q.shape, F32)
    return pl.pallas_call(
        body, name="ret_attn_bwd", grid=(RET_HEADS,),
        in_specs=[pl.BlockSpec((1, 1, 128), lambda h: (h, 0, 0)), full, full,
                  pl.BlockSpec((tn, Dh), lambda h: (0, v_col0 + h)), full],
        out_specs=[full, full, full], out_shape=[sh, sh, sh], compiler_params=_cparams(1))(lg, q, k, v, do)


def _next8_spec(a, tb):
    r = tb // 8
    last = a.shape[0] // 8 - 1
    return pl.BlockSpec((8, a.shape[1]), lambda i: (jnp.minimum((i + 1) * r, last), 0))


def _conv_taps(g_ext, cw_ref, cb_ref):
    return (cw_ref[2:3, :] * g_ext + cw_ref[1:2, :] * pltpu.roll(g_ext, 1, 0)
            + cw_ref[0:1, :] * pltpu.roll(g_ext, 2, 0) + cb_ref[...])


def _glu_fwd(gate, up, cw, cb, tb=TOK_BLOCK):
    tn = gate.shape[0]

    def body(g_ref, gh_ref, u_ref, cw_ref, cb_ref, o_ref):
        halo = jnp.where(pl.program_id(0) == 0, 0.0, gh_ref[...])
        g_ext = jnp.concatenate([halo, g_ref[...]], axis=0)
        gc = _conv_taps(g_ext, cw_ref, cb_ref)[8:, :]
        o_ref[...] = (gc * _sigmoid(gc) * u_ref[...]).astype(o_ref.dtype)

    return pl.pallas_call(
        body, name="glu_fwd", grid=(tn // tb,),
        in_specs=[_blk_spec(gate, tb), _prev8_spec(gate, tb), _blk_spec(up, tb), _full_spec(cw), _full_spec(cb)],
        out_specs=_blk_spec(gate, tb), out_shape=jax.ShapeDtypeStruct(gate.shape, BF16),
        compiler_params=_cparams(1))(gate, gate, up, cw, cb)


def _glu_bwd(gate, up, dact, cw, cb, tb=TOK_BLOCK):
    tn = gate.shape[0]
    nb = tn // tb

    def body(g_ref, gp_ref, gn_ref, u_ref, un_ref, d_ref, dn_ref, cw_ref, cb_ref, dg_ref, du_ref, dcw_ref, dcb_ref):
        i = pl.program_id(0)
        gprev = jnp.where(i == 0, 0.0, gp_ref[...])
        dnext = jnp.where(i == nb - 1, 0.0, dn_ref[...])
        g_ext = jnp.concatenate([gprev, g_ref[...], gn_ref[...]], axis=0)
        gc = _conv_taps(g_ext, cw_ref, cb_ref)[8:, :]
        u_e = jnp.concatenate([u_ref[...], un_ref[...]], axis=0)
        d_e = jnp.concatenate([d_ref[...], dnext], axis=0)
        s = _sigmoid(gc)
        dgc = d_e * u_e * (s * (1.0 + gc * (1.0 - s)))
        du_ref[...] = (d_ref[...] * (gc * s)[:tb, :]).astype(du_ref.dtype)
        n_e = tb + 8
        dg_ref[...] = (cw_ref[2:3, :] * dgc + cw_ref[1:2, :] * pltpu.roll(dgc, n_e - 1, 0)
                       + cw_ref[0:1, :] * pltpu.roll(dgc, n_e - 2, 0))[:tb, :].astype(dg_ref.dtype)

        @pl.when(i == 0)
        def _():
            dcw_ref[...] = jnp.zeros(dcw_ref.shape, F32)
            dcb_ref[...] = jnp.zeros(dcb_ref.shape, F32)

        dgc_b = dgc[:tb, :]
        g0 = g_ext[8:8 + tb, :]
        g1 = pltpu.roll(g_ext, 1, 0)[8:8 + tb, :]
        g2 = pltpu.roll(g_ext, 2, 0)[8:8 + tb, :]
        dcw_ref[2:3, :] += jnp.sum(dgc_b * g0, axis=0, keepdims=True)
        dcw_ref[1:2, :] += jnp.sum(dgc_b * g1, axis=0, keepdims=True)
        dcw_ref[0:1, :] += jnp.sum(dgc_b * g2, axis=0, keepdims=True)
        dcb_ref[...] += jnp.sum(dgc_b, axis=0, keepdims=True)

    sh = jax.ShapeDtypeStruct(gate.shape, BF16)
    return pl.pallas_call(
        body, name="glu_bwd", grid=(nb,),
        in_specs=[_blk_spec(gate, tb), _prev8_spec(gate, tb), _next8_spec(gate, tb), _blk_spec(up, tb),
                  _next8_spec(up, tb), _blk_spec(dact, tb), _next8_spec(dact, tb), _full_spec(cw), _full_spec(cb)],
        out_specs=[_blk_spec(gate, tb), _blk_spec(gate, tb), _full_spec(cw), _full_spec(cb)],
        out_shape=[sh, sh, jax.ShapeDtypeStruct(cw.shape, F32), jax.ShapeDtypeStruct(cb.shape, F32)],
        compiler_params=_cparams(1))(gate, gate, gate, up, up, dact, dact, cw, cb)


def _final_loss(x2, tgt, g, tb=LIGHT_TOK_BLOCK):
    tn, dm = x2.shape

    def body(x_ref, t_ref, g_ref, l_ref, dx_ref, dg_ref):
        y, vjp = jax.vjp(_rms_fn, x_ref[...], g_ref[...])
        err = y - t_ref[...]
        dx, dg = vjp(err * (1.0 / dm))

        @pl.when(pl.program_id(0) == 0)
        def _():
            l_ref[...] = jnp.zeros(l_ref.shape, F32)
            dg_ref[...] = jnp.zeros(dg_ref.shape, F32)

        part = 0.5 * jnp.sum(jnp.mean(err * err, axis=-1, keepdims=True), axis=0, keepdims=True)
        l_ref[...] += jnp.broadcast_to(part, l_ref.shape)
        dx_ref[...] = dx
        dg_ref[...] += dg

    return pl.pallas_call(
        body, name="final_loss", grid=(tn // tb,),
        in_specs=[_blk_spec(x2, tb), _blk_spec(tgt, tb), _full_spec(g)],
        out_specs=[pl.BlockSpec((8, 128), lambda i: (0, 0)), _blk_spec(x2, tb), _full_spec(g)],
        out_shape=[jax.ShapeDtypeStruct((8, 128), F32), jax.ShapeDtypeStruct(x2.shape, F32),
                   jax.ShapeDtypeStruct(g.shape, F32)],
        compiler_params=_cparams(1))(x2, tgt, g)


def _pad_cols(w, n):
    return jnp.pad(w, ((0, 0), (0, n - w.shape[1])))


def _pad_rows(w, n):
    return jnp.pad(w, ((0, n - w.shape[0]), (0, 0)))


def _local_step(x, tgt, W, late):
    tn = x.shape[0]
    Wd = RWKV_WIDTH
    row = lambda z: z.reshape(1, -1)
    g_mix, g_ffn, g_fin = row(W['norm_mix_g']), row(W['norm_ffn_g']), row(W['norm_final_g'])

    (h1,) = _tok_fwd("norm_mix_fwd", lambda a, g: (_rms_fn(a, g),), [x], [g_mix], [(D_MODEL,)])
    proj = _mm("proj_fwd", h1, W['w_in_t'], tb=True)
    p_rkv = _Cols(proj, 3 * Wd, 0)
    pre_consts = [row(W['rwkv_mu_w']), row(W['rwkv_mu_a']), row(W['rwkv_mu_g']), row(W['rwkv_mu_r']),
                  row(W['rwkv_mu_k']), row(W['rwkv_mu_v']), row(W['rwkv_w0']),
                  _pad_cols(W['rwkv_w1'], LORA_PAD), _pad_rows(W['rwkv_w2'], LORA_PAD), row(W['rwkv_a0']),
                  _pad_cols(W['rwkv_a1'], LORA_PAD), _pad_rows(W['rwkv_a2'], LORA_PAD),
                  W['rwkv_g1'], W['rwkv_g2'], row(W['rwkv_k_k']), row(W['rwkv_k_a'])]
    r, k, v, lw, nkk, b, g = _pre_a_fwd(h1, p_rkv, pre_consts)
    y_scan, ck, gathered = _cscan_fwd(r, lw, k, v, nkk, b, late)
    w_out, w_gate_t, w_up_t, w_down = [g_.reshape(-1, D_MODEL) for g_ in gathered]
    post_consts = [row(W['rwkv_lnx_w']), row(W['rwkv_lnx_b']), row(W['rwkv_r_k'])]
    (y_rwkv,) = _tok_fwd("rwkv_post_fwd", _rwkv_post_fn, [y_scan, r, k, v, g], post_consts, [(Wd,)],
                         out_dtypes=[BF16])

    pos = jnp.arange(tn, dtype=F32)
    half = RET_HEAD_DIM // 2
    inv_freq = ROPE_BASE ** (-jnp.arange(half, dtype=F32) / half)
    ang = pos[:, None] * inv_freq[None, :]
    cos2 = jnp.concatenate([jnp.cos(ang), jnp.cos(ang)], axis=1)
    sin2 = jnp.concatenate([-jnp.sin(ang), jnp.sin(ang)], axis=1)
    lg = jnp.log(1.0 - 2.0 ** (-5.0 - jnp.arange(RET_HEADS, dtype=F32)))
    lg = jnp.broadcast_to(lg[:, None, None], (RET_HEADS, 1, 128))
    q_p, k_p, g_ret = _Cols(proj, Wd, 3), _Cols(proj, Wd, 4), _Cols(proj, Wd, 6)
    v_col0 = 5 * Wd // RET_HEAD_DIM
    q_rot, k_rot = _tok_fwd("ret_rotary_fwd", _rotary_fn, [cos2, sin2, q_p, k_p], [], [(RET_WIDTH,)] * 2)
    y_ret_raw = _ret_attn_fwd(lg, q_rot, k_rot, proj, v_col0)
    gn_w = row(W['ret_gn_w'])
    (y_ret,) = _tok_fwd("ret_post_fwd", _ret_post_fn, [y_ret_raw, g_ret], [gn_w], [(RET_WIDTH,)],
                        out_dtypes=[BF16])

    ycat = jnp.concatenate([y_rwkv, y_ret], axis=1)
    x1 = _mm("out_proj_fwd", ycat, w_out, add=x)
    (h2,) = _tok_fwd("norm_ffn_fwd", lambda a_, g_: (_rms_fn(a_, g_),), [x1], [g_ffn], [(D_MODEL,)],
                     out_dtypes=[BF16])
    gate = _mm("ffn_gate_fwd", h2, w_gate_t, tb=True)
    up = _mm("ffn_up_fwd", h2, w_up_t, tb=True)
    cw = W['ffn_conv_w']
    cb = row(W['ffn_conv_b'])
    act = _glu_fwd(gate, up, cw, cb)
    x2 = _mm("ffn_down_fwd", act, w_down, add=x1)
    loss8, dx2, dg_fin = _final_loss(x2, tgt, g_fin)

    G = {'norm_final_g': dg_fin}
    dact = _mm("ffn_down_dx", dx2, w_down, tb=True)
    d_down = _mm("ffn_down_dw", act, dx2, ta=True, out_dtype=BF16)
    dgate, dup, dcw, dcb = _glu_bwd(gate, up, dact, cw, cb)
    G['ffn_conv_w'], G['ffn_conv_b'] = dcw, dcb
    dh2 = _mm("ffn_gate_dx", dgate, w_gate_t)
    dh2 = _mm("ffn_up_dx", dup, w_up_t, add=dh2)
    d_gate_t = _mm("ffn_gate_dw", dgate, h2, ta=True, out_dtype=BF16)
    d_up_t = _mm("ffn_up_dw", dup, h2, ta=True, out_dtype=BF16)
    dx1, G['norm_ffn_g'] = _tok_bwd("norm_ffn_bwd", lambda a_, g_: (_rms_fn(a_, g_),), [], [x1], [g_ffn], [dh2], add=dx2)
    dycat = _mm("out_proj_dx", dx1, w_out, tb=True)
    d_out = _mm("out_proj_dw", ycat, dx1, ta=True, out_dtype=BF16)
    late_grads = [z.reshape(N_DEV, -1, D_MODEL) for z in (d_out, d_gate_t, d_up_t, d_down)]
    dy_rwkv, dy_ret = _Cols(dycat, Wd, 0), _Cols(dycat, Wd, 1)

    dyr_raw, dg_ret, G['ret_gn_w'] = _tok_bwd("ret_post_bwd", _ret_post_fn, [], [y_ret_raw, g_ret], [gn_w], [dy_ret],
                                              tok_dtypes=[F32, BF16])
    dq_rot, dk_rot, dv_ret = _ret_attn_bwd(lg, q_rot, k_rot, proj, dyr_raw, v_col0)
    dq_p, dk_p = _tok_bwd("ret_rotary_bwd", _rotary_fn, [cos2, sin2], [q_p, k_p], [], [dq_rot, dk_rot],
                          tok_dtypes=[BF16, BF16])

    dy_scan, dr1, dk1, dv1, dg, G['rwkv_lnx_w'], G['rwkv_lnx_b'], G['rwkv_r_k'] = _tok_bwd(
        "rwkv_post_bwd", _rwkv_post_fn, [], [y_scan, r, k, v, g], post_consts, [dy_rwkv])
    (dr2, dlw, dk2, dv2, dnkk, db), late_parts = _cscan_bwd(r, lw, k, v, nkk, b, dy_scan, ck, late_grads)
    pre_cts = [(dr1, dr2), (dk1, dk2), (dv1, dv2), dlw, dnkk, db, dg]
    pre_out = _pre_a_bwd(h1, p_rkv, pre_consts, pre_cts)
    dh1_a, dp_rkv = pre_out[0], pre_out[1]
    (G['rwkv_mu_w'], G['rwkv_mu_a'], G['rwkv_mu_g'], G['rwkv_mu_r'], G['rwkv_mu_k'], G['rwkv_mu_v'], G['rwkv_w0'],
     dw1, dw2, G['rwkv_a0'], da1, da2, G['rwkv_g1'], G['rwkv_g2'], G['rwkv_k_k'], G['rwkv_k_a']) = pre_out[2:]
    G['rwkv_w1'], G['rwkv_w2'] = dw1[:, :64], dw2[:64, :]
    G['rwkv_a1'], G['rwkv_a2'] = da1[:, :64], da2[:64, :]

    dproj = jnp.concatenate([dp_rkv, dq_p, dk_p, dv_ret.astype(BF16), dg_ret], axis=1)
    dh1 = _mm("proj_dx", dproj, W['w_in_t'], add=dh1_a)
    G['w_in_t'] = _mm("proj_dw", dproj, h1, ta=True, out_dtype=BF16)
    dx, G['norm_mix_g'] = _tok_bwd("norm_mix_bwd", lambda a_, g_: (_rms_fn(a_, g_),), [], [x], [g_mix], [dh1], add=dx1)
    return loss8[0, 0], dx, G, late_parts


def _adamw_block(p_ref, w_ref, m_ref, v_ref, g_ref, d_ref, nm_ref, nv_ref):
    g = p_ref[0].astype(F32)
    for d in range(1, N_DEV):
        g = g + p_ref[d].astype(F32)
    mn = ADAM_B1 * m_ref[...] + (1.0 - ADAM_B1) * g
    vn = ADAM_B2 * v_ref[...] + (1.0 - ADAM_B2) * (g * g)
    m_hat = mn / (1.0 - ADAM_B1 ** ADAM_STEP)
    v_hat = vn / (1.0 - ADAM_B2 ** ADAM_STEP)
    g_ref[...] = g
    d_ref[...] = -ADAM_LR * (m_hat / (jnp.sqrt(v_hat) + ADAM_EPS) + ADAM_WD * w_ref[...])
    nm_ref[...] = mn
    nv_ref[...] = vn


def _adamw_late(items, xs, tb=32):
    n_it, n_x = len(items), len(xs)
    nbs = [it[1].shape[0] // tb for it in items]
    steps = max(nbs)
    cols = items[0][1].shape[1]

    def body(*refs):
        ins, x_src = refs[:4 * n_it], refs[4 * n_it:4 * n_it + n_x]
        outs = refs[4 * n_it + n_x:8 * n_it + n_x]
        x_dst, sems = refs[8 * n_it + n_x:8 * n_it + 2 * n_x], refs[8 * n_it + 2 * n_x:]
        i = pl.program_id(0)

        @pl.when(i == 0)
        def _():
            _scatter_start(x_src, x_dst, sems)

        for j in range(n_it):
            @pl.when(i < nbs[j])
            def _(j=j):
                _adamw_block(*ins[4 * j:4 * j + 4], *outs[4 * j:4 * j + 4])

        @pl.when(i == steps - 1)
        def _():
            _scatter_wait(x_src, x_dst, sems)

    in_specs, out_specs, out_shape, flat = [], [], [], []
    for (parts, w, m, v), nb in zip(items, nbs):
        blk = pl.BlockSpec((tb, cols), lambda i, nb=nb: (jnp.minimum(i, nb - 1), 0))
        in_specs += [pl.BlockSpec((N_DEV, tb, cols), lambda i, nb=nb: (0, jnp.minimum(i, nb - 1), 0)), blk, blk, blk]
        out_specs += [blk] * 4
        out_shape += [jax.ShapeDtypeStruct(w.shape, F32)] * 4
        flat += [parts, w, m, v]
    res = pl.pallas_call(
        body, name="adamw_late", grid=(steps,), in_specs=in_specs + [_ANY] * n_x, out_specs=out_specs + [_ANY] * n_x,
        out_shape=out_shape + _xchg_out_shapes(xs, True), scratch_shapes=_xchg_sems(n_x),
        compiler_params=_cparams(1))(*flat, *xs)
    return [res[4 * j:4 * j + 4] for j in range(n_it)], res[4 * n_it:]


def _adamw(name, parts, w, m, v):
    rows, cols = w.shape
    sub = 8 * 4 // parts.dtype.itemsize
    tb = max(t for t in range(sub, 65, sub) if rows % t == 0) if rows > 64 else rows
    body = functools.partial(_adamw_block)
    spec = pl.BlockSpec((tb, cols), lambda i: (i, 0))
    sh = jax.ShapeDtypeStruct((rows, cols), F32)
    return pl.pallas_call(
        body, name=name, grid=(rows // tb,),
        in_specs=[pl.BlockSpec((N_DEV, tb, cols), lambda i: (0, i, 0)), spec, spec, spec],
        out_specs=[spec] * 4, out_shape=[sh] * 4, compiler_params=_cparams(1))(parts, w, m, v)


def _local_shape(name):
    gs, ax = SHARDED[name]
    ls = list(gs)
    ls[ax] //= N_DEV
    return tuple(ls)


def _seg(flat, seg):
    n = flat.shape[-1]
    pad = _round_up(n, seg) - n
    if pad:
        flat = jnp.pad(flat, [(0, 0)] * (flat.ndim - 1) + [(0, pad)])
    return flat


def _split3(w):
    hi = w.astype(BF16)
    r1 = w - hi.astype(F32)
    mid = r1.astype(BF16)
    lo = (r1 - mid.astype(F32)).astype(BF16)
    return hi, mid, lo


def _pack_small_shards(shards):
    pieces = []
    for name in SMALL_NAMES:
        flat = shards[name].reshape(-1)
        if name == 'ffn_conv_w':
            pieces += [_seg(p, BF16_SEG) for p in _split3(flat)]
        else:
            pieces.append(flat.astype(BF16))
    return jnp.concatenate(pieces).reshape(-1, 128)


def _unpack_small(gathered):
    flat = gathered.reshape(N_DEV, -1)
    out, off = {}, 0
    for name in SMALL_NAMES:
        gs, ax = SHARDED[name]
        ls = _local_shape(name)
        n = int(np.prod(ls))
        if name == 'ffn_conv_w':
            nseg = _round_up(n, BF16_SEG)
            hi, mid, lo = (flat[:, off + j * nseg: off + j * nseg + n].astype(F32) for j in range(3))
            sh = ((hi + mid) + lo).reshape(N_DEV, 3, -1)
            out[name] = jnp.swapaxes(sh, 0, 1).reshape(3, D_FF)
            off += 3 * nseg
        else:
            sh = flat[:, off:off + n].reshape((N_DEV,) + ls[1:])
            out[name] = sh.reshape(gs[1:]) if ax == 1 else jnp.swapaxes(sh, 0, 1).reshape(gs[1:])
            off += n
    return out


def _small_pieces(sharded, repl):
    return [sharded[n].reshape(-1) for n in SMALL_NAMES] + [repl[n].reshape(-1) for n in REPL_NAMES]


def _pack_small_local(d):
    flat = jnp.concatenate(_small_pieces(d, d))
    return _seg(flat, F32_SEG).reshape(-1, 128)


def _pack_small_grads(G):
    pieces = []
    for name in SMALL_NAMES:
        gs, ax = SHARDED[name]
        g = G[name]
        if name == 'ffn_conv_w':
            sh = jnp.swapaxes(g.reshape(3, N_DEV, -1), 0, 1)
        elif ax == 1:
            sh = g
        else:
            sh = jnp.swapaxes(g.reshape(g.shape[0], N_DEV, -1), 0, 1)
        pieces.append(sh.reshape(N_DEV, -1))
    rep = jnp.concatenate([G[n].reshape(-1) for n in REPL_NAMES])
    pieces.append(jnp.broadcast_to(rep[None, :], (N_DEV, rep.shape[0])))
    flat = _seg(jnp.concatenate(pieces, axis=1), F32_SEG)
    return flat.reshape(N_DEV, -1, 128)


def _unpack_small_local(packed, local_shapes):
    flat = packed.reshape(-1)
    out, off = {}, 0
    for name in SMALL_NAMES + REPL_NAMES:
        n = int(np.prod(local_shapes[name]))
        out[name] = flat[off:off + n].reshape(local_shapes[name])
        off += n
    return out


def kernel(x, *rest):
    nw = len(WEIGHT_NAMES)
    assert len(rest) == 3 * nw + 1
    weights = dict(zip(WEIGHT_NAMES, rest[:nw]))
    loss_target = rest[nw]
    moms = dict(zip(WEIGHT_NAMES, rest[nw + 1:2 * nw + 1]))
    vars_ = dict(zip(WEIGHT_NAMES, rest[2 * nw + 1:]))
    local_shapes = {n: weights[n].shape for n in WEIGHT_NAMES}

    def native2d(name, a):
        a2 = a.reshape(a.shape[-2], a.shape[-1])
        return a2.T if name in BIG_T else a2

    def from2d(name, a2):
        return (a2.T if name in BIG_T else a2).reshape(local_shapes[name])

    big_w = {n: native2d(n, weights[n]) for n in BIG_NAMES}
    w_in_t_sh = big_w['w_in'].astype(BF16)
    late = [big_w[n].astype(BF16) for n in LATE_NAMES]
    small_sh = _pack_small_shards({n: weights[n] for n in SMALL_NAMES})
    w_in_g, small_g = _exchange("weights_all_gather", [w_in_t_sh, small_sh], False)
    W = _unpack_small(small_g)
    W['w_in_t'] = w_in_g.reshape(-1, D_MODEL)
    for n in REPL_NAMES:
        W[n] = weights[n][0] if n != 'norm_final_g' else weights[n]

    loss, dx, G, late_parts = _local_step(x[0], loss_target[0], W, late)

    late_items = [(parts, big_w[n], native2d(n, moms[n]), native2d(n, vars_[n]))
                  for n, parts in zip(LATE_NAMES, late_parts)]
    late_res, (w_in_parts, small_parts) = _adamw_late(
        late_items, [G['w_in_t'].reshape(N_DEV, -1, D_MODEL), _pack_small_grads(G)])
    results = {n: [from2d(n, r) for r in res] for n, res in zip(LATE_NAMES, late_res)}
    res = _adamw("adamw_w_in", w_in_parts, big_w['w_in'], native2d('w_in', moms['w_in']), native2d('w_in', vars_['w_in']))
    results['w_in'] = [from2d('w_in', r) for r in res]
    small_res = _adamw("adamw_small", small_parts, _pack_small_local(weights), _pack_small_local(moms),
                       _pack_small_local(vars_))
    small_out = [_unpack_small_local(p, local_shapes) for p in small_res]

    loss = lax.psum(loss, ("x", "y", "c"))
    outs = [loss, dx[None]]
    for j in range(4):
        outs += [results[n][j] if n in results else small_out[j][n] for n in WEIGHT_NAMES]
    return tuple(outs)
```

```python
import functools
import math

import numpy as np
import jax
import jax.numpy as jnp
from jax import lax
from jax.experimental import pallas as pl
from jax.experimental.pallas import tpu as pltpu

F32 = jnp.float32
BF16 = jnp.bfloat16

N_DEV = 8
D_MODEL = 1024
RWKV_HEADS = 8
RWKV_HEAD_DIM = 64
RWKV_WIDTH = 512
RET_HEADS = 4
RET_HEAD_DIM = 128
RET_WIDTH = 512
LORA_PAD = 128
D_FF = 2816
NORM_EPS = 1e-6
RWKV_GN_EPS = 64e-5
RET_GN_EPS = 1e-5
ROPE_BASE = 10000.0
ADAM_LR, ADAM_B1, ADAM_B2, ADAM_EPS, ADAM_WD, ADAM_STEP = 0.001, 0.9, 0.999, 1e-08, 0.01, 10

VMEM_LIMIT = 56 * 1024 * 1024
TOK_BLOCK = 256
LIGHT_TOK_BLOCK = 512
SCAN_CHUNK = 64
ATT_BLOCK = 512
BF16_SEG = 2048
F32_SEG = 1024

WEIGHT_NAMES = ['norm_mix_g', 'w_in', 'rwkv_mu_r', 'rwkv_mu_k', 'rwkv_mu_v', 'rwkv_mu_w', 'rwkv_mu_a',
                'rwkv_mu_g', 'rwkv_w0', 'rwkv_w1', 'rwkv_w2', 'rwkv_a0', 'rwkv_a1', 'rwkv_a2', 'rwkv_g1',
                'rwkv_g2', 'rwkv_k_k', 'rwkv_k_a', 'rwkv_r_k', 'rwkv_lnx_w', 'rwkv_lnx_b', 'ret_gn_w',
                'w_out', 'norm_ffn_g', 'ffn_w_gate', 'ffn_w_up', 'ffn_conv_w', 'ffn_conv_b', 'ffn_w_down',
                'norm_final_g']
SHARDED = {
    'w_in': ((1, 1024, 3584), 2), 'rwkv_w1': ((1, 1024, 64), 1), 'rwkv_w2': ((1, 64, 512), 2),
    'rwkv_a1': ((1, 1024, 64), 1), 'rwkv_a2': ((1, 64, 512), 2), 'rwkv_g1': ((1, 1024, 128), 1),
    'rwkv_g2': ((1, 128, 512), 2), 'w_out': ((1, 1024, 1024), 1), 'ffn_w_gate': ((1, 1024, 2816), 2),
    'ffn_w_up': ((1, 1024, 2816), 2), 'ffn_conv_w': ((1, 3, 1, 2816), 3), 'ffn_w_down': ((1, 2816, 1024), 1),
}
REPL_NAMES = [n for n in WEIGHT_NAMES if n not in SHARDED]
BIG_NAMES = ['w_in', 'w_out', 'ffn_w_gate', 'ffn_w_up', 'ffn_w_down']
BIG_T = ('w_in', 'ffn_w_gate', 'ffn_w_up')
LATE_NAMES = ['w_out', 'ffn_w_gate', 'ffn_w_up', 'ffn_w_down']
SMALL_NAMES = [n for n in WEIGHT_NAMES if n in SHARDED and n not in BIG_NAMES]


def _cparams(n_grid):
    return pltpu.CompilerParams(dimension_semantics=("arbitrary",) * n_grid, vmem_limit_bytes=VMEM_LIMIT)


def _round_up(n, m):
    return (n + m - 1) // m * m


@jax.custom_vjp
def _bdot(x, w):
    return jnp.dot(x.astype(BF16), w.astype(BF16), preferred_element_type=F32)


def _bdot_fwd(x, w):
    return _bdot(x, w), (x, w)


def _bdot_bwd(res, g):
    x, w = res
    gb = g.astype(BF16)
    dx = lax.dot_general(gb, w.astype(BF16), (((1,), (1,)), ((), ())), preferred_element_type=F32)
    dw = lax.dot_general(x.astype(BF16), gb, (((0,), (0,)), ((), ())), preferred_element_type=F32)
    return dx, dw.astype(w.dtype)


_bdot.defvjp(_bdot_fwd, _bdot_bwd)


@jax.custom_vjp
def _shift_rows(x, prev):
    rolled = pltpu.roll(x, 1, 0)
    row = lax.broadcasted_iota(jnp.int32, x.shape, 0)
    return jnp.where(row == 0, jnp.broadcast_to(prev, x.shape), rolled)


def _shift_rows_fwd(x, prev):
    return _shift_rows(x, prev), None


def _shift_rows_bwd(_, g):
    n = g.shape[0]
    rolled = pltpu.roll(g, n - 1, 0)
    row = lax.broadcasted_iota(jnp.int32, g.shape, 0)
    return jnp.where(row == n - 1, 0.0, rolled), g[0:1, :]


_shift_rows.defvjp(_shift_rows_fwd, _shift_rows_bwd)


@jax.custom_vjp
def _swap_halves(x):
    return pltpu.roll(x, 64, 1)


_swap_halves.defvjp(lambda x: (_swap_halves(x), None), lambda _, g: (pltpu.roll(g, 64, 1),))


def _sigmoid(x):
    return 1.0 / (1.0 + jnp.exp(-x))


def _softplus(x):
    return jnp.maximum(x, 0.0) + jnp.log(1.0 + jnp.exp(-jnp.abs(x)))


def _rms_fn(x, g):
    return x * lax.rsqrt(jnp.mean(x * x, axis=-1, keepdims=True) + NORM_EPS) * g


def _pre_a_fn(h1, h1p, p, pp, mu_w, mu_a, mu_g, mu_r, mu_k, mu_v, w0, w1, w2, a0, a1, a2, g1, g2, k_k, k_a):
    W = RWKV_WIDTH
    h1s = _shift_rows(h1, h1p)
    ps = _shift_rows(p, pp)
    dx = h1s - h1
    xw = h1 + dx * mu_w
    xa = h1 + dx * mu_a
    xg = h1 + dx * mu_g
    dp = ps - p
    r = p[:, 0:W] + dp[:, 0:W] * mu_r
    k0 = p[:, W:2 * W] + dp[:, W:2 * W] * mu_k
    v = p[:, 2 * W:3 * W] + dp[:, 2 * W:3 * W] * mu_v
    wl = w0 + _bdot(jnp.tanh(_bdot(xw, w1)), w2)
    w_log = -_softplus(-wl) - 0.5
    lw = -jnp.exp(w_log)
    a = _sigmoid(a0 + _bdot(_bdot(xa, a1), a2))
    g = _bdot(_sigmoid(_bdot(xg, g1)), g2)
    nkk, k, b = _pre_b_fn(k0, a, k_k, k_a)
    return r, k, v, lw, nkk, b, g


def _head_sum_raw(x):
    n = x.shape[1]
    ii = lax.broadcasted_iota(jnp.int32, (n, n), 0) // RWKV_HEAD_DIM
    jj = lax.broadcasted_iota(jnp.int32, (n, n), 1) // RWKV_HEAD_DIM
    ones = (ii == jj).astype(BF16)
    xh = x.astype(BF16)
    xl = (x - xh.astype(F32)).astype(BF16)
    return jnp.dot(xh, ones, preferred_element_type=F32) + jnp.dot(xl, ones, preferred_element_type=F32)


@jax.custom_vjp
def _head_sum(x):
    return _head_sum_raw(x)


_head_sum.defvjp(lambda x: (_head_sum_raw(x), None), lambda _, g: (_head_sum_raw(g),))


def _pre_b_fn(k0, a, k_k, k_a):
    kkr = k0 * k_k
    nrm = jnp.sqrt(_head_sum(kkr * kkr))
    kk = kkr / jnp.maximum(nrm, 1e-12)
    k = k0 * (1.0 + (a - 1.0) * k_a)
    return -kk, k, kk * a


def _rwkv_post_fn(y, r, k, v, g, lnx_w, lnx_b, r_k):
    inv = 1.0 / RWKV_HEAD_DIM
    mu = _head_sum(y) * inv
    yc = y - mu
    var = _head_sum(yc * yc) * inv
    yn = yc * lax.rsqrt(var + RWKV_GN_EPS) * lnx_w + lnx_b
    bonus = _head_sum(r * k * r_k) * v
    return ((yn + bonus) * g,)


def _rotary_fn(cos2, sin2, q, k):
    qs, ks = [], []
    for h in range(RET_HEADS):
        sl = slice(h * RET_HEAD_DIM, (h + 1) * RET_HEAD_DIM)
        qh, kh = q[:, sl], k[:, sl]
        qs.append(qh * cos2 + _swap_halves(qh) * sin2)
        ks.append((kh * cos2 + _swap_halves(kh) * sin2) * (RET_HEAD_DIM ** -0.5))
    return jnp.concatenate(qs, axis=1), jnp.concatenate(ks, axis=1)


def _ret_post_fn(y, gp, gn_w):
    outs = []
    for h in range(RET_HEADS):
        sl = slice(h * RET_HEAD_DIM, (h + 1) * RET_HEAD_DIM)
        yh = y[:, sl]
        mu = jnp.mean(yh, axis=-1, keepdims=True)
        yc = yh - mu
        var = jnp.mean(yc * yc, axis=-1, keepdims=True)
        outs.append(yc * lax.rsqrt(var + RET_GN_EPS) * gn_w[:, sl])
    yn = jnp.concatenate(outs, axis=1)
    return (gp * _sigmoid(gp) * yn,)


class _Cols:
    def __init__(self, array, width, block):
        self.array, self.width, self.block = array, width, block
        self.shape, self.ndim, self.dtype = (array.shape[0], width), 2, array.dtype


def _arr(a):
    return a.array if isinstance(a, _Cols) else a


def _blk_spec(a, tb, rev_nb=None):
    nd = a.ndim
    tail = (a.block,) if isinstance(a, _Cols) else (0,) * (nd - 1)
    if rev_nb is None:
        return pl.BlockSpec((tb,) + a.shape[1:], lambda i: (i,) + tail)
    return pl.BlockSpec((tb,) + a.shape[1:], lambda i: (rev_nb - 1 - i,) + tail)


def _full_spec(a):
    nd = a.ndim
    return pl.BlockSpec(a.shape, lambda i: (0,) * nd)


def _tok_fwd(name, fn, toks, consts, out_tails, tb=LIGHT_TOK_BLOCK, out_dtypes=None):
    out_dtypes = out_dtypes or [F32] * len(out_tails)
    n_in = len(toks) + len(consts)
    tn = toks[0].shape[0]

    def body(*refs):
        outs = fn(*[r[...] for r in refs[:n_in]])
        for r, o in zip(refs[n_in:], outs):
            r[...] = o.astype(r.dtype)

    out_shape = [jax.ShapeDtypeStruct((tn,) + tuple(s), dt) for s, dt in zip(out_tails, out_dtypes)]
    return pl.pallas_call(
        body, name=name, grid=(tn // tb,),
        in_specs=[_blk_spec(a, tb) for a in toks] + [_full_spec(c) for c in consts],
        out_specs=[_blk_spec(o, tb) for o in out_shape], out_shape=out_shape,
        compiler_params=_cparams(1))(*[_arr(a) for a in toks], *consts)


def _tok_bwd(name, fn, aux, toks, consts, cts, add=None, tb=LIGHT_TOK_BLOCK, tok_dtypes=None):
    n_aux, n_tok, n_c = len(aux), len(toks), len(consts)
    ct_groups = [c if isinstance(c, (tuple, list)) else (c,) for c in cts]
    ct_flat = [a for grp in ct_groups for a in grp]
    n_ct = len(ct_flat)
    n_add = 0 if add is None else 1
    tn = toks[0].shape[0]

    def body(*refs):
        pos = 0
        aux_v = [r[...] for r in refs[pos:pos + n_aux]]; pos += n_aux
        tok_v = [r[...] for r in refs[pos:pos + n_tok]]; pos += n_tok
        const_v = [r[...] for r in refs[pos:pos + n_c]]; pos += n_c
        ct_refs = refs[pos:pos + n_ct]; pos += n_ct
        add_refs = refs[pos:pos + n_add]; pos += n_add
        dtok_refs = refs[pos:pos + n_tok]; pos += n_tok
        dconst_refs = refs[pos:pos + n_c]
        ct_v, q = [], 0
        for grp in ct_groups:
            s = ct_refs[q][...]
            for r in ct_refs[q + 1:q + len(grp)]:
                s = s + r[...]
            q += len(grp)
            ct_v.append(s)
        _, vjp = jax.vjp(lambda *tc: fn(*aux_v, *tc), *tok_v, *const_v)
        grads = vjp(tuple(ct_v))
        for j, r in enumerate(dtok_refs):
            gj = grads[j]
            if j == 0 and n_add:
                gj = gj + add_refs[0][...]
            r[...] = gj.astype(r.dtype)

        @pl.when(pl.program_id(0) == 0)
        def _():
            for r in dconst_refs:
                r[...] = jnp.zeros(r.shape, F32)

        for j, r in enumerate(dconst_refs):
            r[...] += grads[n_tok + j]

    ins = list(aux) + list(toks) + list(consts) + ct_flat + ([add] if n_add else [])
    in_specs = ([_blk_spec(a, tb) for a in aux] + [_blk_spec(a, tb) for a in toks] + [_full_spec(c) for c in consts]
                + [_blk_spec(a, tb) for a in ct_flat] + ([_blk_spec(add, tb)] if n_add else []))
    tok_dtypes = tok_dtypes or [F32] * n_tok
    out_shape = ([jax.ShapeDtypeStruct(a.shape, dt) for a, dt in zip(toks, tok_dtypes)]
                 + [jax.ShapeDtypeStruct(c.shape, F32) for c in consts])
    out_specs = [_blk_spec(o, tb) for o in out_shape[:n_tok]] + [_full_spec(c) for c in consts]
    return pl.pallas_call(body, name=name, grid=(tn // tb,), in_specs=in_specs, out_specs=out_specs,
                          out_shape=out_shape, compiler_params=_cparams(1))(*[_arr(a) for a in ins])


MM_VMEM_BUDGET = 40 * 1024 * 1024
MM_STEP_SECONDS = 0.4e-6
MM_HBM_BYTES_PER_SECOND = 2.5e12
MM_XPOSE_SECONDS_PER_ELEM = 2e-12
MM_MXU_COLUMNS = 256
MM_MXU_FLOPS = 9e14


def _mm_tiles(m, n, kd, a_bytes, b_bytes, o_bytes, has_add, ta):
    divs = lambda d: [t for t in range(128, d + 1, 128) if d % t == 0]
    best = None
    for tm in divs(m):
        for tn in divs(n):
            for tk in divs(kd):
                ni, nj, nk = m // tm, n // tn, kd // tk
                vmem = (2 * tm * tk * a_bytes + 2 * tk * tn * b_bytes + tm * tn * 4 + 2 * tm * tn * o_bytes
                        + (2 * tm * tn * 4 if has_add else 0) + 2 * (tm * tk + tk * tn) + tm * tn * 4)
                if vmem > MM_VMEM_BUDGET:
                    continue
                a_traffic = m * kd * a_bytes * (nj if nk > 1 else 1)
                b_traffic = kd * n * b_bytes * (ni if nj * nk > 1 else 1)
                cost = ni * nj * nk * MM_STEP_SECONDS + (a_traffic + b_traffic) / MM_HBM_BYTES_PER_SECOND
                cost += 2.0 * m * kd * nj * max(tn, MM_MXU_COLUMNS) / MM_MXU_FLOPS
                if ta:
                    cost += m * kd * nj * MM_XPOSE_SECONDS_PER_ELEM
                if best is None or cost < best[0]:
                    best = (cost, tm, tn, tk)
    return best[1:]


def _mm(name, a, b, ta=False, tb=False, add=None, out_dtype=F32, xs=()):
    n_x = len(xs)
    if ta:
        kd, m = a.shape
    else:
        m, kd = a.shape
    if tb:
        n, kb = b.shape
    else:
        kb, n = b.shape
    assert kd == kb, (a.shape, b.shape)
    tm, tn, tk = _mm_tiles(m, n, kd, a.dtype.itemsize, b.dtype.itemsize, jnp.dtype(out_dtype).itemsize,
                           add is not None, ta)
    nk = kd // tk
    has_add = add is not None
    dims = (((0 if ta else 1,), (1 if tb else 0,)), ((), ()))

    n_in = 3 if add is not None else 2
    nj = n // tn
    n_steps = (m // tm) * nj * nk

    def body(*refs):
        a_ref, b_ref = refs[0], refs[1]
        x_src, o_ref = refs[n_in:n_in + n_x], refs[n_in + n_x]
        x_dst, acc_ref, sems = refs[n_in + n_x + 1:n_in + 2 * n_x + 1], refs[n_in + 2 * n_x + 1], refs[n_in + 2 * n_x + 2:]
        k = pl.program_id(2)
        step = (pl.program_id(0) * nj + pl.program_id(1)) * nk + k

        if n_x:
            @pl.when(step == 0)
            def _():
                _scatter_start(x_src, x_dst, sems)

        @pl.when(k == 0)
        def _():
            acc_ref[...] = refs[2][...] if has_add else jnp.zeros(acc_ref.shape, F32)

        acc_ref[...] += lax.dot_general(a_ref[...].astype(BF16), b_ref[...].astype(BF16), dims,
                                        preferred_element_type=F32)

        @pl.when(k == nk - 1)
        def _():
            o_ref[...] = acc_ref[...].astype(out_dtype)

        if n_x:
            @pl.when(step == n_steps - 1)
            def _():
                _scatter_wait(x_src, x_dst, sems)

    a_spec = pl.BlockSpec((tk, tm), lambda i, j, k: (k, i)) if ta else pl.BlockSpec((tm, tk), lambda i, j, k: (i, k))
    b_spec = pl.BlockSpec((tn, tk), lambda i, j, k: (j, k)) if tb else pl.BlockSpec((tk, tn), lambda i, j, k: (k, j))
    o_spec = pl.BlockSpec((tm, tn), lambda i, j, k: (i, j))
    ins = [a, b] + ([add] if has_add else [])
    in_specs = [a_spec, b_spec] + ([o_spec] if has_add else [])
    res = pl.pallas_call(body, name=name, grid=(m // tm, n // tn, nk), in_specs=in_specs + [_ANY] * n_x,
                         out_specs=[o_spec] + [_ANY] * n_x,
                         out_shape=[jax.ShapeDtypeStruct((m, n), out_dtype)] + _xchg_out_shapes(xs, True),
                         scratch_shapes=[pltpu.VMEM((tm, tn), F32)] + (_xchg_sems(n_x) if n_x else []),
                         compiler_params=_cparams(3))(*ins, *xs)
    return (res[0], res[1:]) if n_x else res[0]


def _prev8_spec(a, tb, rev_nb=None):
    r = tb // 8
    if rev_nb is None:
        return pl.BlockSpec((8, a.shape[1]), lambda i: (jnp.maximum(i * r - 1, 0), 0))
    return pl.BlockSpec((8, a.shape[1]), lambda i: (jnp.maximum((rev_nb - 1 - i) * r - 1, 0), 0))


def _pre_a_fwd(h1, p, consts, tb=TOK_BLOCK):
    tn = h1.shape[0]

    def body(h1_ref, h1h_ref, p_ref, ph_ref, *rest):
        c_refs, o_refs = rest[:len(consts)], rest[len(consts):]
        first = pl.program_id(0) == 0
        h1p = jnp.where(first, 0.0, h1h_ref[7:8, :])
        pp = jnp.where(first, 0.0, ph_ref[7:8, :])
        outs = _pre_a_fn(h1_ref[...], h1p, p_ref[...], pp, *[c[...] for c in c_refs])
        for r, o in zip(o_refs, outs):
            r[...] = o

    out_shape = [jax.ShapeDtypeStruct((tn, RWKV_WIDTH), F32) for _ in range(7)]
    return pl.pallas_call(
        body, name="rwkv_pre_a_fwd", grid=(tn // tb,),
        in_specs=[_blk_spec(h1, tb), _prev8_spec(h1, tb), _blk_spec(p, tb), _prev8_spec(p, tb)]
        + [_full_spec(c) for c in consts],
        out_specs=[_blk_spec(o, tb) for o in out_shape], out_shape=out_shape,
        compiler_params=_cparams(1))(h1, h1, _arr(p), _arr(p), *consts)


def _pre_a_bwd(h1, p, consts, cts, tb=TOK_BLOCK):
    tn = h1.shape[0]
    nb = tn // tb
    n_c = len(consts)
    ct_groups = [c if isinstance(c, (tuple, list)) else (c,) for c in cts]
    ct_flat = [a for grp in ct_groups for a in grp]
    n_ct = len(ct_flat)

    def body(*refs):
        h1_ref, h1h_ref, p_ref, ph_ref = refs[:4]
        c_refs = refs[4:4 + n_c]
        ct_refs = refs[4 + n_c:4 + n_c + n_ct]
        dh1_ref, dp_ref = refs[4 + n_c + n_ct:6 + n_c + n_ct]
        dc_refs = refs[6 + n_c + n_ct:6 + 2 * n_c + n_ct]
        ch_ref, cp_ref = refs[-2], refs[-1]
        i = pl.program_id(0)
        first_block = i == nb - 1
        h1p = jnp.where(first_block, 0.0, h1h_ref[7:8, :])
        pp = jnp.where(first_block, 0.0, ph_ref[7:8, :])
        ct_v, q = [], 0
        for grp in ct_groups:
            s = ct_refs[q][...]
            for r in ct_refs[q + 1:q + len(grp)]:
                s = s + r[...]
            q += len(grp)
            ct_v.append(s)
        _, vjp = jax.vjp(_pre_a_fn, h1_ref[...], h1p, p_ref[...], pp, *[c[...] for c in c_refs])
        grads = vjp(tuple(ct_v))

        @pl.when(i == 0)
        def _():
            ch_ref[...] = jnp.zeros(ch_ref.shape, F32)
            cp_ref[...] = jnp.zeros(cp_ref.shape, F32)
            for r in dc_refs:
                r[...] = jnp.zeros(r.shape, F32)

        rowh = lax.broadcasted_iota(jnp.int32, (tb, h1.shape[1]), 0)
        rowp = lax.broadcasted_iota(jnp.int32, (tb, p.shape[1]), 0)
        dh1_ref[...] = grads[0] + jnp.where(rowh == tb - 1, jnp.broadcast_to(ch_ref[0:1, :], rowh.shape), 0.0)
        dp_ref[...] = (grads[2] + jnp.where(rowp == tb - 1, jnp.broadcast_to(cp_ref[0:1, :], rowp.shape), 0.0)
                       ).astype(dp_ref.dtype)
        ch_ref[0:1, :] = grads[1]
        cp_ref[0:1, :] = grads[3]
        for j, r in enumerate(dc_refs):
            r[...] += grads[4 + j]

    ins = [h1, h1, _arr(p), _arr(p)] + list(consts) + ct_flat
    in_specs = ([_blk_spec(h1, tb, nb), _prev8_spec(h1, tb, nb), _blk_spec(p, tb, nb), _prev8_spec(p, tb, nb)]
                + [_full_spec(c) for c in consts] + [_blk_spec(a, tb, nb) for a in ct_flat])
    out_shape = ([jax.ShapeDtypeStruct(h1.shape, F32), jax.ShapeDtypeStruct(p.shape, BF16)]
                 + [jax.ShapeDtypeStruct(c.shape, F32) for c in consts])
    out_specs = [_blk_spec(h1, tb, nb), _blk_spec(p, tb, nb)] + [_full_spec(c) for c in consts]
    return pl.pallas_call(body, name="rwkv_pre_a_bwd", grid=(nb,), in_specs=in_specs, out_specs=out_specs,
                          out_shape=out_shape,
                          scratch_shapes=[pltpu.VMEM((8, h1.shape[1]), F32), pltpu.VMEM((8, p.shape[1]), F32)],
                          compiler_params=_cparams(1))(*ins)


def _my_index():
    return 4 * lax.axis_index("x") + 2 * lax.axis_index("y") + lax.axis_index("c")


def _peer(k):
    x, y, c = lax.axis_index("x"), lax.axis_index("y"), lax.axis_index("c")
    px = 1 - x if k & 4 else x
    py = 1 - y if k & 2 else y
    pc = 1 - c if k & 1 else c
    return (px, py, pc), 4 * px + 2 * py + pc


def _xchg_sems(n):
    return [pltpu.SemaphoreType.DMA((n * (N_DEV - 1),)), pltpu.SemaphoreType.DMA((n * (N_DEV - 1),)),
            pltpu.SemaphoreType.DMA((n,))]


def _scatter_copies(srcs, dsts, sems, incoming=False):
    send_sems, recv_sems, local_sems = sems
    me = _my_index()
    local, remote = [], []
    for i, (s, d) in enumerate(zip(srcs, dsts)):
        if not incoming:
            local.append(pltpu.make_async_copy(s.at[me], d.at[me], local_sems.at[i]))
        for k in range(1, N_DEV):
            peer, plin = _peer(k)
            j = i * (N_DEV - 1) + k - 1
            s_slot, d_slot = (me, plin) if incoming else (plin, me)
            remote.append(pltpu.make_async_remote_copy(
                src_ref=s.at[s_slot], dst_ref=d.at[d_slot], send_sem=send_sems.at[j],
                recv_sem=recv_sems.at[j], device_id=peer, device_id_type=pl.DeviceIdType.MESH))
    return local, remote


def _scatter_start(srcs, dsts, sems):
    local, out = _scatter_copies(srcs, dsts, sems)
    for cp in local + out:
        cp.start()


def _scatter_wait(srcs, dsts, sems):
    for cp in _scatter_copies(srcs, dsts, sems, incoming=True)[1]:
        cp.wait_recv()
    local, out = _scatter_copies(srcs, dsts, sems)
    for cp in out:
        cp.wait_send()
    for cp in local:
        cp.wait()


_ICI_PEERS = (2, 4, 6)


def _gather_copies(srcs, dsts, sems, group):
    send_sems, recv_sems, local_sems = sems
    me = _my_index()
    sib, sib_lin = _peer(1)
    out = []
    for i, (s, d) in enumerate(zip(srcs, dsts)):
        def mk(q, src, dst, dev):
            j = i * (N_DEV - 1) + q
            return pltpu.make_async_remote_copy(src_ref=src, dst_ref=dst, send_sem=send_sems.at[j],
                                                recv_sem=recv_sems.at[j], device_id=dev,
                                                device_id_type=pl.DeviceIdType.MESH)
        if group == 'local':
            out.append(pltpu.make_async_copy(s, d.at[me], local_sems.at[i]))
        elif group == 'own':
            out.append(mk(0, s, d.at[me], sib))
        elif group == 'in_d2d':
            out.append(mk(0, s, d.at[sib_lin], sib))
        for jj, k in enumerate(_ICI_PEERS):
            peer, plin = _peer(k)
            plin_other = _peer(k + 1)[1]
            if group == 'own':
                out.append(mk(1 + jj, s, d.at[me], peer))
            elif group == 'in_ici':
                out.append(mk(1 + jj, s, d.at[plin], peer))
            elif group == 'pass_on':
                out.append(mk(4 + jj, d.at[plin], d.at[plin], sib))
            elif group == 'in_d2d':
                out.append(mk(4 + jj, d.at[plin_other], d.at[plin_other], sib))
    return out


def _gather_start(srcs, dsts, sems):
    for cp in _gather_copies(srcs, dsts, sems, 'local') + _gather_copies(srcs, dsts, sems, 'own'):
        cp.start()


def _gather_pass_on(srcs, dsts, sems):
    for cp in _gather_copies(srcs, dsts, sems, 'in_ici'):
        cp.wait_recv()
    for cp in _gather_copies(srcs, dsts, sems, 'pass_on'):
        cp.start()


def _gather_finish(srcs, dsts, sems):
    for cp in _gather_copies(srcs, dsts, sems, 'in_d2d'):
        cp.wait_recv()
    for cp in _gather_copies(srcs, dsts, sems, 'own') + _gather_copies(srcs, dsts, sems, 'pass_on'):
        cp.wait_send()
    for cp in _gather_copies(srcs, dsts, sems, 'local'):
        cp.wait()


def _xchg_out_shapes(srcs, scatter):
    return [jax.ShapeDtypeStruct(s.shape if scatter else (N_DEV,) + s.shape, s.dtype) for s in srcs]


_ANY = pl.BlockSpec(memory_space=pl.ANY)


def _exchange(name, srcs, scatter):
    n = len(srcs)

    def body(*refs):
        s, d, sems = refs[:n], refs[n:2 * n], refs[2 * n:]
        if scatter:
            _scatter_start(s, d, sems)
            _scatter_wait(s, d, sems)
        else:
            _gather_start(s, d, sems)
            _gather_pass_on(s, d, sems)
            _gather_finish(s, d, sems)

    return pl.pallas_call(body, name=name, in_specs=[_ANY] * n, out_specs=[_ANY] * n,
                          out_shape=_xchg_out_shapes(srcs, scatter), scratch_shapes=_xchg_sems(n))(*srcs)


_MM_DIMS = {'nn': (((1,), (0,)), ((), ())), 'nt': (((1,), (1,)), ((), ())), 'tn': (((0,), (0,)), ((), ()))}


def _cmm_raw(x, y, kind, split):
    dot = functools.partial(lax.dot_general, dimension_numbers=_MM_DIMS[kind], preferred_element_type=F32)
    xh, yh = x.astype(BF16), y.astype(BF16)
    out = dot(xh, yh)
    if split:
        xl = (x - xh.astype(F32)).astype(BF16)
        yl = (y - yh.astype(F32)).astype(BF16)
        out = out + (dot(xh, yl) + dot(xl, yh))
    return out


@functools.partial(jax.custom_vjp, nondiff_argnums=(2, 3))
def _cmm(x, y, kind, split=False):
    return _cmm_raw(x, y, kind, split)


def _cmm_fwd(x, y, kind, split):
    return _cmm_raw(x, y, kind, split), (x, y)


def _cmm_bwd(kind, split, res, g):
    x, y = res
    if kind == 'nn':
        return _cmm_raw(g, y, 'nt', split), _cmm_raw(x, g, 'tn', split)
    if kind == 'nt':
        return _cmm_raw(g, y, 'nn', split), _cmm_raw(g, x, 'tn', split)
    return _cmm_raw(y, g, 'nt', split), _cmm_raw(x, g, 'nn', split)


_cmm.defvjp(_cmm_fwd, _cmm_bwd)


def _tri_sum_raw(tri, x, kind):
    dot = functools.partial(lax.dot_general, dimension_numbers=_MM_DIMS[kind], preferred_element_type=F32)
    tb = tri.astype(BF16)
    hi, mid, lo = _split3(x)
    return (dot(tb, hi) + dot(tb, mid)) + dot(tb, lo)


@functools.partial(jax.custom_vjp, nondiff_argnums=(2,))
def _tri_sum(tri, x, kind):
    return _tri_sum_raw(tri, x, kind)


def _tri_sum_fwd(tri, x, kind):
    return _tri_sum_raw(tri, x, kind), tri


def _tri_sum_bwd(kind, tri, g):
    return jnp.zeros_like(tri), _tri_sum_raw(tri, g, 'tn' if kind == 'nn' else 'nn')


_tri_sum.defvjp(_tri_sum_fwd, _tri_sum_bwd)


def _chunk_fn(S0, r, lw, k, v, a, b):
    hs = range(len(r))
    C = r[0].shape[0]
    ii = lax.broadcasted_iota(jnp.int32, (C, C), 0)
    jj = lax.broadcasted_iota(jnp.int32, (C, C), 1)
    incl, strict = ii >= jj, ii > jj
    eye = (ii == jj).astype(F32)
    inclf = incl.astype(F32)
    cum = [_tri_sum(inclf, lw[h], 'nn') for h in hs]
    e_inv = [jnp.exp(-cum[h]) for h in hs]
    At = [a[h] * jnp.exp(cum[h] - lw[h]) for h in hs]
    Rt = [r[h] * jnp.exp(cum[h]) for h in hs]
    Kh = [k[h] * e_inv[h] for h in hs]
    Bh = [b[h] * e_inv[h] for h in hs]
    Mab = [jnp.where(strict, _cmm(At[h], Bh[h], 'nt', True), 0.0) for h in hs]
    Mak = [jnp.where(strict, _cmm(At[h], Kh[h], 'nt', True), 0.0) for h in hs]
    Mrk = [jnp.where(incl, _cmm(Rt[h], Kh[h], 'nt', True), 0.0) for h in hs]
    Mrb = [jnp.where(incl, _cmm(Rt[h], Bh[h], 'nt', True), 0.0) for h in hs]
    rhs = [_cmm(At[h], S0[h], 'nt') + _cmm(Mak[h], v[h], 'nn') for h in hs]
    P = Mab
    Tm = [eye + P[h] for h in hs]
    n = 1
    while 2 * n < C:
        P = [_cmm(P[h], P[h], 'nn', True) for h in hs]
        Tm = [_cmm(Tm[h], eye + P[h], 'nn', True) for h in hs]
        n *= 2
    U = [_cmm(Tm[h], rhs[h], 'nn', True) for h in hs]
    Y = [_cmm(Rt[h], S0[h], 'nt') + _cmm(Mrk[h], v[h], 'nn') + _cmm(Mrb[h], U[h], 'nn') for h in hs]
    gC = [jnp.exp(jnp.sum(lw[h], axis=0, keepdims=True)) for h in hs]
    SC = [S0[h] * gC[h] + _cmm(v[h], Kh[h] * gC[h], 'tn') + _cmm(U[h], Bh[h] * gC[h], 'tn') for h in hs]
    return tuple(Y), tuple(SC)


def _cscan_fwd(r, lw, k, v, a, b, xs):
    n_x = len(xs)
    tn = r.shape[0]
    H, Dh, Dv = RWKV_HEADS, RWKV_HEAD_DIM, RWKV_HEAD_DIM
    nc = tn // SCAN_CHUNK
    lanes = lambda h: slice(h * Dh, (h + 1) * Dh)
    heads = lambda ref: tuple(ref[:, lanes(h)] for h in range(H))
    mats = lambda ref: tuple(ref[h] for h in range(H))

    def body(r_ref, lw_ref, k_ref, v_ref, a_ref, b_ref, *rest):
        x_src, (y_ref, ck_ref) = rest[:n_x], rest[n_x:n_x + 2]
        x_dst, s_ref, sems = rest[n_x + 2:2 * n_x + 2], rest[2 * n_x + 2], rest[2 * n_x + 3:]

        @pl.when(pl.program_id(0) == 0)
        def _():
            s_ref[...] = jnp.zeros(s_ref.shape, F32)
            _gather_start(x_src, x_dst, sems)

        ck_ref[0] = s_ref[...]
        y, sc = _chunk_fn(mats(s_ref), heads(r_ref), heads(lw_ref), heads(k_ref), heads(v_ref), heads(a_ref),
                          heads(b_ref))
        for h in range(H):
            y_ref[:, lanes(h)] = y[h]
            s_ref[h] = sc[h]

        @pl.when(pl.program_id(0) == max(nc - 4, 0))
        def _():
            _gather_pass_on(x_src, x_dst, sems)

        @pl.when(pl.program_id(0) == nc - 1)
        def _():
            _gather_finish(x_src, x_dst, sems)

    hm = pl.BlockSpec((SCAN_CHUNK, H * Dh), lambda c: (c, 0))
    res = pl.pallas_call(
        body, name="rwkv_scan_fwd", grid=(nc,), in_specs=[hm] * 6 + [_ANY] * n_x,
        out_specs=[hm, pl.BlockSpec((1, H, Dv, Dh), lambda c: (c, 0, 0, 0))] + [_ANY] * n_x,
        out_shape=[jax.ShapeDtypeStruct((tn, H * Dh), F32), jax.ShapeDtypeStruct((nc, H, Dv, Dh), F32)]
        + _xchg_out_shapes(xs, False),
        scratch_shapes=[pltpu.VMEM((H, Dv, Dh), F32)] + _xchg_sems(n_x),
        compiler_params=_cparams(1))(r, lw, k, v, a, b, *xs)
    return res[0], res[1], res[2:]


def _cscan_bwd(r, lw, k, v, a, b, dy, ck, xs):
    n_x = len(xs)
    tn = r.shape[0]
    H, Dh, Dv = RWKV_HEADS, RWKV_HEAD_DIM, RWKV_HEAD_DIM
    nc = tn // SCAN_CHUNK
    lanes = lambda h: slice(h * Dh, (h + 1) * Dh)
    heads = lambda ref: tuple(ref[:, lanes(h)] for h in range(H))
    mats = lambda ref: tuple(ref[h] for h in range(H))

    def body(r_ref, lw_ref, k_ref, v_ref, a_ref, b_ref, dy_ref, ck_ref, *rest):
        x_src = rest[:n_x]
        d_refs = rest[n_x:n_x + 6]
        x_dst = rest[n_x + 6:2 * n_x + 6]
        g_ref = rest[2 * n_x + 6]
        sems = rest[2 * n_x + 7:]

        @pl.when(pl.program_id(0) == 0)
        def _():
            g_ref[...] = jnp.zeros(g_ref.shape, F32)
            _scatter_start(x_src, x_dst, sems)

        s0 = tuple(ck_ref[0, h] for h in range(H))
        _, vjp = jax.vjp(_chunk_fn, s0, heads(r_ref), heads(lw_ref), heads(k_ref), heads(v_ref), heads(a_ref),
                         heads(b_ref))
        grads = vjp((heads(dy_ref), mats(g_ref)))
        for h in range(H):
            g_ref[h] = grads[0][h]
            for d_ref, gz in zip(d_refs, grads[1:]):
                d_ref[:, lanes(h)] = gz[h]

        @pl.when(pl.program_id(0) == nc - 1)
        def _():
            _scatter_wait(x_src, x_dst, sems)

    hm = pl.BlockSpec((SCAN_CHUNK, H * Dh), lambda c: (nc - 1 - c, 0))
    hshape = jax.ShapeDtypeStruct((tn, H * Dh), F32)
    res = pl.pallas_call(
        body, name="rwkv_scan_bwd", grid=(nc,),
        in_specs=[hm] * 7 + [pl.BlockSpec((1, H, Dv, Dh), lambda c: (nc - 1 - c, 0, 0, 0))] + [_ANY] * n_x,
        out_specs=[hm] * 6 + [_ANY] * n_x, out_shape=[hshape] * 6 + _xchg_out_shapes(xs, True),
        scratch_shapes=[pltpu.VMEM((H, Dv, Dh), F32)] + _xchg_sems(n_x),
        compiler_params=_cparams(1))(r, lw, k, v, a, b, dy, ck, *xs)
    return res[:6], res[6:]


def _decay_mask(lg, i, j, blk):
    rows = lax.broadcasted_iota(jnp.int32, (blk, blk), 0)
    cols = lax.broadcasted_iota(jnp.int32, (blk, blk), 1)
    dd = (rows - cols + (i - j) * blk).astype(F32)
    return jnp.where(dd >= 0.0, jnp.exp(lg * jnp.maximum(dd, 0.0)), 0.0)


_NT = (((1,), (1,)), ((), ()))
_TN = (((0,), (0,)), ((), ()))


def _ret_attn_fwd(lg, q, k, v, v_col0=0, blk=ATT_BLOCK):
    tn = q.shape[0]
    Dh = RET_HEAD_DIM

    def body(lg_ref, q_ref, k_ref, v_ref, o_ref):
        i = pl.program_id(1)
        lgv = lg_ref[0][:, 0:1]
        qb = q_ref[...].astype(BF16)

        def jb(j, acc):
            ks = pl.ds(pl.multiple_of(j * blk, blk), blk)
            s = lax.dot_general(qb, k_ref[ks, :].astype(BF16), _NT, preferred_element_type=F32)
            s = s * _decay_mask(lgv, i, j, blk)
            return acc + jnp.dot(s.astype(BF16), v_ref[ks, :].astype(BF16), preferred_element_type=F32)

        o_ref[...] = lax.fori_loop(0, i + 1, jb, jnp.zeros((blk, Dh), F32))

    full = pl.BlockSpec((tn, Dh), lambda h, i: (0, h))
    qs = pl.BlockSpec((blk, Dh), lambda h, i: (i, h))
    return pl.pallas_call(
        body, name="ret_attn_fwd", grid=(RET_HEADS, tn // blk),
        in_specs=[pl.BlockSpec((1, 1, 128), lambda h, i: (h, 0, 0)), qs, full,
                  pl.BlockSpec((tn, Dh), lambda h, i: (0, v_col0 + h))],
        out_specs=qs, out_shape=jax.ShapeDtypeStruct(q.shape, F32), compiler_params=_cparams(2))(lg, q, k, v)


def _ret_attn_bwd(lg, q, k, v, do, v_col0=0, blk=ATT_BLOCK):
    tn = q.shape[0]
    nb = tn // blk
    Dh = RET_HEAD_DIM

    def body(lg_ref, q_ref, k_ref, v_ref, do_ref, dq_ref, dk_ref, dv_ref):
        lgv = lg_ref[0][:, 0:1]
        dk_ref[...] = jnp.zeros(dk_ref.shape, F32)
        dv_ref[...] = jnp.zeros(dv_ref.shape, F32)

        def ib(i, carry):
            qs = pl.ds(pl.multiple_of(i * blk, blk), blk)
            qb = q_ref[qs, :].astype(BF16)
            dob = do_ref[qs, :].astype(BF16)

            def jb(j, dq):
                ks = pl.ds(pl.multiple_of(j * blk, blk), blk)
                kb = k_ref[ks, :].astype(BF16)
                vb = v_ref[ks, :].astype(BF16)
                dm = _decay_mask(lgv, i, j, blk)
                s = lax.dot_general(qb, kb, _NT, preferred_element_type=F32) * dm
                ds = lax.dot_general(dob, vb, _NT, preferred_element_type=F32) * dm
                sb, dsb = s.astype(BF16), ds.astype(BF16)
                dv_ref[ks, :] += lax.dot_general(sb, dob, _TN, preferred_element_type=F32)
                dk_ref[ks, :] += lax.dot_general(dsb, qb, _TN, preferred_element_type=F32)
                return dq + jnp.dot(dsb, kb, preferred_element_type=F32)

            dq_ref[qs, :] = lax.fori_loop(0, i + 1, jb, jnp.zeros((blk, Dh), F32))
            return carry

        lax.fori_loop(0, nb, ib, 0)

    full = pl.BlockSpec((tn, Dh), lambda h: (0, h))
    sh = jax.ShapeDtypeStruct(q.shape, F32)
    return pl.pallas_call(
        body, name="ret_attn_bwd", grid=(RET_HEADS,),
        in_specs=[pl.BlockSpec((1, 1, 128), lambda h: (h, 0, 0)), full, full,
                  pl.BlockSpec((tn, Dh), lambda h: (0, v_col0 + h)), full],
        out_specs=[full, full, full], out_shape=[sh, sh, sh], compiler_params=_cparams(1))(lg, q, k, v, do)


def _next8_spec(a, tb):
    r = tb // 8
    last = a.shape[0] // 8 - 1
    return pl.BlockSpec((8, a.shape[1]), lambda i: (jnp.minimum((i + 1) * r, last), 0))


def _conv_taps(g_ext, cw_ref, cb_ref):
    return (cw_ref[2:3, :] * g_ext + cw_ref[1:2, :] * pltpu.roll(g_ext, 1, 0)
            + cw_ref[0:1, :] * pltpu.roll(g_ext, 2, 0) + cb_ref[...])


def _glu_fwd(gate, up, cw, cb, tb=TOK_BLOCK):
    tn = gate.shape[0]

    def body(g_ref, gh_ref, u_ref, cw_ref, cb_ref, o_ref):
        halo = jnp.where(pl.program_id(0) == 0, 0.0, gh_ref[...])
        g_ext = jnp.concatenate([halo, g_ref[...]], axis=0)
        gc = _conv_taps(g_ext, cw_ref, cb_ref)[8:, :]
        o_ref[...] = (gc * _sigmoid(gc) * u_ref[...]).astype(o_ref.dtype)

    return pl.pallas_call(
        body, name="glu_fwd", grid=(tn // tb,),
        in_specs=[_blk_spec(gate, tb), _prev8_spec(gate, tb), _blk_spec(up, tb), _full_spec(cw), _full_spec(cb)],
        out_specs=_blk_spec(gate, tb), out_shape=jax.ShapeDtypeStruct(gate.shape, BF16),
        compiler_params=_cparams(1))(gate, gate, up, cw, cb)


def _glu_bwd(gate, up, dact, cw, cb, tb=TOK_BLOCK):
    tn = gate.shape[0]
    nb = tn // tb

    def body(g_ref, gp_ref, gn_ref, u_ref, un_ref, d_ref, dn_ref, cw_ref, cb_ref, dg_ref, du_ref, dcw_ref, dcb_ref):
        i = pl.program_id(0)
        gprev = jnp.where(i == 0, 0.0, gp_ref[...])
        dnext = jnp.where(i == nb - 1, 0.0, dn_ref[...])
        g_ext = jnp.concatenate([gprev, g_ref[...], gn_ref[...]], axis=0)
        gc = _conv_taps(g_ext, cw_ref, cb_ref)[8:, :]
        u_e = jnp.concatenate([u_ref[...], un_ref[...]], axis=0)
        d_e = jnp.concatenate([d_ref[...], dnext], axis=0)
        s = _sigmoid(gc)
        dgc = d_e * u_e * (s * (1.0 + gc * (1.0 - s)))
        du_ref[...] = (d_ref[...] * (gc * s)[:tb, :]).astype(du_ref.dtype)
        n_e = tb + 8
        dg_ref[...] = (cw_ref[2:3, :] * dgc + cw_ref[1:2, :] * pltpu.roll(dgc, n_e - 1, 0)
                       + cw_ref[0:1, :] * pltpu.roll(dgc, n_e - 2, 0))[:tb, :].astype(dg_ref.dtype)

        @pl.when(i == 0)
        def _():
            dcw_ref[...] = jnp.zeros(dcw_ref.shape, F32)
            dcb_ref[...] = jnp.zeros(dcb_ref.shape, F32)

        dgc_b = dgc[:tb, :]
        g0 = g_ext[8:8 + tb, :]
        g1 = pltpu.roll(g_ext, 1, 0)[8:8 + tb, :]
        g2 = pltpu.roll(g_ext, 2, 0)[8:8 + tb, :]
        dcw_ref[2:3, :] += jnp.sum(dgc_b * g0, axis=0, keepdims=True)
        dcw_ref[1:2, :] += jnp.sum(dgc_b * g1, axis=0, keepdims=True)
        dcw_ref[0:1, :] += jnp.sum(dgc_b * g2, axis=0, keepdims=True)
        dcb_ref[...] += jnp.sum(dgc_b, axis=0, keepdims=True)

    sh = jax.ShapeDtypeStruct(gate.shape, BF16)
    return pl.pallas_call(
        body, name="glu_bwd", grid=(nb,),
        in_specs=[_blk_spec(gate, tb), _prev8_spec(gate, tb), _next8_spec(gate, tb), _blk_spec(up, tb),
                  _next8_spec(up, tb), _blk_spec(dact, tb), _next8_spec(dact, tb), _full_spec(cw), _full_spec(cb)],
        out_specs=[_blk_spec(gate, tb), _blk_spec(gate, tb), _full_spec(cw), _full_spec(cb)],
        out_shape=[sh, sh, jax.ShapeDtypeStruct(cw.shape, F32), jax.ShapeDtypeStruct(cb.shape, F32)],
        compiler_params=_cparams(1))(gate, gate, gate, up, up, dact, dact, cw, cb)


def _final_loss(x2, tgt, g, tb=LIGHT_TOK_BLOCK):
    tn, dm = x2.shape

    def body(x_ref, t_ref, g_ref, l_ref, dx_ref, dg_ref):
        y, vjp = jax.vjp(_rms_fn, x_ref[...], g_ref[...])
        err = y - t_ref[...]
        dx, dg = vjp(err * (1.0 / dm))

        @pl.when(pl.program_id(0) == 0)
        def _():
            l_ref[...] = jnp.zeros(l_ref.shape, F32)
            dg_ref[...] = jnp.zeros(dg_ref.shape, F32)

        part = 0.5 * jnp.sum(jnp.mean(err * err, axis=-1, keepdims=True), axis=0, keepdims=True)
        l_ref[...] += jnp.broadcast_to(part, l_ref.shape)
        dx_ref[...] = dx
        dg_ref[...] += dg

    return pl.pallas_call(
        body, name="final_loss", grid=(tn // tb,),
        in_specs=[_blk_spec(x2, tb), _blk_spec(tgt, tb), _full_spec(g)],
        out_specs=[pl.BlockSpec((8, 128), lambda i: (0, 0)), _blk_spec(x2, tb), _full_spec(g)],
        out_shape=[jax.ShapeDtypeStruct((8, 128), F32), jax.ShapeDtypeStruct(x2.shape, F32),
                   jax.ShapeDtypeStruct(g.shape, F32)],
        compiler_params=_cparams(1))(x2, tgt, g)


def _pad_cols(w, n):
    return jnp.pad(w, ((0, 0), (0, n - w.shape[1])))


def _pad_rows(w, n):
    return jnp.pad(w, ((0, n - w.shape[0]), (0, 0)))


def _local_step(x, tgt, W, late):
    tn = x.shape[0]
    Wd = RWKV_WIDTH
    row = lambda z: z.reshape(1, -1)
    g_mix, g_ffn, g_fin = row(W['norm_mix_g']), row(W['norm_ffn_g']), row(W['norm_final_g'])

    (h1,) = _tok_fwd("norm_mix_fwd", lambda a, g: (_rms_fn(a, g),), [x], [g_mix], [(D_MODEL,)])
    proj = _mm("proj_fwd", h1, W['w_in_t'], tb=True)
    p_rkv = _Cols(proj, 3 * Wd, 0)
    pre_consts = [row(W['rwkv_mu_w']), row(W['rwkv_mu_a']), row(W['rwkv_mu_g']), row(W['rwkv_mu_r']),
                  row(W['rwkv_mu_k']), row(W['rwkv_mu_v']), row(W['rwkv_w0']),
                  _pad_cols(W['rwkv_w1'], LORA_PAD), _pad_rows(W['rwkv_w2'], LORA_PAD), row(W['rwkv_a0']),
                  _pad_cols(W['rwkv_a1'], LORA_PAD), _pad_rows(W['rwkv_a2'], LORA_PAD),
                  W['rwkv_g1'], W['rwkv_g2'], row(W['rwkv_k_k']), row(W['rwkv_k_a'])]
    r, k, v, lw, nkk, b, g = _pre_a_fwd(h1, p_rkv, pre_consts)
    y_scan, ck, gathered = _cscan_fwd(r, lw, k, v, nkk, b, late)
    w_out, w_gate_t, w_up_t, w_down = [g_.reshape(-1, D_MODEL) for g_ in gathered]
    post_consts = [row(W['rwkv_lnx_w']), row(W['rwkv_lnx_b']), row(W['rwkv_r_k'])]
    (y_rwkv,) = _tok_fwd("rwkv_post_fwd", _rwkv_post_fn, [y_scan, r, k, v, g], post_consts, [(Wd,)],
                         out_dtypes=[BF16])

    pos = jnp.arange(tn, dtype=F32)
    half = RET_HEAD_DIM // 2
    inv_freq = ROPE_BASE ** (-jnp.arange(half, dtype=F32) / half)
    ang = pos[:, None] * inv_freq[None, :]
    cos2 = jnp.concatenate([jnp.cos(ang), jnp.cos(ang)], axis=1)
    sin2 = jnp.concatenate([-jnp.sin(ang), jnp.sin(ang)], axis=1)
    lg = jnp.log(1.0 - 2.0 ** (-5.0 - jnp.arange(RET_HEADS, dtype=F32)))
    lg = jnp.broadcast_to(lg[:, None, None], (RET_HEADS, 1, 128))
    q_p, k_p, g_ret = _Cols(proj, Wd, 3), _Cols(proj, Wd, 4), _Cols(proj, Wd, 6)
    v_col0 = 5 * Wd // RET_HEAD_DIM
    q_rot, k_rot = _tok_fwd("ret_rotary_fwd", _rotary_fn, [cos2, sin2, q_p, k_p], [], [(RET_WIDTH,)] * 2)
    y_ret_raw = _ret_attn_fwd(lg, q_rot, k_rot, proj, v_col0)
    gn_w = row(W['ret_gn_w'])
    (y_ret,) = _tok_fwd("ret_post_fwd", _ret_post_fn, [y_ret_raw, g_ret], [gn_w], [(RET_WIDTH,)],
                        out_dtypes=[BF16])

    ycat = jnp.concatenate([y_rwkv, y_ret], axis=1)
    x1 = _mm("out_proj_fwd", ycat, w_out, add=x)
    (h2,) = _tok_fwd("norm_ffn_fwd", lambda a_, g_: (_rms_fn(a_, g_),), [x1], [g_ffn], [(D_MODEL,)],
                     out_dtypes=[BF16])
    gate = _mm("ffn_gate_fwd", h2, w_gate_t, tb=True)
    up = _mm("ffn_up_fwd", h2, w_up_t, tb=True)
    cw = W['ffn_conv_w']
    cb = row(W['ffn_conv_b'])
    act = _glu_fwd(gate, up, cw, cb)
    x2 = _mm("ffn_down_fwd", act, w_down, add=x1)
    loss8, dx2, dg_fin = _final_loss(x2, tgt, g_fin)

    G = {'norm_final_g': dg_fin}
    dact = _mm("ffn_down_dx", dx2, w_down, tb=True)
    d_down = _mm("ffn_down_dw", act, dx2, ta=True, out_dtype=BF16)
    dgate, dup, dcw, dcb = _glu_bwd(gate, up, dact, cw, cb)
    G['ffn_conv_w'], G['ffn_conv_b'] = dcw, dcb
    dh2 = _mm("ffn_gate_dx", dgate, w_gate_t)
    dh2 = _mm("ffn_up_dx", dup, w_up_t, add=dh2)
    d_gate_t = _mm("ffn_gate_dw", dgate, h2, ta=True, out_dtype=BF16)
    d_up_t = _mm("ffn_up_dw", dup, h2, ta=True, out_dtype=BF16)
    dx1, G['norm_ffn_g'] = _tok_bwd("norm_ffn_bwd", lambda a_, g_: (_rms_fn(a_, g_),), [], [x1], [g_ffn], [dh2], add=dx2)
    dycat = _mm("out_proj_dx", dx1, w_out, tb=True)
    d_out = _mm("out_proj_dw", ycat, dx1, ta=True, out_dtype=BF16)
    late_grads = [z.reshape(N_DEV, -1, D_MODEL) for z in (d_out, d_gate_t, d_up_t, d_down)]
    dy_rwkv, dy_ret = _Cols(dycat, Wd, 0), _Cols(dycat, Wd, 1)

    dyr_raw, dg_ret, G['ret_gn_w'] = _tok_bwd("ret_post_bwd", _ret_post_fn, [], [y_ret_raw, g_ret], [gn_w], [dy_ret],
                                              tok_dtypes=[F32, BF16])
    dq_rot, dk_rot, dv_ret = _ret_attn_bwd(lg, q_rot, k_rot, proj, dyr_raw, v_col0)
    dq_p, dk_p = _tok_bwd("ret_rotary_bwd", _rotary_fn, [cos2, sin2], [q_p, k_p], [], [dq_rot, dk_rot],
                          tok_dtypes=[BF16, BF16])

    dy_scan, dr1, dk1, dv1, dg, G['rwkv_lnx_w'], G['rwkv_lnx_b'], G['rwkv_r_k'] = _tok_bwd(
        "rwkv_post_bwd", _rwkv_post_fn, [], [y_scan, r, k, v, g], post_consts, [dy_rwkv])
    (dr2, dlw, dk2, dv2, dnkk, db), late_parts = _cscan_bwd(r, lw, k, v, nkk, b, dy_scan, ck, late_grads)
    pre_cts = [(dr1, dr2), (dk1, dk2), (dv1, dv2), dlw, dnkk, db, dg]
    pre_out = _pre_a_bwd(h1, p_rkv, pre_consts, pre_cts)
    dh1_a, dp_rkv = pre_out[0], pre_out[1]
    (G['rwkv_mu_w'], G['rwkv_mu_a'], G['rwkv_mu_g'], G['rwkv_mu_r'], G['rwkv_mu_k'], G['rwkv_mu_v'], G['rwkv_w0'],
     dw1, dw2, G['rwkv_a0'], da1, da2, G['rwkv_g1'], G['rwkv_g2'], G['rwkv_k_k'], G['rwkv_k_a']) = pre_out[2:]
    G['rwkv_w1'], G['rwkv_w2'] = dw1[:, :64], dw2[:64, :]
    G['rwkv_a1'], G['rwkv_a2'] = da1[:, :64], da2[:64, :]

    dproj = jnp.concatenate([dp_rkv, dq_p, dk_p, dv_ret.astype(BF16), dg_ret], axis=1)
    d_in_t = _mm("proj_dw", dproj, h1, ta=True, out_dtype=BF16)
    dh1, (w_in_parts,) = _mm("proj_dx", dproj, W['w_in_t'], add=dh1_a, xs=[d_in_t.reshape(N_DEV, -1, D_MODEL)])
    dx, G['norm_mix_g'] = _tok_bwd("norm_mix_bwd", lambda a_, g_: (_rms_fn(a_, g_),), [], [x], [g_mix], [dh1], add=dx1)
    return loss8[0, 0], dx, G, late_parts, w_in_parts


def _adamw_block(p_ref, w_ref, m_ref, v_ref, g_ref, d_ref, nm_ref, nv_ref):
    g = p_ref[0].astype(F32)
    for d in range(1, N_DEV):
        g = g + p_ref[d].astype(F32)
    mn = ADAM_B1 * m_ref[...] + (1.0 - ADAM_B1) * g
    vn = ADAM_B2 * v_ref[...] + (1.0 - ADAM_B2) * (g * g)
    m_hat = mn / (1.0 - ADAM_B1 ** ADAM_STEP)
    v_hat = vn / (1.0 - ADAM_B2 ** ADAM_STEP)
    g_ref[...] = g
    d_ref[...] = -ADAM_LR * (m_hat / (jnp.sqrt(v_hat) + ADAM_EPS) + ADAM_WD * w_ref[...])
    nm_ref[...] = mn
    nv_ref[...] = vn


def _adamw_late(items, xs, tb=32):
    n_it, n_x = len(items), len(xs)
    nbs = [it[1].shape[0] // tb for it in items]
    steps = max(nbs)
    cols = items[0][1].shape[1]

    def body(*refs):
        ins, x_src = refs[:4 * n_it], refs[4 * n_it:4 * n_it + n_x]
        outs = refs[4 * n_it + n_x:8 * n_it + n_x]
        x_dst, sems = refs[8 * n_it + n_x:8 * n_it + 2 * n_x], refs[8 * n_it + 2 * n_x:]
        i = pl.program_id(0)

        @pl.when(i == 0)
        def _():
            _scatter_start(x_src, x_dst, sems)

        for j in range(n_it):
            @pl.when(i < nbs[j])
            def _(j=j):
                _adamw_block(*ins[4 * j:4 * j + 4], *outs[4 * j:4 * j + 4])

        @pl.when(i == steps - 1)
        def _():
            _scatter_wait(x_src, x_dst, sems)

    in_specs, out_specs, out_shape, flat = [], [], [], []
    for (parts, w, m, v), nb in zip(items, nbs):
        blk = pl.BlockSpec((tb, cols), lambda i, nb=nb: (jnp.minimum(i, nb - 1), 0))
        in_specs += [pl.BlockSpec((N_DEV, tb, cols), lambda i, nb=nb: (0, jnp.minimum(i, nb - 1), 0)), blk, blk, blk]
        out_specs += [blk] * 4
        out_shape += [jax.ShapeDtypeStruct(w.shape, F32)] * 4
        flat += [parts, w, m, v]
    res = pl.pallas_call(
        body, name="adamw_late", grid=(steps,), in_specs=in_specs + [_ANY] * n_x, out_specs=out_specs + [_ANY] * n_x,
        out_shape=out_shape + _xchg_out_shapes(xs, True), scratch_shapes=_xchg_sems(n_x),
        compiler_params=_cparams(1))(*flat, *xs)
    return [res[4 * j:4 * j + 4] for j in range(n_it)], res[4 * n_it:]


def _adamw(name, parts, w, m, v):
    rows, cols = w.shape
    sub = 8 * 4 // parts.dtype.itemsize
    tb = max(t for t in range(sub, 65, sub) if rows % t == 0) if rows > 64 else rows
    body = functools.partial(_adamw_block)
    spec = pl.BlockSpec((tb, cols), lambda i: (i, 0))
    sh = jax.ShapeDtypeStruct((rows, cols), F32)
    return pl.pallas_call(
        body, name=name, grid=(rows // tb,),
        in_specs=[pl.BlockSpec((N_DEV, tb, cols), lambda i: (0, i, 0)), spec, spec, spec],
        out_specs=[spec] * 4, out_shape=[sh] * 4, compiler_params=_cparams(1))(parts, w, m, v)


def _local_shape(name):
    gs, ax = SHARDED[name]
    ls = list(gs)
    ls[ax] //= N_DEV
    return tuple(ls)


def _seg(flat, seg):
    n = flat.shape[-1]
    pad = _round_up(n, seg) - n
    if pad:
        flat = jnp.pad(flat, [(0, 0)] * (flat.ndim - 1) + [(0, pad)])
    return flat


def _split3(w):
    hi = w.astype(BF16)
    r1 = w - hi.astype(F32)
    mid = r1.astype(BF16)
    lo = (r1 - mid.astype(F32)).astype(BF16)
    return hi, mid, lo


def _pack_small_shards(shards):
    pieces = []
    for name in SMALL_NAMES:
        flat = shards[name].reshape(-1)
        if name == 'ffn_conv_w':
            pieces += [_seg(p, BF16_SEG) for p in _split3(flat)]
        else:
            pieces.append(flat.astype(BF16))
    return jnp.concatenate(pieces).reshape(-1, 128)


def _unpack_small(gathered):
    flat = gathered.reshape(N_DEV, -1)
    out, off = {}, 0
    for name in SMALL_NAMES:
        gs, ax = SHARDED[name]
        ls = _local_shape(name)
        n = int(np.prod(ls))
        if name == 'ffn_conv_w':
            nseg = _round_up(n, BF16_SEG)
            hi, mid, lo = (flat[:, off + j * nseg: off + j * nseg + n].astype(F32) for j in range(3))
            sh = ((hi + mid) + lo).reshape(N_DEV, 3, -1)
            out[name] = jnp.swapaxes(sh, 0, 1).reshape(3, D_FF)
            off += 3 * nseg
        else:
            sh = flat[:, off:off + n].reshape((N_DEV,) + ls[1:])
            out[name] = sh.reshape(gs[1:]) if ax == 1 else jnp.swapaxes(sh, 0, 1).reshape(gs[1:])
            off += n
    return out


def _small_pieces(sharded, repl):
    return [sharded[n].reshape(-1) for n in SMALL_NAMES] + [repl[n].reshape(-1) for n in REPL_NAMES]


def _pack_small_local(d):
    flat = jnp.concatenate(_small_pieces(d, d))
    return _seg(flat, F32_SEG).reshape(-1, 128)


def _pack_small_grads(G, loss):
    pieces = []
    for name in SMALL_NAMES:
        gs, ax = SHARDED[name]
        g = G[name]
        if name == 'ffn_conv_w':
            sh = jnp.swapaxes(g.reshape(3, N_DEV, -1), 0, 1)
        elif ax == 1:
            sh = g
        else:
            sh = jnp.swapaxes(g.reshape(g.shape[0], N_DEV, -1), 0, 1)
        pieces.append(sh.reshape(N_DEV, -1))
    rep = jnp.concatenate([G[n].reshape(-1) for n in REPL_NAMES] + [loss.reshape(1)])
    pieces.append(jnp.broadcast_to(rep[None, :], (N_DEV, rep.shape[0])))
    flat = _seg(jnp.concatenate(pieces, axis=1), F32_SEG)
    return flat.reshape(N_DEV, -1, 128)


def _unpack_small_local(packed, local_shapes):
    flat = packed.reshape(-1)
    out, off = {}, 0
    for name in SMALL_NAMES + REPL_NAMES:
        n = int(np.prod(local_shapes[name]))
        out[name] = flat[off:off + n].reshape(local_shapes[name])
        off += n
    return out


def kernel(x, *rest):
    nw = len(WEIGHT_NAMES)
    assert len(rest) == 3 * nw + 1
    weights = dict(zip(WEIGHT_NAMES, rest[:nw]))
    loss_target = rest[nw]
    moms = dict(zip(WEIGHT_NAMES, rest[nw + 1:2 * nw + 1]))
    vars_ = dict(zip(WEIGHT_NAMES, rest[2 * nw + 1:]))
    local_shapes = {n: weights[n].shape for n in WEIGHT_NAMES}

    def native2d(name, a):
        a2 = a.reshape(a.shape[-2], a.shape[-1])
        return a2.T if name in BIG_T else a2

    def from2d(name, a2):
        return (a2.T if name in BIG_T else a2).reshape(local_shapes[name])

    big_w = {n: native2d(n, weights[n]) for n in BIG_NAMES}
    w_in_t_sh = big_w['w_in'].astype(BF16)
    late = [big_w[n].astype(BF16) for n in LATE_NAMES]
    small_sh = _pack_small_shards({n: weights[n] for n in SMALL_NAMES})
    w_in_g, small_g = _exchange("weights_all_gather", [w_in_t_sh, small_sh], False)
    W = _unpack_small(small_g)
    W['w_in_t'] = w_in_g.reshape(-1, D_MODEL)
    for n in REPL_NAMES:
        W[n] = weights[n][0] if n != 'norm_final_g' else weights[n]

    loss, dx, G, late_parts, w_in_parts = _local_step(x[0], loss_target[0], W, late)

    late_items = [(parts, big_w[n], native2d(n, moms[n]), native2d(n, vars_[n]))
                  for n, parts in zip(LATE_NAMES, late_parts)]
    late_res, (small_parts,) = _adamw_late(late_items, [_pack_small_grads(G, loss)])
    results = {n: [from2d(n, r) for r in res] for n, res in zip(LATE_NAMES, late_res)}
    res = _adamw("adamw_w_in", w_in_parts, big_w['w_in'], native2d('w_in', moms['w_in']), native2d('w_in', vars_['w_in']))
    results['w_in'] = [from2d('w_in', r) for r in res]
    small_res = _adamw("adamw_small", small_parts, _pack_small_local(weights), _pack_small_local(moms),
                       _pack_small_local(vars_))
    small_out = [_unpack_small_local(p, local_shapes) for p in small_res]

    n_small = sum(int(np.prod(local_shapes[n])) for n in SMALL_NAMES + REPL_NAMES)
    loss = small_res[0].reshape(-1)[n_small]
    outs = [loss, dx[None]]
    for j in range(4):
        outs += [results[n][j] if n in results else small_out[j][n] for n in WEIGHT_NAMES]
    return tuple(outs)
```

```python
import functools
import math

import numpy as np
import jax
import jax.numpy as jnp
from jax import lax
from jax.experimental import pallas as pl
from jax.experimental.pallas import tpu as pltpu

F32 = jnp.float32
BF16 = jnp.bfloat16

N_DEV = 8
D_MODEL = 1024
RWKV_HEADS = 8
RWKV_HEAD_DIM = 64
RWKV_WIDTH = 512
RET_HEADS = 4
RET_HEAD_DIM = 128
RET_WIDTH = 512
LORA_PAD = 128
D_FF = 2816
NORM_EPS = 1e-6
RWKV_GN_EPS = 64e-5
RET_GN_EPS = 1e-5
ROPE_BASE = 10000.0
ADAM_LR, ADAM_B1, ADAM_B2, ADAM_EPS, ADAM_WD, ADAM_STEP = 0.001, 0.9, 0.999, 1e-08, 0.01, 10

VMEM_LIMIT = 56 * 1024 * 1024
TOK_BLOCK = 256
LIGHT_TOK_BLOCK = 512
SCAN_CHUNK = 64
ATT_BLOCK = 512
BF16_SEG = 2048
F32_SEG = 1024

WEIGHT_NAMES = ['norm_mix_g', 'w_in', 'rwkv_mu_r', 'rwkv_mu_k', 'rwkv_mu_v', 'rwkv_mu_w', 'rwkv_mu_a',
                'rwkv_mu_g', 'rwkv_w0', 'rwkv_w1', 'rwkv_w2', 'rwkv_a0', 'rwkv_a1', 'rwkv_a2', 'rwkv_g1',
                'rwkv_g2', 'rwkv_k_k', 'rwkv_k_a', 'rwkv_r_k', 'rwkv_lnx_w', 'rwkv_lnx_b', 'ret_gn_w',
                'w_out', 'norm_ffn_g', 'ffn_w_gate', 'ffn_w_up', 'ffn_conv_w', 'ffn_conv_b', 'ffn_w_down',
                'norm_final_g']
SHARDED = {
    'w_in': ((1, 1024, 3584), 2), 'rwkv_w1': ((1, 1024, 64), 1), 'rwkv_w2': ((1, 64, 512), 2),
    'rwkv_a1': ((1, 1024, 64), 1), 'rwkv_a2': ((1, 64, 512), 2), 'rwkv_g1': ((1, 1024, 128), 1),
    'rwkv_g2': ((1, 128, 512), 2), 'w_out': ((1, 1024, 1024), 1), 'ffn_w_gate': ((1, 1024, 2816), 2),
    'ffn_w_up': ((1, 1024, 2816), 2), 'ffn_conv_w': ((1, 3, 1, 2816), 3), 'ffn_w_down': ((1, 2816, 1024), 1),
}
REPL_NAMES = [n for n in WEIGHT_NAMES if n not in SHARDED]
BIG_NAMES = ['w_in', 'w_out', 'ffn_w_gate', 'ffn_w_up', 'ffn_w_down']
BIG_T = ('w_in', 'ffn_w_gate', 'ffn_w_up')
LATE_NAMES = ['w_out', 'ffn_w_gate', 'ffn_w_up', 'ffn_w_down']
SMALL_NAMES = [n for n in WEIGHT_NAMES if n in SHARDED and n not in BIG_NAMES]


def _cparams(n_grid):
    return pltpu.CompilerParams(dimension_semantics=("arbitrary",) * n_grid, vmem_limit_bytes=VMEM_LIMIT)


def _round_up(n, m):
    return (n + m - 1) // m * m


@jax.custom_vjp
def _bdot(x, w):
    return jnp.dot(x.astype(BF16), w.astype(BF16), preferred_element_type=F32)


def _bdot_fwd(x, w):
    return _bdot(x, w), (x, w)


def _bdot_bwd(res, g):
    x, w = res
    gb = g.astype(BF16)
    dx = lax.dot_general(gb, w.astype(BF16), (((1,), (1,)), ((), ())), preferred_element_type=F32)
    dw = lax.dot_general(x.astype(BF16), gb, (((0,), (0,)), ((), ())), preferred_element_type=F32)
    return dx, dw.astype(w.dtype)


_bdot.defvjp(_bdot_fwd, _bdot_bwd)


@jax.custom_vjp
def _shift_rows(x, prev):
    rolled = pltpu.roll(x, 1, 0)
    row = lax.broadcasted_iota(jnp.int32, x.shape, 0)
    return jnp.where(row == 0, jnp.broadcast_to(prev, x.shape), rolled)


def _shift_rows_fwd(x, prev):
    return _shift_rows(x, prev), None


def _shift_rows_bwd(_, g):
    n = g.shape[0]
    rolled = pltpu.roll(g, n - 1, 0)
    row = lax.broadcasted_iota(jnp.int32, g.shape, 0)
    return jnp.where(row == n - 1, 0.0, rolled), g[0:1, :]


_shift_rows.defvjp(_shift_rows_fwd, _shift_rows_bwd)


@jax.custom_vjp
def _swap_halves(x):
    return pltpu.roll(x, 64, 1)


_swap_halves.defvjp(lambda x: (_swap_halves(x), None), lambda _, g: (pltpu.roll(g, 64, 1),))


def _sigmoid(x):
    return 1.0 / (1.0 + jnp.exp(-x))


def _softplus(x):
    return jnp.maximum(x, 0.0) + jnp.log(1.0 + jnp.exp(-jnp.abs(x)))


def _rms_fn(x, g):
    return x * lax.rsqrt(jnp.mean(x * x, axis=-1, keepdims=True) + NORM_EPS) * g


def _pre_a_fn(h1, h1p, p, pp, mu_w, mu_a, mu_g, mu_r, mu_k, mu_v, w0, w1, w2, a0, a1, a2, g1, g2, k_k, k_a):
    W = RWKV_WIDTH
    h1s = _shift_rows(h1, h1p)
    ps = _shift_rows(p, pp)
    dx = h1s - h1
    xw = h1 + dx * mu_w
    xa = h1 + dx * mu_a
    xg = h1 + dx * mu_g
    dp = ps - p
    r = p[:, 0:W] + dp[:, 0:W] * mu_r
    k0 = p[:, W:2 * W] + dp[:, W:2 * W] * mu_k
    v = p[:, 2 * W:3 * W] + dp[:, 2 * W:3 * W] * mu_v
    wl = w0 + _bdot(jnp.tanh(_bdot(xw, w1)), w2)
    w_log = -_softplus(-wl) - 0.5
    lw = -jnp.exp(w_log)
    a = _sigmoid(a0 + _bdot(_bdot(xa, a1), a2))
    g = _bdot(_sigmoid(_bdot(xg, g1)), g2)
    nkk, k, b = _pre_b_fn(k0, a, k_k, k_a)
    return r, k, v, lw, nkk, b, g


def _head_sum_raw(x):
    n = x.shape[1]
    ii = lax.broadcasted_iota(jnp.int32, (n, n), 0) // RWKV_HEAD_DIM
    jj = lax.broadcasted_iota(jnp.int32, (n, n), 1) // RWKV_HEAD_DIM
    ones = (ii == jj).astype(BF16)
    xh = x.astype(BF16)
    xl = (x - xh.astype(F32)).astype(BF16)
    return jnp.dot(xh, ones, preferred_element_type=F32) + jnp.dot(xl, ones, preferred_element_type=F32)


@jax.custom_vjp
def _head_sum(x):
    return _head_sum_raw(x)


_head_sum.defvjp(lambda x: (_head_sum_raw(x), None), lambda _, g: (_head_sum_raw(g),))


def _pre_b_fn(k0, a, k_k, k_a):
    kkr = k0 * k_k
    nrm = jnp.sqrt(_head_sum(kkr * kkr))
    kk = kkr / jnp.maximum(nrm, 1e-12)
    k = k0 * (1.0 + (a - 1.0) * k_a)
    return -kk, k, kk * a


def _rwkv_post_fn(y, r, k, v, g, lnx_w, lnx_b, r_k):
    inv = 1.0 / RWKV_HEAD_DIM
    mu = _head_sum(y) * inv
    yc = y - mu
    var = _head_sum(yc * yc) * inv
    yn = yc * lax.rsqrt(var + RWKV_GN_EPS) * lnx_w + lnx_b
    bonus = _head_sum(r * k * r_k) * v
    return ((yn + bonus) * g,)


def _rotary_fn(cos2, sin2, q, k):
    qs, ks = [], []
    for h in range(RET_HEADS):
        sl = slice(h * RET_HEAD_DIM, (h + 1) * RET_HEAD_DIM)
        qh, kh = q[:, sl], k[:, sl]
        qs.append(qh * cos2 + _swap_halves(qh) * sin2)
        ks.append((kh * cos2 + _swap_halves(kh) * sin2) * (RET_HEAD_DIM ** -0.5))
    return jnp.concatenate(qs, axis=1), jnp.concatenate(ks, axis=1)


def _ret_post_fn(y, gp, gn_w):
    outs = []
    for h in range(RET_HEADS):
        sl = slice(h * RET_HEAD_DIM, (h + 1) * RET_HEAD_DIM)
        yh = y[:, sl]
        mu = jnp.mean(yh, axis=-1, keepdims=True)
        yc = yh - mu
        var = jnp.mean(yc * yc, axis=-1, keepdims=True)
        outs.append(yc * lax.rsqrt(var + RET_GN_EPS) * gn_w[:, sl])
    yn = jnp.concatenate(outs, axis=1)
    return (gp * _sigmoid(gp) * yn,)


class _Cols:
    def __init__(self, array, width, block):
        self.array, self.width, self.block = array, width, block
        self.shape, self.ndim, self.dtype = (array.shape[0], width), 2, array.dtype


def _arr(a):
    return a.array if isinstance(a, _Cols) else a


def _blk_spec(a, tb, rev_nb=None):
    nd = a.ndim
    tail = (a.block,) if isinstance(a, _Cols) else (0,) * (nd - 1)
    if rev_nb is None:
        return pl.BlockSpec((tb,) + a.shape[1:], lambda i: (i,) + tail)
    return pl.BlockSpec((tb,) + a.shape[1:], lambda i: (rev_nb - 1 - i,) + tail)


def _full_spec(a):
    nd = a.ndim
    return pl.BlockSpec(a.shape, lambda i: (0,) * nd)


def _tok_fwd(name, fn, toks, consts, out_tails, tb=LIGHT_TOK_BLOCK, out_dtypes=None):
    out_dtypes = out_dtypes or [F32] * len(out_tails)
    n_in = len(toks) + len(consts)
    tn = toks[0].shape[0]

    def body(*refs):
        outs = fn(*[r[...] for r in refs[:n_in]])
        for r, o in zip(refs[n_in:], outs):
            r[...] = o.astype(r.dtype)

    out_shape = [jax.ShapeDtypeStruct((tn,) + tuple(s), dt) for s, dt in zip(out_tails, out_dtypes)]
    return pl.pallas_call(
        body, name=name, grid=(tn // tb,),
        in_specs=[_blk_spec(a, tb) for a in toks] + [_full_spec(c) for c in consts],
        out_specs=[_blk_spec(o, tb) for o in out_shape], out_shape=out_shape,
        compiler_params=_cparams(1))(*[_arr(a) for a in toks], *consts)


def _tok_bwd(name, fn, aux, toks, consts, cts, add=None, tb=LIGHT_TOK_BLOCK, tok_dtypes=None):
    n_aux, n_tok, n_c = len(aux), len(toks), len(consts)
    ct_groups = [c if isinstance(c, (tuple, list)) else (c,) for c in cts]
    ct_flat = [a for grp in ct_groups for a in grp]
    n_ct = len(ct_flat)
    n_add = 0 if add is None else 1
    tn = toks[0].shape[0]

    def body(*refs):
        pos = 0
        aux_v = [r[...] for r in refs[pos:pos + n_aux]]; pos += n_aux
        tok_v = [r[...] for r in refs[pos:pos + n_tok]]; pos += n_tok
        const_v = [r[...] for r in refs[pos:pos + n_c]]; pos += n_c
        ct_refs = refs[pos:pos + n_ct]; pos += n_ct
        add_refs = refs[pos:pos + n_add]; pos += n_add
        dtok_refs = refs[pos:pos + n_tok]; pos += n_tok
        dconst_refs = refs[pos:pos + n_c]
        ct_v, q = [], 0
        for grp in ct_groups:
            s = ct_refs[q][...]
            for r in ct_refs[q + 1:q + len(grp)]:
                s = s + r[...]
            q += len(grp)
            ct_v.append(s)
        _, vjp = jax.vjp(lambda *tc: fn(*aux_v, *tc), *tok_v, *const_v)
        grads = vjp(tuple(ct_v))
        for j, r in enumerate(dtok_refs):
            gj = grads[j]
            if j == 0 and n_add:
                gj = gj + add_refs[0][...]
            r[...] = gj.astype(r.dtype)

        @pl.when(pl.program_id(0) == 0)
        def _():
            for r in dconst_refs:
                r[...] = jnp.zeros(r.shape, F32)

        for j, r in enumerate(dconst_refs):
            r[...] += grads[n_tok + j]

    ins = list(aux) + list(toks) + list(consts) + ct_flat + ([add] if n_add else [])
    in_specs = ([_blk_spec(a, tb) for a in aux] + [_blk_spec(a, tb) for a in toks] + [_full_spec(c) for c in consts]
                + [_blk_spec(a, tb) for a in ct_flat] + ([_blk_spec(add, tb)] if n_add else []))
    tok_dtypes = tok_dtypes or [F32] * n_tok
    out_shape = ([jax.ShapeDtypeStruct(a.shape, dt) for a, dt in zip(toks, tok_dtypes)]
                 + [jax.ShapeDtypeStruct(c.shape, F32) for c in consts])
    out_specs = [_blk_spec(o, tb) for o in out_shape[:n_tok]] + [_full_spec(c) for c in consts]
    return pl.pallas_call(body, name=name, grid=(tn // tb,), in_specs=in_specs, out_specs=out_specs,
                          out_shape=out_shape, compiler_params=_cparams(1))(*[_arr(a) for a in ins])


MM_VMEM_BUDGET = 40 * 1024 * 1024
MM_STEP_SECONDS = 0.4e-6
MM_HBM_BYTES_PER_SECOND = 2.5e12
MM_XPOSE_SECONDS_PER_ELEM = 2e-12
MM_MXU_COLUMNS = 256
MM_MXU_FLOPS = 9e14


def _mm_tiles(m, n, kd, a_bytes, b_bytes, o_bytes, has_add, ta):
    divs = lambda d: [t for t in range(128, d + 1, 128) if d % t == 0]
    best = None
    for tm in divs(m):
        for tn in divs(n):
            for tk in divs(kd):
                ni, nj, nk = m // tm, n // tn, kd // tk
                vmem = (2 * tm * tk * a_bytes + 2 * tk * tn * b_bytes + tm * tn * 4 + 2 * tm * tn * o_bytes
                        + (2 * tm * tn * 4 if has_add else 0) + 2 * (tm * tk + tk * tn) + tm * tn * 4)
                if vmem > MM_VMEM_BUDGET:
                    continue
                a_traffic = m * kd * a_bytes * (nj if nk > 1 else 1)
                b_traffic = kd * n * b_bytes * (ni if nj * nk > 1 else 1)
                cost = ni * nj * nk * MM_STEP_SECONDS + (a_traffic + b_traffic) / MM_HBM_BYTES_PER_SECOND
                cost += 2.0 * m * kd * nj * max(tn, MM_MXU_COLUMNS) / MM_MXU_FLOPS
                if ta:
                    cost += m * kd * nj * MM_XPOSE_SECONDS_PER_ELEM
                if best is None or cost < best[0]:
                    best = (cost, tm, tn, tk)
    return best[1:]


def _mm(name, a, b, ta=False, tb=False, add=None, out_dtype=F32, xs=()):
    n_x = len(xs)
    if ta:
        kd, m = a.shape
    else:
        m, kd = a.shape
    if tb:
        n, kb = b.shape
    else:
        kb, n = b.shape
    assert kd == kb, (a.shape, b.shape)
    tm, tn, tk = _mm_tiles(m, n, kd, a.dtype.itemsize, b.dtype.itemsize, jnp.dtype(out_dtype).itemsize,
                           add is not None, ta)
    nk = kd // tk
    has_add = add is not None
    dims = (((0 if ta else 1,), (1 if tb else 0,)), ((), ()))

    n_in = 3 if add is not None else 2
    nj = n // tn
    n_steps = (m // tm) * nj * nk

    def body(*refs):
        a_ref, b_ref = refs[0], refs[1]
        x_src, o_ref = refs[n_in:n_in + n_x], refs[n_in + n_x]
        x_dst, acc_ref, sems = refs[n_in + n_x + 1:n_in + 2 * n_x + 1], refs[n_in + 2 * n_x + 1], refs[n_in + 2 * n_x + 2:]
        k = pl.program_id(2)
        step = (pl.program_id(0) * nj + pl.program_id(1)) * nk + k

        if n_x:
            @pl.when(step == 0)
            def _():
                _scatter_start(x_src, x_dst, sems)

        @pl.when(k == 0)
        def _():
            acc_ref[...] = refs[2][...] if has_add else jnp.zeros(acc_ref.shape, F32)

        acc_ref[...] += lax.dot_general(a_ref[...].astype(BF16), b_ref[...].astype(BF16), dims,
                                        preferred_element_type=F32)

        @pl.when(k == nk - 1)
        def _():
            o_ref[...] = acc_ref[...].astype(out_dtype)

        if n_x:
            @pl.when(step == n_steps - 1)
            def _():
                _scatter_wait(x_src, x_dst, sems)

    a_spec = pl.BlockSpec((tk, tm), lambda i, j, k: (k, i)) if ta else pl.BlockSpec((tm, tk), lambda i, j, k: (i, k))
    b_spec = pl.BlockSpec((tn, tk), lambda i, j, k: (j, k)) if tb else pl.BlockSpec((tk, tn), lambda i, j, k: (k, j))
    o_spec = pl.BlockSpec((tm, tn), lambda i, j, k: (i, j))
    ins = [a, b] + ([add] if has_add else [])
    in_specs = [a_spec, b_spec] + ([o_spec] if has_add else [])
    res = pl.pallas_call(body, name=name, grid=(m // tm, n // tn, nk), in_specs=in_specs + [_ANY] * n_x,
                         out_specs=[o_spec] + [_ANY] * n_x,
                         out_shape=[jax.ShapeDtypeStruct((m, n), out_dtype)] + _xchg_out_shapes(xs, True),
                         scratch_shapes=[pltpu.VMEM((tm, tn), F32)] + (_xchg_sems(n_x) if n_x else []),
                         compiler_params=_cparams(3))(*ins, *xs)
    return (res[0], res[1:]) if n_x else res[0]


def _prev8_spec(a, tb, rev_nb=None):
    r = tb // 8
    if rev_nb is None:
        return pl.BlockSpec((8, a.shape[1]), lambda i: (jnp.maximum(i * r - 1, 0), 0))
    return pl.BlockSpec((8, a.shape[1]), lambda i: (jnp.maximum((rev_nb - 1 - i) * r - 1, 0), 0))


def _pre_a_fwd(h1, p, consts, tb=TOK_BLOCK):
    tn = h1.shape[0]

    def body(h1_ref, h1h_ref, p_ref, ph_ref, *rest):
        c_refs, o_refs = rest[:len(consts)], rest[len(consts):]
        first = pl.program_id(0) == 0
        h1p = jnp.where(first, 0.0, h1h_ref[7:8, :])
        pp = jnp.where(first, 0.0, ph_ref[7:8, :])
        outs = _pre_a_fn(h1_ref[...], h1p, p_ref[...], pp, *[c[...] for c in c_refs])
        for r, o in zip(o_refs, outs):
            r[...] = o

    out_shape = [jax.ShapeDtypeStruct((tn, RWKV_WIDTH), F32) for _ in range(7)]
    return pl.pallas_call(
        body, name="rwkv_pre_a_fwd", grid=(tn // tb,),
        in_specs=[_blk_spec(h1, tb), _prev8_spec(h1, tb), _blk_spec(p, tb), _prev8_spec(p, tb)]
        + [_full_spec(c) for c in consts],
        out_specs=[_blk_spec(o, tb) for o in out_shape], out_shape=out_shape,
        compiler_params=_cparams(1))(h1, h1, _arr(p), _arr(p), *consts)


def _pre_a_bwd(h1, p, consts, cts, tb=TOK_BLOCK):
    tn = h1.shape[0]
    nb = tn // tb
    n_c = len(consts)
    ct_groups = [c if isinstance(c, (tuple, list)) else (c,) for c in cts]
    ct_flat = [a for grp in ct_groups for a in grp]
    n_ct = len(ct_flat)

    def body(*refs):
        h1_ref, h1h_ref, p_ref, ph_ref = refs[:4]
        c_refs = refs[4:4 + n_c]
        ct_refs = refs[4 + n_c:4 + n_c + n_ct]
        dh1_ref, dp_ref = refs[4 + n_c + n_ct:6 + n_c + n_ct]
        dc_refs = refs[6 + n_c + n_ct:6 + 2 * n_c + n_ct]
        ch_ref, cp_ref = refs[-2], refs[-1]
        i = pl.program_id(0)
        first_block = i == nb - 1
        h1p = jnp.where(first_block, 0.0, h1h_ref[7:8, :])
        pp = jnp.where(first_block, 0.0, ph_ref[7:8, :])
        ct_v, q = [], 0
        for grp in ct_groups:
            s = ct_refs[q][...]
            for r in ct_refs[q + 1:q + len(grp)]:
                s = s + r[...]
            q += len(grp)
            ct_v.append(s)
        _, vjp = jax.vjp(_pre_a_fn, h1_ref[...], h1p, p_ref[...], pp, *[c[...] for c in c_refs])
        grads = vjp(tuple(ct_v))

        @pl.when(i == 0)
        def _():
            ch_ref[...] = jnp.zeros(ch_ref.shape, F32)
            cp_ref[...] = jnp.zeros(cp_ref.shape, F32)
            for r in dc_refs:
                r[...] = jnp.zeros(r.shape, F32)

        rowh = lax.broadcasted_iota(jnp.int32, (tb, h1.shape[1]), 0)
        rowp = lax.broadcasted_iota(jnp.int32, (tb, p.shape[1]), 0)
        dh1_ref[...] = grads[0] + jnp.where(rowh == tb - 1, jnp.broadcast_to(ch_ref[0:1, :], rowh.shape), 0.0)
        dp_ref[...] = (grads[2] + jnp.where(rowp == tb - 1, jnp.broadcast_to(cp_ref[0:1, :], rowp.shape), 0.0)
                       ).astype(dp_ref.dtype)
        ch_ref[0:1, :] = grads[1]
        cp_ref[0:1, :] = grads[3]
        for j, r in enumerate(dc_refs):
            r[...] += grads[4 + j]

    ins = [h1, h1, _arr(p), _arr(p)] + list(consts) + ct_flat
    in_specs = ([_blk_spec(h1, tb, nb), _prev8_spec(h1, tb, nb), _blk_spec(p, tb, nb), _prev8_spec(p, tb, nb)]
                + [_full_spec(c) for c in consts] + [_blk_spec(a, tb, nb) for a in ct_flat])
    out_shape = ([jax.ShapeDtypeStruct(h1.shape, F32), jax.ShapeDtypeStruct(p.shape, BF16)]
                 + [jax.ShapeDtypeStruct(c.shape, F32) for c in consts])
    out_specs = [_blk_spec(h1, tb, nb), _blk_spec(p, tb, nb)] + [_full_spec(c) for c in consts]
    return pl.pallas_call(body, name="rwkv_pre_a_bwd", grid=(nb,), in_specs=in_specs, out_specs=out_specs,
                          out_shape=out_shape,
                          scratch_shapes=[pltpu.VMEM((8, h1.shape[1]), F32), pltpu.VMEM((8, p.shape[1]), F32)],
                          compiler_params=_cparams(1))(*ins)


def _my_index():
    return 4 * lax.axis_index("x") + 2 * lax.axis_index("y") + lax.axis_index("c")


def _peer(k):
    x, y, c = lax.axis_index("x"), lax.axis_index("y"), lax.axis_index("c")
    px = 1 - x if k & 4 else x
    py = 1 - y if k & 2 else y
    pc = 1 - c if k & 1 else c
    return (px, py, pc), 4 * px + 2 * py + pc


def _xchg_sems(n):
    return [pltpu.SemaphoreType.DMA((n * (N_DEV - 1),)), pltpu.SemaphoreType.DMA((n * (N_DEV - 1),)),
            pltpu.SemaphoreType.DMA((n,))]


def _scatter_copies(srcs, dsts, sems, incoming=False):
    send_sems, recv_sems, local_sems = sems
    me = _my_index()
    local, remote = [], []
    for i, (s, d) in enumerate(zip(srcs, dsts)):
        if not incoming:
            local.append(pltpu.make_async_copy(s.at[me], d.at[me], local_sems.at[i]))
        for k in range(1, N_DEV):
            peer, plin = _peer(k)
            j = i * (N_DEV - 1) + k - 1
            s_slot, d_slot = (me, plin) if incoming else (plin, me)
            remote.append(pltpu.make_async_remote_copy(
                src_ref=s.at[s_slot], dst_ref=d.at[d_slot], send_sem=send_sems.at[j],
                recv_sem=recv_sems.at[j], device_id=peer, device_id_type=pl.DeviceIdType.MESH))
    return local, remote


def _scatter_start(srcs, dsts, sems):
    local, out = _scatter_copies(srcs, dsts, sems)
    for cp in local + out:
        cp.start()


def _scatter_wait(srcs, dsts, sems):
    for cp in _scatter_copies(srcs, dsts, sems, incoming=True)[1]:
        cp.wait_recv()
    local, out = _scatter_copies(srcs, dsts, sems)
    for cp in out:
        cp.wait_send()
    for cp in local:
        cp.wait()


_ICI_PEERS = (2, 4, 6)


def _gather_copies(srcs, dsts, sems, group):
    send_sems, recv_sems, local_sems = sems
    me = _my_index()
    sib, sib_lin = _peer(1)
    out = []
    for i, (s, d) in enumerate(zip(srcs, dsts)):
        def mk(q, src, dst, dev):
            j = i * (N_DEV - 1) + q
            return pltpu.make_async_remote_copy(src_ref=src, dst_ref=dst, send_sem=send_sems.at[j],
                                                recv_sem=recv_sems.at[j], device_id=dev,
                                                device_id_type=pl.DeviceIdType.MESH)
        if group == 'local':
            out.append(pltpu.make_async_copy(s, d.at[me], local_sems.at[i]))
        elif group == 'own':
            out.append(mk(0, s, d.at[me], sib))
        elif group == 'in_d2d':
            out.append(mk(0, s, d.at[sib_lin], sib))
        for jj, k in enumerate(_ICI_PEERS):
            peer, plin = _peer(k)
            plin_other = _peer(k + 1)[1]
            if group == 'own':
                out.append(mk(1 + jj, s, d.at[me], peer))
            elif group == 'in_ici':
                out.append(mk(1 + jj, s, d.at[plin], peer))
            elif group == 'pass_on':
                out.append(mk(4 + jj, d.at[plin], d.at[plin], sib))
            elif group == 'in_d2d':
                out.append(mk(4 + jj, d.at[plin_other], d.at[plin_other], sib))
    return out


def _gather_start(srcs, dsts, sems):
    for cp in _gather_copies(srcs, dsts, sems, 'local') + _gather_copies(srcs, dsts, sems, 'own'):
        cp.start()


def _gather_pass_on(srcs, dsts, sems):
    for cp in _gather_copies(srcs, dsts, sems, 'in_ici'):
        cp.wait_recv()
    for cp in _gather_copies(srcs, dsts, sems, 'pass_on'):
        cp.start()


def _gather_finish(srcs, dsts, sems):
    for cp in _gather_copies(srcs, dsts, sems, 'in_d2d'):
        cp.wait_recv()
    for cp in _gather_copies(srcs, dsts, sems, 'own') + _gather_copies(srcs, dsts, sems, 'pass_on'):
        cp.wait_send()
    for cp in _gather_copies(srcs, dsts, sems, 'local'):
        cp.wait()


def _xchg_out_shapes(srcs, scatter):
    return [jax.ShapeDtypeStruct(s.shape if scatter else (N_DEV,) + s.shape, s.dtype) for s in srcs]


_ANY = pl.BlockSpec(memory_space=pl.ANY)


def _exchange(name, srcs, scatter):
    n = len(srcs)

    def body(*refs):
        s, d, sems = refs[:n], refs[n:2 * n], refs[2 * n:]
        if scatter:
            _scatter_start(s, d, sems)
            _scatter_wait(s, d, sems)
        else:
            _gather_start(s, d, sems)
            _gather_pass_on(s, d, sems)
            _gather_finish(s, d, sems)

    return pl.pallas_call(body, name=name, in_specs=[_ANY] * n, out_specs=[_ANY] * n,
                          out_shape=_xchg_out_shapes(srcs, scatter), scratch_shapes=_xchg_sems(n))(*srcs)


_MM_DIMS = {'nn': (((1,), (0,)), ((), ())), 'nt': (((1,), (1,)), ((), ())), 'tn': (((0,), (0,)), ((), ()))}


def _cmm_raw(x, y, kind, split):
    dot = functools.partial(lax.dot_general, dimension_numbers=_MM_DIMS[kind], preferred_element_type=F32)
    xh, yh = x.astype(BF16), y.astype(BF16)
    out = dot(xh, yh)
    if split:
        xl = (x - xh.astype(F32)).astype(BF16)
        yl = (y - yh.astype(F32)).astype(BF16)
        out = out + (dot(xh, yl) + dot(xl, yh))
    return out


@functools.partial(jax.custom_vjp, nondiff_argnums=(2, 3))
def _cmm(x, y, kind, split=False):
    return _cmm_raw(x, y, kind, split)


def _cmm_fwd(x, y, kind, split):
    return _cmm_raw(x, y, kind, split), (x, y)


def _cmm_bwd(kind, split, res, g):
    x, y = res
    if kind == 'nn':
        return _cmm_raw(g, y, 'nt', split), _cmm_raw(x, g, 'tn', split)
    if kind == 'nt':
        return _cmm_raw(g, y, 'nn', split), _cmm_raw(g, x, 'tn', split)
    return _cmm_raw(y, g, 'nt', split), _cmm_raw(x, g, 'nn', split)


_cmm.defvjp(_cmm_fwd, _cmm_bwd)


def _tri_sum_raw(tri, x, kind):
    dot = functools.partial(lax.dot_general, dimension_numbers=_MM_DIMS[kind], preferred_element_type=F32)
    tb = tri.astype(BF16)
    hi, mid, lo = _split3(x)
    return (dot(tb, hi) + dot(tb, mid)) + dot(tb, lo)


@functools.partial(jax.custom_vjp, nondiff_argnums=(2,))
def _tri_sum(tri, x, kind):
    return _tri_sum_raw(tri, x, kind)


def _tri_sum_fwd(tri, x, kind):
    return _tri_sum_raw(tri, x, kind), tri


def _tri_sum_bwd(kind, tri, g):
    return jnp.zeros_like(tri), _tri_sum_raw(tri, g, 'tn' if kind == 'nn' else 'nn')


_tri_sum.defvjp(_tri_sum_fwd, _tri_sum_bwd)


def _chunk_fn(S0, r, lw, k, v, a, b):
    hs = range(len(r))
    C = r[0].shape[0]
    ii = lax.broadcasted_iota(jnp.int32, (C, C), 0)
    jj = lax.broadcasted_iota(jnp.int32, (C, C), 1)
    incl, strict = ii >= jj, ii > jj
    eye = (ii == jj).astype(F32)
    inclf = incl.astype(F32)
    cum = [_tri_sum(inclf, lw[h], 'nn') for h in hs]
    e_inv = [jnp.exp(-cum[h]) for h in hs]
    At = [a[h] * jnp.exp(cum[h] - lw[h]) for h in hs]
    Rt = [r[h] * jnp.exp(cum[h]) for h in hs]
    Kh = [k[h] * e_inv[h] for h in hs]
    Bh = [b[h] * e_inv[h] for h in hs]
    Mab = [jnp.where(strict, _cmm(At[h], Bh[h], 'nt'), 0.0) for h in hs]
    Mak = [jnp.where(strict, _cmm(At[h], Kh[h], 'nt', True), 0.0) for h in hs]
    Mrk = [jnp.where(incl, _cmm(Rt[h], Kh[h], 'nt', True), 0.0) for h in hs]
    Mrb = [jnp.where(incl, _cmm(Rt[h], Bh[h], 'nt'), 0.0) for h in hs]
    rhs = [_cmm(At[h], S0[h], 'nt') + _cmm(Mak[h], v[h], 'nn') for h in hs]
    P = Mab
    Tm = [eye + P[h] for h in hs]
    n = 1
    while 2 * n < C:
        P = [_cmm(P[h], P[h], 'nn', True) for h in hs]
        Tm = [_cmm(Tm[h], eye + P[h], 'nn', True) for h in hs]
        n *= 2
    U = [_cmm(Tm[h], rhs[h], 'nn', True) for h in hs]
    Y = [_cmm(Rt[h], S0[h], 'nt') + _cmm(Mrk[h], v[h], 'nn') + _cmm(Mrb[h], U[h], 'nn') for h in hs]
    gC = [jnp.exp(jnp.sum(lw[h], axis=0, keepdims=True)) for h in hs]
    SC = [S0[h] * gC[h] + _cmm(v[h], Kh[h] * gC[h], 'tn') + _cmm(U[h], Bh[h] * gC[h], 'tn') for h in hs]
    return tuple(Y), tuple(SC)


def _cscan_fwd(r, lw, k, v, a, b, xs):
    n_x = len(xs)
    tn = r.shape[0]
    H, Dh, Dv = RWKV_HEADS, RWKV_HEAD_DIM, RWKV_HEAD_DIM
    nc = tn // SCAN_CHUNK
    lanes = lambda h: slice(h * Dh, (h + 1) * Dh)
    heads = lambda ref: tuple(ref[:, lanes(h)] for h in range(H))
    mats = lambda ref: tuple(ref[h] for h in range(H))

    def body(r_ref, lw_ref, k_ref, v_ref, a_ref, b_ref, *rest):
        x_src, (y_ref, ck_ref) = rest[:n_x], rest[n_x:n_x + 2]
        x_dst, s_ref, sems = rest[n_x + 2:2 * n_x + 2], rest[2 * n_x + 2], rest[2 * n_x + 3:]

        @pl.when(pl.program_id(0) == 0)
        def _():
            s_ref[...] = jnp.zeros(s_ref.shape, F32)
            _gather_start(x_src, x_dst, sems)

        ck_ref[0] = s_ref[...]
        y, sc = _chunk_fn(mats(s_ref), heads(r_ref), heads(lw_ref), heads(k_ref), heads(v_ref), heads(a_ref),
                          heads(b_ref))
        for h in range(H):
            y_ref[:, lanes(h)] = y[h]
            s_ref[h] = sc[h]

        @pl.when(pl.program_id(0) == max(nc - 4, 0))
        def _():
            _gather_pass_on(x_src, x_dst, sems)

        @pl.when(pl.program_id(0) == nc - 1)
        def _():
            _gather_finish(x_src, x_dst, sems)

    hm = pl.BlockSpec((SCAN_CHUNK, H * Dh), lambda c: (c, 0))
    res = pl.pallas_call(
        body, name="rwkv_scan_fwd", grid=(nc,), in_specs=[hm] * 6 + [_ANY] * n_x,
        out_specs=[hm, pl.BlockSpec((1, H, Dv, Dh), lambda c: (c, 0, 0, 0))] + [_ANY] * n_x,
        out_shape=[jax.ShapeDtypeStruct((tn, H * Dh), F32), jax.ShapeDtypeStruct((nc, H, Dv, Dh), F32)]
        + _xchg_out_shapes(xs, False),
        scratch_shapes=[pltpu.VMEM((H, Dv, Dh), F32)] + _xchg_sems(n_x),
        compiler_params=_cparams(1))(r, lw, k, v, a, b, *xs)
    return res[0], res[1], res[2:]


def _cscan_bwd(r, lw, k, v, a, b, dy, ck, xs):
    n_x = len(xs)
    tn = r.shape[0]
    H, Dh, Dv = RWKV_HEADS, RWKV_HEAD_DIM, RWKV_HEAD_DIM
    nc = tn // SCAN_CHUNK
    lanes = lambda h: slice(h * Dh, (h + 1) * Dh)
    heads = lambda ref: tuple(ref[:, lanes(h)] for h in range(H))
    mats = lambda ref: tuple(ref[h] for h in range(H))

    def body(r_ref, lw_ref, k_ref, v_ref, a_ref, b_ref, dy_ref, ck_ref, *rest):
        x_src = rest[:n_x]
        d_refs = rest[n_x:n_x + 6]
        x_dst = rest[n_x + 6:2 * n_x + 6]
        g_ref = rest[2 * n_x + 6]
        sems = rest[2 * n_x + 7:]

        @pl.when(pl.program_id(0) == 0)
        def _():
            g_ref[...] = jnp.zeros(g_ref.shape, F32)
            _scatter_start(x_src, x_dst, sems)

        s0 = tuple(ck_ref[0, h] for h in range(H))
        _, vjp = jax.vjp(_chunk_fn, s0, heads(r_ref), heads(lw_ref), heads(k_ref), heads(v_ref), heads(a_ref),
                         heads(b_ref))
        grads = vjp((heads(dy_ref), mats(g_ref)))
        for h in range(H):
            g_ref[h] = grads[0][h]
            for d_ref, gz in zip(d_refs, grads[1:]):
                d_ref[:, lanes(h)] = gz[h]

        @pl.when(pl.program_id(0) == nc - 1)
        def _():
            _scatter_wait(x_src, x_dst, sems)

    hm = pl.BlockSpec((SCAN_CHUNK, H * Dh), lambda c: (nc - 1 - c, 0))
    hshape = jax.ShapeDtypeStruct((tn, H * Dh), F32)
    res = pl.pallas_call(
        body, name="rwkv_scan_bwd", grid=(nc,),
        in_specs=[hm] * 7 + [pl.BlockSpec((1, H, Dv, Dh), lambda c: (nc - 1 - c, 0, 0, 0))] + [_ANY] * n_x,
        out_specs=[hm] * 6 + [_ANY] * n_x, out_shape=[hshape] * 6 + _xchg_out_shapes(xs, True),
        scratch_shapes=[pltpu.VMEM((H, Dv, Dh), F32)] + _xchg_sems(n_x),
        compiler_params=_cparams(1))(r, lw, k, v, a, b, dy, ck, *xs)
    return res[:6], res[6:]


def _decay_mask(lg, i, j, blk):
    rows = lax.broadcasted_iota(jnp.int32, (blk, blk), 0)
    cols = lax.broadcasted_iota(jnp.int32, (blk, blk), 1)
    dd = (rows - cols + (i - j) * blk).astype(F32)
    return jnp.where(dd >= 0.0, jnp.exp(lg * jnp.maximum(dd, 0.0)), 0.0)


_NT = (((1,), (1,)), ((), ()))
_TN = (((0,), (0,)), ((), ()))


def _ret_attn_fwd(lg, q, k, v, v_col0=0, blk=ATT_BLOCK):
    tn = q.shape[0]
    Dh = RET_HEAD_DIM

    def body(lg_ref, q_ref, k_ref, v_ref, o_ref):
        i = pl.program_id(1)
        lgv = lg_ref[0][:, 0:1]
        qb = q_ref[...].astype(BF16)

        def jb(j, acc):
            ks = pl.ds(pl.multiple_of(j * blk, blk), blk)
            s = lax.dot_general(qb, k_ref[ks, :].astype(BF16), _NT, preferred_element_type=F32)
            s = s * _decay_mask(lgv, i, j, blk)
            return acc + jnp.dot(s.astype(BF16), v_ref[ks, :].astype(BF16), preferred_element_type=F32)

        o_ref[...] = lax.fori_loop(0, i + 1, jb, jnp.zeros((blk, Dh), F32))

    full = pl.BlockSpec((tn, Dh), lambda h, i: (0, h))
    qs = pl.BlockSpec((blk, Dh), lambda h, i: (i, h))
    return pl.pallas_call(
        body, name="ret_attn_fwd", grid=(RET_HEADS, tn // blk),
        in_specs=[pl.BlockSpec((1, 1, 128), lambda h, i: (h, 0, 0)), qs, full,
                  pl.BlockSpec((tn, Dh), lambda h, i: (0, v_col0 + h))],
        out_specs=qs, out_shape=jax.ShapeDtypeStruct(q.shape, F32), compiler_params=_cparams(2))(lg, q, k, v)


def _ret_attn_bwd(lg, q, k, v, do, v_col0=0, blk=ATT_BLOCK):
    tn = q.shape[0]
    nb = tn // blk
    Dh = RET_HEAD_DIM

    def body(lg_ref, q_ref, k_ref, v_ref, do_ref, dq_ref, dk_ref, dv_ref):
        lgv = lg_ref[0][:, 0:1]
        dk_ref[...] = jnp.zeros(dk_ref.shape, F32)
        dv_ref[...] = jnp.zeros(dv_ref.shape, F32)

        def ib(i, carry):
            qs = pl.ds(pl.multiple_of(i * blk, blk), blk)
            qb = q_ref[qs, :].astype(BF16)
            dob = do_ref[qs, :].astype(BF16)

            def jb(j, dq):
                ks = pl.ds(pl.multiple_of(j * blk, blk), blk)
                kb = k_ref[ks, :].astype(BF16)
                vb = v_ref[ks, :].astype(BF16)
                dm = _decay_mask(lgv, i, j, blk)
                s = lax.dot_general(qb, kb, _NT, preferred_element_type=F32) * dm
                ds = lax.dot_general(dob, vb, _NT, preferred_element_type=F32) * dm
                sb, dsb = s.astype(BF16), ds.astype(BF16)
                dv_ref[ks, :] += lax.dot_general(sb, dob, _TN, preferred_element_type=F32)
                dk_ref[ks, :] += lax.dot_general(dsb, qb, _TN, preferred_element_type=F32)
                return dq + jnp.dot(dsb, kb, preferred_element_type=F32)

            dq_ref[qs, :] = lax.fori_loop(0, i + 1, jb, jnp.zeros((blk, Dh), F32))
            return carry

        lax.fori_loop(0, nb, ib, 0)

    full = pl.BlockSpec((tn, Dh), lambda h: (0, h))
    sh = jax.ShapeDtypeStruct(q.shape, F32)
    return pl.pallas_call(
        body, name="ret_attn_bwd", grid=(RET_HEADS,),
        in_specs=[pl.BlockSpec((1, 1, 128), lambda h: (h, 0, 0)), full, full,
                  pl.BlockSpec((tn, Dh), lambda h: (0, v_col0 + h)), full],
        out_specs=[full, full, full], out_shape=[sh, sh, sh], compiler_params=_cparams(1))(lg, q, k, v, do)


def _next8_spec(a, tb):
    r = tb // 8
    last = a.shape[0] // 8 - 1
    return pl.BlockSpec((8, a.shape[1]), lambda i: (jnp.minimum((i + 1) * r, last), 0))


def _conv_taps(g_ext, cw_ref, cb_ref):
    return (cw_ref[2:3, :] * g_ext + cw_ref[1:2, :] * pltpu.roll(g_ext, 1, 0)
            + cw_ref[0:1, :] * pltpu.roll(g_ext, 2, 0) + cb_ref[...])


def _glu_fwd(gate, up, cw, cb, tb=TOK_BLOCK):
    tn = gate.shape[0]

    def body(g_ref, gh_ref, u_ref, cw_ref, cb_ref, o_ref):
        halo = jnp.where(pl.program_id(0) == 0, 0.0, gh_ref[...])
        g_ext = jnp.concatenate([halo, g_ref[...]], axis=0)
        gc = _conv_taps(g_ext, cw_ref, cb_ref)[8:, :]
        o_ref[...] = (gc * _sigmoid(gc) * u_ref[...]).astype(o_ref.dtype)

    return pl.pallas_call(
        body, name="glu_fwd", grid=(tn // tb,),
        in_specs=[_blk_spec(gate, tb), _prev8_spec(gate, tb), _blk_spec(up, tb), _full_spec(cw), _full_spec(cb)],
        out_specs=_blk_spec(gate, tb), out_shape=jax.ShapeDtypeStruct(gate.shape, BF16),
        compiler_params=_cparams(1))(gate, gate, up, cw, cb)


def _glu_bwd(gate, up, dact, cw, cb, tb=TOK_BLOCK):
    tn = gate.shape[0]
    nb = tn // tb

    def body(g_ref, gp_ref, gn_ref, u_ref, un_ref, d_ref, dn_ref, cw_ref, cb_ref, dg_ref, du_ref, dcw_ref, dcb_ref):
        i = pl.program_id(0)
        gprev = jnp.where(i == 0, 0.0, gp_ref[...])
        dnext = jnp.where(i == nb - 1, 0.0, dn_ref[...])
        g_ext = jnp.concatenate([gprev, g_ref[...], gn_ref[...]], axis=0)
        gc = _conv_taps(g_ext, cw_ref, cb_ref)[8:, :]
        u_e = jnp.concatenate([u_ref[...], un_ref[...]], axis=0)
        d_e = jnp.concatenate([d_ref[...], dnext], axis=0)
        s = _sigmoid(gc)
        dgc = d_e * u_e * (s * (1.0 + gc * (1.0 - s)))
        du_ref[...] = (d_ref[...] * (gc * s)[:tb, :]).astype(du_ref.dtype)
        n_e = tb + 8
        dg_ref[...] = (cw_ref[2:3, :] * dgc + cw_ref[1:2, :] * pltpu.roll(dgc, n_e - 1, 0)
                       + cw_ref[0:1, :] * pltpu.roll(dgc, n_e - 2, 0))[:tb, :].astype(dg_ref.dtype)

        @pl.when(i == 0)
        def _():
            dcw_ref[...] = jnp.zeros(dcw_ref.shape, F32)
            dcb_ref[...] = jnp.zeros(dcb_ref.shape, F32)

        dgc_b = dgc[:tb, :]
        g0 = g_ext[8:8 + tb, :]
        g1 = pltpu.roll(g_ext, 1, 0)[8:8 + tb, :]
        g2 = pltpu.roll(g_ext, 2, 0)[8:8 + tb, :]
        dcw_ref[2:3, :] += jnp.sum(dgc_b * g0, axis=0, keepdims=True)
        dcw_ref[1:2, :] += jnp.sum(dgc_b * g1, axis=0, keepdims=True)
        dcw_ref[0:1, :] += jnp.sum(dgc_b * g2, axis=0, keepdims=True)
        dcb_ref[...] += jnp.sum(dgc_b, axis=0, keepdims=True)

    sh = jax.ShapeDtypeStruct(gate.shape, BF16)
    return pl.pallas_call(
        body, name="glu_bwd", grid=(nb,),
        in_specs=[_blk_spec(gate, tb), _prev8_spec(gate, tb), _next8_spec(gate, tb), _blk_spec(up, tb),
                  _next8_spec(up, tb), _blk_spec(dact, tb), _next8_spec(dact, tb), _full_spec(cw), _full_spec(cb)],
        out_specs=[_blk_spec(gate, tb), _blk_spec(gate, tb), _full_spec(cw), _full_spec(cb)],
        out_shape=[sh, sh, jax.ShapeDtypeStruct(cw.shape, F32), jax.ShapeDtypeStruct(cb.shape, F32)],
        compiler_params=_cparams(1))(gate, gate, gate, up, up, dact, dact, cw, cb)


def _final_loss(x2, tgt, g, tb=LIGHT_TOK_BLOCK):
    tn, dm = x2.shape

    def body(x_ref, t_ref, g_ref, l_ref, dx_ref, dg_ref):
        y, vjp = jax.vjp(_rms_fn, x_ref[...], g_ref[...])
        err = y - t_ref[...]
        dx, dg = vjp(err * (1.0 / dm))

        @pl.when(pl.program_id(0) == 0)
        def _():
            l_ref[...] = jnp.zeros(l_ref.shape, F32)
            dg_ref[...] = jnp.zeros(dg_ref.shape, F32)

        part = 0.5 * jnp.sum(jnp.mean(err * err, axis=-1, keepdims=True), axis=0, keepdims=True)
        l_ref[...] += jnp.broadcast_to(part, l_ref.shape)
        dx_ref[...] = dx
        dg_ref[...] += dg

    return pl.pallas_call(
        body, name="final_loss", grid=(tn // tb,),
        in_specs=[_blk_spec(x2, tb), _blk_spec(tgt, tb), _full_spec(g)],
        out_specs=[pl.BlockSpec((8, 128), lambda i: (0, 0)), _blk_spec(x2, tb), _full_spec(g)],
        out_shape=[jax.ShapeDtypeStruct((8, 128), F32), jax.ShapeDtypeStruct(x2.shape, F32),
                   jax.ShapeDtypeStruct(g.shape, F32)],
        compiler_params=_cparams(1))(x2, tgt, g)


def _pad_cols(w, n):
    return jnp.pad(w, ((0, 0), (0, n - w.shape[1])))


def _pad_rows(w, n):
    return jnp.pad(w, ((0, n - w.shape[0]), (0, 0)))


def _local_step(x, tgt, W, late):
    tn = x.shape[0]
    Wd = RWKV_WIDTH
    row = lambda z: z.reshape(1, -1)
    g_mix, g_ffn, g_fin = row(W['norm_mix_g']), row(W['norm_ffn_g']), row(W['norm_final_g'])

    (h1,) = _tok_fwd("norm_mix_fwd", lambda a, g: (_rms_fn(a, g),), [x], [g_mix], [(D_MODEL,)])
    proj = _mm("proj_fwd", h1, W['w_in_t'], tb=True)
    p_rkv = _Cols(proj, 3 * Wd, 0)
    pre_consts = [row(W['rwkv_mu_w']), row(W['rwkv_mu_a']), row(W['rwkv_mu_g']), row(W['rwkv_mu_r']),
                  row(W['rwkv_mu_k']), row(W['rwkv_mu_v']), row(W['rwkv_w0']),
                  _pad_cols(W['rwkv_w1'], LORA_PAD), _pad_rows(W['rwkv_w2'], LORA_PAD), row(W['rwkv_a0']),
                  _pad_cols(W['rwkv_a1'], LORA_PAD), _pad_rows(W['rwkv_a2'], LORA_PAD),
                  W['rwkv_g1'], W['rwkv_g2'], row(W['rwkv_k_k']), row(W['rwkv_k_a'])]
    r, k, v, lw, nkk, b, g = _pre_a_fwd(h1, p_rkv, pre_consts)
    y_scan, ck, gathered = _cscan_fwd(r, lw, k, v, nkk, b, late)
    w_out, w_gate_t, w_up_t, w_down = [g_.reshape(-1, D_MODEL) for g_ in gathered]
    post_consts = [row(W['rwkv_lnx_w']), row(W['rwkv_lnx_b']), row(W['rwkv_r_k'])]
    (y_rwkv,) = _tok_fwd("rwkv_post_fwd", _rwkv_post_fn, [y_scan, r, k, v, g], post_consts, [(Wd,)],
                         out_dtypes=[BF16])

    pos = jnp.arange(tn, dtype=F32)
    half = RET_HEAD_DIM // 2
    inv_freq = ROPE_BASE ** (-jnp.arange(half, dtype=F32) / half)
    ang = pos[:, None] * inv_freq[None, :]
    cos2 = jnp.concatenate([jnp.cos(ang), jnp.cos(ang)], axis=1)
    sin2 = jnp.concatenate([-jnp.sin(ang), jnp.sin(ang)], axis=1)
    lg = jnp.log(1.0 - 2.0 ** (-5.0 - jnp.arange(RET_HEADS, dtype=F32)))
    lg = jnp.broadcast_to(lg[:, None, None], (RET_HEADS, 1, 128))
    q_p, k_p, g_ret = _Cols(proj, Wd, 3), _Cols(proj, Wd, 4), _Cols(proj, Wd, 6)
    v_col0 = 5 * Wd // RET_HEAD_DIM
    q_rot, k_rot = _tok_fwd("ret_rotary_fwd", _rotary_fn, [cos2, sin2, q_p, k_p], [], [(RET_WIDTH,)] * 2)
    y_ret_raw = _ret_attn_fwd(lg, q_rot, k_rot, proj, v_col0)
    gn_w = row(W['ret_gn_w'])
    (y_ret,) = _tok_fwd("ret_post_fwd", _ret_post_fn, [y_ret_raw, g_ret], [gn_w], [(RET_WIDTH,)],
                        out_dtypes=[BF16])

    ycat = jnp.concatenate([y_rwkv, y_ret], axis=1)
    x1 = _mm("out_proj_fwd", ycat, w_out, add=x)
    (h2,) = _tok_fwd("norm_ffn_fwd", lambda a_, g_: (_rms_fn(a_, g_),), [x1], [g_ffn], [(D_MODEL,)],
                     out_dtypes=[BF16])
    gate = _mm("ffn_gate_fwd", h2, w_gate_t, tb=True)
    up = _mm("ffn_up_fwd", h2, w_up_t, tb=True)
    cw = W['ffn_conv_w']
    cb = row(W['ffn_conv_b'])
    act = _glu_fwd(gate, up, cw, cb)
    x2 = _mm("ffn_down_fwd", act, w_down, add=x1)
    loss8, dx2, dg_fin = _final_loss(x2, tgt, g_fin)

    G = {'norm_final_g': dg_fin}
    dact = _mm("ffn_down_dx", dx2, w_down, tb=True)
    d_down = _mm("ffn_down_dw", act, dx2, ta=True, out_dtype=BF16)
    dgate, dup, dcw, dcb = _glu_bwd(gate, up, dact, cw, cb)
    G['ffn_conv_w'], G['ffn_conv_b'] = dcw, dcb
    dh2 = _mm("ffn_gate_dx", dgate, w_gate_t)
    dh2 = _mm("ffn_up_dx", dup, w_up_t, add=dh2)
    d_gate_t = _mm("ffn_gate_dw", dgate, h2, ta=True, out_dtype=BF16)
    d_up_t = _mm("ffn_up_dw", dup, h2, ta=True, out_dtype=BF16)
    dx1, G['norm_ffn_g'] = _tok_bwd("norm_ffn_bwd", lambda a_, g_: (_rms_fn(a_, g_),), [], [x1], [g_ffn], [dh2], add=dx2)
    dycat = _mm("out_proj_dx", dx1, w_out, tb=True)
    d_out = _mm("out_proj_dw", ycat, dx1, ta=True, out_dtype=BF16)
    late_grads = [z.reshape(N_DEV, -1, D_MODEL) for z in (d_out, d_gate_t, d_up_t, d_down)]
    dy_rwkv, dy_ret = _Cols(dycat, Wd, 0), _Cols(dycat, Wd, 1)

    dyr_raw, dg_ret, G['ret_gn_w'] = _tok_bwd("ret_post_bwd", _ret_post_fn, [], [y_ret_raw, g_ret], [gn_w], [dy_ret],
                                              tok_dtypes=[F32, BF16])
    dq_rot, dk_rot, dv_ret = _ret_attn_bwd(lg, q_rot, k_rot, proj, dyr_raw, v_col0)
    dq_p, dk_p = _tok_bwd("ret_rotary_bwd", _rotary_fn, [cos2, sin2], [q_p, k_p], [], [dq_rot, dk_rot],
                          tok_dtypes=[BF16, BF16])

    dy_scan, dr1, dk1, dv1, dg, G['rwkv_lnx_w'], G['rwkv_lnx_b'], G['rwkv_r_k'] = _tok_bwd(
        "rwkv_post_bwd", _rwkv_post_fn, [], [y_scan, r, k, v, g], post_consts, [dy_rwkv])
    (dr2, dlw, dk2, dv2, dnkk, db), late_parts = _cscan_bwd(r, lw, k, v, nkk, b, dy_scan, ck, late_grads)
    pre_cts = [(dr1, dr2), (dk1, dk2), (dv1, dv2), dlw, dnkk, db, dg]
    pre_out = _pre_a_bwd(h1, p_rkv, pre_consts, pre_cts)
    dh1_a, dp_rkv = pre_out[0], pre_out[1]
    (G['rwkv_mu_w'], G['rwkv_mu_a'], G['rwkv_mu_g'], G['rwkv_mu_r'], G['rwkv_mu_k'], G['rwkv_mu_v'], G['rwkv_w0'],
     dw1, dw2, G['rwkv_a0'], da1, da2, G['rwkv_g1'], G['rwkv_g2'], G['rwkv_k_k'], G['rwkv_k_a']) = pre_out[2:]
    G['rwkv_w1'], G['rwkv_w2'] = dw1[:, :64], dw2[:64, :]
    G['rwkv_a1'], G['rwkv_a2'] = da1[:, :64], da2[:64, :]

    dproj = jnp.concatenate([dp_rkv, dq_p, dk_p, dv_ret.astype(BF16), dg_ret], axis=1)
    d_in_t = _mm("proj_dw", dproj, h1, ta=True, out_dtype=BF16)
    dh1, (w_in_parts,) = _mm("proj_dx", dproj, W['w_in_t'], add=dh1_a, xs=[d_in_t.reshape(N_DEV, -1, D_MODEL)])
    dx, G['norm_mix_g'] = _tok_bwd("norm_mix_bwd", lambda a_, g_: (_rms_fn(a_, g_),), [], [x], [g_mix], [dh1], add=dx1)
    return loss8[0, 0], dx, G, late_parts, w_in_parts


def _adamw_block(p_ref, w_ref, m_ref, v_ref, g_ref, d_ref, nm_ref, nv_ref):
    g = p_ref[0].astype(F32)
    for d in range(1, N_DEV):
        g = g + p_ref[d].astype(F32)
    mn = ADAM_B1 * m_ref[...] + (1.0 - ADAM_B1) * g
    vn = ADAM_B2 * v_ref[...] + (1.0 - ADAM_B2) * (g * g)
    m_hat = mn / (1.0 - ADAM_B1 ** ADAM_STEP)
    v_hat = vn / (1.0 - ADAM_B2 ** ADAM_STEP)
    g_ref[...] = g
    d_ref[...] = -ADAM_LR * (m_hat / (jnp.sqrt(v_hat) + ADAM_EPS) + ADAM_WD * w_ref[...])
    nm_ref[...] = mn
    nv_ref[...] = vn


def _adamw_late(items, xs, tb=32):
    n_it, n_x = len(items), len(xs)
    nbs = [it[1].shape[0] // tb for it in items]
    steps = max(nbs)
    cols = items[0][1].shape[1]

    def body(*refs):
        ins, x_src = refs[:4 * n_it], refs[4 * n_it:4 * n_it + n_x]
        outs = refs[4 * n_it + n_x:8 * n_it + n_x]
        x_dst, sems = refs[8 * n_it + n_x:8 * n_it + 2 * n_x], refs[8 * n_it + 2 * n_x:]
        i = pl.program_id(0)

        @pl.when(i == 0)
        def _():
            _scatter_start(x_src, x_dst, sems)

        for j in range(n_it):
            @pl.when(i < nbs[j])
            def _(j=j):
                _adamw_block(*ins[4 * j:4 * j + 4], *outs[4 * j:4 * j + 4])

        @pl.when(i == steps - 1)
        def _():
            _scatter_wait(x_src, x_dst, sems)

    in_specs, out_specs, out_shape, flat = [], [], [], []
    for (parts, w, m, v), nb in zip(items, nbs):
        blk = pl.BlockSpec((tb, cols), lambda i, nb=nb: (jnp.minimum(i, nb - 1), 0))
        in_specs += [pl.BlockSpec((N_DEV, tb, cols), lambda i, nb=nb: (0, jnp.minimum(i, nb - 1), 0)), blk, blk, blk]
        out_specs += [blk] * 4
        out_shape += [jax.ShapeDtypeStruct(w.shape, F32)] * 4
        flat += [parts, w, m, v]
    res = pl.pallas_call(
        body, name="adamw_late", grid=(steps,), in_specs=in_specs + [_ANY] * n_x, out_specs=out_specs + [_ANY] * n_x,
        out_shape=out_shape + _xchg_out_shapes(xs, True), scratch_shapes=_xchg_sems(n_x),
        compiler_params=_cparams(1))(*flat, *xs)
    return [res[4 * j:4 * j + 4] for j in range(n_it)], res[4 * n_it:]


def _adamw(name, parts, w, m, v):
    rows, cols = w.shape
    sub = 8 * 4 // parts.dtype.itemsize
    tb = max(t for t in range(sub, 65, sub) if rows % t == 0) if rows > 64 else rows
    body = functools.partial(_adamw_block)
    spec = pl.BlockSpec((tb, cols), lambda i: (i, 0))
    sh = jax.ShapeDtypeStruct((rows, cols), F32)
    return pl.pallas_call(
        body, name=name, grid=(rows // tb,),
        in_specs=[pl.BlockSpec((N_DEV, tb, cols), lambda i: (0, i, 0)), spec, spec, spec],
        out_specs=[spec] * 4, out_shape=[sh] * 4, compiler_params=_cparams(1))(parts, w, m, v)


def _local_shape(name):
    gs, ax = SHARDED[name]
    ls = list(gs)
    ls[ax] //= N_DEV
    return tuple(ls)


def _seg(flat, seg):
    n = flat.shape[-1]
    pad = _round_up(n, seg) - n
    if pad:
        flat = jnp.pad(flat, [(0, 0)] * (flat.ndim - 1) + [(0, pad)])
    return flat


def _split3(w):
    hi = w.astype(BF16)
    r1 = w - hi.astype(F32)
    mid = r1.astype(BF16)
    lo = (r1 - mid.astype(F32)).astype(BF16)
    return hi, mid, lo


def _pack_small_shards(shards):
    pieces = []
    for name in SMALL_NAMES:
        flat = shards[name].reshape(-1)
        if name == 'ffn_conv_w':
            pieces += [_seg(p, BF16_SEG) for p in _split3(flat)]
        else:
            pieces.append(flat.astype(BF16))
    return jnp.concatenate(pieces).reshape(-1, 128)


def _unpack_small(gathered):
    flat = gathered.reshape(N_DEV, -1)
    out, off = {}, 0
    for name in SMALL_NAMES:
        gs, ax = SHARDED[name]
        ls = _local_shape(name)
        n = int(np.prod(ls))
        if name == 'ffn_conv_w':
            nseg = _round_up(n, BF16_SEG)
            hi, mid, lo = (flat[:, off + j * nseg: off + j * nseg + n].astype(F32) for j in range(3))
            sh = ((hi + mid) + lo).reshape(N_DEV, 3, -1)
            out[name] = jnp.swapaxes(sh, 0, 1).reshape(3, D_FF)
            off += 3 * nseg
        else:
            sh = flat[:, off:off + n].reshape((N_DEV,) + ls[1:])
            out[name] = sh.reshape(gs[1:]) if ax == 1 else jnp.swapaxes(sh, 0, 1).reshape(gs[1:])
            off += n
    return out


def _small_pieces(sharded, repl):
    return [sharded[n].reshape(-1) for n in SMALL_NAMES] + [repl[n].reshape(-1) for n in REPL_NAMES]


def _pack_small_local(d):
    flat = jnp.concatenate(_small_pieces(d, d))
    return _seg(flat, F32_SEG).reshape(-1, 128)


def _pack_small_grads(G, loss):
    pieces = []
    for name in SMALL_NAMES:
        gs, ax = SHARDED[name]
        g = G[name]
        if name == 'ffn_conv_w':
            sh = jnp.swapaxes(g.reshape(3, N_DEV, -1), 0, 1)
        elif ax == 1:
            sh = g
        else:
            sh = jnp.swapaxes(g.reshape(g.shape[0], N_DEV, -1), 0, 1)
        pieces.append(sh.reshape(N_DEV, -1))
    rep = jnp.concatenate([G[n].reshape(-1) for n in REPL_NAMES] + [loss.reshape(1)])
    pieces.append(jnp.broadcast_to(rep[None, :], (N_DEV, rep.shape[0])))
    flat = _seg(jnp.concatenate(pieces, axis=1), F32_SEG)
    return flat.reshape(N_DEV, -1, 128)


def _unpack_small_local(packed, local_shapes):
    flat = packed.reshape(-1)
    out, off = {}, 0
    for name in SMALL_NAMES + REPL_NAMES:
        n = int(np.prod(local_shapes[name]))
        out[name] = flat[off:off + n].reshape(local_shapes[name])
        off += n
    return out


def kernel(x, *rest):
    nw = len(WEIGHT_NAMES)
    assert len(rest) == 3 * nw + 1
    weights = dict(zip(WEIGHT_NAMES, rest[:nw]))
    loss_target = rest[nw]
    moms = dict(zip(WEIGHT_NAMES, rest[nw + 1:2 * nw + 1]))
    vars_ = dict(zip(WEIGHT_NAMES, rest[2 * nw + 1:]))
    local_shapes = {n: weights[n].shape for n in WEIGHT_NAMES}

    def native2d(name, a):
        a2 = a.reshape(a.shape[-2], a.shape[-1])
        return a2.T if name in BIG_T else a2

    def from2d(name, a2):
        return (a2.T if name in BIG_T else a2).reshape(local_shapes[name])

    big_w = {n: native2d(n, weights[n]) for n in BIG_NAMES}
    w_in_t_sh = big_w['w_in'].astype(BF16)
    late = [big_w[n].astype(BF16) for n in LATE_NAMES]
    small_sh = _pack_small_shards({n: weights[n] for n in SMALL_NAMES})
    w_in_g, small_g = _exchange("weights_all_gather", [w_in_t_sh, small_sh], False)
    W = _unpack_small(small_g)
    W['w_in_t'] = w_in_g.reshape(-1, D_MODEL)
    for n in REPL_NAMES:
        W[n] = weights[n][0] if n != 'norm_final_g' else weights[n]

    loss, dx, G, late_parts, w_in_parts = _local_step(x[0], loss_target[0], W, late)

    late_items = [(parts, big_w[n], native2d(n, moms[n]), native2d(n, vars_[n]))
                  for n, parts in zip(LATE_NAMES, late_parts)]
    late_res, (small_parts,) = _adamw_late(late_items, [_pack_small_grads(G, loss)])
    results = {n: [from2d(n, r) for r in res] for n, res in zip(LATE_NAMES, late_res)}
    res = _adamw("adamw_w_in", w_in_parts, big_w['w_in'], native2d('w_in', moms['w_in']), native2d('w_in', vars_['w_in']))
    results['w_in'] = [from2d('w_in', r) for r in res]
    small_res = _adamw("adamw_small", small_parts, _pack_small_local(weights), _pack_small_local(moms),
                       _pack_small_local(vars_))
    small_out = [_unpack_small_local(p, local_shapes) for p in small_res]

    n_small = sum(int(np.prod(local_shapes[n])) for n in SMALL_NAMES + REPL_NAMES)
    loss = small_res[0].reshape(-1)[n_small]
    outs = [loss, dx[None]]
    for j in range(4):
        outs += [results[n][j] if n in results else small_out[j][n] for n in WEIGHT_NAMES]
    return tuple(outs)
```

```python
import functools
import math

import numpy as np
import jax
import jax.numpy as jnp
from jax import lax
from jax.experimental import pallas as pl
from jax.experimental.pallas import tpu as pltpu

F32 = jnp.float32
BF16 = jnp.bfloat16

N_DEV = 8
D_MODEL = 1024
RWKV_HEADS = 8
RWKV_HEAD_DIM = 64
RWKV_WIDTH = 512
RET_HEADS = 4
RET_HEAD_DIM = 128
RET_WIDTH = 512
LORA_PAD = 128
D_FF = 2816
NORM_EPS = 1e-6
RWKV_GN_EPS = 64e-5
RET_GN_EPS = 1e-5
ROPE_BASE = 10000.0
ADAM_LR, ADAM_B1, ADAM_B2, ADAM_EPS, ADAM_WD, ADAM_STEP = 0.001, 0.9, 0.999, 1e-08, 0.01, 10

VMEM_LIMIT = 56 * 1024 * 1024
TOK_BLOCK = 256
LIGHT_TOK_BLOCK = 512
SCAN_CHUNK = 64
ATT_BLOCK = 512
BF16_SEG = 2048
F32_SEG = 1024

WEIGHT_NAMES = ['norm_mix_g', 'w_in', 'rwkv_mu_r', 'rwkv_mu_k', 'rwkv_mu_v', 'rwkv_mu_w', 'rwkv_mu_a',
                'rwkv_mu_g', 'rwkv_w0', 'rwkv_w1', 'rwkv_w2', 'rwkv_a0', 'rwkv_a1', 'rwkv_a2', 'rwkv_g1',
                'rwkv_g2', 'rwkv_k_k', 'rwkv_k_a', 'rwkv_r_k', 'rwkv_lnx_w', 'rwkv_lnx_b', 'ret_gn_w',
                'w_out', 'norm_ffn_g', 'ffn_w_gate', 'ffn_w_up', 'ffn_conv_w', 'ffn_conv_b', 'ffn_w_down',
                'norm_final_g']
SHARDED = {
    'w_in': ((1, 1024, 3584), 2), 'rwkv_w1': ((1, 1024, 64), 1), 'rwkv_w2': ((1, 64, 512), 2),
    'rwkv_a1': ((1, 1024, 64), 1), 'rwkv_a2': ((1, 64, 512), 2), 'rwkv_g1': ((1, 1024, 128), 1),
    'rwkv_g2': ((1, 128, 512), 2), 'w_out': ((1, 1024, 1024), 1), 'ffn_w_gate': ((1, 1024, 2816), 2),
    'ffn_w_up': ((1, 1024, 2816), 2), 'ffn_conv_w': ((1, 3, 1, 2816), 3), 'ffn_w_down': ((1, 2816, 1024), 1),
}
REPL_NAMES = [n for n in WEIGHT_NAMES if n not in SHARDED]
BIG_NAMES = ['w_in', 'w_out', 'ffn_w_gate', 'ffn_w_up', 'ffn_w_down']
BIG_T = ('w_in', 'ffn_w_gate', 'ffn_w_up')
LATE_NAMES = ['w_out', 'ffn_w_gate', 'ffn_w_up', 'ffn_w_down']
SMALL_NAMES = [n for n in WEIGHT_NAMES if n in SHARDED and n not in BIG_NAMES]


def _cparams(n_grid):
    return pltpu.CompilerParams(dimension_semantics=("arbitrary",) * n_grid, vmem_limit_bytes=VMEM_LIMIT)


def _round_up(n, m):
    return (n + m - 1) // m * m


@jax.custom_vjp
def _bdot(x, w):
    return jnp.dot(x.astype(BF16), w.astype(BF16), preferred_element_type=F32)


def _bdot_fwd(x, w):
    return _bdot(x, w), (x, w)


def _bdot_bwd(res, g):
    x, w = res
    gb = g.astype(BF16)
    dx = lax.dot_general(gb, w.astype(BF16), (((1,), (1,)), ((), ())), preferred_element_type=F32)
    dw = lax.dot_general(x.astype(BF16), gb, (((0,), (0,)), ((), ())), preferred_element_type=F32)
    return dx, dw.astype(w.dtype)


_bdot.defvjp(_bdot_fwd, _bdot_bwd)


@jax.custom_vjp
def _shift_rows(x, prev):
    rolled = pltpu.roll(x, 1, 0)
    row = lax.broadcasted_iota(jnp.int32, x.shape, 0)
    return jnp.where(row == 0, jnp.broadcast_to(prev, x.shape), rolled)


def _shift_rows_fwd(x, prev):
    return _shift_rows(x, prev), None


def _shift_rows_bwd(_, g):
    n = g.shape[0]
    rolled = pltpu.roll(g, n - 1, 0)
    row = lax.broadcasted_iota(jnp.int32, g.shape, 0)
    return jnp.where(row == n - 1, 0.0, rolled), g[0:1, :]


_shift_rows.defvjp(_shift_rows_fwd, _shift_rows_bwd)


@jax.custom_vjp
def _swap_halves(x):
    return pltpu.roll(x, 64, 1)


_swap_halves.defvjp(lambda x: (_swap_halves(x), None), lambda _, g: (pltpu.roll(g, 64, 1),))


def _sigmoid(x):
    return 1.0 / (1.0 + jnp.exp(-x))


def _softplus(x):
    return jnp.maximum(x, 0.0) + jnp.log(1.0 + jnp.exp(-jnp.abs(x)))


def _rms_fn(x, g):
    return x * lax.rsqrt(jnp.mean(x * x, axis=-1, keepdims=True) + NORM_EPS) * g


def _pre_a_fn(h1, h1p, p, pp, mu_w, mu_a, mu_g, mu_r, mu_k, mu_v, w0, w1, w2, a0, a1, a2, g1, g2, k_k, k_a):
    W = RWKV_WIDTH
    h1s = _shift_rows(h1, h1p)
    ps = _shift_rows(p, pp)
    dx = h1s - h1
    xw = h1 + dx * mu_w
    xa = h1 + dx * mu_a
    xg = h1 + dx * mu_g
    dp = ps - p
    r = p[:, 0:W] + dp[:, 0:W] * mu_r
    k0 = p[:, W:2 * W] + dp[:, W:2 * W] * mu_k
    v = p[:, 2 * W:3 * W] + dp[:, 2 * W:3 * W] * mu_v
    wl = w0 + _bdot(jnp.tanh(_bdot(xw, w1)), w2)
    w_log = -_softplus(-wl) - 0.5
    lw = -jnp.exp(w_log)
    a = _sigmoid(a0 + _bdot(_bdot(xa, a1), a2))
    g = _bdot(_sigmoid(_bdot(xg, g1)), g2)
    nkk, k, b = _pre_b_fn(k0, a, k_k, k_a)
    return r, k, v, lw, nkk, b, g


def _head_sum_raw(x):
    n = x.shape[1]
    ii = lax.broadcasted_iota(jnp.int32, (n, n), 0) // RWKV_HEAD_DIM
    jj = lax.broadcasted_iota(jnp.int32, (n, n), 1) // RWKV_HEAD_DIM
    ones = (ii == jj).astype(BF16)
    xh = x.astype(BF16)
    xl = (x - xh.astype(F32)).astype(BF16)
    return jnp.dot(xh, ones, preferred_element_type=F32) + jnp.dot(xl, ones, preferred_element_type=F32)


@jax.custom_vjp
def _head_sum(x):
    return _head_sum_raw(x)


_head_sum.defvjp(lambda x: (_head_sum_raw(x), None), lambda _, g: (_head_sum_raw(g),))


def _pre_b_fn(k0, a, k_k, k_a):
    kkr = k0 * k_k
    nrm = jnp.sqrt(_head_sum(kkr * kkr))
    kk = kkr / jnp.maximum(nrm, 1e-12)
    k = k0 * (1.0 + (a - 1.0) * k_a)
    return -kk, k, kk * a


def _rwkv_post_fn(y, r, k, v, g, lnx_w, lnx_b, r_k):
    inv = 1.0 / RWKV_HEAD_DIM
    mu = _head_sum(y) * inv
    yc = y - mu
    var = _head_sum(yc * yc) * inv
    yn = yc * lax.rsqrt(var + RWKV_GN_EPS) * lnx_w + lnx_b
    bonus = _head_sum(r * k * r_k) * v
    return ((yn + bonus) * g,)


def _rotary_fn(cos2, sin2, q, k):
    qs, ks = [], []
    for h in range(RET_HEADS):
        sl = slice(h * RET_HEAD_DIM, (h + 1) * RET_HEAD_DIM)
        qh, kh = q[:, sl], k[:, sl]
        qs.append(qh * cos2 + _swap_halves(qh) * sin2)
        ks.append((kh * cos2 + _swap_halves(kh) * sin2) * (RET_HEAD_DIM ** -0.5))
    return jnp.concatenate(qs, axis=1), jnp.concatenate(ks, axis=1)


def _ret_post_fn(y, gp, gn_w):
    outs = []
    for h in range(RET_HEADS):
        sl = slice(h * RET_HEAD_DIM, (h + 1) * RET_HEAD_DIM)
        yh = y[:, sl]
        mu = jnp.mean(yh, axis=-1, keepdims=True)
        yc = yh - mu
        var = jnp.mean(yc * yc, axis=-1, keepdims=True)
        outs.append(yc * lax.rsqrt(var + RET_GN_EPS) * gn_w[:, sl])
    yn = jnp.concatenate(outs, axis=1)
    return (gp * _sigmoid(gp) * yn,)


class _Cols:
    def __init__(self, array, width, block):
        self.array, self.width, self.block = array, width, block
        self.shape, self.ndim, self.dtype = (array.shape[0], width), 2, array.dtype


def _arr(a):
    return a.array if isinstance(a, _Cols) else a


def _blk_spec(a, tb, rev_nb=None):
    nd = a.ndim
    tail = (a.block,) if isinstance(a, _Cols) else (0,) * (nd - 1)
    if rev_nb is None:
        return pl.BlockSpec((tb,) + a.shape[1:], lambda i: (i,) + tail)
    return pl.BlockSpec((tb,) + a.shape[1:], lambda i: (rev_nb - 1 - i,) + tail)


def _full_spec(a):
    nd = a.ndim
    return pl.BlockSpec(a.shape, lambda i: (0,) * nd)


def _tok_fwd(name, fn, toks, consts, out_tails, tb=LIGHT_TOK_BLOCK, out_dtypes=None):
    out_dtypes = out_dtypes or [F32] * len(out_tails)
    n_in = len(toks) + len(consts)
    tn = toks[0].shape[0]

    def body(*refs):
        outs = fn(*[r[...] for r in refs[:n_in]])
        for r, o in zip(refs[n_in:], outs):
            r[...] = o.astype(r.dtype)

    out_shape = [jax.ShapeDtypeStruct((tn,) + tuple(s), dt) for s, dt in zip(out_tails, out_dtypes)]
    return pl.pallas_call(
        body, name=name, grid=(tn // tb,),
        in_specs=[_blk_spec(a, tb) for a in toks] + [_full_spec(c) for c in consts],
        out_specs=[_blk_spec(o, tb) for o in out_shape], out_shape=out_shape,
        compiler_params=_cparams(1))(*[_arr(a) for a in toks], *consts)


def _tok_bwd(name, fn, aux, toks, consts, cts, add=None, tb=LIGHT_TOK_BLOCK, tok_dtypes=None):
    n_aux, n_tok, n_c = len(aux), len(toks), len(consts)
    ct_groups = [c if isinstance(c, (tuple, list)) else (c,) for c in cts]
    ct_flat = [a for grp in ct_groups for a in grp]
    n_ct = len(ct_flat)
    n_add = 0 if add is None else 1
    tn = toks[0].shape[0]

    def body(*refs):
        pos = 0
        aux_v = [r[...] for r in refs[pos:pos + n_aux]]; pos += n_aux
        tok_v = [r[...] for r in refs[pos:pos + n_tok]]; pos += n_tok
        const_v = [r[...] for r in refs[pos:pos + n_c]]; pos += n_c
        ct_refs = refs[pos:pos + n_ct]; pos += n_ct
        add_refs = refs[pos:pos + n_add]; pos += n_add
        dtok_refs = refs[pos:pos + n_tok]; pos += n_tok
        dconst_refs = refs[pos:pos + n_c]
        ct_v, q = [], 0
        for grp in ct_groups:
            s = ct_refs[q][...]
            for r in ct_refs[q + 1:q + len(grp)]:
                s = s + r[...]
            q += len(grp)
            ct_v.append(s)
        _, vjp = jax.vjp(lambda *tc: fn(*aux_v, *tc), *tok_v, *const_v)
        grads = vjp(tuple(ct_v))
        for j, r in enumerate(dtok_refs):
            gj = grads[j]
            if j == 0 and n_add:
                gj = gj + add_refs[0][...]
            r[...] = gj.astype(r.dtype)

        @pl.when(pl.program_id(0) == 0)
        def _():
            for r in dconst_refs:
                r[...] = jnp.zeros(r.shape, F32)

        for j, r in enumerate(dconst_refs):
            r[...] += grads[n_tok + j]

    ins = list(aux) + list(toks) + list(consts) + ct_flat + ([add] if n_add else [])
    in_specs = ([_blk_spec(a, tb) for a in aux] + [_blk_spec(a, tb) for a in toks] + [_full_spec(c) for c in consts]
                + [_blk_spec(a, tb) for a in ct_flat] + ([_blk_spec(add, tb)] if n_add else []))
    tok_dtypes = tok_dtypes or [F32] * n_tok
    out_shape = ([jax.ShapeDtypeStruct(a.shape, dt) for a, dt in zip(toks, tok_dtypes)]
                 + [jax.ShapeDtypeStruct(c.shape, F32) for c in consts])
    out_specs = [_blk_spec(o, tb) for o in out_shape[:n_tok]] + [_full_spec(c) for c in consts]
    return pl.pallas_call(body, name=name, grid=(tn // tb,), in_specs=in_specs, out_specs=out_specs,
                          out_shape=out_shape, compiler_params=_cparams(1))(*[_arr(a) for a in ins])


MM_VMEM_BUDGET = 40 * 1024 * 1024
MM_STEP_SECONDS = 0.4e-6
MM_HBM_BYTES_PER_SECOND = 2.5e12
MM_XPOSE_SECONDS_PER_ELEM = 2e-12
MM_MXU_COLUMNS = 256
MM_MXU_FLOPS = 9e14


def _mm_tiles(m, n, kd, a_bytes, b_bytes, o_bytes, has_add, ta):
    divs = lambda d: [t for t in range(128, d + 1, 128) if d % t == 0]
    best = None
    for tm in divs(m):
        for tn in divs(n):
            for tk in divs(kd):
                ni, nj, nk = m // tm, n // tn, kd // tk
                vmem = (2 * tm * tk * a_bytes + 2 * tk * tn * b_bytes + tm * tn * 4 + 2 * tm * tn * o_bytes
                        + (2 * tm * tn * 4 if has_add else 0) + 2 * (tm * tk + tk * tn) + tm * tn * 4)
                if vmem > MM_VMEM_BUDGET:
                    continue
                a_traffic = m * kd * a_bytes * (nj if nk > 1 else 1)
                b_traffic = kd * n * b_bytes * (ni if nj * nk > 1 else 1)
                cost = ni * nj * nk * MM_STEP_SECONDS + (a_traffic + b_traffic) / MM_HBM_BYTES_PER_SECOND
                cost += 2.0 * m * kd * nj * max(tn, MM_MXU_COLUMNS) / MM_MXU_FLOPS
                if ta:
                    cost += m * kd * nj * MM_XPOSE_SECONDS_PER_ELEM
                if best is None or cost < best[0]:
                    best = (cost, tm, tn, tk)
    return best[1:]


def _mm(name, a, b, ta=False, tb=False, add=None, out_dtype=F32, xs=()):
    n_x = len(xs)
    if ta:
        kd, m = a.shape
    else:
        m, kd = a.shape
    if tb:
        n, kb = b.shape
    else:
        kb, n = b.shape
    assert kd == kb, (a.shape, b.shape)
    tm, tn, tk = _mm_tiles(m, n, kd, a.dtype.itemsize, b.dtype.itemsize, jnp.dtype(out_dtype).itemsize,
                           add is not None, ta)
    nk = kd // tk
    has_add = add is not None
    dims = (((0 if ta else 1,), (1 if tb else 0,)), ((), ()))

    n_in = 3 if add is not None else 2
    nj = n // tn
    n_steps = (m // tm) * nj * nk

    def body(*refs):
        a_ref, b_ref = refs[0], refs[1]
        x_src, o_ref = refs[n_in:n_in + n_x], refs[n_in + n_x]
        x_dst, acc_ref, sems = refs[n_in + n_x + 1:n_in + 2 * n_x + 1], refs[n_in + 2 * n_x + 1], refs[n_in + 2 * n_x + 2:]
        k = pl.program_id(2)
        step = (pl.program_id(0) * nj + pl.program_id(1)) * nk + k

        if n_x:
            @pl.when(step == 0)
            def _():
                _scatter_start(x_src, x_dst, sems)

        @pl.when(k == 0)
        def _():
            acc_ref[...] = refs[2][...] if has_add else jnp.zeros(acc_ref.shape, F32)

        acc_ref[...] += lax.dot_general(a_ref[...].astype(BF16), b_ref[...].astype(BF16), dims,
                                        preferred_element_type=F32)

        @pl.when(k == nk - 1)
        def _():
            o_ref[...] = acc_ref[...].astype(out_dtype)

        if n_x:
            @pl.when(step == n_steps - 1)
            def _():
                _scatter_wait(x_src, x_dst, sems)

    a_spec = pl.BlockSpec((tk, tm), lambda i, j, k: (k, i)) if ta else pl.BlockSpec((tm, tk), lambda i, j, k: (i, k))
    b_spec = pl.BlockSpec((tn, tk), lambda i, j, k: (j, k)) if tb else pl.BlockSpec((tk, tn), lambda i, j, k: (k, j))
    o_spec = pl.BlockSpec((tm, tn), lambda i, j, k: (i, j))
    ins = [a, b] + ([add] if has_add else [])
    in_specs = [a_spec, b_spec] + ([o_spec] if has_add else [])
    res = pl.pallas_call(body, name=name, grid=(m // tm, n // tn, nk), in_specs=in_specs + [_ANY] * n_x,
                         out_specs=[o_spec] + [_ANY] * n_x,
                         out_shape=[jax.ShapeDtypeStruct((m, n), out_dtype)] + _xchg_out_shapes(xs, True),
                         scratch_shapes=[pltpu.VMEM((tm, tn), F32)] + (_xchg_sems(n_x) if n_x else []),
                         compiler_params=_cparams(3))(*ins, *xs)
    return (res[0], res[1:]) if n_x else res[0]


def _prev8_spec(a, tb, rev_nb=None):
    r = tb // 8
    if rev_nb is None:
        return pl.BlockSpec((8, a.shape[1]), lambda i: (jnp.maximum(i * r - 1, 0), 0))
    return pl.BlockSpec((8, a.shape[1]), lambda i: (jnp.maximum((rev_nb - 1 - i) * r - 1, 0), 0))


def _pre_a_fwd(h1, p, consts, tb=TOK_BLOCK):
    tn = h1.shape[0]

    def body(h1_ref, h1h_ref, p_ref, ph_ref, *rest):
        c_refs, o_refs = rest[:len(consts)], rest[len(consts):]
        first = pl.program_id(0) == 0
        h1p = jnp.where(first, 0.0, h1h_ref[7:8, :])
        pp = jnp.where(first, 0.0, ph_ref[7:8, :])
        outs = _pre_a_fn(h1_ref[...], h1p, p_ref[...], pp, *[c[...] for c in c_refs])
        for r, o in zip(o_refs, outs):
            r[...] = o

    out_shape = [jax.ShapeDtypeStruct((tn, RWKV_WIDTH), F32) for _ in range(7)]
    return pl.pallas_call(
        body, name="rwkv_pre_a_fwd", grid=(tn // tb,),
        in_specs=[_blk_spec(h1, tb), _prev8_spec(h1, tb), _blk_spec(p, tb), _prev8_spec(p, tb)]
        + [_full_spec(c) for c in consts],
        out_specs=[_blk_spec(o, tb) for o in out_shape], out_shape=out_shape,
        compiler_params=_cparams(1))(h1, h1, _arr(p), _arr(p), *consts)


def _pre_a_bwd(h1, p, consts, cts, tb=TOK_BLOCK):
    tn = h1.shape[0]
    nb = tn // tb
    n_c = len(consts)
    ct_groups = [c if isinstance(c, (tuple, list)) else (c,) for c in cts]
    ct_flat = [a for grp in ct_groups for a in grp]
    n_ct = len(ct_flat)

    def body(*refs):
        h1_ref, h1h_ref, p_ref, ph_ref = refs[:4]
        c_refs = refs[4:4 + n_c]
        ct_refs = refs[4 + n_c:4 + n_c + n_ct]
        dh1_ref, dp_ref = refs[4 + n_c + n_ct:6 + n_c + n_ct]
        dc_refs = refs[6 + n_c + n_ct:6 + 2 * n_c + n_ct]
        ch_ref, cp_ref = refs[-2], refs[-1]
        i = pl.program_id(0)
        first_block = i == nb - 1
        h1p = jnp.where(first_block, 0.0, h1h_ref[7:8, :])
        pp = jnp.where(first_block, 0.0, ph_ref[7:8, :])
        ct_v, q = [], 0
        for grp in ct_groups:
            s = ct_refs[q][...]
            for r in ct_refs[q + 1:q + len(grp)]:
                s = s + r[...]
            q += len(grp)
            ct_v.append(s)
        _, vjp = jax.vjp(_pre_a_fn, h1_ref[...], h1p, p_ref[...], pp, *[c[...] for c in c_refs])
        grads = vjp(tuple(ct_v))

        @pl.when(i == 0)
        def _():
            ch_ref[...] = jnp.zeros(ch_ref.shape, F32)
            cp_ref[...] = jnp.zeros(cp_ref.shape, F32)
            for r in dc_refs:
                r[...] = jnp.zeros(r.shape, F32)

        rowh = lax.broadcasted_iota(jnp.int32, (tb, h1.shape[1]), 0)
        rowp = lax.broadcasted_iota(jnp.int32, (tb, p.shape[1]), 0)
        dh1_ref[...] = grads[0] + jnp.where(rowh == tb - 1, jnp.broadcast_to(ch_ref[0:1, :], rowh.shape), 0.0)
        dp_ref[...] = (grads[2] + jnp.where(rowp == tb - 1, jnp.broadcast_to(cp_ref[0:1, :], rowp.shape), 0.0)
                       ).astype(dp_ref.dtype)
        ch_ref[0:1, :] = grads[1]
        cp_ref[0:1, :] = grads[3]
        for j, r in enumerate(dc_refs):
            r[...] += grads[4 + j]

    ins = [h1, h1, _arr(p), _arr(p)] + list(consts) + ct_flat
    in_specs = ([_blk_spec(h1, tb, nb), _prev8_spec(h1, tb, nb), _blk_spec(p, tb, nb), _prev8_spec(p, tb, nb)]
                + [_full_spec(c) for c in consts] + [_blk_spec(a, tb, nb) for a in ct_flat])
    out_shape = ([jax.ShapeDtypeStruct(h1.shape, F32), jax.ShapeDtypeStruct(p.shape, BF16)]
                 + [jax.ShapeDtypeStruct(c.shape, F32) for c in consts])
    out_specs = [_blk_spec(h1, tb, nb), _blk_spec(p, tb, nb)] + [_full_spec(c) for c in consts]
    return pl.pallas_call(body, name="rwkv_pre_a_bwd", grid=(nb,), in_specs=in_specs, out_specs=out_specs,
                          out_shape=out_shape,
                          scratch_shapes=[pltpu.VMEM((8, h1.shape[1]), F32), pltpu.VMEM((8, p.shape[1]), F32)],
                          compiler_params=_cparams(1))(*ins)


def _my_index():
    return 4 * lax.axis_index("x") + 2 * lax.axis_index("y") + lax.axis_index("c")


def _peer(k):
    x, y, c = lax.axis_index("x"), lax.axis_index("y"), lax.axis_index("c")
    px = 1 - x if k & 4 else x
    py = 1 - y if k & 2 else y
    pc = 1 - c if k & 1 else c
    return (px, py, pc), 4 * px + 2 * py + pc


def _xchg_sems(n):
    return [pltpu.SemaphoreType.DMA((n * (N_DEV - 1),)), pltpu.SemaphoreType.DMA((n * (N_DEV - 1),)),
            pltpu.SemaphoreType.DMA((n,))]


def _scatter_copies(srcs, dsts, sems, incoming=False):
    send_sems, recv_sems, local_sems = sems
    me = _my_index()
    local, remote = [], []
    for i, (s, d) in enumerate(zip(srcs, dsts)):
        if not incoming:
            local.append(pltpu.make_async_copy(s.at[me], d.at[me], local_sems.at[i]))
        for k in range(1, N_DEV):
            peer, plin = _peer(k)
            j = i * (N_DEV - 1) + k - 1
            s_slot, d_slot = (me, plin) if incoming else (plin, me)
            remote.append(pltpu.make_async_remote_copy(
                src_ref=s.at[s_slot], dst_ref=d.at[d_slot], send_sem=send_sems.at[j],
                recv_sem=recv_sems.at[j], device_id=peer, device_id_type=pl.DeviceIdType.MESH))
    return local, remote


def _scatter_start(srcs, dsts, sems):
    local, out = _scatter_copies(srcs, dsts, sems)
    for cp in local + out:
        cp.start()


def _scatter_wait(srcs, dsts, sems):
    for cp in _scatter_copies(srcs, dsts, sems, incoming=True)[1]:
        cp.wait_recv()
    local, out = _scatter_copies(srcs, dsts, sems)
    for cp in out:
        cp.wait_send()
    for cp in local:
        cp.wait()


_ICI_PEERS = (2, 4, 6)


def _gather_copies(srcs, dsts, sems, group):
    send_sems, recv_sems, local_sems = sems
    me = _my_index()
    sib, sib_lin = _peer(1)
    out = []
    for i, (s, d) in enumerate(zip(srcs, dsts)):
        def mk(q, src, dst, dev):
            j = i * (N_DEV - 1) + q
            return pltpu.make_async_remote_copy(src_ref=src, dst_ref=dst, send_sem=send_sems.at[j],
                                                recv_sem=recv_sems.at[j], device_id=dev,
                                                device_id_type=pl.DeviceIdType.MESH)
        if group == 'local':
            out.append(pltpu.make_async_copy(s, d.at[me], local_sems.at[i]))
        elif group == 'own':
            out.append(mk(0, s, d.at[me], sib))
        elif group == 'in_d2d':
            out.append(mk(0, s, d.at[sib_lin], sib))
        for jj, k in enumerate(_ICI_PEERS):
            peer, plin = _peer(k)
            plin_other = _peer(k + 1)[1]
            if group == 'own':
                out.append(mk(1 + jj, s, d.at[me], peer))
            elif group == 'in_ici':
                out.append(mk(1 + jj, s, d.at[plin], peer))
            elif group == 'pass_on':
                out.append(mk(4 + jj, d.at[plin], d.at[plin], sib))
            elif group == 'in_d2d':
                out.append(mk(4 + jj, d.at[plin_other], d.at[plin_other], sib))
    return out


def _gather_start(srcs, dsts, sems):
    for cp in _gather_copies(srcs, dsts, sems, 'local') + _gather_copies(srcs, dsts, sems, 'own'):
        cp.start()


def _gather_pass_on(srcs, dsts, sems):
    for cp in _gather_copies(srcs, dsts, sems, 'in_ici'):
        cp.wait_recv()
    for cp in _gather_copies(srcs, dsts, sems, 'pass_on'):
        cp.start()


def _gather_finish(srcs, dsts, sems):
    for cp in _gather_copies(srcs, dsts, sems, 'in_d2d'):
        cp.wait_recv()
    for cp in _gather_copies(srcs, dsts, sems, 'own') + _gather_copies(srcs, dsts, sems, 'pass_on'):
        cp.wait_send()
    for cp in _gather_copies(srcs, dsts, sems, 'local'):
        cp.wait()


def _xchg_out_shapes(srcs, scatter):
    return [jax.ShapeDtypeStruct(s.shape if scatter else (N_DEV,) + s.shape, s.dtype) for s in srcs]


_ANY = pl.BlockSpec(memory_space=pl.ANY)


def _exchange(name, srcs, scatter):
    n = len(srcs)

    def body(*refs):
        s, d, sems = refs[:n], refs[n:2 * n], refs[2 * n:]
        if scatter:
            _scatter_start(s, d, sems)
            _scatter_wait(s, d, sems)
        else:
            _gather_start(s, d, sems)
            _gather_pass_on(s, d, sems)
            _gather_finish(s, d, sems)

    return pl.pallas_call(body, name=name, in_specs=[_ANY] * n, out_specs=[_ANY] * n,
                          out_shape=_xchg_out_shapes(srcs, scatter), scratch_shapes=_xchg_sems(n))(*srcs)


_MM_DIMS = {'nn': (((1,), (0,)), ((), ())), 'nt': (((1,), (1,)), ((), ())), 'tn': (((0,), (0,)), ((), ()))}


def _cmm_raw(x, y, kind, split):
    dot = functools.partial(lax.dot_general, dimension_numbers=_MM_DIMS[kind], preferred_element_type=F32)
    xh, yh = x.astype(BF16), y.astype(BF16)
    out = dot(xh, yh)
    if split:
        xl = (x - xh.astype(F32)).astype(BF16)
        yl = (y - yh.astype(F32)).astype(BF16)
        out = out + (dot(xh, yl) + dot(xl, yh))
    return out


@functools.partial(jax.custom_vjp, nondiff_argnums=(2, 3))
def _cmm(x, y, kind, split=False):
    return _cmm_raw(x, y, kind, split)


def _cmm_fwd(x, y, kind, split):
    return _cmm_raw(x, y, kind, split), (x, y)


def _cmm_bwd(kind, split, res, g):
    x, y = res
    if kind == 'nn':
        return _cmm_raw(g, y, 'nt', split), _cmm_raw(x, g, 'tn', split)
    if kind == 'nt':
        return _cmm_raw(g, y, 'nn', split), _cmm_raw(g, x, 'tn', split)
    return _cmm_raw(y, g, 'nt', split), _cmm_raw(x, g, 'nn', split)


_cmm.defvjp(_cmm_fwd, _cmm_bwd)


def _tri_sum_raw(tri, x, kind):
    dot = functools.partial(lax.dot_general, dimension_numbers=_MM_DIMS[kind], preferred_element_type=F32)
    tb = tri.astype(BF16)
    hi, mid, lo = _split3(x)
    return (dot(tb, hi) + dot(tb, mid)) + dot(tb, lo)


@functools.partial(jax.custom_vjp, nondiff_argnums=(2,))
def _tri_sum(tri, x, kind):
    return _tri_sum_raw(tri, x, kind)


def _tri_sum_fwd(tri, x, kind):
    return _tri_sum_raw(tri, x, kind), tri


def _tri_sum_bwd(kind, tri, g):
    return jnp.zeros_like(tri), _tri_sum_raw(tri, g, 'tn' if kind == 'nn' else 'nn')


_tri_sum.defvjp(_tri_sum_fwd, _tri_sum_bwd)


def _chunk_fn(S0, r, lw, k, v, a, b):
    hs = range(len(r))
    C = r[0].shape[0]
    ii = lax.broadcasted_iota(jnp.int32, (C, C), 0)
    jj = lax.broadcasted_iota(jnp.int32, (C, C), 1)
    incl, strict = ii >= jj, ii > jj
    eye = (ii == jj).astype(F32)
    inclf = incl.astype(F32)
    cum = [_tri_sum(inclf, lw[h], 'nn') for h in hs]
    e_inv = [jnp.exp(-cum[h]) for h in hs]
    At = [a[h] * jnp.exp(cum[h] - lw[h]) for h in hs]
    Rt = [r[h] * jnp.exp(cum[h]) for h in hs]
    Kh = [k[h] * e_inv[h] for h in hs]
    Bh = [b[h] * e_inv[h] for h in hs]
    Mab = [jnp.where(strict, _cmm(At[h], Bh[h], 'nt'), 0.0) for h in hs]
    Mak = [jnp.where(strict, _cmm(At[h], Kh[h], 'nt', True), 0.0) for h in hs]
    Mrk = [jnp.where(incl, _cmm(Rt[h], Kh[h], 'nt', True), 0.0) for h in hs]
    Mrb = [jnp.where(incl, _cmm(Rt[h], Bh[h], 'nt'), 0.0) for h in hs]
    rhs = [_cmm(At[h], S0[h], 'nt') + _cmm(Mak[h], v[h], 'nn') for h in hs]
    P = Mab
    Tm = [eye + P[h] for h in hs]
    n = 1
    while 2 * n < C:
        P = [_cmm(P[h], P[h], 'nn') for h in hs]
        Tm = [_cmm(Tm[h], eye + P[h], 'nn') for h in hs]
        n *= 2
    U = [_cmm(Tm[h], rhs[h], 'nn', True) for h in hs]
    Y = [_cmm(Rt[h], S0[h], 'nt') + _cmm(Mrk[h], v[h], 'nn') + _cmm(Mrb[h], U[h], 'nn') for h in hs]
    gC = [jnp.exp(jnp.sum(lw[h], axis=0, keepdims=True)) for h in hs]
    SC = [S0[h] * gC[h] + _cmm(v[h], Kh[h] * gC[h], 'tn') + _cmm(U[h], Bh[h] * gC[h], 'tn') for h in hs]
    return tuple(Y), tuple(SC)


def _cscan_fwd(r, lw, k, v, a, b, xs):
    n_x = len(xs)
    tn = r.shape[0]
    H, Dh, Dv = RWKV_HEADS, RWKV_HEAD_DIM, RWKV_HEAD_DIM
    nc = tn // SCAN_CHUNK
    lanes = lambda h: slice(h * Dh, (h + 1) * Dh)
    heads = lambda ref: tuple(ref[:, lanes(h)] for h in range(H))
    mats = lambda ref: tuple(ref[h] for h in range(H))

    def body(r_ref, lw_ref, k_ref, v_ref, a_ref, b_ref, *rest):
        x_src, (y_ref, ck_ref) = rest[:n_x], rest[n_x:n_x + 2]
        x_dst, s_ref, sems = rest[n_x + 2:2 * n_x + 2], rest[2 * n_x + 2], rest[2 * n_x + 3:]

        @pl.when(pl.program_id(0) == 0)
        def _():
            s_ref[...] = jnp.zeros(s_ref.shape, F32)
            _gather_start(x_src, x_dst, sems)

        ck_ref[0] = s_ref[...]
        y, sc = _chunk_fn(mats(s_ref), heads(r_ref), heads(lw_ref), heads(k_ref), heads(v_ref), heads(a_ref),
                          heads(b_ref))
        for h in range(H):
            y_ref[:, lanes(h)] = y[h]
            s_ref[h] = sc[h]

        @pl.when(pl.program_id(0) == max(nc - 4, 0))
        def _():
            _gather_pass_on(x_src, x_dst, sems)

        @pl.when(pl.program_id(0) == nc - 1)
        def _():
            _gather_finish(x_src, x_dst, sems)

    hm = pl.BlockSpec((SCAN_CHUNK, H * Dh), lambda c: (c, 0))
    res = pl.pallas_call(
        body, name="rwkv_scan_fwd", grid=(nc,), in_specs=[hm] * 6 + [_ANY] * n_x,
        out_specs=[hm, pl.BlockSpec((1, H, Dv, Dh), lambda c: (c, 0, 0, 0))] + [_ANY] * n_x,
        out_shape=[jax.ShapeDtypeStruct((tn, H * Dh), F32), jax.ShapeDtypeStruct((nc, H, Dv, Dh), F32)]
        + _xchg_out_shapes(xs, False),
        scratch_shapes=[pltpu.VMEM((H, Dv, Dh), F32)] + _xchg_sems(n_x),
        compiler_params=_cparams(1))(r, lw, k, v, a, b, *xs)
    return res[0], res[1], res[2:]


def _cscan_bwd(r, lw, k, v, a, b, dy, ck, xs):
    n_x = len(xs)
    tn = r.shape[0]
    H, Dh, Dv = RWKV_HEADS, RWKV_HEAD_DIM, RWKV_HEAD_DIM
    nc = tn // SCAN_CHUNK
    lanes = lambda h: slice(h * Dh, (h + 1) * Dh)
    heads = lambda ref: tuple(ref[:, lanes(h)] for h in range(H))
    mats = lambda ref: tuple(ref[h] for h in range(H))

    def body(r_ref, lw_ref, k_ref, v_ref, a_ref, b_ref, dy_ref, ck_ref, *rest):
        x_src = rest[:n_x]
        d_refs = rest[n_x:n_x + 6]
        x_dst = rest[n_x + 6:2 * n_x + 6]
        g_ref = rest[2 * n_x + 6]
        sems = rest[2 * n_x + 7:]

        @pl.when(pl.program_id(0) == 0)
        def _():
            g_ref[...] = jnp.zeros(g_ref.shape, F32)
            _scatter_start(x_src, x_dst, sems)

        s0 = tuple(ck_ref[0, h] for h in range(H))
        _, vjp = jax.vjp(_chunk_fn, s0, heads(r_ref), heads(lw_ref), heads(k_ref), heads(v_ref), heads(a_ref),
                         heads(b_ref))
        grads = vjp((heads(dy_ref), mats(g_ref)))
        for h in range(H):
            g_ref[h] = grads[0][h]
            for d_ref, gz in zip(d_refs, grads[1:]):
                d_ref[:, lanes(h)] = gz[h]

        @pl.when(pl.program_id(0) == nc - 1)
        def _():
            _scatter_wait(x_src, x_dst, sems)

    hm = pl.BlockSpec((SCAN_CHUNK, H * Dh), lambda c: (nc - 1 - c, 0))
    hshape = jax.ShapeDtypeStruct((tn, H * Dh), F32)
    res = pl.pallas_call(
        body, name="rwkv_scan_bwd", grid=(nc,),
        in_specs=[hm] * 7 + [pl.BlockSpec((1, H, Dv, Dh), lambda c: (nc - 1 - c, 0, 0, 0))] + [_ANY] * n_x,
        out_specs=[hm] * 6 + [_ANY] * n_x, out_shape=[hshape] * 6 + _xchg_out_shapes(xs, True),
        scratch_shapes=[pltpu.VMEM((H, Dv, Dh), F32)] + _xchg_sems(n_x),
        compiler_params=_cparams(1))(r, lw, k, v, a, b, dy, ck, *xs)
    return res[:6], res[6:]


def _decay_mask(lg, i, j, blk):
    rows = lax.broadcasted_iota(jnp.int32, (blk, blk), 0)
    cols = lax.broadcasted_iota(jnp.int32, (blk, blk), 1)
    dd = (rows - cols + (i - j) * blk).astype(F32)
    return jnp.where(dd >= 0.0, jnp.exp(lg * jnp.maximum(dd, 0.0)), 0.0)


_NT = (((1,), (1,)), ((), ()))
_TN = (((0,), (0,)), ((), ()))


def _ret_attn_fwd(lg, q, k, v, v_col0=0, blk=ATT_BLOCK):
    tn = q.shape[0]
    Dh = RET_HEAD_DIM

    def body(lg_ref, q_ref, k_ref, v_ref, o_ref):
        i = pl.program_id(1)
        lgv = lg_ref[0][:, 0:1]
        qb = q_ref[...].astype(BF16)

        def jb(j, acc):
            ks = pl.ds(pl.multiple_of(j * blk, blk), blk)
            s = lax.dot_general(qb, k_ref[ks, :].astype(BF16), _NT, preferred_element_type=F32)
            s = s * _decay_mask(lgv, i, j, blk)
            return acc + jnp.dot(s.astype(BF16), v_ref[ks, :].astype(BF16), preferred_element_type=F32)

        o_ref[...] = lax.fori_loop(0, i + 1, jb, jnp.zeros((blk, Dh), F32))

    full = pl.BlockSpec((tn, Dh), lambda h, i: (0, h))
    qs = pl.BlockSpec((blk, Dh), lambda h, i: (i, h))
    return pl.pallas_call(
        body, name="ret_attn_fwd", grid=(RET_HEADS, tn // blk),
        in_specs=[pl.BlockSpec((1, 1, 128), lambda h, i: (h, 0, 0)), qs, full,
                  pl.BlockSpec((tn, Dh), lambda h, i: (0, v_col0 + h))],
        out_specs=qs, out_shape=jax.ShapeDtypeStruct(q.shape, F32), compiler_params=_cparams(2))(lg, q, k, v)


def _ret_attn_bwd(lg, q, k, v, do, v_col0=0, blk=ATT_BLOCK):
    tn = q.shape[0]
    nb = tn // blk
    Dh = RET_HEAD_DIM

    def body(lg_ref, q_ref, k_ref, v_ref, do_ref, dq_ref, dk_ref, dv_ref):
        lgv = lg_ref[0][:, 0:1]
        dk_ref[...] = jnp.zeros(dk_ref.shape, F32)
        dv_ref[...] = jnp.zeros(dv_ref.shape, F32)

        def ib(i, carry):
            qs = pl.ds(pl.multiple_of(i * blk, blk), blk)
            qb = q_ref[qs, :].astype(BF16)
            dob = do_ref[qs, :].astype(BF16)

            def jb(j, dq):
                ks = pl.ds(pl.multiple_of(j * blk, blk), blk)
                kb = k_ref[ks, :].astype(BF16)
                vb = v_ref[ks, :].astype(BF16)
                dm = _decay_mask(lgv, i, j, blk)
                s = lax.dot_general(qb, kb, _NT, preferred_element_type=F32) * dm
                ds = lax.dot_general(dob, vb, _NT, preferred_element_type=F32) * dm
                sb, dsb = s.astype(BF16), ds.astype(BF16)
                dv_ref[ks, :] += lax.dot_general(sb, dob, _TN, preferred_element_type=F32)
                dk_ref[ks, :] += lax.dot_general(dsb, qb, _TN, preferred_element_type=F32)
                return dq + jnp.dot(dsb, kb, preferred_element_type=F32)

            dq_ref[qs, :] = lax.fori_loop(0, i + 1, jb, jnp.zeros((blk, Dh), F32))
            return carry

        lax.fori_loop(0, nb, ib, 0)

    full = pl.BlockSpec((tn, Dh), lambda h: (0, h))
    sh = jax.ShapeDtypeStruct(q.shape, F32)
    return pl.pallas_call(
        body, name="ret_attn_bwd", grid=(RET_HEADS,),
        in_specs=[pl.BlockSpec((1, 1, 128), lambda h: (h, 0, 0)), full, full,
                  pl.BlockSpec((tn, Dh), lambda h: (0, v_col0 + h)), full],
        out_specs=[full, full, full], out_shape=[sh, sh, sh], compiler_params=_cparams(1))(lg, q, k, v, do)


def _next8_spec(a, tb):
    r = tb // 8
    last = a.shape[0] // 8 - 1
    return pl.BlockSpec((8, a.shape[1]), lambda i: (jnp.minimum((i + 1) * r, last), 0))


def _conv_taps(g_ext, cw_ref, cb_ref):
    return (cw_ref[2:3, :] * g_ext + cw_ref[1:2, :] * pltpu.roll(g_ext, 1, 0)
            + cw_ref[0:1, :] * pltpu.roll(g_ext, 2, 0) + cb_ref[...])


def _glu_fwd(gate, up, cw, cb, tb=TOK_BLOCK):
    tn = gate.shape[0]

    def body(g_ref, gh_ref, u_ref, cw_ref, cb_ref, o_ref):
        halo = jnp.where(pl.program_id(0) == 0, 0.0, gh_ref[...])
        g_ext = jnp.concatenate([halo, g_ref[...]], axis=0)
        gc = _conv_taps(g_ext, cw_ref, cb_ref)[8:, :]
        o_ref[...] = (gc * _sigmoid(gc) * u_ref[...]).astype(o_ref.dtype)

    return pl.pallas_call(
        body, name="glu_fwd", grid=(tn // tb,),
        in_specs=[_blk_spec(gate, tb), _prev8_spec(gate, tb), _blk_spec(up, tb), _full_spec(cw), _full_spec(cb)],
        out_specs=_blk_spec(gate, tb), out_shape=jax.ShapeDtypeStruct(gate.shape, BF16),
        compiler_params=_cparams(1))(gate, gate, up, cw, cb)


def _glu_bwd(gate, up, dact, cw, cb, tb=TOK_BLOCK):
    tn = gate.shape[0]
    nb = tn // tb

    def body(g_ref, gp_ref, gn_ref, u_ref, un_ref, d_ref, dn_ref, cw_ref, cb_ref, dg_ref, du_ref, dcw_ref, dcb_ref):
        i = pl.program_id(0)
        gprev = jnp.where(i == 0, 0.0, gp_ref[...])
        dnext = jnp.where(i == nb - 1, 0.0, dn_ref[...])
        g_ext = jnp.concatenate([gprev, g_ref[...], gn_ref[...]], axis=0)
        gc = _conv_taps(g_ext, cw_ref, cb_ref)[8:, :]
        u_e = jnp.concatenate([u_ref[...], un_ref[...]], axis=0)
        d_e = jnp.concatenate([d_ref[...], dnext], axis=0)
        s = _sigmoid(gc)
        dgc = d_e * u_e * (s * (1.0 + gc * (1.0 - s)))
        du_ref[...] = (d_ref[...] * (gc * s)[:tb, :]).astype(du_ref.dtype)
        n_e = tb + 8
        dg_ref[...] = (cw_ref[2:3, :] * dgc + cw_ref[1:2, :] * pltpu.roll(dgc, n_e - 1, 0)
                       + cw_ref[0:1, :] * pltpu.roll(dgc, n_e - 2, 0))[:tb, :].astype(dg_ref.dtype)

        @pl.when(i == 0)
        def _():
            dcw_ref[...] = jnp.zeros(dcw_ref.shape, F32)
            dcb_ref[...] = jnp.zeros(dcb_ref.shape, F32)

        dgc_b = dgc[:tb, :]
        g0 = g_ext[8:8 + tb, :]
        g1 = pltpu.roll(g_ext, 1, 0)[8:8 + tb, :]
        g2 = pltpu.roll(g_ext, 2, 0)[8:8 + tb, :]
        dcw_ref[2:3, :] += jnp.sum(dgc_b * g0, axis=0, keepdims=True)
        dcw_ref[1:2, :] += jnp.sum(dgc_b * g1, axis=0, keepdims=True)
        dcw_ref[0:1, :] += jnp.sum(dgc_b * g2, axis=0, keepdims=True)
        dcb_ref[...] += jnp.sum(dgc_b, axis=0, keepdims=True)

    sh = jax.ShapeDtypeStruct(gate.shape, BF16)
    return pl.pallas_call(
        body, name="glu_bwd", grid=(nb,),
        in_specs=[_blk_spec(gate, tb), _prev8_spec(gate, tb), _next8_spec(gate, tb), _blk_spec(up, tb),
                  _next8_spec(up, tb), _blk_spec(dact, tb), _next8_spec(dact, tb), _full_spec(cw), _full_spec(cb)],
        out_specs=[_blk_spec(gate, tb), _blk_spec(gate, tb), _full_spec(cw), _full_spec(cb)],
        out_shape=[sh, sh, jax.ShapeDtypeStruct(cw.shape, F32), jax.ShapeDtypeStruct(cb.shape, F32)],
        compiler_params=_cparams(1))(gate, gate, gate, up, up, dact, dact, cw, cb)


def _final_loss(x2, tgt, g, tb=LIGHT_TOK_BLOCK):
    tn, dm = x2.shape

    def body(x_ref, t_ref, g_ref, l_ref, dx_ref, dg_ref):
        y, vjp = jax.vjp(_rms_fn, x_ref[...], g_ref[...])
        err = y - t_ref[...]
        dx, dg = vjp(err * (1.0 / dm))

        @pl.when(pl.program_id(0) == 0)
        def _():
            l_ref[...] = jnp.zeros(l_ref.shape, F32)
            dg_ref[...] = jnp.zeros(dg_ref.shape, F32)

        part = 0.5 * jnp.sum(jnp.mean(err * err, axis=-1, keepdims=True), axis=0, keepdims=True)
        l_ref[...] += jnp.broadcast_to(part, l_ref.shape)
        dx_ref[...] = dx
        dg_ref[...] += dg

    return pl.pallas_call(
        body, name="final_loss", grid=(tn // tb,),
        in_specs=[_blk_spec(x2, tb), _blk_spec(tgt, tb), _full_spec(g)],
        out_specs=[pl.BlockSpec((8, 128), lambda i: (0, 0)), _blk_spec(x2, tb), _full_spec(g)],
        out_shape=[jax.ShapeDtypeStruct((8, 128), F32), jax.ShapeDtypeStruct(x2.shape, F32),
                   jax.ShapeDtypeStruct(g.shape, F32)],
        compiler_params=_cparams(1))(x2, tgt, g)


def _pad_cols(w, n):
    return jnp.pad(w, ((0, 0), (0, n - w.shape[1])))


def _pad_rows(w, n):
    return jnp.pad(w, ((0, n - w.shape[0]), (0, 0)))


def _local_step(x, tgt, W, late):
    tn = x.shape[0]
    Wd = RWKV_WIDTH
    row = lambda z: z.reshape(1, -1)
    g_mix, g_ffn, g_fin = row(W['norm_mix_g']), row(W['norm_ffn_g']), row(W['norm_final_g'])

    (h1,) = _tok_fwd("norm_mix_fwd", lambda a, g: (_rms_fn(a, g),), [x], [g_mix], [(D_MODEL,)])
    proj = _mm("proj_fwd", h1, W['w_in_t'], tb=True)
    p_rkv = _Cols(proj, 3 * Wd, 0)
    pre_consts = [row(W['rwkv_mu_w']), row(W['rwkv_mu_a']), row(W['rwkv_mu_g']), row(W['rwkv_mu_r']),
                  row(W['rwkv_mu_k']), row(W['rwkv_mu_v']), row(W['rwkv_w0']),
                  _pad_cols(W['rwkv_w1'], LORA_PAD), _pad_rows(W['rwkv_w2'], LORA_PAD), row(W['rwkv_a0']),
                  _pad_cols(W['rwkv_a1'], LORA_PAD), _pad_rows(W['rwkv_a2'], LORA_PAD),
                  W['rwkv_g1'], W['rwkv_g2'], row(W['rwkv_k_k']), row(W['rwkv_k_a'])]
    r, k, v, lw, nkk, b, g = _pre_a_fwd(h1, p_rkv, pre_consts)
    y_scan, ck, gathered = _cscan_fwd(r, lw, k, v, nkk, b, late)
    w_out, w_gate_t, w_up_t, w_down = [g_.reshape(-1, D_MODEL) for g_ in gathered]
    post_consts = [row(W['rwkv_lnx_w']), row(W['rwkv_lnx_b']), row(W['rwkv_r_k'])]
    (y_rwkv,) = _tok_fwd("rwkv_post_fwd", _rwkv_post_fn, [y_scan, r, k, v, g], post_consts, [(Wd,)],
                         out_dtypes=[BF16])

    pos = jnp.arange(tn, dtype=F32)
    half = RET_HEAD_DIM // 2
    inv_freq = ROPE_BASE ** (-jnp.arange(half, dtype=F32) / half)
    ang = pos[:, None] * inv_freq[None, :]
    cos2 = jnp.concatenate([jnp.cos(ang), jnp.cos(ang)], axis=1)
    sin2 = jnp.concatenate([-jnp.sin(ang), jnp.sin(ang)], axis=1)
    lg = jnp.log(1.0 - 2.0 ** (-5.0 - jnp.arange(RET_HEADS, dtype=F32)))
    lg = jnp.broadcast_to(lg[:, None, None], (RET_HEADS, 1, 128))
    q_p, k_p, g_ret = _Cols(proj, Wd, 3), _Cols(proj, Wd, 4), _Cols(proj, Wd, 6)
    v_col0 = 5 * Wd // RET_HEAD_DIM
    q_rot, k_rot = _tok_fwd("ret_rotary_fwd", _rotary_fn, [cos2, sin2, q_p, k_p], [], [(RET_WIDTH,)] * 2)
    y_ret_raw = _ret_attn_fwd(lg, q_rot, k_rot, proj, v_col0)
    gn_w = row(W['ret_gn_w'])
    (y_ret,) = _tok_fwd("ret_post_fwd", _ret_post_fn, [y_ret_raw, g_ret], [gn_w], [(RET_WIDTH,)],
                        out_dtypes=[BF16])

    ycat = jnp.concatenate([y_rwkv, y_ret], axis=1)
    x1 = _mm("out_proj_fwd", ycat, w_out, add=x)
    (h2,) = _tok_fwd("norm_ffn_fwd", lambda a_, g_: (_rms_fn(a_, g_),), [x1], [g_ffn], [(D_MODEL,)],
                     out_dtypes=[BF16])
    gate = _mm("ffn_gate_fwd", h2, w_gate_t, tb=True)
    up = _mm("ffn_up_fwd", h2, w_up_t, tb=True)
    cw = W['ffn_conv_w']
    cb = row(W['ffn_conv_b'])
    act = _glu_fwd(gate, up, cw, cb)
    x2 = _mm("ffn_down_fwd", act, w_down, add=x1)
    loss8, dx2, dg_fin = _final_loss(x2, tgt, g_fin)

    G = {'norm_final_g': dg_fin}
    dact = _mm("ffn_down_dx", dx2, w_down, tb=True)
    d_down = _mm("ffn_down_dw", act, dx2, ta=True, out_dtype=BF16)
    dgate, dup, dcw, dcb = _glu_bwd(gate, up, dact, cw, cb)
    G['ffn_conv_w'], G['ffn_conv_b'] = dcw, dcb
    dh2 = _mm("ffn_gate_dx", dgate, w_gate_t)
    dh2 = _mm("ffn_up_dx", dup, w_up_t, add=dh2)
    d_gate_t = _mm("ffn_gate_dw", dgate, h2, ta=True, out_dtype=BF16)
    d_up_t = _mm("ffn_up_dw", dup, h2, ta=True, out_dtype=BF16)
    dx1, G['norm_ffn_g'] = _tok_bwd("norm_ffn_bwd", lambda a_, g_: (_rms_fn(a_, g_),), [], [x1], [g_ffn], [dh2], add=dx2)
    dycat = _mm("out_proj_dx", dx1, w_out, tb=True)
    d_out = _mm("out_proj_dw", ycat, dx1, ta=True, out_dtype=BF16)
    late_grads = [z.reshape(N_DEV, -1, D_MODEL) for z in (d_out, d_gate_t, d_up_t, d_down)]
    dy_rwkv, dy_ret = _Cols(dycat, Wd, 0), _Cols(dycat, Wd, 1)

    dyr_raw, dg_ret, G['ret_gn_w'] = _tok_bwd("ret_post_bwd", _ret_post_fn, [], [y_ret_raw, g_ret], [gn_w], [dy_ret],
                                              tok_dtypes=[F32, BF16])
    dq_rot, dk_rot, dv_ret = _ret_attn_bwd(lg, q_rot, k_rot, proj, dyr_raw, v_col0)
    dq_p, dk_p = _tok_bwd("ret_rotary_bwd", _rotary_fn, [cos2, sin2], [q_p, k_p], [], [dq_rot, dk_rot],
                          tok_dtypes=[BF16, BF16])

    dy_scan, dr1, dk1, dv1, dg, G['rwkv_lnx_w'], G['rwkv_lnx_b'], G['rwkv_r_k'] = _tok_bwd(
        "rwkv_post_bwd", _rwkv_post_fn, [], [y_scan, r, k, v, g], post_consts, [dy_rwkv])
    (dr2, dlw, dk2, dv2, dnkk, db), late_parts = _cscan_bwd(r, lw, k, v, nkk, b, dy_scan, ck, late_grads)
    pre_cts = [(dr1, dr2), (dk1, dk2), (dv1, dv2), dlw, dnkk, db, dg]
    pre_out = _pre_a_bwd(h1, p_rkv, pre_consts, pre_cts)
    dh1_a, dp_rkv = pre_out[0], pre_out[1]
    (G['rwkv_mu_w'], G['rwkv_mu_a'], G['rwkv_mu_g'], G['rwkv_mu_r'], G['rwkv_mu_k'], G['rwkv_mu_v'], G['rwkv_w0'],
     dw1, dw2, G['rwkv_a0'], da1, da2, G['rwkv_g1'], G['rwkv_g2'], G['rwkv_k_k'], G['rwkv_k_a']) = pre_out[2:]
    G['rwkv_w1'], G['rwkv_w2'] = dw1[:, :64], dw2[:64, :]
    G['rwkv_a1'], G['rwkv_a2'] = da1[:, :64], da2[:64, :]

    dproj = jnp.concatenate([dp_rkv, dq_p, dk_p, dv_ret.astype(BF16), dg_ret], axis=1)
    d_in_t = _mm("proj_dw", dproj, h1, ta=True, out_dtype=BF16)
    dh1, (w_in_parts,) = _mm("proj_dx", dproj, W['w_in_t'], add=dh1_a, xs=[d_in_t.reshape(N_DEV, -1, D_MODEL)])
    dx, G['norm_mix_g'] = _tok_bwd("norm_mix_bwd", lambda a_, g_: (_rms_fn(a_, g_),), [], [x], [g_mix], [dh1], add=dx1)
    return loss8[0, 0], dx, G, late_parts, w_in_parts


def _adamw_block(p_ref, w_ref, m_ref, v_ref, g_ref, d_ref, nm_ref, nv_ref):
    g = p_ref[0].astype(F32)
    for d in range(1, N_DEV):
        g = g + p_ref[d].astype(F32)
    mn = ADAM_B1 * m_ref[...] + (1.0 - ADAM_B1) * g
    vn = ADAM_B2 * v_ref[...] + (1.0 - ADAM_B2) * (g * g)
    m_hat = mn / (1.0 - ADAM_B1 ** ADAM_STEP)
    v_hat = vn / (1.0 - ADAM_B2 ** ADAM_STEP)
    g_ref[...] = g
    d_ref[...] = -ADAM_LR * (m_hat / (jnp.sqrt(v_hat) + ADAM_EPS) + ADAM_WD * w_ref[...])
    nm_ref[...] = mn
    nv_ref[...] = vn


def _adamw_late(items, xs, tb=32):
    n_it, n_x = len(items), len(xs)
    nbs = [it[1].shape[0] // tb for it in items]
    steps = max(nbs)
    cols = items[0][1].shape[1]

    def body(*refs):
        ins, x_src = refs[:4 * n_it], refs[4 * n_it:4 * n_it + n_x]
        outs = refs[4 * n_it + n_x:8 * n_it + n_x]
        x_dst, sems = refs[8 * n_it + n_x:8 * n_it + 2 * n_x], refs[8 * n_it + 2 * n_x:]
        i = pl.program_id(0)

        @pl.when(i == 0)
        def _():
            _scatter_start(x_src, x_dst, sems)

        for j in range(n_it):
            @pl.when(i < nbs[j])
            def _(j=j):
                _adamw_block(*ins[4 * j:4 * j + 4], *outs[4 * j:4 * j + 4])

        @pl.when(i == steps - 1)
        def _():
            _scatter_wait(x_src, x_dst, sems)

    in_specs, out_specs, out_shape, flat = [], [], [], []
    for (parts, w, m, v), nb in zip(items, nbs):
        blk = pl.BlockSpec((tb, cols), lambda i, nb=nb: (jnp.minimum(i, nb - 1), 0))
        in_specs += [pl.BlockSpec((N_DEV, tb, cols), lambda i, nb=nb: (0, jnp.minimum(i, nb - 1), 0)), blk, blk, blk]
        out_specs += [blk] * 4
        out_shape += [jax.ShapeDtypeStruct(w.shape, F32)] * 4
        flat += [parts, w, m, v]
    res = pl.pallas_call(
        body, name="adamw_late", grid=(steps,), in_specs=in_specs + [_ANY] * n_x, out_specs=out_specs + [_ANY] * n_x,
        out_shape=out_shape + _xchg_out_shapes(xs, True), scratch_shapes=_xchg_sems(n_x),
        compiler_params=_cparams(1))(*flat, *xs)
    return [res[4 * j:4 * j + 4] for j in range(n_it)], res[4 * n_it:]


def _adamw(name, parts, w, m, v):
    rows, cols = w.shape
    sub = 8 * 4 // parts.dtype.itemsize
    tb = max(t for t in range(sub, 65, sub) if rows % t == 0) if rows > 64 else rows
    body = functools.partial(_adamw_block)
    spec = pl.BlockSpec((tb, cols), lambda i: (i, 0))
    sh = jax.ShapeDtypeStruct((rows, cols), F32)
    return pl.pallas_call(
        body, name=name, grid=(rows // tb,),
        in_specs=[pl.BlockSpec((N_DEV, tb, cols), lambda i: (0, i, 0)), spec, spec, spec],
        out_specs=[spec] * 4, out_shape=[sh] * 4, compiler_params=_cparams(1))(parts, w, m, v)


def _local_shape(name):
    gs, ax = SHARDED[name]
    ls = list(gs)
    ls[ax] //= N_DEV
    return tuple(ls)


def _seg(flat, seg):
    n = flat.shape[-1]
    pad = _round_up(n, seg) - n
    if pad:
        flat = jnp.pad(flat, [(0, 0)] * (flat.ndim - 1) + [(0, pad)])
    return flat


def _split3(w):
    hi = w.astype(BF16)
    r1 = w - hi.astype(F32)
    mid = r1.astype(BF16)
    lo = (r1 - mid.astype(F32)).astype(BF16)
    return hi, mid, lo


def _pack_small_shards(shards):
    pieces = []
    for name in SMALL_NAMES:
        flat = shards[name].reshape(-1)
        if name == 'ffn_conv_w':
            pieces += [_seg(p, BF16_SEG) for p in _split3(flat)]
        else:
            pieces.append(flat.astype(BF16))
    return jnp.concatenate(pieces).reshape(-1, 128)


def _unpack_small(gathered):
    flat = gathered.reshape(N_DEV, -1)
    out, off = {}, 0
    for name in SMALL_NAMES:
        gs, ax = SHARDED[name]
        ls = _local_shape(name)
        n = int(np.prod(ls))
        if name == 'ffn_conv_w':
            nseg = _round_up(n, BF16_SEG)
            hi, mid, lo = (flat[:, off + j * nseg: off + j * nseg + n].astype(F32) for j in range(3))
            sh = ((hi + mid) + lo).reshape(N_DEV, 3, -1)
            out[name] = jnp.swapaxes(sh, 0, 1).reshape(3, D_FF)
            off += 3 * nseg
        else:
            sh = flat[:, off:off + n].reshape((N_DEV,) + ls[1:])
            out[name] = sh.reshape(gs[1:]) if ax == 1 else jnp.swapaxes(sh, 0, 1).reshape(gs[1:])
            off += n
    return out


def _small_pieces(sharded, repl):
    return [sharded[n].reshape(-1) for n in SMALL_NAMES] + [repl[n].reshape(-1) for n in REPL_NAMES]


def _pack_small_local(d):
    flat = jnp.concatenate(_small_pieces(d, d))
    return _seg(flat, F32_SEG).reshape(-1, 128)


def _pack_small_grads(G, loss):
    pieces = []
    for name in SMALL_NAMES:
        gs, ax = SHARDED[name]
        g = G[name]
        if name == 'ffn_conv_w':
            sh = jnp.swapaxes(g.reshape(3, N_DEV, -1), 0, 1)
        elif ax == 1:
            sh = g
        else:
            sh = jnp.swapaxes(g.reshape(g.shape[0], N_DEV, -1), 0, 1)
        pieces.append(sh.reshape(N_DEV, -1))
    rep = jnp.concatenate([G[n].reshape(-1) for n in REPL_NAMES] + [loss.reshape(1)])
    pieces.append(jnp.broadcast_to(rep[None, :], (N_DEV, rep.shape[0])))
    flat = _seg(jnp.concatenate(pieces, axis=1), F32_SEG)
    return flat.reshape(N_DEV, -1, 128)


def _unpack_small_local(packed, local_shapes):
    flat = packed.reshape(-1)
    out, off = {}, 0
    for name in SMALL_NAMES + REPL_NAMES:
        n = int(np.prod(local_shapes[name]))
        out[name] = flat[off:off + n].reshape(local_shapes[name])
        off += n
    return out


def kernel(x, *rest):
    nw = len(WEIGHT_NAMES)
    assert len(rest) == 3 * nw + 1
    weights = dict(zip(WEIGHT_NAMES, rest[:nw]))
    loss_target = rest[nw]
    moms = dict(zip(WEIGHT_NAMES, rest[nw + 1:2 * nw + 1]))
    vars_ = dict(zip(WEIGHT_NAMES, rest[2 * nw + 1:]))
    local_shapes = {n: weights[n].shape for n in WEIGHT_NAMES}

    def native2d(name, a):
        a2 = a.reshape(a.shape[-2], a.shape[-1])
        return a2.T if name in BIG_T else a2

    def from2d(name, a2):
        return (a2.T if name in BIG_T else a2).reshape(local_shapes[name])

    big_w = {n: native2d(n, weights[n]) for n in BIG_NAMES}
    w_in_t_sh = big_w['w_in'].astype(BF16)
    late = [big_w[n].astype(BF16) for n in LATE_NAMES]
    small_sh = _pack_small_shards({n: weights[n] for n in SMALL_NAMES})
    w_in_g, small_g = _exchange("weights_all_gather", [w_in_t_sh, small_sh], False)
    W = _unpack_small(small_g)
    W['w_in_t'] = w_in_g.reshape(-1, D_MODEL)
    for n in REPL_NAMES:
        W[n] = weights[n][0] if n != 'norm_final_g' else weights[n]

    loss, dx, G, late_parts, w_in_parts = _local_step(x[0], loss_target[0], W, late)

    late_items = [(parts, big_w[n], native2d(n, moms[n]), native2d(n, vars_[n]))
                  for n, parts in zip(LATE_NAMES, late_parts)]
    late_res, (small_parts,) = _adamw_late(late_items, [_pack_small_grads(G, loss)])
    results = {n: [from2d(n, r) for r in res] for n, res in zip(LATE_NAMES, late_res)}
    res = _adamw("adamw_w_in", w_in_parts, big_w['w_in'], native2d('w_in', moms['w_in']), native2d('w_in', vars_['w_in']))
    results['w_in'] = [from2d('w_in', r) for r in res]
    small_res = _adamw("adamw_small", small_parts, _pack_small_local(weights), _pack_small_local(moms),
                       _pack_small_local(vars_))
    small_out = [_unpack_small_local(p, local_shapes) for p in small_res]

    n_small = sum(int(np.prod(local_shapes[n])) for n in SMALL_NAMES + REPL_NAMES)
    loss = small_res[0].reshape(-1)[n_small]
    outs = [loss, dx[None]]
    for j in range(4):
        outs += [results[n][j] if n in results else small_out[j][n] for n in WEIGHT_NAMES]
    return tuple(outs)
```

```python
import functools
import math

import numpy as np
import jax
import jax.numpy as jnp
from jax import lax
from jax.experimental import pallas as pl
from jax.experimental.pallas import tpu as pltpu

F32 = jnp.float32
BF16 = jnp.bfloat16

N_DEV = 8
D_MODEL = 1024
RWKV_HEADS = 8
RWKV_HEAD_DIM = 64
RWKV_WIDTH = 512
RET_HEADS = 4
RET_HEAD_DIM = 128
RET_WIDTH = 512
LORA_PAD = 128
D_FF = 2816
NORM_EPS = 1e-6
RWKV_GN_EPS = 64e-5
RET_GN_EPS = 1e-5
ROPE_BASE = 10000.0
ADAM_LR, ADAM_B1, ADAM_B2, ADAM_EPS, ADAM_WD, ADAM_STEP = 0.001, 0.9, 0.999, 1e-08, 0.01, 10

VMEM_LIMIT = 56 * 1024 * 1024
TOK_BLOCK = 256
LIGHT_TOK_BLOCK = 512
SCAN_CHUNK = 64
ATT_BLOCK = 512
BF16_SEG = 2048
F32_SEG = 1024

WEIGHT_NAMES = ['norm_mix_g', 'w_in', 'rwkv_mu_r', 'rwkv_mu_k', 'rwkv_mu_v', 'rwkv_mu_w', 'rwkv_mu_a',
                'rwkv_mu_g', 'rwkv_w0', 'rwkv_w1', 'rwkv_w2', 'rwkv_a0', 'rwkv_a1', 'rwkv_a2', 'rwkv_g1',
                'rwkv_g2', 'rwkv_k_k', 'rwkv_k_a', 'rwkv_r_k', 'rwkv_lnx_w', 'rwkv_lnx_b', 'ret_gn_w',
                'w_out', 'norm_ffn_g', 'ffn_w_gate', 'ffn_w_up', 'ffn_conv_w', 'ffn_conv_b', 'ffn_w_down',
                'norm_final_g']
SHARDED = {
    'w_in': ((1, 1024, 3584), 2), 'rwkv_w1': ((1, 1024, 64), 1), 'rwkv_w2': ((1, 64, 512), 2),
    'rwkv_a1': ((1, 1024, 64), 1), 'rwkv_a2': ((1, 64, 512), 2), 'rwkv_g1': ((1, 1024, 128), 1),
    'rwkv_g2': ((1, 128, 512), 2), 'w_out': ((1, 1024, 1024), 1), 'ffn_w_gate': ((1, 1024, 2816), 2),
    'ffn_w_up': ((1, 1024, 2816), 2), 'ffn_conv_w': ((1, 3, 1, 2816), 3), 'ffn_w_down': ((1, 2816, 1024), 1),
}
REPL_NAMES = [n for n in WEIGHT_NAMES if n not in SHARDED]
BIG_NAMES = ['w_in', 'w_out', 'ffn_w_gate', 'ffn_w_up', 'ffn_w_down']
BIG_T = ('w_in', 'ffn_w_gate', 'ffn_w_up')
LATE_NAMES = ['w_out', 'ffn_w_gate', 'ffn_w_up', 'ffn_w_down']
SMALL_NAMES = [n for n in WEIGHT_NAMES if n in SHARDED and n not in BIG_NAMES]


def _cparams(n_grid):
    return pltpu.CompilerParams(dimension_semantics=("arbitrary",) * n_grid, vmem_limit_bytes=VMEM_LIMIT)


def _round_up(n, m):
    return (n + m - 1) // m * m


@jax.custom_vjp
def _bdot(x, w):
    return jnp.dot(x.astype(BF16), w.astype(BF16), preferred_element_type=F32)


def _bdot_fwd(x, w):
    return _bdot(x, w), (x, w)


def _bdot_bwd(res, g):
    x, w = res
    gb = g.astype(BF16)
    dx = lax.dot_general(gb, w.astype(BF16), (((1,), (1,)), ((), ())), preferred_element_type=F32)
    dw = lax.dot_general(x.astype(BF16), gb, (((0,), (0,)), ((), ())), preferred_element_type=F32)
    return dx, dw.astype(w.dtype)


_bdot.defvjp(_bdot_fwd, _bdot_bwd)


@jax.custom_vjp
def _shift_rows(x, prev):
    rolled = pltpu.roll(x, 1, 0)
    row = lax.broadcasted_iota(jnp.int32, x.shape, 0)
    return jnp.where(row == 0, jnp.broadcast_to(prev, x.shape), rolled)


def _shift_rows_fwd(x, prev):
    return _shift_rows(x, prev), None


def _shift_rows_bwd(_, g):
    n = g.shape[0]
    rolled = pltpu.roll(g, n - 1, 0)
    row = lax.broadcasted_iota(jnp.int32, g.shape, 0)
    return jnp.where(row == n - 1, 0.0, rolled), g[0:1, :]


_shift_rows.defvjp(_shift_rows_fwd, _shift_rows_bwd)


@jax.custom_vjp
def _swap_halves(x):
    return pltpu.roll(x, 64, 1)


_swap_halves.defvjp(lambda x: (_swap_halves(x), None), lambda _, g: (pltpu.roll(g, 64, 1),))


def _sigmoid(x):
    return 1.0 / (1.0 + jnp.exp(-x))


def _softplus(x):
    return jnp.maximum(x, 0.0) + jnp.log(1.0 + jnp.exp(-jnp.abs(x)))


def _rms_fn(x, g):
    return x * lax.rsqrt(jnp.mean(x * x, axis=-1, keepdims=True) + NORM_EPS) * g


def _pre_a_fn(h1, h1p, p, pp, mu_w, mu_a, mu_g, mu_r, mu_k, mu_v, w0, w1, w2, a0, a1, a2, g1, g2, k_k, k_a):
    W = RWKV_WIDTH
    h1s = _shift_rows(h1, h1p)
    ps = _shift_rows(p, pp)
    dx = h1s - h1
    xw = h1 + dx * mu_w
    xa = h1 + dx * mu_a
    xg = h1 + dx * mu_g
    dp = ps - p
    r = p[:, 0:W] + dp[:, 0:W] * mu_r
    k0 = p[:, W:2 * W] + dp[:, W:2 * W] * mu_k
    v = p[:, 2 * W:3 * W] + dp[:, 2 * W:3 * W] * mu_v
    wl = w0 + _bdot(jnp.tanh(_bdot(xw, w1)), w2)
    w_log = -_softplus(-wl) - 0.5
    lw = -jnp.exp(w_log)
    a = _sigmoid(a0 + _bdot(_bdot(xa, a1), a2))
    g = _bdot(_sigmoid(_bdot(xg, g1)), g2)
    nkk, k, b = _pre_b_fn(k0, a, k_k, k_a)
    return r, k, v, lw, nkk, b, g


def _head_sum_raw(x):
    n = x.shape[1]
    ii = lax.broadcasted_iota(jnp.int32, (n, n), 0) // RWKV_HEAD_DIM
    jj = lax.broadcasted_iota(jnp.int32, (n, n), 1) // RWKV_HEAD_DIM
    ones = (ii == jj).astype(BF16)
    xh = x.astype(BF16)
    xl = (x - xh.astype(F32)).astype(BF16)
    return jnp.dot(xh, ones, preferred_element_type=F32) + jnp.dot(xl, ones, preferred_element_type=F32)


@jax.custom_vjp
def _head_sum(x):
    return _head_sum_raw(x)


_head_sum.defvjp(lambda x: (_head_sum_raw(x), None), lambda _, g: (_head_sum_raw(g),))


def _pre_b_fn(k0, a, k_k, k_a):
    kkr = k0 * k_k
    nrm = jnp.sqrt(_head_sum(kkr * kkr))
    kk = kkr / jnp.maximum(nrm, 1e-12)
    k = k0 * (1.0 + (a - 1.0) * k_a)
    return -kk, k, kk * a


def _rwkv_post_fn(y, r, k, v, g, lnx_w, lnx_b, r_k):
    inv = 1.0 / RWKV_HEAD_DIM
    mu = _head_sum(y) * inv
    yc = y - mu
    var = _head_sum(yc * yc) * inv
    yn = yc * lax.rsqrt(var + RWKV_GN_EPS) * lnx_w + lnx_b
    bonus = _head_sum(r * k * r_k) * v
    return ((yn + bonus) * g,)


def _rotary_fn(cos2, sin2, q, k):
    qs, ks = [], []
    for h in range(RET_HEADS):
        sl = slice(h * RET_HEAD_DIM, (h + 1) * RET_HEAD_DIM)
        qh, kh = q[:, sl], k[:, sl]
        qs.append(qh * cos2 + _swap_halves(qh) * sin2)
        ks.append((kh * cos2 + _swap_halves(kh) * sin2) * (RET_HEAD_DIM ** -0.5))
    return jnp.concatenate(qs, axis=1), jnp.concatenate(ks, axis=1)


def _ret_post_fn(y, gp, gn_w):
    outs = []
    for h in range(RET_HEADS):
        sl = slice(h * RET_HEAD_DIM, (h + 1) * RET_HEAD_DIM)
        yh = y[:, sl]
        mu = jnp.mean(yh, axis=-1, keepdims=True)
        yc = yh - mu
        var = jnp.mean(yc * yc, axis=-1, keepdims=True)
        outs.append(yc * lax.rsqrt(var + RET_GN_EPS) * gn_w[:, sl])
    yn = jnp.concatenate(outs, axis=1)
    return (gp * _sigmoid(gp) * yn,)


class _Cols:
    def __init__(self, array, width, block):
        self.array, self.width, self.block = array, width, block
        self.shape, self.ndim, self.dtype = (array.shape[0], width), 2, array.dtype


def _arr(a):
    return a.array if isinstance(a, _Cols) else a


def _blk_spec(a, tb, rev_nb=None):
    nd = a.ndim
    tail = (a.block,) if isinstance(a, _Cols) else (0,) * (nd - 1)
    if rev_nb is None:
        return pl.BlockSpec((tb,) + a.shape[1:], lambda i: (i,) + tail)
    return pl.BlockSpec((tb,) + a.shape[1:], lambda i: (rev_nb - 1 - i,) + tail)


def _full_spec(a):
    nd = a.ndim
    return pl.BlockSpec(a.shape, lambda i: (0,) * nd)


def _tok_fwd(name, fn, toks, consts, out_tails, tb=LIGHT_TOK_BLOCK, out_dtypes=None):
    out_dtypes = out_dtypes or [F32] * len(out_tails)
    n_in = len(toks) + len(consts)
    tn = toks[0].shape[0]

    def body(*refs):
        outs = fn(*[r[...] for r in refs[:n_in]])
        for r, o in zip(refs[n_in:], outs):
            r[...] = o.astype(r.dtype)

    out_shape = [jax.ShapeDtypeStruct((tn,) + tuple(s), dt) for s, dt in zip(out_tails, out_dtypes)]
    return pl.pallas_call(
        body, name=name, grid=(tn // tb,),
        in_specs=[_blk_spec(a, tb) for a in toks] + [_full_spec(c) for c in consts],
        out_specs=[_blk_spec(o, tb) for o in out_shape], out_shape=out_shape,
        compiler_params=_cparams(1))(*[_arr(a) for a in toks], *consts)


def _tok_bwd(name, fn, aux, toks, consts, cts, add=None, tb=LIGHT_TOK_BLOCK, tok_dtypes=None):
    n_aux, n_tok, n_c = len(aux), len(toks), len(consts)
    ct_groups = [c if isinstance(c, (tuple, list)) else (c,) for c in cts]
    ct_flat = [a for grp in ct_groups for a in grp]
    n_ct = len(ct_flat)
    n_add = 0 if add is None else 1
    tn = toks[0].shape[0]

    def body(*refs):
        pos = 0
        aux_v = [r[...] for r in refs[pos:pos + n_aux]]; pos += n_aux
        tok_v = [r[...] for r in refs[pos:pos + n_tok]]; pos += n_tok
        const_v = [r[...] for r in refs[pos:pos + n_c]]; pos += n_c
        ct_refs = refs[pos:pos + n_ct]; pos += n_ct
        add_refs = refs[pos:pos + n_add]; pos += n_add
        dtok_refs = refs[pos:pos + n_tok]; pos += n_tok
        dconst_refs = refs[pos:pos + n_c]
        ct_v, q = [], 0
        for grp in ct_groups:
            s = ct_refs[q][...]
            for r in ct_refs[q + 1:q + len(grp)]:
                s = s + r[...]
            q += len(grp)
            ct_v.append(s)
        _, vjp = jax.vjp(lambda *tc: fn(*aux_v, *tc), *tok_v, *const_v)
        grads = vjp(tuple(ct_v))
        for j, r in enumerate(dtok_refs):
            gj = grads[j]
            if j == 0 and n_add:
                gj = gj + add_refs[0][...]
            r[...] = gj.astype(r.dtype)

        @pl.when(pl.program_id(0) == 0)
        def _():
            for r in dconst_refs:
                r[...] = jnp.zeros(r.shape, F32)

        for j, r in enumerate(dconst_refs):
            r[...] += grads[n_tok + j]

    ins = list(aux) + list(toks) + list(consts) + ct_flat + ([add] if n_add else [])
    in_specs = ([_blk_spec(a, tb) for a in aux] + [_blk_spec(a, tb) for a in toks] + [_full_spec(c) for c in consts]
                + [_blk_spec(a, tb) for a in ct_flat] + ([_blk_spec(add, tb)] if n_add else []))
    tok_dtypes = tok_dtypes or [F32] * n_tok
    out_shape = ([jax.ShapeDtypeStruct(a.shape, dt) for a, dt in zip(toks, tok_dtypes)]
                 + [jax.ShapeDtypeStruct(c.shape, F32) for c in consts])
    out_specs = [_blk_spec(o, tb) for o in out_shape[:n_tok]] + [_full_spec(c) for c in consts]
    return pl.pallas_call(body, name=name, grid=(tn // tb,), in_specs=in_specs, out_specs=out_specs,
                          out_shape=out_shape, compiler_params=_cparams(1))(*[_arr(a) for a in ins])


MM_VMEM_BUDGET = 40 * 1024 * 1024
MM_STEP_SECONDS = 0.4e-6
MM_HBM_BYTES_PER_SECOND = 2.5e12
MM_XPOSE_SECONDS_PER_ELEM = 2e-12
MM_MXU_COLUMNS = 256
MM_MXU_FLOPS = 9e14


def _mm_tiles(m, n, kd, a_bytes, b_bytes, o_bytes, has_add, ta):
    divs = lambda d: [t for t in range(128, d + 1, 128) if d % t == 0]
    best = None
    for tm in divs(m):
        for tn in divs(n):
            for tk in divs(kd):
                ni, nj, nk = m // tm, n // tn, kd // tk
                vmem = (2 * tm * tk * a_bytes + 2 * tk * tn * b_bytes + tm * tn * 4 + 2 * tm * tn * o_bytes
                        + (2 * tm * tn * 4 if has_add else 0) + 2 * (tm * tk + tk * tn) + tm * tn * 4)
                if vmem > MM_VMEM_BUDGET:
                    continue
                a_traffic = m * kd * a_bytes * (nj if nk > 1 else 1)
                b_traffic = kd * n * b_bytes * (ni if nj * nk > 1 else 1)
                cost = ni * nj * nk * MM_STEP_SECONDS + (a_traffic + b_traffic) / MM_HBM_BYTES_PER_SECOND
                cost += 2.0 * m * kd * nj * max(tn, MM_MXU_COLUMNS) / MM_MXU_FLOPS
                if ta:
                    cost += m * kd * nj * MM_XPOSE_SECONDS_PER_ELEM
                if best is None or cost < best[0]:
                    best = (cost, tm, tn, tk)
    return best[1:]


def _mm(name, a, b, ta=False, tb=False, add=None, out_dtype=F32, xs=()):
    n_x = len(xs)
    if ta:
        kd, m = a.shape
    else:
        m, kd = a.shape
    if tb:
        n, kb = b.shape
    else:
        kb, n = b.shape
    assert kd == kb, (a.shape, b.shape)
    tm, tn, tk = _mm_tiles(m, n, kd, a.dtype.itemsize, b.dtype.itemsize, jnp.dtype(out_dtype).itemsize,
                           add is not None, ta)
    nk = kd // tk
    has_add = add is not None
    dims = (((0 if ta else 1,), (1 if tb else 0,)), ((), ()))

    n_in = 3 if add is not None else 2
    nj = n // tn
    n_steps = (m // tm) * nj * nk

    def body(*refs):
        a_ref, b_ref = refs[0], refs[1]
        x_src, o_ref = refs[n_in:n_in + n_x], refs[n_in + n_x]
        x_dst, acc_ref, sems = refs[n_in + n_x + 1:n_in + 2 * n_x + 1], refs[n_in + 2 * n_x + 1], refs[n_in + 2 * n_x + 2:]
        k = pl.program_id(2)
        step = (pl.program_id(0) * nj + pl.program_id(1)) * nk + k

        if n_x:
            @pl.when(step == 0)
            def _():
                _scatter_start(x_src, x_dst, sems)

        @pl.when(k == 0)
        def _():
            acc_ref[...] = refs[2][...] if has_add else jnp.zeros(acc_ref.shape, F32)

        acc_ref[...] += lax.dot_general(a_ref[...].astype(BF16), b_ref[...].astype(BF16), dims,
                                        preferred_element_type=F32)

        @pl.when(k == nk - 1)
        def _():
            o_ref[...] = acc_ref[...].astype(out_dtype)

        if n_x:
            @pl.when(step == n_steps - 1)
            def _():
                _scatter_wait(x_src, x_dst, sems)

    a_spec = pl.BlockSpec((tk, tm), lambda i, j, k: (k, i)) if ta else pl.BlockSpec((tm, tk), lambda i, j, k: (i, k))
    b_spec = pl.BlockSpec((tn, tk), lambda i, j, k: (j, k)) if tb else pl.BlockSpec((tk, tn), lambda i, j, k: (k, j))
    o_spec = pl.BlockSpec((tm, tn), lambda i, j, k: (i, j))
    ins = [a, b] + ([add] if has_add else [])
    in_specs = [a_spec, b_spec] + ([o_spec] if has_add else [])
    res = pl.pallas_call(body, name=name, grid=(m // tm, n // tn, nk), in_specs=in_specs + [_ANY] * n_x,
                         out_specs=[o_spec] + [_ANY] * n_x,
                         out_shape=[jax.ShapeDtypeStruct((m, n), out_dtype)] + _xchg_out_shapes(xs, True),
                         scratch_shapes=[pltpu.VMEM((tm, tn), F32)] + (_xchg_sems(n_x) if n_x else []),
                         compiler_params=_cparams(3))(*ins, *xs)
    return (res[0], res[1:]) if n_x else res[0]


def _prev8_spec(a, tb, rev_nb=None):
    r = tb // 8
    if rev_nb is None:
        return pl.BlockSpec((8, a.shape[1]), lambda i: (jnp.maximum(i * r - 1, 0), 0))
    return pl.BlockSpec((8, a.shape[1]), lambda i: (jnp.maximum((rev_nb - 1 - i) * r - 1, 0), 0))


def _pre_a_fwd(h1, p, consts, tb=TOK_BLOCK):
    tn = h1.shape[0]

    def body(h1_ref, h1h_ref, p_ref, ph_ref, *rest):
        c_refs, o_refs = rest[:len(consts)], rest[len(consts):]
        first = pl.program_id(0) == 0
        h1p = jnp.where(first, 0.0, h1h_ref[7:8, :])
        pp = jnp.where(first, 0.0, ph_ref[7:8, :])
        outs = _pre_a_fn(h1_ref[...], h1p, p_ref[...], pp, *[c[...] for c in c_refs])
        for r, o in zip(o_refs, outs):
            r[...] = o

    out_shape = [jax.ShapeDtypeStruct((tn, RWKV_WIDTH), F32) for _ in range(7)]
    return pl.pallas_call(
        body, name="rwkv_pre_a_fwd", grid=(tn // tb,),
        in_specs=[_blk_spec(h1, tb), _prev8_spec(h1, tb), _blk_spec(p, tb), _prev8_spec(p, tb)]
        + [_full_spec(c) for c in consts],
        out_specs=[_blk_spec(o, tb) for o in out_shape], out_shape=out_shape,
        compiler_params=_cparams(1))(h1, h1, _arr(p), _arr(p), *consts)


def _pre_a_bwd(h1, p, consts, cts, tb=TOK_BLOCK):
    tn = h1.shape[0]
    nb = tn // tb
    n_c = len(consts)
    ct_groups = [c if isinstance(c, (tuple, list)) else (c,) for c in cts]
    ct_flat = [a for grp in ct_groups for a in grp]
    n_ct = len(ct_flat)

    def body(*refs):
        h1_ref, h1h_ref, p_ref, ph_ref = refs[:4]
        c_refs = refs[4:4 + n_c]
        ct_refs = refs[4 + n_c:4 + n_c + n_ct]
        dh1_ref, dp_ref = refs[4 + n_c + n_ct:6 + n_c + n_ct]
        dc_refs = refs[6 + n_c + n_ct:6 + 2 * n_c + n_ct]
        ch_ref, cp_ref = refs[-2], refs[-1]
        i = pl.program_id(0)
        first_block = i == nb - 1
        h1p = jnp.where(first_block, 0.0, h1h_ref[7:8, :])
        pp = jnp.where(first_block, 0.0, ph_ref[7:8, :])
        ct_v, q = [], 0
        for grp in ct_groups:
            s = ct_refs[q][...]
            for r in ct_refs[q + 1:q + len(grp)]:
                s = s + r[...]
            q += len(grp)
            ct_v.append(s)
        _, vjp = jax.vjp(_pre_a_fn, h1_ref[...], h1p, p_ref[...], pp, *[c[...] for c in c_refs])
        grads = vjp(tuple(ct_v))

        @pl.when(i == 0)
        def _():
            ch_ref[...] = jnp.zeros(ch_ref.shape, F32)
            cp_ref[...] = jnp.zeros(cp_ref.shape, F32)
            for r in dc_refs:
                r[...] = jnp.zeros(r.shape, F32)

        rowh = lax.broadcasted_iota(jnp.int32, (tb, h1.shape[1]), 0)
        rowp = lax.broadcasted_iota(jnp.int32, (tb, p.shape[1]), 0)
        dh1_ref[...] = grads[0] + jnp.where(rowh == tb - 1, jnp.broadcast_to(ch_ref[0:1, :], rowh.shape), 0.0)
        dp_ref[...] = (grads[2] + jnp.where(rowp == tb - 1, jnp.broadcast_to(cp_ref[0:1, :], rowp.shape), 0.0)
                       ).astype(dp_ref.dtype)
        ch_ref[0:1, :] = grads[1]
        cp_ref[0:1, :] = grads[3]
        for j, r in enumerate(dc_refs):
            r[...] += grads[4 + j]

    ins = [h1, h1, _arr(p), _arr(p)] + list(consts) + ct_flat
    in_specs = ([_blk_spec(h1, tb, nb), _prev8_spec(h1, tb, nb), _blk_spec(p, tb, nb), _prev8_spec(p, tb, nb)]
                + [_full_spec(c) for c in consts] + [_blk_spec(a, tb, nb) for a in ct_flat])
    out_shape = ([jax.ShapeDtypeStruct(h1.shape, F32), jax.ShapeDtypeStruct(p.shape, BF16)]
                 + [jax.ShapeDtypeStruct(c.shape, F32) for c in consts])
    out_specs = [_blk_spec(h1, tb, nb), _blk_spec(p, tb, nb)] + [_full_spec(c) for c in consts]
    return pl.pallas_call(body, name="rwkv_pre_a_bwd", grid=(nb,), in_specs=in_specs, out_specs=out_specs,
                          out_shape=out_shape,
                          scratch_shapes=[pltpu.VMEM((8, h1.shape[1]), F32), pltpu.VMEM((8, p.shape[1]), F32)],
                          compiler_params=_cparams(1))(*ins)


def _my_index():
    return 4 * lax.axis_index("x") + 2 * lax.axis_index("y") + lax.axis_index("c")


def _peer(k):
    x, y, c = lax.axis_index("x"), lax.axis_index("y"), lax.axis_index("c")
    px = 1 - x if k & 4 else x
    py = 1 - y if k & 2 else y
    pc = 1 - c if k & 1 else c
    return (px, py, pc), 4 * px + 2 * py + pc


def _xchg_sems(n):
    return [pltpu.SemaphoreType.DMA((n * (N_DEV - 1),)), pltpu.SemaphoreType.DMA((n * (N_DEV - 1),)),
            pltpu.SemaphoreType.DMA((n,))]


def _scatter_copies(srcs, dsts, sems, incoming=False):
    send_sems, recv_sems, local_sems = sems
    me = _my_index()
    local, remote = [], []
    for i, (s, d) in enumerate(zip(srcs, dsts)):
        if not incoming:
            local.append(pltpu.make_async_copy(s.at[me], d.at[me], local_sems.at[i]))
        for k in range(1, N_DEV):
            peer, plin = _peer(k)
            j = i * (N_DEV - 1) + k - 1
            s_slot, d_slot = (me, plin) if incoming else (plin, me)
            remote.append(pltpu.make_async_remote_copy(
                src_ref=s.at[s_slot], dst_ref=d.at[d_slot], send_sem=send_sems.at[j],
                recv_sem=recv_sems.at[j], device_id=peer, device_id_type=pl.DeviceIdType.MESH))
    return local, remote


def _scatter_start(srcs, dsts, sems):
    local, out = _scatter_copies(srcs, dsts, sems)
    for cp in local + out:
        cp.start()


def _scatter_wait(srcs, dsts, sems):
    for cp in _scatter_copies(srcs, dsts, sems, incoming=True)[1]:
        cp.wait_recv()
    local, out = _scatter_copies(srcs, dsts, sems)
    for cp in out:
        cp.wait_send()
    for cp in local:
        cp.wait()


_ICI_PEERS = (2, 4, 6)


def _gather_copies(srcs, dsts, sems, group):
    send_sems, recv_sems, local_sems = sems
    me = _my_index()
    sib, sib_lin = _peer(1)
    out = []
    for i, (s, d) in enumerate(zip(srcs, dsts)):
        def mk(q, src, dst, dev):
            j = i * (N_DEV - 1) + q
            return pltpu.make_async_remote_copy(src_ref=src, dst_ref=dst, send_sem=send_sems.at[j],
                                                recv_sem=recv_sems.at[j], device_id=dev,
                                                device_id_type=pl.DeviceIdType.MESH)
        if group == 'local':
            out.append(pltpu.make_async_copy(s, d.at[me], local_sems.at[i]))
        elif group == 'own':
            out.append(mk(0, s, d.at[me], sib))
        elif group == 'in_d2d':
            out.append(mk(0, s, d.at[sib_lin], sib))
        for jj, k in enumerate(_ICI_PEERS):
            peer, plin = _peer(k)
            plin_other = _peer(k + 1)[1]
            if group == 'own':
                out.append(mk(1 + jj, s, d.at[me], peer))
            elif group == 'in_ici':
                out.append(mk(1 + jj, s, d.at[plin], peer))
            elif group == 'pass_on':
                out.append(mk(4 + jj, d.at[plin], d.at[plin], sib))
            elif group == 'in_d2d':
                out.append(mk(4 + jj, d.at[plin_other], d.at[plin_other], sib))
    return out


def _gather_start(srcs, dsts, sems):
    for cp in _gather_copies(srcs, dsts, sems, 'local') + _gather_copies(srcs, dsts, sems, 'own'):
        cp.start()


def _gather_pass_on(srcs, dsts, sems):
    for cp in _gather_copies(srcs, dsts, sems, 'in_ici'):
        cp.wait_recv()
    for cp in _gather_copies(srcs, dsts, sems, 'pass_on'):
        cp.start()


def _gather_finish(srcs, dsts, sems):
    for cp in _gather_copies(srcs, dsts, sems, 'in_d2d'):
        cp.wait_recv()
    for cp in _gather_copies(srcs, dsts, sems, 'own') + _gather_copies(srcs, dsts, sems, 'pass_on'):
        cp.wait_send()
    for cp in _gather_copies(srcs, dsts, sems, 'local'):
        cp.wait()


def _xchg_out_shapes(srcs, scatter):
    return [jax.ShapeDtypeStruct(s.shape if scatter else (N_DEV,) + s.shape, s.dtype) for s in srcs]


_ANY = pl.BlockSpec(memory_space=pl.ANY)


def _exchange(name, srcs, scatter):
    n = len(srcs)

    def body(*refs):
        s, d, sems = refs[:n], refs[n:2 * n], refs[2 * n:]
        if scatter:
            _scatter_start(s, d, sems)
            _scatter_wait(s, d, sems)
        else:
            _gather_start(s, d, sems)
            _gather_pass_on(s, d, sems)
            _gather_finish(s, d, sems)

    return pl.pallas_call(body, name=name, in_specs=[_ANY] * n, out_specs=[_ANY] * n,
                          out_shape=_xchg_out_shapes(srcs, scatter), scratch_shapes=_xchg_sems(n))(*srcs)


_MM_DIMS = {'nn': (((1,), (0,)), ((), ())), 'nt': (((1,), (1,)), ((), ())), 'tn': (((0,), (0,)), ((), ()))}


def _cmm_raw(x, y, kind, split):
    dot = functools.partial(lax.dot_general, dimension_numbers=_MM_DIMS[kind], preferred_element_type=F32)
    xh, yh = x.astype(BF16), y.astype(BF16)
    out = dot(xh, yh)
    if split:
        xl = (x - xh.astype(F32)).astype(BF16)
        yl = (y - yh.astype(F32)).astype(BF16)
        out = out + (dot(xh, yl) + dot(xl, yh))
    return out


@functools.partial(jax.custom_vjp, nondiff_argnums=(2, 3))
def _cmm(x, y, kind, split=False):
    return _cmm_raw(x, y, kind, split)


def _cmm_fwd(x, y, kind, split):
    return _cmm_raw(x, y, kind, split), (x, y)


def _cmm_bwd(kind, split, res, g):
    x, y = res
    if kind == 'nn':
        return _cmm_raw(g, y, 'nt', split), _cmm_raw(x, g, 'tn', split)
    if kind == 'nt':
        return _cmm_raw(g, y, 'nn', split), _cmm_raw(g, x, 'tn', split)
    return _cmm_raw(y, g, 'nt', split), _cmm_raw(x, g, 'nn', split)


_cmm.defvjp(_cmm_fwd, _cmm_bwd)


def _tri_sum_raw(tri, x, kind):
    dot = functools.partial(lax.dot_general, dimension_numbers=_MM_DIMS[kind], preferred_element_type=F32)
    tb = tri.astype(BF16)
    hi, mid, lo = _split3(x)
    return (dot(tb, hi) + dot(tb, mid)) + dot(tb, lo)


@functools.partial(jax.custom_vjp, nondiff_argnums=(2,))
def _tri_sum(tri, x, kind):
    return _tri_sum_raw(tri, x, kind)


def _tri_sum_fwd(tri, x, kind):
    return _tri_sum_raw(tri, x, kind), tri


def _tri_sum_bwd(kind, tri, g):
    return jnp.zeros_like(tri), _tri_sum_raw(tri, g, 'tn' if kind == 'nn' else 'nn')


_tri_sum.defvjp(_tri_sum_fwd, _tri_sum_bwd)


def _chunk_fn(S0, r, lw, k, v, a, b):
    hs = range(len(r))
    C = r[0].shape[0]
    ii = lax.broadcasted_iota(jnp.int32, (C, C), 0)
    jj = lax.broadcasted_iota(jnp.int32, (C, C), 1)
    incl, strict = ii >= jj, ii > jj
    eye = (ii == jj).astype(F32)
    inclf = incl.astype(F32)
    cum = [_tri_sum(inclf, lw[h], 'nn') for h in hs]
    e_inv = [jnp.exp(-cum[h]) for h in hs]
    At = [a[h] * jnp.exp(cum[h] - lw[h]) for h in hs]
    Rt = [r[h] * jnp.exp(cum[h]) for h in hs]
    Kh = [k[h] * e_inv[h] for h in hs]
    Bh = [b[h] * e_inv[h] for h in hs]
    Mab = [jnp.where(strict, _cmm(At[h], Bh[h], 'nt'), 0.0) for h in hs]
    Mak = [jnp.where(strict, _cmm(At[h], Kh[h], 'nt', True), 0.0) for h in hs]
    Mrk = [jnp.where(incl, _cmm(Rt[h], Kh[h], 'nt', True), 0.0) for h in hs]
    Mrb = [jnp.where(incl, _cmm(Rt[h], Bh[h], 'nt'), 0.0) for h in hs]
    rhs = [_cmm(At[h], S0[h], 'nt') + _cmm(Mak[h], v[h], 'nn') for h in hs]
    P = Mab
    Tm = [eye + P[h] for h in hs]
    n = 1
    while 2 * n < C:
        P = [_cmm(P[h], P[h], 'nn') for h in hs]
        Tm = [_cmm(Tm[h], eye + P[h], 'nn') for h in hs]
        n *= 2
    U = [_cmm(Tm[h], rhs[h], 'nn') for h in hs]
    Y = [_cmm(Rt[h], S0[h], 'nt') + _cmm(Mrk[h], v[h], 'nn') + _cmm(Mrb[h], U[h], 'nn') for h in hs]
    gC = [jnp.exp(jnp.sum(lw[h], axis=0, keepdims=True)) for h in hs]
    SC = [S0[h] * gC[h] + _cmm(v[h], Kh[h] * gC[h], 'tn') + _cmm(U[h], Bh[h] * gC[h], 'tn') for h in hs]
    return tuple(Y), tuple(SC)


def _cscan_fwd(r, lw, k, v, a, b, xs):
    n_x = len(xs)
    tn = r.shape[0]
    H, Dh, Dv = RWKV_HEADS, RWKV_HEAD_DIM, RWKV_HEAD_DIM
    nc = tn // SCAN_CHUNK
    lanes = lambda h: slice(h * Dh, (h + 1) * Dh)
    heads = lambda ref: tuple(ref[:, lanes(h)] for h in range(H))
    mats = lambda ref: tuple(ref[h] for h in range(H))

    def body(r_ref, lw_ref, k_ref, v_ref, a_ref, b_ref, *rest):
        x_src, (y_ref, ck_ref) = rest[:n_x], rest[n_x:n_x + 2]
        x_dst, s_ref, sems = rest[n_x + 2:2 * n_x + 2], rest[2 * n_x + 2], rest[2 * n_x + 3:]

        @pl.when(pl.program_id(0) == 0)
        def _():
            s_ref[...] = jnp.zeros(s_ref.shape, F32)
            _gather_start(x_src, x_dst, sems)

        ck_ref[0] = s_ref[...]
        y, sc = _chunk_fn(mats(s_ref), heads(r_ref), heads(lw_ref), heads(k_ref), heads(v_ref), heads(a_ref),
                          heads(b_ref))
        for h in range(H):
            y_ref[:, lanes(h)] = y[h]
            s_ref[h] = sc[h]

        @pl.when(pl.program_id(0) == max(nc - 4, 0))
        def _():
            _gather_pass_on(x_src, x_dst, sems)

        @pl.when(pl.program_id(0) == nc - 1)
        def _():
            _gather_finish(x_src, x_dst, sems)

    hm = pl.BlockSpec((SCAN_CHUNK, H * Dh), lambda c: (c, 0))
    res = pl.pallas_call(
        body, name="rwkv_scan_fwd", grid=(nc,), in_specs=[hm] * 6 + [_ANY] * n_x,
        out_specs=[hm, pl.BlockSpec((1, H, Dv, Dh), lambda c: (c, 0, 0, 0))] + [_ANY] * n_x,
        out_shape=[jax.ShapeDtypeStruct((tn, H * Dh), F32), jax.ShapeDtypeStruct((nc, H, Dv, Dh), F32)]
        + _xchg_out_shapes(xs, False),
        scratch_shapes=[pltpu.VMEM((H, Dv, Dh), F32)] + _xchg_sems(n_x),
        compiler_params=_cparams(1))(r, lw, k, v, a, b, *xs)
    return res[0], res[1], res[2:]


def _cscan_bwd(r, lw, k, v, a, b, dy, ck, xs):
    n_x = len(xs)
    tn = r.shape[0]
    H, Dh, Dv = RWKV_HEADS, RWKV_HEAD_DIM, RWKV_HEAD_DIM
    nc = tn // SCAN_CHUNK
    lanes = lambda h: slice(h * Dh, (h + 1) * Dh)
    heads = lambda ref: tuple(ref[:, lanes(h)] for h in range(H))
    mats = lambda ref: tuple(ref[h] for h in range(H))

    def body(r_ref, lw_ref, k_ref, v_ref, a_ref, b_ref, dy_ref, ck_ref, *rest):
        x_src = rest[:n_x]
        d_refs = rest[n_x:n_x + 6]
        x_dst = rest[n_x + 6:2 * n_x + 6]
        g_ref = rest[2 * n_x + 6]
        sems = rest[2 * n_x + 7:]

        @pl.when(pl.program_id(0) == 0)
        def _():
            g_ref[...] = jnp.zeros(g_ref.shape, F32)
            _scatter_start(x_src, x_dst, sems)

        s0 = tuple(ck_ref[0, h] for h in range(H))
        _, vjp = jax.vjp(_chunk_fn, s0, heads(r_ref), heads(lw_ref), heads(k_ref), heads(v_ref), heads(a_ref),
                         heads(b_ref))
        grads = vjp((heads(dy_ref), mats(g_ref)))
        for h in range(H):
            g_ref[h] = grads[0][h]
            for d_ref, gz in zip(d_refs, grads[1:]):
                d_ref[:, lanes(h)] = gz[h]

        @pl.when(pl.program_id(0) == nc - 1)
        def _():
            _scatter_wait(x_src, x_dst, sems)

    hm = pl.BlockSpec((SCAN_CHUNK, H * Dh), lambda c: (nc - 1 - c, 0))
    hshape = jax.ShapeDtypeStruct((tn, H * Dh), F32)
    res = pl.pallas_call(
        body, name="rwkv_scan_bwd", grid=(nc,),
        in_specs=[hm] * 7 + [pl.BlockSpec((1, H, Dv, Dh), lambda c: (nc - 1 - c, 0, 0, 0))] + [_ANY] * n_x,
        out_specs=[hm] * 6 + [_ANY] * n_x, out_shape=[hshape] * 6 + _xchg_out_shapes(xs, True),
        scratch_shapes=[pltpu.VMEM((H, Dv, Dh), F32)] + _xchg_sems(n_x),
        compiler_params=_cparams(1))(r, lw, k, v, a, b, dy, ck, *xs)
    return res[:6], res[6:]


def _decay_mask(lg, i, j, blk):
    rows = lax.broadcasted_iota(jnp.int32, (blk, blk), 0)
    cols = lax.broadcasted_iota(jnp.int32, (blk, blk), 1)
    dd = (rows - cols + (i - j) * blk).astype(F32)
    return jnp.where(dd >= 0.0, jnp.exp(lg * jnp.maximum(dd, 0.0)), 0.0)


_NT = (((1,), (1,)), ((), ()))
_TN = (((0,), (0,)), ((), ()))


def _ret_attn_fwd(lg, q, k, v, v_col0=0, blk=ATT_BLOCK):
    tn = q.shape[0]
    Dh = RET_HEAD_DIM

    def body(lg_ref, q_ref, k_ref, v_ref, o_ref):
        i = pl.program_id(1)
        lgv = lg_ref[0][:, 0:1]
        qb = q_ref[...].astype(BF16)

        def jb(j, acc):
            ks = pl.ds(pl.multiple_of(j * blk, blk), blk)
            s = lax.dot_general(qb, k_ref[ks, :].astype(BF16), _NT, preferred_element_type=F32)
            s = s * _decay_mask(lgv, i, j, blk)
            return acc + jnp.dot(s.astype(BF16), v_ref[ks, :].astype(BF16), preferred_element_type=F32)

        o_ref[...] = lax.fori_loop(0, i + 1, jb, jnp.zeros((blk, Dh), F32))

    full = pl.BlockSpec((tn, Dh), lambda h, i: (0, h))
    qs = pl.BlockSpec((blk, Dh), lambda h, i: (i, h))
    return pl.pallas_call(
        body, name="ret_attn_fwd", grid=(RET_HEADS, tn // blk),
        in_specs=[pl.BlockSpec((1, 1, 128), lambda h, i: (h, 0, 0)), qs, full,
                  pl.BlockSpec((tn, Dh), lambda h, i: (0, v_col0 + h))],
        out_specs=qs, out_shape=jax.ShapeDtypeStruct(q.shape, F32), compiler_params=_cparams(2))(lg, q, k, v)


def _ret_attn_bwd(lg, q, k, v, do, v_col0=0, blk=ATT_BLOCK):
    tn = q.shape[0]
    nb = tn // blk
    Dh = RET_HEAD_DIM

    def body(lg_ref, q_ref, k_ref, v_ref, do_ref, dq_ref, dk_ref, dv_ref):
        lgv = lg_ref[0][:, 0:1]
        dk_ref[...] = jnp.zeros(dk_ref.shape, F32)
        dv_ref[...] = jnp.zeros(dv_ref.shape, F32)

        def ib(i, carry):
            qs = pl.ds(pl.multiple_of(i * blk, blk), blk)
            qb = q_ref[qs, :].astype(BF16)
            dob = do_ref[qs, :].astype(BF16)

            def jb(j, dq):
                ks = pl.ds(pl.multiple_of(j * blk, blk), blk)
                kb = k_ref[ks, :].astype(BF16)
                vb = v_ref[ks, :].astype(BF16)
                dm = _decay_mask(lgv, i, j, blk)
                s = lax.dot_general(qb, kb, _NT, preferred_element_type=F32) * dm
                ds = lax.dot_general(dob, vb, _NT, preferred_element_type=F32) * dm
                sb, dsb = s.astype(BF16), ds.astype(BF16)
                dv_ref[ks, :] += lax.dot_general(sb, dob, _TN, preferred_element_type=F32)
                dk_ref[ks, :] += lax.dot_general(dsb, qb, _TN, preferred_element_type=F32)
                return dq + jnp.dot(dsb, kb, preferred_element_type=F32)

            dq_ref[qs, :] = lax.fori_loop(0, i + 1, jb, jnp.zeros((blk, Dh), F32))
            return carry

        lax.fori_loop(0, nb, ib, 0)

    full = pl.BlockSpec((tn, Dh), lambda h: (0, h))
    sh = jax.ShapeDtypeStruct(q.shape, F32)
    return pl.pallas_call(
        body, name="ret_attn_bwd", grid=(RET_HEADS,),
        in_specs=[pl.BlockSpec((1, 1, 128), lambda h: (h, 0, 0)), full, full,
                  pl.BlockSpec((tn, Dh), lambda h: (0, v_col0 + h)), full],
        out_specs=[full, full, full], out_shape=[sh, sh, sh], compiler_params=_cparams(1))(lg, q, k, v, do)


def _next8_spec(a, tb):
    r = tb // 8
    last = a.shape[0] // 8 - 1
    return pl.BlockSpec((8, a.shape[1]), lambda i: (jnp.minimum((i + 1) * r, last), 0))


def _conv_taps(g_ext, cw_ref, cb_ref):
    return (cw_ref[2:3, :] * g_ext + cw_ref[1:2, :] * pltpu.roll(g_ext, 1, 0)
            + cw_ref[0:1, :] * pltpu.roll(g_ext, 2, 0) + cb_ref[...])


def _glu_fwd(gate, up, cw, cb, tb=TOK_BLOCK):
    tn = gate.shape[0]

    def body(g_ref, gh_ref, u_ref, cw_ref, cb_ref, o_ref):
        halo = jnp.where(pl.program_id(0) == 0, 0.0, gh_ref[...])
        g_ext = jnp.concatenate([halo, g_ref[...]], axis=0)
        gc = _conv_taps(g_ext, cw_ref, cb_ref)[8:, :]
        o_ref[...] = (gc * _sigmoid(gc) * u_ref[...]).astype(o_ref.dtype)

    return pl.pallas_call(
        body, name="glu_fwd", grid=(tn // tb,),
        in_specs=[_blk_spec(gate, tb), _prev8_spec(gate, tb), _blk_spec(up, tb), _full_spec(cw), _full_spec(cb)],
        out_specs=_blk_spec(gate, tb), out_shape=jax.ShapeDtypeStruct(gate.shape, BF16),
        compiler_params=_cparams(1))(gate, gate, up, cw, cb)


def _glu_bwd(gate, up, dact, cw, cb, tb=TOK_BLOCK):
    tn = gate.shape[0]
    nb = tn // tb

    def body(g_ref, gp_ref, gn_ref, u_ref, un_ref, d_ref, dn_ref, cw_ref, cb_ref, dg_ref, du_ref, dcw_ref, dcb_ref):
        i = pl.program_id(0)
        gprev = jnp.where(i == 0, 0.0, gp_ref[...])
        dnext = jnp.where(i == nb - 1, 0.0, dn_ref[...])
        g_ext = jnp.concatenate([gprev, g_ref[...], gn_ref[...]], axis=0)
        gc = _conv_taps(g_ext, cw_ref, cb_ref)[8:, :]
        u_e = jnp.concatenate([u_ref[...], un_ref[...]], axis=0)
        d_e = jnp.concatenate([d_ref[...], dnext], axis=0)
        s = _sigmoid(gc)
        dgc = d_e * u_e * (s * (1.0 + gc * (1.0 - s)))
        du_ref[...] = (d_ref[...] * (gc * s)[:tb, :]).astype(du_ref.dtype)
        n_e = tb + 8
        dg_ref[...] = (cw_ref[2:3, :] * dgc + cw_ref[1:2, :] * pltpu.roll(dgc, n_e - 1, 0)
                       + cw_ref[0:1, :] * pltpu.roll(dgc, n_e - 2, 0))[:tb, :].astype(dg_ref.dtype)

        @pl.when(i == 0)
        def _():
            dcw_ref[...] = jnp.zeros(dcw_ref.shape, F32)
            dcb_ref[...] = jnp.zeros(dcb_ref.shape, F32)

        dgc_b = dgc[:tb, :]
        g0 = g_ext[8:8 + tb, :]
        g1 = pltpu.roll(g_ext, 1, 0)[8:8 + tb, :]
        g2 = pltpu.roll(g_ext, 2, 0)[8:8 + tb, :]
        dcw_ref[2:3, :] += jnp.sum(dgc_b * g0, axis=0, keepdims=True)
        dcw_ref[1:2, :] += jnp.sum(dgc_b * g1, axis=0, keepdims=True)
        dcw_ref[0:1, :] += jnp.sum(dgc_b * g2, axis=0, keepdims=True)
        dcb_ref[...] += jnp.sum(dgc_b, axis=0, keepdims=True)

    sh = jax.ShapeDtypeStruct(gate.shape, BF16)
    return pl.pallas_call(
        body, name="glu_bwd", grid=(nb,),
        in_specs=[_blk_spec(gate, tb), _prev8_spec(gate, tb), _next8_spec(gate, tb), _blk_spec(up, tb),
                  _next8_spec(up, tb), _blk_spec(dact, tb), _next8_spec(dact, tb), _full_spec(cw), _full_spec(cb)],
        out_specs=[_blk_spec(gate, tb), _blk_spec(gate, tb), _full_spec(cw), _full_spec(cb)],
        out_shape=[sh, sh, jax.ShapeDtypeStruct(cw.shape, F32), jax.ShapeDtypeStruct(cb.shape, F32)],
        compiler_params=_cparams(1))(gate, gate, gate, up, up, dact, dact, cw, cb)


def _final_loss(x2, tgt, g, tb=LIGHT_TOK_BLOCK):
    tn, dm = x2.shape

    def body(x_ref, t_ref, g_ref, l_ref, dx_ref, dg_ref):
        y, vjp = jax.vjp(_rms_fn, x_ref[...], g_ref[...])
        err = y - t_ref[...]
        dx, dg = vjp(err * (1.0 / dm))

        @pl.when(pl.program_id(0) == 0)
        def _():
            l_ref[...] = jnp.zeros(l_ref.shape, F32)
            dg_ref[...] = jnp.zeros(dg_ref.shape, F32)

        part = 0.5 * jnp.sum(jnp.mean(err * err, axis=-1, keepdims=True), axis=0, keepdims=True)
        l_ref[...] += jnp.broadcast_to(part, l_ref.shape)
        dx_ref[...] = dx
        dg_ref[...] += dg

    return pl.pallas_call(
        body, name="final_loss", grid=(tn // tb,),
        in_specs=[_blk_spec(x2, tb), _blk_spec(tgt, tb), _full_spec(g)],
        out_specs=[pl.BlockSpec((8, 128), lambda i: (0, 0)), _blk_spec(x2, tb), _full_spec(g)],
        out_shape=[jax.ShapeDtypeStruct((8, 128), F32), jax.ShapeDtypeStruct(x2.shape, F32),
                   jax.ShapeDtypeStruct(g.shape, F32)],
        compiler_params=_cparams(1))(x2, tgt, g)


def _pad_cols(w, n):
    return jnp.pad(w, ((0, 0), (0, n - w.shape[1])))


def _pad_rows(w, n):
    return jnp.pad(w, ((0, n - w.shape[0]), (0, 0)))


def _local_step(x, tgt, W, late):
    tn = x.shape[0]
    Wd = RWKV_WIDTH
    row = lambda z: z.reshape(1, -1)
    g_mix, g_ffn, g_fin = row(W['norm_mix_g']), row(W['norm_ffn_g']), row(W['norm_final_g'])

    (h1,) = _tok_fwd("norm_mix_fwd", lambda a, g: (_rms_fn(a, g),), [x], [g_mix], [(D_MODEL,)])
    proj = _mm("proj_fwd", h1, W['w_in_t'], tb=True)
    p_rkv = _Cols(proj, 3 * Wd, 0)
    pre_consts = [row(W['rwkv_mu_w']), row(W['rwkv_mu_a']), row(W['rwkv_mu_g']), row(W['rwkv_mu_r']),
                  row(W['rwkv_mu_k']), row(W['rwkv_mu_v']), row(W['rwkv_w0']),
                  _pad_cols(W['rwkv_w1'], LORA_PAD), _pad_rows(W['rwkv_w2'], LORA_PAD), row(W['rwkv_a0']),
                  _pad_cols(W['rwkv_a1'], LORA_PAD), _pad_rows(W['rwkv_a2'], LORA_PAD),
                  W['rwkv_g1'], W['rwkv_g2'], row(W['rwkv_k_k']), row(W['rwkv_k_a'])]
    r, k, v, lw, nkk, b, g = _pre_a_fwd(h1, p_rkv, pre_consts)
    y_scan, ck, gathered = _cscan_fwd(r, lw, k, v, nkk, b, late)
    w_out, w_gate_t, w_up_t, w_down = [g_.reshape(-1, D_MODEL) for g_ in gathered]
    post_consts = [row(W['rwkv_lnx_w']), row(W['rwkv_lnx_b']), row(W['rwkv_r_k'])]
    (y_rwkv,) = _tok_fwd("rwkv_post_fwd", _rwkv_post_fn, [y_scan, r, k, v, g], post_consts, [(Wd,)],
                         out_dtypes=[BF16])

    pos = jnp.arange(tn, dtype=F32)
    half = RET_HEAD_DIM // 2
    inv_freq = ROPE_BASE ** (-jnp.arange(half, dtype=F32) / half)
    ang = pos[:, None] * inv_freq[None, :]
    cos2 = jnp.concatenate([jnp.cos(ang), jnp.cos(ang)], axis=1)
    sin2 = jnp.concatenate([-jnp.sin(ang), jnp.sin(ang)], axis=1)
    lg = jnp.log(1.0 - 2.0 ** (-5.0 - jnp.arange(RET_HEADS, dtype=F32)))
    lg = jnp.broadcast_to(lg[:, None, None], (RET_HEADS, 1, 128))
    q_p, k_p, g_ret = _Cols(proj, Wd, 3), _Cols(proj, Wd, 4), _Cols(proj, Wd, 6)
    v_col0 = 5 * Wd // RET_HEAD_DIM
    q_rot, k_rot = _tok_fwd("ret_rotary_fwd", _rotary_fn, [cos2, sin2, q_p, k_p], [], [(RET_WIDTH,)] * 2)
    y_ret_raw = _ret_attn_fwd(lg, q_rot, k_rot, proj, v_col0)
    gn_w = row(W['ret_gn_w'])
    (y_ret,) = _tok_fwd("ret_post_fwd", _ret_post_fn, [y_ret_raw, g_ret], [gn_w], [(RET_WIDTH,)],
                        out_dtypes=[BF16])

    ycat = jnp.concatenate([y_rwkv, y_ret], axis=1)
    x1 = _mm("out_proj_fwd", ycat, w_out, add=x)
    (h2,) = _tok_fwd("norm_ffn_fwd", lambda a_, g_: (_rms_fn(a_, g_),), [x1], [g_ffn], [(D_MODEL,)],
                     out_dtypes=[BF16])
    gate = _mm("ffn_gate_fwd", h2, w_gate_t, tb=True)
    up = _mm("ffn_up_fwd", h2, w_up_t, tb=True)
    cw = W['ffn_conv_w']
    cb = row(W['ffn_conv_b'])
    act = _glu_fwd(gate, up, cw, cb)
    x2 = _mm("ffn_down_fwd", act, w_down, add=x1)
    loss8, dx2, dg_fin = _final_loss(x2, tgt, g_fin)

    G = {'norm_final_g': dg_fin}
    dact = _mm("ffn_down_dx", dx2, w_down, tb=True)
    d_down = _mm("ffn_down_dw", act, dx2, ta=True, out_dtype=BF16)
    dgate, dup, dcw, dcb = _glu_bwd(gate, up, dact, cw, cb)
    G['ffn_conv_w'], G['ffn_conv_b'] = dcw, dcb
    dh2 = _mm("ffn_gate_dx", dgate, w_gate_t)
    dh2 = _mm("ffn_up_dx", dup, w_up_t, add=dh2)
    d_gate_t = _mm("ffn_gate_dw", dgate, h2, ta=True, out_dtype=BF16)
    d_up_t = _mm("ffn_up_dw", dup, h2, ta=True, out_dtype=BF16)
    dx1, G['norm_ffn_g'] = _tok_bwd("norm_ffn_bwd", lambda a_, g_: (_rms_fn(a_, g_),), [], [x1], [g_ffn], [dh2], add=dx2)
    dycat = _mm("out_proj_dx", dx1, w_out, tb=True)
    d_out = _mm("out_proj_dw", ycat, dx1, ta=True, out_dtype=BF16)
    late_grads = [z.reshape(N_DEV, -1, D_MODEL) for z in (d_out, d_gate_t, d_up_t, d_down)]
    dy_rwkv, dy_ret = _Cols(dycat, Wd, 0), _Cols(dycat, Wd, 1)

    dyr_raw, dg_ret, G['ret_gn_w'] = _tok_bwd("ret_post_bwd", _ret_post_fn, [], [y_ret_raw, g_ret], [gn_w], [dy_ret],
                                              tok_dtypes=[F32, BF16])
    dq_rot, dk_rot, dv_ret = _ret_attn_bwd(lg, q_rot, k_rot, proj, dyr_raw, v_col0)
    dq_p, dk_p = _tok_bwd("ret_rotary_bwd", _rotary_fn, [cos2, sin2], [q_p, k_p], [], [dq_rot, dk_rot],
                          tok_dtypes=[BF16, BF16])

    dy_scan, dr1, dk1, dv1, dg, G['rwkv_lnx_w'], G['rwkv_lnx_b'], G['rwkv_r_k'] = _tok_bwd(
        "rwkv_post_bwd", _rwkv_post_fn, [], [y_scan, r, k, v, g], post_consts, [dy_rwkv])
    (dr2, dlw, dk2, dv2, dnkk, db), late_parts = _cscan_bwd(r, lw, k, v, nkk, b, dy_scan, ck, late_grads)
    pre_cts = [(dr1, dr2), (dk1, dk2), (dv1, dv2), dlw, dnkk, db, dg]
    pre_out = _pre_a_bwd(h1, p_rkv, pre_consts, pre_cts)
    dh1_a, dp_rkv = pre_out[0], pre_out[1]
    (G['rwkv_mu_w'], G['rwkv_mu_a'], G['rwkv_mu_g'], G['rwkv_mu_r'], G['rwkv_mu_k'], G['rwkv_mu_v'], G['rwkv_w0'],
     dw1, dw2, G['rwkv_a0'], da1, da2, G['rwkv_g1'], G['rwkv_g2'], G['rwkv_k_k'], G['rwkv_k_a']) = pre_out[2:]
    G['rwkv_w1'], G['rwkv_w2'] = dw1[:, :64], dw2[:64, :]
    G['rwkv_a1'], G['rwkv_a2'] = da1[:, :64], da2[:64, :]

    dproj = jnp.concatenate([dp_rkv, dq_p, dk_p, dv_ret.astype(BF16), dg_ret], axis=1)
    d_in_t = _mm("proj_dw", dproj, h1, ta=True, out_dtype=BF16)
    dh1, (w_in_parts,) = _mm("proj_dx", dproj, W['w_in_t'], add=dh1_a, xs=[d_in_t.reshape(N_DEV, -1, D_MODEL)])
    dx, G['norm_mix_g'] = _tok_bwd("norm_mix_bwd", lambda a_, g_: (_rms_fn(a_, g_),), [], [x], [g_mix], [dh1], add=dx1)
    return loss8[0, 0], dx, G, late_parts, w_in_parts


def _adamw_block(p_ref, w_ref, m_ref, v_ref, g_ref, d_ref, nm_ref, nv_ref):
    g = p_ref[0].astype(F32)
    for d in range(1, N_DEV):
        g = g + p_ref[d].astype(F32)
    mn = ADAM_B1 * m_ref[...] + (1.0 - ADAM_B1) * g
    vn = ADAM_B2 * v_ref[...] + (1.0 - ADAM_B2) * (g * g)
    m_hat = mn / (1.0 - ADAM_B1 ** ADAM_STEP)
    v_hat = vn / (1.0 - ADAM_B2 ** ADAM_STEP)
    g_ref[...] = g
    d_ref[...] = -ADAM_LR * (m_hat / (jnp.sqrt(v_hat) + ADAM_EPS) + ADAM_WD * w_ref[...])
    nm_ref[...] = mn
    nv_ref[...] = vn


def _adamw_late(items, xs, tb=32):
    n_it, n_x = len(items), len(xs)
    nbs = [it[1].shape[0] // tb for it in items]
    steps = max(nbs)
    cols = items[0][1].shape[1]

    def body(*refs):
        ins, x_src = refs[:4 * n_it], refs[4 * n_it:4 * n_it + n_x]
        outs = refs[4 * n_it + n_x:8 * n_it + n_x]
        x_dst, sems = refs[8 * n_it + n_x:8 * n_it + 2 * n_x], refs[8 * n_it + 2 * n_x:]
        i = pl.program_id(0)

        @pl.when(i == 0)
        def _():
            _scatter_start(x_src, x_dst, sems)

        for j in range(n_it):
            @pl.when(i < nbs[j])
            def _(j=j):
                _adamw_block(*ins[4 * j:4 * j + 4], *outs[4 * j:4 * j + 4])

        @pl.when(i == steps - 1)
        def _():
            _scatter_wait(x_src, x_dst, sems)

    in_specs, out_specs, out_shape, flat = [], [], [], []
    for (parts, w, m, v), nb in zip(items, nbs):
        blk = pl.BlockSpec((tb, cols), lambda i, nb=nb: (jnp.minimum(i, nb - 1), 0))
        in_specs += [pl.BlockSpec((N_DEV, tb, cols), lambda i, nb=nb: (0, jnp.minimum(i, nb - 1), 0)), blk, blk, blk]
        out_specs += [blk] * 4
        out_shape += [jax.ShapeDtypeStruct(w.shape, F32)] * 4
        flat += [parts, w, m, v]
    res = pl.pallas_call(
        body, name="adamw_late", grid=(steps,), in_specs=in_specs + [_ANY] * n_x, out_specs=out_specs + [_ANY] * n_x,
        out_shape=out_shape + _xchg_out_shapes(xs, True), scratch_shapes=_xchg_sems(n_x),
        compiler_params=_cparams(1))(*flat, *xs)
    return [res[4 * j:4 * j + 4] for j in range(n_it)], res[4 * n_it:]


def _adamw(name, parts, w, m, v):
    rows, cols = w.shape
    sub = 8 * 4 // parts.dtype.itemsize
    tb = max(t for t in range(sub, 65, sub) if rows % t == 0) if rows > 64 else rows
    body = functools.partial(_adamw_block)
    spec = pl.BlockSpec((tb, cols), lambda i: (i, 0))
    sh = jax.ShapeDtypeStruct((rows, cols), F32)
    return pl.pallas_call(
        body, name=name, grid=(rows // tb,),
        in_specs=[pl.BlockSpec((N_DEV, tb, cols), lambda i: (0, i, 0)), spec, spec, spec],
        out_specs=[spec] * 4, out_shape=[sh] * 4, compiler_params=_cparams(1))(parts, w, m, v)


def _local_shape(name):
    gs, ax = SHARDED[name]
    ls = list(gs)
    ls[ax] //= N_DEV
    return tuple(ls)


def _seg(flat, seg):
    n = flat.shape[-1]
    pad = _round_up(n, seg) - n
    if pad:
        flat = jnp.pad(flat, [(0, 0)] * (flat.ndim - 1) + [(0, pad)])
    return flat


def _split3(w):
    hi = w.astype(BF16)
    r1 = w - hi.astype(F32)
    mid = r1.astype(BF16)
    lo = (r1 - mid.astype(F32)).astype(BF16)
    return hi, mid, lo


def _pack_small_shards(shards):
    pieces = []
    for name in SMALL_NAMES:
        flat = shards[name].reshape(-1)
        if name == 'ffn_conv_w':
            pieces += [_seg(p, BF16_SEG) for p in _split3(flat)]
        else:
            pieces.append(flat.astype(BF16))
    return jnp.concatenate(pieces).reshape(-1, 128)


def _unpack_small(gathered):
    flat = gathered.reshape(N_DEV, -1)
    out, off = {}, 0
    for name in SMALL_NAMES:
        gs, ax = SHARDED[name]
        ls = _local_shape(name)
        n = int(np.prod(ls))
        if name == 'ffn_conv_w':
            nseg = _round_up(n, BF16_SEG)
            hi, mid, lo = (flat[:, off + j * nseg: off + j * nseg + n].astype(F32) for j in range(3))
            sh = ((hi + mid) + lo).reshape(N_DEV, 3, -1)
            out[name] = jnp.swapaxes(sh, 0, 1).reshape(3, D_FF)
            off += 3 * nseg
        else:
            sh = flat[:, off:off + n].reshape((N_DEV,) + ls[1:])
            out[name] = sh.reshape(gs[1:]) if ax == 1 else jnp.swapaxes(sh, 0, 1).reshape(gs[1:])
            off += n
    return out


def _small_pieces(sharded, repl):
    return [sharded[n].reshape(-1) for n in SMALL_NAMES] + [repl[n].reshape(-1) for n in REPL_NAMES]


def _pack_small_local(d):
    flat = jnp.concatenate(_small_pieces(d, d))
    return _seg(flat, F32_SEG).reshape(-1, 128)


def _pack_small_grads(G, loss):
    pieces = []
    for name in SMALL_NAMES:
        gs, ax = SHARDED[name]
        g = G[name]
        if name == 'ffn_conv_w':
            sh = jnp.swapaxes(g.reshape(3, N_DEV, -1), 0, 1)
        elif ax == 1:
            sh = g
        else:
            sh = jnp.swapaxes(g.reshape(g.shape[0], N_DEV, -1), 0, 1)
        pieces.append(sh.reshape(N_DEV, -1))
    rep = jnp.concatenate([G[n].reshape(-1) for n in REPL_NAMES] + [loss.reshape(1)])
    pieces.append(jnp.broadcast_to(rep[None, :], (N_DEV, rep.shape[0])))
    flat = _seg(jnp.concatenate(pieces, axis=1), F32_SEG)
    return flat.reshape(N_DEV, -1, 128)


def _unpack_small_local(packed, local_shapes):
    flat = packed.reshape(-1)
    out, off = {}, 0
    for name in SMALL_NAMES + REPL_NAMES:
        n = int(np.prod(local_shapes[name]))
        out[name] = flat[off:off + n].reshape(local_shapes[name])
        off += n
    return out


def kernel(x, *rest):
    nw = len(WEIGHT_NAMES)
    assert len(rest) == 3 * nw + 1
    weights = dict(zip(WEIGHT_NAMES, rest[:nw]))
    loss_target = rest[nw]
    moms = dict(zip(WEIGHT_NAMES, rest[nw + 1:2 * nw + 1]))
    vars_ = dict(zip(WEIGHT_NAMES, rest[2 * nw + 1:]))
    local_shapes = {n: weights[n].shape for n in WEIGHT_NAMES}

    def native2d(name, a):
        a2 = a.reshape(a.shape[-2], a.shape[-1])
        return a2.T if name in BIG_T else a2

    def from2d(name, a2):
        return (a2.T if name in BIG_T else a2).reshape(local_shapes[name])

    big_w = {n: native2d(n, weights[n]) for n in BIG_NAMES}
    w_in_t_sh = big_w['w_in'].astype(BF16)
    late = [big_w[n].astype(BF16) for n in LATE_NAMES]
    small_sh = _pack_small_shards({n: weights[n] for n in SMALL_NAMES})
    w_in_g, small_g = _exchange("weights_all_gather", [w_in_t_sh, small_sh], False)
    W = _unpack_small(small_g)
    W['w_in_t'] = w_in_g.reshape(-1, D_MODEL)
    for n in REPL_NAMES:
        W[n] = weights[n][0] if n != 'norm_final_g' else weights[n]

    loss, dx, G, late_parts, w_in_parts = _local_step(x[0], loss_target[0], W, late)

    late_items = [(parts, big_w[n], native2d(n, moms[n]), native2d(n, vars_[n]))
                  for n, parts in zip(LATE_NAMES, late_parts)]
    late_res, (small_parts,) = _adamw_late(late_items, [_pack_small_grads(G, loss)])
    results = {n: [from2d(n, r) for r in res] for n, res in zip(LATE_NAMES, late_res)}
    res = _adamw("adamw_w_in", w_in_parts, big_w['w_in'], native2d('w_in', moms['w_in']), native2d('w_in', vars_['w_in']))
    results['w_in'] = [from2d('w_in', r) for r in res]
    small_res = _adamw("adamw_small", small_parts, _pack_small_local(weights), _pack_small_local(moms),
                       _pack_small_local(vars_))
    small_out = [_unpack_small_local(p, local_shapes) for p in small_res]

    n_small = sum(int(np.prod(local_shapes[n])) for n in SMALL_NAMES + REPL_NAMES)
    loss = small_res[0].reshape(-1)[n_small]
    outs = [loss, dx[None]]
    for j in range(4):
        outs += [results[n][j] if n in results else small_out[j][n] for n in WEIGHT_NAMES]
    return tuple(outs)
```
